```python
import math
import jax, jax.numpy as jnp
from jax import lax
import numpy as np

D_MODEL = 1024
BATCH = 16
SEQ = 2048
DEPTH = 1

N_META = 16
GRID_W = 64
CONV_DIM = 1024
CONV_K = 31
N_HEADS = 16
N_KV_HEADS = 4
HEAD_DIM = 64
GQA_GROUP = N_HEADS // N_KV_HEADS
ATTN_DIM = N_HEADS * HEAD_DIM
KV_DIM = N_KV_HEADS * HEAD_DIM
ROPE_FREQS = HEAD_DIM // 4
ROPE_THETA = 10000.0
Q_BLOCK = 128
NORM_EPS = 1e-6

IN_SPLITS = [CONV_DIM, CONV_DIM, CONV_DIM,
             ATTN_DIM, KV_DIM, KV_DIM, ATTN_DIM,
             D_MODEL, D_MODEL]
IN_DIM = sum(IN_SPLITS)
IN_OFFSETS = list(np.cumsum(IN_SPLITS)[:-1].tolist())

kernel_name = "hybrid_conformer_gqa_gated_encoder"


def rms_norm(x, g, eps=NORM_EPS):
    xf = x.astype(jnp.float32)
    y = xf * lax.rsqrt(jnp.mean(xf * xf, axis=-1, keepdims=True) + eps)
    return (y * g.astype(jnp.float32)).astype(x.dtype)


def layer_norm(x, g, b, eps=NORM_EPS):
    xf = x.astype(jnp.float32)
    mu = jnp.mean(xf, axis=-1, keepdims=True)
    xc = xf - mu
    y = xc * lax.rsqrt(jnp.mean(xc * xc, axis=-1, keepdims=True) + eps)
    return (y * g.astype(jnp.float32) + b.astype(jnp.float32)).astype(x.dtype)


def rope_tables(n_tok):
    rows = n_tok // GRID_W
    row_ids = jnp.concatenate([jnp.zeros((N_META,), jnp.float32),
                               jnp.repeat(jnp.arange(rows, dtype=jnp.float32), GRID_W)])
    col_ids = jnp.concatenate([jnp.zeros((N_META,), jnp.float32),
                               jnp.tile(jnp.arange(GRID_W, dtype=jnp.float32), rows)])
    inv_freq = ROPE_THETA ** (-jnp.arange(ROPE_FREQS, dtype=jnp.float32) / ROPE_FREQS)
    a_row = row_ids[:, None] * inv_freq[None, :]
    a_col = col_ids[:, None] * inv_freq[None, :]
    ang = jnp.concatenate([a_row, a_row, a_col, a_col], axis=-1)
    return jnp.cos(ang), jnp.sin(ang)


def apply_rope2d(x, cos, sin):
    xs = x.reshape(x.shape[:-1] + (2, 2, ROPE_FREQS))
    rot = jnp.stack([-xs[..., 1, :], xs[..., 0, :]], axis=-2).reshape(x.shape)
    c = cos[None, :, None, :].astype(x.dtype)
    s = sin[None, :, None, :].astype(x.dtype)
    return x * c + rot * s


def conv_branch(val, glu_gate, z, conv_w, conv_b, cn_g, cn_b, w_proj):
    u = val * jax.nn.sigmoid(glu_gate)
    kern = conv_w.reshape(CONV_K, 1, CONV_DIM).astype(u.dtype)
    pad = CONV_K // 2
    c = lax.conv_general_dilated(u, kern, window_strides=(1,), padding=[(pad, pad)],
                                 dimension_numbers=("NWC", "WIO", "NWC"),
                                 feature_group_count=CONV_DIM)
    c = c + conv_b.astype(c.dtype)
    c = jax.nn.silu(layer_norm(c, cn_g, cn_b))
    c = c * jax.nn.silu(z)
    return jnp.einsum("blc,cd->bld", c, w_proj.astype(c.dtype))


def attn_branch(q, k, v, z, q_g, k_g, w_proj, cos, sin):
    B, L, _ = q.shape
    n_tok = L - N_META
    q = q.reshape(B, L, N_HEADS, HEAD_DIM)
    k = k.reshape(B, L, N_KV_HEADS, HEAD_DIM)
    v = v.reshape(B, L, N_KV_HEADS, HEAD_DIM)
    q = apply_rope2d(rms_norm(q, q_g), cos, sin)
    k = apply_rope2d(rms_norm(k, k_g), cos, sin)
    q = q.reshape(B, L, N_KV_HEADS, GQA_GROUP, HEAD_DIM)
    scale = 1.0 / math.sqrt(HEAD_DIM)

    def attend(qb):
        s = jnp.einsum("bqkgd,bskd->bkgqs", qb, k).astype(jnp.float32) * scale
        p = jax.nn.softmax(s, axis=-1).astype(v.dtype)
        return jnp.einsum("bkgqs,bskd->bqkgd", p, v)

    o_meta = attend(q[:, :N_META])
    n_blk = n_tok // Q_BLOCK
    q_real = q[:, N_META:].reshape(B, n_blk, Q_BLOCK, N_KV_HEADS, GQA_GROUP, HEAD_DIM)
    o_real = lax.map(attend, jnp.moveaxis(q_real, 1, 0))
    o_real = jnp.moveaxis(o_real, 0, 1).reshape(B, n_tok, N_KV_HEADS, GQA_GROUP, HEAD_DIM)
    o = jnp.concatenate([o_meta, o_real], axis=1).reshape(B, L, ATTN_DIM)
    o = o * jax.nn.silu(z)
    return jnp.einsum("bla,ad->bld", o, w_proj.astype(o.dtype))


def hybrid_layer(h, norm_g, w_in, conv_w, conv_b, cn_g, cn_b, w_conv_out,
                 q_g, k_g, w_attn_out, w_out, cos, sin):
    xn = rms_norm(h, norm_g)
    proj = jnp.einsum("bld,de->ble", xn, w_in.astype(xn.dtype))
    (c_val, c_glu, c_z, q, k, v, a_z, g_c, g_a) = jnp.split(proj, IN_OFFSETS, axis=-1)
    y_c = conv_branch(c_val, c_glu, c_z, conv_w, conv_b, cn_g, cn_b, w_conv_out)
    y_a = attn_branch(q, k, v, a_z, q_g, k_g, w_attn_out, cos, sin)
    merged = jax.nn.sigmoid(g_c) * y_c + jax.nn.sigmoid(g_a) * y_a
    return jnp.einsum("bld,de->ble", merged, w_out.astype(merged.dtype))


def _fwd_setup_inputs(seed: int = 0) -> dict:
    key = jax.random.key(seed)
    ks = jax.random.split(key, 16)
    f32 = jnp.float32
    nrm = lambda k, shape, s: jax.random.normal(k, shape, f32) * s
    return {
        "x": nrm(ks[0], (BATCH, SEQ, D_MODEL), 1.0),
        "meta_tokens": nrm(ks[1], (N_META, D_MODEL), 1.0),
        "norm_g": 1.0 + nrm(ks[2], (DEPTH, D_MODEL), 0.02),
        "w_in": nrm(ks[3], (DEPTH, D_MODEL, IN_DIM), D_MODEL ** -0.5),
        "conv_w": nrm(ks[4], (DEPTH, CONV_K, CONV_DIM), CONV_K ** -0.5),
        "conv_b": nrm(ks[5], (DEPTH, CONV_DIM), 0.02),
        "conv_norm_g": 1.0 + nrm(ks[6], (DEPTH, CONV_DIM), 0.02),
        "conv_norm_b": nrm(ks[7], (DEPTH, CONV_DIM), 0.02),
        "w_conv_out": nrm(ks[8], (DEPTH, CONV_DIM, D_MODEL), CONV_DIM ** -0.5),
        "q_norm_g": 1.0 + nrm(ks[9], (DEPTH, HEAD_DIM), 0.02),
        "k_norm_g": 1.0 + nrm(ks[10], (DEPTH, HEAD_DIM), 0.02),
        "w_attn_out": nrm(ks[11], (DEPTH, ATTN_DIM, D_MODEL), ATTN_DIM ** -0.5),
        "w_out": nrm(ks[12], (DEPTH, D_MODEL, D_MODEL), D_MODEL ** -0.5),
    }


def _fwd_reference(x, meta_tokens, norm_g, w_in, conv_w, conv_b, conv_norm_g, conv_norm_b,
              w_conv_out, q_norm_g, k_norm_g, w_attn_out, w_out):
    B, n_tok, _ = x.shape
    meta = jnp.broadcast_to(meta_tokens[None].astype(x.dtype), (B, N_META, D_MODEL))
    h = jnp.concatenate([meta, x], axis=1)
    cos, sin = rope_tables(n_tok)
    for layer in range(DEPTH):
        h = h + hybrid_layer(h, norm_g[layer], w_in[layer], conv_w[layer], conv_b[layer],
                             conv_norm_g[layer], conv_norm_b[layer], w_conv_out[layer],
                             q_norm_g[layer], k_norm_g[layer], w_attn_out[layer],
                             w_out[layer], cos, sin)
    return h[:, N_META:]


import jax as _jax
import jax.numpy as _jnp

TWIN_FORMAT = 'train_step'
FWD_PARAMS = ['x', 'meta_tokens', 'norm_g', 'w_in', 'conv_w', 'conv_b', 'conv_norm_g', 'conv_norm_b', 'w_conv_out', 'q_norm_g', 'k_norm_g', 'w_attn_out', 'w_out']
TWIN_WEIGHTS = ['meta_tokens', 'norm_g', 'w_in', 'conv_w', 'conv_b', 'conv_norm_g', 'conv_norm_b', 'w_conv_out', 'q_norm_g', 'k_norm_g', 'w_attn_out', 'w_out']
TWIN_DIFF_INPUT = 'x'
TWIN_INPUTS = ['x', 'meta_tokens', 'norm_g', 'w_in', 'conv_w', 'conv_b', 'conv_norm_g', 'conv_norm_b', 'w_conv_out', 'q_norm_g', 'k_norm_g', 'w_attn_out', 'w_out', 'loss_target', 'm_meta_tokens', 'm_norm_g', 'm_w_in', 'm_conv_w', 'm_conv_b', 'm_conv_norm_g', 'm_conv_norm_b', 'm_w_conv_out', 'm_q_norm_g', 'm_k_norm_g', 'm_w_attn_out', 'm_w_out', 'v_meta_tokens', 'v_norm_g', 'v_w_in', 'v_conv_w', 'v_conv_b', 'v_conv_norm_g', 'v_conv_norm_b', 'v_w_conv_out', 'v_q_norm_g', 'v_k_norm_g', 'v_w_attn_out', 'v_w_out']
TWIN_OUTPUTS = ['loss', 'grad_x', 'grad_meta_tokens', 'grad_norm_g', 'grad_w_in', 'grad_conv_w', 'grad_conv_b', 'grad_conv_norm_g', 'grad_conv_norm_b', 'grad_w_conv_out', 'grad_q_norm_g', 'grad_k_norm_g', 'grad_w_attn_out', 'grad_w_out', 'delta_meta_tokens', 'delta_norm_g', 'delta_w_in', 'delta_conv_w', 'delta_conv_b', 'delta_conv_norm_g', 'delta_conv_norm_b', 'delta_w_conv_out', 'delta_q_norm_g', 'delta_k_norm_g', 'delta_w_attn_out', 'delta_w_out', 'new_m_meta_tokens', 'new_m_norm_g', 'new_m_w_in', 'new_m_conv_w', 'new_m_conv_b', 'new_m_conv_norm_g', 'new_m_conv_norm_b', 'new_m_w_conv_out', 'new_m_q_norm_g', 'new_m_k_norm_g', 'new_m_w_attn_out', 'new_m_w_out', 'new_v_meta_tokens', 'new_v_norm_g', 'new_v_w_in', 'new_v_conv_w', 'new_v_conv_b', 'new_v_conv_norm_g', 'new_v_conv_norm_b', 'new_v_w_conv_out', 'new_v_q_norm_g', 'new_v_k_norm_g', 'new_v_w_attn_out', 'new_v_w_out']
TWIN_LEAF_KINDS = {'loss': 'loss', 'grad_x': 'grad_x', 'grad_meta_tokens': 'grad_w', 'grad_norm_g': 'grad_w', 'grad_w_in': 'grad_w', 'grad_conv_w': 'grad_w', 'grad_conv_b': 'grad_w', 'grad_conv_norm_g': 'grad_w', 'grad_conv_norm_b': 'grad_w', 'grad_w_conv_out': 'grad_w', 'grad_q_norm_g': 'grad_w', 'grad_k_norm_g': 'grad_w', 'grad_w_attn_out': 'grad_w', 'grad_w_out': 'grad_w', 'delta_meta_tokens': 'delta_w', 'delta_norm_g': 'delta_w', 'delta_w_in': 'delta_w', 'delta_conv_w': 'delta_w', 'delta_conv_b': 'delta_w', 'delta_conv_norm_g': 'delta_w', 'delta_conv_norm_b': 'delta_w', 'delta_w_conv_out': 'delta_w', 'delta_q_norm_g': 'delta_w', 'delta_k_norm_g': 'delta_w', 'delta_w_attn_out': 'delta_w', 'delta_w_out': 'delta_w', 'new_m_meta_tokens': 'new_m', 'new_m_norm_g': 'new_m', 'new_m_w_in': 'new_m', 'new_m_conv_w': 'new_m', 'new_m_conv_b': 'new_m', 'new_m_conv_norm_g': 'new_m', 'new_m_conv_norm_b': 'new_m', 'new_m_w_conv_out': 'new_m', 'new_m_q_norm_g': 'new_m', 'new_m_k_norm_g': 'new_m', 'new_m_w_attn_out': 'new_m', 'new_m_w_out': 'new_m', 'new_v_meta_tokens': 'new_v', 'new_v_norm_g': 'new_v', 'new_v_w_in': 'new_v', 'new_v_conv_w': 'new_v', 'new_v_conv_b': 'new_v', 'new_v_conv_norm_g': 'new_v', 'new_v_conv_norm_b': 'new_v', 'new_v_w_conv_out': 'new_v', 'new_v_q_norm_g': 'new_v', 'new_v_k_norm_g': 'new_v', 'new_v_w_attn_out': 'new_v', 'new_v_w_out': 'new_v'}


def _forward(args):
    return _fwd_reference(*[args[k] for k in FWD_PARAMS])


def _output_shape():
    out = _jax.eval_shape(lambda: _forward(_fwd_setup_inputs(0)))
    return out.shape, out.dtype

N_MICROBATCH = 1
ADAM_LR = 0.001
ADAM_B1 = 0.9
ADAM_B2 = 0.999
ADAM_EPS = 1e-08
ADAM_WD = 0.01
ADAM_STEP = 10
PER_EXAMPLE_BATCH_AXIS = {'x': 0, 'loss_target': 0}
SHARED_INPUTS = []
_WEIGHT_DTYPES = {'meta_tokens': _jnp.float32, 'norm_g': _jnp.float32, 'w_in': _jnp.float32, 'conv_w': _jnp.float32, 'conv_b': _jnp.float32, 'conv_norm_g': _jnp.float32, 'conv_norm_b': _jnp.float32, 'w_conv_out': _jnp.float32, 'q_norm_g': _jnp.float32, 'k_norm_g': _jnp.float32, 'w_attn_out': _jnp.float32, 'w_out': _jnp.float32}
MOMENT_SCALE = {'meta_tokens': 2.175505e-03, 'norm_g': 1.389581e+00, 'w_in': 3.492708e-02, 'conv_w': 6.199754e-02, 'conv_b': 3.244485e-01, 'conv_norm_g': 1.373273e+00, 'conv_norm_b': 8.772510e-01, 'w_conv_out': 6.487753e-02, 'q_norm_g': 8.296039e-02, 'k_norm_g': 8.354703e-02, 'w_attn_out': 8.449174e-03, 'w_out': 6.297163e-02}


def _to_microbatches(a, axis):
    t = _jnp.moveaxis(a, axis, 0)
    t = t.reshape((N_MICROBATCH, t.shape[0] // N_MICROBATCH) + t.shape[1:])
    return _jnp.moveaxis(t, 1, axis + 1)


def setup_inputs(seed: int = 0) -> dict:
    inp = _fwd_setup_inputs(seed)
    key = _jax.random.fold_in(_jax.random.key(seed), 7919)
    shape, _ = _output_shape()
    out = dict(inp)
    out["loss_target"] = _jax.random.normal(_jax.random.fold_in(key, 0), shape, _jnp.float32)
    for i, name in enumerate(TWIN_WEIGHTS):
        w = inp[name].astype(_jnp.float32)
        if MOMENT_SCALE is None:
            s = _jnp.sqrt(_jnp.mean(_jnp.square(w)) + 1e-30)
        else:
            s = MOMENT_SCALE[name]
        km, kv = _jax.random.split(_jax.random.fold_in(key, i + 1))
        out[name] = w
        out["m_" + name] = s * _jax.random.normal(km, w.shape, _jnp.float32)
        out["v_" + name] = (s * s) * _jax.random.uniform(kv, w.shape, _jnp.float32, 0.5, 1.5)
    if N_MICROBATCH > 1:
        for name, axis in PER_EXAMPLE_BATCH_AXIS.items():
            out[name] = _to_microbatches(out[name], axis)
    return {'x': out['x'], 'meta_tokens': out['meta_tokens'], 'norm_g': out['norm_g'], 'w_in': out['w_in'], 'conv_w': out['conv_w'], 'conv_b': out['conv_b'], 'conv_norm_g': out['conv_norm_g'], 'conv_norm_b': out['conv_norm_b'], 'w_conv_out': out['w_conv_out'], 'q_norm_g': out['q_norm_g'], 'k_norm_g': out['k_norm_g'], 'w_attn_out': out['w_attn_out'], 'w_out': out['w_out'], 'loss_target': out['loss_target'], 'm_meta_tokens': out['m_meta_tokens'], 'm_norm_g': out['m_norm_g'], 'm_w_in': out['m_w_in'], 'm_conv_w': out['m_conv_w'], 'm_conv_b': out['m_conv_b'], 'm_conv_norm_g': out['m_conv_norm_g'], 'm_conv_norm_b': out['m_conv_norm_b'], 'm_w_conv_out': out['m_w_conv_out'], 'm_q_norm_g': out['m_q_norm_g'], 'm_k_norm_g': out['m_k_norm_g'], 'm_w_attn_out': out['m_w_attn_out'], 'm_w_out': out['m_w_out'], 'v_meta_tokens': out['v_meta_tokens'], 'v_norm_g': out['v_norm_g'], 'v_w_in': out['v_w_in'], 'v_conv_w': out['v_conv_w'], 'v_conv_b': out['v_conv_b'], 'v_conv_norm_g': out['v_conv_norm_g'], 'v_conv_norm_b': out['v_conv_norm_b'], 'v_w_conv_out': out['v_w_conv_out'], 'v_q_norm_g': out['v_q_norm_g'], 'v_k_norm_g': out['v_k_norm_g'], 'v_w_attn_out': out['v_w_attn_out'], 'v_w_out': out['v_w_out']}


def _loss(weights, diff, rest, loss_target):
    with _jax.named_scope("forward"):
        args = {**rest, TWIN_DIFF_INPUT: diff, **{k: w.astype(_WEIGHT_DTYPES[k]) for k, w in weights.items()}}
        y = _forward(args)
    with _jax.named_scope("loss_head"):
        err = _jnp.square(y.astype(_jnp.float32) - loss_target)
        return 0.5 * _jnp.sum(_jnp.mean(err, axis=-1)) if err.ndim else 0.5 * err


def _adamw(w, g, m, v):
    m = ADAM_B1 * m + (1.0 - ADAM_B1) * g
    v = ADAM_B2 * v + (1.0 - ADAM_B2) * _jnp.square(g)
    m_hat = m / (1.0 - ADAM_B1 ** ADAM_STEP)
    v_hat = v / (1.0 - ADAM_B2 ** ADAM_STEP)
    delta = -ADAM_LR * (m_hat / (_jnp.sqrt(v_hat) + ADAM_EPS) + ADAM_WD * w)
    return delta, m, v


def reference(x, meta_tokens, norm_g, w_in, conv_w, conv_b, conv_norm_g, conv_norm_b, w_conv_out, q_norm_g, k_norm_g, w_attn_out, w_out, loss_target, m_meta_tokens, m_norm_g, m_w_in, m_conv_w, m_conv_b, m_conv_norm_g, m_conv_norm_b, m_w_conv_out, m_q_norm_g, m_k_norm_g, m_w_attn_out, m_w_out, v_meta_tokens, v_norm_g, v_w_in, v_conv_w, v_conv_b, v_conv_norm_g, v_conv_norm_b, v_w_conv_out, v_q_norm_g, v_k_norm_g, v_w_attn_out, v_w_out):
    given = dict(x=x, meta_tokens=meta_tokens, norm_g=norm_g, w_in=w_in, conv_w=conv_w, conv_b=conv_b, conv_norm_g=conv_norm_g, conv_norm_b=conv_norm_b, w_conv_out=w_conv_out, q_norm_g=q_norm_g, k_norm_g=k_norm_g, w_attn_out=w_attn_out, w_out=w_out, loss_target=loss_target, m_meta_tokens=m_meta_tokens, m_norm_g=m_norm_g, m_w_in=m_w_in, m_conv_w=m_conv_w, m_conv_b=m_conv_b, m_conv_norm_g=m_conv_norm_g, m_conv_norm_b=m_conv_norm_b, m_w_conv_out=m_w_conv_out, m_q_norm_g=m_q_norm_g, m_k_norm_g=m_k_norm_g, m_w_attn_out=m_w_attn_out, m_w_out=m_w_out, v_meta_tokens=v_meta_tokens, v_norm_g=v_norm_g, v_w_in=v_w_in, v_conv_w=v_conv_w, v_conv_b=v_conv_b, v_conv_norm_g=v_conv_norm_g, v_conv_norm_b=v_conv_norm_b, v_w_conv_out=v_w_conv_out, v_q_norm_g=v_q_norm_g, v_k_norm_g=v_k_norm_g, v_w_attn_out=v_w_attn_out, v_w_out=v_w_out)
    weights = {n: given[n] for n in TWIN_WEIGHTS}
    shared = {n: given[n] for n in SHARED_INPUTS}
    per_example = {n: given[n] for n in ['x']}
    grad_fn = _jax.value_and_grad(_loss, argnums=(0, 1))

    def one_microbatch(ex, loss_target):
        ex = dict(ex)
        diff = ex.pop(TWIN_DIFF_INPUT)
        return grad_fn(weights, diff, {**shared, **ex}, loss_target)

    if N_MICROBATCH == 1:
        loss, (grad_w, grad_x) = one_microbatch(per_example, given["loss_target"])
    else:
        def body(carry, xs):
            loss_sum, grad_sum = carry
            l_k, (gw_k, gx_k) = one_microbatch(xs[0], xs[1])
            with _jax.named_scope("update"):
                return (loss_sum + l_k, _jax.tree.map(_jnp.add, grad_sum, gw_k)), gx_k

        init = (_jnp.zeros((), _jnp.float32), _jax.tree.map(_jnp.zeros_like, weights))
        (loss, grad_w), grad_x = _jax.lax.scan(body, init, (per_example, given["loss_target"]))
    with _jax.named_scope("update"):
        delta_w, new_m, new_v = {}, {}, {}
        for n in TWIN_WEIGHTS:
            delta_w[n], new_m[n], new_v[n] = _adamw(weights[n], grad_w[n], given["m_" + n], given["v_" + n])
    return (loss, grad_x, *[grad_w[n] for n in TWIN_WEIGHTS], *[delta_w[n] for n in TWIN_WEIGHTS],
            *[new_m[n] for n in TWIN_WEIGHTS], *[new_v[n] for n in TWIN_WEIGHTS])
```

```python
import functools
import math

import jax
import jax.numpy as jnp
from jax import lax
from jax.experimental import pallas as pl
from jax.experimental.pallas import tpu as pltpu

F32, BF16 = jnp.float32, jnp.bfloat16
MESH = pl.DeviceIdType.MESH

D = 1024
N_META = 16
CONV_K = 31
N_KV = 4
GQA = 4
HEAD_DIM = 64
GROUP_W = GQA * HEAD_DIM
GRID_W = 64
ROPE_FREQS = 16
ROPE_THETA = 10000.0
EPS = 1e-6
IN_DIM = 7680
KEY_PAD = 128
G_CONV, G_CZ, G_Q, G_KV, G_E = (0, 2048), (2048, 1024), (3072, 1024), (4096, 512), (4608, 3072)
N_CHIPS = 4
W_IN_SHARD = IN_DIM // N_CHIPS
ROW_SHARD = D // N_CHIPS

ADAM_LR, ADAM_B1, ADAM_B2, ADAM_EPS, ADAM_WD, ADAM_STEP = 0.001, 0.9, 0.999, 1e-08, 0.01, 10

NT_DIMS = (((1,), (1,)), ((), ()))


def _params(sem=None, vmem_mb=48):
    return pltpu.CompilerParams(dimension_semantics=sem, vmem_limit_bytes=vmem_mb << 20)


def _sig(v):
    return jax.nn.sigmoid(v)


def _dsilu(v, s):
    return s * (1.0 + v * (1.0 - s))


def _dot(a, b):
    return jnp.dot(a, b, preferred_element_type=F32)


def _dot_nt(a, b):
    return lax.dot_general(a, b, NT_DIMS, preferred_element_type=F32)


def _head_mean(v):
    lane = lax.broadcasted_iota(jnp.int32, v.shape, 1)
    out = jnp.zeros_like(v)
    for h in range(GQA):
        m = jnp.sum(v[:, HEAD_DIM * h:HEAD_DIM * (h + 1)], axis=-1, keepdims=True) * (1.0 / HEAD_DIM)
        out = jnp.where((lane >= HEAD_DIM * h) & (lane < HEAD_DIM * (h + 1)), m, out)
    return out


def _rot(v):
    lane = lax.broadcasted_iota(jnp.int32, v.shape, 1)
    return jnp.where((lane & 16) == 0, -pltpu.roll(v, GROUP_W - 16, 1), pltpu.roll(v, 16, 1))


def _qk_fwd(v, g, cos, sin):
    r = lax.rsqrt(_head_mean(v * v) + EPS)
    n = v * r * g
    return n * cos + _rot(n) * sin, r


def _qk_bwd(dy, v, r, g, cos, sin):
    dn = dy * cos - _rot(dy) * sin
    dyg = dn * g
    dv = r * dyg - v * (r * r * r) * _head_mean(dyg * v)
    return dv, dn * v * r


def _rms_bwd(dxn, v, r, g):
    dxg = dxn * g
    return r * dxg - v * (r * r * r) * jnp.mean(dxg * v, axis=-1, keepdims=True)


def _glu(a):
    return a[:, :D] * _sig(a[:, D:])


def _gather_weights(w_in_s, w3_s, conv_w_s, meta_s):
    def body(win_ref, w3_ref, cw_ref, mt_ref, win_o, w3_o, cw_o, mt_o, win_b, w3_b, send, recv, lsem):
        x, y, c = lax.axis_index("x"), lax.axis_index("y"), lax.axis_index("c")
        me = 2 * x + y
        win_b[...] = win_ref[...].astype(BF16)
        w3_b[...] = w3_ref[...].astype(BF16)
        items = (
            (win_b, lambda p: win_o.at[:, pl.ds(p * W_IN_SHARD, W_IN_SHARD)]),
            (w3_b, lambda p: w3_o.at[:, pl.ds(p * ROW_SHARD, ROW_SHARD), :]),
            (cw_ref, lambda p: cw_o.at[:, pl.ds(p * ROW_SHARD, ROW_SHARD)]),
            (mt_ref, lambda p: mt_o.at[:, pl.ds(p * ROW_SHARD, ROW_SHARD)]),
        )
        peers = ((1 - x, y), (x, 1 - y), (1 - x, 1 - y))
        local, remote = [], []
        for a, (src, dst) in enumerate(items):
            loc = pltpu.make_async_copy(src, dst(me), lsem.at[a])
            loc.start()
            local.append(loc)
            for k, (px, py) in enumerate(peers):
                cp = pltpu.make_async_remote_copy(src_ref=src, dst_ref=dst(me), send_sem=send.at[a, k],
                                                  recv_sem=recv.at[a, k], device_id=(px, py, c), device_id_type=MESH)
                cp.start()
                remote.append(cp)
        for a, (src, dst) in enumerate(items):
            for k, (px, py) in enumerate(peers):
                pltpu.make_async_remote_copy(src_ref=src, dst_ref=dst(2 * px + py), send_sem=send.at[a, k],
                                             recv_sem=recv.at[a, k], device_id=(px, py, c),
                                             device_id_type=MESH).wait_recv()
        for cp in remote:
            cp.wait_send()
        for loc in local:
            loc.wait()

    any_spec = pl.BlockSpec(memory_space=pl.ANY)
    vmem = pl.BlockSpec(memory_space=pltpu.VMEM)
    return pl.pallas_call(
        body, name="gather_weights",
        out_shape=(jax.ShapeDtypeStruct((D, IN_DIM), BF16), jax.ShapeDtypeStruct((3, D, D), BF16),
                   jax.ShapeDtypeStruct((32, D), F32), jax.ShapeDtypeStruct((N_META, D), F32)),
        in_specs=[vmem, vmem, vmem, vmem],
        out_specs=(any_spec, any_spec, any_spec, any_spec),
        scratch_shapes=[pltpu.VMEM((D, W_IN_SHARD), BF16), pltpu.VMEM((3, ROW_SHARD, D), BF16),
                        pltpu.SemaphoreType.DMA((4, 3)), pltpu.SemaphoreType.DMA((4, 3)),
                        pltpu.SemaphoreType.DMA((4,))],
        compiler_params=pltpu.CompilerParams(vmem_limit_bytes=40 << 20),
    )(w_in_s, w3_s, conv_w_s, meta_s)


def _meta_fwd(meta_full, norm_g, w_full):
    def body(m_ref, g_ref, wc_ref, wkv_ref, xnt_ref, pc_ref, pkv_ref):
        v = m_ref[...]
        r = lax.rsqrt(jnp.mean(v * v, axis=-1, keepdims=True) + EPS)
        xn = v * r * g_ref[...]
        xnb = xn.astype(BF16)
        pad = jnp.concatenate([xn, jnp.zeros((128 - N_META, D), F32)], axis=0)
        xnt_ref[...] = pad.T.astype(BF16)
        pc_ref[...] = _dot(xnb, wc_ref[...])
        pkv_ref[...] = _dot(xnb, wkv_ref[...])

    return pl.pallas_call(
        body, name="meta_fwd", grid=(1,),
        out_shape=(jax.ShapeDtypeStruct((D, 128), BF16), jax.ShapeDtypeStruct((N_META, 2048), F32),
                   jax.ShapeDtypeStruct((N_META, 512), F32)),
        in_specs=[pl.BlockSpec((N_META, D), lambda i: (0, 0)), pl.BlockSpec((1, D), lambda i: (0, 0)),
                  pl.BlockSpec((D, 2048), lambda i: (0, 0)), pl.BlockSpec((D, 512), lambda i: (0, G_KV[0] // 512))],
        out_specs=(pl.BlockSpec((D, 128), lambda i: (0, 0)), pl.BlockSpec((N_META, 2048), lambda i: (0, 0)),
                   pl.BlockSpec((N_META, 512), lambda i: (0, 0))),
        compiler_params=_params(("arbitrary",), 32),
    )(meta_full, norm_g, w_full, w_full)


def _in_proj(x2, norm_g, w_full, tm):
    rows = x2.shape[0]
    groups = (G_CONV, G_CZ, G_Q, G_KV, G_E)

    def body(x_ref, g_ref, w_hbm, *rest):
        outs, xnt_ref, w_vmem, sem = rest[:5], rest[5], rest[6], rest[7]

        @pl.when(pl.program_id(0) == 0)
        def _():
            cp = pltpu.make_async_copy(w_hbm, w_vmem, sem)
            cp.start()
            cp.wait()

        v = x_ref[...]
        r = lax.rsqrt(jnp.mean(v * v, axis=-1, keepdims=True) + EPS)
        xn = v * r * g_ref[...]
        xnb = xn.astype(BF16)
        xnt_ref[...] = xn.T.astype(BF16)
        for ref, (off, wd) in zip(outs, groups):
            for c0 in range(0, wd, 512):
                ref[:, c0:c0 + 512] = _dot(xnb, w_vmem[:, off + c0:off + c0 + 512])

    return pl.pallas_call(
        body, name="in_proj", grid=(rows // tm,),
        out_shape=tuple(jax.ShapeDtypeStruct((rows, wd), F32) for _, wd in groups)
        + (jax.ShapeDtypeStruct((D, rows), BF16),),
        in_specs=[pl.BlockSpec((tm, D), lambda i: (i, 0)), pl.BlockSpec((1, D), lambda i: (0, 0)),
                  pl.BlockSpec(memory_space=pl.ANY)],
        out_specs=tuple(pl.BlockSpec((tm, wd), lambda i: (i, 0)) for _, wd in groups)
        + (pl.BlockSpec((D, tm), lambda i: (0, i)),),
        scratch_shapes=[pltpu.VMEM((D, IN_DIM), BF16), pltpu.SemaphoreType.DMA],
        compiler_params=_params(("arbitrary",), 56),
    )(x2, norm_g, w_full)


def _halo_specs(width, tm, nt, rows):
    h16 = tm // 16
    return [pl.BlockSpec((tm, width), lambda b, i: (b * nt + i, 0)),
            pl.BlockSpec((16, width), lambda b, i: (jnp.maximum((b * nt + i) * h16 - 1, 0), 0)),
            pl.BlockSpec((16, width), lambda b, i: (jnp.minimum((b * nt + i + 1) * h16, rows // 16 - 1), 0))]


def _fill_uext(uext, cur, prev, nxt, meta, i, nt, tm):
    uext[0:16] = jnp.where(i == 0, _glu(meta[...]), _glu(prev[...]))
    uext[16:16 + tm] = _glu(cur[...])
    uext[16 + tm:32 + tm] = jnp.where(i == nt - 1, 0.0, _glu(nxt[...]))


def _conv_fwd(pconv, pm_conv, conv_w, conv_b, nb, tm):
    rows = pconv.shape[0]
    nt = rows // nb // tm

    def body(cur, prev, nxt, meta, w_ref, b_ref, o_ref, uext):
        i = pl.program_id(1)
        _fill_uext(uext, cur, prev, nxt, meta, i, nt, tm)
        for r0 in range(0, tm, 32):
            for c0 in range(0, D, 256):
                acc = jnp.zeros((32, 256), F32) + b_ref[:, c0:c0 + 256]
                for j in range(CONV_K):
                    acc = acc + uext[r0 + j + 1:r0 + j + 33, c0:c0 + 256] * w_ref[j:j + 1, c0:c0 + 256]
                o_ref[r0:r0 + 32, c0:c0 + 256] = acc

    return pl.pallas_call(
        body, name="conv_fwd", grid=(nb, nt),
        out_shape=jax.ShapeDtypeStruct((rows, D), F32),
        in_specs=_halo_specs(2048, tm, nt, rows)
        + [pl.BlockSpec((16, 2048), lambda b, i: (0, 0)), pl.BlockSpec((32, D), lambda b, i: (0, 0)),
           pl.BlockSpec((1, D), lambda b, i: (0, 0))],
        out_specs=pl.BlockSpec((tm, D), lambda b, i: (b * nt + i, 0)),
        scratch_shapes=[pltpu.VMEM((tm + 32, D), F32)],
        compiler_params=_params(("parallel", "parallel"), 40),
    )(pconv, pconv, pconv, pm_conv, conv_w, conv_b)


def _kv_prep(pkv, pm_kv, kg, cos, sin, nb):
    rows = pkv.shape[0]
    s_len = rows // nb
    tk = 128
    nt = s_len // tk

    def body(kv_ref, m_ref, g_ref, cos_ref, sin_ref, k_o, v_o):
        i = pl.program_id(1)

        @pl.when(i < nt)
        def _():
            kv = kv_ref[...]
            kr, _ = _qk_fwd(kv[:, :GROUP_W], g_ref[...], cos_ref[...], sin_ref[...])
            for h in range(N_KV):
                k_o[0, h] = kr[:, HEAD_DIM * h:HEAD_DIM * (h + 1)].astype(BF16)
                v_o[0, h] = kv[:, GROUP_W + HEAD_DIM * h:GROUP_W + HEAD_DIM * (h + 1)].astype(BF16)

        @pl.when(i == nt)
        def _():
            kv = m_ref[...]
            km = kv[:, :GROUP_W]
            kn = km * lax.rsqrt(_head_mean(km * km) + EPS) * g_ref[...]
            zeros = jnp.zeros((tk - N_META, GROUP_W), F32)
            kfull = jnp.concatenate([kn, zeros], axis=0)
            vfull = jnp.concatenate([kv[:, GROUP_W:], zeros], axis=0)
            for h in range(N_KV):
                k_o[0, h] = kfull[:, HEAD_DIM * h:HEAD_DIM * (h + 1)].astype(BF16)
                v_o[0, h] = vfull[:, HEAD_DIM * h:HEAD_DIM * (h + 1)].astype(BF16)

    lk = s_len + KEY_PAD
    last = nt - 1
    return pl.pallas_call(
        body, name="kv_prep", grid=(nb, nt + 1),
        out_shape=(jax.ShapeDtypeStruct((nb, N_KV, lk, HEAD_DIM), BF16),) * 2,
        in_specs=[pl.BlockSpec((tk, 512), lambda b, i: (b * nt + jnp.minimum(i, last), 0)),
                  pl.BlockSpec((N_META, 512), lambda b, i: (0, 0)), pl.BlockSpec((1, GROUP_W), lambda b, i: (0, 0)),
                  pl.BlockSpec((tk, GROUP_W), lambda b, i: (jnp.minimum(i, last), 0)),
                  pl.BlockSpec((tk, GROUP_W), lambda b, i: (jnp.minimum(i, last), 0))],
        out_specs=(pl.BlockSpec((1, N_KV, tk, HEAD_DIM), lambda b, i: (b, 0, i, 0)),) * 2,
        compiler_params=_params(("parallel", "arbitrary"), 32),
    )(pkv, pm_kv, kg, cos, sin)


def _key_bias(s_len):
    col = lax.broadcasted_iota(jnp.int32, (1, s_len + KEY_PAD), 1)
    return jnp.where(col < s_len + N_META, 0.0, -1e30).astype(F32)


def _attn_fwd(pq, kr, vb, qg, cos, sin, nb, tq):
    rows = pq.shape[0]
    s_len = rows // nb
    nq = s_len // tq
    lk = s_len + KEY_PAD

    def body(q_ref, k_ref, v_ref, g_ref, cos_ref, sin_ref, o_ref):
        qr, _ = _qk_fwd(q_ref[...], g_ref[...], cos_ref[...], sin_ref[...])
        qs = (qr * (1.0 / math.sqrt(HEAD_DIM))).astype(BF16)
        k, v = k_ref[0, 0], v_ref[0, 0]
        bias = _key_bias(s_len)
        outs = []
        for h in range(GQA):
            s = _dot_nt(qs[:, HEAD_DIM * h:HEAD_DIM * (h + 1)], k) + bias
            p = jnp.exp(s - jnp.max(s, axis=-1, keepdims=True))
            inv = 1.0 / jnp.sum(p, axis=-1, keepdims=True)
            outs.append(_dot(p.astype(BF16), v) * inv)
        o_ref[...] = jnp.concatenate(outs, axis=1)

    return pl.pallas_call(
        body, name="attn_fwd", grid=(nb, N_KV, nq),
        out_shape=jax.ShapeDtypeStruct((rows, D), F32),
        in_specs=[pl.BlockSpec((tq, GROUP_W), lambda b, g, i: (b * nq + i, g)),
                  pl.BlockSpec((1, 1, lk, HEAD_DIM), lambda b, g, i: (b, g, 0, 0)),
                  pl.BlockSpec((1, 1, lk, HEAD_DIM), lambda b, g, i: (b, g, 0, 0)),
                  pl.BlockSpec((1, GROUP_W), lambda b, g, i: (0, 0)),
                  pl.BlockSpec((tq, GROUP_W), lambda b, g, i: (i, 0)),
                  pl.BlockSpec((tq, GROUP_W), lambda b, g, i: (i, 0))],
        out_specs=pl.BlockSpec((tq, GROUP_W), lambda b, g, i: (b * nq + i, g)),
        compiler_params=_params(("parallel", "parallel", "parallel"), 48),
    )(pq, kr, vb, qg, cos, sin)


def _mid(x2, t2, c0, cz, o, e, w3, cn_g, cn_b, tm):
    rows = x2.shape[0]

    def body(x_ref, t_ref, c0_ref, cz_ref, o_ref, e_ref, w_ref, g_ref, b_ref,
             dy_o, mt_o, c3t_o, o2t_o, dyc_o, dya_o, do_o, dc0_o, dcz_o, de_o, sums_o):
        wco, wao, wo = w_ref[0], w_ref[1], w_ref[2]
        cn_g_v = g_ref[...]
        c0v = c0_ref[...]
        xc = c0v - jnp.mean(c0v, axis=-1, keepdims=True)
        rstd = lax.rsqrt(jnp.mean(xc * xc, axis=-1, keepdims=True) + EPS)
        n = xc * rstd
        c1 = n * cn_g_v + b_ref[...]
        s1 = _sig(c1)
        c2 = c1 * s1
        czv = cz_ref[...]
        sz = _sig(czv)
        gz = czv * sz
        c3 = c2 * gz
        yc = _dot(c3.astype(BF16), wco)
        az, gc, ga = e_ref[:, :D], e_ref[:, D:2 * D], e_ref[:, 2 * D:]
        saz = _sig(az)
        gaz = az * saz
        ov = o_ref[...]
        o2 = ov * gaz
        ya = _dot(o2.astype(BF16), wao)
        sc, sa = _sig(gc), _sig(ga)
        merged = sc * yc + sa * ya
        out = _dot(merged.astype(BF16), wo)
        err = x_ref[...] + out - t_ref[...]
        dy = err * (1.0 / D)
        dy_o[...] = dy
        dm = _dot_nt(dy.astype(BF16), wo)
        dyc = dm * sc
        dya = dm * sa
        dycb, dyab = dyc.astype(BF16), dya.astype(BF16)
        dyc_o[...] = dycb
        dya_o[...] = dyab
        de_o[:, D:2 * D] = (dyc * yc * (1.0 - sc)).astype(BF16)
        de_o[:, 2 * D:] = (dya * ya * (1.0 - sa)).astype(BF16)
        dc3 = _dot_nt(dycb, wco)
        do2 = _dot_nt(dyab, wao)
        do_o[...] = do2 * gaz
        de_o[:, :D] = (do2 * ov * _dsilu(az, saz)).astype(BF16)
        dcz_o[...] = (dc3 * c2 * _dsilu(czv, sz)).astype(BF16)
        dc1 = dc3 * gz * _dsilu(c1, s1)
        dn = dc1 * cn_g_v
        dc0 = rstd * (dn - jnp.mean(dn, axis=-1, keepdims=True) - n * jnp.mean(dn * n, axis=-1, keepdims=True))
        dc0_o[...] = dc0
        mt_o[...] = merged.T.astype(BF16)
        c3t_o[...] = c3.T.astype(BF16)
        o2t_o[...] = o2.T.astype(BF16)

        @pl.when(pl.program_id(0) == 0)
        def _():
            sums_o[...] = jnp.zeros_like(sums_o)

        sums_o[0:1, :] += jnp.sum(dc1 * n, axis=0, keepdims=True)
        sums_o[1:2, :] += jnp.sum(dc1, axis=0, keepdims=True)
        sums_o[2:3, :] += jnp.sum(dc0, axis=0, keepdims=True)
        sums_o[3:4, :] += jnp.sum(err * err, axis=0, keepdims=True)

    row = lambda wd: pl.BlockSpec((tm, wd), lambda i: (i, 0))
    col = pl.BlockSpec((D, tm), lambda i: (0, i))
    vec = pl.BlockSpec((1, D), lambda i: (0, 0))
    f32o = lambda wd: jax.ShapeDtypeStruct((rows, wd), F32)
    b16o = lambda wd: jax.ShapeDtypeStruct((rows, wd), BF16)
    tpo = jax.ShapeDtypeStruct((D, rows), BF16)
    return pl.pallas_call(
        body, name="mid", grid=(rows // tm,),
        out_shape=(f32o(D), tpo, tpo, tpo, b16o(D), b16o(D), f32o(D), f32o(D), b16o(D), b16o(3 * D),
                   jax.ShapeDtypeStruct((8, D), F32)),
        in_specs=[row(D), row(D), row(D), row(D), row(D), row(3 * D),
                  pl.BlockSpec((3, D, D), lambda i: (0, 0, 0)), vec, vec],
        out_specs=(row(D), col, col, col, row(D), row(D), row(D), row(D), row(D), row(3 * D),
                   pl.BlockSpec((8, D), lambda i: (0, 0))),
        compiler_params=_params(("arbitrary",), 56),
    )(x2, t2, c0, cz, o, e, w3, cn_g, cn_b)


def _attn_bwd(pq, kr, vb, d_o, o, qg, cos, sin, nb, tq):
    rows = pq.shape[0]
    s_len = rows // nb
    nq = s_len // tq
    lk = s_len + KEY_PAD
    scale = 1.0 / math.sqrt(HEAD_DIM)

    def body(q_ref, k_ref, v_ref, do_ref, o_ref, g_ref, cos_ref, sin_ref, dq_o, dkt_o, dvt_o, dg_o):
        b, g, i = pl.program_id(0), pl.program_id(1), pl.program_id(2)
        qv, gv, cosv, sinv = q_ref[...], g_ref[...], cos_ref[...], sin_ref[...]
        qr, r = _qk_fwd(qv, gv, cosv, sinv)
        qs = qr * scale
        qsb = qs.astype(BF16)
        dov, ov = do_ref[...], o_ref[...]
        dob = dov.astype(BF16)
        qst = qs.T.astype(BF16)
        dot_ = dov.T.astype(BF16)
        k, v = k_ref[0, 0], v_ref[0, 0]
        bias = _key_bias(s_len)
        dkt = jnp.zeros((HEAD_DIM, lk), F32)
        dvt = jnp.zeros((HEAD_DIM, lk), F32)
        dqs = []
        for h in range(GQA):
            hs = slice(HEAD_DIM * h, HEAD_DIM * (h + 1))
            s = _dot_nt(qsb[:, hs], k) + bias
            p = jnp.exp(s - jnp.max(s, axis=-1, keepdims=True))
            p = p * (1.0 / jnp.sum(p, axis=-1, keepdims=True))
            dp = _dot_nt(dob[:, hs], v)
            delta = jnp.sum(dov[:, hs] * ov[:, hs], axis=-1, keepdims=True)
            dsb = (p * (dp - delta)).astype(BF16)
            dqs.append(_dot(dsb, k))
            dkt = dkt + _dot(qst[hs, :], dsb)
            dvt = dvt + _dot(dot_[hs, :], p.astype(BF16))
        dqr = jnp.concatenate(dqs, axis=1) * scale
        dq, dgr = _qk_bwd(dqr, qv, r, gv, cosv, sinv)
        dq_o[...] = dq.astype(BF16)

        @pl.when(i == 0)
        def _():
            dkt_o[0, 0] = dkt
            dvt_o[0, 0] = dvt

        @pl.when(i > 0)
        def _():
            dkt_o[0, 0] += dkt
            dvt_o[0, 0] += dvt

        @pl.when((b == 0) & (g == 0) & (i == 0))
        def _():
            dg_o[...] = jnp.zeros_like(dg_o)

        dg_o[...] += jnp.sum(dgr, axis=0, keepdims=True)

    qspec = pl.BlockSpec((tq, GROUP_W), lambda b, g, i: (b * nq + i, g))
    kspec = pl.BlockSpec((1, 1, lk, HEAD_DIM), lambda b, g, i: (b, g, 0, 0))
    tspec = pl.BlockSpec((1, 1, HEAD_DIM, lk), lambda b, g, i: (b, g, 0, 0))
    rope = pl.BlockSpec((tq, GROUP_W), lambda b, g, i: (i, 0))
    vec = pl.BlockSpec((1, GROUP_W), lambda b, g, i: (0, 0))
    return pl.pallas_call(
        body, name="attn_bwd", grid=(nb, N_KV, nq),
        out_shape=(jax.ShapeDtypeStruct((rows, D), BF16), jax.ShapeDtypeStruct((nb, N_KV, HEAD_DIM, lk), F32),
                   jax.ShapeDtypeStruct((nb, N_KV, HEAD_DIM, lk), F32), jax.ShapeDtypeStruct((1, GROUP_W), F32)),
        in_specs=[qspec, kspec, kspec, qspec, qspec, vec, rope, rope],
        out_specs=(qspec, tspec, tspec, vec),
        compiler_params=_params(("arbitrary", "arbitrary", "arbitrary"), 56),
    )(pq, kr, vb, d_o, o, qg, cos, sin)


def _kv_bwd(dkt, dvt, pkv, pm_kv, kg, cos, sin, nb):
    rows = pkv.shape[0]
    s_len = rows // nb
    tk = 128
    nt = s_len // tk
    last = nt - 1

    def body(dk_ref, dv_ref, kv_ref, m_ref, g_ref, cos_ref, sin_ref, d_o, dm_o, dg_o):
        b, i = pl.program_id(0), pl.program_id(1)
        dkr = dk_ref[0].T
        dv = dv_ref[0].T
        gv = g_ref[...]

        @pl.when((b == 0) & (i == 0))
        def _():
            dg_o[...] = jnp.zeros_like(dg_o)

        @pl.when(i < nt)
        def _():
            kx = kv_ref[:, :GROUP_W]
            r = lax.rsqrt(_head_mean(kx * kx) + EPS)
            dk, dgr = _qk_bwd(dkr, kx, r, gv, cos_ref[...], sin_ref[...])
            d_o[:, :GROUP_W] = dk.astype(BF16)
            d_o[:, GROUP_W:] = dv.astype(BF16)
            dg_o[...] += jnp.sum(dgr, axis=0, keepdims=True)

        @pl.when(i == nt)
        def _():
            kx = m_ref[:, :GROUP_W]
            r = lax.rsqrt(_head_mean(kx * kx) + EPS)
            dn = dkr[0:N_META]
            dyg = dn * gv
            dm_o[0, :, :GROUP_W] = r * dyg - kx * (r * r * r) * _head_mean(dyg * kx)
            dm_o[0, :, GROUP_W:] = dv[0:N_META]
            dg_o[...] += jnp.sum(dn * kx * r, axis=0, keepdims=True)

    tspec = pl.BlockSpec((1, GROUP_W, tk), lambda b, i: (b, 0, i))
    rope = pl.BlockSpec((tk, GROUP_W), lambda b, i: (jnp.minimum(i, last), 0))
    return pl.pallas_call(
        body, name="kv_bwd", grid=(nb, nt + 1),
        out_shape=(jax.ShapeDtypeStruct((rows, 512), BF16), jax.ShapeDtypeStruct((nb, N_META, 512), F32),
                   jax.ShapeDtypeStruct((1, GROUP_W), F32)),
        in_specs=[tspec, tspec, pl.BlockSpec((tk, 512), lambda b, i: (b * nt + jnp.minimum(i, last), 0)),
                  pl.BlockSpec((N_META, 512), lambda b, i: (0, 0)), pl.BlockSpec((1, GROUP_W), lambda b, i: (0, 0)),
                  rope, rope],
        out_specs=(pl.BlockSpec((tk, 512), lambda b, i: (b * nt + jnp.minimum(i, last), 0)),
                   pl.BlockSpec((1, N_META, 512), lambda b, i: (b, 0, 0)),
                   pl.BlockSpec((1, GROUP_W), lambda b, i: (0, 0))),
        compiler_params=_params(("arbitrary", "arbitrary"), 32),
    )(dkt, dvt, pkv, pm_kv, kg, cos, sin)


def _conv_bwd(dc0, pconv, pm_conv, conv_w, nb, tm):
    rows = pconv.shape[0]
    nt = rows // nb // tm

    def body(dcur, dprev, dnxt, cur, prev, nxt, meta, w_ref, da_o, dam_o, gw_o, uext, dext):
        b, i = pl.program_id(0), pl.program_id(1)
        _fill_uext(uext, cur, prev, nxt, meta, i, nt, tm)
        dext[0:16] = jnp.zeros((16, D), F32)
        dext[16:32] = jnp.where(i == 0, 0.0, dprev[...])
        dext[32:32 + tm] = dcur[...]
        dext[32 + tm:48 + tm] = jnp.where(i == nt - 1, 0.0, dnxt[...])

        @pl.when((b == 0) & (i == 0))
        def _():
            gw_o[...] = jnp.zeros_like(gw_o)

        for c0 in range(0, D, 256):
            cs = slice(c0, c0 + 256)
            for r0 in range(0, tm, 32):
                acc = jnp.zeros((32, 256), F32)
                for j in range(CONV_K):
                    acc = acc + dext[r0 + 47 - j:r0 + 79 - j, cs] * w_ref[j:j + 1, cs]
                cv = cur[r0:r0 + 32, c0:c0 + 256]
                sg = _sig(cur[r0:r0 + 32, D + c0:D + c0 + 256])
                da_o[r0:r0 + 32, cs] = (acc * sg).astype(BF16)
                da_o[r0:r0 + 32, D + c0:D + c0 + 256] = (acc * cv * sg * (1.0 - sg)).astype(BF16)
            for j in range(CONV_K):
                acc = jnp.zeros((32, 256), F32)
                for r0 in range(0, tm, 32):
                    acc = acc + dext[32 + r0:64 + r0, cs] * uext[r0 + j + 1:r0 + j + 33, cs]
                gw_o[j:j + 1, cs] += jnp.sum(acc, axis=0, keepdims=True)

        @pl.when(i == 0)
        def _():
            for c0 in range(0, D, 256):
                cs = slice(c0, c0 + 256)
                acc = jnp.zeros((16, 256), F32)
                for j in range(CONV_K):
                    acc = acc + dext[31 - j:47 - j, cs] * w_ref[j:j + 1, cs]
                cv = meta[:, c0:c0 + 256]
                sg = _sig(meta[:, D + c0:D + c0 + 256])
                dam_o[0, :, cs] = acc * sg
                dam_o[0, :, D + c0:D + c0 + 256] = acc * cv * sg * (1.0 - sg)

    return pl.pallas_call(
        body, name="conv_bwd", grid=(nb, nt),
        out_shape=(jax.ShapeDtypeStruct((rows, 2048), BF16), jax.ShapeDtypeStruct((nb, N_META, 2048), F32),
                   jax.ShapeDtypeStruct((32, D), F32)),
        in_specs=_halo_specs(D, tm, nt, rows) + _halo_specs(2048, tm, nt, rows)
        + [pl.BlockSpec((16, 2048), lambda b, i: (0, 0)), pl.BlockSpec((32, D), lambda b, i: (0, 0))],
        out_specs=(pl.BlockSpec((tm, 2048), lambda b, i: (b * nt + i, 0)),
                   pl.BlockSpec((1, N_META, 2048), lambda b, i: (b, 0, 0)),
                   pl.BlockSpec((32, D), lambda b, i: (0, 0))),
        scratch_shapes=[pltpu.VMEM((tm + 32, D), F32), pltpu.VMEM((tm + 48, D), F32)],
        compiler_params=_params(("arbitrary", "arbitrary"), 40),
    )(dc0, dc0, dc0, pconv, pconv, pconv, pm_conv, conv_w)


def _meta_bwd(dam, ddm, w_full, meta_full, norm_g):
    nb = dam.shape[0]

    def body(a_ref, d_ref, wc_ref, wkv_ref, m_ref, g_ref, gm_o, dg_o):
        a, d = a_ref[0], d_ref[0]
        for b in range(1, nb):
            a = a + a_ref[b]
            d = d + d_ref[b]
        dxn = _dot_nt(a.astype(BF16), wc_ref[...]) + _dot_nt(d.astype(BF16), wkv_ref[...])
        v = m_ref[...]
        r = lax.rsqrt(jnp.mean(v * v, axis=-1, keepdims=True) + EPS)
        gm_o[...] = _rms_bwd(dxn, v, r, g_ref[...])
        dg_o[...] = jnp.sum(dxn * v * r, axis=0, keepdims=True)

    return pl.pallas_call(
        body, name="meta_bwd", grid=(1,),
        out_shape=(jax.ShapeDtypeStruct((N_META, D), F32), jax.ShapeDtypeStruct((1, D), F32)),
        in_specs=[pl.BlockSpec((nb, N_META, 2048), lambda i: (0, 0, 0)), pl.BlockSpec((nb, N_META, 512), lambda i: (0, 0, 0)),
                  pl.BlockSpec((D, 2048), lambda i: (0, 0)), pl.BlockSpec((D, 512), lambda i: (0, G_KV[0] // 512)),
                  pl.BlockSpec((N_META, D), lambda i: (0, 0)), pl.BlockSpec((1, D), lambda i: (0, 0))],
        out_specs=(pl.BlockSpec((N_META, D), lambda i: (0, 0)), pl.BlockSpec((1, D), lambda i: (0, 0))),
        compiler_params=_params(("arbitrary",), 32),
    )(dam, ddm, w_full, w_full, meta_full, norm_g)


def _dxn(d_groups, w_full, x2, dy, norm_g, dg_init, tm):
    rows = x2.shape[0]
    groups = (G_CONV, G_CZ, G_Q, G_KV, G_E)

    def body(da, db, dq, dd, de, w_hbm, x_ref, dy_ref, g_ref, gi_ref, gx_o, dg_o, w_vmem, sem):
        @pl.when(pl.program_id(0) == 0)
        def _():
            cp = pltpu.make_async_copy(w_hbm, w_vmem, sem)
            cp.start()
            cp.wait()
            dg_o[...] = gi_ref[...]

        dxn = jnp.zeros((tm, D), F32)
        for ref, (off, wd) in zip((da, db, dq, dd, de), groups):
            for c0 in range(0, wd, 512):
                dxn = dxn + _dot_nt(ref[:, c0:c0 + 512], w_vmem[:, off + c0:off + c0 + 512])
        v = x_ref[...]
        r = lax.rsqrt(jnp.mean(v * v, axis=-1, keepdims=True) + EPS)
        gx_o[...] = dy_ref[...] + _rms_bwd(dxn, v, r, g_ref[...])
        dg_o[...] += jnp.sum(dxn * v * r, axis=0, keepdims=True)

    row = lambda wd: pl.BlockSpec((tm, wd), lambda i: (i, 0))
    vec = pl.BlockSpec((1, D), lambda i: (0, 0))
    return pl.pallas_call(
        body, name="dxn", grid=(rows // tm,),
        out_shape=(jax.ShapeDtypeStruct((rows, D), F32), jax.ShapeDtypeStruct((1, D), F32)),
        in_specs=[row(wd) for _, wd in groups] + [pl.BlockSpec(memory_space=pl.ANY), row(D), row(D), vec, vec],
        out_specs=(row(D), vec),
        scratch_shapes=[pltpu.VMEM((D, IN_DIM), BF16), pltpu.SemaphoreType.DMA],
        compiler_params=_params(("arbitrary",), 56),
    )(*d_groups, w_full, x2, dy, norm_g, dg_init)


def _wgrad(at, b, buf, slot, col_off, name, meta=None):
    rows, n = b.shape
    tn, tk = 512, 512
    nk = rows // tk
    j0 = col_off // tn

    def body(*refs):
        if meta is None:
            at_ref, b_ref, _, o_ref = refs
        else:
            at_ref, b_ref, xm_ref, dm_ref, _, o_ref = refs
        k = pl.program_id(1)

        @pl.when(k == 0)
        def _():
            if meta is None:
                o_ref[0] = jnp.zeros((D, tn), F32)
            else:
                dm = dm_ref[0]
                for e in range(1, dm_ref.shape[0]):
                    dm = dm + dm_ref[e]
                dm = jnp.concatenate([dm, jnp.zeros((128 - N_META, tn), F32)], axis=0)
                o_ref[0] = _dot(xm_ref[...], dm.astype(BF16))

        o_ref[0] += _dot(at_ref[...], b_ref[...].astype(BF16))

    in_specs = [pl.BlockSpec((D, tk), lambda j, k: (0, k)), pl.BlockSpec((tk, tn), lambda j, k: (k, j))]
    args = [at, b]
    if meta is not None:
        xmt, dm = meta
        in_specs += [pl.BlockSpec((D, 128), lambda j, k: (0, 0)),
                     pl.BlockSpec((dm.shape[0], N_META, tn), lambda j, k: (0, 0, j))]
        args += [xmt, dm]
    in_specs.append(pl.BlockSpec(memory_space=pl.ANY))
    args.append(buf)
    return pl.pallas_call(
        body, name=name, grid=(n // tn, nk),
        out_shape=jax.ShapeDtypeStruct(buf.shape, F32),
        in_specs=in_specs,
        out_specs=pl.BlockSpec((1, D, tn), lambda j, k: (slot, 0, j0 + j)),
        input_output_aliases={len(args) - 1: 0},
        compiler_params=_params(("parallel", "arbitrary"), 32),
    )(*args)


def _rope_tables(s_len):
    pos = jnp.arange(s_len, dtype=jnp.int32)
    row_ids = (pos // GRID_W).astype(F32)
    col_ids = (pos % GRID_W).astype(F32)
    inv_freq = ROPE_THETA ** (-jnp.arange(ROPE_FREQS, dtype=F32) / ROPE_FREQS)
    a_row = row_ids[:, None] * inv_freq[None, :]
    a_col = col_ids[:, None] * inv_freq[None, :]
    ang = jnp.concatenate([a_row, a_row, a_col, a_col], axis=-1)
    return jnp.tile(jnp.cos(ang), (1, GQA)), jnp.tile(jnp.sin(ang), (1, GQA))


def _local_step(x, loss_target, norm_g, conv_b, cn_g, cn_b, q_g, k_g, w_full, w3_full, conv_w_full, meta_full):
    nb, s_len, _ = x.shape
    rows = nb * s_len
    x2 = x.reshape(rows, D)
    t2 = loss_target.reshape(rows, D)
    cos, sin = _rope_tables(s_len)
    qg = jnp.tile(q_g, (1, GQA))
    kg = jnp.tile(k_g, (1, N_KV))

    xnmt, pm_conv, pm_kv = _meta_fwd(meta_full, norm_g, w_full)
    pconv, pcz, pq, pkv, pe, xnt = _in_proj(x2, norm_g, w_full, 256)
    c0 = _conv_fwd(pconv, pm_conv, conv_w_full, conv_b, nb, 256)
    kr, vb = _kv_prep(pkv, pm_kv, kg, cos, sin, nb)
    o = _attn_fwd(pq, kr, vb, qg, cos, sin, nb, 256)
    dy, mt, c3t, o2t, dyc, dya, d_o, dc0, dcz, de, sums = _mid(x2, t2, c0, pcz, o, pe, w3_full, cn_g, cn_b, 128)
    dq, dkt, dvt, dqg = _attn_bwd(pq, kr, vb, d_o, o, qg, cos, sin, nb, 256)
    lk = s_len + KEY_PAD
    dd, ddm, dkg = _kv_bwd(dkt.reshape(nb, GROUP_W, lk), dvt.reshape(nb, GROUP_W, lk), pkv, pm_kv, kg, cos, sin, nb)
    da, dam, gcw = _conv_bwd(dc0, pconv, pm_conv, conv_w_full, nb, 256)
    gmeta, dng_m = _meta_bwd(dam, ddm, w_full, meta_full, norm_g)
    gx, dng = _dxn((da, dcz, dq, dd, de), w_full, x2, dy, norm_g, dng_m, 256)

    gw3 = lax.empty((3, D, D), F32)
    gw3 = _wgrad(c3t, dyc, gw3, 0, 0, "wgrad_conv_out")
    gw3 = _wgrad(o2t, dya, gw3, 1, 0, "wgrad_attn_out")
    gw3 = _wgrad(mt, dy, gw3, 2, 0, "wgrad_out")
    gwin = lax.empty((1, D, IN_DIM), F32)
    gwin = _wgrad(xnt, da, gwin, 0, G_CONV[0], "wgrad_in_conv", meta=(xnmt, dam))
    gwin = _wgrad(xnt, dcz, gwin, 0, G_CZ[0], "wgrad_in_cz")
    gwin = _wgrad(xnt, dq, gwin, 0, G_Q[0], "wgrad_in_q")
    gwin = _wgrad(xnt, dd, gwin, 0, G_KV[0], "wgrad_in_kv", meta=(xnmt, ddm))
    gwin = _wgrad(xnt, de, gwin, 0, G_E[0], "wgrad_in_e")

    zeros = jnp.zeros((1, D - 2 * GROUP_W), F32)
    smalls = jnp.concatenate([dng, sums[2:3], sums[0:1], sums[1:2], jnp.concatenate([dqg, dkg, zeros], axis=1),
                              sums[3:4], jnp.zeros((2, D), F32)], axis=0)
    return gx.reshape(nb, s_len, D), gwin, gw3, gcw, gmeta, smalls


def _xyc():
    return lax.axis_index("x"), lax.axis_index("y"), lax.axis_index("c")


def _reduce_sibling(gwin, gw3v, gcm, smalls):
    def body(gwin_ref, gw3_ref, gcm_ref, sm_ref, r_win, r_w3, r_cm, r_sm, send, recv, ssend, srecv, lsem):
        x, y, c = _xyc()
        o = 1 - c
        sib = (x, y, o)
        half = D // 2
        outs = ((gwin_ref.at[pl.ds(o * half, half), :], r_win), (gw3_ref.at[:, :, o], r_w3), (gcm_ref.at[o], r_cm))
        cps = []
        for a, (src, dst) in enumerate(outs):
            cp = pltpu.make_async_remote_copy(src_ref=src, dst_ref=dst, send_sem=send.at[a], recv_sem=recv.at[a],
                                              device_id=sib, device_id_type=MESH)
            cp.start()
            cps.append(cp)
        me = 4 * x + 2 * y + c
        loc = pltpu.make_async_copy(sm_ref, r_sm.at[me], lsem)
        loc.start()
        scps = []
        for d in range(1, 8):
            px, py, pc = (x + (d >> 2)) % 2, (y + ((d >> 1) & 1)) % 2, (c + (d & 1)) % 2
            cp = pltpu.make_async_remote_copy(src_ref=sm_ref, dst_ref=r_sm.at[me], send_sem=ssend.at[d - 1],
                                              recv_sem=srecv.at[d - 1], device_id=(px, py, pc), device_id_type=MESH)
            cp.start()
            scps.append((cp, 4 * px + 2 * py + pc))
        for cp in cps:
            cp.wait()
        for d, (cp, pid) in enumerate(scps):
            pltpu.make_async_remote_copy(src_ref=sm_ref, dst_ref=r_sm.at[pid], send_sem=ssend.at[d],
                                         recv_sem=srecv.at[d], device_id=(x, y, c), device_id_type=MESH).wait_recv()
            cp.wait_send()
        loc.wait()

    any_spec = pl.BlockSpec(memory_space=pl.ANY)
    return pl.pallas_call(
        body, name="reduce_sibling",
        out_shape=(jax.ShapeDtypeStruct((D // 2, IN_DIM), F32), jax.ShapeDtypeStruct((3, 4, 128, D), F32),
                   jax.ShapeDtypeStruct((24, D), F32), jax.ShapeDtypeStruct((8, 8, D), F32)),
        in_specs=[any_spec] * 4, out_specs=(any_spec,) * 4,
        scratch_shapes=[pltpu.SemaphoreType.DMA((3,)), pltpu.SemaphoreType.DMA((3,)),
                        pltpu.SemaphoreType.DMA((7,)), pltpu.SemaphoreType.DMA((7,)), pltpu.SemaphoreType.DMA],
    )(gwin, gw3v, gcm, smalls)


def _add_sibling(gwin, gw3v, gcm, r_win, r_w3, r_cm):
    c = lax.axis_index("c").astype(jnp.int32).reshape(1)
    half = D // 2
    tr = 64

    def body1(c_ref, a_ref, b_ref, o_ref):
        o_ref[...] = a_ref[...] + b_ref[...]

    cs_win = pl.pallas_call(
        body1, name="add_sibling_w_in", out_shape=jax.ShapeDtypeStruct((half, IN_DIM), F32),
        grid_spec=pltpu.PrefetchScalarGridSpec(
            num_scalar_prefetch=1, grid=(half // tr,),
            in_specs=[pl.BlockSpec((tr, IN_DIM), lambda i, c_ref: (c_ref[0] * (half // tr) + i, 0)),
                      pl.BlockSpec((tr, IN_DIM), lambda i, c_ref: (i, 0))],
            out_specs=pl.BlockSpec((tr, IN_DIM), lambda i, c_ref: (i, 0))),
        compiler_params=_params(("parallel",), 32),
    )(c, gwin, r_win)

    def body2(c_ref, a_ref, b_ref, o_ref):
        o_ref[0, 0] = a_ref[0, 0, 0] + b_ref[0, 0]

    cs_w3 = pl.pallas_call(
        body2, name="add_sibling_w3", out_shape=jax.ShapeDtypeStruct((3, 4, 128, D), F32),
        grid_spec=pltpu.PrefetchScalarGridSpec(
            num_scalar_prefetch=1, grid=(3, 4),
            in_specs=[pl.BlockSpec((1, 1, 1, 128, D), lambda w, s, c_ref: (w, s, c_ref[0], 0, 0)),
                      pl.BlockSpec((1, 1, 128, D), lambda w, s, c_ref: (w, s, 0, 0))],
            out_specs=pl.BlockSpec((1, 1, 128, D), lambda w, s, c_ref: (w, s, 0, 0))),
        compiler_params=_params(("parallel", "parallel"), 32),
    )(c, gw3v, r_w3)

    def body3(c_ref, a_ref, b_ref, o_ref):
        o_ref[...] = a_ref[0] + b_ref[...]

    cs_cm = pl.pallas_call(
        body3, name="add_sibling_cm", out_shape=jax.ShapeDtypeStruct((24, D), F32),
        grid_spec=pltpu.PrefetchScalarGridSpec(
            num_scalar_prefetch=1, grid=(1,),
            in_specs=[pl.BlockSpec((1, 24, D), lambda i, c_ref: (c_ref[0], 0, 0)),
                      pl.BlockSpec((24, D), lambda i, c_ref: (0, 0))],
            out_specs=pl.BlockSpec((24, D), lambda i, c_ref: (0, 0))),
        compiler_params=_params(("arbitrary",), 32),
    )(c, gcm, r_cm)
    return cs_win, cs_w3, cs_cm


def _reduce_chips(cs_win, cs_w3, cs_cm):
    def body(win_ref, w3_ref, cm_ref, r_win, r_w3, r_cm, send, recv):
        x, y, c = _xyc()
        peers = ((1 - x, y), (x, 1 - y), (1 - x, 1 - y))
        cps = []
        for k, (px, py) in enumerate(peers):
            ps = 2 * px + py
            items = ((win_ref.at[:, pl.ds(ps * W_IN_SHARD, W_IN_SHARD)], r_win.at[k]),
                     (w3_ref.at[:, ps], r_w3.at[k]),
                     (cm_ref.at[:, pl.ds(ps * ROW_SHARD, ROW_SHARD)], r_cm.at[k]))
            for a, (src, dst) in enumerate(items):
                cp = pltpu.make_async_remote_copy(src_ref=src, dst_ref=dst, send_sem=send.at[a, k],
                                                  recv_sem=recv.at[a, k], device_id=(px, py, c), device_id_type=MESH)
                cp.start()
                cps.append(cp)
        for cp in cps:
            cp.wait()

    any_spec = pl.BlockSpec(memory_space=pl.ANY)
    return pl.pallas_call(
        body, name="reduce_chips",
        out_shape=(jax.ShapeDtypeStruct((3, D // 2, W_IN_SHARD), F32), jax.ShapeDtypeStruct((3, 3, 128, D), F32),
                   jax.ShapeDtypeStruct((3, 24, ROW_SHARD), F32)),
        in_specs=[any_spec] * 3, out_specs=(any_spec,) * 3,
        scratch_shapes=[pltpu.SemaphoreType.DMA((3, 3)), pltpu.SemaphoreType.DMA((3, 3))],
    )(cs_win, cs_w3, cs_cm)


def _add_chips(cs_win, cs_w3, cs_cm, r_win, r_w3, r_cm):
    x, y, c = _xyc()
    idx = jnp.stack([2 * x + y, c]).astype(jnp.int32)
    half = D // 2
    tr = 128

    def body1(i_ref, a_ref, b_ref, o_ref):
        o_ref[0] = (a_ref[...] + b_ref[2]) + (b_ref[0] + b_ref[1])

    f_win = pl.pallas_call(
        body1, name="add_chips_w_in", out_shape=jax.ShapeDtypeStruct((2, half, W_IN_SHARD), F32),
        grid_spec=pltpu.PrefetchScalarGridSpec(
            num_scalar_prefetch=1, grid=(half // tr,),
            in_specs=[pl.BlockSpec((tr, W_IN_SHARD), lambda i, r: (i, r[0])),
                      pl.BlockSpec((3, tr, W_IN_SHARD), lambda i, r: (0, i, 0))],
            out_specs=pl.BlockSpec((1, tr, W_IN_SHARD), lambda i, r: (r[1], i, 0))),
        compiler_params=_params(("parallel",), 32),
    )(idx, cs_win, r_win)

    def body2(i_ref, a_ref, b_ref, o_ref):
        o_ref[0, 0] = (a_ref[0, 0] + b_ref[2, 0]) + (b_ref[0, 0] + b_ref[1, 0])

    f_w3 = pl.pallas_call(
        body2, name="add_chips_w3", out_shape=jax.ShapeDtypeStruct((3, 2, 128, D), F32),
        grid_spec=pltpu.PrefetchScalarGridSpec(
            num_scalar_prefetch=1, grid=(3,),
            in_specs=[pl.BlockSpec((1, 1, 128, D), lambda w, r: (w, r[0], 0, 0)),
                      pl.BlockSpec((3, 1, 128, D), lambda w, r: (0, w, 0, 0))],
            out_specs=pl.BlockSpec((1, 1, 128, D), lambda w, r: (w, r[1], 0, 0))),
        compiler_params=_params(("parallel",), 32),
    )(idx, cs_w3, r_w3)

    def body3(i_ref, a_ref, b_ref, o_ref):
        o_ref[0] = (a_ref[...] + b_ref[2]) + (b_ref[0] + b_ref[1])

    f_cm = pl.pallas_call(
        body3, name="add_chips_cm", out_shape=jax.ShapeDtypeStruct((2, 24, ROW_SHARD), F32),
        grid_spec=pltpu.PrefetchScalarGridSpec(
            num_scalar_prefetch=1, grid=(1,),
            in_specs=[pl.BlockSpec((24, ROW_SHARD), lambda i, r: (0, r[0])),
                      pl.BlockSpec((3, 24, ROW_SHARD), lambda i, r: (0, 0, 0))],
            out_specs=pl.BlockSpec((1, 24, ROW_SHARD), lambda i, r: (r[1], 0, 0))),
        compiler_params=_params(("arbitrary",), 32),
    )(idx, cs_cm, r_cm)
    return f_win, f_w3, f_cm


def _share_sibling(f_win, f_w3, f_cm):
    def body(win_in, w3_in, cm_in, win_ref, w3_ref, cm_ref, send, recv):
        x, y, c = _xyc()
        o = 1 - c
        cps = []
        for a, (ref, sl) in enumerate(((win_ref, lambda h: win_ref.at[h]), (w3_ref, lambda h: w3_ref.at[:, h]),
                                       (cm_ref, lambda h: cm_ref.at[h]))):
            cp = pltpu.make_async_remote_copy(src_ref=sl(c), dst_ref=sl(c), send_sem=send.at[a], recv_sem=recv.at[a],
                                              device_id=(x, y, o), device_id_type=MESH)
            cp.start()
            cps.append((cp, sl))
        for a, (cp, sl) in enumerate(cps):
            pltpu.make_async_remote_copy(src_ref=sl(o), dst_ref=sl(o), send_sem=send.at[a], recv_sem=recv.at[a],
                                         device_id=(x, y, o), device_id_type=MESH).wait_recv()
            cp.wait_send()

    any_spec = pl.BlockSpec(memory_space=pl.ANY)
    return pl.pallas_call(
        body, name="share_sibling",
        out_shape=(jax.ShapeDtypeStruct(f_win.shape, F32), jax.ShapeDtypeStruct(f_w3.shape, F32),
                   jax.ShapeDtypeStruct(f_cm.shape, F32)),
        in_specs=[any_spec] * 3, out_specs=(any_spec,) * 3,
        input_output_aliases={0: 0, 1: 1, 2: 2},
        scratch_shapes=[pltpu.SemaphoreType.DMA((3,)), pltpu.SemaphoreType.DMA((3,))],
    )(f_win, f_w3, f_cm)


def _adamw_math(w, g, m, v):
    m = ADAM_B1 * m + (1.0 - ADAM_B1) * g
    v = ADAM_B2 * v + (1.0 - ADAM_B2) * (g * g)
    m_hat = m / (1.0 - ADAM_B1 ** ADAM_STEP)
    v_hat = v / (1.0 - ADAM_B2 ** ADAM_STEP)
    delta = -ADAM_LR * (m_hat / (jnp.sqrt(v_hat) + ADAM_EPS) + ADAM_WD * w)
    return delta, m, v


def _adamw(w, g, m, v, tr, name):
    rows, cols = w.shape

    def body(w_ref, g_ref, m_ref, v_ref, d_o, m_o, v_o):
        d_o[...], m_o[...], v_o[...] = _adamw_math(w_ref[...], g_ref[...], m_ref[...], v_ref[...])

    spec = pl.BlockSpec((tr, cols), lambda i: (i, 0))
    return pl.pallas_call(
        body, name=name, grid=(rows // tr,),
        out_shape=(jax.ShapeDtypeStruct((rows, cols), F32),) * 3,
        in_specs=[spec] * 4, out_specs=(spec,) * 3,
        compiler_params=_params(("parallel",), 32),
    )(w, g, m, v)


def _adamw_small(r_sm, w8, m8, v8):
    def body(s_ref, w_ref, m_ref, v_ref, g_o, d_o, m_o, v_o):
        g = s_ref[0]
        for dev in range(1, 8):
            g = g + s_ref[dev]
        qk = g[4:5, :]
        qg = qk[:, 0:HEAD_DIM]
        kg = qk[:, GROUP_W:GROUP_W + HEAD_DIM]
        for h in range(1, GQA):
            qg = qg + qk[:, HEAD_DIM * h:HEAD_DIM * (h + 1)]
            kg = kg + qk[:, GROUP_W + HEAD_DIM * h:GROUP_W + HEAD_DIM * (h + 1)]
        row4 = jnp.concatenate([qg, kg, jnp.zeros((1, D - 2 * HEAD_DIM), F32)], axis=1)
        loss = (0.5 / D) * jnp.sum(g[5:6, :], axis=-1, keepdims=True)
        row5 = jnp.broadcast_to(loss, (1, D))
        g = jnp.concatenate([g[0:4], row4, row5, g[6:8]], axis=0)
        g_o[...] = g
        d_o[...], m_o[...], v_o[...] = _adamw_math(w_ref[...], g, m_ref[...], v_ref[...])

    return pl.pallas_call(
        body, name="adamw_small", out_shape=(jax.ShapeDtypeStruct((8, D), F32),) * 4,
    )(r_sm, w8, m8, v8)


def _pack8(norm_g, conv_b, cn_g, cn_b, q_g, k_g):
    qk = jnp.concatenate([q_g, k_g, jnp.zeros((1, D - 2 * HEAD_DIM), F32)], axis=1)
    return jnp.concatenate([norm_g, conv_b, cn_g, cn_b, qk, jnp.zeros((3, D), F32)], axis=0)


def _unpack8(a):
    return a[0:1], a[1:2], a[2:3], a[3:4], a[4:5, 0:HEAD_DIM], a[4:5, HEAD_DIM:2 * HEAD_DIM]


def kernel(x, meta_tokens, norm_g, w_in, conv_w, conv_b, conv_norm_g, conv_norm_b, w_conv_out, q_norm_g, k_norm_g, w_attn_out, w_out, loss_target, m_meta_tokens, m_norm_g, m_w_in, m_conv_w, m_conv_b, m_conv_norm_g, m_conv_norm_b, m_w_conv_out, m_q_norm_g, m_k_norm_g, m_w_attn_out, m_w_out, v_meta_tokens, v_norm_g, v_w_in, v_conv_w, v_conv_b, v_conv_norm_g, v_conv_norm_b, v_w_conv_out, v_q_norm_g, v_k_norm_g, v_w_attn_out, v_w_out):
    pad_k = lambda a: jnp.pad(a[0], ((0, 32 - CONV_K), (0, 0)))
    w3_s = jnp.concatenate([w_conv_out, w_attn_out, w_out], axis=0)
    w_full, w3_full, conv_w_full, meta_full = _gather_weights(w_in[0], w3_s, pad_k(conv_w), meta_tokens)

    gx, gwin, gw3, gcw, gmeta, smalls = _local_step(
        x, loss_target, norm_g, conv_b, conv_norm_g, conv_norm_b, q_norm_g, k_norm_g,
        w_full, w3_full, conv_w_full, meta_full)

    gwin2 = gwin.reshape(D, IN_DIM)
    gw3v = gw3.reshape(3, N_CHIPS, 2, 128, D)
    gcm = jnp.concatenate([gcw.reshape(2, 16, D), gmeta.reshape(2, 8, D)], axis=1)
    r_win, r_w3, r_cm, r_sm = _reduce_sibling(gwin2, gw3v, gcm, smalls)
    cs_win, cs_w3, cs_cm = _add_sibling(gwin2, gw3v, gcm, r_win, r_w3, r_cm)
    r2_win, r2_w3, r2_cm = _reduce_chips(cs_win, cs_w3, cs_cm)
    f_win, f_w3, f_cm = _add_chips(cs_win, cs_w3, cs_cm, r2_win, r2_w3, r2_cm)
    f_win, f_w3, f_cm = _share_sibling(f_win, f_w3, f_cm)

    g_w_in = f_win.reshape(D, W_IN_SHARD)
    g_w3 = f_w3.reshape(3, ROW_SHARD, D)
    g_conv_w = f_cm[:, 0:16].reshape(32, ROW_SHARD)
    g_meta = f_cm[:, 16:24].reshape(N_META, ROW_SHARD)

    d_w_in, nm_w_in, nv_w_in = _adamw(w_in[0], g_w_in, m_w_in[0], v_w_in[0], 128, "adamw_w_in")
    m3 = jnp.concatenate([m_w_conv_out, m_w_attn_out, m_w_out], axis=0).reshape(3 * ROW_SHARD, D)
    v3 = jnp.concatenate([v_w_conv_out, v_w_attn_out, v_w_out], axis=0).reshape(3 * ROW_SHARD, D)
    d_w3, nm_w3, nv_w3 = _adamw(w3_s.reshape(3 * ROW_SHARD, D), g_w3.reshape(3 * ROW_SHARD, D), m3, v3, 256, "adamw_w3")
    cm_w = jnp.concatenate([pad_k(conv_w), meta_tokens], axis=0)
    cm_m = jnp.concatenate([pad_k(m_conv_w), m_meta_tokens], axis=0)
    cm_v = jnp.concatenate([pad_k(v_conv_w), v_meta_tokens], axis=0)
    cm_g = jnp.concatenate([g_conv_w, g_meta], axis=0)
    d_cm, nm_cm, nv_cm = _adamw(cm_w, cm_g, cm_m, cm_v, 48, "adamw_cm")
    g8, d8, nm8, nv8 = _adamw_small(
        r_sm, _pack8(norm_g, conv_b, conv_norm_g, conv_norm_b, q_norm_g, k_norm_g),
        _pack8(m_norm_g, m_conv_b, m_conv_norm_g, m_conv_norm_b, m_q_norm_g, m_k_norm_g),
        _pack8(v_norm_g, v_conv_b, v_conv_norm_g, v_conv_norm_b, v_q_norm_g, v_k_norm_g))

    def assemble(big_in, w3x, cmx, s8):
        w3x = w3x.reshape(3, 1, ROW_SHARD, D)
        ng, cb, cng, cnb, qg, kg = _unpack8(s8)
        return (cmx[32:48], ng, big_in[None], cmx[None, 0:CONV_K], cb, cng, cnb, w3x[0], qg, kg, w3x[1], w3x[2])

    loss = g8[5, 0]
    grads = assemble(g_w_in, g_w3, cm_g, g8)
    deltas = assemble(d_w_in, d_w3, d_cm, d8)
    new_m = assemble(nm_w_in, nm_w3, nm_cm, nm8)
    new_v = assemble(nv_w_in, nv_w3, nv_cm, nv8)
    return (loss, gx, *grads, *deltas, *new_m, *new_v)
```

```python
import functools
import math

import jax
import jax.numpy as jnp
from jax import lax
from jax.experimental import pallas as pl
from jax.experimental.pallas import tpu as pltpu

F32, BF16 = jnp.float32, jnp.bfloat16
MESH = pl.DeviceIdType.MESH

D = 1024
N_META = 16
CONV_K = 31
N_KV = 4
GQA = 4
HEAD_DIM = 64
GROUP_W = GQA * HEAD_DIM
GRID_W = 64
ROPE_FREQS = 16
ROPE_THETA = 10000.0
EPS = 1e-6
IN_DIM = 7680
KEY_PAD = 128
G_CONV, G_CZ, G_Q, G_KV, G_E = (0, 2048), (2048, 1024), (3072, 1024), (4096, 512), (4608, 3072)
N_CHIPS = 4
W_IN_SHARD = IN_DIM // N_CHIPS
ROW_SHARD = D // N_CHIPS

ADAM_LR, ADAM_B1, ADAM_B2, ADAM_EPS, ADAM_WD, ADAM_STEP = 0.001, 0.9, 0.999, 1e-08, 0.01, 10

NT_DIMS = (((1,), (1,)), ((), ()))


def _params(sem=None, vmem_mb=48):
    return pltpu.CompilerParams(dimension_semantics=sem, vmem_limit_bytes=vmem_mb << 20)


def _sig(v):
    return jax.nn.sigmoid(v)


def _dsilu(v, s):
    return s * (1.0 + v * (1.0 - s))


def _dot(a, b):
    return jnp.dot(a, b, preferred_element_type=F32)


def _dot_nt(a, b):
    return lax.dot_general(a, b, NT_DIMS, preferred_element_type=F32)


def _head_mean(v):
    lane = lax.broadcasted_iota(jnp.int32, v.shape, 1)
    out = jnp.zeros_like(v)
    for h in range(GQA):
        m = jnp.sum(v[:, HEAD_DIM * h:HEAD_DIM * (h + 1)], axis=-1, keepdims=True) * (1.0 / HEAD_DIM)
        out = jnp.where((lane >= HEAD_DIM * h) & (lane < HEAD_DIM * (h + 1)), m, out)
    return out


def _rot(v):
    lane = lax.broadcasted_iota(jnp.int32, v.shape, 1)
    return jnp.where((lane & 16) == 0, -pltpu.roll(v, GROUP_W - 16, 1), pltpu.roll(v, 16, 1))


def _qk_fwd(v, g, cos, sin):
    r = lax.rsqrt(_head_mean(v * v) + EPS)
    n = v * r * g
    return n * cos + _rot(n) * sin, r


def _qk_bwd(dy, v, r, g, cos, sin):
    dn = dy * cos - _rot(dy) * sin
    dyg = dn * g
    dv = r * dyg - v * (r * r * r) * _head_mean(dyg * v)
    return dv, dn * v * r


def _rms_bwd(dxn, v, r, g):
    dxg = dxn * g
    return r * dxg - v * (r * r * r) * jnp.mean(dxg * v, axis=-1, keepdims=True)


def _glu(a):
    return a[:, :D] * _sig(a[:, D:])


def _gather_weights(w_in_s, w3_s, conv_w_s, meta_s):
    def body(win_ref, w3_ref, cw_ref, mt_ref, win_o, w3_o, cw_o, mt_o, win_b, w3_b, send, recv, fsend, frecv, lsem):
        x, y, c = _xyc()
        o = 1 - c
        me = 2 * x + y
        win_b[...] = win_ref[...].astype(BF16)
        w3_b[...] = w3_ref[...].astype(BF16)
        items = (
            (lambda h: win_b.at[pl.ds(h * 512, 512), :],
             lambda p, h: win_o.at[pl.ds(h * 512, 512), pl.ds(p * W_IN_SHARD, W_IN_SHARD)]),
            (lambda h: w3_b.at[:, pl.ds(h * 128, 128), :],
             lambda p, h: w3_o.at[:, pl.ds(p * ROW_SHARD + h * 128, 128), :]),
            (lambda h: cw_ref.at[pl.ds(h * 16, 16), :],
             lambda p, h: cw_o.at[pl.ds(h * 16, 16), pl.ds(p * ROW_SHARD, ROW_SHARD)]),
            (lambda h: mt_ref.at[pl.ds(h * 8, 8), :],
             lambda p, h: mt_o.at[pl.ds(h * 8, 8), pl.ds(p * ROW_SHARD, ROW_SHARD)]),
        )
        peers = ((1 - x, y), (x, 1 - y), (1 - x, 1 - y))

        def remote(src, dst, s_sem, r_sem, to):
            return pltpu.make_async_remote_copy(src_ref=src, dst_ref=dst, send_sem=s_sem, recv_sem=r_sem,
                                                device_id=to, device_id_type=MESH)

        started = []
        for a, (half, place) in enumerate(items):
            for h in range(2):
                loc = pltpu.make_async_copy(half(h), place(me, h), lsem.at[a, h])
                loc.start()
                started.append(loc.wait)
            for k, (px, py) in enumerate(peers):
                cp = remote(half(c), place(me, c), send.at[a, k], recv.at[a, k], (px, py, c))
                cp.start()
                started.append(cp.wait_send)
        for k, (px, py) in enumerate(peers):
            for a, (half, place) in enumerate(items):
                got = place(2 * px + py, c)
                remote(got, got, send.at[a, k], recv.at[a, k], (px, py, c)).wait_recv()
                fw = remote(got, got, fsend.at[a, k], frecv.at[a, k], (x, y, o))
                fw.start()
                started.append(fw.wait_send)
        for k, (px, py) in enumerate(peers):
            for a, (half, place) in enumerate(items):
                theirs = place(2 * px + py, o)
                remote(theirs, theirs, fsend.at[a, k], frecv.at[a, k], (x, y, o)).wait_recv()
        for wait in started:
            wait()

    any_spec = pl.BlockSpec(memory_space=pl.ANY)
    vmem = pl.BlockSpec(memory_space=pltpu.VMEM)
    return pl.pallas_call(
        body, name="gather_weights",
        out_shape=(jax.ShapeDtypeStruct((D, IN_DIM), BF16), jax.ShapeDtypeStruct((3, D, D), BF16),
                   jax.ShapeDtypeStruct((32, D), F32), jax.ShapeDtypeStruct((N_META, D), F32)),
        in_specs=[vmem, vmem, vmem, vmem],
        out_specs=(any_spec, any_spec, any_spec, any_spec),
        scratch_shapes=[pltpu.VMEM((D, W_IN_SHARD), BF16), pltpu.VMEM((3, ROW_SHARD, D), BF16),
                        pltpu.SemaphoreType.DMA((4, 3)), pltpu.SemaphoreType.DMA((4, 3)),
                        pltpu.SemaphoreType.DMA((4, 3)), pltpu.SemaphoreType.DMA((4, 3)),
                        pltpu.SemaphoreType.DMA((4, 2))],
        compiler_params=pltpu.CompilerParams(vmem_limit_bytes=40 << 20),
    )(w_in_s, w3_s, conv_w_s, meta_s)


def _meta_fwd(meta_full, norm_g, w_full):
    def body(m_ref, g_ref, wc_ref, wkv_ref, xnt_ref, pc_ref, pkv_ref):
        v = m_ref[...]
        r = lax.rsqrt(jnp.mean(v * v, axis=-1, keepdims=True) + EPS)
        xn = v * r * g_ref[...]
        xnb = xn.astype(BF16)
        pad = jnp.concatenate([xn, jnp.zeros((128 - N_META, D), F32)], axis=0)
        xnt_ref[...] = pad.T.astype(BF16)
        pc_ref[...] = _dot(xnb, wc_ref[...])
        pkv_ref[...] = _dot(xnb, wkv_ref[...])

    return pl.pallas_call(
        body, name="meta_fwd", grid=(1,),
        out_shape=(jax.ShapeDtypeStruct((D, 128), BF16), jax.ShapeDtypeStruct((N_META, 2048), F32),
                   jax.ShapeDtypeStruct((N_META, 512), F32)),
        in_specs=[pl.BlockSpec((N_META, D), lambda i: (0, 0)), pl.BlockSpec((1, D), lambda i: (0, 0)),
                  pl.BlockSpec((D, 2048), lambda i: (0, 0)), pl.BlockSpec((D, 512), lambda i: (0, G_KV[0] // 512))],
        out_specs=(pl.BlockSpec((D, 128), lambda i: (0, 0)), pl.BlockSpec((N_META, 2048), lambda i: (0, 0)),
                   pl.BlockSpec((N_META, 512), lambda i: (0, 0))),
        compiler_params=_params(("arbitrary",), 32),
    )(meta_full, norm_g, w_full, w_full)


def _in_proj(x2, norm_g, w_full, tm):
    rows = x2.shape[0]
    groups = (G_CONV, G_CZ, G_Q, G_KV, G_E)

    def body(x_ref, g_ref, w_hbm, *rest):
        outs, xnt_ref, w_vmem, sem = rest[:5], rest[5], rest[6], rest[7]

        @pl.when(pl.program_id(0) == 0)
        def _():
            cp = pltpu.make_async_copy(w_hbm, w_vmem, sem)
            cp.start()
            cp.wait()

        v = x_ref[...]
        r = lax.rsqrt(jnp.mean(v * v, axis=-1, keepdims=True) + EPS)
        xn = v * r * g_ref[...]
        xnb = xn.astype(BF16)
        xnt_ref[...] = xn.T.astype(BF16)
        for ref, (off, wd) in zip(outs, groups):
            for c0 in range(0, wd, 512):
                ref[:, c0:c0 + 512] = _dot(xnb, w_vmem[:, off + c0:off + c0 + 512])

    return pl.pallas_call(
        body, name="in_proj", grid=(rows // tm,),
        out_shape=tuple(jax.ShapeDtypeStruct((rows, wd), F32) for _, wd in groups)
        + (jax.ShapeDtypeStruct((D, rows), BF16),),
        in_specs=[pl.BlockSpec((tm, D), lambda i: (i, 0)), pl.BlockSpec((1, D), lambda i: (0, 0)),
                  pl.BlockSpec(memory_space=pl.ANY)],
        out_specs=tuple(pl.BlockSpec((tm, wd), lambda i: (i, 0)) for _, wd in groups)
        + (pl.BlockSpec((D, tm), lambda i: (0, i)),),
        scratch_shapes=[pltpu.VMEM((D, IN_DIM), BF16), pltpu.SemaphoreType.DMA],
        compiler_params=_params(("arbitrary",), 56),
    )(x2, norm_g, w_full)


def _halo_specs(width, tm, nt, rows):
    h16 = tm // 16
    return [pl.BlockSpec((tm, width), lambda b, i: (b * nt + i, 0)),
            pl.BlockSpec((16, width), lambda b, i: (jnp.maximum((b * nt + i) * h16 - 1, 0), 0)),
            pl.BlockSpec((16, width), lambda b, i: (jnp.minimum((b * nt + i + 1) * h16, rows // 16 - 1), 0))]


def _fill_uext(uext, cur, prev, nxt, meta, i, nt, tm):
    uext[0:16] = jnp.where(i == 0, _glu(meta[...]), _glu(prev[...]))
    uext[16:16 + tm] = _glu(cur[...])
    uext[16 + tm:32 + tm] = jnp.where(i == nt - 1, 0.0, _glu(nxt[...]))


def _conv_fwd(pconv, pm_conv, conv_w, conv_b, nb, tm):
    rows = pconv.shape[0]
    nt = rows // nb // tm

    def body(cur, prev, nxt, meta, w_ref, b_ref, o_ref, uext):
        i = pl.program_id(1)
        _fill_uext(uext, cur, prev, nxt, meta, i, nt, tm)
        for r0 in range(0, tm, 32):
            for c0 in range(0, D, 256):
                acc = jnp.zeros((32, 256), F32) + b_ref[:, c0:c0 + 256]
                for j in range(CONV_K):
                    acc = acc + uext[r0 + j + 1:r0 + j + 33, c0:c0 + 256] * w_ref[j:j + 1, c0:c0 + 256]
                o_ref[r0:r0 + 32, c0:c0 + 256] = acc

    return pl.pallas_call(
        body, name="conv_fwd", grid=(nb, nt),
        out_shape=jax.ShapeDtypeStruct((rows, D), F32),
        in_specs=_halo_specs(2048, tm, nt, rows)
        + [pl.BlockSpec((16, 2048), lambda b, i: (0, 0)), pl.BlockSpec((32, D), lambda b, i: (0, 0)),
           pl.BlockSpec((1, D), lambda b, i: (0, 0))],
        out_specs=pl.BlockSpec((tm, D), lambda b, i: (b * nt + i, 0)),
        scratch_shapes=[pltpu.VMEM((tm + 32, D), F32)],
        compiler_params=_params(("parallel", "parallel"), 40),
    )(pconv, pconv, pconv, pm_conv, conv_w, conv_b)


def _kv_prep(pkv, pm_kv, kg, cos, sin, nb):
    rows = pkv.shape[0]
    s_len = rows // nb
    tk = 128
    nt = s_len // tk

    def body(kv_ref, m_ref, g_ref, cos_ref, sin_ref, k_o, v_o):
        i = pl.program_id(1)

        @pl.when(i < nt)
        def _():
            kv = kv_ref[...]
            kr, _ = _qk_fwd(kv[:, :GROUP_W], g_ref[...], cos_ref[...], sin_ref[...])
            for h in range(N_KV):
                k_o[0, h] = kr[:, HEAD_DIM * h:HEAD_DIM * (h + 1)].astype(BF16)
                v_o[0, h] = kv[:, GROUP_W + HEAD_DIM * h:GROUP_W + HEAD_DIM * (h + 1)].astype(BF16)

        @pl.when(i == nt)
        def _():
            kv = m_ref[...]
            km = kv[:, :GROUP_W]
            kn = km * lax.rsqrt(_head_mean(km * km) + EPS) * g_ref[...]
            zeros = jnp.zeros((tk - N_META, GROUP_W), F32)
            kfull = jnp.concatenate([kn, zeros], axis=0)
            vfull = jnp.concatenate([kv[:, GROUP_W:], zeros], axis=0)
            for h in range(N_KV):
                k_o[0, h] = kfull[:, HEAD_DIM * h:HEAD_DIM * (h + 1)].astype(BF16)
                v_o[0, h] = vfull[:, HEAD_DIM * h:HEAD_DIM * (h + 1)].astype(BF16)

    lk = s_len + KEY_PAD
    last = nt - 1
    return pl.pallas_call(
        body, name="kv_prep", grid=(nb, nt + 1),
        out_shape=(jax.ShapeDtypeStruct((nb, N_KV, lk, HEAD_DIM), BF16),) * 2,
        in_specs=[pl.BlockSpec((tk, 512), lambda b, i: (b * nt + jnp.minimum(i, last), 0)),
                  pl.BlockSpec((N_META, 512), lambda b, i: (0, 0)), pl.BlockSpec((1, GROUP_W), lambda b, i: (0, 0)),
                  pl.BlockSpec((tk, GROUP_W), lambda b, i: (jnp.minimum(i, last), 0)),
                  pl.BlockSpec((tk, GROUP_W), lambda b, i: (jnp.minimum(i, last), 0))],
        out_specs=(pl.BlockSpec((1, N_KV, tk, HEAD_DIM), lambda b, i: (b, 0, i, 0)),) * 2,
        compiler_params=_params(("parallel", "arbitrary"), 32),
    )(pkv, pm_kv, kg, cos, sin)


def _key_bias(s_len):
    col = lax.broadcasted_iota(jnp.int32, (1, s_len + KEY_PAD), 1)
    return jnp.where(col < s_len + N_META, 0.0, -1e30).astype(F32)


def _attn_fwd(pq, kr, vb, qg, cos, sin, nb, tq):
    rows = pq.shape[0]
    s_len = rows // nb
    nq = s_len // tq
    lk = s_len + KEY_PAD

    def body(q_ref, k_ref, v_ref, g_ref, cos_ref, sin_ref, o_ref):
        qr, _ = _qk_fwd(q_ref[...], g_ref[...], cos_ref[...], sin_ref[...])
        qs = (qr * (1.0 / math.sqrt(HEAD_DIM))).astype(BF16)
        k, v = k_ref[0, 0], v_ref[0, 0]
        bias = _key_bias(s_len)
        outs = []
        for h in range(GQA):
            s = _dot_nt(qs[:, HEAD_DIM * h:HEAD_DIM * (h + 1)], k) + bias
            p = jnp.exp(s - jnp.max(s, axis=-1, keepdims=True))
            inv = 1.0 / jnp.sum(p, axis=-1, keepdims=True)
            outs.append(_dot(p.astype(BF16), v) * inv)
        o_ref[...] = jnp.concatenate(outs, axis=1)

    return pl.pallas_call(
        body, name="attn_fwd", grid=(nb, N_KV, nq),
        out_shape=jax.ShapeDtypeStruct((rows, D), F32),
        in_specs=[pl.BlockSpec((tq, GROUP_W), lambda b, g, i: (b * nq + i, g)),
                  pl.BlockSpec((1, 1, lk, HEAD_DIM), lambda b, g, i: (b, g, 0, 0)),
                  pl.BlockSpec((1, 1, lk, HEAD_DIM), lambda b, g, i: (b, g, 0, 0)),
                  pl.BlockSpec((1, GROUP_W), lambda b, g, i: (0, 0)),
                  pl.BlockSpec((tq, GROUP_W), lambda b, g, i: (i, 0)),
                  pl.BlockSpec((tq, GROUP_W), lambda b, g, i: (i, 0))],
        out_specs=pl.BlockSpec((tq, GROUP_W), lambda b, g, i: (b * nq + i, g)),
        compiler_params=_params(("parallel", "parallel", "parallel"), 48),
    )(pq, kr, vb, qg, cos, sin)


def _mid(x2, t2, c0, cz, o, e, w3, cn_g, cn_b, tm):
    rows = x2.shape[0]

    def body(x_ref, t_ref, c0_ref, cz_ref, o_ref, e_ref, w_ref, g_ref, b_ref,
             dy_o, mt_o, c3t_o, o2t_o, dyc_o, dya_o, do_o, dc0_o, dcz_o, de_o, sums_o):
        wco, wao, wo = w_ref[0], w_ref[1], w_ref[2]
        cn_g_v = g_ref[...]
        c0v = c0_ref[...]
        xc = c0v - jnp.mean(c0v, axis=-1, keepdims=True)
        rstd = lax.rsqrt(jnp.mean(xc * xc, axis=-1, keepdims=True) + EPS)
        n = xc * rstd
        c1 = n * cn_g_v + b_ref[...]
        s1 = _sig(c1)
        c2 = c1 * s1
        czv = cz_ref[...]
        sz = _sig(czv)
        gz = czv * sz
        c3 = c2 * gz
        yc = _dot(c3.astype(BF16), wco)
        az, gc, ga = e_ref[:, :D], e_ref[:, D:2 * D], e_ref[:, 2 * D:]
        saz = _sig(az)
        gaz = az * saz
        ov = o_ref[...]
        o2 = ov * gaz
        ya = _dot(o2.astype(BF16), wao)
        sc, sa = _sig(gc), _sig(ga)
        merged = sc * yc + sa * ya
        out = _dot(merged.astype(BF16), wo)
        err = x_ref[...] + out - t_ref[...]
        dy = err * (1.0 / D)
        dy_o[...] = dy
        dm = _dot_nt(dy.astype(BF16), wo)
        dyc = dm * sc
        dya = dm * sa
        dycb, dyab = dyc.astype(BF16), dya.astype(BF16)
        dyc_o[...] = dycb
        dya_o[...] = dyab
        de_o[:, D:2 * D] = (dyc * yc * (1.0 - sc)).astype(BF16)
        de_o[:, 2 * D:] = (dya * ya * (1.0 - sa)).astype(BF16)
        dc3 = _dot_nt(dycb, wco)
        do2 = _dot_nt(dyab, wao)
        do_o[...] = do2 * gaz
        de_o[:, :D] = (do2 * ov * _dsilu(az, saz)).astype(BF16)
        dcz_o[...] = (dc3 * c2 * _dsilu(czv, sz)).astype(BF16)
        dc1 = dc3 * gz * _dsilu(c1, s1)
        dn = dc1 * cn_g_v
        dc0 = rstd * (dn - jnp.mean(dn, axis=-1, keepdims=True) - n * jnp.mean(dn * n, axis=-1, keepdims=True))
        dc0_o[...] = dc0
        mt_o[...] = merged.T.astype(BF16)
        c3t_o[...] = c3.T.astype(BF16)
        o2t_o[...] = o2.T.astype(BF16)

        @pl.when(pl.program_id(0) == 0)
        def _():
            sums_o[...] = jnp.zeros_like(sums_o)

        sums_o[0:1, :] += jnp.sum(dc1 * n, axis=0, keepdims=True)
        sums_o[1:2, :] += jnp.sum(dc1, axis=0, keepdims=True)
        sums_o[2:3, :] += jnp.sum(dc0, axis=0, keepdims=True)
        sums_o[3:4, :] += jnp.sum(err * err, axis=0, keepdims=True)

    row = lambda wd: pl.BlockSpec((tm, wd), lambda i: (i, 0))
    col = pl.BlockSpec((D, tm), lambda i: (0, i))
    vec = pl.BlockSpec((1, D), lambda i: (0, 0))
    f32o = lambda wd: jax.ShapeDtypeStruct((rows, wd), F32)
    b16o = lambda wd: jax.ShapeDtypeStruct((rows, wd), BF16)
    tpo = jax.ShapeDtypeStruct((D, rows), BF16)
    return pl.pallas_call(
        body, name="mid", grid=(rows // tm,),
        out_shape=(f32o(D), tpo, tpo, tpo, b16o(D), b16o(D), f32o(D), f32o(D), b16o(D), b16o(3 * D),
                   jax.ShapeDtypeStruct((8, D), F32)),
        in_specs=[row(D), row(D), row(D), row(D), row(D), row(3 * D),
                  pl.BlockSpec((3, D, D), lambda i: (0, 0, 0)), vec, vec],
        out_specs=(row(D), col, col, col, row(D), row(D), row(D), row(D), row(D), row(3 * D),
                   pl.BlockSpec((8, D), lambda i: (0, 0))),
        compiler_params=_params(("arbitrary",), 56),
    )(x2, t2, c0, cz, o, e, w3, cn_g, cn_b)


def _attn_bwd(pq, kr, vb, d_o, o, qg, cos, sin, nb, tq):
    rows = pq.shape[0]
    s_len = rows // nb
    nq = s_len // tq
    lk = s_len + KEY_PAD
    scale = 1.0 / math.sqrt(HEAD_DIM)

    def body(q_ref, k_ref, v_ref, do_ref, o_ref, g_ref, cos_ref, sin_ref, dq_o, dkt_o, dvt_o, dg_o):
        b, g, i = pl.program_id(0), pl.program_id(1), pl.program_id(2)
        qv, gv, cosv, sinv = q_ref[...], g_ref[...], cos_ref[...], sin_ref[...]
        qr, r = _qk_fwd(qv, gv, cosv, sinv)
        qs = qr * scale
        qsb = qs.astype(BF16)
        dov, ov = do_ref[...], o_ref[...]
        dob = dov.astype(BF16)
        qst = qs.T.astype(BF16)
        dot_ = dov.T.astype(BF16)
        k, v = k_ref[0, 0], v_ref[0, 0]
        bias = _key_bias(s_len)
        dkt = jnp.zeros((HEAD_DIM, lk), F32)
        dvt = jnp.zeros((HEAD_DIM, lk), F32)
        dqs = []
        for h in range(GQA):
            hs = slice(HEAD_DIM * h, HEAD_DIM * (h + 1))
            s = _dot_nt(qsb[:, hs], k) + bias
            p = jnp.exp(s - jnp.max(s, axis=-1, keepdims=True))
            p = p * (1.0 / jnp.sum(p, axis=-1, keepdims=True))
            dp = _dot_nt(dob[:, hs], v)
            delta = jnp.sum(dov[:, hs] * ov[:, hs], axis=-1, keepdims=True)
            dsb = (p * (dp - delta)).astype(BF16)
            dqs.append(_dot(dsb, k))
            dkt = dkt + _dot(qst[hs, :], dsb)
            dvt = dvt + _dot(dot_[hs, :], p.astype(BF16))
        dqr = jnp.concatenate(dqs, axis=1) * scale
        dq, dgr = _qk_bwd(dqr, qv, r, gv, cosv, sinv)
        dq_o[...] = dq.astype(BF16)

        @pl.when(i == 0)
        def _():
            dkt_o[0, 0] = dkt
            dvt_o[0, 0] = dvt

        @pl.when(i > 0)
        def _():
            dkt_o[0, 0] += dkt
            dvt_o[0, 0] += dvt

        @pl.when((b == 0) & (g == 0) & (i == 0))
        def _():
            dg_o[...] = jnp.zeros_like(dg_o)

        dg_o[...] += jnp.sum(dgr, axis=0, keepdims=True)

    qspec = pl.BlockSpec((tq, GROUP_W), lambda b, g, i: (b * nq + i, g))
    kspec = pl.BlockSpec((1, 1, lk, HEAD_DIM), lambda b, g, i: (b, g, 0, 0))
    tspec = pl.BlockSpec((1, 1, HEAD_DIM, lk), lambda b, g, i: (b, g, 0, 0))
    rope = pl.BlockSpec((tq, GROUP_W), lambda b, g, i: (i, 0))
    vec = pl.BlockSpec((1, GROUP_W), lambda b, g, i: (0, 0))
    return pl.pallas_call(
        body, name="attn_bwd", grid=(nb, N_KV, nq),
        out_shape=(jax.ShapeDtypeStruct((rows, D), BF16), jax.ShapeDtypeStruct((nb, N_KV, HEAD_DIM, lk), F32),
                   jax.ShapeDtypeStruct((nb, N_KV, HEAD_DIM, lk), F32), jax.ShapeDtypeStruct((1, GROUP_W), F32)),
        in_specs=[qspec, kspec, kspec, qspec, qspec, vec, rope, rope],
        out_specs=(qspec, tspec, tspec, vec),
        compiler_params=_params(("arbitrary", "arbitrary", "arbitrary"), 56),
    )(pq, kr, vb, d_o, o, qg, cos, sin)


def _kv_bwd(dkt, dvt, pkv, pm_kv, kg, cos, sin, nb):
    rows = pkv.shape[0]
    s_len = rows // nb
    tk = 128
    nt = s_len // tk
    last = nt - 1

    def body(dk_ref, dv_ref, kv_ref, m_ref, g_ref, cos_ref, sin_ref, d_o, dm_o, dg_o):
        b, i = pl.program_id(0), pl.program_id(1)
        dkr = dk_ref[0].T
        dv = dv_ref[0].T
        gv = g_ref[...]

        @pl.when((b == 0) & (i == 0))
        def _():
            dg_o[...] = jnp.zeros_like(dg_o)

        @pl.when(i < nt)
        def _():
            kx = kv_ref[:, :GROUP_W]
            r = lax.rsqrt(_head_mean(kx * kx) + EPS)
            dk, dgr = _qk_bwd(dkr, kx, r, gv, cos_ref[...], sin_ref[...])
            d_o[:, :GROUP_W] = dk.astype(BF16)
            d_o[:, GROUP_W:] = dv.astype(BF16)
            dg_o[...] += jnp.sum(dgr, axis=0, keepdims=True)

        @pl.when(i == nt)
        def _():
            kx = m_ref[:, :GROUP_W]
            r = lax.rsqrt(_head_mean(kx * kx) + EPS)
            dn = dkr[0:N_META]
            dyg = dn * gv
            dm_o[0, :, :GROUP_W] = r * dyg - kx * (r * r * r) * _head_mean(dyg * kx)
            dm_o[0, :, GROUP_W:] = dv[0:N_META]
            dg_o[...] += jnp.sum(dn * kx * r, axis=0, keepdims=True)

    tspec = pl.BlockSpec((1, GROUP_W, tk), lambda b, i: (b, 0, i))
    rope = pl.BlockSpec((tk, GROUP_W), lambda b, i: (jnp.minimum(i, last), 0))
    return pl.pallas_call(
        body, name="kv_bwd", grid=(nb, nt + 1),
        out_shape=(jax.ShapeDtypeStruct((rows, 512), BF16), jax.ShapeDtypeStruct((nb, N_META, 512), F32),
                   jax.ShapeDtypeStruct((1, GROUP_W), F32)),
        in_specs=[tspec, tspec, pl.BlockSpec((tk, 512), lambda b, i: (b * nt + jnp.minimum(i, last), 0)),
                  pl.BlockSpec((N_META, 512), lambda b, i: (0, 0)), pl.BlockSpec((1, GROUP_W), lambda b, i: (0, 0)),
                  rope, rope],
        out_specs=(pl.BlockSpec((tk, 512), lambda b, i: (b * nt + jnp.minimum(i, last), 0)),
                   pl.BlockSpec((1, N_META, 512), lambda b, i: (b, 0, 0)),
                   pl.BlockSpec((1, GROUP_W), lambda b, i: (0, 0))),
        compiler_params=_params(("arbitrary", "arbitrary"), 32),
    )(dkt, dvt, pkv, pm_kv, kg, cos, sin)


def _conv_bwd(dc0, pconv, pm_conv, conv_w, nb, tm):
    rows = pconv.shape[0]
    nt = rows // nb // tm

    def body(dcur, dprev, dnxt, cur, prev, nxt, meta, w_ref, da_o, dam_o, gw_o, uext, dext):
        b, i = pl.program_id(0), pl.program_id(1)
        _fill_uext(uext, cur, prev, nxt, meta, i, nt, tm)
        dext[0:16] = jnp.zeros((16, D), F32)
        dext[16:32] = jnp.where(i == 0, 0.0, dprev[...])
        dext[32:32 + tm] = dcur[...]
        dext[32 + tm:48 + tm] = jnp.where(i == nt - 1, 0.0, dnxt[...])

        @pl.when((b == 0) & (i == 0))
        def _():
            gw_o[...] = jnp.zeros_like(gw_o)

        for c0 in range(0, D, 256):
            cs = slice(c0, c0 + 256)
            for r0 in range(0, tm, 32):
                acc = jnp.zeros((32, 256), F32)
                for j in range(CONV_K):
                    acc = acc + dext[r0 + 47 - j:r0 + 79 - j, cs] * w_ref[j:j + 1, cs]
                cv = cur[r0:r0 + 32, c0:c0 + 256]
                sg = _sig(cur[r0:r0 + 32, D + c0:D + c0 + 256])
                da_o[r0:r0 + 32, cs] = (acc * sg).astype(BF16)
                da_o[r0:r0 + 32, D + c0:D + c0 + 256] = (acc * cv * sg * (1.0 - sg)).astype(BF16)
            for j in range(CONV_K):
                acc = jnp.zeros((32, 256), F32)
                for r0 in range(0, tm, 32):
                    acc = acc + dext[32 + r0:64 + r0, cs] * uext[r0 + j + 1:r0 + j + 33, cs]
                gw_o[j:j + 1, cs] += jnp.sum(acc, axis=0, keepdims=True)

        @pl.when(i == 0)
        def _():
            for c0 in range(0, D, 256):
                cs = slice(c0, c0 + 256)
                acc = jnp.zeros((16, 256), F32)
                for j in range(CONV_K):
                    acc = acc + dext[31 - j:47 - j, cs] * w_ref[j:j + 1, cs]
                cv = meta[:, c0:c0 + 256]
                sg = _sig(meta[:, D + c0:D + c0 + 256])
                dam_o[0, :, cs] = acc * sg
                dam_o[0, :, D + c0:D + c0 + 256] = acc * cv * sg * (1.0 - sg)

    return pl.pallas_call(
        body, name="conv_bwd", grid=(nb, nt),
        out_shape=(jax.ShapeDtypeStruct((rows, 2048), BF16), jax.ShapeDtypeStruct((nb, N_META, 2048), F32),
                   jax.ShapeDtypeStruct((32, D), F32)),
        in_specs=_halo_specs(D, tm, nt, rows) + _halo_specs(2048, tm, nt, rows)
        + [pl.BlockSpec((16, 2048), lambda b, i: (0, 0)), pl.BlockSpec((32, D), lambda b, i: (0, 0))],
        out_specs=(pl.BlockSpec((tm, 2048), lambda b, i: (b * nt + i, 0)),
                   pl.BlockSpec((1, N_META, 2048), lambda b, i: (b, 0, 0)),
                   pl.BlockSpec((32, D), lambda b, i: (0, 0))),
        scratch_shapes=[pltpu.VMEM((tm + 32, D), F32), pltpu.VMEM((tm + 48, D), F32)],
        compiler_params=_params(("arbitrary", "arbitrary"), 40),
    )(dc0, dc0, dc0, pconv, pconv, pconv, pm_conv, conv_w)


def _meta_bwd(dam, ddm, w_full, meta_full, norm_g):
    nb = dam.shape[0]

    def body(a_ref, d_ref, wc_ref, wkv_ref, m_ref, g_ref, gm_o, dg_o):
        a, d = a_ref[0], d_ref[0]
        for b in range(1, nb):
            a = a + a_ref[b]
            d = d + d_ref[b]
        dxn = _dot_nt(a.astype(BF16), wc_ref[...]) + _dot_nt(d.astype(BF16), wkv_ref[...])
        v = m_ref[...]
        r = lax.rsqrt(jnp.mean(v * v, axis=-1, keepdims=True) + EPS)
        gm_o[...] = _rms_bwd(dxn, v, r, g_ref[...])
        dg_o[...] = jnp.sum(dxn * v * r, axis=0, keepdims=True)

    return pl.pallas_call(
        body, name="meta_bwd", grid=(1,),
        out_shape=(jax.ShapeDtypeStruct((N_META, D), F32), jax.ShapeDtypeStruct((1, D), F32)),
        in_specs=[pl.BlockSpec((nb, N_META, 2048), lambda i: (0, 0, 0)), pl.BlockSpec((nb, N_META, 512), lambda i: (0, 0, 0)),
                  pl.BlockSpec((D, 2048), lambda i: (0, 0)), pl.BlockSpec((D, 512), lambda i: (0, G_KV[0] // 512)),
                  pl.BlockSpec((N_META, D), lambda i: (0, 0)), pl.BlockSpec((1, D), lambda i: (0, 0))],
        out_specs=(pl.BlockSpec((N_META, D), lambda i: (0, 0)), pl.BlockSpec((1, D), lambda i: (0, 0))),
        compiler_params=_params(("arbitrary",), 32),
    )(dam, ddm, w_full, w_full, meta_full, norm_g)


def _dxn(d_groups, w_full, x2, dy, norm_g, dg_init, tm):
    rows = x2.shape[0]
    groups = (G_CONV, G_CZ, G_Q, G_KV, G_E)

    def body(da, db, dq, dd, de, w_hbm, x_ref, dy_ref, g_ref, gi_ref, gx_o, dg_o, w_vmem, sem):
        @pl.when(pl.program_id(0) == 0)
        def _():
            cp = pltpu.make_async_copy(w_hbm, w_vmem, sem)
            cp.start()
            cp.wait()
            dg_o[...] = gi_ref[...]

        dxn = jnp.zeros((tm, D), F32)
        for ref, (off, wd) in zip((da, db, dq, dd, de), groups):
            for c0 in range(0, wd, 512):
                dxn = dxn + _dot_nt(ref[:, c0:c0 + 512], w_vmem[:, off + c0:off + c0 + 512])
        v = x_ref[...]
        r = lax.rsqrt(jnp.mean(v * v, axis=-1, keepdims=True) + EPS)
        gx_o[...] = dy_ref[...] + _rms_bwd(dxn, v, r, g_ref[...])
        dg_o[...] += jnp.sum(dxn * v * r, axis=0, keepdims=True)

    row = lambda wd: pl.BlockSpec((tm, wd), lambda i: (i, 0))
    vec = pl.BlockSpec((1, D), lambda i: (0, 0))
    return pl.pallas_call(
        body, name="dxn", grid=(rows // tm,),
        out_shape=(jax.ShapeDtypeStruct((rows, D), F32), jax.ShapeDtypeStruct((1, D), F32)),
        in_specs=[row(wd) for _, wd in groups] + [pl.BlockSpec(memory_space=pl.ANY), row(D), row(D), vec, vec],
        out_specs=(row(D), vec),
        scratch_shapes=[pltpu.VMEM((D, IN_DIM), BF16), pltpu.SemaphoreType.DMA],
        compiler_params=_params(("arbitrary",), 56),
    )(*d_groups, w_full, x2, dy, norm_g, dg_init)


def _wgrad(at, b, buf, slot, col_off, name, meta=None):
    rows, n = b.shape
    tn, tk = 512, 512
    nk = rows // tk
    j0 = col_off // tn

    def body(*refs):
        if meta is None:
            at_ref, b_ref, _, o_ref = refs
        else:
            at_ref, b_ref, xm_ref, dm_ref, _, o_ref = refs
        k = pl.program_id(1)

        @pl.when(k == 0)
        def _():
            if meta is None:
                o_ref[0] = jnp.zeros((D, tn), F32)
            else:
                dm = dm_ref[0]
                for e in range(1, dm_ref.shape[0]):
                    dm = dm + dm_ref[e]
                dm = jnp.concatenate([dm, jnp.zeros((128 - N_META, tn), F32)], axis=0)
                o_ref[0] = _dot(xm_ref[...], dm.astype(BF16))

        o_ref[0] += _dot(at_ref[...], b_ref[...].astype(BF16))

    in_specs = [pl.BlockSpec((D, tk), lambda j, k: (0, k)), pl.BlockSpec((tk, tn), lambda j, k: (k, j))]
    args = [at, b]
    if meta is not None:
        xmt, dm = meta
        in_specs += [pl.BlockSpec((D, 128), lambda j, k: (0, 0)),
                     pl.BlockSpec((dm.shape[0], N_META, tn), lambda j, k: (0, 0, j))]
        args += [xmt, dm]
    in_specs.append(pl.BlockSpec(memory_space=pl.ANY))
    args.append(buf)
    return pl.pallas_call(
        body, name=name, grid=(n // tn, nk),
        out_shape=jax.ShapeDtypeStruct(buf.shape, F32),
        in_specs=in_specs,
        out_specs=pl.BlockSpec((1, D, tn), lambda j, k: (slot, 0, j0 + j)),
        input_output_aliases={len(args) - 1: 0},
        compiler_params=_params(("parallel", "arbitrary"), 32),
    )(*args)


def _rope_tables(s_len):
    pos = jnp.arange(s_len, dtype=jnp.int32)
    row_ids = (pos // GRID_W).astype(F32)
    col_ids = (pos % GRID_W).astype(F32)
    inv_freq = ROPE_THETA ** (-jnp.arange(ROPE_FREQS, dtype=F32) / ROPE_FREQS)
    a_row = row_ids[:, None] * inv_freq[None, :]
    a_col = col_ids[:, None] * inv_freq[None, :]
    ang = jnp.concatenate([a_row, a_row, a_col, a_col], axis=-1)
    return jnp.tile(jnp.cos(ang), (1, GQA)), jnp.tile(jnp.sin(ang), (1, GQA))


def _local_step(x, loss_target, norm_g, conv_b, cn_g, cn_b, q_g, k_g, w_full, w3_full, conv_w_full, meta_full):
    nb, s_len, _ = x.shape
    rows = nb * s_len
    x2 = x.reshape(rows, D)
    t2 = loss_target.reshape(rows, D)
    cos, sin = _rope_tables(s_len)
    qg = jnp.tile(q_g, (1, GQA))
    kg = jnp.tile(k_g, (1, N_KV))

    xnmt, pm_conv, pm_kv = _meta_fwd(meta_full, norm_g, w_full)
    pconv, pcz, pq, pkv, pe, xnt = _in_proj(x2, norm_g, w_full, 256)
    c0 = _conv_fwd(pconv, pm_conv, conv_w_full, conv_b, nb, 256)
    kr, vb = _kv_prep(pkv, pm_kv, kg, cos, sin, nb)
    o = _attn_fwd(pq, kr, vb, qg, cos, sin, nb, 256)
    dy, mt, c3t, o2t, dyc, dya, d_o, dc0, dcz, de, sums = _mid(x2, t2, c0, pcz, o, pe, w3_full, cn_g, cn_b, 128)
    dq, dkt, dvt, dqg = _attn_bwd(pq, kr, vb, d_o, o, qg, cos, sin, nb, 256)
    lk = s_len + KEY_PAD
    dd, ddm, dkg = _kv_bwd(dkt.reshape(nb, GROUP_W, lk), dvt.reshape(nb, GROUP_W, lk), pkv, pm_kv, kg, cos, sin, nb)
    da, dam, gcw = _conv_bwd(dc0, pconv, pm_conv, conv_w_full, nb, 256)
    gmeta, dng_m = _meta_bwd(dam, ddm, w_full, meta_full, norm_g)
    gx, dng = _dxn((da, dcz, dq, dd, de), w_full, x2, dy, norm_g, dng_m, 256)

    gw3 = lax.empty((3, D, D), F32)
    gw3 = _wgrad(c3t, dyc, gw3, 0, 0, "wgrad_conv_out")
    gw3 = _wgrad(o2t, dya, gw3, 1, 0, "wgrad_attn_out")
    gw3 = _wgrad(mt, dy, gw3, 2, 0, "wgrad_out")
    gwin = lax.empty((1, D, IN_DIM), F32)
    gwin = _wgrad(xnt, da, gwin, 0, G_CONV[0], "wgrad_in_conv", meta=(xnmt, dam))
    gwin = _wgrad(xnt, dcz, gwin, 0, G_CZ[0], "wgrad_in_cz")
    gwin = _wgrad(xnt, dq, gwin, 0, G_Q[0], "wgrad_in_q")
    gwin = _wgrad(xnt, dd, gwin, 0, G_KV[0], "wgrad_in_kv", meta=(xnmt, ddm))
    gwin = _wgrad(xnt, de, gwin, 0, G_E[0], "wgrad_in_e")

    zeros = jnp.zeros((1, D - 2 * GROUP_W), F32)
    smalls = jnp.concatenate([dng, sums[2:3], sums[0:1], sums[1:2], jnp.concatenate([dqg, dkg, zeros], axis=1),
                              sums[3:4], jnp.zeros((2, D), F32)], axis=0)
    return gx.reshape(nb, s_len, D), gwin, gw3, gcw, gmeta, smalls


def _xyc():
    return lax.axis_index("x"), lax.axis_index("y"), lax.axis_index("c")


def _reduce_sibling(gwin, gw3v, gcm, smalls):
    def body(gwin_ref, gw3_ref, gcm_ref, sm_ref, r_win, r_w3, r_cm, r_sm, send, recv, ssend, srecv, lsem):
        x, y, c = _xyc()
        o = 1 - c
        sib = (x, y, o)
        half = D // 2
        outs = ((gwin_ref.at[pl.ds(o * half, half), :], r_win), (gw3_ref.at[:, :, o], r_w3), (gcm_ref.at[o], r_cm))
        cps = []
        for a, (src, dst) in enumerate(outs):
            cp = pltpu.make_async_remote_copy(src_ref=src, dst_ref=dst, send_sem=send.at[a], recv_sem=recv.at[a],
                                              device_id=sib, device_id_type=MESH)
            cp.start()
            cps.append(cp)
        me = 4 * x + 2 * y + c
        loc = pltpu.make_async_copy(sm_ref, r_sm.at[me], lsem)
        loc.start()
        scps = []
        for d in range(1, 8):
            px, py, pc = (x + (d >> 2)) % 2, (y + ((d >> 1) & 1)) % 2, (c + (d & 1)) % 2
            cp = pltpu.make_async_remote_copy(src_ref=sm_ref, dst_ref=r_sm.at[me], send_sem=ssend.at[d - 1],
                                              recv_sem=srecv.at[d - 1], device_id=(px, py, pc), device_id_type=MESH)
            cp.start()
            scps.append((cp, 4 * px + 2 * py + pc))
        for cp in cps:
            cp.wait()
        for d, (cp, pid) in enumerate(scps):
            pltpu.make_async_remote_copy(src_ref=sm_ref, dst_ref=r_sm.at[pid], send_sem=ssend.at[d],
                                         recv_sem=srecv.at[d], device_id=(x, y, c), device_id_type=MESH).wait_recv()
            cp.wait_send()
        loc.wait()

    any_spec = pl.BlockSpec(memory_space=pl.ANY)
    return pl.pallas_call(
        body, name="reduce_sibling",
        out_shape=(jax.ShapeDtypeStruct((D // 2, IN_DIM), F32), jax.ShapeDtypeStruct((3, 4, 128, D), F32),
                   jax.ShapeDtypeStruct((24, D), F32), jax.ShapeDtypeStruct((8, 8, D), F32)),
        in_specs=[any_spec] * 4, out_specs=(any_spec,) * 4,
        scratch_shapes=[pltpu.SemaphoreType.DMA((3,)), pltpu.SemaphoreType.DMA((3,)),
                        pltpu.SemaphoreType.DMA((7,)), pltpu.SemaphoreType.DMA((7,)), pltpu.SemaphoreType.DMA],
    )(gwin, gw3v, gcm, smalls)


def _add_sibling(gwin, gw3v, gcm, r_win, r_w3, r_cm):
    c = lax.axis_index("c").astype(jnp.int32).reshape(1)
    half = D // 2
    tr = 64

    def body1(c_ref, a_ref, b_ref, o_ref):
        o_ref[...] = (a_ref[...] + b_ref[...]).astype(BF16)

    cs_win = pl.pallas_call(
        body1, name="add_sibling_w_in", out_shape=jax.ShapeDtypeStruct((half, IN_DIM), BF16),
        grid_spec=pltpu.PrefetchScalarGridSpec(
            num_scalar_prefetch=1, grid=(half // tr,),
            in_specs=[pl.BlockSpec((tr, IN_DIM), lambda i, c_ref: (c_ref[0] * (half // tr) + i, 0)),
                      pl.BlockSpec((tr, IN_DIM), lambda i, c_ref: (i, 0))],
            out_specs=pl.BlockSpec((tr, IN_DIM), lambda i, c_ref: (i, 0))),
        compiler_params=_params(("parallel",), 32),
    )(c, gwin, r_win)

    def body2(c_ref, a_ref, b_ref, o_ref):
        o_ref[0, 0] = (a_ref[0, 0, 0] + b_ref[0, 0]).astype(BF16)

    cs_w3 = pl.pallas_call(
        body2, name="add_sibling_w3", out_shape=jax.ShapeDtypeStruct((3, 4, 128, D), BF16),
        grid_spec=pltpu.PrefetchScalarGridSpec(
            num_scalar_prefetch=1, grid=(3, 4),
            in_specs=[pl.BlockSpec((1, 1, 1, 128, D), lambda w, s, c_ref: (w, s, c_ref[0], 0, 0)),
                      pl.BlockSpec((1, 1, 128, D), lambda w, s, c_ref: (w, s, 0, 0))],
            out_specs=pl.BlockSpec((1, 1, 128, D), lambda w, s, c_ref: (w, s, 0, 0))),
        compiler_params=_params(("parallel", "parallel"), 32),
    )(c, gw3v, r_w3)

    def body3(c_ref, a_ref, b_ref, o_ref):
        o_ref[...] = a_ref[0] + b_ref[...]

    cs_cm = pl.pallas_call(
        body3, name="add_sibling_cm", out_shape=jax.ShapeDtypeStruct((24, D), F32),
        grid_spec=pltpu.PrefetchScalarGridSpec(
            num_scalar_prefetch=1, grid=(1,),
            in_specs=[pl.BlockSpec((1, 24, D), lambda i, c_ref: (c_ref[0], 0, 0)),
                      pl.BlockSpec((24, D), lambda i, c_ref: (0, 0))],
            out_specs=pl.BlockSpec((24, D), lambda i, c_ref: (0, 0))),
        compiler_params=_params(("arbitrary",), 32),
    )(c, gcm, r_cm)
    return cs_win, cs_w3, cs_cm


def _reduce_chips(cs_win, cs_w3, cs_cm):
    def body(win_ref, w3_ref, cm_ref, r_win, r_w3, r_cm, send, recv):
        x, y, c = _xyc()
        peers = ((1 - x, y), (x, 1 - y), (1 - x, 1 - y))
        cps = []
        for k, (px, py) in enumerate(peers):
            ps = 2 * px + py
            items = ((win_ref.at[:, pl.ds(ps * W_IN_SHARD, W_IN_SHARD)], r_win.at[k]),
                     (w3_ref.at[:, ps], r_w3.at[k]),
                     (cm_ref.at[:, pl.ds(ps * ROW_SHARD, ROW_SHARD)], r_cm.at[k]))
            for a, (src, dst) in enumerate(items):
                cp = pltpu.make_async_remote_copy(src_ref=src, dst_ref=dst, send_sem=send.at[a, k],
                                                  recv_sem=recv.at[a, k], device_id=(px, py, c), device_id_type=MESH)
                cp.start()
                cps.append(cp)
        for cp in cps:
            cp.wait()

    any_spec = pl.BlockSpec(memory_space=pl.ANY)
    return pl.pallas_call(
        body, name="reduce_chips",
        out_shape=(jax.ShapeDtypeStruct((3, D // 2, W_IN_SHARD), BF16), jax.ShapeDtypeStruct((3, 3, 128, D), BF16),
                   jax.ShapeDtypeStruct((3, 24, ROW_SHARD), F32)),
        in_specs=[any_spec] * 3, out_specs=(any_spec,) * 3,
        scratch_shapes=[pltpu.SemaphoreType.DMA((3, 3)), pltpu.SemaphoreType.DMA((3, 3))],
    )(cs_win, cs_w3, cs_cm)


def _add_chips(cs_win, cs_w3, cs_cm, r_win, r_w3, r_cm):
    x, y, c = _xyc()
    idx = jnp.stack([2 * x + y, c]).astype(jnp.int32)
    half = D // 2
    tr = 128

    def body1(i_ref, a_ref, b_ref, o_ref):
        f = lambda v: v.astype(F32)
        o_ref[0] = (f(a_ref[...]) + f(b_ref[2])) + (f(b_ref[0]) + f(b_ref[1]))

    f_win = pl.pallas_call(
        body1, name="add_chips_w_in", out_shape=jax.ShapeDtypeStruct((2, half, W_IN_SHARD), F32),
        grid_spec=pltpu.PrefetchScalarGridSpec(
            num_scalar_prefetch=1, grid=(half // tr,),
            in_specs=[pl.BlockSpec((tr, W_IN_SHARD), lambda i, r: (i, r[0])),
                      pl.BlockSpec((3, tr, W_IN_SHARD), lambda i, r: (0, i, 0))],
            out_specs=pl.BlockSpec((1, tr, W_IN_SHARD), lambda i, r: (r[1], i, 0))),
        compiler_params=_params(("parallel",), 32),
    )(idx, cs_win, r_win)

    def body2(i_ref, a_ref, b_ref, o_ref):
        f = lambda v: v.astype(F32)
        o_ref[0, 0] = (f(a_ref[0, 0]) + f(b_ref[2, 0])) + (f(b_ref[0, 0]) + f(b_ref[1, 0]))

    f_w3 = pl.pallas_call(
        body2, name="add_chips_w3", out_shape=jax.ShapeDtypeStruct((3, 2, 128, D), F32),
        grid_spec=pltpu.PrefetchScalarGridSpec(
            num_scalar_prefetch=1, grid=(3,),
            in_specs=[pl.BlockSpec((1, 1, 128, D), lambda w, r: (w, r[0], 0, 0)),
                      pl.BlockSpec((3, 1, 128, D), lambda w, r: (0, w, 0, 0))],
            out_specs=pl.BlockSpec((1, 1, 128, D), lambda w, r: (w, r[1], 0, 0))),
        compiler_params=_params(("parallel",), 32),
    )(idx, cs_w3, r_w3)

    def body3(i_ref, a_ref, b_ref, o_ref):
        o_ref[0] = (a_ref[...] + b_ref[2]) + (b_ref[0] + b_ref[1])

    f_cm = pl.pallas_call(
        body3, name="add_chips_cm", out_shape=jax.ShapeDtypeStruct((2, 24, ROW_SHARD), F32),
        grid_spec=pltpu.PrefetchScalarGridSpec(
            num_scalar_prefetch=1, grid=(1,),
            in_specs=[pl.BlockSpec((24, ROW_SHARD), lambda i, r: (0, r[0])),
                      pl.BlockSpec((3, 24, ROW_SHARD), lambda i, r: (0, 0, 0))],
            out_specs=pl.BlockSpec((1, 24, ROW_SHARD), lambda i, r: (r[1], 0, 0))),
        compiler_params=_params(("arbitrary",), 32),
    )(idx, cs_cm, r_cm)
    return f_win, f_w3, f_cm


def _share_sibling(f_win, f_w3, f_cm):
    def body(win_in, w3_in, cm_in, win_ref, w3_ref, cm_ref, send, recv):
        x, y, c = _xyc()
        o = 1 - c
        cps = []
        for a, (ref, sl) in enumerate(((win_ref, lambda h: win_ref.at[h]), (w3_ref, lambda h: w3_ref.at[:, h]),
                                       (cm_ref, lambda h: cm_ref.at[h]))):
            cp = pltpu.make_async_remote_copy(src_ref=sl(c), dst_ref=sl(c), send_sem=send.at[a], recv_sem=recv.at[a],
                                              device_id=(x, y, o), device_id_type=MESH)
            cp.start()
            cps.append((cp, sl))
        for a, (cp, sl) in enumerate(cps):
            pltpu.make_async_remote_copy(src_ref=sl(o), dst_ref=sl(o), send_sem=send.at[a], recv_sem=recv.at[a],
                                         device_id=(x, y, o), device_id_type=MESH).wait_recv()
            cp.wait_send()

    any_spec = pl.BlockSpec(memory_space=pl.ANY)
    return pl.pallas_call(
        body, name="share_sibling",
        out_shape=(jax.ShapeDtypeStruct(f_win.shape, F32), jax.ShapeDtypeStruct(f_w3.shape, F32),
                   jax.ShapeDtypeStruct(f_cm.shape, F32)),
        in_specs=[any_spec] * 3, out_specs=(any_spec,) * 3,
        input_output_aliases={0: 0, 1: 1, 2: 2},
        scratch_shapes=[pltpu.SemaphoreType.DMA((3,)), pltpu.SemaphoreType.DMA((3,))],
    )(f_win, f_w3, f_cm)


def _adamw_math(w, g, m, v):
    m = ADAM_B1 * m + (1.0 - ADAM_B1) * g
    v = ADAM_B2 * v + (1.0 - ADAM_B2) * (g * g)
    m_hat = m / (1.0 - ADAM_B1 ** ADAM_STEP)
    v_hat = v / (1.0 - ADAM_B2 ** ADAM_STEP)
    delta = -ADAM_LR * (m_hat / (jnp.sqrt(v_hat) + ADAM_EPS) + ADAM_WD * w)
    return delta, m, v


def _adamw(w, g, m, v, tr, name):
    rows, cols = w.shape

    def body(w_ref, g_ref, m_ref, v_ref, d_o, m_o, v_o):
        d_o[...], m_o[...], v_o[...] = _adamw_math(w_ref[...], g_ref[...], m_ref[...], v_ref[...])

    spec = pl.BlockSpec((tr, cols), lambda i: (i, 0))
    return pl.pallas_call(
        body, name=name, grid=(rows // tr,),
        out_shape=(jax.ShapeDtypeStruct((rows, cols), F32),) * 3,
        in_specs=[spec] * 4, out_specs=(spec,) * 3,
        compiler_params=_params(("parallel",), 32),
    )(w, g, m, v)


def _adamw_small(r_sm, ws, ms, vs):
    def body(s_ref, *refs):
        w_refs, m_refs, v_refs, outs = refs[0:6], refs[6:12], refs[12:18], refs[18:]
        loss_o, g_os, d_os, m_os, v_os = outs[0], outs[1:7], outs[7:13], outs[13:19], outs[19:25]
        g = s_ref[0]
        for dev in range(1, 8):
            g = g + s_ref[dev]
        qk = g[4:5, :]
        qg = qk[:, 0:HEAD_DIM]
        kg = qk[:, GROUP_W:GROUP_W + HEAD_DIM]
        for h in range(1, GQA):
            qg = qg + qk[:, HEAD_DIM * h:HEAD_DIM * (h + 1)]
            kg = kg + qk[:, GROUP_W + HEAD_DIM * h:GROUP_W + HEAD_DIM * (h + 1)]
        loss_o[...] = (0.5 / D) * jnp.sum(g[5:6, :], axis=-1, keepdims=True)
        for i, gi in enumerate((g[0:1], g[1:2], g[2:3], g[3:4], qg, kg)):
            g_os[i][...] = gi
            d_os[i][...], m_os[i][...], v_os[i][...] = _adamw_math(w_refs[i][...], gi, m_refs[i][...], v_refs[i][...])

    six = tuple(jax.ShapeDtypeStruct(w.shape, F32) for w in ws)
    return pl.pallas_call(
        body, name="adamw_small", out_shape=(jax.ShapeDtypeStruct((1, 1), F32),) + six * 4,
    )(r_sm, *ws, *ms, *vs)


def kernel(x, meta_tokens, norm_g, w_in, conv_w, conv_b, conv_norm_g, conv_norm_b, w_conv_out, q_norm_g, k_norm_g, w_attn_out, w_out, loss_target, m_meta_tokens, m_norm_g, m_w_in, m_conv_w, m_conv_b, m_conv_norm_g, m_conv_norm_b, m_w_conv_out, m_q_norm_g, m_k_norm_g, m_w_attn_out, m_w_out, v_meta_tokens, v_norm_g, v_w_in, v_conv_w, v_conv_b, v_conv_norm_g, v_conv_norm_b, v_w_conv_out, v_q_norm_g, v_k_norm_g, v_w_attn_out, v_w_out):
    pad_k = lambda a: jnp.pad(a[0], ((0, 32 - CONV_K), (0, 0)))
    w3_s = jnp.concatenate([w_conv_out, w_attn_out, w_out], axis=0)
    w_full, w3_full, conv_w_full, meta_full = _gather_weights(w_in[0], w3_s, pad_k(conv_w), meta_tokens)

    gx, gwin, gw3, gcw, gmeta, smalls = _local_step(
        x, loss_target, norm_g, conv_b, conv_norm_g, conv_norm_b, q_norm_g, k_norm_g,
        w_full, w3_full, conv_w_full, meta_full)

    gwin2 = gwin.reshape(D, IN_DIM)
    gw3v = gw3.reshape(3, N_CHIPS, 2, 128, D)
    gcm = jnp.concatenate([gcw.reshape(2, 16, D), gmeta.reshape(2, 8, D)], axis=1)
    r_win, r_w3, r_cm, r_sm = _reduce_sibling(gwin2, gw3v, gcm, smalls)
    cs_win, cs_w3, cs_cm = _add_sibling(gwin2, gw3v, gcm, r_win, r_w3, r_cm)
    r2_win, r2_w3, r2_cm = _reduce_chips(cs_win, cs_w3, cs_cm)
    f_win, f_w3, f_cm = _add_chips(cs_win, cs_w3, cs_cm, r2_win, r2_w3, r2_cm)
    f_win, f_w3, f_cm = _share_sibling(f_win, f_w3, f_cm)

    g_w_in = f_win.reshape(D, W_IN_SHARD)
    g_w3 = f_w3.reshape(3, ROW_SHARD, D)
    g_conv_w = f_cm[:, 0:16].reshape(32, ROW_SHARD)
    g_meta = f_cm[:, 16:24].reshape(N_META, ROW_SHARD)

    d_w_in, nm_w_in, nv_w_in = _adamw(w_in[0], g_w_in, m_w_in[0], v_w_in[0], 128, "adamw_w_in")
    m3 = jnp.concatenate([m_w_conv_out, m_w_attn_out, m_w_out], axis=0).reshape(3 * ROW_SHARD, D)
    v3 = jnp.concatenate([v_w_conv_out, v_w_attn_out, v_w_out], axis=0).reshape(3 * ROW_SHARD, D)
    d_w3, nm_w3, nv_w3 = _adamw(w3_s.reshape(3 * ROW_SHARD, D), g_w3.reshape(3 * ROW_SHARD, D), m3, v3, 256, "adamw_w3")
    cm_w = jnp.concatenate([pad_k(conv_w), meta_tokens], axis=0)
    cm_m = jnp.concatenate([pad_k(m_conv_w), m_meta_tokens], axis=0)
    cm_v = jnp.concatenate([pad_k(v_conv_w), v_meta_tokens], axis=0)
    cm_g = jnp.concatenate([g_conv_w, g_meta], axis=0)
    d_cm, nm_cm, nv_cm = _adamw(cm_w, cm_g, cm_m, cm_v, 48, "adamw_cm")
    small = _adamw_small(
        r_sm, (norm_g, conv_b, conv_norm_g, conv_norm_b, q_norm_g, k_norm_g),
        (m_norm_g, m_conv_b, m_conv_norm_g, m_conv_norm_b, m_q_norm_g, m_k_norm_g),
        (v_norm_g, v_conv_b, v_conv_norm_g, v_conv_norm_b, v_q_norm_g, v_k_norm_g))

    def assemble(big_in, w3x, cmx, s6):
        w3x = w3x.reshape(3, 1, ROW_SHARD, D)
        ng, cb, cng, cnb, qg, kg = s6
        return (cmx[32:48], ng, big_in[None], cmx[None, 0:CONV_K], cb, cng, cnb, w3x[0], qg, kg, w3x[1], w3x[2])

    loss = small[0].reshape(())
    grads = assemble(g_w_in, g_w3, cm_g, small[1:7])
    deltas = assemble(d_w_in, d_w3, d_cm, small[7:13])
    new_m = assemble(nm_w_in, nm_w3, nm_cm, small[13:19])
    new_v = assemble(nv_w_in, nv_w3, nv_cm, small[19:25])
    return (loss, gx, *grads, *deltas, *new_m, *new_v)
```

```python
import functools
import math

import jax
import jax.numpy as jnp
from jax import lax
from jax.experimental import pallas as pl
from jax.experimental.pallas import tpu as pltpu

F32, BF16 = jnp.float32, jnp.bfloat16
MESH = pl.DeviceIdType.MESH

D = 1024
N_META = 16
CONV_K = 31
N_KV = 4
GQA = 4
HEAD_DIM = 64
GROUP_W = GQA * HEAD_DIM
GRID_W = 64
ROPE_FREQS = 16
ROPE_THETA = 10000.0
EPS = 1e-6
IN_DIM = 7680
KEY_PAD = 128
G_CONV, G_CZ, G_Q, G_KV, G_E = (0, 2048), (2048, 1024), (3072, 1024), (4096, 512), (4608, 3072)
N_CHIPS = 4
W_IN_SHARD = IN_DIM // N_CHIPS
ROW_SHARD = D // N_CHIPS

ADAM_LR, ADAM_B1, ADAM_B2, ADAM_EPS, ADAM_WD, ADAM_STEP = 0.001, 0.9, 0.999, 1e-08, 0.01, 10

NT_DIMS = (((1,), (1,)), ((), ()))


def _params(sem=None, vmem_mb=48):
    return pltpu.CompilerParams(dimension_semantics=sem, vmem_limit_bytes=vmem_mb << 20)


def _sig(v):
    return jax.nn.sigmoid(v)


def _dsilu(v, s):
    return s * (1.0 + v * (1.0 - s))


def _dot(a, b):
    return jnp.dot(a, b, preferred_element_type=F32)


def _dot_nt(a, b):
    return lax.dot_general(a, b, NT_DIMS, preferred_element_type=F32)


def _qk_mats():
    i = lax.broadcasted_iota(jnp.int32, (GROUP_W, GROUP_W), 0)
    j = lax.broadcasted_iota(jnp.int32, (GROUP_W, GROUP_W), 1)
    mean = jnp.where((i >> 6) == (j >> 6), 1.0 / HEAD_DIM, 0.0).astype(BF16)
    turn = jnp.where((i == j + 16) & ((j & 16) == 0), -1.0,
                     jnp.where((i == j - 16) & ((j & 16) != 0), 1.0, 0.0)).astype(BF16)
    return mean, turn


def _apply(v, mat):
    hi = v.astype(BF16)
    lo = (v - hi.astype(F32)).astype(BF16)
    return _dot(hi, mat) + _dot(lo, mat)


def _qk_fwd(v, g, cos, sin, mats):
    mean, turn = mats
    r = lax.rsqrt(_apply(v * v, mean) + EPS)
    n = v * r * g
    return n * cos + _apply(n, turn) * sin, r


def _qk_bwd(dy, v, r, g, cos, sin, mats):
    mean, turn = mats
    dn = dy * cos - _apply(dy, turn) * sin
    dyg = dn * g
    dv = r * dyg - v * (r * r * r) * _apply(dyg * v, mean)
    return dv, dn * v * r


def _rms_bwd(dxn, v, r, g):
    dxg = dxn * g
    return r * dxg - v * (r * r * r) * jnp.mean(dxg * v, axis=-1, keepdims=True)


def _glu(a):
    return a[:, :D] * _sig(a[:, D:])


def _gather_weights(w_in_s, w3_s, conv_w_s, meta_s):
    def body(win_ref, w3_ref, cw_ref, mt_ref, win_o, w3_o, cw_o, mt_o, win_b, w3_b, send, recv, fsend, frecv, lsem):
        x, y, c = _xyc()
        o = 1 - c
        me = 2 * x + y
        win_b[...] = win_ref[...].astype(BF16)
        w3_b[...] = w3_ref[...].astype(BF16)
        items = (
            (lambda h: win_b.at[pl.ds(h * 512, 512), :],
             lambda p, h: win_o.at[pl.ds(h * 512, 512), pl.ds(p * W_IN_SHARD, W_IN_SHARD)]),
            (lambda h: w3_b.at[:, pl.ds(h * 128, 128), :],
             lambda p, h: w3_o.at[:, pl.ds(p * ROW_SHARD + h * 128, 128), :]),
            (lambda h: cw_ref.at[pl.ds(h * 16, 16), :],
             lambda p, h: cw_o.at[pl.ds(h * 16, 16), pl.ds(p * ROW_SHARD, ROW_SHARD)]),
            (lambda h: mt_ref.at[pl.ds(h * 8, 8), :],
             lambda p, h: mt_o.at[pl.ds(h * 8, 8), pl.ds(p * ROW_SHARD, ROW_SHARD)]),
        )
        peers = ((1 - x, y), (x, 1 - y), (1 - x, 1 - y))

        def remote(src, dst, s_sem, r_sem, to):
            return pltpu.make_async_remote_copy(src_ref=src, dst_ref=dst, send_sem=s_sem, recv_sem=r_sem,
                                                device_id=to, device_id_type=MESH)

        started = []
        for a, (half, place) in enumerate(items):
            for h in range(2):
                loc = pltpu.make_async_copy(half(h), place(me, h), lsem.at[a, h])
                loc.start()
                started.append(loc.wait)
            for k, (px, py) in enumerate(peers):
                cp = remote(half(c), place(me, c), send.at[a, k], recv.at[a, k], (px, py, c))
                cp.start()
                started.append(cp.wait_send)
        for k, (px, py) in enumerate(peers):
            for a, (half, place) in enumerate(items):
                got = place(2 * px + py, c)
                remote(got, got, send.at[a, k], recv.at[a, k], (px, py, c)).wait_recv()
                fw = remote(got, got, fsend.at[a, k], frecv.at[a, k], (x, y, o))
                fw.start()
                started.append(fw.wait_send)
        for k, (px, py) in enumerate(peers):
            for a, (half, place) in enumerate(items):
                theirs = place(2 * px + py, o)
                remote(theirs, theirs, fsend.at[a, k], frecv.at[a, k], (x, y, o)).wait_recv()
        for wait in started:
            wait()

    any_spec = pl.BlockSpec(memory_space=pl.ANY)
    vmem = pl.BlockSpec(memory_space=pltpu.VMEM)
    return pl.pallas_call(
        body, name="gather_weights",
        out_shape=(jax.ShapeDtypeStruct((D, IN_DIM), BF16), jax.ShapeDtypeStruct((3, D, D), BF16),
                   jax.ShapeDtypeStruct((32, D), F32), jax.ShapeDtypeStruct((N_META, D), F32)),
        in_specs=[vmem, vmem, vmem, vmem],
        out_specs=(any_spec, any_spec, any_spec, any_spec),
        scratch_shapes=[pltpu.VMEM((D, W_IN_SHARD), BF16), pltpu.VMEM((3, ROW_SHARD, D), BF16),
                        pltpu.SemaphoreType.DMA((4, 3)), pltpu.SemaphoreType.DMA((4, 3)),
                        pltpu.SemaphoreType.DMA((4, 3)), pltpu.SemaphoreType.DMA((4, 3)),
                        pltpu.SemaphoreType.DMA((4, 2))],
        compiler_params=pltpu.CompilerParams(vmem_limit_bytes=40 << 20),
    )(w_in_s, w3_s, conv_w_s, meta_s)


def _meta_fwd(meta_full, norm_g, w_full):
    def body(m_ref, g_ref, wc_ref, wkv_ref, xnt_ref, pc_ref, pkv_ref):
        v = m_ref[...]
        r = lax.rsqrt(jnp.mean(v * v, axis=-1, keepdims=True) + EPS)
        xn = v * r * g_ref[...]
        xnb = xn.astype(BF16)
        pad = jnp.concatenate([xn, jnp.zeros((128 - N_META, D), F32)], axis=0)
        xnt_ref[...] = pad.T.astype(BF16)
        pc_ref[...] = _dot(xnb, wc_ref[...])
        pkv_ref[...] = _dot(xnb, wkv_ref[...])

    return pl.pallas_call(
        body, name="meta_fwd", grid=(1,),
        out_shape=(jax.ShapeDtypeStruct((D, 128), BF16), jax.ShapeDtypeStruct((N_META, 2048), F32),
                   jax.ShapeDtypeStruct((N_META, 512), F32)),
        in_specs=[pl.BlockSpec((N_META, D), lambda i: (0, 0)), pl.BlockSpec((1, D), lambda i: (0, 0)),
                  pl.BlockSpec((D, 2048), lambda i: (0, 0)), pl.BlockSpec((D, 512), lambda i: (0, G_KV[0] // 512))],
        out_specs=(pl.BlockSpec((D, 128), lambda i: (0, 0)), pl.BlockSpec((N_META, 2048), lambda i: (0, 0)),
                   pl.BlockSpec((N_META, 512), lambda i: (0, 0))),
        compiler_params=_params(("arbitrary",), 32),
    )(meta_full, norm_g, w_full, w_full)


def _in_proj(x2, norm_g, w_full, tm):
    rows = x2.shape[0]
    groups = (G_CONV, G_CZ, G_Q, G_KV, G_E)

    def body(x_ref, g_ref, w_hbm, *rest):
        outs, xnt_ref, w_vmem, sem = rest[:5], rest[5], rest[6], rest[7]

        @pl.when(pl.program_id(0) == 0)
        def _():
            cp = pltpu.make_async_copy(w_hbm, w_vmem, sem)
            cp.start()
            cp.wait()

        v = x_ref[...]
        r = lax.rsqrt(jnp.mean(v * v, axis=-1, keepdims=True) + EPS)
        xn = v * r * g_ref[...]
        xnb = xn.astype(BF16)
        xnt_ref[...] = xn.T.astype(BF16)
        for ref, (off, wd) in zip(outs, groups):
            for c0 in range(0, wd, 512):
                ref[:, c0:c0 + 512] = _dot(xnb, w_vmem[:, off + c0:off + c0 + 512])

    return pl.pallas_call(
        body, name="in_proj", grid=(rows // tm,),
        out_shape=tuple(jax.ShapeDtypeStruct((rows, wd), F32) for _, wd in groups)
        + (jax.ShapeDtypeStruct((D, rows), BF16),),
        in_specs=[pl.BlockSpec((tm, D), lambda i: (i, 0)), pl.BlockSpec((1, D), lambda i: (0, 0)),
                  pl.BlockSpec(memory_space=pl.ANY)],
        out_specs=tuple(pl.BlockSpec((tm, wd), lambda i: (i, 0)) for _, wd in groups)
        + (pl.BlockSpec((D, tm), lambda i: (0, i)),),
        scratch_shapes=[pltpu.VMEM((D, IN_DIM), BF16), pltpu.SemaphoreType.DMA],
        compiler_params=_params(("arbitrary",), 56),
    )(x2, norm_g, w_full)


def _halo_specs(width, tm, nt, rows):
    h16 = tm // 16
    return [pl.BlockSpec((tm, width), lambda b, i: (b * nt + i, 0)),
            pl.BlockSpec((16, width), lambda b, i: (jnp.maximum((b * nt + i) * h16 - 1, 0), 0)),
            pl.BlockSpec((16, width), lambda b, i: (jnp.minimum((b * nt + i + 1) * h16, rows // 16 - 1), 0))]


def _fill_uext(uext, cur, prev, nxt, meta, i, nt, tm):
    uext[0:16] = jnp.where(i == 0, _glu(meta[...]), _glu(prev[...]))
    uext[16:16 + tm] = _glu(cur[...])
    uext[16 + tm:32 + tm] = jnp.where(i == nt - 1, 0.0, _glu(nxt[...]))


def _shifted_copies(dst, src, n):
    for r in range(1, 8):
        dst[r, 0:n] = src[r:r + n]


def _rows32(shifted, src, start, cols):
    q8, r = divmod(start, 8)
    if r == 0:
        return src[start:start + 32, cols]
    return shifted[r, 8 * q8:8 * q8 + 32, cols]


def _conv_fwd(pconv, pm_conv, conv_w, conv_b, nb, tm):
    rows = pconv.shape[0]
    nt = rows // nb // tm

    def body(cur, prev, nxt, meta, w_ref, b_ref, o_ref, uext, ush):
        i = pl.program_id(1)
        _fill_uext(uext, cur, prev, nxt, meta, i, nt, tm)
        _shifted_copies(ush, uext, tm + 24)
        for r0 in range(0, tm, 32):
            for c0 in range(0, D, 256):
                acc = jnp.zeros((32, 256), F32) + b_ref[:, c0:c0 + 256]
                for j in range(CONV_K):
                    acc = acc + _rows32(ush, uext, r0 + j + 1, slice(c0, c0 + 256)) * w_ref[j:j + 1, c0:c0 + 256]
                o_ref[r0:r0 + 32, c0:c0 + 256] = acc

    return pl.pallas_call(
        body, name="conv_fwd", grid=(nb, nt),
        out_shape=jax.ShapeDtypeStruct((rows, D), F32),
        in_specs=_halo_specs(2048, tm, nt, rows)
        + [pl.BlockSpec((16, 2048), lambda b, i: (0, 0)), pl.BlockSpec((32, D), lambda b, i: (0, 0)),
           pl.BlockSpec((1, D), lambda b, i: (0, 0))],
        out_specs=pl.BlockSpec((tm, D), lambda b, i: (b * nt + i, 0)),
        scratch_shapes=[pltpu.VMEM((tm + 32, D), F32), pltpu.VMEM((8, tm + 24, D), F32)],
        compiler_params=_params(("parallel", "parallel"), 40),
    )(pconv, pconv, pconv, pm_conv, conv_w, conv_b)


def _kv_prep(pkv, pm_kv, kg, cos, sin, nb):
    rows = pkv.shape[0]
    s_len = rows // nb
    tk = 128
    nt = s_len // tk

    def body(kv_ref, m_ref, g_ref, cos_ref, sin_ref, k_o, v_o):
        i = pl.program_id(1)
        mats = _qk_mats()

        @pl.when(i < nt)
        def _():
            kv = kv_ref[...]
            kr, _ = _qk_fwd(kv[:, :GROUP_W], g_ref[...], cos_ref[...], sin_ref[...], mats)
            ones = _ones_cols(tk, tk)
            for h in range(N_KV):
                k_o[0, h] = kr[:, HEAD_DIM * h:HEAD_DIM * (h + 1)].astype(BF16)
                vh = kv[:, GROUP_W + HEAD_DIM * h:GROUP_W + HEAD_DIM * (h + 1)]
                v_o[0, h] = jnp.concatenate([vh, ones], axis=1).astype(BF16)

        @pl.when(i == nt)
        def _():
            kv = m_ref[...]
            km = kv[:, :GROUP_W]
            kn = km * lax.rsqrt(_apply(km * km, mats[0]) + EPS) * g_ref[...]
            zeros = jnp.zeros((tk - N_META, GROUP_W), F32)
            kfull = jnp.concatenate([kn, zeros], axis=0)
            vfull = jnp.concatenate([kv[:, GROUP_W:], zeros], axis=0)
            ones = _ones_cols(tk, N_META)
            for h in range(N_KV):
                k_o[0, h] = kfull[:, HEAD_DIM * h:HEAD_DIM * (h + 1)].astype(BF16)
                v_o[0, h] = jnp.concatenate([vfull[:, HEAD_DIM * h:HEAD_DIM * (h + 1)], ones], axis=1).astype(BF16)

    lk = s_len + KEY_PAD
    last = nt - 1
    return pl.pallas_call(
        body, name="kv_prep", grid=(nb, nt + 1),
        out_shape=(jax.ShapeDtypeStruct((nb, N_KV, lk, HEAD_DIM), BF16),
                   jax.ShapeDtypeStruct((nb, N_KV, lk, 2 * HEAD_DIM), BF16)),
        in_specs=[pl.BlockSpec((tk, 512), lambda b, i: (b * nt + jnp.minimum(i, last), 0)),
                  pl.BlockSpec((N_META, 512), lambda b, i: (0, 0)), pl.BlockSpec((1, GROUP_W), lambda b, i: (0, 0)),
                  pl.BlockSpec((tk, GROUP_W), lambda b, i: (jnp.minimum(i, last), 0)),
                  pl.BlockSpec((tk, GROUP_W), lambda b, i: (jnp.minimum(i, last), 0))],
        out_specs=(pl.BlockSpec((1, N_KV, tk, HEAD_DIM), lambda b, i: (b, 0, i, 0)),
                   pl.BlockSpec((1, N_KV, tk, 2 * HEAD_DIM), lambda b, i: (b, 0, i, 0))),
        compiler_params=_params(("parallel", "arbitrary"), 32),
    )(pkv, pm_kv, kg, cos, sin)


def _ones_cols(rows, valid):
    r = lax.broadcasted_iota(jnp.int32, (rows, HEAD_DIM), 0)
    col = lax.broadcasted_iota(jnp.int32, (rows, HEAD_DIM), 1)
    return jnp.where((col < 2) & (r < valid), 1.0, 0.0).astype(F32)


def _tail_bias():
    col = lax.broadcasted_iota(jnp.int32, (1, KEY_PAD), 1)
    return jnp.where(col < N_META, 0.0, -1e30).astype(F32)


LOG2E = 1.4426950408889634


def _q_prep(pq, qg, cos, sin, nb, tm):
    rows = pq.shape[0]
    nt = rows // nb // tm
    scale = 1.0 / math.sqrt(HEAD_DIM)

    def body(q_ref, g_ref, cos_ref, sin_ref, q2_o, qt_o):
        gv, cosv, sinv = g_ref[...], cos_ref[...], sin_ref[...]
        mats = _qk_mats()
        for g in range(N_KV):
            gs = slice(GROUP_W * g, GROUP_W * (g + 1))
            qr, _ = _qk_fwd(q_ref[:, gs], gv, cosv, sinv, mats)
            q2_o[:, gs] = (qr * (scale * LOG2E)).astype(BF16)
            qt_o[gs, :] = (qr * scale).T.astype(BF16)

    row = pl.BlockSpec((tm, D), lambda b, i: (b * nt + i, 0))
    rope = pl.BlockSpec((tm, GROUP_W), lambda b, i: (i, 0))
    return pl.pallas_call(
        body, name="q_prep", grid=(nb, nt),
        out_shape=(jax.ShapeDtypeStruct((rows, D), BF16), jax.ShapeDtypeStruct((D, rows), BF16)),
        in_specs=[row, pl.BlockSpec((1, GROUP_W), lambda b, i: (0, 0)), rope, rope],
        out_specs=(row, pl.BlockSpec((D, tm), lambda b, i: (0, b * nt + i))),
        compiler_params=_params(("parallel", "parallel"), 32),
    )(pq, qg, cos, sin)


def _attn_fwd(q2, kr, ve, nb, tq):
    rows = q2.shape[0]
    s_len = rows // nb
    nq = s_len // tq
    lk = s_len + KEY_PAD

    def body(q_ref, k_ref, v_ref, o_ref, lse_ref):
        qs = q_ref[...]
        k1, k2 = k_ref[0, 0, 0:s_len, :], k_ref[0, 0, s_len:lk, :]
        v1, v2 = v_ref[0, 0, 0:s_len, :], v_ref[0, 0, s_len:lk, :]
        bias = _tail_bias()
        outs, lses = [], []

        def scores(h):
            qh = qs[:, HEAD_DIM * h:HEAD_DIM * (h + 1)]
            return _dot_nt(qh, k1), _dot_nt(qh, k2) + bias

        ahead = scores(0)
        for h in range(GQA):
            s1, s2 = ahead
            if h + 1 < GQA:
                ahead = scores(h + 1)
            m = jnp.maximum(jnp.max(s1, axis=-1, keepdims=True), jnp.max(s2, axis=-1, keepdims=True))
            oe = _dot(jnp.exp2(s1 - m).astype(BF16), v1) + _dot(jnp.exp2(s2 - m).astype(BF16), v2)
            l = oe[:, HEAD_DIM:HEAD_DIM + 1]
            outs.append(oe[:, :HEAD_DIM] / l)
            lses.append(m + jnp.log2(l))
        o_ref[...] = jnp.concatenate(outs, axis=1)
        lse_ref[0, 0] = jnp.concatenate(lses, axis=1)

    return pl.pallas_call(
        body, name="attn_fwd", grid=(nb, N_KV, nq),
        out_shape=(jax.ShapeDtypeStruct((rows, D), F32), jax.ShapeDtypeStruct((nb, N_KV, s_len, GQA), F32)),
        in_specs=[pl.BlockSpec((tq, GROUP_W), lambda b, g, i: (b * nq + i, g)),
                  pl.BlockSpec((1, 1, lk, HEAD_DIM), lambda b, g, i: (b, g, 0, 0)),
                  pl.BlockSpec((1, 1, lk, 2 * HEAD_DIM), lambda b, g, i: (b, g, 0, 0))],
        out_specs=(pl.BlockSpec((tq, GROUP_W), lambda b, g, i: (b * nq + i, g)),
                   pl.BlockSpec((1, 1, tq, GQA), lambda b, g, i: (b, g, i, 0))),
        compiler_params=_params(("parallel", "parallel", "parallel"), 48),
    )(q2, kr, ve)


def _mid(x2, t2, c0, cz, o, e, w3, cn_g, cn_b, tm):
    rows = x2.shape[0]

    def body(x_ref, t_ref, c0_ref, cz_ref, o_ref, e_ref, w_ref, g_ref, b_ref,
             dy_o, mt_o, c3t_o, o2t_o, dyc_o, dya_o, do_o, dc0_o, dcz_o, de_o, sums_o):
        wco, wao, wo = w_ref[0], w_ref[1], w_ref[2]
        cn_g_v = g_ref[...]
        c0v = c0_ref[...]
        xc = c0v - jnp.mean(c0v, axis=-1, keepdims=True)
        rstd = lax.rsqrt(jnp.mean(xc * xc, axis=-1, keepdims=True) + EPS)
        n = xc * rstd
        c1 = n * cn_g_v + b_ref[...]
        s1 = _sig(c1)
        c2 = c1 * s1
        czv = cz_ref[...]
        sz = _sig(czv)
        gz = czv * sz
        c3 = c2 * gz
        yc = _dot(c3.astype(BF16), wco)
        az, gc, ga = e_ref[:, :D], e_ref[:, D:2 * D], e_ref[:, 2 * D:]
        saz = _sig(az)
        gaz = az * saz
        ov = o_ref[...]
        o2 = ov * gaz
        ya = _dot(o2.astype(BF16), wao)
        sc, sa = _sig(gc), _sig(ga)
        merged = sc * yc + sa * ya
        out = _dot(merged.astype(BF16), wo)
        err = x_ref[...] + out - t_ref[...]
        dy = err * (1.0 / D)
        dy_o[...] = dy
        dm = _dot_nt(dy.astype(BF16), wo)
        dyc = dm * sc
        dya = dm * sa
        dycb, dyab = dyc.astype(BF16), dya.astype(BF16)
        dyc_o[...] = dycb
        dya_o[...] = dyab
        de_o[:, D:2 * D] = (dyc * yc * (1.0 - sc)).astype(BF16)
        de_o[:, 2 * D:] = (dya * ya * (1.0 - sa)).astype(BF16)
        dc3 = _dot_nt(dycb, wco)
        do2 = _dot_nt(dyab, wao)
        do_o[...] = do2 * gaz
        de_o[:, :D] = (do2 * ov * _dsilu(az, saz)).astype(BF16)
        dcz_o[...] = (dc3 * c2 * _dsilu(czv, sz)).astype(BF16)
        dc1 = dc3 * gz * _dsilu(c1, s1)
        dn = dc1 * cn_g_v
        dc0 = rstd * (dn - jnp.mean(dn, axis=-1, keepdims=True) - n * jnp.mean(dn * n, axis=-1, keepdims=True))
        dc0_o[...] = dc0
        mt_o[...] = merged.T.astype(BF16)
        c3t_o[...] = c3.T.astype(BF16)
        o2t_o[...] = o2.T.astype(BF16)

        @pl.when(pl.program_id(0) == 0)
        def _():
            sums_o[...] = jnp.zeros_like(sums_o)

        sums_o[0:1, :] += jnp.sum(dc1 * n, axis=0, keepdims=True)
        sums_o[1:2, :] += jnp.sum(dc1, axis=0, keepdims=True)
        sums_o[2:3, :] += jnp.sum(dc0, axis=0, keepdims=True)
        sums_o[3:4, :] += jnp.sum(err * err, axis=0, keepdims=True)

    row = lambda wd: pl.BlockSpec((tm, wd), lambda i: (i, 0))
    col = pl.BlockSpec((D, tm), lambda i: (0, i))
    vec = pl.BlockSpec((1, D), lambda i: (0, 0))
    f32o = lambda wd: jax.ShapeDtypeStruct((rows, wd), F32)
    b16o = lambda wd: jax.ShapeDtypeStruct((rows, wd), BF16)
    tpo = jax.ShapeDtypeStruct((D, rows), BF16)
    return pl.pallas_call(
        body, name="mid", grid=(rows // tm,),
        out_shape=(f32o(D), tpo, tpo, tpo, b16o(D), b16o(D), f32o(D), f32o(D), b16o(D), b16o(3 * D),
                   jax.ShapeDtypeStruct((8, D), F32)),
        in_specs=[row(D), row(D), row(D), row(D), row(D), row(3 * D),
                  pl.BlockSpec((3, D, D), lambda i: (0, 0, 0)), vec, vec],
        out_specs=(row(D), col, col, col, row(D), row(D), row(D), row(D), row(D), row(3 * D),
                   pl.BlockSpec((8, D), lambda i: (0, 0))),
        compiler_params=_params(("arbitrary",), 56),
    )(x2, t2, c0, cz, o, e, w3, cn_g, cn_b)


def _do_prep(d_o, o, tm):
    rows = d_o.shape[0]

    def body(do_ref, o_ref, doe_o, dot_o):
        dov = do_ref[...]
        prod = dov * o_ref[...]
        col = lax.broadcasted_iota(jnp.int32, (tm, HEAD_DIM), 1)
        for h in range(D // HEAD_DIM):
            hs = slice(HEAD_DIM * h, HEAD_DIM * (h + 1))
            delta = jnp.sum(prod[:, hs], axis=-1, keepdims=True)
            d_hi = delta.astype(BF16).astype(F32)
            tail = jnp.where(col == 0, -d_hi, jnp.where(col == 1, d_hi - delta, 0.0))
            doe_o[:, 2 * HEAD_DIM * h:2 * HEAD_DIM * (h + 1)] = jnp.concatenate([dov[:, hs], tail], axis=1).astype(BF16)
        dot_o[...] = dov.T.astype(BF16)

    row = pl.BlockSpec((tm, D), lambda i: (i, 0))
    return pl.pallas_call(
        body, name="do_prep", grid=(rows // tm,),
        out_shape=(jax.ShapeDtypeStruct((rows, 2 * D), BF16), jax.ShapeDtypeStruct((D, rows), BF16)),
        in_specs=[row, row],
        out_specs=(pl.BlockSpec((tm, 2 * D), lambda i: (i, 0)), pl.BlockSpec((D, tm), lambda i: (0, i))),
        compiler_params=_params(("parallel",), 32),
    )(d_o, o)


def _q_post(dqr, pq, qg, cos, sin, nb, tm):
    rows = pq.shape[0]
    nt = rows // nb // tm

    def body(dq_ref, q_ref, g_ref, cos_ref, sin_ref, dq_o, dg_o):
        @pl.when((pl.program_id(0) == 0) & (pl.program_id(1) == 0))
        def _():
            dg_o[...] = jnp.zeros_like(dg_o)

        gv, cosv, sinv = g_ref[...], cos_ref[...], sin_ref[...]
        acc = jnp.zeros((1, GROUP_W), F32)
        mats = _qk_mats()
        for g in range(N_KV):
            gs = slice(GROUP_W * g, GROUP_W * (g + 1))
            qv = q_ref[:, gs]
            r = lax.rsqrt(_apply(qv * qv, mats[0]) + EPS)
            dq, dgr = _qk_bwd(dq_ref[:, gs], qv, r, gv, cosv, sinv, mats)
            dq_o[:, gs] = dq.astype(BF16)
            acc = acc + jnp.sum(dgr, axis=0, keepdims=True)
        dg_o[...] += acc

    row = pl.BlockSpec((tm, D), lambda b, i: (b * nt + i, 0))
    rope = pl.BlockSpec((tm, GROUP_W), lambda b, i: (i, 0))
    vec = pl.BlockSpec((1, GROUP_W), lambda b, i: (0, 0))
    return pl.pallas_call(
        body, name="q_post", grid=(nb, nt),
        out_shape=(jax.ShapeDtypeStruct((rows, D), BF16), jax.ShapeDtypeStruct((1, GROUP_W), F32)),
        in_specs=[row, row, vec, rope, rope], out_specs=(row, vec),
        compiler_params=_params(("arbitrary", "arbitrary"), 32),
    )(dqr, pq, qg, cos, sin)


def _attn_bwd(q2, qst, kr, ve, doe, dot_, lse, nb, tq):
    rows = q2.shape[0]
    s_len = rows // nb
    nq = s_len // tq
    lk = s_len + KEY_PAD
    scale = 1.0 / math.sqrt(HEAD_DIM)

    def body(q_ref, qt_ref, k_ref, v_ref, doe_ref, dot_ref, lse_ref, dq_o, dkt_o, dvt_o):
        i = pl.program_id(2)
        lse = lse_ref[0, 0]
        k1, k2 = k_ref[0, 0, 0:s_len, :], k_ref[0, 0, s_len:lk, :]
        v1, v2 = v_ref[0, 0, 0:s_len, :], v_ref[0, 0, s_len:lk, :]
        bias = _tail_bias()
        dkt1, dkt2 = jnp.zeros((HEAD_DIM, s_len), F32), jnp.zeros((HEAD_DIM, KEY_PAD), F32)
        dvt1, dvt2 = jnp.zeros((HEAD_DIM, s_len), F32), jnp.zeros((HEAD_DIM, KEY_PAD), F32)

        def products(h):
            qh = q_ref[:, HEAD_DIM * h:HEAD_DIM * (h + 1)]
            dh = doe_ref[:, 2 * HEAD_DIM * h:2 * HEAD_DIM * (h + 1)]
            return _dot_nt(qh, k1), _dot_nt(qh, k2) + bias, _dot_nt(dh, v1), _dot_nt(dh, v2)

        ahead = products(0)
        for h in range(GQA):
            hs = slice(HEAD_DIM * h, HEAD_DIM * (h + 1))
            s1, s2, dp1, dp2 = ahead
            if h + 1 < GQA:
                ahead = products(h + 1)
            lse_h = lse[:, h:h + 1]
            p1 = jnp.exp2(s1 - lse_h)
            p2 = jnp.exp2(s2 - lse_h)
            ds1 = (p1 * dp1).astype(BF16)
            ds2 = (p2 * dp2).astype(BF16)
            dq_o[:, hs] = (_dot(ds1, k1) + _dot(ds2, k2)) * scale
            dkt1 = dkt1 + _dot(qt_ref[hs, :], ds1)
            dkt2 = dkt2 + _dot(qt_ref[hs, :], ds2)
            dvt1 = dvt1 + _dot(dot_ref[hs, :], p1.astype(BF16))
            dvt2 = dvt2 + _dot(dot_ref[hs, :], p2.astype(BF16))

        @pl.when(i == 0)
        def _():
            dkt_o[0, 0, :, 0:s_len] = dkt1
            dkt_o[0, 0, :, s_len:lk] = dkt2
            dvt_o[0, 0, :, 0:s_len] = dvt1
            dvt_o[0, 0, :, s_len:lk] = dvt2

        @pl.when(i > 0)
        def _():
            dkt_o[0, 0, :, 0:s_len] += dkt1
            dkt_o[0, 0, :, s_len:lk] += dkt2
            dvt_o[0, 0, :, 0:s_len] += dvt1
            dvt_o[0, 0, :, s_len:lk] += dvt2

    qspec = pl.BlockSpec((tq, GROUP_W), lambda b, g, i: (b * nq + i, g))
    qtspec = pl.BlockSpec((GROUP_W, tq), lambda b, g, i: (g, b * nq + i))
    kspec = pl.BlockSpec((1, 1, lk, HEAD_DIM), lambda b, g, i: (b, g, 0, 0))
    tspec = pl.BlockSpec((1, 1, HEAD_DIM, lk), lambda b, g, i: (b, g, 0, 0))
    return pl.pallas_call(
        body, name="attn_bwd", grid=(nb, N_KV, nq),
        out_shape=(jax.ShapeDtypeStruct((rows, D), F32), jax.ShapeDtypeStruct((nb, N_KV, HEAD_DIM, lk), F32),
                   jax.ShapeDtypeStruct((nb, N_KV, HEAD_DIM, lk), F32)),
        in_specs=[qspec, qtspec, kspec, pl.BlockSpec((1, 1, lk, 2 * HEAD_DIM), lambda b, g, i: (b, g, 0, 0)),
                  pl.BlockSpec((tq, 2 * GROUP_W), lambda b, g, i: (b * nq + i, g)), qtspec,
                  pl.BlockSpec((1, 1, tq, GQA), lambda b, g, i: (b, g, i, 0))],
        out_specs=(qspec, tspec, tspec),
        compiler_params=_params(("parallel", "parallel", "arbitrary"), 56),
    )(q2, qst, kr, ve, doe, dot_, lse)


def _kv_bwd(dkt, dvt, pkv, pm_kv, kg, cos, sin, nb):
    rows = pkv.shape[0]
    s_len = rows // nb
    tk = 128
    nt = s_len // tk
    last = nt - 1

    def body(dk_ref, dv_ref, kv_ref, m_ref, g_ref, cos_ref, sin_ref, d_o, dm_o, dg_o):
        b, i = pl.program_id(0), pl.program_id(1)
        dkr = dk_ref[0].T
        dv = dv_ref[0].T
        gv = g_ref[...]
        mats = _qk_mats()

        @pl.when((b == 0) & (i == 0))
        def _():
            dg_o[...] = jnp.zeros_like(dg_o)

        @pl.when(i < nt)
        def _():
            kx = kv_ref[:, :GROUP_W]
            r = lax.rsqrt(_apply(kx * kx, mats[0]) + EPS)
            dk, dgr = _qk_bwd(dkr, kx, r, gv, cos_ref[...], sin_ref[...], mats)
            d_o[:, :GROUP_W] = dk.astype(BF16)
            d_o[:, GROUP_W:] = dv.astype(BF16)
            dg_o[...] += jnp.sum(dgr, axis=0, keepdims=True)

        @pl.when(i == nt)
        def _():
            kx = m_ref[:, :GROUP_W]
            r = lax.rsqrt(_apply(kx * kx, mats[0]) + EPS)
            dn = dkr[0:N_META]
            dyg = dn * gv
            dm_o[0, :, :GROUP_W] = r * dyg - kx * (r * r * r) * _apply(dyg * kx, mats[0])
            dm_o[0, :, GROUP_W:] = dv[0:N_META]
            dg_o[...] += jnp.sum(dn * kx * r, axis=0, keepdims=True)

    tspec = pl.BlockSpec((1, GROUP_W, tk), lambda b, i: (b, 0, i))
    rope = pl.BlockSpec((tk, GROUP_W), lambda b, i: (jnp.minimum(i, last), 0))
    return pl.pallas_call(
        body, name="kv_bwd", grid=(nb, nt + 1),
        out_shape=(jax.ShapeDtypeStruct((rows, 512), BF16), jax.ShapeDtypeStruct((nb, N_META, 512), F32),
                   jax.ShapeDtypeStruct((1, GROUP_W), F32)),
        in_specs=[tspec, tspec, pl.BlockSpec((tk, 512), lambda b, i: (b * nt + jnp.minimum(i, last), 0)),
                  pl.BlockSpec((N_META, 512), lambda b, i: (0, 0)), pl.BlockSpec((1, GROUP_W), lambda b, i: (0, 0)),
                  rope, rope],
        out_specs=(pl.BlockSpec((tk, 512), lambda b, i: (b * nt + jnp.minimum(i, last), 0)),
                   pl.BlockSpec((1, N_META, 512), lambda b, i: (b, 0, 0)),
                   pl.BlockSpec((1, GROUP_W), lambda b, i: (0, 0))),
        compiler_params=_params(("arbitrary", "arbitrary"), 32),
    )(dkt, dvt, pkv, pm_kv, kg, cos, sin)


def _conv_bwd(dc0, pconv, pm_conv, conv_w, nb, tm):
    rows = pconv.shape[0]
    nt = rows // nb // tm

    def body(dcur, dprev, dnxt, cur, meta, w_ref, da_o, dam_o, gw_o, ucur, dext, dsh):
        b, i = pl.program_id(0), pl.program_id(1)
        ucur[...] = _glu(cur[...])
        dext[0:16] = jnp.zeros((16, D), F32)
        dext[16:32] = jnp.where(i == 0, 0.0, dprev[...])
        dext[32:32 + tm] = dcur[...]
        dext[32 + tm:48 + tm] = jnp.where(i == nt - 1, 0.0, dnxt[...])
        _shifted_copies(dsh, dext, tm + 40)

        @pl.when((b == 0) & (i == 0))
        def _():
            gw_o[...] = jnp.zeros_like(gw_o)

        for c0 in range(0, D, 256):
            cs = slice(c0, c0 + 256)
            for r0 in range(0, tm, 32):
                acc = jnp.zeros((32, 256), F32)
                for j in range(CONV_K):
                    acc = acc + _rows32(dsh, dext, r0 + 47 - j, cs) * w_ref[j:j + 1, cs]
                cv = cur[r0:r0 + 32, c0:c0 + 256]
                sg = _sig(cur[r0:r0 + 32, D + c0:D + c0 + 256])
                da_o[r0:r0 + 32, cs] = (acc * sg).astype(BF16)
                da_o[r0:r0 + 32, D + c0:D + c0 + 256] = (acc * cv * sg * (1.0 - sg)).astype(BF16)
            for j in range(CONV_K):
                acc = jnp.zeros((32, 256), F32)
                for r0 in range(0, tm, 32):
                    acc = acc + _rows32(dsh, dext, r0 + 47 - j, cs) * ucur[r0:r0 + 32, cs]
                gw_o[j:j + 1, cs] += jnp.sum(acc, axis=0, keepdims=True)

        @pl.when(i == 0)
        def _():
            for c0 in range(0, D, 256):
                cs = slice(c0, c0 + 256)
                cv = meta[:, c0:c0 + 256]
                sg = _sig(meta[:, D + c0:D + c0 + 256])
                um = cv * sg
                acc = jnp.zeros((16, 256), F32)
                for j in range(CONV_K):
                    d = dext[31 - j:47 - j, cs]
                    acc = acc + d * w_ref[j:j + 1, cs]
                    gw_o[j:j + 1, cs] += jnp.sum(d * um, axis=0, keepdims=True)
                dam_o[0, :, cs] = acc * sg
                dam_o[0, :, D + c0:D + c0 + 256] = acc * cv * sg * (1.0 - sg)

    return pl.pallas_call(
        body, name="conv_bwd", grid=(nb, nt),
        out_shape=(jax.ShapeDtypeStruct((rows, 2048), BF16), jax.ShapeDtypeStruct((nb, N_META, 2048), F32),
                   jax.ShapeDtypeStruct((32, D), F32)),
        in_specs=_halo_specs(D, tm, nt, rows)
        + [pl.BlockSpec((tm, 2048), lambda b, i: (b * nt + i, 0)),
           pl.BlockSpec((16, 2048), lambda b, i: (0, 0)), pl.BlockSpec((32, D), lambda b, i: (0, 0))],
        out_specs=(pl.BlockSpec((tm, 2048), lambda b, i: (b * nt + i, 0)),
                   pl.BlockSpec((1, N_META, 2048), lambda b, i: (b, 0, 0)),
                   pl.BlockSpec((32, D), lambda b, i: (0, 0))),
        scratch_shapes=[pltpu.VMEM((tm, D), F32), pltpu.VMEM((tm + 48, D), F32), pltpu.VMEM((8, tm + 40, D), F32)],
        compiler_params=_params(("arbitrary", "arbitrary"), 48),
    )(dc0, dc0, dc0, pconv, pm_conv, conv_w)


def _meta_bwd(dam, ddm, w_full, meta_full, norm_g):
    nb = dam.shape[0]

    def body(a_ref, d_ref, wc_ref, wkv_ref, m_ref, g_ref, gm_o, dg_o):
        a, d = a_ref[0], d_ref[0]
        for b in range(1, nb):
            a = a + a_ref[b]
            d = d + d_ref[b]
        dxn = _dot_nt(a.astype(BF16), wc_ref[...]) + _dot_nt(d.astype(BF16), wkv_ref[...])
        v = m_ref[...]
        r = lax.rsqrt(jnp.mean(v * v, axis=-1, keepdims=True) + EPS)
        gm_o[...] = _rms_bwd(dxn, v, r, g_ref[...])
        dg_o[...] = jnp.sum(dxn * v * r, axis=0, keepdims=True)

    return pl.pallas_call(
        body, name="meta_bwd", grid=(1,),
        out_shape=(jax.ShapeDtypeStruct((N_META, D), F32), jax.ShapeDtypeStruct((1, D), F32)),
        in_specs=[pl.BlockSpec((nb, N_META, 2048), lambda i: (0, 0, 0)), pl.BlockSpec((nb, N_META, 512), lambda i: (0, 0, 0)),
                  pl.BlockSpec((D, 2048), lambda i: (0, 0)), pl.BlockSpec((D, 512), lambda i: (0, G_KV[0] // 512)),
                  pl.BlockSpec((N_META, D), lambda i: (0, 0)), pl.BlockSpec((1, D), lambda i: (0, 0))],
        out_specs=(pl.BlockSpec((N_META, D), lambda i: (0, 0)), pl.BlockSpec((1, D), lambda i: (0, 0))),
        compiler_params=_params(("arbitrary",), 32),
    )(dam, ddm, w_full, w_full, meta_full, norm_g)


def _dxn(d_groups, w_full, x2, dy, norm_g, dg_init, tm):
    rows = x2.shape[0]
    groups = (G_CONV, G_CZ, G_Q, G_KV, G_E)

    def body(da, db, dq, dd, de, w_hbm, x_ref, dy_ref, g_ref, gi_ref, gx_o, dg_o, w_vmem, sem):
        @pl.when(pl.program_id(0) == 0)
        def _():
            cp = pltpu.make_async_copy(w_hbm, w_vmem, sem)
            cp.start()
            cp.wait()
            dg_o[...] = gi_ref[...]

        dxn = jnp.zeros((tm, D), F32)
        for ref, (off, wd) in zip((da, db, dq, dd, de), groups):
            for c0 in range(0, wd, 512):
                dxn = dxn + _dot_nt(ref[:, c0:c0 + 512], w_vmem[:, off + c0:off + c0 + 512])
        v = x_ref[...]
        r = lax.rsqrt(jnp.mean(v * v, axis=-1, keepdims=True) + EPS)
        gx_o[...] = dy_ref[...] + _rms_bwd(dxn, v, r, g_ref[...])
        dg_o[...] += jnp.sum(dxn * v * r, axis=0, keepdims=True)

    row = lambda wd: pl.BlockSpec((tm, wd), lambda i: (i, 0))
    vec = pl.BlockSpec((1, D), lambda i: (0, 0))
    return pl.pallas_call(
        body, name="dxn", grid=(rows // tm,),
        out_shape=(jax.ShapeDtypeStruct((rows, D), F32), jax.ShapeDtypeStruct((1, D), F32)),
        in_specs=[row(wd) for _, wd in groups] + [pl.BlockSpec(memory_space=pl.ANY), row(D), row(D), vec, vec],
        out_specs=(row(D), vec),
        scratch_shapes=[pltpu.VMEM((D, IN_DIM), BF16), pltpu.SemaphoreType.DMA],
        compiler_params=_params(("arbitrary",), 56),
    )(*d_groups, w_full, x2, dy, norm_g, dg_init)


def _wgrad(at, b, buf, slot, col_off, name, meta=None):
    rows, n = b.shape
    tn, tk = 512, min(2048, rows)
    nk = rows // tk
    j0 = col_off // tn

    def body(*refs):
        if meta is None:
            at_ref, b_ref, _, o_ref = refs
        else:
            at_ref, b_ref, xm_ref, dm_ref, _, o_ref = refs
        k = pl.program_id(1)

        @pl.when(k == 0)
        def _():
            if meta is None:
                o_ref[0] = jnp.zeros((D, tn), F32)
            else:
                dm = dm_ref[0]
                for e in range(1, dm_ref.shape[0]):
                    dm = dm + dm_ref[e]
                dm = jnp.concatenate([dm, jnp.zeros((128 - N_META, tn), F32)], axis=0)
                o_ref[0] = _dot(xm_ref[...], dm.astype(BF16))

        o_ref[0] += _dot(at_ref[...], b_ref[...].astype(BF16))

    in_specs = [pl.BlockSpec((D, tk), lambda j, k: (0, k)), pl.BlockSpec((tk, tn), lambda j, k: (k, j))]
    args = [at, b]
    if meta is not None:
        xmt, dm = meta
        in_specs += [pl.BlockSpec((D, 128), lambda j, k: (0, 0)),
                     pl.BlockSpec((dm.shape[0], N_META, tn), lambda j, k: (0, 0, j))]
        args += [xmt, dm]
    in_specs.append(pl.BlockSpec(memory_space=pl.ANY))
    args.append(buf)
    return pl.pallas_call(
        body, name=name, grid=(n // tn, nk),
        out_shape=jax.ShapeDtypeStruct(buf.shape, F32),
        in_specs=in_specs,
        out_specs=pl.BlockSpec((1, D, tn), lambda j, k: (slot, 0, j0 + j)),
        input_output_aliases={len(args) - 1: 0},
        compiler_params=_params(("parallel", "arbitrary"), 32),
    )(*args)


def _rope_tables(s_len):
    pos = jnp.arange(s_len, dtype=jnp.int32)
    row_ids = (pos // GRID_W).astype(F32)
    col_ids = (pos % GRID_W).astype(F32)
    inv_freq = ROPE_THETA ** (-jnp.arange(ROPE_FREQS, dtype=F32) / ROPE_FREQS)
    a_row = row_ids[:, None] * inv_freq[None, :]
    a_col = col_ids[:, None] * inv_freq[None, :]
    ang = jnp.concatenate([a_row, a_row, a_col, a_col], axis=-1)
    return jnp.tile(jnp.cos(ang), (1, GQA)), jnp.tile(jnp.sin(ang), (1, GQA))


def _local_step(x, loss_target, norm_g, conv_b, cn_g, cn_b, q_g, k_g, w_full, w3_full, conv_w_full, meta_full):
    nb, s_len, _ = x.shape
    rows = nb * s_len
    x2 = x.reshape(rows, D)
    t2 = loss_target.reshape(rows, D)
    cos, sin = _rope_tables(s_len)
    qg = jnp.tile(q_g, (1, GQA))
    kg = jnp.tile(k_g, (1, N_KV))

    xnmt, pm_conv, pm_kv = _meta_fwd(meta_full, norm_g, w_full)
    pconv, pcz, pq, pkv, pe, xnt = _in_proj(x2, norm_g, w_full, 256)
    c0 = _conv_fwd(pconv, pm_conv, conv_w_full, conv_b, nb, 256)
    kr, ve = _kv_prep(pkv, pm_kv, kg, cos, sin, nb)
    q2, qst = _q_prep(pq, qg, cos, sin, nb, 256)
    o, lse = _attn_fwd(q2, kr, ve, nb, 256)
    dy, mt, c3t, o2t, dyc, dya, d_o, dc0, dcz, de, sums = _mid(x2, t2, c0, pcz, o, pe, w3_full, cn_g, cn_b, 128)
    doe, dot_ = _do_prep(d_o, o, 256)
    dqr, dkt, dvt = _attn_bwd(q2, qst, kr, ve, doe, dot_, lse, nb, 256)
    dq, dqg = _q_post(dqr, pq, qg, cos, sin, nb, 256)
    lk = s_len + KEY_PAD
    dd, ddm, dkg = _kv_bwd(dkt.reshape(nb, GROUP_W, lk), dvt.reshape(nb, GROUP_W, lk), pkv, pm_kv, kg, cos, sin, nb)
    da, dam, gcw = _conv_bwd(dc0, pconv, pm_conv, conv_w_full, nb, 256)
    gmeta, dng_m = _meta_bwd(dam, ddm, w_full, meta_full, norm_g)
    gx, dng = _dxn((da, dcz, dq, dd, de), w_full, x2, dy, norm_g, dng_m, 256)

    gw3 = lax.empty((3, D, D), F32)
    gw3 = _wgrad(c3t, dyc, gw3, 0, 0, "wgrad_conv_out")
    gw3 = _wgrad(o2t, dya, gw3, 1, 0, "wgrad_attn_out")
    gw3 = _wgrad(mt, dy, gw3, 2, 0, "wgrad_out")
    gwin = lax.empty((1, D, IN_DIM), F32)
    gwin = _wgrad(xnt, da, gwin, 0, G_CONV[0], "wgrad_in_conv", meta=(xnmt, dam))
    gwin = _wgrad(xnt, dcz, gwin, 0, G_CZ[0], "wgrad_in_cz")
    gwin = _wgrad(xnt, dq, gwin, 0, G_Q[0], "wgrad_in_q")
    gwin = _wgrad(xnt, dd, gwin, 0, G_KV[0], "wgrad_in_kv", meta=(xnmt, ddm))
    gwin = _wgrad(xnt, de, gwin, 0, G_E[0], "wgrad_in_e")

    zeros = jnp.zeros((1, D - 2 * GROUP_W), F32)
    smalls = jnp.concatenate([dng, sums[2:3], sums[0:1], sums[1:2], jnp.concatenate([dqg, dkg, zeros], axis=1),
                              sums[3:4], jnp.zeros((2, D), F32)], axis=0)
    return gx.reshape(nb, s_len, D), gwin, gw3, gcw, gmeta, smalls


def _xyc():
    return lax.axis_index("x"), lax.axis_index("y"), lax.axis_index("c")


def _reduce_sibling(gwin, gw3v, gcm, smalls):
    def body(gwin_ref, gw3_ref, gcm_ref, sm_ref, r_win, r_w3, r_cm, r_sm, send, recv, ssend, srecv, lsem):
        x, y, c = _xyc()
        o = 1 - c
        sib = (x, y, o)
        half = D // 2
        outs = ((gwin_ref.at[pl.ds(o * half, half), :], r_win), (gw3_ref.at[:, :, o], r_w3), (gcm_ref.at[o], r_cm))
        cps = []
        for a, (src, dst) in enumerate(outs):
            cp = pltpu.make_async_remote_copy(src_ref=src, dst_ref=dst, send_sem=send.at[a], recv_sem=recv.at[a],
                                              device_id=sib, device_id_type=MESH)
            cp.start()
            cps.append(cp)
        me = 4 * x + 2 * y + c
        loc = pltpu.make_async_copy(sm_ref, r_sm.at[me], lsem)
        loc.start()
        scps = []
        for d in range(1, 8):
            px, py, pc = (x + (d >> 2)) % 2, (y + ((d >> 1) & 1)) % 2, (c + (d & 1)) % 2
            cp = pltpu.make_async_remote_copy(src_ref=sm_ref, dst_ref=r_sm.at[me], send_sem=ssend.at[d - 1],
                                              recv_sem=srecv.at[d - 1], device_id=(px, py, pc), device_id_type=MESH)
            cp.start()
            scps.append((cp, 4 * px + 2 * py + pc))
        for cp in cps:
            cp.wait()
        for d, (cp, pid) in enumerate(scps):
            pltpu.make_async_remote_copy(src_ref=sm_ref, dst_ref=r_sm.at[pid], send_sem=ssend.at[d],
                                         recv_sem=srecv.at[d], device_id=(x, y, c), device_id_type=MESH).wait_recv()
            cp.wait_send()
        loc.wait()

    any_spec = pl.BlockSpec(memory_space=pl.ANY)
    return pl.pallas_call(
        body, name="reduce_sibling",
        out_shape=(jax.ShapeDtypeStruct((D // 2, IN_DIM), F32), jax.ShapeDtypeStruct((3, 4, 128, D), F32),
                   jax.ShapeDtypeStruct((24, D), F32), jax.ShapeDtypeStruct((8, 8, D), F32)),
        in_specs=[any_spec] * 4, out_specs=(any_spec,) * 4,
        scratch_shapes=[pltpu.SemaphoreType.DMA((3,)), pltpu.SemaphoreType.DMA((3,)),
                        pltpu.SemaphoreType.DMA((7,)), pltpu.SemaphoreType.DMA((7,)), pltpu.SemaphoreType.DMA],
    )(gwin, gw3v, gcm, smalls)


def _add_sibling(gwin, gw3v, gcm, r_win, r_w3, r_cm):
    c = lax.axis_index("c").astype(jnp.int32).reshape(1)
    half = D // 2
    tr = 64

    def body1(c_ref, a_ref, b_ref, o_ref):
        o_ref[...] = (a_ref[...] + b_ref[...]).astype(BF16)

    cs_win = pl.pallas_call(
        body1, name="add_sibling_w_in", out_shape=jax.ShapeDtypeStruct((half, IN_DIM), BF16),
        grid_spec=pltpu.PrefetchScalarGridSpec(
            num_scalar_prefetch=1, grid=(half // tr,),
            in_specs=[pl.BlockSpec((tr, IN_DIM), lambda i, c_ref: (c_ref[0] * (half // tr) + i, 0)),
                      pl.BlockSpec((tr, IN_DIM), lambda i, c_ref: (i, 0))],
            out_specs=pl.BlockSpec((tr, IN_DIM), lambda i, c_ref: (i, 0))),
        compiler_params=_params(("parallel",), 32),
    )(c, gwin, r_win)

    def body2(c_ref, a_ref, b_ref, o_ref):
        o_ref[0, 0] = (a_ref[0, 0, 0] + b_ref[0, 0]).astype(BF16)

    cs_w3 = pl.pallas_call(
        body2, name="add_sibling_w3", out_shape=jax.ShapeDtypeStruct((3, 4, 128, D), BF16),
        grid_spec=pltpu.PrefetchScalarGridSpec(
            num_scalar_prefetch=1, grid=(3, 4),
            in_specs=[pl.BlockSpec((1, 1, 1, 128, D), lambda w, s, c_ref: (w, s, c_ref[0], 0, 0)),
                      pl.BlockSpec((1, 1, 128, D), lambda w, s, c_ref: (w, s, 0, 0))],
            out_specs=pl.BlockSpec((1, 1, 128, D), lambda w, s, c_ref: (w, s, 0, 0))),
        compiler_params=_params(("parallel", "parallel"), 32),
    )(c, gw3v, r_w3)

    def body3(c_ref, a_ref, b_ref, o_ref):
        o_ref[...] = a_ref[0] + b_ref[...]

    cs_cm = pl.pallas_call(
        body3, name="add_sibling_cm", out_shape=jax.ShapeDtypeStruct((24, D), F32),
        grid_spec=pltpu.PrefetchScalarGridSpec(
            num_scalar_prefetch=1, grid=(1,),
            in_specs=[pl.BlockSpec((1, 24, D), lambda i, c_ref: (c_ref[0], 0, 0)),
                      pl.BlockSpec((24, D), lambda i, c_ref: (0, 0))],
            out_specs=pl.BlockSpec((24, D), lambda i, c_ref: (0, 0))),
        compiler_params=_params(("arbitrary",), 32),
    )(c, gcm, r_cm)
    return cs_win, cs_w3, cs_cm


def _reduce_chips(cs_win, cs_w3, cs_cm):
    def body(win_ref, w3_ref, cm_ref, r_win, r_w3, r_cm, send, recv):
        x, y, c = _xyc()
        peers = ((1 - x, y), (x, 1 - y), (1 - x, 1 - y))
        cps = []
        for k, (px, py) in enumerate(peers):
            ps = 2 * px + py
            items = ((win_ref.at[:, pl.ds(ps * W_IN_SHARD, W_IN_SHARD)], r_win.at[k]),
                     (w3_ref.at[:, ps], r_w3.at[k]),
                     (cm_ref.at[:, pl.ds(ps * ROW_SHARD, ROW_SHARD)], r_cm.at[k]))
            for a, (src, dst) in enumerate(items):
                cp = pltpu.make_async_remote_copy(src_ref=src, dst_ref=dst, send_sem=send.at[a, k],
                                                  recv_sem=recv.at[a, k], device_id=(px, py, c), device_id_type=MESH)
                cp.start()
                cps.append(cp)
        for cp in cps:
            cp.wait()

    any_spec = pl.BlockSpec(memory_space=pl.ANY)
    return pl.pallas_call(
        body, name="reduce_chips",
        out_shape=(jax.ShapeDtypeStruct((3, D // 2, W_IN_SHARD), BF16), jax.ShapeDtypeStruct((3, 3, 128, D), BF16),
                   jax.ShapeDtypeStruct((3, 24, ROW_SHARD), F32)),
        in_specs=[any_spec] * 3, out_specs=(any_spec,) * 3,
        scratch_shapes=[pltpu.SemaphoreType.DMA((3, 3)), pltpu.SemaphoreType.DMA((3, 3))],
    )(cs_win, cs_w3, cs_cm)


def _add_chips(cs_win, cs_w3, cs_cm, r_win, r_w3, r_cm):
    x, y, c = _xyc()
    idx = jnp.stack([2 * x + y, c]).astype(jnp.int32)
    half = D // 2
    tr = 128

    def body1(i_ref, a_ref, b_ref, o_ref):
        f = lambda v: v.astype(F32)
        o_ref[0] = (f(a_ref[...]) + f(b_ref[2])) + (f(b_ref[0]) + f(b_ref[1]))

    f_win = pl.pallas_call(
        body1, name="add_chips_w_in", out_shape=jax.ShapeDtypeStruct((2, half, W_IN_SHARD), F32),
        grid_spec=pltpu.PrefetchScalarGridSpec(
            num_scalar_prefetch=1, grid=(half // tr,),
            in_specs=[pl.BlockSpec((tr, W_IN_SHARD), lambda i, r: (i, r[0])),
                      pl.BlockSpec((3, tr, W_IN_SHARD), lambda i, r: (0, i, 0))],
            out_specs=pl.BlockSpec((1, tr, W_IN_SHARD), lambda i, r: (r[1], i, 0))),
        compiler_params=_params(("parallel",), 32),
    )(idx, cs_win, r_win)

    def body2(i_ref, a_ref, b_ref, o_ref):
        f = lambda v: v.astype(F32)
        o_ref[0, 0] = (f(a_ref[0, 0]) + f(b_ref[2, 0])) + (f(b_ref[0, 0]) + f(b_ref[1, 0]))

    f_w3 = pl.pallas_call(
        body2, name="add_chips_w3", out_shape=jax.ShapeDtypeStruct((3, 2, 128, D), F32),
        grid_spec=pltpu.PrefetchScalarGridSpec(
            num_scalar_prefetch=1, grid=(3,),
            in_specs=[pl.BlockSpec((1, 1, 128, D), lambda w, r: (w, r[0], 0, 0)),
                      pl.BlockSpec((3, 1, 128, D), lambda w, r: (0, w, 0, 0))],
            out_specs=pl.BlockSpec((1, 1, 128, D), lambda w, r: (w, r[1], 0, 0))),
        compiler_params=_params(("parallel",), 32),
    )(idx, cs_w3, r_w3)

    def body3(i_ref, a_ref, b_ref, o_ref):
        o_ref[0] = (a_ref[...] + b_ref[2]) + (b_ref[0] + b_ref[1])

    f_cm = pl.pallas_call(
        body3, name="add_chips_cm", out_shape=jax.ShapeDtypeStruct((2, 24, ROW_SHARD), F32),
        grid_spec=pltpu.PrefetchScalarGridSpec(
            num_scalar_prefetch=1, grid=(1,),
            in_specs=[pl.BlockSpec((24, ROW_SHARD), lambda i, r: (0, r[0])),
                      pl.BlockSpec((3, 24, ROW_SHARD), lambda i, r: (0, 0, 0))],
            out_specs=pl.BlockSpec((1, 24, ROW_SHARD), lambda i, r: (r[1], 0, 0))),
        compiler_params=_params(("arbitrary",), 32),
    )(idx, cs_cm, r_cm)
    return f_win, f_w3, f_cm


def _share_sibling(f_win, f_w3, f_cm):
    def body(win_in, w3_in, cm_in, win_ref, w3_ref, cm_ref, send, recv):
        x, y, c = _xyc()
        o = 1 - c
        cps = []
        for a, (ref, sl) in enumerate(((win_ref, lambda h: win_ref.at[h]), (w3_ref, lambda h: w3_ref.at[:, h]),
                                       (cm_ref, lambda h: cm_ref.at[h]))):
            cp = pltpu.make_async_remote_copy(src_ref=sl(c), dst_ref=sl(c), send_sem=send.at[a], recv_sem=recv.at[a],
                                              device_id=(x, y, o), device_id_type=MESH)
            cp.start()
            cps.append((cp, sl))
        for a, (cp, sl) in enumerate(cps):
            pltpu.make_async_remote_copy(src_ref=sl(o), dst_ref=sl(o), send_sem=send.at[a], recv_sem=recv.at[a],
                                         device_id=(x, y, o), device_id_type=MESH).wait_recv()
            cp.wait_send()

    any_spec = pl.BlockSpec(memory_space=pl.ANY)
    return pl.pallas_call(
        body, name="share_sibling",
        out_shape=(jax.ShapeDtypeStruct(f_win.shape, F32), jax.ShapeDtypeStruct(f_w3.shape, F32),
                   jax.ShapeDtypeStruct(f_cm.shape, F32)),
        in_specs=[any_spec] * 3, out_specs=(any_spec,) * 3,
        input_output_aliases={0: 0, 1: 1, 2: 2},
        scratch_shapes=[pltpu.SemaphoreType.DMA((3,)), pltpu.SemaphoreType.DMA((3,))],
    )(f_win, f_w3, f_cm)


def _adamw_math(w, g, m, v):
    m = ADAM_B1 * m + (1.0 - ADAM_B1) * g
    v = ADAM_B2 * v + (1.0 - ADAM_B2) * (g * g)
    m_hat = m / (1.0 - ADAM_B1 ** ADAM_STEP)
    v_hat = v / (1.0 - ADAM_B2 ** ADAM_STEP)
    delta = -ADAM_LR * (m_hat / (jnp.sqrt(v_hat) + ADAM_EPS) + ADAM_WD * w)
    return delta, m, v


def _adamw(w, g, m, v, tr, name):
    rows, cols = w.shape

    def body(w_ref, g_ref, m_ref, v_ref, d_o, m_o, v_o):
        d_o[...], m_o[...], v_o[...] = _adamw_math(w_ref[...], g_ref[...], m_ref[...], v_ref[...])

    spec = pl.BlockSpec((tr, cols), lambda i: (i, 0))
    return pl.pallas_call(
        body, name=name, grid=(rows // tr,),
        out_shape=(jax.ShapeDtypeStruct((rows, cols), F32),) * 3,
        in_specs=[spec] * 4, out_specs=(spec,) * 3,
        compiler_params=_params(("parallel",), 32),
    )(w, g, m, v)


def _adamw_small(r_sm, ws, ms, vs):
    def body(s_ref, *refs):
        w_refs, m_refs, v_refs, outs = refs[0:6], refs[6:12], refs[12:18], refs[18:]
        loss_o, g_os, d_os, m_os, v_os = outs[0], outs[1:7], outs[7:13], outs[13:19], outs[19:25]
        g = s_ref[0]
        for dev in range(1, 8):
            g = g + s_ref[dev]
        qk = g[4:5, :]
        qg = qk[:, 0:HEAD_DIM]
        kg = qk[:, GROUP_W:GROUP_W + HEAD_DIM]
        for h in range(1, GQA):
            qg = qg + qk[:, HEAD_DIM * h:HEAD_DIM * (h + 1)]
            kg = kg + qk[:, GROUP_W + HEAD_DIM * h:GROUP_W + HEAD_DIM * (h + 1)]
        loss_o[...] = (0.5 / D) * jnp.sum(g[5:6, :], axis=-1, keepdims=True)
        for i, gi in enumerate((g[0:1], g[1:2], g[2:3], g[3:4], qg, kg)):
            g_os[i][...] = gi
            d_os[i][...], m_os[i][...], v_os[i][...] = _adamw_math(w_refs[i][...], gi, m_refs[i][...], v_refs[i][...])

    six = tuple(jax.ShapeDtypeStruct(w.shape, F32) for w in ws)
    return pl.pallas_call(
        body, name="adamw_small", out_shape=(jax.ShapeDtypeStruct((1, 1), F32),) + six * 4,
    )(r_sm, *ws, *ms, *vs)


def kernel(x, meta_tokens, norm_g, w_in, conv_w, conv_b, conv_norm_g, conv_norm_b, w_conv_out, q_norm_g, k_norm_g, w_attn_out, w_out, loss_target, m_meta_tokens, m_norm_g, m_w_in, m_conv_w, m_conv_b, m_conv_norm_g, m_conv_norm_b, m_w_conv_out, m_q_norm_g, m_k_norm_g, m_w_attn_out, m_w_out, v_meta_tokens, v_norm_g, v_w_in, v_conv_w, v_conv_b, v_conv_norm_g, v_conv_norm_b, v_w_conv_out, v_q_norm_g, v_k_norm_g, v_w_attn_out, v_w_out):
    pad_k = lambda a: jnp.pad(a[0], ((0, 32 - CONV_K), (0, 0)))
    w3_s = jnp.concatenate([w_conv_out, w_attn_out, w_out], axis=0)
    w_full, w3_full, conv_w_full, meta_full = _gather_weights(w_in[0], w3_s, pad_k(conv_w), meta_tokens)

    gx, gwin, gw3, gcw, gmeta, smalls = _local_step(
        x, loss_target, norm_g, conv_b, conv_norm_g, conv_norm_b, q_norm_g, k_norm_g,
        w_full, w3_full, conv_w_full, meta_full)

    gwin2 = gwin.reshape(D, IN_DIM)
    gw3v = gw3.reshape(3, N_CHIPS, 2, 128, D)
    gcm = jnp.concatenate([gcw.reshape(2, 16, D), gmeta.reshape(2, 8, D)], axis=1)
    r_win, r_w3, r_cm, r_sm = _reduce_sibling(gwin2, gw3v, gcm, smalls)
    cs_win, cs_w3, cs_cm = _add_sibling(gwin2, gw3v, gcm, r_win, r_w3, r_cm)
    r2_win, r2_w3, r2_cm = _reduce_chips(cs_win, cs_w3, cs_cm)
    f_win, f_w3, f_cm = _add_chips(cs_win, cs_w3, cs_cm, r2_win, r2_w3, r2_cm)
    f_win, f_w3, f_cm = _share_sibling(f_win, f_w3, f_cm)

    g_w_in = f_win.reshape(D, W_IN_SHARD)
    g_w3 = f_w3.reshape(3, ROW_SHARD, D)
    g_conv_w = f_cm[:, 0:16].reshape(32, ROW_SHARD)
    g_meta = f_cm[:, 16:24].reshape(N_META, ROW_SHARD)

    d_w_in, nm_w_in, nv_w_in = _adamw(w_in[0], g_w_in, m_w_in[0], v_w_in[0], 128, "adamw_w_in")
    m3 = jnp.concatenate([m_w_conv_out, m_w_attn_out, m_w_out], axis=0).reshape(3 * ROW_SHARD, D)
    v3 = jnp.concatenate([v_w_conv_out, v_w_attn_out, v_w_out], axis=0).reshape(3 * ROW_SHARD, D)
    d_w3, nm_w3, nv_w3 = _adamw(w3_s.reshape(3 * ROW_SHARD, D), g_w3.reshape(3 * ROW_SHARD, D), m3, v3, 256, "adamw_w3")
    cm_w = jnp.concatenate([pad_k(conv_w), meta_tokens], axis=0)
    cm_m = jnp.concatenate([pad_k(m_conv_w), m_meta_tokens], axis=0)
    cm_v = jnp.concatenate([pad_k(v_conv_w), v_meta_tokens], axis=0)
    cm_g = jnp.concatenate([g_conv_w, g_meta], axis=0)
    d_cm, nm_cm, nv_cm = _adamw(cm_w, cm_g, cm_m, cm_v, 48, "adamw_cm")
    small = _adamw_small(
        r_sm, (norm_g, conv_b, conv_norm_g, conv_norm_b, q_norm_g, k_norm_g),
        (m_norm_g, m_conv_b, m_conv_norm_g, m_conv_norm_b, m_q_norm_g, m_k_norm_g),
        (v_norm_g, v_conv_b, v_conv_norm_g, v_conv_norm_b, v_q_norm_g, v_k_norm_g))

    def assemble(big_in, w3x, cmx, s6):
        w3x = w3x.reshape(3, 1, ROW_SHARD, D)
        ng, cb, cng, cnb, qg, kg = s6
        return (cmx[32:48], ng, big_in[None], cmx[None, 0:CONV_K], cb, cng, cnb, w3x[0], qg, kg, w3x[1], w3x[2])

    loss = small[0].reshape(())
    grads = assemble(g_w_in, g_w3, cm_g, small[1:7])
    deltas = assemble(d_w_in, d_w3, d_cm, small[7:13])
    new_m = assemble(nm_w_in, nm_w3, nm_cm, small[13:19])
    new_v = assemble(nv_w_in, nv_w3, nv_cm, small[19:25])
    return (loss, gx, *grads, *deltas, *new_m, *new_v)
```

```python
import functools
import math

import jax
import jax.numpy as jnp
from jax import lax
from jax.experimental import pallas as pl
from jax.experimental.pallas import tpu as pltpu

F32, BF16 = jnp.float32, jnp.bfloat16
MESH = pl.DeviceIdType.MESH

D = 1024
N_META = 16
CONV_K = 31
N_KV = 4
GQA = 4
HEAD_DIM = 64
GROUP_W = GQA * HEAD_DIM
GRID_W = 64
ROPE_FREQS = 16
ROPE_THETA = 10000.0
EPS = 1e-6
IN_DIM = 7680
KEY_PAD = 128
G_CONV, G_CZ, G_Q, G_KV, G_E = (0, 2048), (2048, 1024), (3072, 1024), (4096, 512), (4608, 3072)
N_CHIPS = 4
W_IN_SHARD = IN_DIM // N_CHIPS
ROW_SHARD = D // N_CHIPS

ADAM_LR, ADAM_B1, ADAM_B2, ADAM_EPS, ADAM_WD, ADAM_STEP = 0.001, 0.9, 0.999, 1e-08, 0.01, 10

NT_DIMS = (((1,), (1,)), ((), ()))


def _params(sem=None, vmem_mb=48):
    return pltpu.CompilerParams(dimension_semantics=sem, vmem_limit_bytes=vmem_mb << 20)


def _sig(v):
    return jax.nn.sigmoid(v)


def _dsilu(v, s):
    return s * (1.0 + v * (1.0 - s))


def _dot(a, b):
    return jnp.dot(a, b, preferred_element_type=F32)


def _dot_nt(a, b):
    return lax.dot_general(a, b, NT_DIMS, preferred_element_type=F32)


def _qk_mats():
    i = lax.broadcasted_iota(jnp.int32, (GROUP_W, GROUP_W), 0)
    j = lax.broadcasted_iota(jnp.int32, (GROUP_W, GROUP_W), 1)
    mean = jnp.where((i >> 6) == (j >> 6), 1.0 / HEAD_DIM, 0.0).astype(BF16)
    turn = jnp.where((i == j + 16) & ((j & 16) == 0), -1.0,
                     jnp.where((i == j - 16) & ((j & 16) != 0), 1.0, 0.0)).astype(BF16)
    return mean, turn


def _apply(v, mat):
    hi = v.astype(BF16)
    lo = (v - hi.astype(F32)).astype(BF16)
    return _dot(hi, mat) + _dot(lo, mat)


def _qk_fwd(v, g, cos, sin, mats):
    mean, turn = mats
    r = lax.rsqrt(_apply(v * v, mean) + EPS)
    n = v * r * g
    return n * cos + _apply(n, turn) * sin, r


def _qk_bwd(dy, v, r, g, cos, sin, mats):
    mean, turn = mats
    dn = dy * cos - _apply(dy, turn) * sin
    dyg = dn * g
    dv = r * dyg - v * (r * r * r) * _apply(dyg * v, mean)
    return dv, dn * v * r


def _rms_bwd(dxn, v, r, g):
    dxg = dxn * g
    return r * dxg - v * (r * r * r) * jnp.mean(dxg * v, axis=-1, keepdims=True)


def _glu(a):
    return a[:, :D] * _sig(a[:, D:])


def _gather_weights(w_in_s, w3_s, conv_w_s, meta_s):
    def body(win_ref, w3_ref, cw_ref, mt_ref, win_o, w3_o, cw_o, mt_o, win_b, w3_b, send, recv, fsend, frecv, lsem):
        x, y, c = _xyc()
        o = 1 - c
        me = 2 * x + y
        win_b[...] = win_ref[...].astype(BF16)
        w3_b[...] = w3_ref[...].astype(BF16)
        items = (
            (lambda h: win_b.at[pl.ds(h * 512, 512), :],
             lambda p, h: win_o.at[pl.ds(h * 512, 512), pl.ds(p * W_IN_SHARD, W_IN_SHARD)]),
            (lambda h: w3_b.at[:, pl.ds(h * 128, 128), :],
             lambda p, h: w3_o.at[:, pl.ds(p * ROW_SHARD + h * 128, 128), :]),
            (lambda h: cw_ref.at[pl.ds(h * 16, 16), :],
             lambda p, h: cw_o.at[pl.ds(h * 16, 16), pl.ds(p * ROW_SHARD, ROW_SHARD)]),
            (lambda h: mt_ref.at[pl.ds(h * 8, 8), :],
             lambda p, h: mt_o.at[pl.ds(h * 8, 8), pl.ds(p * ROW_SHARD, ROW_SHARD)]),
        )
        peers = ((1 - x, y), (x, 1 - y), (1 - x, 1 - y))

        def remote(src, dst, s_sem, r_sem, to):
            return pltpu.make_async_remote_copy(src_ref=src, dst_ref=dst, send_sem=s_sem, recv_sem=r_sem,
                                                device_id=to, device_id_type=MESH)

        started = []
        for a, (half, place) in enumerate(items):
            for h in range(2):
                loc = pltpu.make_async_copy(half(h), place(me, h), lsem.at[a, h])
                loc.start()
                started.append(loc.wait)
            for k, (px, py) in enumerate(peers):
                cp = remote(half(c), place(me, c), send.at[a, k], recv.at[a, k], (px, py, c))
                cp.start()
                started.append(cp.wait_send)
        for k, (px, py) in enumerate(peers):
            for a, (half, place) in enumerate(items):
                got = place(2 * px + py, c)
                remote(got, got, send.at[a, k], recv.at[a, k], (px, py, c)).wait_recv()
                fw = remote(got, got, fsend.at[a, k], frecv.at[a, k], (x, y, o))
                fw.start()
                started.append(fw.wait_send)
        for k, (px, py) in enumerate(peers):
            for a, (half, place) in enumerate(items):
                theirs = place(2 * px + py, o)
                remote(theirs, theirs, fsend.at[a, k], frecv.at[a, k], (x, y, o)).wait_recv()
        for wait in started:
            wait()

    any_spec = pl.BlockSpec(memory_space=pl.ANY)
    vmem = pl.BlockSpec(memory_space=pltpu.VMEM)
    return pl.pallas_call(
        body, name="gather_weights",
        out_shape=(jax.ShapeDtypeStruct((D, IN_DIM), BF16), jax.ShapeDtypeStruct((3, D, D), BF16),
                   jax.ShapeDtypeStruct((32, D), F32), jax.ShapeDtypeStruct((N_META, D), F32)),
        in_specs=[vmem, vmem, vmem, vmem],
        out_specs=(any_spec, any_spec, any_spec, any_spec),
        scratch_shapes=[pltpu.VMEM((D, W_IN_SHARD), BF16), pltpu.VMEM((3, ROW_SHARD, D), BF16),
                        pltpu.SemaphoreType.DMA((4, 3)), pltpu.SemaphoreType.DMA((4, 3)),
                        pltpu.SemaphoreType.DMA((4, 3)), pltpu.SemaphoreType.DMA((4, 3)),
                        pltpu.SemaphoreType.DMA((4, 2))],
        compiler_params=pltpu.CompilerParams(vmem_limit_bytes=40 << 20),
    )(w_in_s, w3_s, conv_w_s, meta_s)


def _meta_fwd(meta_full, norm_g, w_full):
    def body(m_ref, g_ref, wc_ref, wkv_ref, xnt_ref, pc_ref, pkv_ref):
        v = m_ref[...]
        r = lax.rsqrt(jnp.mean(v * v, axis=-1, keepdims=True) + EPS)
        xn = v * r * g_ref[...]
        xnb = xn.astype(BF16)
        pad = jnp.concatenate([xn, jnp.zeros((128 - N_META, D), F32)], axis=0)
        xnt_ref[...] = pad.T.astype(BF16)
        pc_ref[...] = _dot(xnb, wc_ref[...])
        pkv_ref[...] = _dot(xnb, wkv_ref[...])

    return pl.pallas_call(
        body, name="meta_fwd", grid=(1,),
        out_shape=(jax.ShapeDtypeStruct((D, 128), BF16), jax.ShapeDtypeStruct((N_META, 2048), F32),
                   jax.ShapeDtypeStruct((N_META, 512), F32)),
        in_specs=[pl.BlockSpec((N_META, D), lambda i: (0, 0)), pl.BlockSpec((1, D), lambda i: (0, 0)),
                  pl.BlockSpec((D, 2048), lambda i: (0, 0)), pl.BlockSpec((D, 512), lambda i: (0, G_KV[0] // 512))],
        out_specs=(pl.BlockSpec((D, 128), lambda i: (0, 0)), pl.BlockSpec((N_META, 2048), lambda i: (0, 0)),
                   pl.BlockSpec((N_META, 512), lambda i: (0, 0))),
        compiler_params=_params(("arbitrary",), 32),
    )(meta_full, norm_g, w_full, w_full)


def _in_proj(x2, norm_g, w_full, tm):
    rows = x2.shape[0]
    groups = (G_CONV, G_CZ, G_Q, G_KV, G_E)

    def body(x_ref, g_ref, w_hbm, *rest):
        outs, xnt_ref, w_vmem, sem = rest[:5], rest[5], rest[6], rest[7]

        @pl.when(pl.program_id(0) == 0)
        def _():
            cp = pltpu.make_async_copy(w_hbm, w_vmem, sem)
            cp.start()
            cp.wait()

        v = x_ref[...]
        r = lax.rsqrt(jnp.mean(v * v, axis=-1, keepdims=True) + EPS)
        xn = v * r * g_ref[...]
        xnb = xn.astype(BF16)
        xnt_ref[...] = xn.T.astype(BF16)
        for ref, (off, wd) in zip(outs, groups):
            for c0 in range(0, wd, 512):
                ref[:, c0:c0 + 512] = _dot(xnb, w_vmem[:, off + c0:off + c0 + 512])

    return pl.pallas_call(
        body, name="in_proj", grid=(rows // tm,),
        out_shape=tuple(jax.ShapeDtypeStruct((rows, wd), F32) for _, wd in groups)
        + (jax.ShapeDtypeStruct((D, rows), BF16),),
        in_specs=[pl.BlockSpec((tm, D), lambda i: (i, 0)), pl.BlockSpec((1, D), lambda i: (0, 0)),
                  pl.BlockSpec(memory_space=pl.ANY)],
        out_specs=tuple(pl.BlockSpec((tm, wd), lambda i: (i, 0)) for _, wd in groups)
        + (pl.BlockSpec((D, tm), lambda i: (0, i)),),
        scratch_shapes=[pltpu.VMEM((D, IN_DIM), BF16), pltpu.SemaphoreType.DMA],
        compiler_params=_params(("arbitrary",), 56),
    )(x2, norm_g, w_full)


def _halo_specs(width, tm, nt, rows):
    h16 = tm // 16
    return [pl.BlockSpec((tm, width), lambda b, i: (b * nt + i, 0)),
            pl.BlockSpec((16, width), lambda b, i: (jnp.maximum((b * nt + i) * h16 - 1, 0), 0)),
            pl.BlockSpec((16, width), lambda b, i: (jnp.minimum((b * nt + i + 1) * h16, rows // 16 - 1), 0))]


def _fill_uext(uext, cur, prev, nxt, meta, i, nt, tm):
    uext[0:16] = jnp.where(i == 0, _glu(meta[...]), _glu(prev[...]))
    uext[16:16 + tm] = _glu(cur[...])
    uext[16 + tm:32 + tm] = jnp.where(i == nt - 1, 0.0, _glu(nxt[...]))


def _shifted_copies(dst, src, n):
    for r in range(1, 8):
        dst[r, 0:n] = src[r:r + n]


def _rows32(shifted, src, start, cols):
    q8, r = divmod(start, 8)
    if r == 0:
        return src[start:start + 32, cols]
    return shifted[r, 8 * q8:8 * q8 + 32, cols]


def _conv_fwd(pconv, pm_conv, conv_w, conv_b, nb, tm):
    rows = pconv.shape[0]
    nt = rows // nb // tm

    def body(cur, prev, nxt, meta, w_ref, b_ref, o_ref, uext, ush):
        i = pl.program_id(1)
        _fill_uext(uext, cur, prev, nxt, meta, i, nt, tm)
        _shifted_copies(ush, uext, tm + 24)
        for r0 in range(0, tm, 32):
            for c0 in range(0, D, 256):
                acc = jnp.zeros((32, 256), F32) + b_ref[:, c0:c0 + 256]
                for j in range(CONV_K):
                    acc = acc + _rows32(ush, uext, r0 + j + 1, slice(c0, c0 + 256)) * w_ref[j:j + 1, c0:c0 + 256]
                o_ref[r0:r0 + 32, c0:c0 + 256] = acc

    return pl.pallas_call(
        body, name="conv_fwd", grid=(nb, nt),
        out_shape=jax.ShapeDtypeStruct((rows, D), F32),
        in_specs=_halo_specs(2048, tm, nt, rows)
        + [pl.BlockSpec((16, 2048), lambda b, i: (0, 0)), pl.BlockSpec((32, D), lambda b, i: (0, 0)),
           pl.BlockSpec((1, D), lambda b, i: (0, 0))],
        out_specs=pl.BlockSpec((tm, D), lambda b, i: (b * nt + i, 0)),
        scratch_shapes=[pltpu.VMEM((tm + 32, D), F32), pltpu.VMEM((8, tm + 24, D), F32)],
        compiler_params=_params(("parallel", "parallel"), 40),
    )(pconv, pconv, pconv, pm_conv, conv_w, conv_b)


def _kv_prep(pkv, pm_kv, kg, cos, sin, nb):
    rows = pkv.shape[0]
    s_len = rows // nb
    tk = 128
    nt = s_len // tk

    def body(kv_ref, m_ref, g_ref, cos_ref, sin_ref, k_o, v_o):
        i = pl.program_id(1)
        mats = _qk_mats()

        @pl.when(i < nt)
        def _():
            kv = kv_ref[...]
            kr, _ = _qk_fwd(kv[:, :GROUP_W], g_ref[...], cos_ref[...], sin_ref[...], mats)
            ones = _ones_cols(tk, tk)
            for h in range(N_KV):
                k_o[0, h] = kr[:, HEAD_DIM * h:HEAD_DIM * (h + 1)].astype(BF16)
                vh = kv[:, GROUP_W + HEAD_DIM * h:GROUP_W + HEAD_DIM * (h + 1)]
                v_o[0, h] = jnp.concatenate([vh, ones], axis=1).astype(BF16)

        @pl.when(i == nt)
        def _():
            kv = m_ref[...]
            km = kv[:, :GROUP_W]
            kn = km * lax.rsqrt(_apply(km * km, mats[0]) + EPS) * g_ref[...]
            zeros = jnp.zeros((tk - N_META, GROUP_W), F32)
            kfull = jnp.concatenate([kn, zeros], axis=0)
            vfull = jnp.concatenate([kv[:, GROUP_W:], zeros], axis=0)
            ones = _ones_cols(tk, N_META)
            for h in range(N_KV):
                k_o[0, h] = kfull[:, HEAD_DIM * h:HEAD_DIM * (h + 1)].astype(BF16)
                v_o[0, h] = jnp.concatenate([vfull[:, HEAD_DIM * h:HEAD_DIM * (h + 1)], ones], axis=1).astype(BF16)

    lk = s_len + KEY_PAD
    last = nt - 1
    return pl.pallas_call(
        body, name="kv_prep", grid=(nb, nt + 1),
        out_shape=(jax.ShapeDtypeStruct((nb, N_KV, lk, HEAD_DIM), BF16),
                   jax.ShapeDtypeStruct((nb, N_KV, lk, 2 * HEAD_DIM), BF16)),
        in_specs=[pl.BlockSpec((tk, 512), lambda b, i: (b * nt + jnp.minimum(i, last), 0)),
                  pl.BlockSpec((N_META, 512), lambda b, i: (0, 0)), pl.BlockSpec((1, GROUP_W), lambda b, i: (0, 0)),
                  pl.BlockSpec((tk, GROUP_W), lambda b, i: (jnp.minimum(i, last), 0)),
                  pl.BlockSpec((tk, GROUP_W), lambda b, i: (jnp.minimum(i, last), 0))],
        out_specs=(pl.BlockSpec((1, N_KV, tk, HEAD_DIM), lambda b, i: (b, 0, i, 0)),
                   pl.BlockSpec((1, N_KV, tk, 2 * HEAD_DIM), lambda b, i: (b, 0, i, 0))),
        compiler_params=_params(("parallel", "arbitrary"), 32),
    )(pkv, pm_kv, kg, cos, sin)


def _ones_cols(rows, valid):
    r = lax.broadcasted_iota(jnp.int32, (rows, HEAD_DIM), 0)
    col = lax.broadcasted_iota(jnp.int32, (rows, HEAD_DIM), 1)
    return jnp.where((col < 2) & (r < valid), 1.0, 0.0).astype(F32)


def _tail_bias():
    col = lax.broadcasted_iota(jnp.int32, (1, KEY_PAD), 1)
    return jnp.where(col < N_META, 0.0, -1e30).astype(F32)


LOG2E = 1.4426950408889634


def _q_prep(pq, qg, cos, sin, nb, tm):
    rows = pq.shape[0]
    nt = rows // nb // tm
    scale = 1.0 / math.sqrt(HEAD_DIM)

    def body(q_ref, g_ref, cos_ref, sin_ref, q2_o, qt_o):
        gv, cosv, sinv = g_ref[...], cos_ref[...], sin_ref[...]
        mats = _qk_mats()
        for g in range(N_KV):
            gs = slice(GROUP_W * g, GROUP_W * (g + 1))
            qr, _ = _qk_fwd(q_ref[:, gs], gv, cosv, sinv, mats)
            q2_o[:, gs] = (qr * (scale * LOG2E)).astype(BF16)
            qt_o[gs, :] = (qr * scale).T.astype(BF16)

    row = pl.BlockSpec((tm, D), lambda b, i: (b * nt + i, 0))
    rope = pl.BlockSpec((tm, GROUP_W), lambda b, i: (i, 0))
    return pl.pallas_call(
        body, name="q_prep", grid=(nb, nt),
        out_shape=(jax.ShapeDtypeStruct((rows, D), BF16), jax.ShapeDtypeStruct((D, rows), BF16)),
        in_specs=[row, pl.BlockSpec((1, GROUP_W), lambda b, i: (0, 0)), rope, rope],
        out_specs=(row, pl.BlockSpec((D, tm), lambda b, i: (0, b * nt + i))),
        compiler_params=_params(("parallel", "parallel"), 32),
    )(pq, qg, cos, sin)


def _attn_fwd(q2, kr, ve, nb, tq):
    rows = q2.shape[0]
    s_len = rows // nb
    nq = s_len // tq
    lk = s_len + KEY_PAD

    def body(q_ref, k_ref, v_ref, o_ref, lse_ref):
        qs = q_ref[...]
        k1, k2 = k_ref[0, 0, 0:s_len, :], k_ref[0, 0, s_len:lk, :]
        v1, v2 = v_ref[0, 0, 0:s_len, :], v_ref[0, 0, s_len:lk, :]
        bias = _tail_bias()
        outs, lses = [], []

        def scores(h):
            qh = qs[:, HEAD_DIM * h:HEAD_DIM * (h + 1)]
            return _dot_nt(qh, k1), _dot_nt(qh, k2) + bias

        ahead = scores(0)
        for h in range(GQA):
            s1, s2 = ahead
            if h + 1 < GQA:
                ahead = scores(h + 1)
            m = jnp.maximum(jnp.max(s1, axis=-1, keepdims=True), jnp.max(s2, axis=-1, keepdims=True))
            oe = _dot(jnp.exp2(s1 - m).astype(BF16), v1) + _dot(jnp.exp2(s2 - m).astype(BF16), v2)
            l = oe[:, HEAD_DIM:HEAD_DIM + 1]
            outs.append(oe[:, :HEAD_DIM] / l)
            lses.append(m + jnp.log2(l))
        o_ref[...] = jnp.concatenate(outs, axis=1)
        lse_ref[0, 0] = jnp.concatenate(lses, axis=1)

    return pl.pallas_call(
        body, name="attn_fwd", grid=(nb, N_KV, nq),
        out_shape=(jax.ShapeDtypeStruct((rows, D), F32), jax.ShapeDtypeStruct((nb, N_KV, s_len, GQA), F32)),
        in_specs=[pl.BlockSpec((tq, GROUP_W), lambda b, g, i: (b * nq + i, g)),
                  pl.BlockSpec((1, 1, lk, HEAD_DIM), lambda b, g, i: (b, g, 0, 0)),
                  pl.BlockSpec((1, 1, lk, 2 * HEAD_DIM), lambda b, g, i: (b, g, 0, 0))],
        out_specs=(pl.BlockSpec((tq, GROUP_W), lambda b, g, i: (b * nq + i, g)),
                   pl.BlockSpec((1, 1, tq, GQA), lambda b, g, i: (b, g, i, 0))),
        compiler_params=_params(("parallel", "parallel", "parallel"), 48),
    )(q2, kr, ve)


def _mid(x2, t2, c0, cz, o, e, w3, cn_g, cn_b, tm):
    rows = x2.shape[0]

    def body(x_ref, t_ref, c0_ref, cz_ref, o_ref, e_ref, w_ref, g_ref, b_ref,
             dy_o, mt_o, c3t_o, o2t_o, dyc_o, dya_o, do_o, dc0_o, dcz_o, de_o, sums_o):
        wco, wao, wo = w_ref[0], w_ref[1], w_ref[2]
        cn_g_v = g_ref[...]
        c0v = c0_ref[...]
        xc = c0v - jnp.mean(c0v, axis=-1, keepdims=True)
        rstd = lax.rsqrt(jnp.mean(xc * xc, axis=-1, keepdims=True) + EPS)
        n = xc * rstd
        c1 = n * cn_g_v + b_ref[...]
        s1 = _sig(c1)
        c2 = c1 * s1
        czv = cz_ref[...]
        sz = _sig(czv)
        gz = czv * sz
        c3 = c2 * gz
        yc = _dot(c3.astype(BF16), wco)
        az, gc, ga = e_ref[:, :D], e_ref[:, D:2 * D], e_ref[:, 2 * D:]
        saz = _sig(az)
        gaz = az * saz
        ov = o_ref[...]
        o2 = ov * gaz
        ya = _dot(o2.astype(BF16), wao)
        sc, sa = _sig(gc), _sig(ga)
        merged = sc * yc + sa * ya
        out = _dot(merged.astype(BF16), wo)
        err = x_ref[...] + out - t_ref[...]
        dy = err * (1.0 / D)
        dy_o[...] = dy
        dm = _dot_nt(dy.astype(BF16), wo)
        dyc = dm * sc
        dya = dm * sa
        dycb, dyab = dyc.astype(BF16), dya.astype(BF16)
        dyc_o[...] = dycb
        dya_o[...] = dyab
        de_o[:, D:2 * D] = (dyc * yc * (1.0 - sc)).astype(BF16)
        de_o[:, 2 * D:] = (dya * ya * (1.0 - sa)).astype(BF16)
        dc3 = _dot_nt(dycb, wco)
        do2 = _dot_nt(dyab, wao)
        do_o[...] = do2 * gaz
        de_o[:, :D] = (do2 * ov * _dsilu(az, saz)).astype(BF16)
        dcz_o[...] = (dc3 * c2 * _dsilu(czv, sz)).astype(BF16)
        dc1 = dc3 * gz * _dsilu(c1, s1)
        dn = dc1 * cn_g_v
        dc0 = rstd * (dn - jnp.mean(dn, axis=-1, keepdims=True) - n * jnp.mean(dn * n, axis=-1, keepdims=True))
        dc0_o[...] = dc0
        mt_o[...] = merged.T.astype(BF16)
        c3t_o[...] = c3.T.astype(BF16)
        o2t_o[...] = o2.T.astype(BF16)

        @pl.when(pl.program_id(0) == 0)
        def _():
            sums_o[...] = jnp.zeros_like(sums_o)

        sums_o[0:1, :] += jnp.sum(dc1 * n, axis=0, keepdims=True)
        sums_o[1:2, :] += jnp.sum(dc1, axis=0, keepdims=True)
        sums_o[2:3, :] += jnp.sum(dc0, axis=0, keepdims=True)
        sums_o[3:4, :] += jnp.sum(err * err, axis=0, keepdims=True)

    row = lambda wd: pl.BlockSpec((tm, wd), lambda i: (i, 0))
    col = pl.BlockSpec((D, tm), lambda i: (0, i))
    vec = pl.BlockSpec((1, D), lambda i: (0, 0))
    f32o = lambda wd: jax.ShapeDtypeStruct((rows, wd), F32)
    b16o = lambda wd: jax.ShapeDtypeStruct((rows, wd), BF16)
    tpo = jax.ShapeDtypeStruct((D, rows), BF16)
    return pl.pallas_call(
        body, name="mid", grid=(rows // tm,),
        out_shape=(f32o(D), tpo, tpo, tpo, b16o(D), b16o(D), f32o(D), f32o(D), b16o(D), b16o(3 * D),
                   jax.ShapeDtypeStruct((8, D), F32)),
        in_specs=[row(D), row(D), row(D), row(D), row(D), row(3 * D),
                  pl.BlockSpec((3, D, D), lambda i: (0, 0, 0)), vec, vec],
        out_specs=(row(D), col, col, col, row(D), row(D), row(D), row(D), row(D), row(3 * D),
                   pl.BlockSpec((8, D), lambda i: (0, 0))),
        compiler_params=_params(("arbitrary",), 56),
    )(x2, t2, c0, cz, o, e, w3, cn_g, cn_b)


def _do_prep(d_o, o, tm):
    rows = d_o.shape[0]

    def body(do_ref, o_ref, doe_o, dot_o):
        dov = do_ref[...]
        prod = dov * o_ref[...]
        col = lax.broadcasted_iota(jnp.int32, (tm, HEAD_DIM), 1)
        for h in range(D // HEAD_DIM):
            hs = slice(HEAD_DIM * h, HEAD_DIM * (h + 1))
            delta = jnp.sum(prod[:, hs], axis=-1, keepdims=True)
            d_hi = delta.astype(BF16).astype(F32)
            tail = jnp.where(col == 0, -d_hi, jnp.where(col == 1, d_hi - delta, 0.0))
            doe_o[:, 2 * HEAD_DIM * h:2 * HEAD_DIM * (h + 1)] = jnp.concatenate([dov[:, hs], tail], axis=1).astype(BF16)
        dot_o[...] = dov.T.astype(BF16)

    row = pl.BlockSpec((tm, D), lambda i: (i, 0))
    return pl.pallas_call(
        body, name="do_prep", grid=(rows // tm,),
        out_shape=(jax.ShapeDtypeStruct((rows, 2 * D), BF16), jax.ShapeDtypeStruct((D, rows), BF16)),
        in_specs=[row, row],
        out_specs=(pl.BlockSpec((tm, 2 * D), lambda i: (i, 0)), pl.BlockSpec((D, tm), lambda i: (0, i))),
        compiler_params=_params(("parallel",), 32),
    )(d_o, o)


def _q_post(dqr, pq, qg, cos, sin, nb, tm):
    rows = pq.shape[0]
    nt = rows // nb // tm

    def body(dq_ref, q_ref, g_ref, cos_ref, sin_ref, dq_o, dg_o):
        @pl.when((pl.program_id(0) == 0) & (pl.program_id(1) == 0))
        def _():
            dg_o[...] = jnp.zeros_like(dg_o)

        gv, cosv, sinv = g_ref[...], cos_ref[...], sin_ref[...]
        acc = jnp.zeros((1, GROUP_W), F32)
        mats = _qk_mats()
        for g in range(N_KV):
            gs = slice(GROUP_W * g, GROUP_W * (g + 1))
            qv = q_ref[:, gs]
            r = lax.rsqrt(_apply(qv * qv, mats[0]) + EPS)
            dq, dgr = _qk_bwd(dq_ref[:, gs], qv, r, gv, cosv, sinv, mats)
            dq_o[:, gs] = dq.astype(BF16)
            acc = acc + jnp.sum(dgr, axis=0, keepdims=True)
        dg_o[...] += acc

    row = pl.BlockSpec((tm, D), lambda b, i: (b * nt + i, 0))
    rope = pl.BlockSpec((tm, GROUP_W), lambda b, i: (i, 0))
    vec = pl.BlockSpec((1, GROUP_W), lambda b, i: (0, 0))
    return pl.pallas_call(
        body, name="q_post", grid=(nb, nt),
        out_shape=(jax.ShapeDtypeStruct((rows, D), BF16), jax.ShapeDtypeStruct((1, GROUP_W), F32)),
        in_specs=[row, row, vec, rope, rope], out_specs=(row, vec),
        compiler_params=_params(("arbitrary", "arbitrary"), 32),
    )(dqr, pq, qg, cos, sin)


def _attn_bwd(q2, qst, kr, ve, doe, dot_, lse, nb, tq):
    rows = q2.shape[0]
    s_len = rows // nb
    nq = s_len // tq
    lk = s_len + KEY_PAD
    scale = 1.0 / math.sqrt(HEAD_DIM)

    def body(q_ref, qt_ref, k_ref, v_ref, doe_ref, dot_ref, lse_ref, dq_o, dkt_o, dvt_o):
        i = pl.program_id(2)
        lse = lse_ref[0, 0]
        k1, k2 = k_ref[0, 0, 0:s_len, :], k_ref[0, 0, s_len:lk, :]
        v1, v2 = v_ref[0, 0, 0:s_len, :], v_ref[0, 0, s_len:lk, :]
        bias = _tail_bias()
        dkt1, dkt2 = jnp.zeros((HEAD_DIM, s_len), F32), jnp.zeros((HEAD_DIM, KEY_PAD), F32)
        dvt1, dvt2 = jnp.zeros((HEAD_DIM, s_len), F32), jnp.zeros((HEAD_DIM, KEY_PAD), F32)

        def products(h):
            qh = q_ref[:, HEAD_DIM * h:HEAD_DIM * (h + 1)]
            dh = doe_ref[:, 2 * HEAD_DIM * h:2 * HEAD_DIM * (h + 1)]
            return _dot_nt(qh, k1), _dot_nt(qh, k2) + bias, _dot_nt(dh, v1), _dot_nt(dh, v2)

        ahead = products(0)
        for h in range(GQA):
            hs = slice(HEAD_DIM * h, HEAD_DIM * (h + 1))
            s1, s2, dp1, dp2 = ahead
            if h + 1 < GQA:
                ahead = products(h + 1)
            lse_h = lse[:, h:h + 1]
            p1 = jnp.exp2(s1 - lse_h)
            p2 = jnp.exp2(s2 - lse_h)
            ds1 = (p1 * dp1).astype(BF16)
            ds2 = (p2 * dp2).astype(BF16)
            dq_o[:, hs] = (_dot(ds1, k1) + _dot(ds2, k2)) * scale
            dkt1 = dkt1 + _dot(qt_ref[hs, :], ds1)
            dkt2 = dkt2 + _dot(qt_ref[hs, :], ds2)
            dvt1 = dvt1 + _dot(dot_ref[hs, :], p1.astype(BF16))
            dvt2 = dvt2 + _dot(dot_ref[hs, :], p2.astype(BF16))

        @pl.when(i == 0)
        def _():
            dkt_o[0, 0, :, 0:s_len] = dkt1
            dkt_o[0, 0, :, s_len:lk] = dkt2
            dvt_o[0, 0, :, 0:s_len] = dvt1
            dvt_o[0, 0, :, s_len:lk] = dvt2

        @pl.when(i > 0)
        def _():
            dkt_o[0, 0, :, 0:s_len] += dkt1
            dkt_o[0, 0, :, s_len:lk] += dkt2
            dvt_o[0, 0, :, 0:s_len] += dvt1
            dvt_o[0, 0, :, s_len:lk] += dvt2

    qspec = pl.BlockSpec((tq, GROUP_W), lambda b, g, i: (b * nq + i, g))
    qtspec = pl.BlockSpec((GROUP_W, tq), lambda b, g, i: (g, b * nq + i))
    kspec = pl.BlockSpec((1, 1, lk, HEAD_DIM), lambda b, g, i: (b, g, 0, 0))
    tspec = pl.BlockSpec((1, 1, HEAD_DIM, lk), lambda b, g, i: (b, g, 0, 0))
    return pl.pallas_call(
        body, name="attn_bwd", grid=(nb, N_KV, nq),
        out_shape=(jax.ShapeDtypeStruct((rows, D), F32), jax.ShapeDtypeStruct((nb, N_KV, HEAD_DIM, lk), F32),
                   jax.ShapeDtypeStruct((nb, N_KV, HEAD_DIM, lk), F32)),
        in_specs=[qspec, qtspec, kspec, pl.BlockSpec((1, 1, lk, 2 * HEAD_DIM), lambda b, g, i: (b, g, 0, 0)),
                  pl.BlockSpec((tq, 2 * GROUP_W), lambda b, g, i: (b * nq + i, g)), qtspec,
                  pl.BlockSpec((1, 1, tq, GQA), lambda b, g, i: (b, g, i, 0))],
        out_specs=(qspec, tspec, tspec),
        compiler_params=_params(("parallel", "parallel", "arbitrary"), 56),
    )(q2, qst, kr, ve, doe, dot_, lse)


def _kv_bwd(dkt, dvt, pkv, pm_kv, kg, cos, sin, nb):
    rows = pkv.shape[0]
    s_len = rows // nb
    tk = 128
    nt = s_len // tk
    last = nt - 1

    def body(dk_ref, dv_ref, kv_ref, m_ref, g_ref, cos_ref, sin_ref, d_o, dm_o, dg_o):
        b, i = pl.program_id(0), pl.program_id(1)
        dkr = dk_ref[0].T
        dv = dv_ref[0].T
        gv = g_ref[...]
        mats = _qk_mats()

        @pl.when((b == 0) & (i == 0))
        def _():
            dg_o[...] = jnp.zeros_like(dg_o)

        @pl.when(i < nt)
        def _():
            kx = kv_ref[:, :GROUP_W]
            r = lax.rsqrt(_apply(kx * kx, mats[0]) + EPS)
            dk, dgr = _qk_bwd(dkr, kx, r, gv, cos_ref[...], sin_ref[...], mats)
            d_o[:, :GROUP_W] = dk.astype(BF16)
            d_o[:, GROUP_W:] = dv.astype(BF16)
            dg_o[...] += jnp.sum(dgr, axis=0, keepdims=True)

        @pl.when(i == nt)
        def _():
            kx = m_ref[:, :GROUP_W]
            r = lax.rsqrt(_apply(kx * kx, mats[0]) + EPS)
            dn = dkr[0:N_META]
            dyg = dn * gv
            dm_o[0, :, :GROUP_W] = r * dyg - kx * (r * r * r) * _apply(dyg * kx, mats[0])
            dm_o[0, :, GROUP_W:] = dv[0:N_META]
            dg_o[...] += jnp.sum(dn * kx * r, axis=0, keepdims=True)

    tspec = pl.BlockSpec((1, GROUP_W, tk), lambda b, i: (b, 0, i))
    rope = pl.BlockSpec((tk, GROUP_W), lambda b, i: (jnp.minimum(i, last), 0))
    return pl.pallas_call(
        body, name="kv_bwd", grid=(nb, nt + 1),
        out_shape=(jax.ShapeDtypeStruct((rows, 512), BF16), jax.ShapeDtypeStruct((nb, N_META, 512), F32),
                   jax.ShapeDtypeStruct((1, GROUP_W), F32)),
        in_specs=[tspec, tspec, pl.BlockSpec((tk, 512), lambda b, i: (b * nt + jnp.minimum(i, last), 0)),
                  pl.BlockSpec((N_META, 512), lambda b, i: (0, 0)), pl.BlockSpec((1, GROUP_W), lambda b, i: (0, 0)),
                  rope, rope],
        out_specs=(pl.BlockSpec((tk, 512), lambda b, i: (b * nt + jnp.minimum(i, last), 0)),
                   pl.BlockSpec((1, N_META, 512), lambda b, i: (b, 0, 0)),
                   pl.BlockSpec((1, GROUP_W), lambda b, i: (0, 0))),
        compiler_params=_params(("arbitrary", "arbitrary"), 32),
    )(dkt, dvt, pkv, pm_kv, kg, cos, sin)


def _conv_bwd(dc0, pconv, pm_conv, conv_w, nb, tm):
    rows = pconv.shape[0]
    nt = rows // nb // tm

    def body(dcur, dprev, dnxt, cur, meta, w_ref, da_o, dam_o, gw_o, ucur, dext, dsh):
        b, i = pl.program_id(0), pl.program_id(1)
        ucur[...] = _glu(cur[...])
        dext[0:16] = jnp.zeros((16, D), F32)
        dext[16:32] = jnp.where(i == 0, 0.0, dprev[...])
        dext[32:32 + tm] = dcur[...]
        dext[32 + tm:48 + tm] = jnp.where(i == nt - 1, 0.0, dnxt[...])
        _shifted_copies(dsh, dext, tm + 40)

        @pl.when((b == 0) & (i == 0))
        def _():
            gw_o[...] = jnp.zeros_like(gw_o)

        for c0 in range(0, D, 256):
            cs = slice(c0, c0 + 256)
            for r0 in range(0, tm, 32):
                acc = jnp.zeros((32, 256), F32)
                for j in range(CONV_K):
                    acc = acc + _rows32(dsh, dext, r0 + 47 - j, cs) * w_ref[j:j + 1, cs]
                cv = cur[r0:r0 + 32, c0:c0 + 256]
                sg = _sig(cur[r0:r0 + 32, D + c0:D + c0 + 256])
                da_o[r0:r0 + 32, cs] = (acc * sg).astype(BF16)
                da_o[r0:r0 + 32, D + c0:D + c0 + 256] = (acc * cv * sg * (1.0 - sg)).astype(BF16)
            for j in range(CONV_K):
                acc = jnp.zeros((32, 256), F32)
                for r0 in range(0, tm, 32):
                    acc = acc + _rows32(dsh, dext, r0 + 47 - j, cs) * ucur[r0:r0 + 32, cs]
                gw_o[j:j + 1, cs] += jnp.sum(acc, axis=0, keepdims=True)

        @pl.when(i == 0)
        def _():
            for c0 in range(0, D, 256):
                cs = slice(c0, c0 + 256)
                cv = meta[:, c0:c0 + 256]
                sg = _sig(meta[:, D + c0:D + c0 + 256])
                um = cv * sg
                acc = jnp.zeros((16, 256), F32)
                for j in range(CONV_K):
                    d = dext[31 - j:47 - j, cs]
                    acc = acc + d * w_ref[j:j + 1, cs]
                    gw_o[j:j + 1, cs] += jnp.sum(d * um, axis=0, keepdims=True)
                dam_o[0, :, cs] = acc * sg
                dam_o[0, :, D + c0:D + c0 + 256] = acc * cv * sg * (1.0 - sg)

    return pl.pallas_call(
        body, name="conv_bwd", grid=(nb, nt),
        out_shape=(jax.ShapeDtypeStruct((rows, 2048), BF16), jax.ShapeDtypeStruct((nb, N_META, 2048), F32),
                   jax.ShapeDtypeStruct((32, D), F32)),
        in_specs=_halo_specs(D, tm, nt, rows)
        + [pl.BlockSpec((tm, 2048), lambda b, i: (b * nt + i, 0)),
           pl.BlockSpec((16, 2048), lambda b, i: (0, 0)), pl.BlockSpec((32, D), lambda b, i: (0, 0))],
        out_specs=(pl.BlockSpec((tm, 2048), lambda b, i: (b * nt + i, 0)),
                   pl.BlockSpec((1, N_META, 2048), lambda b, i: (b, 0, 0)),
                   pl.BlockSpec((32, D), lambda b, i: (0, 0))),
        scratch_shapes=[pltpu.VMEM((tm, D), F32), pltpu.VMEM((tm + 48, D), F32), pltpu.VMEM((8, tm + 40, D), F32)],
        compiler_params=_params(("arbitrary", "arbitrary"), 48),
    )(dc0, dc0, dc0, pconv, pm_conv, conv_w)


def _meta_bwd(dam, ddm, w_full, meta_full, norm_g):
    nb = dam.shape[0]

    def body(a_ref, d_ref, wc_ref, wkv_ref, m_ref, g_ref, gm_o, dg_o):
        a, d = a_ref[0], d_ref[0]
        for b in range(1, nb):
            a = a + a_ref[b]
            d = d + d_ref[b]
        dxn = _dot_nt(a.astype(BF16), wc_ref[...]) + _dot_nt(d.astype(BF16), wkv_ref[...])
        v = m_ref[...]
        r = lax.rsqrt(jnp.mean(v * v, axis=-1, keepdims=True) + EPS)
        gm_o[...] = _rms_bwd(dxn, v, r, g_ref[...])
        dg_o[...] = jnp.sum(dxn * v * r, axis=0, keepdims=True)

    return pl.pallas_call(
        body, name="meta_bwd", grid=(1,),
        out_shape=(jax.ShapeDtypeStruct((N_META, D), F32), jax.ShapeDtypeStruct((1, D), F32)),
        in_specs=[pl.BlockSpec((nb, N_META, 2048), lambda i: (0, 0, 0)), pl.BlockSpec((nb, N_META, 512), lambda i: (0, 0, 0)),
                  pl.BlockSpec((D, 2048), lambda i: (0, 0)), pl.BlockSpec((D, 512), lambda i: (0, G_KV[0] // 512)),
                  pl.BlockSpec((N_META, D), lambda i: (0, 0)), pl.BlockSpec((1, D), lambda i: (0, 0))],
        out_specs=(pl.BlockSpec((N_META, D), lambda i: (0, 0)), pl.BlockSpec((1, D), lambda i: (0, 0))),
        compiler_params=_params(("arbitrary",), 32),
    )(dam, ddm, w_full, w_full, meta_full, norm_g)


def _dxn(d_groups, w_full, x2, dy, norm_g, dg_init, tm):
    rows = x2.shape[0]
    groups = (G_CONV, G_CZ, G_Q, G_KV, G_E)

    def body(da, db, dq, dd, de, w_hbm, x_ref, dy_ref, g_ref, gi_ref, gx_o, dg_o, w_vmem, sem):
        @pl.when(pl.program_id(0) == 0)
        def _():
            cp = pltpu.make_async_copy(w_hbm, w_vmem, sem)
            cp.start()
            cp.wait()
            dg_o[...] = gi_ref[...]

        dxn = jnp.zeros((tm, D), F32)
        for ref, (off, wd) in zip((da, db, dq, dd, de), groups):
            for c0 in range(0, wd, 512):
                dxn = dxn + _dot_nt(ref[:, c0:c0 + 512], w_vmem[:, off + c0:off + c0 + 512])
        v = x_ref[...]
        r = lax.rsqrt(jnp.mean(v * v, axis=-1, keepdims=True) + EPS)
        gx_o[...] = dy_ref[...] + _rms_bwd(dxn, v, r, g_ref[...])
        dg_o[...] += jnp.sum(dxn * v * r, axis=0, keepdims=True)

    row = lambda wd: pl.BlockSpec((tm, wd), lambda i: (i, 0))
    vec = pl.BlockSpec((1, D), lambda i: (0, 0))
    return pl.pallas_call(
        body, name="dxn", grid=(rows // tm,),
        out_shape=(jax.ShapeDtypeStruct((rows, D), F32), jax.ShapeDtypeStruct((1, D), F32)),
        in_specs=[row(wd) for _, wd in groups] + [pl.BlockSpec(memory_space=pl.ANY), row(D), row(D), vec, vec],
        out_specs=(row(D), vec),
        scratch_shapes=[pltpu.VMEM((D, IN_DIM), BF16), pltpu.SemaphoreType.DMA],
        compiler_params=_params(("arbitrary",), 56),
    )(*d_groups, w_full, x2, dy, norm_g, dg_init)


def _wgrad(at, b, buf, slot, col_off, name, meta=None):
    rows, n = b.shape
    tn, tk = 512, min(2048, rows)
    nk = rows // tk
    j0 = col_off // tn

    def body(*refs):
        if meta is None:
            at_ref, b_ref, _, o_ref = refs
        else:
            at_ref, b_ref, xm_ref, dm_ref, _, o_ref = refs
        k = pl.program_id(1)

        @pl.when(k == 0)
        def _():
            if meta is None:
                o_ref[0] = jnp.zeros((D, tn), F32)
            else:
                dm = dm_ref[0]
                for e in range(1, dm_ref.shape[0]):
                    dm = dm + dm_ref[e]
                dm = jnp.concatenate([dm, jnp.zeros((128 - N_META, tn), F32)], axis=0)
                o_ref[0] = _dot(xm_ref[...], dm.astype(BF16))

        o_ref[0] += _dot(at_ref[...], b_ref[...].astype(BF16))

    in_specs = [pl.BlockSpec((D, tk), lambda j, k: (0, k)), pl.BlockSpec((tk, tn), lambda j, k: (k, j))]
    args = [at, b]
    if meta is not None:
        xmt, dm = meta
        in_specs += [pl.BlockSpec((D, 128), lambda j, k: (0, 0)),
                     pl.BlockSpec((dm.shape[0], N_META, tn), lambda j, k: (0, 0, j))]
        args += [xmt, dm]
    in_specs.append(pl.BlockSpec(memory_space=pl.ANY))
    args.append(buf)
    return pl.pallas_call(
        body, name=name, grid=(n // tn, nk),
        out_shape=jax.ShapeDtypeStruct(buf.shape, F32),
        in_specs=in_specs,
        out_specs=pl.BlockSpec((1, D, tn), lambda j, k: (slot, 0, j0 + j)),
        input_output_aliases={len(args) - 1: 0},
        compiler_params=_params(("parallel", "arbitrary"), 32),
    )(*args)


def _rope_tables(s_len):
    pos = jnp.arange(s_len, dtype=jnp.int32)
    row_ids = (pos // GRID_W).astype(F32)
    col_ids = (pos % GRID_W).astype(F32)
    inv_freq = ROPE_THETA ** (-jnp.arange(ROPE_FREQS, dtype=F32) / ROPE_FREQS)
    a_row = row_ids[:, None] * inv_freq[None, :]
    a_col = col_ids[:, None] * inv_freq[None, :]
    ang = jnp.concatenate([a_row, a_row, a_col, a_col], axis=-1)
    return jnp.tile(jnp.cos(ang), (1, GQA)), jnp.tile(jnp.sin(ang), (1, GQA))


def _local_step(x, loss_target, norm_g, conv_b, cn_g, cn_b, q_g, k_g, w_full, w3_full, conv_w_full, meta_full,
                reduce_start=None):
    nb, s_len, _ = x.shape
    rows = nb * s_len
    x2 = x.reshape(rows, D)
    t2 = loss_target.reshape(rows, D)
    cos, sin = _rope_tables(s_len)
    qg = jnp.tile(q_g, (1, GQA))
    kg = jnp.tile(k_g, (1, N_KV))

    xnmt, pm_conv, pm_kv = _meta_fwd(meta_full, norm_g, w_full)
    pconv, pcz, pq, pkv, pe, xnt = _in_proj(x2, norm_g, w_full, 256)
    c0 = _conv_fwd(pconv, pm_conv, conv_w_full, conv_b, nb, 256)
    kr, ve = _kv_prep(pkv, pm_kv, kg, cos, sin, nb)
    q2, qst = _q_prep(pq, qg, cos, sin, nb, 256)
    o, lse = _attn_fwd(q2, kr, ve, nb, 256)
    dy, mt, c3t, o2t, dyc, dya, d_o, dc0, dcz, de, sums = _mid(x2, t2, c0, pcz, o, pe, w3_full, cn_g, cn_b, 128)
    doe, dot_ = _do_prep(d_o, o, 256)
    dqr, dkt, dvt = _attn_bwd(q2, qst, kr, ve, doe, dot_, lse, nb, 256)
    dq, dqg = _q_post(dqr, pq, qg, cos, sin, nb, 256)
    lk = s_len + KEY_PAD
    dd, ddm, dkg = _kv_bwd(dkt.reshape(nb, GROUP_W, lk), dvt.reshape(nb, GROUP_W, lk), pkv, pm_kv, kg, cos, sin, nb)
    da, dam, gcw = _conv_bwd(dc0, pconv, pm_conv, conv_w_full, nb, 256)
    gmeta, dng_m = _meta_bwd(dam, ddm, w_full, meta_full, norm_g)

    gw3 = lax.empty((3, D, D), F32)
    gw3 = _wgrad(c3t, dyc, gw3, 0, 0, "wgrad_conv_out")
    gw3 = _wgrad(o2t, dya, gw3, 1, 0, "wgrad_attn_out")
    gw3 = _wgrad(mt, dy, gw3, 2, 0, "wgrad_out")
    gwin = lax.empty((1, D, IN_DIM), F32)
    gwin = _wgrad(xnt, da, gwin, 0, G_CONV[0], "wgrad_in_conv", meta=(xnmt, dam))
    gwin = _wgrad(xnt, dcz, gwin, 0, G_CZ[0], "wgrad_in_cz")
    gwin = _wgrad(xnt, dq, gwin, 0, G_Q[0], "wgrad_in_q")
    gwin = _wgrad(xnt, dd, gwin, 0, G_KV[0], "wgrad_in_kv", meta=(xnmt, ddm))
    gwin = _wgrad(xnt, de, gwin, 0, G_E[0], "wgrad_in_e")

    pending = None
    if reduce_start is not None:
        token, pending = reduce_start(gwin, gw3, gcw, gmeta)
        dng_m = dng_m + token[0:1, 0:1]
    gx, dng = _dxn((da, dcz, dq, dd, de), w_full, x2, dy, norm_g, dng_m, 256)

    zeros = jnp.zeros((1, D - 2 * GROUP_W), F32)
    smalls = jnp.concatenate([dng, sums[2:3], sums[0:1], sums[1:2], jnp.concatenate([dqg, dkg, zeros], axis=1),
                              sums[3:4], jnp.zeros((2, D), F32)], axis=0)
    return gx.reshape(nb, s_len, D), gwin, gw3, gcw, gmeta, smalls, pending


def _xyc():
    return lax.axis_index("x"), lax.axis_index("y"), lax.axis_index("c")


def _reduce_sibling(gwin, gw3v, gcm):
    def body(gwin_ref, gw3_ref, gcm_ref, r_win, r_w3, r_cm, send, recv):
        x, y, c = _xyc()
        o = 1 - c
        half = D // 2
        outs = ((gwin_ref.at[pl.ds(o * half, half), :], r_win), (gw3_ref.at[:, :, o], r_w3), (gcm_ref.at[o], r_cm))
        cps = []
        for a, (src, dst) in enumerate(outs):
            cp = pltpu.make_async_remote_copy(src_ref=src, dst_ref=dst, send_sem=send.at[a], recv_sem=recv.at[a],
                                              device_id=(x, y, o), device_id_type=MESH)
            cp.start()
            cps.append(cp)
        for cp in cps:
            cp.wait()

    any_spec = pl.BlockSpec(memory_space=pl.ANY)
    return pl.pallas_call(
        body, name="reduce_sibling",
        out_shape=(jax.ShapeDtypeStruct((D // 2, IN_DIM), F32), jax.ShapeDtypeStruct((3, 4, 128, D), F32),
                   jax.ShapeDtypeStruct((24, D), F32)),
        in_specs=[any_spec] * 3, out_specs=(any_spec,) * 3,
        scratch_shapes=[pltpu.SemaphoreType.DMA((3,)), pltpu.SemaphoreType.DMA((3,))],
    )(gwin, gw3v, gcm)


def _gather_smalls(smalls):
    def body(sm_ref, r_sm, ssend, srecv, lsem):
        x, y, c = _xyc()
        me = 4 * x + 2 * y + c
        loc = pltpu.make_async_copy(sm_ref, r_sm.at[me], lsem)
        loc.start()
        scps = []
        for d in range(1, 8):
            px, py, pc = (x + (d >> 2)) % 2, (y + ((d >> 1) & 1)) % 2, (c + (d & 1)) % 2
            cp = pltpu.make_async_remote_copy(src_ref=sm_ref, dst_ref=r_sm.at[me], send_sem=ssend.at[d - 1],
                                              recv_sem=srecv.at[d - 1], device_id=(px, py, pc), device_id_type=MESH)
            cp.start()
            scps.append((cp, 4 * px + 2 * py + pc))
        for d, (cp, pid) in enumerate(scps):
            pltpu.make_async_remote_copy(src_ref=sm_ref, dst_ref=r_sm.at[pid], send_sem=ssend.at[d],
                                         recv_sem=srecv.at[d], device_id=(x, y, c), device_id_type=MESH).wait_recv()
            cp.wait_send()
        loc.wait()

    any_spec = pl.BlockSpec(memory_space=pl.ANY)
    return pl.pallas_call(
        body, name="gather_smalls", out_shape=jax.ShapeDtypeStruct((8, 8, D), F32),
        in_specs=[any_spec], out_specs=any_spec,
        scratch_shapes=[pltpu.SemaphoreType.DMA((7,)), pltpu.SemaphoreType.DMA((7,)), pltpu.SemaphoreType.DMA],
    )(smalls)


def _add_sibling(gwin, gw3v, gcm, r_win, r_w3, r_cm):
    c = lax.axis_index("c").astype(jnp.int32).reshape(1)
    half = D // 2
    tr = 64

    def body1(c_ref, a_ref, b_ref, o_ref):
        o_ref[...] = (a_ref[...] + b_ref[...]).astype(BF16)

    cs_win = pl.pallas_call(
        body1, name="add_sibling_w_in", out_shape=jax.ShapeDtypeStruct((half, IN_DIM), BF16),
        grid_spec=pltpu.PrefetchScalarGridSpec(
            num_scalar_prefetch=1, grid=(half // tr,),
            in_specs=[pl.BlockSpec((tr, IN_DIM), lambda i, c_ref: (c_ref[0] * (half // tr) + i, 0)),
                      pl.BlockSpec((tr, IN_DIM), lambda i, c_ref: (i, 0))],
            out_specs=pl.BlockSpec((tr, IN_DIM), lambda i, c_ref: (i, 0))),
        compiler_params=_params(("parallel",), 32),
    )(c, gwin, r_win)

    def body2(c_ref, a_ref, b_ref, o_ref):
        o_ref[0, 0] = (a_ref[0, 0, 0] + b_ref[0, 0]).astype(BF16)

    cs_w3 = pl.pallas_call(
        body2, name="add_sibling_w3", out_shape=jax.ShapeDtypeStruct((3, 4, 128, D), BF16),
        grid_spec=pltpu.PrefetchScalarGridSpec(
            num_scalar_prefetch=1, grid=(3, 4),
            in_specs=[pl.BlockSpec((1, 1, 1, 128, D), lambda w, s, c_ref: (w, s, c_ref[0], 0, 0)),
                      pl.BlockSpec((1, 1, 128, D), lambda w, s, c_ref: (w, s, 0, 0))],
            out_specs=pl.BlockSpec((1, 1, 128, D), lambda w, s, c_ref: (w, s, 0, 0))),
        compiler_params=_params(("parallel", "parallel"), 32),
    )(c, gw3v, r_w3)

    def body3(c_ref, a_ref, b_ref, o_ref):
        o_ref[...] = a_ref[0] + b_ref[...]

    cs_cm = pl.pallas_call(
        body3, name="add_sibling_cm", out_shape=jax.ShapeDtypeStruct((24, D), F32),
        grid_spec=pltpu.PrefetchScalarGridSpec(
            num_scalar_prefetch=1, grid=(1,),
            in_specs=[pl.BlockSpec((1, 24, D), lambda i, c_ref: (c_ref[0], 0, 0)),
                      pl.BlockSpec((24, D), lambda i, c_ref: (0, 0))],
            out_specs=pl.BlockSpec((24, D), lambda i, c_ref: (0, 0))),
        compiler_params=_params(("arbitrary",), 32),
    )(c, gcm, r_cm)
    return cs_win, cs_w3, cs_cm


def _reduce_chips_copies(srcs, lands, send, recv):
    win_ref, w3_ref, cm_ref = srcs
    r_win, r_w3, r_cm = lands
    x, y, c = _xyc()
    peers = ((1 - x, y), (x, 1 - y), (1 - x, 1 - y))
    cps = []
    for k, (px, py) in enumerate(peers):
        ps = 2 * px + py
        items = ((win_ref.at[:, pl.ds(ps * W_IN_SHARD, W_IN_SHARD)], r_win.at[k]),
                 (w3_ref.at[:, ps], r_w3.at[k]),
                 (cm_ref.at[:, pl.ds(ps * ROW_SHARD, ROW_SHARD)], r_cm.at[k]))
        for a, (src, dst) in enumerate(items):
            cps.append(pltpu.make_async_remote_copy(src_ref=src, dst_ref=dst, send_sem=send.at[3 * a + k],
                                                    recv_sem=recv.at[3 * a + k], device_id=(px, py, c),
                                                    device_id_type=MESH))
    return cps


_HBM = pl.BlockSpec(memory_space=pltpu.HBM)
_SEM = pl.BlockSpec(memory_space=pltpu.SEMAPHORE)
_EFFECT = pltpu.SideEffectType.DATAFLOW_SIDE_EFFECTING


def _reduce_chips_start(cs_win, cs_w3, cs_cm):
    srcs = (cs_win, cs_w3, cs_cm)
    lands = (lax.empty((3, D // 2, W_IN_SHARD), BF16), lax.empty((3, 3, 128, D), BF16),
             lax.empty((3, 24, ROW_SHARD), F32))

    def body(*refs):
        srcs_in, lands_in, send, recv, token = refs[0:3], refs[3:6], refs[6], refs[7], refs[14]
        for cp in _reduce_chips_copies(srcs_in, lands_in, send, recv):
            cp.start()
        token[...] = jnp.zeros_like(token)

    hbm = lambda a: pltpu.HBM(a.shape, a.dtype)
    outs = pl.pallas_call(
        body, name="reduce_chips_start",
        out_shape=(pltpu.SemaphoreType.DMA((9,)), pltpu.SemaphoreType.DMA((9,)),
                   *[hbm(a) for a in srcs], *[hbm(a) for a in lands], jax.ShapeDtypeStruct((8, 128), F32)),
        in_specs=[_HBM] * 6,
        out_specs=(_SEM, _SEM, *[_HBM] * 6, pl.BlockSpec(memory_space=pltpu.VMEM)),
        input_output_aliases={i: i + 2 for i in range(6)},
        compiler_params=pltpu.CompilerParams(has_side_effects=_EFFECT),
    )(*[pltpu.with_memory_space_constraint(a, pltpu.HBM) for a in srcs + lands])
    return outs[0], outs[1], outs[2:5], outs[5:8], outs[8]


def _reduce_chips_wait(send, recv, srcs, lands, after):
    def body(*refs):
        srcs_in, lands_in, send_ref, recv_ref = refs[0:3], refs[3:6], refs[6], refs[7]
        for cp in _reduce_chips_copies(srcs_in, lands_in, send_ref, recv_ref):
            cp.wait_send()
            cp.wait_recv()

    hbm = lambda a: pltpu.HBM(a.shape, a.dtype)
    outs = pl.pallas_call(
        body, name="reduce_chips_wait",
        out_shape=(*[hbm(a) for a in srcs], *[hbm(a) for a in lands]),
        in_specs=[_HBM] * 6 + [_SEM, _SEM, pl.BlockSpec(memory_space=pl.ANY)],
        out_specs=(_HBM,) * 6,
        input_output_aliases={i: i for i in range(6)},
        compiler_params=pltpu.CompilerParams(has_side_effects=_EFFECT),
    )(*srcs, *lands, send, recv, after)
    return outs[0:3], outs[3:6]


def _add_chips(cs_win, cs_w3, cs_cm, r_win, r_w3, r_cm):
    x, y, c = _xyc()
    idx = jnp.stack([2 * x + y, c]).astype(jnp.int32)
    half = D // 2
    tr = 128

    def body1(i_ref, a_ref, b_ref, o_ref):
        f = lambda v: v.astype(F32)
        o_ref[0] = (f(a_ref[...]) + f(b_ref[2])) + (f(b_ref[0]) + f(b_ref[1]))

    f_win = pl.pallas_call(
        body1, name="add_chips_w_in", out_shape=jax.ShapeDtypeStruct((2, half, W_IN_SHARD), F32),
        grid_spec=pltpu.PrefetchScalarGridSpec(
            num_scalar_prefetch=1, grid=(half // tr,),
            in_specs=[pl.BlockSpec((tr, W_IN_SHARD), lambda i, r: (i, r[0])),
                      pl.BlockSpec((3, tr, W_IN_SHARD), lambda i, r: (0, i, 0))],
            out_specs=pl.BlockSpec((1, tr, W_IN_SHARD), lambda i, r: (r[1], i, 0))),
        compiler_params=_params(("parallel",), 32),
    )(idx, cs_win, r_win)

    def body2(i_ref, a_ref, b_ref, o_ref):
        f = lambda v: v.astype(F32)
        o_ref[0, 0] = (f(a_ref[0, 0]) + f(b_ref[2, 0])) + (f(b_ref[0, 0]) + f(b_ref[1, 0]))

    f_w3 = pl.pallas_call(
        body2, name="add_chips_w3", out_shape=jax.ShapeDtypeStruct((3, 2, 128, D), F32),
        grid_spec=pltpu.PrefetchScalarGridSpec(
            num_scalar_prefetch=1, grid=(3,),
            in_specs=[pl.BlockSpec((1, 1, 128, D), lambda w, r: (w, r[0], 0, 0)),
                      pl.BlockSpec((3, 1, 128, D), lambda w, r: (0, w, 0, 0))],
            out_specs=pl.BlockSpec((1, 1, 128, D), lambda w, r: (w, r[1], 0, 0))),
        compiler_params=_params(("parallel",), 32),
    )(idx, cs_w3, r_w3)

    def body3(i_ref, a_ref, b_ref, o_ref):
        o_ref[0] = (a_ref[...] + b_ref[2]) + (b_ref[0] + b_ref[1])

    f_cm = pl.pallas_call(
        body3, name="add_chips_cm", out_shape=jax.ShapeDtypeStruct((2, 24, ROW_SHARD), F32),
        grid_spec=pltpu.PrefetchScalarGridSpec(
            num_scalar_prefetch=1, grid=(1,),
            in_specs=[pl.BlockSpec((24, ROW_SHARD), lambda i, r: (0, r[0])),
                      pl.BlockSpec((3, 24, ROW_SHARD), lambda i, r: (0, 0, 0))],
            out_specs=pl.BlockSpec((1, 24, ROW_SHARD), lambda i, r: (r[1], 0, 0))),
        compiler_params=_params(("arbitrary",), 32),
    )(idx, cs_cm, r_cm)
    return f_win, f_w3, f_cm


def _share_sibling(f_win, f_w3, f_cm):
    def body(win_in, w3_in, cm_in, win_ref, w3_ref, cm_ref, send, recv):
        x, y, c = _xyc()
        o = 1 - c
        cps = []
        for a, (ref, sl) in enumerate(((win_ref, lambda h: win_ref.at[h]), (w3_ref, lambda h: w3_ref.at[:, h]),
                                       (cm_ref, lambda h: cm_ref.at[h]))):
            cp = pltpu.make_async_remote_copy(src_ref=sl(c), dst_ref=sl(c), send_sem=send.at[a], recv_sem=recv.at[a],
                                              device_id=(x, y, o), device_id_type=MESH)
            cp.start()
            cps.append((cp, sl))
        for a, (cp, sl) in enumerate(cps):
            pltpu.make_async_remote_copy(src_ref=sl(o), dst_ref=sl(o), send_sem=send.at[a], recv_sem=recv.at[a],
                                         device_id=(x, y, o), device_id_type=MESH).wait_recv()
            cp.wait_send()

    any_spec = pl.BlockSpec(memory_space=pl.ANY)
    return pl.pallas_call(
        body, name="share_sibling",
        out_shape=(jax.ShapeDtypeStruct(f_win.shape, F32), jax.ShapeDtypeStruct(f_w3.shape, F32),
                   jax.ShapeDtypeStruct(f_cm.shape, F32)),
        in_specs=[any_spec] * 3, out_specs=(any_spec,) * 3,
        input_output_aliases={0: 0, 1: 1, 2: 2},
        scratch_shapes=[pltpu.SemaphoreType.DMA((3,)), pltpu.SemaphoreType.DMA((3,))],
    )(f_win, f_w3, f_cm)


def _adamw_math(w, g, m, v):
    m = ADAM_B1 * m + (1.0 - ADAM_B1) * g
    v = ADAM_B2 * v + (1.0 - ADAM_B2) * (g * g)
    m_hat = m / (1.0 - ADAM_B1 ** ADAM_STEP)
    v_hat = v / (1.0 - ADAM_B2 ** ADAM_STEP)
    delta = -ADAM_LR * (m_hat / (jnp.sqrt(v_hat) + ADAM_EPS) + ADAM_WD * w)
    return delta, m, v


def _adamw(w, g, m, v, tr, name):
    rows, cols = w.shape

    def body(w_ref, g_ref, m_ref, v_ref, d_o, m_o, v_o):
        d_o[...], m_o[...], v_o[...] = _adamw_math(w_ref[...], g_ref[...], m_ref[...], v_ref[...])

    spec = pl.BlockSpec((tr, cols), lambda i: (i, 0))
    return pl.pallas_call(
        body, name=name, grid=(rows // tr,),
        out_shape=(jax.ShapeDtypeStruct((rows, cols), F32),) * 3,
        in_specs=[spec] * 4, out_specs=(spec,) * 3,
        compiler_params=_params(("parallel",), 32),
    )(w, g, m, v)


def _adamw_small(r_sm, ws, ms, vs):
    def body(s_ref, *refs):
        w_refs, m_refs, v_refs, outs = refs[0:6], refs[6:12], refs[12:18], refs[18:]
        loss_o, g_os, d_os, m_os, v_os = outs[0], outs[1:7], outs[7:13], outs[13:19], outs[19:25]
        g = s_ref[0]
        for dev in range(1, 8):
            g = g + s_ref[dev]
        qk = g[4:5, :]
        qg = qk[:, 0:HEAD_DIM]
        kg = qk[:, GROUP_W:GROUP_W + HEAD_DIM]
        for h in range(1, GQA):
            qg = qg + qk[:, HEAD_DIM * h:HEAD_DIM * (h + 1)]
            kg = kg + qk[:, GROUP_W + HEAD_DIM * h:GROUP_W + HEAD_DIM * (h + 1)]
        loss_o[...] = (0.5 / D) * jnp.sum(g[5:6, :], axis=-1, keepdims=True)
        for i, gi in enumerate((g[0:1], g[1:2], g[2:3], g[3:4], qg, kg)):
            g_os[i][...] = gi
            d_os[i][...], m_os[i][...], v_os[i][...] = _adamw_math(w_refs[i][...], gi, m_refs[i][...], v_refs[i][...])

    six = tuple(jax.ShapeDtypeStruct(w.shape, F32) for w in ws)
    return pl.pallas_call(
        body, name="adamw_small", out_shape=(jax.ShapeDtypeStruct((1, 1), F32),) + six * 4,
    )(r_sm, *ws, *ms, *vs)


def kernel(x, meta_tokens, norm_g, w_in, conv_w, conv_b, conv_norm_g, conv_norm_b, w_conv_out, q_norm_g, k_norm_g, w_attn_out, w_out, loss_target, m_meta_tokens, m_norm_g, m_w_in, m_conv_w, m_conv_b, m_conv_norm_g, m_conv_norm_b, m_w_conv_out, m_q_norm_g, m_k_norm_g, m_w_attn_out, m_w_out, v_meta_tokens, v_norm_g, v_w_in, v_conv_w, v_conv_b, v_conv_norm_g, v_conv_norm_b, v_w_conv_out, v_q_norm_g, v_k_norm_g, v_w_attn_out, v_w_out):
    pad_k = lambda a: jnp.pad(a[0], ((0, 32 - CONV_K), (0, 0)))
    w3_s = jnp.concatenate([w_conv_out, w_attn_out, w_out], axis=0)
    w_full, w3_full, conv_w_full, meta_full = _gather_weights(w_in[0], w3_s, pad_k(conv_w), meta_tokens)

    def reduce_start(gwin, gw3, gcw, gmeta):
        gwin2 = gwin.reshape(D, IN_DIM)
        gw3v = gw3.reshape(3, N_CHIPS, 2, 128, D)
        gcm = jnp.concatenate([gcw.reshape(2, 16, D), gmeta.reshape(2, 8, D)], axis=1)
        r_win, r_w3, r_cm = _reduce_sibling(gwin2, gw3v, gcm)
        cs = _add_sibling(gwin2, gw3v, gcm, r_win, r_w3, r_cm)
        send, recv, srcs, lands, token = _reduce_chips_start(*cs)
        return token, (send, recv, srcs, lands)

    gx, _, _, _, _, smalls, pending = _local_step(
        x, loss_target, norm_g, conv_b, conv_norm_g, conv_norm_b, q_norm_g, k_norm_g,
        w_full, w3_full, conv_w_full, meta_full, reduce_start)
    (cs_win, cs_w3, cs_cm), (r2_win, r2_w3, r2_cm) = _reduce_chips_wait(*pending, gx)
    f_win, f_w3, f_cm = _add_chips(cs_win, cs_w3, cs_cm, r2_win, r2_w3, r2_cm)
    f_win, f_w3, f_cm = _share_sibling(f_win, f_w3, f_cm)
    r_sm = _gather_smalls(smalls)

    g_w_in = f_win.reshape(D, W_IN_SHARD)
    g_w3 = f_w3.reshape(3, ROW_SHARD, D)
    g_conv_w = f_cm[:, 0:16].reshape(32, ROW_SHARD)
    g_meta = f_cm[:, 16:24].reshape(N_META, ROW_SHARD)

    d_w_in, nm_w_in, nv_w_in = _adamw(w_in[0], g_w_in, m_w_in[0], v_w_in[0], 128, "adamw_w_in")
    m3 = jnp.concatenate([m_w_conv_out, m_w_attn_out, m_w_out], axis=0).reshape(3 * ROW_SHARD, D)
    v3 = jnp.concatenate([v_w_conv_out, v_w_attn_out, v_w_out], axis=0).reshape(3 * ROW_SHARD, D)
    d_w3, nm_w3, nv_w3 = _adamw(w3_s.reshape(3 * ROW_SHARD, D), g_w3.reshape(3 * ROW_SHARD, D), m3, v3, 256, "adamw_w3")
    cm_w = jnp.concatenate([pad_k(conv_w), meta_tokens], axis=0)
    cm_m = jnp.concatenate([pad_k(m_conv_w), m_meta_tokens], axis=0)
    cm_v = jnp.concatenate([pad_k(v_conv_w), v_meta_tokens], axis=0)
    cm_g = jnp.concatenate([g_conv_w, g_meta], axis=0)
    d_cm, nm_cm, nv_cm = _adamw(cm_w, cm_g, cm_m, cm_v, 48, "adamw_cm")
    small = _adamw_small(
        r_sm, (norm_g, conv_b, conv_norm_g, conv_norm_b, q_norm_g, k_norm_g),
        (m_norm_g, m_conv_b, m_conv_norm_g, m_conv_norm_b, m_q_norm_g, m_k_norm_g),
        (v_norm_g, v_conv_b, v_conv_norm_g, v_conv_norm_b, v_q_norm_g, v_k_norm_g))

    def assemble(big_in, w3x, cmx, s6):
        w3x = w3x.reshape(3, 1, ROW_SHARD, D)
        ng, cb, cng, cnb, qg, kg = s6
        return (cmx[32:48], ng, big_in[None], cmx[None, 0:CONV_K], cb, cng, cnb, w3x[0], qg, kg, w3x[1], w3x[2])

    loss = small[0].reshape(())
    grads = assemble(g_w_in, g_w3, cm_g, small[1:7])
    deltas = assemble(d_w_in, d_w3, d_cm, small[7:13])
    new_m = assemble(nm_w_in, nm_w3, nm_cm, small[13:19])
    new_v = assemble(nv_w_in, nv_w3, nv_cm, small[19:25])
    return (loss, gx, *grads, *deltas, *new_m, *new_v)
```

```python
import functools
import math

import jax
import jax.numpy as jnp
from jax import lax
from jax.experimental import pallas as pl
from jax.experimental.pallas import tpu as pltpu

F32, BF16 = jnp.float32, jnp.bfloat16
MESH = pl.DeviceIdType.MESH

D = 1024
N_META = 16
CONV_K = 31
N_KV = 4
GQA = 4
HEAD_DIM = 64
GROUP_W = GQA * HEAD_DIM
GRID_W = 64
ROPE_FREQS = 16
ROPE_THETA = 10000.0
EPS = 1e-6
IN_DIM = 7680
KEY_PAD = 128
G_CONV, G_CZ, G_Q, G_KV, G_E = (0, 2048), (2048, 1024), (3072, 1024), (4096, 512), (4608, 3072)
N_CHIPS = 4
W_IN_SHARD = IN_DIM // N_CHIPS
ROW_SHARD = D // N_CHIPS

ADAM_LR, ADAM_B1, ADAM_B2, ADAM_EPS, ADAM_WD, ADAM_STEP = 0.001, 0.9, 0.999, 1e-08, 0.01, 10

NT_DIMS = (((1,), (1,)), ((), ()))


def _params(sem=None, vmem_mb=48):
    return pltpu.CompilerParams(dimension_semantics=sem, vmem_limit_bytes=vmem_mb << 20)


def _sds(shape, dtype):
    return pltpu.HBM(tuple(shape), dtype)


def _pin(*arrays):
    return [pltpu.with_memory_space_constraint(a, pltpu.HBM) for a in arrays]


def _sig(v):
    return jax.nn.sigmoid(v)


def _dsilu(v, s):
    return s * (1.0 + v * (1.0 - s))


def _dot(a, b):
    return jnp.dot(a, b, preferred_element_type=F32)


def _dot_nt(a, b):
    return lax.dot_general(a, b, NT_DIMS, preferred_element_type=F32)


def _qk_mats():
    i = lax.broadcasted_iota(jnp.int32, (GROUP_W, GROUP_W), 0)
    j = lax.broadcasted_iota(jnp.int32, (GROUP_W, GROUP_W), 1)
    mean = jnp.where((i >> 6) == (j >> 6), 1.0 / HEAD_DIM, 0.0).astype(BF16)
    turn = jnp.where((i == j + 16) & ((j & 16) == 0), -1.0,
                     jnp.where((i == j - 16) & ((j & 16) != 0), 1.0, 0.0)).astype(BF16)
    return mean, turn


def _apply(v, mat):
    hi = v.astype(BF16)
    lo = (v - hi.astype(F32)).astype(BF16)
    return _dot(hi, mat) + _dot(lo, mat)


def _qk_fwd(v, g, cos, sin, mats):
    mean, turn = mats
    r = lax.rsqrt(_apply(v * v, mean) + EPS)
    n = v * r * g
    return n * cos + _apply(n, turn) * sin, r


def _qk_bwd(dy, v, r, g, cos, sin, mats):
    mean, turn = mats
    dn = dy * cos - _apply(dy, turn) * sin
    dyg = dn * g
    dv = r * dyg - v * (r * r * r) * _apply(dyg * v, mean)
    return dv, dn * v * r


def _rms_bwd(dxn, v, r, g):
    dxg = dxn * g
    return r * dxg - v * (r * r * r) * jnp.mean(dxg * v, axis=-1, keepdims=True)


def _glu(a):
    return a[:, :D] * _sig(a[:, D:])


def _gather_weights(w_in_s, w3_s, conv_w_s, meta_s):
    def body(win_ref, wa_ref, wb_ref, wc_ref, cw_ref, mt_ref, win_o, w3_o, cw_o, mt_o, win_b, w3_b,
             send, recv, fsend, frecv, lsem):
        x, y, c = _xyc()
        o = 1 - c
        me = 2 * x + y
        win_b[...] = win_ref[...].astype(BF16)
        for i, ref in enumerate((wa_ref, wb_ref, wc_ref)):
            w3_b[i] = ref[0].astype(BF16)
        items = (
            (lambda h: win_b.at[pl.ds(h * 512, 512), :],
             lambda p, h: win_o.at[pl.ds(h * 512, 512), pl.ds(p * W_IN_SHARD, W_IN_SHARD)]),
            (lambda h: w3_b.at[:, pl.ds(h * 128, 128), :],
             lambda p, h: w3_o.at[:, pl.ds(p * ROW_SHARD + h * 128, 128), :]),
            (lambda h: cw_ref.at[pl.ds(h * 16, 16), :],
             lambda p, h: cw_o.at[pl.ds(h * 16, 16), pl.ds(p * ROW_SHARD, ROW_SHARD)]),
            (lambda h: mt_ref.at[pl.ds(h * 8, 8), :],
             lambda p, h: mt_o.at[pl.ds(h * 8, 8), pl.ds(p * ROW_SHARD, ROW_SHARD)]),
        )
        peers = ((1 - x, y), (x, 1 - y), (1 - x, 1 - y))

        def remote(src, dst, s_sem, r_sem, to):
            return pltpu.make_async_remote_copy(src_ref=src, dst_ref=dst, send_sem=s_sem, recv_sem=r_sem,
                                                device_id=to, device_id_type=MESH)

        started = []
        for a, (half, place) in enumerate(items):
            for h in range(2):
                loc = pltpu.make_async_copy(half(h), place(me, h), lsem.at[a, h])
                loc.start()
                started.append(loc.wait)
            for k, (px, py) in enumerate(peers):
                cp = remote(half(c), place(me, c), send.at[a, k], recv.at[a, k], (px, py, c))
                cp.start()
                started.append(cp.wait_send)
        for k, (px, py) in enumerate(peers):
            for a, (half, place) in enumerate(items):
                got = place(2 * px + py, c)
                remote(got, got, send.at[a, k], recv.at[a, k], (px, py, c)).wait_recv()
                fw = remote(got, got, fsend.at[a, k], frecv.at[a, k], (x, y, o))
                fw.start()
                started.append(fw.wait_send)
        for k, (px, py) in enumerate(peers):
            for a, (half, place) in enumerate(items):
                theirs = place(2 * px + py, o)
                remote(theirs, theirs, fsend.at[a, k], frecv.at[a, k], (x, y, o)).wait_recv()
        for wait in started:
            wait()

    any_spec = pl.BlockSpec(memory_space=pl.ANY)
    vmem = pl.BlockSpec(memory_space=pltpu.VMEM)
    return pl.pallas_call(
        body, name="gather_weights",
        out_shape=(_sds((D, IN_DIM), BF16), _sds((3, D, D), BF16),
                   _sds((32, D), F32), _sds((N_META, D), F32)),
        in_specs=[vmem] * 6,
        out_specs=(any_spec, any_spec, any_spec, any_spec),
        scratch_shapes=[pltpu.VMEM((D, W_IN_SHARD), BF16), pltpu.VMEM((3, ROW_SHARD, D), BF16),
                        pltpu.SemaphoreType.DMA((4, 3)), pltpu.SemaphoreType.DMA((4, 3)),
                        pltpu.SemaphoreType.DMA((4, 3)), pltpu.SemaphoreType.DMA((4, 3)),
                        pltpu.SemaphoreType.DMA((4, 2))],
        compiler_params=pltpu.CompilerParams(vmem_limit_bytes=40 << 20),
    )(w_in_s, *w3_s, conv_w_s, meta_s)


def _meta_fwd(meta_full, norm_g, w_full):
    def body(m_ref, g_ref, wc_ref, wkv_ref, xnt_ref, pc_ref, pkv_ref):
        v = m_ref[...]
        r = lax.rsqrt(jnp.mean(v * v, axis=-1, keepdims=True) + EPS)
        xn = v * r * g_ref[...]
        xnb = xn.astype(BF16)
        pad = jnp.concatenate([xn, jnp.zeros((128 - N_META, D), F32)], axis=0)
        xnt_ref[...] = pad.T.astype(BF16)
        pc_ref[...] = _dot(xnb, wc_ref[...])
        pkv_ref[...] = _dot(xnb, wkv_ref[...])

    return pl.pallas_call(
        body, name="meta_fwd", grid=(1,),
        out_shape=(_sds((D, 128), BF16), _sds((N_META, 2048), F32),
                   _sds((N_META, 512), F32)),
        in_specs=[pl.BlockSpec((N_META, D), lambda i: (0, 0)), pl.BlockSpec((1, D), lambda i: (0, 0)),
                  pl.BlockSpec((D, 2048), lambda i: (0, 0)), pl.BlockSpec((D, 512), lambda i: (0, G_KV[0] // 512))],
        out_specs=(pl.BlockSpec((D, 128), lambda i: (0, 0)), pl.BlockSpec((N_META, 2048), lambda i: (0, 0)),
                   pl.BlockSpec((N_META, 512), lambda i: (0, 0))),
        compiler_params=_params(("arbitrary",), 32),
    )(*_pin(meta_full, norm_g, w_full, w_full))


def _in_proj(x2, norm_g, w_full, tm):
    rows = x2.shape[0]
    groups = (G_CONV, G_CZ, G_Q, G_KV, G_E)

    def body(x_ref, g_ref, w_hbm, *rest):
        outs, xnt_ref, w_vmem, sem = rest[:5], rest[5], rest[6], rest[7]

        @pl.when(pl.program_id(0) == 0)
        def _():
            cp = pltpu.make_async_copy(w_hbm, w_vmem, sem)
            cp.start()
            cp.wait()

        v = x_ref[...]
        r = lax.rsqrt(jnp.mean(v * v, axis=-1, keepdims=True) + EPS)
        xn = v * r * g_ref[...]
        xnb = xn.astype(BF16)
        xnt_ref[...] = xn.T.astype(BF16)
        for ref, (off, wd) in zip(outs, groups):
            for c0 in range(0, wd, 512):
                ref[:, c0:c0 + 512] = _dot(xnb, w_vmem[:, off + c0:off + c0 + 512])

    return pl.pallas_call(
        body, name="in_proj", grid=(rows // tm,),
        out_shape=tuple(_sds((rows, wd), F32) for _, wd in groups)
        + (_sds((D, rows), BF16),),
        in_specs=[pl.BlockSpec((tm, D), lambda i: (i, 0)), pl.BlockSpec((1, D), lambda i: (0, 0)),
                  pl.BlockSpec(memory_space=pl.ANY)],
        out_specs=tuple(pl.BlockSpec((tm, wd), lambda i: (i, 0)) for _, wd in groups)
        + (pl.BlockSpec((D, tm), lambda i: (0, i)),),
        scratch_shapes=[pltpu.VMEM((D, IN_DIM), BF16), pltpu.SemaphoreType.DMA],
        compiler_params=_params(("arbitrary",), 56),
    )(*_pin(x2, norm_g, w_full))


def _halo_specs(width, tm, nt, rows):
    h16 = tm // 16
    return [pl.BlockSpec((tm, width), lambda b, i: (b * nt + i, 0)),
            pl.BlockSpec((16, width), lambda b, i: (jnp.maximum((b * nt + i) * h16 - 1, 0), 0)),
            pl.BlockSpec((16, width), lambda b, i: (jnp.minimum((b * nt + i + 1) * h16, rows // 16 - 1), 0))]


def _fill_uext(uext, cur, prev, nxt, meta, i, nt, tm):
    uext[0:16] = jnp.where(i == 0, _glu(meta[...]), _glu(prev[...]))
    uext[16:16 + tm] = _glu(cur[...])
    uext[16 + tm:32 + tm] = jnp.where(i == nt - 1, 0.0, _glu(nxt[...]))


def _shifted_copies(dst, src, n):
    for r in range(1, 8):
        dst[r, 0:n] = src[r:r + n]


def _rows32(shifted, src, start, cols):
    q8, r = divmod(start, 8)
    if r == 0:
        return src[start:start + 32, cols]
    return shifted[r, 8 * q8:8 * q8 + 32, cols]


def _conv_fwd(pconv, pm_conv, conv_w, conv_b, nb, tm):
    rows = pconv.shape[0]
    nt = rows // nb // tm

    def body(cur, prev, nxt, meta, w_ref, b_ref, o_ref, uext, ush):
        i = pl.program_id(1)
        _fill_uext(uext, cur, prev, nxt, meta, i, nt, tm)
        _shifted_copies(ush, uext, tm + 24)
        for r0 in range(0, tm, 32):
            for c0 in range(0, D, 256):
                acc = jnp.zeros((32, 256), F32) + b_ref[:, c0:c0 + 256]
                for j in range(CONV_K):
                    acc = acc + _rows32(ush, uext, r0 + j + 1, slice(c0, c0 + 256)) * w_ref[j:j + 1, c0:c0 + 256]
                o_ref[r0:r0 + 32, c0:c0 + 256] = acc

    return pl.pallas_call(
        body, name="conv_fwd", grid=(nb, nt),
        out_shape=_sds((rows, D), F32),
        in_specs=_halo_specs(2048, tm, nt, rows)
        + [pl.BlockSpec((16, 2048), lambda b, i: (0, 0)), pl.BlockSpec((32, D), lambda b, i: (0, 0)),
           pl.BlockSpec((1, D), lambda b, i: (0, 0))],
        out_specs=pl.BlockSpec((tm, D), lambda b, i: (b * nt + i, 0)),
        scratch_shapes=[pltpu.VMEM((tm + 32, D), F32), pltpu.VMEM((8, tm + 24, D), F32)],
        compiler_params=_params(("parallel", "parallel"), 40),
    )(*_pin(pconv, pconv, pconv, pm_conv, conv_w, conv_b))


def _kv_prep(pkv, pm_kv, kg, cos, sin, nb):
    rows = pkv.shape[0]
    s_len = rows // nb
    tk = 128
    nt = s_len // tk

    def body(kv_ref, m_ref, g_ref, cos_ref, sin_ref, k_o, v_o):
        i = pl.program_id(1)
        mats = _qk_mats()

        @pl.when(i < nt)
        def _():
            kv = kv_ref[...]
            kr, _ = _qk_fwd(kv[:, :GROUP_W], g_ref[...], cos_ref[...], sin_ref[...], mats)
            ones = _ones_cols(tk, tk)
            for h in range(N_KV):
                k_o[0, h] = kr[:, HEAD_DIM * h:HEAD_DIM * (h + 1)].astype(BF16)
                vh = kv[:, GROUP_W + HEAD_DIM * h:GROUP_W + HEAD_DIM * (h + 1)]
                v_o[0, h] = jnp.concatenate([vh, ones], axis=1).astype(BF16)

        @pl.when(i == nt)
        def _():
            kv = m_ref[...]
            km = kv[:, :GROUP_W]
            kn = km * lax.rsqrt(_apply(km * km, mats[0]) + EPS) * g_ref[...]
            zeros = jnp.zeros((tk - N_META, GROUP_W), F32)
            kfull = jnp.concatenate([kn, zeros], axis=0)
            vfull = jnp.concatenate([kv[:, GROUP_W:], zeros], axis=0)
            ones = _ones_cols(tk, N_META)
            for h in range(N_KV):
                k_o[0, h] = kfull[:, HEAD_DIM * h:HEAD_DIM * (h + 1)].astype(BF16)
                v_o[0, h] = jnp.concatenate([vfull[:, HEAD_DIM * h:HEAD_DIM * (h + 1)], ones], axis=1).astype(BF16)

    lk = s_len + KEY_PAD
    last = nt - 1
    return pl.pallas_call(
        body, name="kv_prep", grid=(nb, nt + 1),
        out_shape=(_sds((nb, N_KV, lk, HEAD_DIM), BF16),
                   _sds((nb, N_KV, lk, 2 * HEAD_DIM), BF16)),
        in_specs=[pl.BlockSpec((tk, 512), lambda b, i: (b * nt + jnp.minimum(i, last), 0)),
                  pl.BlockSpec((N_META, 512), lambda b, i: (0, 0)), pl.BlockSpec((1, GROUP_W), lambda b, i: (0, 0)),
                  pl.BlockSpec((tk, GROUP_W), lambda b, i: (jnp.minimum(i, last), 0)),
                  pl.BlockSpec((tk, GROUP_W), lambda b, i: (jnp.minimum(i, last), 0))],
        out_specs=(pl.BlockSpec((1, N_KV, tk, HEAD_DIM), lambda b, i: (b, 0, i, 0)),
                   pl.BlockSpec((1, N_KV, tk, 2 * HEAD_DIM), lambda b, i: (b, 0, i, 0))),
        compiler_params=_params(("parallel", "arbitrary"), 32),
    )(*_pin(pkv, pm_kv, kg, cos, sin))


def _ones_cols(rows, valid):
    r = lax.broadcasted_iota(jnp.int32, (rows, HEAD_DIM), 0)
    col = lax.broadcasted_iota(jnp.int32, (rows, HEAD_DIM), 1)
    return jnp.where((col < 2) & (r < valid), 1.0, 0.0).astype(F32)


def _tail_bias():
    col = lax.broadcasted_iota(jnp.int32, (1, KEY_PAD), 1)
    return jnp.where(col < N_META, 0.0, -1e30).astype(F32)


LOG2E = 1.4426950408889634


def _q_prep(pq, qg, cos, sin, nb, tm):
    rows = pq.shape[0]
    nt = rows // nb // tm
    scale = 1.0 / math.sqrt(HEAD_DIM)

    def body(q_ref, g_ref, cos_ref, sin_ref, q2_o, qt_o):
        gv, cosv, sinv = g_ref[...], cos_ref[...], sin_ref[...]
        mats = _qk_mats()
        for g in range(N_KV):
            gs = slice(GROUP_W * g, GROUP_W * (g + 1))
            qr, _ = _qk_fwd(q_ref[:, gs], gv, cosv, sinv, mats)
            q2_o[:, gs] = (qr * (scale * LOG2E)).astype(BF16)
            qt_o[gs, :] = (qr * scale).T.astype(BF16)

    row = pl.BlockSpec((tm, D), lambda b, i: (b * nt + i, 0))
    rope = pl.BlockSpec((tm, GROUP_W), lambda b, i: (i, 0))
    return pl.pallas_call(
        body, name="q_prep", grid=(nb, nt),
        out_shape=(_sds((rows, D), BF16), _sds((D, rows), BF16)),
        in_specs=[row, pl.BlockSpec((1, GROUP_W), lambda b, i: (0, 0)), rope, rope],
        out_specs=(row, pl.BlockSpec((D, tm), lambda b, i: (0, b * nt + i))),
        compiler_params=_params(("parallel", "parallel"), 32),
    )(*_pin(pq, qg, cos, sin))


def _attn_fwd(q2, kr, ve, nb, tq):
    rows = q2.shape[0]
    s_len = rows // nb
    nq = s_len // tq
    lk = s_len + KEY_PAD

    def body(q_ref, k_ref, v_ref, o_ref, lse_ref):
        qs = q_ref[...]
        k1, k2 = k_ref[0, 0, 0:s_len, :], k_ref[0, 0, s_len:lk, :]
        v1, v2 = v_ref[0, 0, 0:s_len, :], v_ref[0, 0, s_len:lk, :]
        bias = _tail_bias()
        outs, lses = [], []

        def scores(h):
            qh = qs[:, HEAD_DIM * h:HEAD_DIM * (h + 1)]
            return _dot_nt(qh, k1), _dot_nt(qh, k2) + bias

        ahead = scores(0)
        for h in range(GQA):
            s1, s2 = ahead
            if h + 1 < GQA:
                ahead = scores(h + 1)
            m = jnp.maximum(jnp.max(s1, axis=-1, keepdims=True), jnp.max(s2, axis=-1, keepdims=True))
            oe = _dot(jnp.exp2(s1 - m).astype(BF16), v1) + _dot(jnp.exp2(s2 - m).astype(BF16), v2)
            l = oe[:, HEAD_DIM:HEAD_DIM + 1]
            outs.append(oe[:, :HEAD_DIM] / l)
            lses.append(m + jnp.log2(l))
        o_ref[...] = jnp.concatenate(outs, axis=1)
        lse_ref[0, 0] = jnp.concatenate(lses, axis=1)

    return pl.pallas_call(
        body, name="attn_fwd", grid=(nb, N_KV, nq),
        out_shape=(_sds((rows, D), F32), _sds((nb, N_KV, s_len, GQA), F32)),
        in_specs=[pl.BlockSpec((tq, GROUP_W), lambda b, g, i: (b * nq + i, g)),
                  pl.BlockSpec((1, 1, lk, HEAD_DIM), lambda b, g, i: (b, g, 0, 0)),
                  pl.BlockSpec((1, 1, lk, 2 * HEAD_DIM), lambda b, g, i: (b, g, 0, 0))],
        out_specs=(pl.BlockSpec((tq, GROUP_W), lambda b, g, i: (b * nq + i, g)),
                   pl.BlockSpec((1, 1, tq, GQA), lambda b, g, i: (b, g, i, 0))),
        compiler_params=_params(("parallel", "parallel", "parallel"), 48),
    )(*_pin(q2, kr, ve))


def _mid(x2, t2, c0, cz, o, e, w3, cn_g, cn_b, tm):
    rows = x2.shape[0]

    def body(x_ref, t_ref, c0_ref, cz_ref, o_ref, e_ref, w_ref, g_ref, b_ref,
             dy_o, mt_o, c3t_o, o2t_o, dyc_o, dya_o, do_o, dc0_o, dcz_o, de_o, sums_o):
        wco, wao, wo = w_ref[0], w_ref[1], w_ref[2]
        cn_g_v = g_ref[...]
        c0v = c0_ref[...]
        xc = c0v - jnp.mean(c0v, axis=-1, keepdims=True)
        rstd = lax.rsqrt(jnp.mean(xc * xc, axis=-1, keepdims=True) + EPS)
        n = xc * rstd
        c1 = n * cn_g_v + b_ref[...]
        s1 = _sig(c1)
        c2 = c1 * s1
        czv = cz_ref[...]
        sz = _sig(czv)
        gz = czv * sz
        c3 = c2 * gz
        yc = _dot(c3.astype(BF16), wco)
        az, gc, ga = e_ref[:, :D], e_ref[:, D:2 * D], e_ref[:, 2 * D:]
        saz = _sig(az)
        gaz = az * saz
        ov = o_ref[...]
        o2 = ov * gaz
        ya = _dot(o2.astype(BF16), wao)
        sc, sa = _sig(gc), _sig(ga)
        merged = sc * yc + sa * ya
        out = _dot(merged.astype(BF16), wo)
        err = x_ref[...] + out - t_ref[...]
        dy = err * (1.0 / D)
        dy_o[...] = dy
        dm = _dot_nt(dy.astype(BF16), wo)
        dyc = dm * sc
        dya = dm * sa
        dycb, dyab = dyc.astype(BF16), dya.astype(BF16)
        dyc_o[...] = dycb
        dya_o[...] = dyab
        de_o[:, D:2 * D] = (dyc * yc * (1.0 - sc)).astype(BF16)
        de_o[:, 2 * D:] = (dya * ya * (1.0 - sa)).astype(BF16)
        dc3 = _dot_nt(dycb, wco)
        do2 = _dot_nt(dyab, wao)
        do_o[...] = do2 * gaz
        de_o[:, :D] = (do2 * ov * _dsilu(az, saz)).astype(BF16)
        dcz_o[...] = (dc3 * c2 * _dsilu(czv, sz)).astype(BF16)
        dc1 = dc3 * gz * _dsilu(c1, s1)
        dn = dc1 * cn_g_v
        dc0 = rstd * (dn - jnp.mean(dn, axis=-1, keepdims=True) - n * jnp.mean(dn * n, axis=-1, keepdims=True))
        dc0_o[...] = dc0
        mt_o[...] = merged.T.astype(BF16)
        c3t_o[...] = c3.T.astype(BF16)
        o2t_o[...] = o2.T.astype(BF16)

        @pl.when(pl.program_id(0) == 0)
        def _():
            sums_o[...] = jnp.zeros_like(sums_o)

        sums_o[0:1, :] += jnp.sum(dc1 * n, axis=0, keepdims=True)
        sums_o[1:2, :] += jnp.sum(dc1, axis=0, keepdims=True)
        sums_o[2:3, :] += jnp.sum(dc0, axis=0, keepdims=True)
        sums_o[3:4, :] += jnp.sum(err * err, axis=0, keepdims=True)

    row = lambda wd: pl.BlockSpec((tm, wd), lambda i: (i, 0))
    col = pl.BlockSpec((D, tm), lambda i: (0, i))
    vec = pl.BlockSpec((1, D), lambda i: (0, 0))
    f32o = lambda wd: _sds((rows, wd), F32)
    b16o = lambda wd: _sds((rows, wd), BF16)
    tpo = _sds((D, rows), BF16)
    return pl.pallas_call(
        body, name="mid", grid=(rows // tm,),
        out_shape=(f32o(D), tpo, tpo, tpo, b16o(D), b16o(D), f32o(D), f32o(D), b16o(D), b16o(3 * D),
                   _sds((8, D), F32)),
        in_specs=[row(D), row(D), row(D), row(D), row(D), row(3 * D),
                  pl.BlockSpec((3, D, D), lambda i: (0, 0, 0)), vec, vec],
        out_specs=(row(D), col, col, col, row(D), row(D), row(D), row(D), row(D), row(3 * D),
                   pl.BlockSpec((8, D), lambda i: (0, 0))),
        compiler_params=_params(("arbitrary",), 56),
    )(*_pin(x2, t2, c0, cz, o, e, w3, cn_g, cn_b))


def _do_prep(d_o, o, tm):
    rows = d_o.shape[0]

    def body(do_ref, o_ref, doe_o, dot_o):
        dov = do_ref[...]
        prod = dov * o_ref[...]
        col = lax.broadcasted_iota(jnp.int32, (tm, HEAD_DIM), 1)
        for h in range(D // HEAD_DIM):
            hs = slice(HEAD_DIM * h, HEAD_DIM * (h + 1))
            delta = jnp.sum(prod[:, hs], axis=-1, keepdims=True)
            d_hi = delta.astype(BF16).astype(F32)
            tail = jnp.where(col == 0, -d_hi, jnp.where(col == 1, d_hi - delta, 0.0))
            doe_o[:, 2 * HEAD_DIM * h:2 * HEAD_DIM * (h + 1)] = jnp.concatenate([dov[:, hs], tail], axis=1).astype(BF16)
        dot_o[...] = dov.T.astype(BF16)

    row = pl.BlockSpec((tm, D), lambda i: (i, 0))
    return pl.pallas_call(
        body, name="do_prep", grid=(rows // tm,),
        out_shape=(_sds((rows, 2 * D), BF16), _sds((D, rows), BF16)),
        in_specs=[row, row],
        out_specs=(pl.BlockSpec((tm, 2 * D), lambda i: (i, 0)), pl.BlockSpec((D, tm), lambda i: (0, i))),
        compiler_params=_params(("parallel",), 32),
    )(*_pin(d_o, o))


def _q_post(dqr, pq, qg, cos, sin, nb, tm):
    rows = pq.shape[0]
    nt = rows // nb // tm

    def body(dq_ref, q_ref, g_ref, cos_ref, sin_ref, dq_o, dg_o):
        @pl.when((pl.program_id(0) == 0) & (pl.program_id(1) == 0))
        def _():
            dg_o[...] = jnp.zeros_like(dg_o)

        gv, cosv, sinv = g_ref[...], cos_ref[...], sin_ref[...]
        acc = jnp.zeros((1, GROUP_W), F32)
        mats = _qk_mats()
        for g in range(N_KV):
            gs = slice(GROUP_W * g, GROUP_W * (g + 1))
            qv = q_ref[:, gs]
            r = lax.rsqrt(_apply(qv * qv, mats[0]) + EPS)
            dq, dgr = _qk_bwd(dq_ref[:, gs], qv, r, gv, cosv, sinv, mats)
            dq_o[:, gs] = dq.astype(BF16)
            acc = acc + jnp.sum(dgr, axis=0, keepdims=True)
        dg_o[...] += acc

    row = pl.BlockSpec((tm, D), lambda b, i: (b * nt + i, 0))
    rope = pl.BlockSpec((tm, GROUP_W), lambda b, i: (i, 0))
    vec = pl.BlockSpec((1, GROUP_W), lambda b, i: (0, 0))
    return pl.pallas_call(
        body, name="q_post", grid=(nb, nt),
        out_shape=(_sds((rows, D), BF16), _sds((1, GROUP_W), F32)),
        in_specs=[row, row, vec, rope, rope], out_specs=(row, vec),
        compiler_params=_params(("arbitrary", "arbitrary"), 32),
    )(*_pin(dqr, pq, qg, cos, sin))


def _attn_bwd(q2, qst, kr, ve, doe, dot_, lse, nb, tq):
    rows = q2.shape[0]
    s_len = rows // nb
    nq = s_len // tq
    lk = s_len + KEY_PAD
    scale = 1.0 / math.sqrt(HEAD_DIM)

    def body(q_ref, qt_ref, k_ref, v_ref, doe_ref, dot_ref, lse_ref, dq_o, dkt_o, dvt_o):
        i = pl.program_id(2)
        lse = lse_ref[0, 0]
        k1, k2 = k_ref[0, 0, 0:s_len, :], k_ref[0, 0, s_len:lk, :]
        v1, v2 = v_ref[0, 0, 0:s_len, :], v_ref[0, 0, s_len:lk, :]
        bias = _tail_bias()
        dkt1, dkt2 = jnp.zeros((HEAD_DIM, s_len), F32), jnp.zeros((HEAD_DIM, KEY_PAD), F32)
        dvt1, dvt2 = jnp.zeros((HEAD_DIM, s_len), F32), jnp.zeros((HEAD_DIM, KEY_PAD), F32)

        def products(h):
            qh = q_ref[:, HEAD_DIM * h:HEAD_DIM * (h + 1)]
            dh = doe_ref[:, 2 * HEAD_DIM * h:2 * HEAD_DIM * (h + 1)]
            return _dot_nt(qh, k1), _dot_nt(qh, k2) + bias, _dot_nt(dh, v1), _dot_nt(dh, v2)

        ahead = products(0)
        for h in range(GQA):
            hs = slice(HEAD_DIM * h, HEAD_DIM * (h + 1))
            s1, s2, dp1, dp2 = ahead
            if h + 1 < GQA:
                ahead = products(h + 1)
            lse_h = lse[:, h:h + 1]
            p1 = jnp.exp2(s1 - lse_h)
            p2 = jnp.exp2(s2 - lse_h)
            ds1 = (p1 * dp1).astype(BF16)
            ds2 = (p2 * dp2).astype(BF16)
            dq_o[:, hs] = (_dot(ds1, k1) + _dot(ds2, k2)) * scale
            dkt1 = dkt1 + _dot(qt_ref[hs, :], ds1)
            dkt2 = dkt2 + _dot(qt_ref[hs, :], ds2)
            dvt1 = dvt1 + _dot(dot_ref[hs, :], p1.astype(BF16))
            dvt2 = dvt2 + _dot(dot_ref[hs, :], p2.astype(BF16))

        @pl.when(i == 0)
        def _():
            dkt_o[0, 0, :, 0:s_len] = dkt1
            dkt_o[0, 0, :, s_len:lk] = dkt2
            dvt_o[0, 0, :, 0:s_len] = dvt1
            dvt_o[0, 0, :, s_len:lk] = dvt2

        @pl.when(i > 0)
        def _():
            dkt_o[0, 0, :, 0:s_len] += dkt1
            dkt_o[0, 0, :, s_len:lk] += dkt2
            dvt_o[0, 0, :, 0:s_len] += dvt1
            dvt_o[0, 0, :, s_len:lk] += dvt2

    qspec = pl.BlockSpec((tq, GROUP_W), lambda b, g, i: (b * nq + i, g))
    qtspec = pl.BlockSpec((GROUP_W, tq), lambda b, g, i: (g, b * nq + i))
    kspec = pl.BlockSpec((1, 1, lk, HEAD_DIM), lambda b, g, i: (b, g, 0, 0))
    tspec = pl.BlockSpec((1, 1, HEAD_DIM, lk), lambda b, g, i: (b, g, 0, 0))
    return pl.pallas_call(
        body, name="attn_bwd", grid=(nb, N_KV, nq),
        out_shape=(_sds((rows, D), F32), _sds((nb, N_KV, HEAD_DIM, lk), F32),
                   _sds((nb, N_KV, HEAD_DIM, lk), F32)),
        in_specs=[qspec, qtspec, kspec, pl.BlockSpec((1, 1, lk, 2 * HEAD_DIM), lambda b, g, i: (b, g, 0, 0)),
                  pl.BlockSpec((tq, 2 * GROUP_W), lambda b, g, i: (b * nq + i, g)), qtspec,
                  pl.BlockSpec((1, 1, tq, GQA), lambda b, g, i: (b, g, i, 0))],
        out_specs=(qspec, tspec, tspec),
        compiler_params=_params(("parallel", "parallel", "arbitrary"), 56),
    )(*_pin(q2, qst, kr, ve, doe, dot_, lse))


def _kv_bwd(dkt, dvt, pkv, pm_kv, kg, cos, sin, nb):
    rows = pkv.shape[0]
    s_len = rows // nb
    tk = 128
    nt = s_len // tk
    last = nt - 1

    def body(dk_ref, dv_ref, kv_ref, m_ref, g_ref, cos_ref, sin_ref, d_o, dm_o, dg_o):
        b, i = pl.program_id(0), pl.program_id(1)
        dkr = dk_ref[0].T
        dv = dv_ref[0].T
        gv = g_ref[...]
        mats = _qk_mats()

        @pl.when((b == 0) & (i == 0))
        def _():
            dg_o[...] = jnp.zeros_like(dg_o)

        @pl.when(i < nt)
        def _():
            kx = kv_ref[:, :GROUP_W]
            r = lax.rsqrt(_apply(kx * kx, mats[0]) + EPS)
            dk, dgr = _qk_bwd(dkr, kx, r, gv, cos_ref[...], sin_ref[...], mats)
            d_o[:, :GROUP_W] = dk.astype(BF16)
            d_o[:, GROUP_W:] = dv.astype(BF16)
            dg_o[...] += jnp.sum(dgr, axis=0, keepdims=True)

        @pl.when(i == nt)
        def _():
            kx = m_ref[:, :GROUP_W]
            r = lax.rsqrt(_apply(kx * kx, mats[0]) + EPS)
            dn = dkr[0:N_META]
            dyg = dn * gv
            dm_o[0, :, :GROUP_W] = r * dyg - kx * (r * r * r) * _apply(dyg * kx, mats[0])
            dm_o[0, :, GROUP_W:] = dv[0:N_META]
            dg_o[...] += jnp.sum(dn * kx * r, axis=0, keepdims=True)

    tspec = pl.BlockSpec((1, GROUP_W, tk), lambda b, i: (b, 0, i))
    rope = pl.BlockSpec((tk, GROUP_W), lambda b, i: (jnp.minimum(i, last), 0))
    return pl.pallas_call(
        body, name="kv_bwd", grid=(nb, nt + 1),
        out_shape=(_sds((rows, 512), BF16), _sds((nb, N_META, 512), F32),
                   _sds((1, GROUP_W), F32)),
        in_specs=[tspec, tspec, pl.BlockSpec((tk, 512), lambda b, i: (b * nt + jnp.minimum(i, last), 0)),
                  pl.BlockSpec((N_META, 512), lambda b, i: (0, 0)), pl.BlockSpec((1, GROUP_W), lambda b, i: (0, 0)),
                  rope, rope],
        out_specs=(pl.BlockSpec((tk, 512), lambda b, i: (b * nt + jnp.minimum(i, last), 0)),
                   pl.BlockSpec((1, N_META, 512), lambda b, i: (b, 0, 0)),
                   pl.BlockSpec((1, GROUP_W), lambda b, i: (0, 0))),
        compiler_params=_params(("arbitrary", "arbitrary"), 32),
    )(*_pin(dkt, dvt, pkv, pm_kv, kg, cos, sin))


def _conv_bwd(dc0, pconv, pm_conv, conv_w, nb, tm):
    rows = pconv.shape[0]
    nt = rows // nb // tm

    def body(dcur, dprev, dnxt, cur, meta, w_ref, da_o, dam_o, gw_o, ucur, dext, dsh):
        b, i = pl.program_id(0), pl.program_id(1)
        ucur[...] = _glu(cur[...])
        dext[0:16] = jnp.zeros((16, D), F32)
        dext[16:32] = jnp.where(i == 0, 0.0, dprev[...])
        dext[32:32 + tm] = dcur[...]
        dext[32 + tm:48 + tm] = jnp.where(i == nt - 1, 0.0, dnxt[...])
        _shifted_copies(dsh, dext, tm + 40)

        @pl.when((b == 0) & (i == 0))
        def _():
            gw_o[...] = jnp.zeros_like(gw_o)

        for c0 in range(0, D, 256):
            cs = slice(c0, c0 + 256)
            for r0 in range(0, tm, 32):
                acc = jnp.zeros((32, 256), F32)
                for j in range(CONV_K):
                    acc = acc + _rows32(dsh, dext, r0 + 47 - j, cs) * w_ref[j:j + 1, cs]
                cv = cur[r0:r0 + 32, c0:c0 + 256]
                sg = _sig(cur[r0:r0 + 32, D + c0:D + c0 + 256])
                da_o[r0:r0 + 32, cs] = (acc * sg).astype(BF16)
                da_o[r0:r0 + 32, D + c0:D + c0 + 256] = (acc * cv * sg * (1.0 - sg)).astype(BF16)
            for j in range(CONV_K):
                acc = jnp.zeros((32, 256), F32)
                for r0 in range(0, tm, 32):
                    acc = acc + _rows32(dsh, dext, r0 + 47 - j, cs) * ucur[r0:r0 + 32, cs]
                gw_o[j:j + 1, cs] += jnp.sum(acc, axis=0, keepdims=True)

        @pl.when(i == 0)
        def _():
            for c0 in range(0, D, 256):
                cs = slice(c0, c0 + 256)
                cv = meta[:, c0:c0 + 256]
                sg = _sig(meta[:, D + c0:D + c0 + 256])
                um = cv * sg
                acc = jnp.zeros((16, 256), F32)
                for j in range(CONV_K):
                    d = dext[31 - j:47 - j, cs]
                    acc = acc + d * w_ref[j:j + 1, cs]
                    gw_o[j:j + 1, cs] += jnp.sum(d * um, axis=0, keepdims=True)
                dam_o[0, :, cs] = acc * sg
                dam_o[0, :, D + c0:D + c0 + 256] = acc * cv * sg * (1.0 - sg)

    return pl.pallas_call(
        body, name="conv_bwd", grid=(nb, nt),
        out_shape=(_sds((rows, 2048), BF16), _sds((nb, N_META, 2048), F32),
                   _sds((32, D), F32)),
        in_specs=_halo_specs(D, tm, nt, rows)
        + [pl.BlockSpec((tm, 2048), lambda b, i: (b * nt + i, 0)),
           pl.BlockSpec((16, 2048), lambda b, i: (0, 0)), pl.BlockSpec((32, D), lambda b, i: (0, 0))],
        out_specs=(pl.BlockSpec((tm, 2048), lambda b, i: (b * nt + i, 0)),
                   pl.BlockSpec((1, N_META, 2048), lambda b, i: (b, 0, 0)),
                   pl.BlockSpec((32, D), lambda b, i: (0, 0))),
        scratch_shapes=[pltpu.VMEM((tm, D), F32), pltpu.VMEM((tm + 48, D), F32), pltpu.VMEM((8, tm + 40, D), F32)],
        compiler_params=_params(("arbitrary", "arbitrary"), 48),
    )(*_pin(dc0, dc0, dc0, pconv, pm_conv, conv_w))


def _meta_bwd(dam, ddm, w_full, meta_full, norm_g):
    nb = dam.shape[0]

    def body(a_ref, d_ref, wc_ref, wkv_ref, m_ref, g_ref, gm_o, dg_o):
        a, d = a_ref[0], d_ref[0]
        for b in range(1, nb):
            a = a + a_ref[b]
            d = d + d_ref[b]
        dxn = _dot_nt(a.astype(BF16), wc_ref[...]) + _dot_nt(d.astype(BF16), wkv_ref[...])
        v = m_ref[...]
        r = lax.rsqrt(jnp.mean(v * v, axis=-1, keepdims=True) + EPS)
        gm_o[...] = _rms_bwd(dxn, v, r, g_ref[...])
        dg_o[...] = jnp.sum(dxn * v * r, axis=0, keepdims=True)

    return pl.pallas_call(
        body, name="meta_bwd", grid=(1,),
        out_shape=(_sds((N_META, D), F32), _sds((1, D), F32)),
        in_specs=[pl.BlockSpec((nb, N_META, 2048), lambda i: (0, 0, 0)), pl.BlockSpec((nb, N_META, 512), lambda i: (0, 0, 0)),
                  pl.BlockSpec((D, 2048), lambda i: (0, 0)), pl.BlockSpec((D, 512), lambda i: (0, G_KV[0] // 512)),
                  pl.BlockSpec((N_META, D), lambda i: (0, 0)), pl.BlockSpec((1, D), lambda i: (0, 0))],
        out_specs=(pl.BlockSpec((N_META, D), lambda i: (0, 0)), pl.BlockSpec((1, D), lambda i: (0, 0))),
        compiler_params=_params(("arbitrary",), 32),
    )(*_pin(dam, ddm, w_full, w_full, meta_full, norm_g))


def _dxn(d_groups, w_full, x2, dy, norm_g, dg_init, tm):
    rows = x2.shape[0]
    groups = (G_CONV, G_CZ, G_Q, G_KV, G_E)

    def body(da, db, dq, dd, de, w_hbm, x_ref, dy_ref, g_ref, gi_ref, gx_o, dg_o, w_vmem, sem):
        @pl.when(pl.program_id(0) == 0)
        def _():
            cp = pltpu.make_async_copy(w_hbm, w_vmem, sem)
            cp.start()
            cp.wait()
            dg_o[...] = gi_ref[...]

        dxn = jnp.zeros((tm, D), F32)
        for ref, (off, wd) in zip((da, db, dq, dd, de), groups):
            for c0 in range(0, wd, 512):
                dxn = dxn + _dot_nt(ref[:, c0:c0 + 512], w_vmem[:, off + c0:off + c0 + 512])
        v = x_ref[...]
        r = lax.rsqrt(jnp.mean(v * v, axis=-1, keepdims=True) + EPS)
        gx_o[...] = dy_ref[...] + _rms_bwd(dxn, v, r, g_ref[...])
        dg_o[...] += jnp.sum(dxn * v * r, axis=0, keepdims=True)

    row = lambda wd: pl.BlockSpec((tm, wd), lambda i: (i, 0))
    vec = pl.BlockSpec((1, D), lambda i: (0, 0))
    return pl.pallas_call(
        body, name="dxn", grid=(rows // tm,),
        out_shape=(_sds((rows, D), F32), _sds((1, D), F32)),
        in_specs=[row(wd) for _, wd in groups] + [pl.BlockSpec(memory_space=pl.ANY), row(D), row(D), vec, vec],
        out_specs=(row(D), vec),
        scratch_shapes=[pltpu.VMEM((D, IN_DIM), BF16), pltpu.SemaphoreType.DMA],
        compiler_params=_params(("arbitrary",), 56),
    )(*_pin(*d_groups, w_full, x2, dy, norm_g, dg_init))


def _wgrad(at, b, buf, slot, col_off, name, meta=None):
    rows, n = b.shape
    tn, tk = 512, min(2048, rows)
    nk = rows // tk
    j0 = col_off // tn

    def body(*refs):
        if meta is None:
            at_ref, b_ref, _, o_ref = refs
        else:
            at_ref, b_ref, xm_ref, dm_ref, _, o_ref = refs
        k = pl.program_id(1)

        @pl.when(k == 0)
        def _():
            if meta is None:
                o_ref[0] = jnp.zeros((D, tn), F32)
            else:
                dm = dm_ref[0]
                for e in range(1, dm_ref.shape[0]):
                    dm = dm + dm_ref[e]
                dm = jnp.concatenate([dm, jnp.zeros((128 - N_META, tn), F32)], axis=0)
                o_ref[0] = _dot(xm_ref[...], dm.astype(BF16))

        o_ref[0] += _dot(at_ref[...], b_ref[...].astype(BF16))

    in_specs = [pl.BlockSpec((D, tk), lambda j, k: (0, k)), pl.BlockSpec((tk, tn), lambda j, k: (k, j))]
    args = [at, b]
    if meta is not None:
        xmt, dm = meta
        in_specs += [pl.BlockSpec((D, 128), lambda j, k: (0, 0)),
                     pl.BlockSpec((dm.shape[0], N_META, tn), lambda j, k: (0, 0, j))]
        args += [xmt, dm]
    in_specs.append(pl.BlockSpec(memory_space=pl.ANY))
    args.append(buf)
    return pl.pallas_call(
        body, name=name, grid=(n // tn, nk),
        out_shape=_sds(buf.shape, F32),
        in_specs=in_specs,
        out_specs=pl.BlockSpec((1, D, tn), lambda j, k: (slot, 0, j0 + j)),
        input_output_aliases={len(args) - 1: 0},
        compiler_params=_params(("parallel", "arbitrary"), 32),
    )(*_pin(*args))


def _rope_tables(s_len):
    pos = jnp.arange(s_len, dtype=jnp.int32)
    row_ids = (pos // GRID_W).astype(F32)
    col_ids = (pos % GRID_W).astype(F32)
    inv_freq = ROPE_THETA ** (-jnp.arange(ROPE_FREQS, dtype=F32) / ROPE_FREQS)
    a_row = row_ids[:, None] * inv_freq[None, :]
    a_col = col_ids[:, None] * inv_freq[None, :]
    ang = jnp.concatenate([a_row, a_row, a_col, a_col], axis=-1)
    return jnp.tile(jnp.cos(ang), (1, GQA)), jnp.tile(jnp.sin(ang), (1, GQA))


def _local_step(x, loss_target, norm_g, conv_b, cn_g, cn_b, q_g, k_g, w_full, w3_full, conv_w_full, meta_full,
                reduce_start=None):
    nb, s_len, _ = x.shape
    rows = nb * s_len
    x2 = x.reshape(rows, D)
    t2 = loss_target.reshape(rows, D)
    cos, sin = _rope_tables(s_len)
    qg = jnp.tile(q_g, (1, GQA))
    kg = jnp.tile(k_g, (1, N_KV))

    xnmt, pm_conv, pm_kv = _meta_fwd(meta_full, norm_g, w_full)
    pconv, pcz, pq, pkv, pe, xnt = _in_proj(x2, norm_g, w_full, 256)
    c0 = _conv_fwd(pconv, pm_conv, conv_w_full, conv_b, nb, 256)
    kr, ve = _kv_prep(pkv, pm_kv, kg, cos, sin, nb)
    q2, qst = _q_prep(pq, qg, cos, sin, nb, 256)
    o, lse = _attn_fwd(q2, kr, ve, nb, 256)
    dy, mt, c3t, o2t, dyc, dya, d_o, dc0, dcz, de, sums = _mid(x2, t2, c0, pcz, o, pe, w3_full, cn_g, cn_b, 128)
    doe, dot_ = _do_prep(d_o, o, 256)
    dqr, dkt, dvt = _attn_bwd(q2, qst, kr, ve, doe, dot_, lse, nb, 256)
    dq, dqg = _q_post(dqr, pq, qg, cos, sin, nb, 256)
    lk = s_len + KEY_PAD
    dd, ddm, dkg = _kv_bwd(dkt.reshape(nb, GROUP_W, lk), dvt.reshape(nb, GROUP_W, lk), pkv, pm_kv, kg, cos, sin, nb)
    da, dam, gcw = _conv_bwd(dc0, pconv, pm_conv, conv_w_full, nb, 256)
    gmeta, dng_m = _meta_bwd(dam, ddm, w_full, meta_full, norm_g)

    gw3 = lax.empty((3, D, D), F32)
    gw3 = _wgrad(c3t, dyc, gw3, 0, 0, "wgrad_conv_out")
    gw3 = _wgrad(o2t, dya, gw3, 1, 0, "wgrad_attn_out")
    gw3 = _wgrad(mt, dy, gw3, 2, 0, "wgrad_out")
    gwin = lax.empty((1, D, IN_DIM), F32)
    gwin = _wgrad(xnt, da, gwin, 0, G_CONV[0], "wgrad_in_conv", meta=(xnmt, dam))
    gwin = _wgrad(xnt, dcz, gwin, 0, G_CZ[0], "wgrad_in_cz")
    gwin = _wgrad(xnt, dq, gwin, 0, G_Q[0], "wgrad_in_q")
    gwin = _wgrad(xnt, dd, gwin, 0, G_KV[0], "wgrad_in_kv", meta=(xnmt, ddm))
    gwin = _wgrad(xnt, de, gwin, 0, G_E[0], "wgrad_in_e")

    pending = None
    if reduce_start is not None:
        token, pending = reduce_start(gwin, gw3, gcw, gmeta)
        dng_m = dng_m + token[0:1, 0:1]
    gx, dng = _dxn((da, dcz, dq, dd, de), w_full, x2, dy, norm_g, dng_m, 256)

    zeros = jnp.zeros((1, D - 2 * GROUP_W), F32)
    smalls = jnp.concatenate([dng, sums[2:3], sums[0:1], sums[1:2], jnp.concatenate([dqg, dkg, zeros], axis=1),
                              sums[3:4], jnp.zeros((2, D), F32)], axis=0)
    return gx.reshape(nb, s_len, D), gwin, gw3, gcw, gmeta, smalls, pending


def _xyc():
    return lax.axis_index("x"), lax.axis_index("y"), lax.axis_index("c")


def _reduce_sibling(gwin, gw3v, gcm):
    def body(gwin_ref, gw3_ref, gcm_ref, r_win, r_w3, r_cm, send, recv):
        x, y, c = _xyc()
        o = 1 - c
        half = D // 2
        outs = ((gwin_ref.at[pl.ds(o * half, half), :], r_win), (gw3_ref.at[:, :, o], r_w3), (gcm_ref.at[o], r_cm))
        cps = []
        for a, (src, dst) in enumerate(outs):
            cp = pltpu.make_async_remote_copy(src_ref=src, dst_ref=dst, send_sem=send.at[a], recv_sem=recv.at[a],
                                              device_id=(x, y, o), device_id_type=MESH)
            cp.start()
            cps.append(cp)
        for cp in cps:
            cp.wait()

    any_spec = pl.BlockSpec(memory_space=pl.ANY)
    return pl.pallas_call(
        body, name="reduce_sibling",
        out_shape=(_sds((D // 2, IN_DIM), F32), _sds((3, 4, 128, D), F32),
                   _sds((24, D), F32)),
        in_specs=[any_spec] * 3, out_specs=(any_spec,) * 3,
        scratch_shapes=[pltpu.SemaphoreType.DMA((3,)), pltpu.SemaphoreType.DMA((3,))],
    )(*_pin(gwin, gw3v, gcm))


def _add_sibling(gwin, gw3v, gcm, r_win, r_w3, r_cm):
    c = lax.axis_index("c").astype(jnp.int32).reshape(1)
    half = D // 2
    tr = 64

    def body1(c_ref, a_ref, b_ref, o_ref):
        o_ref[...] = (a_ref[...] + b_ref[...]).astype(BF16)

    cs_win = pl.pallas_call(
        body1, name="add_sibling_w_in", out_shape=_sds((half, IN_DIM), BF16),
        grid_spec=pltpu.PrefetchScalarGridSpec(
            num_scalar_prefetch=1, grid=(half // tr,),
            in_specs=[pl.BlockSpec((tr, IN_DIM), lambda i, c_ref: (c_ref[0] * (half // tr) + i, 0)),
                      pl.BlockSpec((tr, IN_DIM), lambda i, c_ref: (i, 0))],
            out_specs=pl.BlockSpec((tr, IN_DIM), lambda i, c_ref: (i, 0))),
        compiler_params=_params(("parallel",), 32),
    )(c, *_pin(gwin, r_win))

    def body2(c_ref, a_ref, b_ref, o_ref):
        o_ref[0, 0] = (a_ref[0, 0, 0] + b_ref[0, 0]).astype(BF16)

    cs_w3 = pl.pallas_call(
        body2, name="add_sibling_w3", out_shape=_sds((3, 4, 128, D), BF16),
        grid_spec=pltpu.PrefetchScalarGridSpec(
            num_scalar_prefetch=1, grid=(3, 4),
            in_specs=[pl.BlockSpec((1, 1, 1, 128, D), lambda w, s, c_ref: (w, s, c_ref[0], 0, 0)),
                      pl.BlockSpec((1, 1, 128, D), lambda w, s, c_ref: (w, s, 0, 0))],
            out_specs=pl.BlockSpec((1, 1, 128, D), lambda w, s, c_ref: (w, s, 0, 0))),
        compiler_params=_params(("parallel", "parallel"), 32),
    )(c, *_pin(gw3v, r_w3))

    def body3(c_ref, a_ref, b_ref, o_ref):
        o_ref[...] = a_ref[0] + b_ref[...]

    cs_cm = pl.pallas_call(
        body3, name="add_sibling_cm", out_shape=_sds((24, D), F32),
        grid_spec=pltpu.PrefetchScalarGridSpec(
            num_scalar_prefetch=1, grid=(1,),
            in_specs=[pl.BlockSpec((1, 24, D), lambda i, c_ref: (c_ref[0], 0, 0)),
                      pl.BlockSpec((24, D), lambda i, c_ref: (0, 0))],
            out_specs=pl.BlockSpec((24, D), lambda i, c_ref: (0, 0))),
        compiler_params=_params(("arbitrary",), 32),
    )(c, *_pin(gcm, r_cm))
    return cs_win, cs_w3, cs_cm


def _reduce_chips_copies(srcs, lands, send, recv):
    win_ref, w3_ref, cm_ref = srcs
    r_win, r_w3, r_cm = lands
    x, y, c = _xyc()
    peers = ((1 - x, y), (x, 1 - y), (1 - x, 1 - y))
    cps = []
    for k, (px, py) in enumerate(peers):
        ps = 2 * px + py
        items = ((win_ref.at[:, pl.ds(ps * W_IN_SHARD, W_IN_SHARD)], r_win.at[k]),
                 (w3_ref.at[:, ps], r_w3.at[k]),
                 (cm_ref.at[:, pl.ds(ps * ROW_SHARD, ROW_SHARD)], r_cm.at[k]))
        for a, (src, dst) in enumerate(items):
            cps.append(pltpu.make_async_remote_copy(src_ref=src, dst_ref=dst, send_sem=send.at[3 * a + k],
                                                    recv_sem=recv.at[3 * a + k], device_id=(px, py, c),
                                                    device_id_type=MESH))
    return cps


_HBM = pl.BlockSpec(memory_space=pltpu.HBM)
_SEM = pl.BlockSpec(memory_space=pltpu.SEMAPHORE)
_EFFECT = pltpu.SideEffectType.DATAFLOW_SIDE_EFFECTING


def _reduce_chips_start(cs_win, cs_w3, cs_cm):
    srcs = (cs_win, cs_w3, cs_cm)
    lands = (lax.empty((3, D // 2, W_IN_SHARD), BF16), lax.empty((3, 3, 128, D), BF16),
             lax.empty((3, 24, ROW_SHARD), F32))

    def body(*refs):
        srcs_in, lands_in, send, recv, token = refs[0:3], refs[3:6], refs[6], refs[7], refs[14]
        for cp in _reduce_chips_copies(srcs_in, lands_in, send, recv):
            cp.start()
        token[...] = jnp.zeros_like(token)

    hbm = lambda a: pltpu.HBM(a.shape, a.dtype)
    outs = pl.pallas_call(
        body, name="reduce_chips_start",
        out_shape=(pltpu.SemaphoreType.DMA((9,)), pltpu.SemaphoreType.DMA((9,)),
                   *[hbm(a) for a in srcs], *[hbm(a) for a in lands], jax.ShapeDtypeStruct((8, 128), F32)),
        in_specs=[_HBM] * 6,
        out_specs=(_SEM, _SEM, *[_HBM] * 6, pl.BlockSpec(memory_space=pltpu.VMEM)),
        input_output_aliases={i: i + 2 for i in range(6)},
        compiler_params=pltpu.CompilerParams(has_side_effects=_EFFECT),
    )(*[pltpu.with_memory_space_constraint(a, pltpu.HBM) for a in srcs + lands])
    return outs[0], outs[1], outs[2:5], outs[5:8], outs[8]


def _reduce_chips_wait(send, recv, srcs, lands, after):
    def body(*refs):
        srcs_in, lands_in, send_ref, recv_ref = refs[0:3], refs[3:6], refs[6], refs[7]
        for cp in _reduce_chips_copies(srcs_in, lands_in, send_ref, recv_ref):
            cp.wait_send()
            cp.wait_recv()

    hbm = lambda a: pltpu.HBM(a.shape, a.dtype)
    outs = pl.pallas_call(
        body, name="reduce_chips_wait",
        out_shape=(*[hbm(a) for a in srcs], *[hbm(a) for a in lands]),
        in_specs=[_HBM] * 6 + [_SEM, _SEM, pl.BlockSpec(memory_space=pl.ANY)],
        out_specs=(_HBM,) * 6,
        input_output_aliases={i: i for i in range(6)},
        compiler_params=pltpu.CompilerParams(has_side_effects=_EFFECT),
    )(*srcs, *lands, send, recv, after)
    return outs[0:3], outs[3:6]


def _add_chips(cs_win, cs_w3, cs_cm, r_win, r_w3, r_cm):
    x, y, c = _xyc()
    idx = jnp.stack([2 * x + y, c]).astype(jnp.int32)
    half = D // 2
    tr = 128

    def body1(i_ref, a_ref, b_ref, o_ref):
        f = lambda v: v.astype(F32)
        o_ref[0] = (f(a_ref[...]) + f(b_ref[2])) + (f(b_ref[0]) + f(b_ref[1]))

    f_win = pl.pallas_call(
        body1, name="add_chips_w_in", out_shape=_sds((2, half, W_IN_SHARD), F32),
        grid_spec=pltpu.PrefetchScalarGridSpec(
            num_scalar_prefetch=1, grid=(half // tr,),
            in_specs=[pl.BlockSpec((tr, W_IN_SHARD), lambda i, r: (i, r[0])),
                      pl.BlockSpec((3, tr, W_IN_SHARD), lambda i, r: (0, i, 0))],
            out_specs=pl.BlockSpec((1, tr, W_IN_SHARD), lambda i, r: (r[1], i, 0))),
        compiler_params=_params(("parallel",), 32),
    )(idx, *_pin(cs_win, r_win))

    def body2(i_ref, a_ref, b_ref, o_ref):
        f = lambda v: v.astype(F32)
        o_ref[0, 0] = (f(a_ref[0, 0]) + f(b_ref[2, 0])) + (f(b_ref[0, 0]) + f(b_ref[1, 0]))

    f_w3 = pl.pallas_call(
        body2, name="add_chips_w3", out_shape=_sds((3, 2, 128, D), F32),
        grid_spec=pltpu.PrefetchScalarGridSpec(
            num_scalar_prefetch=1, grid=(3,),
            in_specs=[pl.BlockSpec((1, 1, 128, D), lambda w, r: (w, r[0], 0, 0)),
                      pl.BlockSpec((3, 1, 128, D), lambda w, r: (0, w, 0, 0))],
            out_specs=pl.BlockSpec((1, 1, 128, D), lambda w, r: (w, r[1], 0, 0))),
        compiler_params=_params(("parallel",), 32),
    )(idx, *_pin(cs_w3, r_w3))

    def body3(i_ref, a_ref, b_ref, o_ref):
        o_ref[0] = (a_ref[...] + b_ref[2]) + (b_ref[0] + b_ref[1])

    f_cm = pl.pallas_call(
        body3, name="add_chips_cm", out_shape=_sds((2, 24, ROW_SHARD), F32),
        grid_spec=pltpu.PrefetchScalarGridSpec(
            num_scalar_prefetch=1, grid=(1,),
            in_specs=[pl.BlockSpec((24, ROW_SHARD), lambda i, r: (0, r[0])),
                      pl.BlockSpec((3, 24, ROW_SHARD), lambda i, r: (0, 0, 0))],
            out_specs=pl.BlockSpec((1, 24, ROW_SHARD), lambda i, r: (r[1], 0, 0))),
        compiler_params=_params(("arbitrary",), 32),
    )(idx, *_pin(cs_cm, r_cm))
    return f_win, f_w3, f_cm


def _share_sibling(f_win, f_w3, f_cm, smalls):
    def body(win_in, w3_in, cm_in, sm_ref, win_ref, w3_ref, cm_ref, r_sm, send, recv, ssend, srecv, lsem):
        x, y, c = _xyc()
        o = 1 - c
        cps = []
        for a, (ref, sl) in enumerate(((win_ref, lambda h: win_ref.at[h]), (w3_ref, lambda h: w3_ref.at[:, h]),
                                       (cm_ref, lambda h: cm_ref.at[h]))):
            cp = pltpu.make_async_remote_copy(src_ref=sl(c), dst_ref=sl(c), send_sem=send.at[a], recv_sem=recv.at[a],
                                              device_id=(x, y, o), device_id_type=MESH)
            cp.start()
            cps.append((cp, sl))
        me = 4 * x + 2 * y + c
        loc = pltpu.make_async_copy(sm_ref, r_sm.at[me], lsem)
        loc.start()
        scps = []
        for d in range(1, 8):
            px, py, pc = (x + (d >> 2)) % 2, (y + ((d >> 1) & 1)) % 2, (c + (d & 1)) % 2
            cp = pltpu.make_async_remote_copy(src_ref=sm_ref, dst_ref=r_sm.at[me], send_sem=ssend.at[d - 1],
                                              recv_sem=srecv.at[d - 1], device_id=(px, py, pc), device_id_type=MESH)
            cp.start()
            scps.append((cp, 4 * px + 2 * py + pc))
        for a, (cp, sl) in enumerate(cps):
            pltpu.make_async_remote_copy(src_ref=sl(o), dst_ref=sl(o), send_sem=send.at[a], recv_sem=recv.at[a],
                                         device_id=(x, y, o), device_id_type=MESH).wait_recv()
            cp.wait_send()
        for d, (cp, pid) in enumerate(scps):
            pltpu.make_async_remote_copy(src_ref=sm_ref, dst_ref=r_sm.at[pid], send_sem=ssend.at[d],
                                         recv_sem=srecv.at[d], device_id=(x, y, c), device_id_type=MESH).wait_recv()
            cp.wait_send()
        loc.wait()

    any_spec = pl.BlockSpec(memory_space=pl.ANY)
    return pl.pallas_call(
        body, name="share_sibling",
        out_shape=(_sds(f_win.shape, F32), _sds(f_w3.shape, F32), _sds(f_cm.shape, F32), _sds((8, 8, D), F32)),
        in_specs=[any_spec] * 4, out_specs=(any_spec,) * 4,
        input_output_aliases={0: 0, 1: 1, 2: 2},
        scratch_shapes=[pltpu.SemaphoreType.DMA((3,)), pltpu.SemaphoreType.DMA((3,)),
                        pltpu.SemaphoreType.DMA((7,)), pltpu.SemaphoreType.DMA((7,)), pltpu.SemaphoreType.DMA],
    )(*_pin(f_win, f_w3, f_cm, smalls))


def _adamw_math(w, g, m, v):
    m = ADAM_B1 * m + (1.0 - ADAM_B1) * g
    v = ADAM_B2 * v + (1.0 - ADAM_B2) * (g * g)
    m_hat = m / (1.0 - ADAM_B1 ** ADAM_STEP)
    v_hat = v / (1.0 - ADAM_B2 ** ADAM_STEP)
    delta = -ADAM_LR * (m_hat / (jnp.sqrt(v_hat) + ADAM_EPS) + ADAM_WD * w)
    return delta, m, v


def _adamw(w, g, m, v, tr, name):
    rows, cols = w.shape

    def body(w_ref, g_ref, m_ref, v_ref, d_o, m_o, v_o):
        d_o[...], m_o[...], v_o[...] = _adamw_math(w_ref[...], g_ref[...], m_ref[...], v_ref[...])

    spec = pl.BlockSpec((tr, cols), lambda i: (i, 0))
    return pl.pallas_call(
        body, name=name, grid=(rows // tr,),
        out_shape=(_sds((rows, cols), F32),) * 3,
        in_specs=[spec] * 4, out_specs=(spec,) * 3,
        compiler_params=_params(("parallel",), 32),
    )(*_pin(w, g, m, v))


def _adamw3(g3, ws, ms, vs):
    def body(g_ref, *refs):
        w_refs, m_refs, v_refs, outs = refs[0:3], refs[3:6], refs[6:9], refs[9:]
        g_os, d_os, m_os, v_os = outs[0:3], outs[3:6], outs[6:9], outs[9:12]
        for i in range(3):
            g = g_ref[i]
            g_os[i][0] = g
            d_os[i][0], m_os[i][0], v_os[i][0] = _adamw_math(w_refs[i][0], g, m_refs[i][0], v_refs[i][0])

    return pl.pallas_call(
        body, name="adamw_w3", out_shape=(jax.ShapeDtypeStruct((1, ROW_SHARD, D), F32),) * 12,
        compiler_params=pltpu.CompilerParams(vmem_limit_bytes=48 << 20),
    )(g3, *ws, *ms, *vs)


def _adamw_cm(f_cm, ws, ms, vs):
    def body(f_ref, *refs):
        w_refs, m_refs, v_refs, outs = refs[0:2], refs[2:4], refs[4:6], refs[6:14]
        gcw, gmt = refs[14], refs[15]
        gcw[0:16] = f_ref[0, 0:16]
        gcw[16:32] = f_ref[1, 0:16]
        gmt[0:8] = f_ref[0, 16:24]
        gmt[8:16] = f_ref[1, 16:24]
        g_conv = gcw[0:CONV_K, :]
        g_meta = gmt[...]
        outs[0][0] = g_conv
        outs[1][...] = g_meta
        outs[2][0], outs[4][0], outs[6][0] = _adamw_math(w_refs[0][0], g_conv, m_refs[0][0], v_refs[0][0])
        outs[3][...], outs[5][...], outs[7][...] = _adamw_math(w_refs[1][...], g_meta, m_refs[1][...], v_refs[1][...])

    pair = (jax.ShapeDtypeStruct((1, CONV_K, ROW_SHARD), F32), jax.ShapeDtypeStruct((N_META, ROW_SHARD), F32))
    return pl.pallas_call(
        body, name="adamw_cm", out_shape=pair * 4,
        scratch_shapes=[pltpu.VMEM((32, ROW_SHARD), F32), pltpu.VMEM((N_META, ROW_SHARD), F32)],
    )(f_cm, *ws, *ms, *vs)


def _adamw_small(r_sm, ws, ms, vs):
    def body(s_ref, *refs):
        w_refs, m_refs, v_refs, outs = refs[0:6], refs[6:12], refs[12:18], refs[18:]
        loss_o, g_os, d_os, m_os, v_os = outs[0], outs[1:7], outs[7:13], outs[13:19], outs[19:25]
        g = s_ref[0]
        for dev in range(1, 8):
            g = g + s_ref[dev]
        qk = g[4:5, :]
        qg = qk[:, 0:HEAD_DIM]
        kg = qk[:, GROUP_W:GROUP_W + HEAD_DIM]
        for h in range(1, GQA):
            qg = qg + qk[:, HEAD_DIM * h:HEAD_DIM * (h + 1)]
            kg = kg + qk[:, GROUP_W + HEAD_DIM * h:GROUP_W + HEAD_DIM * (h + 1)]
        loss_o[...] = (0.5 / D) * jnp.sum(g[5:6, :], axis=-1, keepdims=True)
        for i, gi in enumerate((g[0:1], g[1:2], g[2:3], g[3:4], qg, kg)):
            g_os[i][...] = gi
            d_os[i][...], m_os[i][...], v_os[i][...] = _adamw_math(w_refs[i][...], gi, m_refs[i][...], v_refs[i][...])

    six = tuple(jax.ShapeDtypeStruct(w.shape, F32) for w in ws)
    return pl.pallas_call(
        body, name="adamw_small", out_shape=(jax.ShapeDtypeStruct((1, 1), F32),) + six * 4,
    )(r_sm, *ws, *ms, *vs)


def kernel(x, meta_tokens, norm_g, w_in, conv_w, conv_b, conv_norm_g, conv_norm_b, w_conv_out, q_norm_g, k_norm_g, w_attn_out, w_out, loss_target, m_meta_tokens, m_norm_g, m_w_in, m_conv_w, m_conv_b, m_conv_norm_g, m_conv_norm_b, m_w_conv_out, m_q_norm_g, m_k_norm_g, m_w_attn_out, m_w_out, v_meta_tokens, v_norm_g, v_w_in, v_conv_w, v_conv_b, v_conv_norm_g, v_conv_norm_b, v_w_conv_out, v_q_norm_g, v_k_norm_g, v_w_attn_out, v_w_out):
    pad_k = lambda a: jnp.pad(a[0], ((0, 32 - CONV_K), (0, 0)))
    w3_s = (w_conv_out, w_attn_out, w_out)
    w_full, w3_full, conv_w_full, meta_full = _gather_weights(w_in[0], w3_s, pad_k(conv_w), meta_tokens)

    def reduce_start(gwin, gw3, gcw, gmeta):
        gwin2 = gwin.reshape(D, IN_DIM)
        gw3v = gw3.reshape(3, N_CHIPS, 2, 128, D)
        gcm = jnp.concatenate([gcw.reshape(2, 16, D), gmeta.reshape(2, 8, D)], axis=1)
        r_win, r_w3, r_cm = _reduce_sibling(gwin2, gw3v, gcm)
        cs = _add_sibling(gwin2, gw3v, gcm, r_win, r_w3, r_cm)
        send, recv, srcs, lands, token = _reduce_chips_start(*cs)
        return token, (send, recv, srcs, lands)

    gx, _, _, _, _, smalls, pending = _local_step(
        x, loss_target, norm_g, conv_b, conv_norm_g, conv_norm_b, q_norm_g, k_norm_g,
        w_full, w3_full, conv_w_full, meta_full, reduce_start)
    (cs_win, cs_w3, cs_cm), (r2_win, r2_w3, r2_cm) = _reduce_chips_wait(*pending, gx)
    f_win, f_w3, f_cm = _add_chips(cs_win, cs_w3, cs_cm, r2_win, r2_w3, r2_cm)
    f_win, f_w3, f_cm, r_sm = _share_sibling(f_win, f_w3, f_cm, smalls)

    g_w_in = f_win.reshape(D, W_IN_SHARD)
    d_w_in, nm_w_in, nv_w_in = _adamw(w_in[0], g_w_in, m_w_in[0], v_w_in[0], 128, "adamw_w_in")
    w3 = _adamw3(f_w3.reshape(3, ROW_SHARD, D), w3_s, (m_w_conv_out, m_w_attn_out, m_w_out),
                 (v_w_conv_out, v_w_attn_out, v_w_out))
    cm = _adamw_cm(f_cm, (conv_w, meta_tokens), (m_conv_w, m_meta_tokens), (v_conv_w, v_meta_tokens))
    small = _adamw_small(
        r_sm, (norm_g, conv_b, conv_norm_g, conv_norm_b, q_norm_g, k_norm_g),
        (m_norm_g, m_conv_b, m_conv_norm_g, m_conv_norm_b, m_q_norm_g, m_k_norm_g),
        (v_norm_g, v_conv_b, v_conv_norm_g, v_conv_norm_b, v_q_norm_g, v_k_norm_g))

    def assemble(big_in, w3x, cmx, s6):
        ng, cb, cng, cnb, qg, kg = s6
        return (cmx[1], ng, big_in[None], cmx[0], cb, cng, cnb, w3x[0], qg, kg, w3x[1], w3x[2])

    loss = small[0].reshape(())
    grads = assemble(g_w_in, w3[0:3], cm[0:2], small[1:7])
    deltas = assemble(d_w_in, w3[3:6], cm[2:4], small[7:13])
    new_m = assemble(nm_w_in, w3[6:9], cm[4:6], small[13:19])
    new_v = assemble(nv_w_in, w3[9:12], cm[6:8], small[19:25])
    return (loss, gx, *grads, *deltas, *new_m, *new_v)
```

```python
import functools
import math

import jax
import jax.numpy as jnp
from jax import lax
from jax.experimental import pallas as pl
from jax.experimental.pallas import tpu as pltpu

F32, BF16 = jnp.float32, jnp.bfloat16
MESH = pl.DeviceIdType.MESH

D = 1024
N_META = 16
CONV_K = 31
N_KV = 4
GQA = 4
HEAD_DIM = 64
GROUP_W = GQA * HEAD_DIM
GRID_W = 64
ROPE_FREQS = 16
ROPE_THETA = 10000.0
EPS = 1e-6
IN_DIM = 7680
KEY_PAD = 128
G_CONV, G_CZ, G_Q, G_KV, G_E = (0, 2048), (2048, 1024), (3072, 1024), (4096, 512), (4608, 3072)
N_CHIPS = 4
W_IN_SHARD = IN_DIM // N_CHIPS
ROW_SHARD = D // N_CHIPS

ADAM_LR, ADAM_B1, ADAM_B2, ADAM_EPS, ADAM_WD, ADAM_STEP = 0.001, 0.9, 0.999, 1e-08, 0.01, 10

NT_DIMS = (((1,), (1,)), ((), ()))


def _params(sem=None, vmem_mb=48):
    return pltpu.CompilerParams(dimension_semantics=sem, vmem_limit_bytes=vmem_mb << 20)


def _sds(shape, dtype):
    return pltpu.HBM(tuple(shape), dtype)


def _pin(*arrays):
    return [pltpu.with_memory_space_constraint(a, pltpu.HBM) for a in arrays]


def _sig(v):
    return jax.nn.sigmoid(v)


def _dsilu(v, s):
    return s * (1.0 + v * (1.0 - s))


def _dot(a, b):
    return jnp.dot(a, b, preferred_element_type=F32)


def _dot_nt(a, b):
    return lax.dot_general(a, b, NT_DIMS, preferred_element_type=F32)


def _qk_mats():
    i = lax.broadcasted_iota(jnp.int32, (GROUP_W, GROUP_W), 0)
    j = lax.broadcasted_iota(jnp.int32, (GROUP_W, GROUP_W), 1)
    mean = jnp.where((i >> 6) == (j >> 6), 1.0 / HEAD_DIM, 0.0).astype(BF16)
    turn = jnp.where((i == j + 16) & ((j & 16) == 0), -1.0,
                     jnp.where((i == j - 16) & ((j & 16) != 0), 1.0, 0.0)).astype(BF16)
    return mean, turn


def _apply(v, mat):
    hi = v.astype(BF16)
    lo = (v - hi.astype(F32)).astype(BF16)
    return _dot(hi, mat) + _dot(lo, mat)


def _qk_fwd(v, g, cos, sin, mats):
    mean, turn = mats
    r = lax.rsqrt(_apply(v * v, mean) + EPS)
    n = v * r * g
    return n * cos + _apply(n, turn) * sin, r


def _qk_bwd(dy, v, r, g, cos, sin, mats):
    mean, turn = mats
    dn = dy * cos - _apply(dy, turn) * sin
    dyg = dn * g
    dv = r * dyg - v * (r * r * r) * _apply(dyg * v, mean)
    return dv, dn * v * r


def _rms_bwd(dxn, v, r, g):
    dxg = dxn * g
    return r * dxg - v * (r * r * r) * jnp.mean(dxg * v, axis=-1, keepdims=True)


def _glu(a):
    return a[:, :D] * _sig(a[:, D:])


def _gather_weights(w_in_s, w3_s, conv_w_s, meta_s):
    def body(win_ref, wa_ref, wb_ref, wc_ref, cw_ref, mt_ref, win_o, w3_o, cw_o, mt_o, win_b, w3_b,
             send, recv, fsend, frecv, lsem):
        x, y, c = _xyc()
        o = 1 - c
        me = 2 * x + y
        win_b[...] = win_ref[...].astype(BF16)
        for i, ref in enumerate((wa_ref, wb_ref, wc_ref)):
            w3_b[i] = ref[0].astype(BF16)
        items = (
            (lambda h: win_b.at[pl.ds(h * 512, 512), :],
             lambda p, h: win_o.at[pl.ds(h * 512, 512), pl.ds(p * W_IN_SHARD, W_IN_SHARD)]),
            (lambda h: w3_b.at[:, pl.ds(h * 128, 128), :],
             lambda p, h: w3_o.at[:, pl.ds(p * ROW_SHARD + h * 128, 128), :]),
            (lambda h: cw_ref.at[pl.ds(h * 16, 16), :],
             lambda p, h: cw_o.at[pl.ds(h * 16, 16), pl.ds(p * ROW_SHARD, ROW_SHARD)]),
            (lambda h: mt_ref.at[pl.ds(h * 8, 8), :],
             lambda p, h: mt_o.at[pl.ds(h * 8, 8), pl.ds(p * ROW_SHARD, ROW_SHARD)]),
        )
        peers = ((1 - x, y), (x, 1 - y), (1 - x, 1 - y))

        def remote(src, dst, s_sem, r_sem, to):
            return pltpu.make_async_remote_copy(src_ref=src, dst_ref=dst, send_sem=s_sem, recv_sem=r_sem,
                                                device_id=to, device_id_type=MESH)

        started = []
        for a, (half, place) in enumerate(items):
            for h in range(2):
                loc = pltpu.make_async_copy(half(h), place(me, h), lsem.at[a, h])
                loc.start()
                started.append(loc.wait)
            for k, (px, py) in enumerate(peers):
                cp = remote(half(c), place(me, c), send.at[a, k], recv.at[a, k], (px, py, c))
                cp.start()
                started.append(cp.wait_send)
        for k, (px, py) in enumerate(peers):
            for a, (half, place) in enumerate(items):
                got = place(2 * px + py, c)
                remote(got, got, send.at[a, k], recv.at[a, k], (px, py, c)).wait_recv()
                fw = remote(got, got, fsend.at[a, k], frecv.at[a, k], (x, y, o))
                fw.start()
                started.append(fw.wait_send)
        for k, (px, py) in enumerate(peers):
            for a, (half, place) in enumerate(items):
                theirs = place(2 * px + py, o)
                remote(theirs, theirs, fsend.at[a, k], frecv.at[a, k], (x, y, o)).wait_recv()
        for wait in started:
            wait()

    any_spec = pl.BlockSpec(memory_space=pl.ANY)
    vmem = pl.BlockSpec(memory_space=pltpu.VMEM)
    return pl.pallas_call(
        body, name="gather_weights",
        out_shape=(_sds((D, IN_DIM), BF16), _sds((3, D, D), BF16),
                   _sds((32, D), F32), _sds((N_META, D), F32)),
        in_specs=[vmem] * 6,
        out_specs=(any_spec, any_spec, any_spec, any_spec),
        scratch_shapes=[pltpu.VMEM((D, W_IN_SHARD), BF16), pltpu.VMEM((3, ROW_SHARD, D), BF16),
                        pltpu.SemaphoreType.DMA((4, 3)), pltpu.SemaphoreType.DMA((4, 3)),
                        pltpu.SemaphoreType.DMA((4, 3)), pltpu.SemaphoreType.DMA((4, 3)),
                        pltpu.SemaphoreType.DMA((4, 2))],
        compiler_params=pltpu.CompilerParams(vmem_limit_bytes=40 << 20),
    )(w_in_s, *w3_s, conv_w_s, meta_s)


def _meta_fwd(meta_full, norm_g, w_full):
    def body(m_ref, g_ref, wc_ref, wkv_ref, xnt_ref, pc_ref, pkv_ref):
        v = m_ref[...]
        r = lax.rsqrt(jnp.mean(v * v, axis=-1, keepdims=True) + EPS)
        xn = v * r * g_ref[...]
        xnb = xn.astype(BF16)
        pad = jnp.concatenate([xn, jnp.zeros((128 - N_META, D), F32)], axis=0)
        xnt_ref[...] = pad.T.astype(BF16)
        pc_ref[...] = _dot(xnb, wc_ref[...])
        pkv_ref[...] = _dot(xnb, wkv_ref[...])

    return pl.pallas_call(
        body, name="meta_fwd", grid=(1,),
        out_shape=(_sds((D, 128), BF16), _sds((N_META, 2048), F32),
                   _sds((N_META, 512), F32)),
        in_specs=[pl.BlockSpec((N_META, D), lambda i: (0, 0)), pl.BlockSpec((1, D), lambda i: (0, 0)),
                  pl.BlockSpec((D, 2048), lambda i: (0, 0)), pl.BlockSpec((D, 512), lambda i: (0, G_KV[0] // 512))],
        out_specs=(pl.BlockSpec((D, 128), lambda i: (0, 0)), pl.BlockSpec((N_META, 2048), lambda i: (0, 0)),
                   pl.BlockSpec((N_META, 512), lambda i: (0, 0))),
        compiler_params=_params(("arbitrary",), 32),
    )(*_pin(meta_full, norm_g, w_full, w_full))


def _in_proj(x2, norm_g, w_full, tm):
    rows = x2.shape[0]
    groups = (G_CONV, G_CZ, G_Q, G_KV, G_E)

    def body(x_ref, g_ref, w_hbm, *rest):
        outs, xnt_ref, w_vmem, sem = rest[:5], rest[5], rest[6], rest[7]

        @pl.when(pl.program_id(0) == 0)
        def _():
            cp = pltpu.make_async_copy(w_hbm, w_vmem, sem)
            cp.start()
            cp.wait()

        v = x_ref[...]
        r = lax.rsqrt(jnp.mean(v * v, axis=-1, keepdims=True) + EPS)
        xn = v * r * g_ref[...]
        xnb = xn.astype(BF16)
        xnt_ref[...] = xn.T.astype(BF16)
        for ref, (off, wd) in zip(outs, groups):
            for c0 in range(0, wd, 512):
                ref[:, c0:c0 + 512] = _dot(xnb, w_vmem[:, off + c0:off + c0 + 512])

    return pl.pallas_call(
        body, name="in_proj", grid=(rows // tm,),
        out_shape=tuple(_sds((rows, wd), F32) for _, wd in groups)
        + (_sds((D, rows), BF16),),
        in_specs=[pl.BlockSpec((tm, D), lambda i: (i, 0)), pl.BlockSpec((1, D), lambda i: (0, 0)),
                  pl.BlockSpec(memory_space=pl.ANY)],
        out_specs=tuple(pl.BlockSpec((tm, wd), lambda i: (i, 0)) for _, wd in groups)
        + (pl.BlockSpec((D, tm), lambda i: (0, i)),),
        scratch_shapes=[pltpu.VMEM((D, IN_DIM), BF16), pltpu.SemaphoreType.DMA],
        compiler_params=_params(("arbitrary",), 56),
    )(*_pin(x2, norm_g, w_full))


def _halo_specs(width, tm, nt, rows):
    h16 = tm // 16
    return [pl.BlockSpec((tm, width), lambda b, i: (b * nt + i, 0)),
            pl.BlockSpec((16, width), lambda b, i: (jnp.maximum((b * nt + i) * h16 - 1, 0), 0)),
            pl.BlockSpec((16, width), lambda b, i: (jnp.minimum((b * nt + i + 1) * h16, rows // 16 - 1), 0))]


def _fill_uext(uext, cur, prev, nxt, meta, i, nt, tm):
    uext[0:16] = jnp.where(i == 0, _glu(meta[...]), _glu(prev[...]))
    uext[16:16 + tm] = _glu(cur[...])
    uext[16 + tm:32 + tm] = jnp.where(i == nt - 1, 0.0, _glu(nxt[...]))


def _shifted_copies(dst, src, n):
    for r in range(1, 8):
        dst[r, 0:n] = src[r:r + n]


def _rows32(shifted, src, start, cols):
    q8, r = divmod(start, 8)
    if r == 0:
        return src[start:start + 32, cols]
    return shifted[r, 8 * q8:8 * q8 + 32, cols]


def _conv_fwd(pconv, pm_conv, conv_w, conv_b, nb, tm):
    rows = pconv.shape[0]
    nt = rows // nb // tm

    def body(cur, prev, nxt, meta, w_ref, b_ref, o_ref, uext, ush):
        i = pl.program_id(1)
        _fill_uext(uext, cur, prev, nxt, meta, i, nt, tm)
        _shifted_copies(ush, uext, tm + 24)
        for r0 in range(0, tm, 32):
            for c0 in range(0, D, 256):
                acc = jnp.zeros((32, 256), F32) + b_ref[:, c0:c0 + 256]
                for j in range(CONV_K):
                    acc = acc + _rows32(ush, uext, r0 + j + 1, slice(c0, c0 + 256)) * w_ref[j:j + 1, c0:c0 + 256]
                o_ref[r0:r0 + 32, c0:c0 + 256] = acc

    return pl.pallas_call(
        body, name="conv_fwd", grid=(nb, nt),
        out_shape=_sds((rows, D), F32),
        in_specs=_halo_specs(2048, tm, nt, rows)
        + [pl.BlockSpec((16, 2048), lambda b, i: (0, 0)), pl.BlockSpec((32, D), lambda b, i: (0, 0)),
           pl.BlockSpec((1, D), lambda b, i: (0, 0))],
        out_specs=pl.BlockSpec((tm, D), lambda b, i: (b * nt + i, 0)),
        scratch_shapes=[pltpu.VMEM((tm + 32, D), F32), pltpu.VMEM((8, tm + 24, D), F32)],
        compiler_params=_params(("parallel", "parallel"), 40),
    )(*_pin(pconv, pconv, pconv, pm_conv, conv_w, conv_b))


def _kv_prep(pkv, pm_kv, kg, cos, sin, nb):
    rows = pkv.shape[0]
    s_len = rows // nb
    tk = min(512, s_len)
    nt = s_len // tk

    def body(kv_ref, m_ref, g_ref, cos_ref, sin_ref, k_o, v_o, k2_o, v2_o):
        i = pl.program_id(1)
        mats = _qk_mats()
        kv = kv_ref[...]
        kr, _ = _qk_fwd(kv[:, :GROUP_W], g_ref[...], cos_ref[...], sin_ref[...], mats)
        ones = _ones_cols(tk, tk)
        for h in range(N_KV):
            k_o[0, h] = kr[:, HEAD_DIM * h:HEAD_DIM * (h + 1)].astype(BF16)
            vh = kv[:, GROUP_W + HEAD_DIM * h:GROUP_W + HEAD_DIM * (h + 1)]
            v_o[0, h] = jnp.concatenate([vh, ones], axis=1).astype(BF16)

        @pl.when(i == 0)
        def _():
            kvm = m_ref[...]
            km = kvm[:, :GROUP_W]
            kn = km * lax.rsqrt(_apply(km * km, mats[0]) + EPS) * g_ref[...]
            zeros = jnp.zeros((KEY_PAD - N_META, GROUP_W), F32)
            kfull = jnp.concatenate([kn, zeros], axis=0)
            vfull = jnp.concatenate([kvm[:, GROUP_W:], zeros], axis=0)
            ones_m = _ones_cols(KEY_PAD, N_META)
            for h in range(N_KV):
                k2_o[0, h] = kfull[:, HEAD_DIM * h:HEAD_DIM * (h + 1)].astype(BF16)
                v2_o[0, h] = jnp.concatenate([vfull[:, HEAD_DIM * h:HEAD_DIM * (h + 1)], ones_m], axis=1).astype(BF16)

    return pl.pallas_call(
        body, name="kv_prep", grid=(nb, nt),
        out_shape=(_sds((nb, N_KV, s_len, HEAD_DIM), BF16), _sds((nb, N_KV, s_len, 2 * HEAD_DIM), BF16),
                   _sds((nb, N_KV, KEY_PAD, HEAD_DIM), BF16), _sds((nb, N_KV, KEY_PAD, 2 * HEAD_DIM), BF16)),
        in_specs=[pl.BlockSpec((tk, 512), lambda b, i: (b * nt + i, 0)),
                  pl.BlockSpec((N_META, 512), lambda b, i: (0, 0)), pl.BlockSpec((1, GROUP_W), lambda b, i: (0, 0)),
                  pl.BlockSpec((tk, GROUP_W), lambda b, i: (i, 0)),
                  pl.BlockSpec((tk, GROUP_W), lambda b, i: (i, 0))],
        out_specs=(pl.BlockSpec((1, N_KV, tk, HEAD_DIM), lambda b, i: (b, 0, i, 0)),
                   pl.BlockSpec((1, N_KV, tk, 2 * HEAD_DIM), lambda b, i: (b, 0, i, 0)),
                   pl.BlockSpec((1, N_KV, KEY_PAD, HEAD_DIM), lambda b, i: (b, 0, 0, 0)),
                   pl.BlockSpec((1, N_KV, KEY_PAD, 2 * HEAD_DIM), lambda b, i: (b, 0, 0, 0))),
        compiler_params=_params(("parallel", "arbitrary"), 40),
    )(*_pin(pkv, pm_kv, kg, cos, sin))


def _ones_cols(rows, valid):
    r = lax.broadcasted_iota(jnp.int32, (rows, HEAD_DIM), 0)
    col = lax.broadcasted_iota(jnp.int32, (rows, HEAD_DIM), 1)
    return jnp.where((col < 2) & (r < valid), 1.0, 0.0).astype(F32)


def _tail_bias():
    col = lax.broadcasted_iota(jnp.int32, (1, KEY_PAD), 1)
    return jnp.where(col < N_META, 0.0, -1e30).astype(F32)


LOG2E = 1.4426950408889634


def _q_prep(pq, qg, cos, sin, nb, tm):
    rows = pq.shape[0]
    nt = rows // nb // tm
    scale = 1.0 / math.sqrt(HEAD_DIM)

    def body(q_ref, g_ref, cos_ref, sin_ref, q2_o, qt_o):
        gv, cosv, sinv = g_ref[...], cos_ref[...], sin_ref[...]
        mats = _qk_mats()
        for g in range(N_KV):
            gs = slice(GROUP_W * g, GROUP_W * (g + 1))
            qr, _ = _qk_fwd(q_ref[:, gs], gv, cosv, sinv, mats)
            q2_o[:, gs] = (qr * (scale * LOG2E)).astype(BF16)
            qt_o[gs, :] = (qr * scale).T.astype(BF16)

    row = pl.BlockSpec((tm, D), lambda b, i: (b * nt + i, 0))
    rope = pl.BlockSpec((tm, GROUP_W), lambda b, i: (i, 0))
    return pl.pallas_call(
        body, name="q_prep", grid=(nb, nt),
        out_shape=(_sds((rows, D), BF16), _sds((D, rows), BF16)),
        in_specs=[row, pl.BlockSpec((1, GROUP_W), lambda b, i: (0, 0)), rope, rope],
        out_specs=(row, pl.BlockSpec((D, tm), lambda b, i: (0, b * nt + i))),
        compiler_params=_params(("parallel", "parallel"), 32),
    )(*_pin(pq, qg, cos, sin))


def _kv_specs(s_len):
    return [pl.BlockSpec((1, 1, s_len, HEAD_DIM), lambda b, g, i: (b, g, 0, 0)),
            pl.BlockSpec((1, 1, s_len, 2 * HEAD_DIM), lambda b, g, i: (b, g, 0, 0)),
            pl.BlockSpec((1, 1, KEY_PAD, HEAD_DIM), lambda b, g, i: (b, g, 0, 0)),
            pl.BlockSpec((1, 1, KEY_PAD, 2 * HEAD_DIM), lambda b, g, i: (b, g, 0, 0))]


def _attn_fwd(q2, kv4, nb, tq):
    rows = q2.shape[0]
    s_len = rows // nb
    nq = s_len // tq

    def body(q_ref, k1_ref, v1_ref, k2_ref, v2_ref, o_ref, lse_ref):
        qs = q_ref[...]
        k1, k2, v1, v2 = k1_ref[0, 0], k2_ref[0, 0], v1_ref[0, 0], v2_ref[0, 0]
        bias = _tail_bias()
        outs, lses = [], []

        def scores(h):
            qh = qs[:, HEAD_DIM * h:HEAD_DIM * (h + 1)]
            return _dot_nt(qh, k1), _dot_nt(qh, k2) + bias

        ahead = scores(0)
        for h in range(GQA):
            s1, s2 = ahead
            if h + 1 < GQA:
                ahead = scores(h + 1)
            m = jnp.maximum(jnp.max(s1, axis=-1, keepdims=True), jnp.max(s2, axis=-1, keepdims=True))
            oe = _dot(jnp.exp2(s1 - m).astype(BF16), v1) + _dot(jnp.exp2(s2 - m).astype(BF16), v2)
            l = oe[:, HEAD_DIM:HEAD_DIM + 1]
            outs.append(oe[:, :HEAD_DIM] / l)
            lses.append(m + jnp.log2(l))
        o_ref[...] = jnp.concatenate(outs, axis=1)
        lse_ref[0, 0] = jnp.concatenate(lses, axis=1)

    return pl.pallas_call(
        body, name="attn_fwd", grid=(nb, N_KV, nq),
        out_shape=(_sds((rows, D), F32), _sds((nb, N_KV, s_len, GQA), F32)),
        in_specs=[pl.BlockSpec((tq, GROUP_W), lambda b, g, i: (b * nq + i, g))] + _kv_specs(s_len),
        out_specs=(pl.BlockSpec((tq, GROUP_W), lambda b, g, i: (b * nq + i, g)),
                   pl.BlockSpec((1, 1, tq, GQA), lambda b, g, i: (b, g, i, 0))),
        compiler_params=_params(("parallel", "parallel", "parallel"), 48),
    )(*_pin(q2, *kv4))


def _mid(x2, t2, c0, cz, o, e, w3, cn_g, cn_b, tm):
    rows = x2.shape[0]

    def body(x_ref, t_ref, c0_ref, cz_ref, o_ref, e_ref, w_ref, g_ref, b_ref,
             dy_o, mt_o, c3t_o, o2t_o, dyc_o, dya_o, do_o, dc0_o, dcz_o, de_o, sums_o):
        wco, wao, wo = w_ref[0], w_ref[1], w_ref[2]
        cn_g_v = g_ref[...]
        c0v = c0_ref[...]
        xc = c0v - jnp.mean(c0v, axis=-1, keepdims=True)
        rstd = lax.rsqrt(jnp.mean(xc * xc, axis=-1, keepdims=True) + EPS)
        n = xc * rstd
        c1 = n * cn_g_v + b_ref[...]
        s1 = _sig(c1)
        c2 = c1 * s1
        czv = cz_ref[...]
        sz = _sig(czv)
        gz = czv * sz
        c3 = c2 * gz
        yc = _dot(c3.astype(BF16), wco)
        az, gc, ga = e_ref[:, :D], e_ref[:, D:2 * D], e_ref[:, 2 * D:]
        saz = _sig(az)
        gaz = az * saz
        ov = o_ref[...]
        o2 = ov * gaz
        ya = _dot(o2.astype(BF16), wao)
        sc, sa = _sig(gc), _sig(ga)
        merged = sc * yc + sa * ya
        out = _dot(merged.astype(BF16), wo)
        err = x_ref[...] + out - t_ref[...]
        dy = err * (1.0 / D)
        dy_o[...] = dy
        dm = _dot_nt(dy.astype(BF16), wo)
        dyc = dm * sc
        dya = dm * sa
        dycb, dyab = dyc.astype(BF16), dya.astype(BF16)
        dyc_o[...] = dycb
        dya_o[...] = dyab
        de_o[:, D:2 * D] = (dyc * yc * (1.0 - sc)).astype(BF16)
        de_o[:, 2 * D:] = (dya * ya * (1.0 - sa)).astype(BF16)
        dc3 = _dot_nt(dycb, wco)
        do2 = _dot_nt(dyab, wao)
        do_o[...] = do2 * gaz
        de_o[:, :D] = (do2 * ov * _dsilu(az, saz)).astype(BF16)
        dcz_o[...] = (dc3 * c2 * _dsilu(czv, sz)).astype(BF16)
        dc1 = dc3 * gz * _dsilu(c1, s1)
        dn = dc1 * cn_g_v
        dc0 = rstd * (dn - jnp.mean(dn, axis=-1, keepdims=True) - n * jnp.mean(dn * n, axis=-1, keepdims=True))
        dc0_o[...] = dc0
        mt_o[...] = merged.T.astype(BF16)
        c3t_o[...] = c3.T.astype(BF16)
        o2t_o[...] = o2.T.astype(BF16)

        @pl.when(pl.program_id(0) == 0)
        def _():
            sums_o[...] = jnp.zeros_like(sums_o)

        sums_o[0:1, :] += jnp.sum(dc1 * n, axis=0, keepdims=True)
        sums_o[1:2, :] += jnp.sum(dc1, axis=0, keepdims=True)
        sums_o[2:3, :] += jnp.sum(dc0, axis=0, keepdims=True)
        sums_o[3:4, :] += jnp.sum(err * err, axis=0, keepdims=True)

    row = lambda wd: pl.BlockSpec((tm, wd), lambda i: (i, 0))
    col = pl.BlockSpec((D, tm), lambda i: (0, i))
    vec = pl.BlockSpec((1, D), lambda i: (0, 0))
    f32o = lambda wd: _sds((rows, wd), F32)
    b16o = lambda wd: _sds((rows, wd), BF16)
    tpo = _sds((D, rows), BF16)
    return pl.pallas_call(
        body, name="mid", grid=(rows // tm,),
        out_shape=(f32o(D), tpo, tpo, tpo, b16o(D), b16o(D), f32o(D), f32o(D), b16o(D), b16o(3 * D),
                   _sds((8, D), F32)),
        in_specs=[row(D), row(D), row(D), row(D), row(D), row(3 * D),
                  pl.BlockSpec((3, D, D), lambda i: (0, 0, 0)), vec, vec],
        out_specs=(row(D), col, col, col, row(D), row(D), row(D), row(D), row(D), row(3 * D),
                   pl.BlockSpec((8, D), lambda i: (0, 0))),
        compiler_params=_params(("arbitrary",), 60),
    )(*_pin(x2, t2, c0, cz, o, e, w3, cn_g, cn_b))


def _do_prep(d_o, o, tm):
    rows = d_o.shape[0]

    def body(do_ref, o_ref, doe_o, dot_o):
        dov = do_ref[...]
        prod = dov * o_ref[...]
        col = lax.broadcasted_iota(jnp.int32, (tm, HEAD_DIM), 1)
        for h in range(D // HEAD_DIM):
            hs = slice(HEAD_DIM * h, HEAD_DIM * (h + 1))
            delta = jnp.sum(prod[:, hs], axis=-1, keepdims=True)
            d_hi = delta.astype(BF16).astype(F32)
            tail = jnp.where(col == 0, -d_hi, jnp.where(col == 1, d_hi - delta, 0.0))
            doe_o[:, 2 * HEAD_DIM * h:2 * HEAD_DIM * (h + 1)] = jnp.concatenate([dov[:, hs], tail], axis=1).astype(BF16)
        dot_o[...] = dov.T.astype(BF16)

    row = pl.BlockSpec((tm, D), lambda i: (i, 0))
    return pl.pallas_call(
        body, name="do_prep", grid=(rows // tm,),
        out_shape=(_sds((rows, 2 * D), BF16), _sds((D, rows), BF16)),
        in_specs=[row, row],
        out_specs=(pl.BlockSpec((tm, 2 * D), lambda i: (i, 0)), pl.BlockSpec((D, tm), lambda i: (0, i))),
        compiler_params=_params(("parallel",), 32),
    )(*_pin(d_o, o))


def _q_post(dqr, pq, qg, cos, sin, nb, tm):
    rows = pq.shape[0]
    nt = rows // nb // tm

    def body(dq_ref, q_ref, g_ref, cos_ref, sin_ref, dq_o, dg_o):
        @pl.when((pl.program_id(0) == 0) & (pl.program_id(1) == 0))
        def _():
            dg_o[...] = jnp.zeros_like(dg_o)

        gv, cosv, sinv = g_ref[...], cos_ref[...], sin_ref[...]
        acc = jnp.zeros((1, GROUP_W), F32)
        mats = _qk_mats()
        for g in range(N_KV):
            gs = slice(GROUP_W * g, GROUP_W * (g + 1))
            qv = q_ref[:, gs]
            r = lax.rsqrt(_apply(qv * qv, mats[0]) + EPS)
            dq, dgr = _qk_bwd(dq_ref[:, gs], qv, r, gv, cosv, sinv, mats)
            dq_o[:, gs] = dq.astype(BF16)
            acc = acc + jnp.sum(dgr, axis=0, keepdims=True)
        dg_o[...] += acc

    row = pl.BlockSpec((tm, D), lambda b, i: (b * nt + i, 0))
    rope = pl.BlockSpec((tm, GROUP_W), lambda b, i: (i, 0))
    vec = pl.BlockSpec((1, GROUP_W), lambda b, i: (0, 0))
    return pl.pallas_call(
        body, name="q_post", grid=(nb, nt),
        out_shape=(_sds((rows, D), BF16), _sds((1, GROUP_W), F32)),
        in_specs=[row, row, vec, rope, rope], out_specs=(row, vec),
        compiler_params=_params(("arbitrary", "arbitrary"), 32),
    )(*_pin(dqr, pq, qg, cos, sin))


def _attn_bwd(q2, qst, kv4, doe, dot_, lse, nb, tq):
    rows = q2.shape[0]
    s_len = rows // nb
    nq = s_len // tq
    scale = 1.0 / math.sqrt(HEAD_DIM)

    def body(q_ref, qt_ref, k1_ref, v1_ref, k2_ref, v2_ref, doe_ref, dot_ref, lse_ref,
             dq_o, dkt_o, dvt_o, dkt2_o, dvt2_o):
        i = pl.program_id(2)
        lse = lse_ref[0, 0]
        k1, k2, v1, v2 = k1_ref[0, 0], k2_ref[0, 0], v1_ref[0, 0], v2_ref[0, 0]
        bias = _tail_bias()
        dkt1, dkt2 = jnp.zeros((HEAD_DIM, s_len), F32), jnp.zeros((HEAD_DIM, KEY_PAD), F32)
        dvt1, dvt2 = jnp.zeros((HEAD_DIM, s_len), F32), jnp.zeros((HEAD_DIM, KEY_PAD), F32)

        def products(h):
            qh = q_ref[:, HEAD_DIM * h:HEAD_DIM * (h + 1)]
            dh = doe_ref[:, 2 * HEAD_DIM * h:2 * HEAD_DIM * (h + 1)]
            return _dot_nt(qh, k1), _dot_nt(qh, k2) + bias, _dot_nt(dh, v1), _dot_nt(dh, v2)

        ahead = products(0)
        for h in range(GQA):
            hs = slice(HEAD_DIM * h, HEAD_DIM * (h + 1))
            s1, s2, dp1, dp2 = ahead
            if h + 1 < GQA:
                ahead = products(h + 1)
            lse_h = lse[:, h:h + 1]
            p1 = jnp.exp2(s1 - lse_h)
            p2 = jnp.exp2(s2 - lse_h)
            ds1 = (p1 * dp1).astype(BF16)
            ds2 = (p2 * dp2).astype(BF16)
            dq_o[:, hs] = (_dot(ds1, k1) + _dot(ds2, k2)) * scale
            dkt1 = dkt1 + _dot(qt_ref[hs, :], ds1)
            dkt2 = dkt2 + _dot(qt_ref[hs, :], ds2)
            dvt1 = dvt1 + _dot(dot_ref[hs, :], p1.astype(BF16))
            dvt2 = dvt2 + _dot(dot_ref[hs, :], p2.astype(BF16))

        @pl.when(i == 0)
        def _():
            dkt_o[0, 0], dkt2_o[0, 0], dvt_o[0, 0], dvt2_o[0, 0] = dkt1, dkt2, dvt1, dvt2

        @pl.when(i > 0)
        def _():
            dkt_o[0, 0] += dkt1
            dkt2_o[0, 0] += dkt2
            dvt_o[0, 0] += dvt1
            dvt2_o[0, 0] += dvt2

    qspec = pl.BlockSpec((tq, GROUP_W), lambda b, g, i: (b * nq + i, g))
    qtspec = pl.BlockSpec((GROUP_W, tq), lambda b, g, i: (g, b * nq + i))
    tspec = pl.BlockSpec((1, 1, HEAD_DIM, s_len), lambda b, g, i: (b, g, 0, 0))
    t2spec = pl.BlockSpec((1, 1, HEAD_DIM, KEY_PAD), lambda b, g, i: (b, g, 0, 0))
    tshape = _sds((nb, N_KV, HEAD_DIM, s_len), F32)
    t2shape = _sds((nb, N_KV, HEAD_DIM, KEY_PAD), F32)
    return pl.pallas_call(
        body, name="attn_bwd", grid=(nb, N_KV, nq),
        out_shape=(_sds((rows, D), F32), tshape, tshape, t2shape, t2shape),
        in_specs=[qspec, qtspec] + _kv_specs(s_len)
        + [pl.BlockSpec((tq, 2 * GROUP_W), lambda b, g, i: (b * nq + i, g)), qtspec,
           pl.BlockSpec((1, 1, tq, GQA), lambda b, g, i: (b, g, i, 0))],
        out_specs=(qspec, tspec, tspec, t2spec, t2spec),
        compiler_params=_params(("parallel", "parallel", "arbitrary"), 56),
    )(*_pin(q2, qst, *kv4, doe, dot_, lse))


def _kv_bwd(dkt, dvt, dkt2, dvt2, pkv, pm_kv, kg, cos, sin, nb):
    rows = pkv.shape[0]
    s_len = rows // nb
    tk = min(512, s_len)
    nt = s_len // tk

    def body(dk_ref, dv_ref, dk2_ref, dv2_ref, kv_ref, m_ref, g_ref, cos_ref, sin_ref, d_o, dm_o, dg_o):
        b, i = pl.program_id(0), pl.program_id(1)
        gv = g_ref[...]
        mats = _qk_mats()

        @pl.when((b == 0) & (i == 0))
        def _():
            dg_o[...] = jnp.zeros_like(dg_o)

        kx = kv_ref[:, :GROUP_W]
        r = lax.rsqrt(_apply(kx * kx, mats[0]) + EPS)
        dk, dgr = _qk_bwd(dk_ref[0].T, kx, r, gv, cos_ref[...], sin_ref[...], mats)
        d_o[:, :GROUP_W] = dk.astype(BF16)
        d_o[:, GROUP_W:] = dv_ref[0].T.astype(BF16)
        dg_o[...] += jnp.sum(dgr, axis=0, keepdims=True)

        @pl.when(i == 0)
        def _():
            kxm = m_ref[:, :GROUP_W]
            rm = lax.rsqrt(_apply(kxm * kxm, mats[0]) + EPS)
            dn = dk2_ref[0].T[0:N_META]
            dyg = dn * gv
            dm_o[0, :, :GROUP_W] = rm * dyg - kxm * (rm * rm * rm) * _apply(dyg * kxm, mats[0])
            dm_o[0, :, GROUP_W:] = dv2_ref[0].T[0:N_META]
            dg_o[...] += jnp.sum(dn * kxm * rm, axis=0, keepdims=True)

    tspec = pl.BlockSpec((1, GROUP_W, tk), lambda b, i: (b, 0, i))
    t2spec = pl.BlockSpec((1, GROUP_W, KEY_PAD), lambda b, i: (b, 0, 0))
    rope = pl.BlockSpec((tk, GROUP_W), lambda b, i: (i, 0))
    return pl.pallas_call(
        body, name="kv_bwd", grid=(nb, nt),
        out_shape=(_sds((rows, 512), BF16), _sds((nb, N_META, 512), F32),
                   _sds((1, GROUP_W), F32)),
        in_specs=[tspec, tspec, t2spec, t2spec, pl.BlockSpec((tk, 512), lambda b, i: (b * nt + i, 0)),
                  pl.BlockSpec((N_META, 512), lambda b, i: (0, 0)), pl.BlockSpec((1, GROUP_W), lambda b, i: (0, 0)),
                  rope, rope],
        out_specs=(pl.BlockSpec((tk, 512), lambda b, i: (b * nt + i, 0)),
                   pl.BlockSpec((1, N_META, 512), lambda b, i: (b, 0, 0)),
                   pl.BlockSpec((1, GROUP_W), lambda b, i: (0, 0))),
        compiler_params=_params(("arbitrary", "arbitrary"), 40),
    )(*_pin(dkt, dvt, dkt2, dvt2, pkv, pm_kv, kg, cos, sin))


def _conv_bwd(dc0, pconv, pm_conv, conv_w, nb, tm):
    rows = pconv.shape[0]
    nt = rows // nb // tm

    def body(dcur, dprev, dnxt, cur, meta, w_ref, da_o, dam_o, gw_o, ucur, dext, dsh):
        b, i = pl.program_id(0), pl.program_id(1)
        ucur[...] = _glu(cur[...])
        dext[0:16] = jnp.zeros((16, D), F32)
        dext[16:32] = jnp.where(i == 0, 0.0, dprev[...])
        dext[32:32 + tm] = dcur[...]
        dext[32 + tm:48 + tm] = jnp.where(i == nt - 1, 0.0, dnxt[...])
        _shifted_copies(dsh, dext, tm + 40)

        @pl.when((b == 0) & (i == 0))
        def _():
            gw_o[...] = jnp.zeros_like(gw_o)

        for c0 in range(0, D, 256):
            cs = slice(c0, c0 + 256)
            for r0 in range(0, tm, 32):
                acc = jnp.zeros((32, 256), F32)
                for j in range(CONV_K):
                    acc = acc + _rows32(dsh, dext, r0 + 47 - j, cs) * w_ref[j:j + 1, cs]
                cv = cur[r0:r0 + 32, c0:c0 + 256]
                sg = _sig(cur[r0:r0 + 32, D + c0:D + c0 + 256])
                da_o[r0:r0 + 32, cs] = (acc * sg).astype(BF16)
                da_o[r0:r0 + 32, D + c0:D + c0 + 256] = (acc * cv * sg * (1.0 - sg)).astype(BF16)
            for j in range(CONV_K):
                acc = jnp.zeros((32, 256), F32)
                for r0 in range(0, tm, 32):
                    acc = acc + _rows32(dsh, dext, r0 + 47 - j, cs) * ucur[r0:r0 + 32, cs]
                gw_o[j:j + 1, cs] += jnp.sum(acc, axis=0, keepdims=True)

        @pl.when(i == 0)
        def _():
            for c0 in range(0, D, 256):
                cs = slice(c0, c0 + 256)
                cv = meta[:, c0:c0 + 256]
                sg = _sig(meta[:, D + c0:D + c0 + 256])
                um = cv * sg
                acc = jnp.zeros((16, 256), F32)
                for j in range(CONV_K):
                    d = dext[31 - j:47 - j, cs]
                    acc = acc + d * w_ref[j:j + 1, cs]
                    gw_o[j:j + 1, cs] += jnp.sum(d * um, axis=0, keepdims=True)
                dam_o[0, :, cs] = acc * sg
                dam_o[0, :, D + c0:D + c0 + 256] = acc * cv * sg * (1.0 - sg)

    return pl.pallas_call(
        body, name="conv_bwd", grid=(nb, nt),
        out_shape=(_sds((rows, 2048), BF16), _sds((nb, N_META, 2048), F32),
                   _sds((32, D), F32)),
        in_specs=_halo_specs(D, tm, nt, rows)
        + [pl.BlockSpec((tm, 2048), lambda b, i: (b * nt + i, 0)),
           pl.BlockSpec((16, 2048), lambda b, i: (0, 0)), pl.BlockSpec((32, D), lambda b, i: (0, 0))],
        out_specs=(pl.BlockSpec((tm, 2048), lambda b, i: (b * nt + i, 0)),
                   pl.BlockSpec((1, N_META, 2048), lambda b, i: (b, 0, 0)),
                   pl.BlockSpec((32, D), lambda b, i: (0, 0))),
        scratch_shapes=[pltpu.VMEM((tm, D), F32), pltpu.VMEM((tm + 48, D), F32), pltpu.VMEM((8, tm + 40, D), F32)],
        compiler_params=_params(("arbitrary", "arbitrary"), 48),
    )(*_pin(dc0, dc0, dc0, pconv, pm_conv, conv_w))


def _meta_bwd(dam, ddm, w_full, meta_full, norm_g):
    nb = dam.shape[0]

    def body(a_ref, d_ref, wc_ref, wkv_ref, m_ref, g_ref, gm_o, dg_o):
        a, d = a_ref[0], d_ref[0]
        for b in range(1, nb):
            a = a + a_ref[b]
            d = d + d_ref[b]
        dxn = _dot_nt(a.astype(BF16), wc_ref[...]) + _dot_nt(d.astype(BF16), wkv_ref[...])
        v = m_ref[...]
        r = lax.rsqrt(jnp.mean(v * v, axis=-1, keepdims=True) + EPS)
        gm_o[...] = _rms_bwd(dxn, v, r, g_ref[...])
        dg_o[...] = jnp.sum(dxn * v * r, axis=0, keepdims=True)

    return pl.pallas_call(
        body, name="meta_bwd", grid=(1,),
        out_shape=(_sds((N_META, D), F32), _sds((1, D), F32)),
        in_specs=[pl.BlockSpec((nb, N_META, 2048), lambda i: (0, 0, 0)), pl.BlockSpec((nb, N_META, 512), lambda i: (0, 0, 0)),
                  pl.BlockSpec((D, 2048), lambda i: (0, 0)), pl.BlockSpec((D, 512), lambda i: (0, G_KV[0] // 512)),
                  pl.BlockSpec((N_META, D), lambda i: (0, 0)), pl.BlockSpec((1, D), lambda i: (0, 0))],
        out_specs=(pl.BlockSpec((N_META, D), lambda i: (0, 0)), pl.BlockSpec((1, D), lambda i: (0, 0))),
        compiler_params=_params(("arbitrary",), 32),
    )(*_pin(dam, ddm, w_full, w_full, meta_full, norm_g))


def _dxn(d_groups, w_full, x2, dy, norm_g, dg_init, tm):
    rows = x2.shape[0]
    groups = (G_CONV, G_CZ, G_Q, G_KV, G_E)

    def body(da, db, dq, dd, de, w_hbm, x_ref, dy_ref, g_ref, gi_ref, gx_o, dg_o, w_vmem, sem):
        @pl.when(pl.program_id(0) == 0)
        def _():
            cp = pltpu.make_async_copy(w_hbm, w_vmem, sem)
            cp.start()
            cp.wait()
            dg_o[...] = gi_ref[...]

        dxn = jnp.zeros((tm, D), F32)
        for ref, (off, wd) in zip((da, db, dq, dd, de), groups):
            for c0 in range(0, wd, 512):
                dxn = dxn + _dot_nt(ref[:, c0:c0 + 512], w_vmem[:, off + c0:off + c0 + 512])
        v = x_ref[...]
        r = lax.rsqrt(jnp.mean(v * v, axis=-1, keepdims=True) + EPS)
        gx_o[...] = dy_ref[...] + _rms_bwd(dxn, v, r, g_ref[...])
        dg_o[...] += jnp.sum(dxn * v * r, axis=0, keepdims=True)

    row = lambda wd: pl.BlockSpec((tm, wd), lambda i: (i, 0))
    vec = pl.BlockSpec((1, D), lambda i: (0, 0))
    return pl.pallas_call(
        body, name="dxn", grid=(rows // tm,),
        out_shape=(_sds((rows, D), F32), _sds((1, D), F32)),
        in_specs=[row(wd) for _, wd in groups] + [pl.BlockSpec(memory_space=pl.ANY), row(D), row(D), vec, vec],
        out_specs=(row(D), vec),
        scratch_shapes=[pltpu.VMEM((D, IN_DIM), BF16), pltpu.SemaphoreType.DMA],
        compiler_params=_params(("arbitrary",), 56),
    )(*_pin(*d_groups, w_full, x2, dy, norm_g, dg_init))


def _wgrad(at, b, buf, slot, col_off, name, meta=None):
    rows, n = b.shape
    tn, tk = 512, min(2048, rows)
    nk = rows // tk
    j0 = col_off // tn

    def body(*refs):
        if meta is None:
            at_ref, b_ref, _, o_ref = refs
        else:
            at_ref, b_ref, xm_ref, dm_ref, _, o_ref = refs
        k = pl.program_id(1)

        @pl.when(k == 0)
        def _():
            if meta is None:
                o_ref[0] = jnp.zeros((D, tn), F32)
            else:
                dm = dm_ref[0]
                for e in range(1, dm_ref.shape[0]):
                    dm = dm + dm_ref[e]
                dm = jnp.concatenate([dm, jnp.zeros((128 - N_META, tn), F32)], axis=0)
                o_ref[0] = _dot(xm_ref[...], dm.astype(BF16))

        o_ref[0] += _dot(at_ref[...], b_ref[...].astype(BF16))

    in_specs = [pl.BlockSpec((D, tk), lambda j, k: (0, k)), pl.BlockSpec((tk, tn), lambda j, k: (k, j))]
    args = [at, b]
    if meta is not None:
        xmt, dm = meta
        in_specs += [pl.BlockSpec((D, 128), lambda j, k: (0, 0)),
                     pl.BlockSpec((dm.shape[0], N_META, tn), lambda j, k: (0, 0, j))]
        args += [xmt, dm]
    in_specs.append(pl.BlockSpec(memory_space=pl.ANY))
    args.append(buf)
    return pl.pallas_call(
        body, name=name, grid=(n // tn, nk),
        out_shape=_sds(buf.shape, F32),
        in_specs=in_specs,
        out_specs=pl.BlockSpec((1, D, tn), lambda j, k: (slot, 0, j0 + j)),
        input_output_aliases={len(args) - 1: 0},
        compiler_params=_params(("parallel", "arbitrary"), 32),
    )(*_pin(*args))


def _rope_tables(s_len):
    pos = jnp.arange(s_len, dtype=jnp.int32)
    row_ids = (pos // GRID_W).astype(F32)
    col_ids = (pos % GRID_W).astype(F32)
    inv_freq = ROPE_THETA ** (-jnp.arange(ROPE_FREQS, dtype=F32) / ROPE_FREQS)
    a_row = row_ids[:, None] * inv_freq[None, :]
    a_col = col_ids[:, None] * inv_freq[None, :]
    ang = jnp.concatenate([a_row, a_row, a_col, a_col], axis=-1)
    return jnp.tile(jnp.cos(ang), (1, GQA)), jnp.tile(jnp.sin(ang), (1, GQA))


def _local_step(x, loss_target, norm_g, conv_b, cn_g, cn_b, q_g, k_g, w_full, w3_full, conv_w_full, meta_full,
                reduce_start=None):
    nb, s_len, _ = x.shape
    rows = nb * s_len
    x2 = x.reshape(rows, D)
    t2 = loss_target.reshape(rows, D)
    cos, sin = _rope_tables(s_len)
    qg = jnp.tile(q_g, (1, GQA))
    kg = jnp.tile(k_g, (1, N_KV))

    xnmt, pm_conv, pm_kv = _meta_fwd(meta_full, norm_g, w_full)
    pconv, pcz, pq, pkv, pe, xnt = _in_proj(x2, norm_g, w_full, 256)
    c0 = _conv_fwd(pconv, pm_conv, conv_w_full, conv_b, nb, 256)
    tq = min(512, s_len)
    kv4 = _kv_prep(pkv, pm_kv, kg, cos, sin, nb)
    q2, qst = _q_prep(pq, qg, cos, sin, nb, 256)
    o, lse = _attn_fwd(q2, kv4, nb, tq)
    dy, mt, c3t, o2t, dyc, dya, d_o, dc0, dcz, de, sums = _mid(x2, t2, c0, pcz, o, pe, w3_full, cn_g, cn_b, 256)
    doe, dot_ = _do_prep(d_o, o, 256)
    dqr, dkt, dvt, dkt2, dvt2 = _attn_bwd(q2, qst, kv4, doe, dot_, lse, nb, tq)
    dq, dqg = _q_post(dqr, pq, qg, cos, sin, nb, 256)
    dd, ddm, dkg = _kv_bwd(dkt.reshape(nb, GROUP_W, s_len), dvt.reshape(nb, GROUP_W, s_len),
                           dkt2.reshape(nb, GROUP_W, KEY_PAD), dvt2.reshape(nb, GROUP_W, KEY_PAD),
                           pkv, pm_kv, kg, cos, sin, nb)
    da, dam, gcw = _conv_bwd(dc0, pconv, pm_conv, conv_w_full, nb, 256)
    gmeta, dng_m = _meta_bwd(dam, ddm, w_full, meta_full, norm_g)

    gw3 = lax.empty((3, D, D), F32)
    gw3 = _wgrad(c3t, dyc, gw3, 0, 0, "wgrad_conv_out")
    gw3 = _wgrad(o2t, dya, gw3, 1, 0, "wgrad_attn_out")
    gw3 = _wgrad(mt, dy, gw3, 2, 0, "wgrad_out")
    gwin = lax.empty((1, D, IN_DIM), F32)
    gwin = _wgrad(xnt, da, gwin, 0, G_CONV[0], "wgrad_in_conv", meta=(xnmt, dam))
    gwin = _wgrad(xnt, dcz, gwin, 0, G_CZ[0], "wgrad_in_cz")
    gwin = _wgrad(xnt, dq, gwin, 0, G_Q[0], "wgrad_in_q")
    gwin = _wgrad(xnt, dd, gwin, 0, G_KV[0], "wgrad_in_kv", meta=(xnmt, ddm))
    gwin = _wgrad(xnt, de, gwin, 0, G_E[0], "wgrad_in_e")

    pending = None
    if reduce_start is not None:
        token, pending = reduce_start(gwin, gw3, gcw, gmeta)
        dng_m = dng_m + token[0:1, 0:1]
    gx, dng = _dxn((da, dcz, dq, dd, de), w_full, x2, dy, norm_g, dng_m, 256)

    zeros = jnp.zeros((1, D - 2 * GROUP_W), F32)
    smalls = jnp.concatenate([dng, sums[2:3], sums[0:1], sums[1:2], jnp.concatenate([dqg, dkg, zeros], axis=1),
                              sums[3:4], jnp.zeros((2, D), F32)], axis=0)
    return gx.reshape(nb, s_len, D), gwin, gw3, gcw, gmeta, smalls, pending


def _xyc():
    return lax.axis_index("x"), lax.axis_index("y"), lax.axis_index("c")


def _reduce_sibling(gwin, gw3v, gcm):
    def body(gwin_ref, gw3_ref, gcm_ref, r_win, r_w3, r_cm, send, recv):
        x, y, c = _xyc()
        o = 1 - c
        half = D // 2
        outs = ((gwin_ref.at[pl.ds(o * half, half), :], r_win), (gw3_ref.at[:, :, o], r_w3), (gcm_ref.at[o], r_cm))
        cps = []
        for a, (src, dst) in enumerate(outs):
            cp = pltpu.make_async_remote_copy(src_ref=src, dst_ref=dst, send_sem=send.at[a], recv_sem=recv.at[a],
                                              device_id=(x, y, o), device_id_type=MESH)
            cp.start()
            cps.append(cp)
        for cp in cps:
            cp.wait()

    any_spec = pl.BlockSpec(memory_space=pl.ANY)
    return pl.pallas_call(
        body, name="reduce_sibling",
        out_shape=(_sds((D // 2, IN_DIM), F32), _sds((3, 4, 128, D), F32),
                   _sds((24, D), F32)),
        in_specs=[any_spec] * 3, out_specs=(any_spec,) * 3,
        scratch_shapes=[pltpu.SemaphoreType.DMA((3,)), pltpu.SemaphoreType.DMA((3,))],
    )(*_pin(gwin, gw3v, gcm))


def _add_sibling(gwin, gw3v, gcm, r_win, r_w3, r_cm):
    c = lax.axis_index("c").astype(jnp.int32).reshape(1)
    half = D // 2
    tr = 64

    def body1(c_ref, a_ref, b_ref, o_ref):
        o_ref[...] = (a_ref[...] + b_ref[...]).astype(BF16)

    cs_win = pl.pallas_call(
        body1, name="add_sibling_w_in", out_shape=_sds((half, IN_DIM), BF16),
        grid_spec=pltpu.PrefetchScalarGridSpec(
            num_scalar_prefetch=1, grid=(half // tr,),
            in_specs=[pl.BlockSpec((tr, IN_DIM), lambda i, c_ref: (c_ref[0] * (half // tr) + i, 0)),
                      pl.BlockSpec((tr, IN_DIM), lambda i, c_ref: (i, 0))],
            out_specs=pl.BlockSpec((tr, IN_DIM), lambda i, c_ref: (i, 0))),
        compiler_params=_params(("parallel",), 32),
    )(c, *_pin(gwin, r_win))

    def body2(c_ref, a_ref, b_ref, o_ref):
        o_ref[0, 0] = (a_ref[0, 0, 0] + b_ref[0, 0]).astype(BF16)

    cs_w3 = pl.pallas_call(
        body2, name="add_sibling_w3", out_shape=_sds((3, 4, 128, D), BF16),
        grid_spec=pltpu.PrefetchScalarGridSpec(
            num_scalar_prefetch=1, grid=(3, 4),
            in_specs=[pl.BlockSpec((1, 1, 1, 128, D), lambda w, s, c_ref: (w, s, c_ref[0], 0, 0)),
                      pl.BlockSpec((1, 1, 128, D), lambda w, s, c_ref: (w, s, 0, 0))],
            out_specs=pl.BlockSpec((1, 1, 128, D), lambda w, s, c_ref: (w, s, 0, 0))),
        compiler_params=_params(("parallel", "parallel"), 32),
    )(c, *_pin(gw3v, r_w3))

    def body3(c_ref, a_ref, b_ref, o_ref):
        o_ref[...] = a_ref[0] + b_ref[...]

    cs_cm = pl.pallas_call(
        body3, name="add_sibling_cm", out_shape=_sds((24, D), F32),
        grid_spec=pltpu.PrefetchScalarGridSpec(
            num_scalar_prefetch=1, grid=(1,),
            in_specs=[pl.BlockSpec((1, 24, D), lambda i, c_ref: (c_ref[0], 0, 0)),
                      pl.BlockSpec((24, D), lambda i, c_ref: (0, 0))],
            out_specs=pl.BlockSpec((24, D), lambda i, c_ref: (0, 0))),
        compiler_params=_params(("arbitrary",), 32),
    )(c, *_pin(gcm, r_cm))
    return cs_win, cs_w3, cs_cm


def _reduce_chips_copies(srcs, lands, send, recv):
    win_ref, w3_ref, cm_ref = srcs
    r_win, r_w3, r_cm = lands
    x, y, c = _xyc()
    peers = ((1 - x, y), (x, 1 - y), (1 - x, 1 - y))
    cps = []
    for k, (px, py) in enumerate(peers):
        ps = 2 * px + py
        items = ((win_ref.at[:, pl.ds(ps * W_IN_SHARD, W_IN_SHARD)], r_win.at[k]),
                 (w3_ref.at[:, ps], r_w3.at[k]),
                 (cm_ref.at[:, pl.ds(ps * ROW_SHARD, ROW_SHARD)], r_cm.at[k]))
        for a, (src, dst) in enumerate(items):
            cps.append(pltpu.make_async_remote_copy(src_ref=src, dst_ref=dst, send_sem=send.at[3 * a + k],
                                                    recv_sem=recv.at[3 * a + k], device_id=(px, py, c),
                                                    device_id_type=MESH))
    return cps


_HBM = pl.BlockSpec(memory_space=pltpu.HBM)
_SEM = pl.BlockSpec(memory_space=pltpu.SEMAPHORE)
_EFFECT = pltpu.SideEffectType.DATAFLOW_SIDE_EFFECTING


def _reduce_chips_start(cs_win, cs_w3, cs_cm):
    srcs = (cs_win, cs_w3, cs_cm)
    lands = (lax.empty((3, D // 2, W_IN_SHARD), BF16), lax.empty((3, 3, 128, D), BF16),
             lax.empty((3, 24, ROW_SHARD), F32))

    def body(*refs):
        srcs_in, lands_in, send, recv, token = refs[0:3], refs[3:6], refs[6], refs[7], refs[14]
        for cp in _reduce_chips_copies(srcs_in, lands_in, send, recv):
            cp.start()
        token[...] = jnp.zeros_like(token)

    hbm = lambda a: pltpu.HBM(a.shape, a.dtype)
    outs = pl.pallas_call(
        body, name="reduce_chips_start",
        out_shape=(pltpu.SemaphoreType.DMA((9,)), pltpu.SemaphoreType.DMA((9,)),
                   *[hbm(a) for a in srcs], *[hbm(a) for a in lands], jax.ShapeDtypeStruct((8, 128), F32)),
        in_specs=[_HBM] * 6,
        out_specs=(_SEM, _SEM, *[_HBM] * 6, pl.BlockSpec(memory_space=pltpu.VMEM)),
        input_output_aliases={i: i + 2 for i in range(6)},
        compiler_params=pltpu.CompilerParams(has_side_effects=_EFFECT),
    )(*[pltpu.with_memory_space_constraint(a, pltpu.HBM) for a in srcs + lands])
    return outs[0], outs[1], outs[2:5], outs[5:8], outs[8]


def _reduce_chips_wait(send, recv, srcs, lands, after):
    def body(*refs):
        srcs_in, lands_in, send_ref, recv_ref = refs[0:3], refs[3:6], refs[6], refs[7]
        for cp in _reduce_chips_copies(srcs_in, lands_in, send_ref, recv_ref):
            cp.wait_send()
            cp.wait_recv()

    hbm = lambda a: pltpu.HBM(a.shape, a.dtype)
    outs = pl.pallas_call(
        body, name="reduce_chips_wait",
        out_shape=(*[hbm(a) for a in srcs], *[hbm(a) for a in lands]),
        in_specs=[_HBM] * 6 + [_SEM, _SEM, pl.BlockSpec(memory_space=pl.ANY)],
        out_specs=(_HBM,) * 6,
        input_output_aliases={i: i for i in range(6)},
        compiler_params=pltpu.CompilerParams(has_side_effects=_EFFECT),
    )(*srcs, *lands, send, recv, after)
    return outs[0:3], outs[3:6]


def _add_chips(cs_win, cs_w3, cs_cm, r_win, r_w3, r_cm):
    x, y, c = _xyc()
    idx = jnp.stack([2 * x + y, c]).astype(jnp.int32)
    half = D // 2
    tr = 128

    def body1(i_ref, a_ref, b_ref, o_ref):
        f = lambda v: v.astype(F32)
        o_ref[0] = (f(a_ref[...]) + f(b_ref[2])) + (f(b_ref[0]) + f(b_ref[1]))

    f_win = pl.pallas_call(
        body1, name="add_chips_w_in", out_shape=_sds((2, half, W_IN_SHARD), F32),
        grid_spec=pltpu.PrefetchScalarGridSpec(
            num_scalar_prefetch=1, grid=(half // tr,),
            in_specs=[pl.BlockSpec((tr, W_IN_SHARD), lambda i, r: (i, r[0])),
                      pl.BlockSpec((3, tr, W_IN_SHARD), lambda i, r: (0, i, 0))],
            out_specs=pl.BlockSpec((1, tr, W_IN_SHARD), lambda i, r: (r[1], i, 0))),
        compiler_params=_params(("parallel",), 32),
    )(idx, *_pin(cs_win, r_win))

    def body2(i_ref, a_ref, b_ref, o_ref):
        f = lambda v: v.astype(F32)
        o_ref[0, 0] = (f(a_ref[0, 0]) + f(b_ref[2, 0])) + (f(b_ref[0, 0]) + f(b_ref[1, 0]))

    f_w3 = pl.pallas_call(
        body2, name="add_chips_w3", out_shape=_sds((3, 2, 128, D), F32),
        grid_spec=pltpu.PrefetchScalarGridSpec(
            num_scalar_prefetch=1, grid=(3,),
            in_specs=[pl.BlockSpec((1, 1, 128, D), lambda w, r: (w, r[0], 0, 0)),
                      pl.BlockSpec((3, 1, 128, D), lambda w, r: (0, w, 0, 0))],
            out_specs=pl.BlockSpec((1, 1, 128, D), lambda w, r: (w, r[1], 0, 0))),
        compiler_params=_params(("parallel",), 32),
    )(idx, *_pin(cs_w3, r_w3))

    def body3(i_ref, a_ref, b_ref, o_ref):
        o_ref[0] = (a_ref[...] + b_ref[2]) + (b_ref[0] + b_ref[1])

    f_cm = pl.pallas_call(
        body3, name="add_chips_cm", out_shape=_sds((2, 24, ROW_SHARD), F32),
        grid_spec=pltpu.PrefetchScalarGridSpec(
            num_scalar_prefetch=1, grid=(1,),
            in_specs=[pl.BlockSpec((24, ROW_SHARD), lambda i, r: (0, r[0])),
                      pl.BlockSpec((3, 24, ROW_SHARD), lambda i, r: (0, 0, 0))],
            out_specs=pl.BlockSpec((1, 24, ROW_SHARD), lambda i, r: (r[1], 0, 0))),
        compiler_params=_params(("arbitrary",), 32),
    )(idx, *_pin(cs_cm, r_cm))
    return f_win, f_w3, f_cm


def _share_sibling(f_win, f_w3, f_cm, smalls):
    def body(win_in, w3_in, cm_in, sm_ref, win_ref, w3_ref, cm_ref, r_sm, send, recv, ssend, srecv, lsem):
        x, y, c = _xyc()
        o = 1 - c
        cps = []
        for a, (ref, sl) in enumerate(((win_ref, lambda h: win_ref.at[h]), (w3_ref, lambda h: w3_ref.at[:, h]),
                                       (cm_ref, lambda h: cm_ref.at[h]))):
            cp = pltpu.make_async_remote_copy(src_ref=sl(c), dst_ref=sl(c), send_sem=send.at[a], recv_sem=recv.at[a],
                                              device_id=(x, y, o), device_id_type=MESH)
            cp.start()
            cps.append((cp, sl))
        me = 4 * x + 2 * y + c
        loc = pltpu.make_async_copy(sm_ref, r_sm.at[me], lsem)
        loc.start()
        scps = []
        for d in range(1, 8):
            px, py, pc = (x + (d >> 2)) % 2, (y + ((d >> 1) & 1)) % 2, (c + (d & 1)) % 2
            cp = pltpu.make_async_remote_copy(src_ref=sm_ref, dst_ref=r_sm.at[me], send_sem=ssend.at[d - 1],
                                              recv_sem=srecv.at[d - 1], device_id=(px, py, pc), device_id_type=MESH)
            cp.start()
            scps.append((cp, 4 * px + 2 * py + pc))
        for a, (cp, sl) in enumerate(cps):
            pltpu.make_async_remote_copy(src_ref=sl(o), dst_ref=sl(o), send_sem=send.at[a], recv_sem=recv.at[a],
                                         device_id=(x, y, o), device_id_type=MESH).wait_recv()
            cp.wait_send()
        for d, (cp, pid) in enumerate(scps):
            pltpu.make_async_remote_copy(src_ref=sm_ref, dst_ref=r_sm.at[pid], send_sem=ssend.at[d],
                                         recv_sem=srecv.at[d], device_id=(x, y, c), device_id_type=MESH).wait_recv()
            cp.wait_send()
        loc.wait()

    any_spec = pl.BlockSpec(memory_space=pl.ANY)
    return pl.pallas_call(
        body, name="share_sibling",
        out_shape=(_sds(f_win.shape, F32), _sds(f_w3.shape, F32), _sds(f_cm.shape, F32), _sds((8, 8, D), F32)),
        in_specs=[any_spec] * 4, out_specs=(any_spec,) * 4,
        input_output_aliases={0: 0, 1: 1, 2: 2},
        scratch_shapes=[pltpu.SemaphoreType.DMA((3,)), pltpu.SemaphoreType.DMA((3,)),
                        pltpu.SemaphoreType.DMA((7,)), pltpu.SemaphoreType.DMA((7,)), pltpu.SemaphoreType.DMA],
    )(*_pin(f_win, f_w3, f_cm, smalls))


def _adamw_math(w, g, m, v):
    m = ADAM_B1 * m + (1.0 - ADAM_B1) * g
    v = ADAM_B2 * v + (1.0 - ADAM_B2) * (g * g)
    m_hat = m / (1.0 - ADAM_B1 ** ADAM_STEP)
    v_hat = v / (1.0 - ADAM_B2 ** ADAM_STEP)
    delta = -ADAM_LR * (m_hat / (jnp.sqrt(v_hat) + ADAM_EPS) + ADAM_WD * w)
    return delta, m, v


def _adamw(w, g, m, v, tr, name):
    rows, cols = w.shape

    def body(w_ref, g_ref, m_ref, v_ref, d_o, m_o, v_o):
        d_o[...], m_o[...], v_o[...] = _adamw_math(w_ref[...], g_ref[...], m_ref[...], v_ref[...])

    spec = pl.BlockSpec((tr, cols), lambda i: (i, 0))
    return pl.pallas_call(
        body, name=name, grid=(rows // tr,),
        out_shape=(_sds((rows, cols), F32),) * 3,
        in_specs=[spec] * 4, out_specs=(spec,) * 3,
        compiler_params=_params(("parallel",), 32),
    )(*_pin(w, g, m, v))


def _adamw3(g3, ws, ms, vs):
    def body(g_ref, *refs):
        w_refs, m_refs, v_refs, outs = refs[0:3], refs[3:6], refs[6:9], refs[9:]
        g_os, d_os, m_os, v_os = outs[0:3], outs[3:6], outs[6:9], outs[9:12]
        for i in range(3):
            g = g_ref[i]
            g_os[i][0] = g
            d_os[i][0], m_os[i][0], v_os[i][0] = _adamw_math(w_refs[i][0], g, m_refs[i][0], v_refs[i][0])

    return pl.pallas_call(
        body, name="adamw_w3", out_shape=(jax.ShapeDtypeStruct((1, ROW_SHARD, D), F32),) * 12,
        compiler_params=pltpu.CompilerParams(vmem_limit_bytes=48 << 20),
    )(g3, *ws, *ms, *vs)


def _adamw_cm(f_cm, ws, ms, vs):
    def body(f_ref, *refs):
        w_refs, m_refs, v_refs, outs = refs[0:2], refs[2:4], refs[4:6], refs[6:14]
        gcw, gmt = refs[14], refs[15]
        gcw[0:16] = f_ref[0, 0:16]
        gcw[16:32] = f_ref[1, 0:16]
        gmt[0:8] = f_ref[0, 16:24]
        gmt[8:16] = f_ref[1, 16:24]
        g_conv = gcw[0:CONV_K, :]
        g_meta = gmt[...]
        outs[0][0] = g_conv
        outs[1][...] = g_meta
        outs[2][0], outs[4][0], outs[6][0] = _adamw_math(w_refs[0][0], g_conv, m_refs[0][0], v_refs[0][0])
        outs[3][...], outs[5][...], outs[7][...] = _adamw_math(w_refs[1][...], g_meta, m_refs[1][...], v_refs[1][...])

    pair = (jax.ShapeDtypeStruct((1, CONV_K, ROW_SHARD), F32), jax.ShapeDtypeStruct((N_META, ROW_SHARD), F32))
    return pl.pallas_call(
        body, name="adamw_cm", out_shape=pair * 4,
        scratch_shapes=[pltpu.VMEM((32, ROW_SHARD), F32), pltpu.VMEM((N_META, ROW_SHARD), F32)],
    )(f_cm, *ws, *ms, *vs)


def _adamw_small(r_sm, ws, ms, vs):
    def body(s_ref, *refs):
        w_refs, m_refs, v_refs, outs = refs[0:6], refs[6:12], refs[12:18], refs[18:]
        loss_o, g_os, d_os, m_os, v_os = outs[0], outs[1:7], outs[7:13], outs[13:19], outs[19:25]
        g = s_ref[0]
        for dev in range(1, 8):
            g = g + s_ref[dev]
        qk = g[4:5, :]
        qg = qk[:, 0:HEAD_DIM]
        kg = qk[:, GROUP_W:GROUP_W + HEAD_DIM]
        for h in range(1, GQA):
            qg = qg + qk[:, HEAD_DIM * h:HEAD_DIM * (h + 1)]
            kg = kg + qk[:, GROUP_W + HEAD_DIM * h:GROUP_W + HEAD_DIM * (h + 1)]
        loss_o[...] = (0.5 / D) * jnp.sum(g[5:6, :], axis=-1, keepdims=True)
        for i, gi in enumerate((g[0:1], g[1:2], g[2:3], g[3:4], qg, kg)):
            g_os[i][...] = gi
            d_os[i][...], m_os[i][...], v_os[i][...] = _adamw_math(w_refs[i][...], gi, m_refs[i][...], v_refs[i][...])

    six = tuple(jax.ShapeDtypeStruct(w.shape, F32) for w in ws)
    return pl.pallas_call(
        body, name="adamw_small", out_shape=(jax.ShapeDtypeStruct((1, 1), F32),) + six * 4,
    )(r_sm, *ws, *ms, *vs)


def kernel(x, meta_tokens, norm_g, w_in, conv_w, conv_b, conv_norm_g, conv_norm_b, w_conv_out, q_norm_g, k_norm_g, w_attn_out, w_out, loss_target, m_meta_tokens, m_norm_g, m_w_in, m_conv_w, m_conv_b, m_conv_norm_g, m_conv_norm_b, m_w_conv_out, m_q_norm_g, m_k_norm_g, m_w_attn_out, m_w_out, v_meta_tokens, v_norm_g, v_w_in, v_conv_w, v_conv_b, v_conv_norm_g, v_conv_norm_b, v_w_conv_out, v_q_norm_g, v_k_norm_g, v_w_attn_out, v_w_out):
    pad_k = lambda a: jnp.pad(a[0], ((0, 32 - CONV_K), (0, 0)))
    w3_s = (w_conv_out, w_attn_out, w_out)
    w_full, w3_full, conv_w_full, meta_full = _gather_weights(w_in[0], w3_s, pad_k(conv_w), meta_tokens)

    def reduce_start(gwin, gw3, gcw, gmeta):
        gwin2 = gwin.reshape(D, IN_DIM)
        gw3v = gw3.reshape(3, N_CHIPS, 2, 128, D)
        gcm = jnp.concatenate([gcw.reshape(2, 16, D), gmeta.reshape(2, 8, D)], axis=1)
        r_win, r_w3, r_cm = _reduce_sibling(gwin2, gw3v, gcm)
        cs = _add_sibling(gwin2, gw3v, gcm, r_win, r_w3, r_cm)
        send, recv, srcs, lands, token = _reduce_chips_start(*cs)
        return token, (send, recv, srcs, lands)

    gx, _, _, _, _, smalls, pending = _local_step(
        x, loss_target, norm_g, conv_b, conv_norm_g, conv_norm_b, q_norm_g, k_norm_g,
        w_full, w3_full, conv_w_full, meta_full, reduce_start)
    (cs_win, cs_w3, cs_cm), (r2_win, r2_w3, r2_cm) = _reduce_chips_wait(*pending, gx)
    f_win, f_w3, f_cm = _add_chips(cs_win, cs_w3, cs_cm, r2_win, r2_w3, r2_cm)
    f_win, f_w3, f_cm, r_sm = _share_sibling(f_win, f_w3, f_cm, smalls)

    g_w_in = f_win.reshape(D, W_IN_SHARD)
    d_w_in, nm_w_in, nv_w_in = _adamw(w_in[0], g_w_in, m_w_in[0], v_w_in[0], 128, "adamw_w_in")
    w3 = _adamw3(f_w3.reshape(3, ROW_SHARD, D), w3_s, (m_w_conv_out, m_w_attn_out, m_w_out),
                 (v_w_conv_out, v_w_attn_out, v_w_out))
    cm = _adamw_cm(f_cm, (conv_w, meta_tokens), (m_conv_w, m_meta_tokens), (v_conv_w, v_meta_tokens))
    small = _adamw_small(
        r_sm, (norm_g, conv_b, conv_norm_g, conv_norm_b, q_norm_g, k_norm_g),
        (m_norm_g, m_conv_b, m_conv_norm_g, m_conv_norm_b, m_q_norm_g, m_k_norm_g),
        (v_norm_g, v_conv_b, v_conv_norm_g, v_conv_norm_b, v_q_norm_g, v_k_norm_g))

    def assemble(big_in, w3x, cmx, s6):
        ng, cb, cng, cnb, qg, kg = s6
        return (cmx[1], ng, big_in[None], cmx[0], cb, cng, cnb, w3x[0], qg, kg, w3x[1], w3x[2])

    loss = small[0].reshape(())
    grads = assemble(g_w_in, w3[0:3], cm[0:2], small[1:7])
    deltas = assemble(d_w_in, w3[3:6], cm[2:4], small[7:13])
    new_m = assemble(nm_w_in, w3[6:9], cm[4:6], small[13:19])
    new_v = assemble(nv_w_in, w3[9:12], cm[6:8], small[19:25])
    return (loss, gx, *grads, *deltas, *new_m, *new_v)
```

```python
import functools
import math

import jax
import jax.numpy as jnp
from jax import lax
from jax.experimental import pallas as pl
from jax.experimental.pallas import tpu as pltpu

F32, BF16 = jnp.float32, jnp.bfloat16
MESH = pl.DeviceIdType.MESH

D = 1024
N_META = 16
CONV_K = 31
N_KV = 4
GQA = 4
HEAD_DIM = 64
GROUP_W = GQA * HEAD_DIM
GRID_W = 64
ROPE_FREQS = 16
ROPE_THETA = 10000.0
EPS = 1e-6
IN_DIM = 7680
KEY_PAD = 128
G_CONV, G_CZ, G_Q, G_KV, G_E = (0, 2048), (2048, 1024), (3072, 1024), (4096, 512), (4608, 3072)
N_CHIPS = 4
W_IN_SHARD = IN_DIM // N_CHIPS
ROW_SHARD = D // N_CHIPS

ADAM_LR, ADAM_B1, ADAM_B2, ADAM_EPS, ADAM_WD, ADAM_STEP = 0.001, 0.9, 0.999, 1e-08, 0.01, 10

NT_DIMS = (((1,), (1,)), ((), ()))


def _params(sem=None, vmem_mb=48):
    return pltpu.CompilerParams(dimension_semantics=sem, vmem_limit_bytes=vmem_mb << 20)


def _sds(shape, dtype):
    return pltpu.HBM(tuple(shape), dtype)


def _pin(*arrays):
    return [pltpu.with_memory_space_constraint(a, pltpu.HBM) for a in arrays]


def _sig(v):
    return jax.nn.sigmoid(v)


def _dsilu(v, s):
    return s * (1.0 + v * (1.0 - s))


def _dot(a, b):
    return jnp.dot(a, b, preferred_element_type=F32)


def _dot_nt(a, b):
    return lax.dot_general(a, b, NT_DIMS, preferred_element_type=F32)


def _qk_mats():
    i = lax.broadcasted_iota(jnp.int32, (GROUP_W, GROUP_W), 0)
    j = lax.broadcasted_iota(jnp.int32, (GROUP_W, GROUP_W), 1)
    mean = jnp.where((i >> 6) == (j >> 6), 1.0 / HEAD_DIM, 0.0).astype(BF16)
    turn = jnp.where((i == j + 16) & ((j & 16) == 0), -1.0,
                     jnp.where((i == j - 16) & ((j & 16) != 0), 1.0, 0.0)).astype(BF16)
    return mean, turn


def _apply(v, mat):
    hi = v.astype(BF16)
    lo = (v - hi.astype(F32)).astype(BF16)
    return _dot(hi, mat) + _dot(lo, mat)


def _qk_fwd(v, g, cos, sin, mats):
    mean, turn = mats
    r = lax.rsqrt(_apply(v * v, mean) + EPS)
    n = v * r * g
    return n * cos + _apply(n, turn) * sin, r


def _qk_bwd(dy, v, r, g, cos, sin, mats):
    mean, turn = mats
    dn = dy * cos - _apply(dy, turn) * sin
    dyg = dn * g
    dv = r * dyg - v * (r * r * r) * _apply(dyg * v, mean)
    return dv, dn * v * r


def _rms_bwd(dxn, v, r, g):
    dxg = dxn * g
    return r * dxg - v * (r * r * r) * jnp.mean(dxg * v, axis=-1, keepdims=True)


def _glu(a):
    return a[:, :D] * _sig(a[:, D:])


def _gather_weights(w_in_s, w3_s, conv_w_s, meta_s):
    def body(win_ref, wa_ref, wb_ref, wc_ref, cw_ref, mt_ref, win_o, cw_o, mt_o, w3b_o, win_b, w3_b,
             send, recv, fsend, frecv, lsem, csem):
        x, y, c = _xyc()
        o = 1 - c
        me = 2 * x + y
        win_b[...] = win_ref[...].astype(BF16)
        for i, ref in enumerate((wa_ref, wb_ref, wc_ref)):
            w3_b[i] = ref[0].astype(BF16)
        cast = pltpu.make_async_copy(w3_b, w3b_o, csem)
        cast.start()
        items = (
            (lambda h: win_b.at[pl.ds(h * 512, 512), :],
             lambda p, h: win_o.at[pl.ds(h * 512, 512), pl.ds(p * W_IN_SHARD, W_IN_SHARD)]),
            (lambda h: cw_ref.at[pl.ds(h * 16, 16), :],
             lambda p, h: cw_o.at[pl.ds(h * 16, 16), pl.ds(p * ROW_SHARD, ROW_SHARD)]),
            (lambda h: mt_ref.at[pl.ds(h * 8, 8), :],
             lambda p, h: mt_o.at[pl.ds(h * 8, 8), pl.ds(p * ROW_SHARD, ROW_SHARD)]),
        )
        peers = ((1 - x, y), (x, 1 - y), (1 - x, 1 - y))

        def remote(src, dst, s_sem, r_sem, to):
            return pltpu.make_async_remote_copy(src_ref=src, dst_ref=dst, send_sem=s_sem, recv_sem=r_sem,
                                                device_id=to, device_id_type=MESH)

        started = []
        for a, (half, place) in enumerate(items):
            for h in range(2):
                loc = pltpu.make_async_copy(half(h), place(me, h), lsem.at[a, h])
                loc.start()
                started.append(loc.wait)
            for k, (px, py) in enumerate(peers):
                cp = remote(half(c), place(me, c), send.at[a, k], recv.at[a, k], (px, py, c))
                cp.start()
                started.append(cp.wait_send)
        for k, (px, py) in enumerate(peers):
            for a, (half, place) in enumerate(items):
                got = place(2 * px + py, c)
                remote(got, got, send.at[a, k], recv.at[a, k], (px, py, c)).wait_recv()
                fw = remote(got, got, fsend.at[a, k], frecv.at[a, k], (x, y, o))
                fw.start()
                started.append(fw.wait_send)
        for k, (px, py) in enumerate(peers):
            for a, (half, place) in enumerate(items):
                theirs = place(2 * px + py, o)
                remote(theirs, theirs, fsend.at[a, k], frecv.at[a, k], (x, y, o)).wait_recv()
        for wait in started:
            wait()
        cast.wait()

    any_spec = pl.BlockSpec(memory_space=pl.ANY)
    vmem = pl.BlockSpec(memory_space=pltpu.VMEM)
    return pl.pallas_call(
        body, name="gather_weights",
        out_shape=(_sds((D, IN_DIM), BF16), _sds((32, D), F32), _sds((N_META, D), F32),
                   _sds((3, ROW_SHARD, D), BF16)),
        in_specs=[vmem] * 6,
        out_specs=(any_spec, any_spec, any_spec, any_spec),
        scratch_shapes=[pltpu.VMEM((D, W_IN_SHARD), BF16), pltpu.VMEM((3, ROW_SHARD, D), BF16),
                        pltpu.SemaphoreType.DMA((3, 3)), pltpu.SemaphoreType.DMA((3, 3)),
                        pltpu.SemaphoreType.DMA((3, 3)), pltpu.SemaphoreType.DMA((3, 3)),
                        pltpu.SemaphoreType.DMA((3, 2)), pltpu.SemaphoreType.DMA],
        compiler_params=pltpu.CompilerParams(vmem_limit_bytes=40 << 20),
    )(w_in_s, *w3_s, conv_w_s, meta_s)


def _w3_place(ref, p, h):
    return ref.at[:, pl.ds(p * ROW_SHARD + h * 128, 128), :]


def _w3_copies(w3b_ref, land_ref, send, recv):
    x, y, c = _xyc()
    me = 2 * x + y
    peers = ((1 - x, y), (x, 1 - y), (1 - x, 1 - y))
    return [pltpu.make_async_remote_copy(src_ref=w3b_ref.at[:, pl.ds(c * 128, 128), :], dst_ref=_w3_place(land_ref, me, c),
                                         send_sem=send.at[k], recv_sem=recv.at[k], device_id=(px, py, c),
                                         device_id_type=MESH)
            for k, (px, py) in enumerate(peers)]


def _w3_start(w3b, after):
    land = lax.empty((3, D, D), BF16)

    def body(w3b_ref, land_ref, after_ref, send, recv, w3b_thru, land_thru):
        for cp in _w3_copies(w3b_ref, land_ref, send, recv):
            cp.start()

    outs = pl.pallas_call(
        body, name="w3_start",
        out_shape=(pltpu.SemaphoreType.DMA((3,)), pltpu.SemaphoreType.DMA((3,)),
                   pltpu.HBM(w3b.shape, BF16), pltpu.HBM(land.shape, BF16)),
        in_specs=[_HBM, _HBM, pl.BlockSpec(memory_space=pl.ANY)],
        out_specs=(_SEM, _SEM, _HBM, _HBM),
        input_output_aliases={0: 2, 1: 3},
        compiler_params=pltpu.CompilerParams(has_side_effects=_EFFECT),
    )(*_pin(w3b, land), after)
    return outs


def _w3_wait(send, recv, w3b, land, after):
    def body(w3b_ref, land_ref, send_ref, recv_ref, after_ref, w3b_out, land_out):
        x, y, c = _xyc()
        peers = ((1 - x, y), (x, 1 - y), (1 - x, 1 - y))
        for k, (cp, (px, py)) in enumerate(zip(_w3_copies(w3b_ref, land_ref, send_ref, recv_ref), peers)):
            cp.wait_send()
            got = _w3_place(land_ref, 2 * px + py, c)
            pltpu.make_async_remote_copy(src_ref=got, dst_ref=got, send_sem=send_ref.at[k], recv_sem=recv_ref.at[k],
                                         device_id=(px, py, c), device_id_type=MESH).wait_recv()

    outs = pl.pallas_call(
        body, name="w3_wait",
        out_shape=(pltpu.HBM(w3b.shape, BF16), pltpu.HBM(land.shape, BF16)),
        in_specs=[_HBM, _HBM, _SEM, _SEM, pl.BlockSpec(memory_space=pl.ANY)],
        out_specs=(_HBM, _HBM),
        input_output_aliases={0: 0, 1: 1},
        compiler_params=pltpu.CompilerParams(has_side_effects=_EFFECT),
    )(w3b, land, send, recv, after)
    return outs


def _w3_finish(w3b, land):
    def body(w3b_ref, land_in, land_ref, fsend, frecv, lsem):
        x, y, c = _xyc()
        o = 1 - c
        me = 2 * x + y
        peers = ((1 - x, y), (x, 1 - y), (1 - x, 1 - y))
        loc = pltpu.make_async_copy(w3b_ref, land_ref.at[:, pl.ds(me * ROW_SHARD, ROW_SHARD), :], lsem)
        loc.start()
        fws = []
        for k, (px, py) in enumerate(peers):
            got = _w3_place(land_ref, 2 * px + py, c)
            fw = pltpu.make_async_remote_copy(src_ref=got, dst_ref=got, send_sem=fsend.at[k], recv_sem=frecv.at[k],
                                              device_id=(x, y, o), device_id_type=MESH)
            fw.start()
            fws.append(fw)
        for k, (px, py) in enumerate(peers):
            theirs = _w3_place(land_ref, 2 * px + py, o)
            pltpu.make_async_remote_copy(src_ref=theirs, dst_ref=theirs, send_sem=fsend.at[k], recv_sem=frecv.at[k],
                                         device_id=(x, y, o), device_id_type=MESH).wait_recv()
        for fw in fws:
            fw.wait_send()
        loc.wait()

    any_spec = pl.BlockSpec(memory_space=pl.ANY)
    return pl.pallas_call(
        body, name="w3_finish", out_shape=_sds((3, D, D), BF16),
        in_specs=[any_spec, any_spec], out_specs=any_spec,
        input_output_aliases={1: 0},
        scratch_shapes=[pltpu.SemaphoreType.DMA((3,)), pltpu.SemaphoreType.DMA((3,)), pltpu.SemaphoreType.DMA],
    )(*_pin(w3b, land))


def _meta_fwd(meta_full, norm_g, w_full):
    def body(m_ref, g_ref, wc_ref, wkv_ref, xnt_ref, pc_ref, pkv_ref):
        v = m_ref[...]
        r = lax.rsqrt(jnp.mean(v * v, axis=-1, keepdims=True) + EPS)
        xn = v * r * g_ref[...]
        xnb = xn.astype(BF16)
        pad = jnp.concatenate([xn, jnp.zeros((128 - N_META, D), F32)], axis=0)
        xnt_ref[...] = pad.T.astype(BF16)
        pc_ref[...] = _dot(xnb, wc_ref[...])
        pkv_ref[...] = _dot(xnb, wkv_ref[...])

    return pl.pallas_call(
        body, name="meta_fwd", grid=(1,),
        out_shape=(_sds((D, 128), BF16), _sds((N_META, 2048), F32),
                   _sds((N_META, 512), F32)),
        in_specs=[pl.BlockSpec((N_META, D), lambda i: (0, 0)), pl.BlockSpec((1, D), lambda i: (0, 0)),
                  pl.BlockSpec((D, 2048), lambda i: (0, 0)), pl.BlockSpec((D, 512), lambda i: (0, G_KV[0] // 512))],
        out_specs=(pl.BlockSpec((D, 128), lambda i: (0, 0)), pl.BlockSpec((N_META, 2048), lambda i: (0, 0)),
                   pl.BlockSpec((N_META, 512), lambda i: (0, 0))),
        compiler_params=_params(("arbitrary",), 32),
    )(*_pin(meta_full, norm_g, w_full, w_full))


def _in_proj(x2, norm_g, w_full, tm):
    rows = x2.shape[0]
    groups = (G_CONV, G_CZ, G_Q, G_KV, G_E)

    def body(x_ref, g_ref, w_hbm, *rest):
        outs, xnt_ref, w_vmem, sem = rest[:5], rest[5], rest[6], rest[7]

        @pl.when(pl.program_id(0) == 0)
        def _():
            cp = pltpu.make_async_copy(w_hbm, w_vmem, sem)
            cp.start()
            cp.wait()

        v = x_ref[...]
        r = lax.rsqrt(jnp.mean(v * v, axis=-1, keepdims=True) + EPS)
        xn = v * r * g_ref[...]
        xnb = xn.astype(BF16)
        xnt_ref[...] = xn.T.astype(BF16)
        for ref, (off, wd) in zip(outs, groups):
            for c0 in range(0, wd, 512):
                ref[:, c0:c0 + 512] = _dot(xnb, w_vmem[:, off + c0:off + c0 + 512])

    return pl.pallas_call(
        body, name="in_proj", grid=(rows // tm,),
        out_shape=tuple(_sds((rows, wd), F32) for _, wd in groups)
        + (_sds((D, rows), BF16),),
        in_specs=[pl.BlockSpec((tm, D), lambda i: (i, 0)), pl.BlockSpec((1, D), lambda i: (0, 0)),
                  pl.BlockSpec(memory_space=pl.ANY)],
        out_specs=tuple(pl.BlockSpec((tm, wd), lambda i: (i, 0)) for _, wd in groups)
        + (pl.BlockSpec((D, tm), lambda i: (0, i)),),
        scratch_shapes=[pltpu.VMEM((D, IN_DIM), BF16), pltpu.SemaphoreType.DMA],
        compiler_params=_params(("arbitrary",), 56),
    )(*_pin(x2, norm_g, w_full))


def _halo_specs(width, tm, nt, rows):
    h16 = tm // 16
    return [pl.BlockSpec((tm, width), lambda b, i: (b * nt + i, 0)),
            pl.BlockSpec((16, width), lambda b, i: (jnp.maximum((b * nt + i) * h16 - 1, 0), 0)),
            pl.BlockSpec((16, width), lambda b, i: (jnp.minimum((b * nt + i + 1) * h16, rows // 16 - 1), 0))]


def _fill_uext(uext, cur, prev, nxt, meta, i, nt, tm):
    uext[0:16] = jnp.where(i == 0, _glu(meta[...]), _glu(prev[...]))
    uext[16:16 + tm] = _glu(cur[...])
    uext[16 + tm:32 + tm] = jnp.where(i == nt - 1, 0.0, _glu(nxt[...]))


def _shifted_copies(dst, src, n):
    for r in range(1, 8):
        dst[r, 0:n] = src[r:r + n]


def _rows32(shifted, src, start, cols):
    q8, r = divmod(start, 8)
    if r == 0:
        return src[start:start + 32, cols]
    return shifted[r, 8 * q8:8 * q8 + 32, cols]


def _conv_fwd(pconv, pm_conv, conv_w, conv_b, nb, tm):
    rows = pconv.shape[0]
    nt = rows // nb // tm

    def body(cur, prev, nxt, meta, w_ref, b_ref, o_ref, uext, ush):
        i = pl.program_id(1)
        _fill_uext(uext, cur, prev, nxt, meta, i, nt, tm)
        _shifted_copies(ush, uext, tm + 24)
        for r0 in range(0, tm, 32):
            for c0 in range(0, D, 256):
                acc = jnp.zeros((32, 256), F32) + b_ref[:, c0:c0 + 256]
                for j in range(CONV_K):
                    acc = acc + _rows32(ush, uext, r0 + j + 1, slice(c0, c0 + 256)) * w_ref[j:j + 1, c0:c0 + 256]
                o_ref[r0:r0 + 32, c0:c0 + 256] = acc

    return pl.pallas_call(
        body, name="conv_fwd", grid=(nb, nt),
        out_shape=_sds((rows, D), F32),
        in_specs=_halo_specs(2048, tm, nt, rows)
        + [pl.BlockSpec((16, 2048), lambda b, i: (0, 0)), pl.BlockSpec((32, D), lambda b, i: (0, 0)),
           pl.BlockSpec((1, D), lambda b, i: (0, 0))],
        out_specs=pl.BlockSpec((tm, D), lambda b, i: (b * nt + i, 0)),
        scratch_shapes=[pltpu.VMEM((tm + 32, D), F32), pltpu.VMEM((8, tm + 24, D), F32)],
        compiler_params=_params(("parallel", "parallel"), 40),
    )(*_pin(pconv, pconv, pconv, pm_conv, conv_w, conv_b))


def _kv_prep(pkv, pm_kv, kg, cos, sin, nb):
    rows = pkv.shape[0]
    s_len = rows // nb
    tk = min(512, s_len)
    nt = s_len // tk

    def body(kv_ref, m_ref, g_ref, cos_ref, sin_ref, k_o, v_o, k2_o, v2_o):
        i = pl.program_id(1)
        mats = _qk_mats()
        kv = kv_ref[...]
        kr, _ = _qk_fwd(kv[:, :GROUP_W], g_ref[...], cos_ref[...], sin_ref[...], mats)
        ones = _ones_cols(tk, tk)
        for h in range(N_KV):
            k_o[0, h] = kr[:, HEAD_DIM * h:HEAD_DIM * (h + 1)].astype(BF16)
            vh = kv[:, GROUP_W + HEAD_DIM * h:GROUP_W + HEAD_DIM * (h + 1)]
            v_o[0, h] = jnp.concatenate([vh, ones], axis=1).astype(BF16)

        @pl.when(i == 0)
        def _():
            kvm = m_ref[...]
            km = kvm[:, :GROUP_W]
            kn = km * lax.rsqrt(_apply(km * km, mats[0]) + EPS) * g_ref[...]
            zeros = jnp.zeros((KEY_PAD - N_META, GROUP_W), F32)
            kfull = jnp.concatenate([kn, zeros], axis=0)
            vfull = jnp.concatenate([kvm[:, GROUP_W:], zeros], axis=0)
            ones_m = _ones_cols(KEY_PAD, N_META)
            for h in range(N_KV):
                k2_o[0, h] = kfull[:, HEAD_DIM * h:HEAD_DIM * (h + 1)].astype(BF16)
                v2_o[0, h] = jnp.concatenate([vfull[:, HEAD_DIM * h:HEAD_DIM * (h + 1)], ones_m], axis=1).astype(BF16)

    return pl.pallas_call(
        body, name="kv_prep", grid=(nb, nt),
        out_shape=(_sds((nb, N_KV, s_len, HEAD_DIM), BF16), _sds((nb, N_KV, s_len, 2 * HEAD_DIM), BF16),
                   _sds((nb, N_KV, KEY_PAD, HEAD_DIM), BF16), _sds((nb, N_KV, KEY_PAD, 2 * HEAD_DIM), BF16)),
        in_specs=[pl.BlockSpec((tk, 512), lambda b, i: (b * nt + i, 0)),
                  pl.BlockSpec((N_META, 512), lambda b, i: (0, 0)), pl.BlockSpec((1, GROUP_W), lambda b, i: (0, 0)),
                  pl.BlockSpec((tk, GROUP_W), lambda b, i: (i, 0)),
                  pl.BlockSpec((tk, GROUP_W), lambda b, i: (i, 0))],
        out_specs=(pl.BlockSpec((1, N_KV, tk, HEAD_DIM), lambda b, i: (b, 0, i, 0)),
                   pl.BlockSpec((1, N_KV, tk, 2 * HEAD_DIM), lambda b, i: (b, 0, i, 0)),
                   pl.BlockSpec((1, N_KV, KEY_PAD, HEAD_DIM), lambda b, i: (b, 0, 0, 0)),
                   pl.BlockSpec((1, N_KV, KEY_PAD, 2 * HEAD_DIM), lambda b, i: (b, 0, 0, 0))),
        compiler_params=_params(("parallel", "arbitrary"), 40),
    )(*_pin(pkv, pm_kv, kg, cos, sin))


def _ones_cols(rows, valid):
    r = lax.broadcasted_iota(jnp.int32, (rows, HEAD_DIM), 0)
    col = lax.broadcasted_iota(jnp.int32, (rows, HEAD_DIM), 1)
    return jnp.where((col < 2) & (r < valid), 1.0, 0.0).astype(F32)


def _tail_bias():
    col = lax.broadcasted_iota(jnp.int32, (1, KEY_PAD), 1)
    return jnp.where(col < N_META, 0.0, -1e30).astype(F32)


LOG2E = 1.4426950408889634


def _q_prep(pq, qg, cos, sin, nb, tm):
    rows = pq.shape[0]
    nt = rows // nb // tm
    scale = 1.0 / math.sqrt(HEAD_DIM)

    def body(q_ref, g_ref, cos_ref, sin_ref, q2_o, qt_o):
        gv, cosv, sinv = g_ref[...], cos_ref[...], sin_ref[...]
        mats = _qk_mats()
        for g in range(N_KV):
            gs = slice(GROUP_W * g, GROUP_W * (g + 1))
            qr, _ = _qk_fwd(q_ref[:, gs], gv, cosv, sinv, mats)
            q2_o[:, gs] = (qr * (scale * LOG2E)).astype(BF16)
            qt_o[gs, :] = (qr * scale).T.astype(BF16)

    row = pl.BlockSpec((tm, D), lambda b, i: (b * nt + i, 0))
    rope = pl.BlockSpec((tm, GROUP_W), lambda b, i: (i, 0))
    return pl.pallas_call(
        body, name="q_prep", grid=(nb, nt),
        out_shape=(_sds((rows, D), BF16), _sds((D, rows), BF16)),
        in_specs=[row, pl.BlockSpec((1, GROUP_W), lambda b, i: (0, 0)), rope, rope],
        out_specs=(row, pl.BlockSpec((D, tm), lambda b, i: (0, b * nt + i))),
        compiler_params=_params(("parallel", "parallel"), 32),
    )(*_pin(pq, qg, cos, sin))


def _kv_specs(s_len):
    return [pl.BlockSpec((1, 1, s_len, HEAD_DIM), lambda b, g, i: (b, g, 0, 0)),
            pl.BlockSpec((1, 1, s_len, 2 * HEAD_DIM), lambda b, g, i: (b, g, 0, 0)),
            pl.BlockSpec((1, 1, KEY_PAD, HEAD_DIM), lambda b, g, i: (b, g, 0, 0)),
            pl.BlockSpec((1, 1, KEY_PAD, 2 * HEAD_DIM), lambda b, g, i: (b, g, 0, 0))]


def _attn_fwd(q2, kv4, nb, tq):
    rows = q2.shape[0]
    s_len = rows // nb
    nq = s_len // tq

    def body(q_ref, k1_ref, v1_ref, k2_ref, v2_ref, o_ref, lse_ref):
        qs = q_ref[...]
        k1, k2, v1, v2 = k1_ref[0, 0], k2_ref[0, 0], v1_ref[0, 0], v2_ref[0, 0]
        bias = _tail_bias()
        outs, lses = [], []

        def scores(h):
            qh = qs[:, HEAD_DIM * h:HEAD_DIM * (h + 1)]
            return _dot_nt(qh, k1), _dot_nt(qh, k2) + bias

        ahead = scores(0)
        for h in range(GQA):
            s1, s2 = ahead
            if h + 1 < GQA:
                ahead = scores(h + 1)
            m = jnp.maximum(jnp.max(s1, axis=-1, keepdims=True), jnp.max(s2, axis=-1, keepdims=True))
            oe = _dot(jnp.exp2(s1 - m).astype(BF16), v1) + _dot(jnp.exp2(s2 - m).astype(BF16), v2)
            l = oe[:, HEAD_DIM:HEAD_DIM + 1]
            outs.append(oe[:, :HEAD_DIM] / l)
            lses.append(m + jnp.log2(l))
        o_ref[...] = jnp.concatenate(outs, axis=1)
        lse_ref[0, 0] = jnp.concatenate(lses, axis=1)

    return pl.pallas_call(
        body, name="attn_fwd", grid=(nb, N_KV, nq),
        out_shape=(_sds((rows, D), F32), _sds((nb, N_KV, s_len, GQA), F32)),
        in_specs=[pl.BlockSpec((tq, GROUP_W), lambda b, g, i: (b * nq + i, g))] + _kv_specs(s_len),
        out_specs=(pl.BlockSpec((tq, GROUP_W), lambda b, g, i: (b * nq + i, g)),
                   pl.BlockSpec((1, 1, tq, GQA), lambda b, g, i: (b, g, i, 0))),
        compiler_params=_params(("parallel", "parallel", "parallel"), 48),
    )(*_pin(q2, *kv4))


def _mid(x2, t2, c0, cz, o, e, w3, cn_g, cn_b, tm):
    rows = x2.shape[0]

    def body(x_ref, t_ref, c0_ref, cz_ref, o_ref, e_ref, w_ref, g_ref, b_ref,
             dy_o, mt_o, c3t_o, o2t_o, dyc_o, dya_o, do_o, dc0_o, dcz_o, de_o, sums_o):
        wco, wao, wo = w_ref[0], w_ref[1], w_ref[2]
        cn_g_v = g_ref[...]
        c0v = c0_ref[...]
        xc = c0v - jnp.mean(c0v, axis=-1, keepdims=True)
        rstd = lax.rsqrt(jnp.mean(xc * xc, axis=-1, keepdims=True) + EPS)
        n = xc * rstd
        c1 = n * cn_g_v + b_ref[...]
        s1 = _sig(c1)
        c2 = c1 * s1
        czv = cz_ref[...]
        sz = _sig(czv)
        gz = czv * sz
        c3 = c2 * gz
        yc = _dot(c3.astype(BF16), wco)
        az, gc, ga = e_ref[:, :D], e_ref[:, D:2 * D], e_ref[:, 2 * D:]
        saz = _sig(az)
        gaz = az * saz
        ov = o_ref[...]
        o2 = ov * gaz
        ya = _dot(o2.astype(BF16), wao)
        sc, sa = _sig(gc), _sig(ga)
        merged = sc * yc + sa * ya
        out = _dot(merged.astype(BF16), wo)
        err = x_ref[...] + out - t_ref[...]
        dy = err * (1.0 / D)
        dy_o[...] = dy
        dm = _dot_nt(dy.astype(BF16), wo)
        dyc = dm * sc
        dya = dm * sa
        dycb, dyab = dyc.astype(BF16), dya.astype(BF16)
        dyc_o[...] = dycb
        dya_o[...] = dyab
        de_o[:, D:2 * D] = (dyc * yc * (1.0 - sc)).astype(BF16)
        de_o[:, 2 * D:] = (dya * ya * (1.0 - sa)).astype(BF16)
        dc3 = _dot_nt(dycb, wco)
        do2 = _dot_nt(dyab, wao)
        do_o[...] = do2 * gaz
        de_o[:, :D] = (do2 * ov * _dsilu(az, saz)).astype(BF16)
        dcz_o[...] = (dc3 * c2 * _dsilu(czv, sz)).astype(BF16)
        dc1 = dc3 * gz * _dsilu(c1, s1)
        dn = dc1 * cn_g_v
        dc0 = rstd * (dn - jnp.mean(dn, axis=-1, keepdims=True) - n * jnp.mean(dn * n, axis=-1, keepdims=True))
        dc0_o[...] = dc0
        mt_o[...] = merged.T.astype(BF16)
        c3t_o[...] = c3.T.astype(BF16)
        o2t_o[...] = o2.T.astype(BF16)

        @pl.when(pl.program_id(0) == 0)
        def _():
            sums_o[...] = jnp.zeros_like(sums_o)

        sums_o[0:1, :] += jnp.sum(dc1 * n, axis=0, keepdims=True)
        sums_o[1:2, :] += jnp.sum(dc1, axis=0, keepdims=True)
        sums_o[2:3, :] += jnp.sum(dc0, axis=0, keepdims=True)
        sums_o[3:4, :] += jnp.sum(err * err, axis=0, keepdims=True)

    row = lambda wd: pl.BlockSpec((tm, wd), lambda i: (i, 0))
    col = pl.BlockSpec((D, tm), lambda i: (0, i))
    vec = pl.BlockSpec((1, D), lambda i: (0, 0))
    f32o = lambda wd: _sds((rows, wd), F32)
    b16o = lambda wd: _sds((rows, wd), BF16)
    tpo = _sds((D, rows), BF16)
    return pl.pallas_call(
        body, name="mid", grid=(rows // tm,),
        out_shape=(f32o(D), tpo, tpo, tpo, b16o(D), b16o(D), f32o(D), f32o(D), b16o(D), b16o(3 * D),
                   _sds((8, D), F32)),
        in_specs=[row(D), row(D), row(D), row(D), row(D), row(3 * D),
                  pl.BlockSpec((3, D, D), lambda i: (0, 0, 0)), vec, vec],
        out_specs=(row(D), col, col, col, row(D), row(D), row(D), row(D), row(D), row(3 * D),
                   pl.BlockSpec((8, D), lambda i: (0, 0))),
        compiler_params=_params(("arbitrary",), 60),
    )(*_pin(x2, t2, c0, cz, o, e, w3, cn_g, cn_b))


def _do_prep(d_o, o, tm):
    rows = d_o.shape[0]

    def body(do_ref, o_ref, doe_o, dot_o):
        dov = do_ref[...]
        prod = dov * o_ref[...]
        col = lax.broadcasted_iota(jnp.int32, (tm, HEAD_DIM), 1)
        for h in range(D // HEAD_DIM):
            hs = slice(HEAD_DIM * h, HEAD_DIM * (h + 1))
            delta = jnp.sum(prod[:, hs], axis=-1, keepdims=True)
            d_hi = delta.astype(BF16).astype(F32)
            tail = jnp.where(col == 0, -d_hi, jnp.where(col == 1, d_hi - delta, 0.0))
            doe_o[:, 2 * HEAD_DIM * h:2 * HEAD_DIM * (h + 1)] = jnp.concatenate([dov[:, hs], tail], axis=1).astype(BF16)
        dot_o[...] = dov.T.astype(BF16)

    row = pl.BlockSpec((tm, D), lambda i: (i, 0))
    return pl.pallas_call(
        body, name="do_prep", grid=(rows // tm,),
        out_shape=(_sds((rows, 2 * D), BF16), _sds((D, rows), BF16)),
        in_specs=[row, row],
        out_specs=(pl.BlockSpec((tm, 2 * D), lambda i: (i, 0)), pl.BlockSpec((D, tm), lambda i: (0, i))),
        compiler_params=_params(("parallel",), 32),
    )(*_pin(d_o, o))


def _q_post(dqr, pq, qg, cos, sin, nb, tm):
    rows = pq.shape[0]
    nt = rows // nb // tm

    def body(dq_ref, q_ref, g_ref, cos_ref, sin_ref, dq_o, dg_o):
        @pl.when((pl.program_id(0) == 0) & (pl.program_id(1) == 0))
        def _():
            dg_o[...] = jnp.zeros_like(dg_o)

        gv, cosv, sinv = g_ref[...], cos_ref[...], sin_ref[...]
        acc = jnp.zeros((1, GROUP_W), F32)
        mats = _qk_mats()
        for g in range(N_KV):
            gs = slice(GROUP_W * g, GROUP_W * (g + 1))
            qv = q_ref[:, gs]
            r = lax.rsqrt(_apply(qv * qv, mats[0]) + EPS)
            dq, dgr = _qk_bwd(dq_ref[:, gs], qv, r, gv, cosv, sinv, mats)
            dq_o[:, gs] = dq.astype(BF16)
            acc = acc + jnp.sum(dgr, axis=0, keepdims=True)
        dg_o[...] += acc

    row = pl.BlockSpec((tm, D), lambda b, i: (b * nt + i, 0))
    rope = pl.BlockSpec((tm, GROUP_W), lambda b, i: (i, 0))
    vec = pl.BlockSpec((1, GROUP_W), lambda b, i: (0, 0))
    return pl.pallas_call(
        body, name="q_post", grid=(nb, nt),
        out_shape=(_sds((rows, D), BF16), _sds((1, GROUP_W), F32)),
        in_specs=[row, row, vec, rope, rope], out_specs=(row, vec),
        compiler_params=_params(("arbitrary", "arbitrary"), 32),
    )(*_pin(dqr, pq, qg, cos, sin))


def _attn_bwd(q2, qst, kv4, doe, dot_, lse, nb, tq):
    rows = q2.shape[0]
    s_len = rows // nb
    nq = s_len // tq
    scale = 1.0 / math.sqrt(HEAD_DIM)

    def body(q_ref, qt_ref, k1_ref, v1_ref, k2_ref, v2_ref, doe_ref, dot_ref, lse_ref,
             dq_o, dkt_o, dvt_o, dkt2_o, dvt2_o):
        i = pl.program_id(2)
        lse = lse_ref[0, 0]
        k1, k2, v1, v2 = k1_ref[0, 0], k2_ref[0, 0], v1_ref[0, 0], v2_ref[0, 0]
        bias = _tail_bias()
        dkt1, dkt2 = jnp.zeros((HEAD_DIM, s_len), F32), jnp.zeros((HEAD_DIM, KEY_PAD), F32)
        dvt1, dvt2 = jnp.zeros((HEAD_DIM, s_len), F32), jnp.zeros((HEAD_DIM, KEY_PAD), F32)

        def products(h):
            qh = q_ref[:, HEAD_DIM * h:HEAD_DIM * (h + 1)]
            dh = doe_ref[:, 2 * HEAD_DIM * h:2 * HEAD_DIM * (h + 1)]
            return _dot_nt(qh, k1), _dot_nt(qh, k2) + bias, _dot_nt(dh, v1), _dot_nt(dh, v2)

        ahead = products(0)
        for h in range(GQA):
            hs = slice(HEAD_DIM * h, HEAD_DIM * (h + 1))
            s1, s2, dp1, dp2 = ahead
            if h + 1 < GQA:
                ahead = products(h + 1)
            lse_h = lse[:, h:h + 1]
            p1 = jnp.exp2(s1 - lse_h)
            p2 = jnp.exp2(s2 - lse_h)
            ds1 = (p1 * dp1).astype(BF16)
            ds2 = (p2 * dp2).astype(BF16)
            dq_o[:, hs] = (_dot(ds1, k1) + _dot(ds2, k2)) * scale
            dkt1 = dkt1 + _dot(qt_ref[hs, :], ds1)
            dkt2 = dkt2 + _dot(qt_ref[hs, :], ds2)
            dvt1 = dvt1 + _dot(dot_ref[hs, :], p1.astype(BF16))
            dvt2 = dvt2 + _dot(dot_ref[hs, :], p2.astype(BF16))

        @pl.when(i == 0)
        def _():
            dkt_o[0, 0], dkt2_o[0, 0], dvt_o[0, 0], dvt2_o[0, 0] = dkt1, dkt2, dvt1, dvt2

        @pl.when(i > 0)
        def _():
            dkt_o[0, 0] += dkt1
            dkt2_o[0, 0] += dkt2
            dvt_o[0, 0] += dvt1
            dvt2_o[0, 0] += dvt2

    qspec = pl.BlockSpec((tq, GROUP_W), lambda b, g, i: (b * nq + i, g))
    qtspec = pl.BlockSpec((GROUP_W, tq), lambda b, g, i: (g, b * nq + i))
    tspec = pl.BlockSpec((1, 1, HEAD_DIM, s_len), lambda b, g, i: (b, g, 0, 0))
    t2spec = pl.BlockSpec((1, 1, HEAD_DIM, KEY_PAD), lambda b, g, i: (b, g, 0, 0))
    tshape = _sds((nb, N_KV, HEAD_DIM, s_len), F32)
    t2shape = _sds((nb, N_KV, HEAD_DIM, KEY_PAD), F32)
    return pl.pallas_call(
        body, name="attn_bwd", grid=(nb, N_KV, nq),
        out_shape=(_sds((rows, D), F32), tshape, tshape, t2shape, t2shape),
        in_specs=[qspec, qtspec] + _kv_specs(s_len)
        + [pl.BlockSpec((tq, 2 * GROUP_W), lambda b, g, i: (b * nq + i, g)), qtspec,
           pl.BlockSpec((1, 1, tq, GQA), lambda b, g, i: (b, g, i, 0))],
        out_specs=(qspec, tspec, tspec, t2spec, t2spec),
        compiler_params=_params(("parallel", "parallel", "arbitrary"), 56),
    )(*_pin(q2, qst, *kv4, doe, dot_, lse))


def _kv_bwd(dkt, dvt, dkt2, dvt2, pkv, pm_kv, kg, cos, sin, nb):
    rows = pkv.shape[0]
    s_len = rows // nb
    tk = min(512, s_len)
    nt = s_len // tk

    def body(dk_ref, dv_ref, dk2_ref, dv2_ref, kv_ref, m_ref, g_ref, cos_ref, sin_ref, d_o, dm_o, dg_o):
        b, i = pl.program_id(0), pl.program_id(1)
        gv = g_ref[...]
        mats = _qk_mats()

        @pl.when((b == 0) & (i == 0))
        def _():
            dg_o[...] = jnp.zeros_like(dg_o)

        kx = kv_ref[:, :GROUP_W]
        r = lax.rsqrt(_apply(kx * kx, mats[0]) + EPS)
        dk, dgr = _qk_bwd(dk_ref[0].T, kx, r, gv, cos_ref[...], sin_ref[...], mats)
        d_o[:, :GROUP_W] = dk.astype(BF16)
        d_o[:, GROUP_W:] = dv_ref[0].T.astype(BF16)
        dg_o[...] += jnp.sum(dgr, axis=0, keepdims=True)

        @pl.when(i == 0)
        def _():
            kxm = m_ref[:, :GROUP_W]
            rm = lax.rsqrt(_apply(kxm * kxm, mats[0]) + EPS)
            dn = dk2_ref[0].T[0:N_META]
            dyg = dn * gv
            dm_o[0, :, :GROUP_W] = rm * dyg - kxm * (rm * rm * rm) * _apply(dyg * kxm, mats[0])
            dm_o[0, :, GROUP_W:] = dv2_ref[0].T[0:N_META]
            dg_o[...] += jnp.sum(dn * kxm * rm, axis=0, keepdims=True)

    tspec = pl.BlockSpec((1, GROUP_W, tk), lambda b, i: (b, 0, i))
    t2spec = pl.BlockSpec((1, GROUP_W, KEY_PAD), lambda b, i: (b, 0, 0))
    rope = pl.BlockSpec((tk, GROUP_W), lambda b, i: (i, 0))
    return pl.pallas_call(
        body, name="kv_bwd", grid=(nb, nt),
        out_shape=(_sds((rows, 512), BF16), _sds((nb, N_META, 512), F32),
                   _sds((1, GROUP_W), F32)),
        in_specs=[tspec, tspec, t2spec, t2spec, pl.BlockSpec((tk, 512), lambda b, i: (b * nt + i, 0)),
                  pl.BlockSpec((N_META, 512), lambda b, i: (0, 0)), pl.BlockSpec((1, GROUP_W), lambda b, i: (0, 0)),
                  rope, rope],
        out_specs=(pl.BlockSpec((tk, 512), lambda b, i: (b * nt + i, 0)),
                   pl.BlockSpec((1, N_META, 512), lambda b, i: (b, 0, 0)),
                   pl.BlockSpec((1, GROUP_W), lambda b, i: (0, 0))),
        compiler_params=_params(("arbitrary", "arbitrary"), 40),
    )(*_pin(dkt, dvt, dkt2, dvt2, pkv, pm_kv, kg, cos, sin))


def _conv_bwd(dc0, pconv, pm_conv, conv_w, nb, tm):
    rows = pconv.shape[0]
    nt = rows // nb // tm

    def body(dcur, dprev, dnxt, cur, meta, w_ref, da_o, dam_o, gw_o, ucur, dext, dsh):
        b, i = pl.program_id(0), pl.program_id(1)
        ucur[...] = _glu(cur[...])
        dext[0:16] = jnp.zeros((16, D), F32)
        dext[16:32] = jnp.where(i == 0, 0.0, dprev[...])
        dext[32:32 + tm] = dcur[...]
        dext[32 + tm:48 + tm] = jnp.where(i == nt - 1, 0.0, dnxt[...])
        _shifted_copies(dsh, dext, tm + 40)

        @pl.when((b == 0) & (i == 0))
        def _():
            gw_o[...] = jnp.zeros_like(gw_o)

        for c0 in range(0, D, 256):
            cs = slice(c0, c0 + 256)
            for r0 in range(0, tm, 32):
                acc = jnp.zeros((32, 256), F32)
                for j in range(CONV_K):
                    acc = acc + _rows32(dsh, dext, r0 + 47 - j, cs) * w_ref[j:j + 1, cs]
                cv = cur[r0:r0 + 32, c0:c0 + 256]
                sg = _sig(cur[r0:r0 + 32, D + c0:D + c0 + 256])
                da_o[r0:r0 + 32, cs] = (acc * sg).astype(BF16)
                da_o[r0:r0 + 32, D + c0:D + c0 + 256] = (acc * cv * sg * (1.0 - sg)).astype(BF16)
            for j in range(CONV_K):
                acc = jnp.zeros((32, 256), F32)
                for r0 in range(0, tm, 32):
                    acc = acc + _rows32(dsh, dext, r0 + 47 - j, cs) * ucur[r0:r0 + 32, cs]
                gw_o[j:j + 1, cs] += jnp.sum(acc, axis=0, keepdims=True)

        @pl.when(i == 0)
        def _():
            for c0 in range(0, D, 256):
                cs = slice(c0, c0 + 256)
                cv = meta[:, c0:c0 + 256]
                sg = _sig(meta[:, D + c0:D + c0 + 256])
                um = cv * sg
                acc = jnp.zeros((16, 256), F32)
                for j in range(CONV_K):
                    d = dext[31 - j:47 - j, cs]
                    acc = acc + d * w_ref[j:j + 1, cs]
                    gw_o[j:j + 1, cs] += jnp.sum(d * um, axis=0, keepdims=True)
                dam_o[0, :, cs] = acc * sg
                dam_o[0, :, D + c0:D + c0 + 256] = acc * cv * sg * (1.0 - sg)

    return pl.pallas_call(
        body, name="conv_bwd", grid=(nb, nt),
        out_shape=(_sds((rows, 2048), BF16), _sds((nb, N_META, 2048), F32),
                   _sds((32, D), F32)),
        in_specs=_halo_specs(D, tm, nt, rows)
        + [pl.BlockSpec((tm, 2048), lambda b, i: (b * nt + i, 0)),
           pl.BlockSpec((16, 2048), lambda b, i: (0, 0)), pl.BlockSpec((32, D), lambda b, i: (0, 0))],
        out_specs=(pl.BlockSpec((tm, 2048), lambda b, i: (b * nt + i, 0)),
                   pl.BlockSpec((1, N_META, 2048), lambda b, i: (b, 0, 0)),
                   pl.BlockSpec((32, D), lambda b, i: (0, 0))),
        scratch_shapes=[pltpu.VMEM((tm, D), F32), pltpu.VMEM((tm + 48, D), F32), pltpu.VMEM((8, tm + 40, D), F32)],
        compiler_params=_params(("arbitrary", "arbitrary"), 48),
    )(*_pin(dc0, dc0, dc0, pconv, pm_conv, conv_w))


def _meta_bwd(dam, ddm, w_full, meta_full, norm_g):
    nb = dam.shape[0]

    def body(a_ref, d_ref, wc_ref, wkv_ref, m_ref, g_ref, gm_o, dg_o):
        a, d = a_ref[0], d_ref[0]
        for b in range(1, nb):
            a = a + a_ref[b]
            d = d + d_ref[b]
        dxn = _dot_nt(a.astype(BF16), wc_ref[...]) + _dot_nt(d.astype(BF16), wkv_ref[...])
        v = m_ref[...]
        r = lax.rsqrt(jnp.mean(v * v, axis=-1, keepdims=True) + EPS)
        gm_o[...] = _rms_bwd(dxn, v, r, g_ref[...])
        dg_o[...] = jnp.sum(dxn * v * r, axis=0, keepdims=True)

    return pl.pallas_call(
        body, name="meta_bwd", grid=(1,),
        out_shape=(_sds((N_META, D), F32), _sds((1, D), F32)),
        in_specs=[pl.BlockSpec((nb, N_META, 2048), lambda i: (0, 0, 0)), pl.BlockSpec((nb, N_META, 512), lambda i: (0, 0, 0)),
                  pl.BlockSpec((D, 2048), lambda i: (0, 0)), pl.BlockSpec((D, 512), lambda i: (0, G_KV[0] // 512)),
                  pl.BlockSpec((N_META, D), lambda i: (0, 0)), pl.BlockSpec((1, D), lambda i: (0, 0))],
        out_specs=(pl.BlockSpec((N_META, D), lambda i: (0, 0)), pl.BlockSpec((1, D), lambda i: (0, 0))),
        compiler_params=_params(("arbitrary",), 32),
    )(*_pin(dam, ddm, w_full, w_full, meta_full, norm_g))


def _dxn(d_groups, w_full, x2, dy, norm_g, dg_init, tm):
    rows = x2.shape[0]
    groups = (G_CONV, G_CZ, G_Q, G_KV, G_E)

    def body(da, db, dq, dd, de, w_hbm, x_ref, dy_ref, g_ref, gi_ref, gx_o, dg_o, w_vmem, sem):
        @pl.when(pl.program_id(0) == 0)
        def _():
            cp = pltpu.make_async_copy(w_hbm, w_vmem, sem)
            cp.start()
            cp.wait()
            dg_o[...] = gi_ref[...]

        dxn = jnp.zeros((tm, D), F32)
        for ref, (off, wd) in zip((da, db, dq, dd, de), groups):
            for c0 in range(0, wd, 512):
                dxn = dxn + _dot_nt(ref[:, c0:c0 + 512], w_vmem[:, off + c0:off + c0 + 512])
        v = x_ref[...]
        r = lax.rsqrt(jnp.mean(v * v, axis=-1, keepdims=True) + EPS)
        gx_o[...] = dy_ref[...] + _rms_bwd(dxn, v, r, g_ref[...])
        dg_o[...] += jnp.sum(dxn * v * r, axis=0, keepdims=True)

    row = lambda wd: pl.BlockSpec((tm, wd), lambda i: (i, 0))
    vec = pl.BlockSpec((1, D), lambda i: (0, 0))
    return pl.pallas_call(
        body, name="dxn", grid=(rows // tm,),
        out_shape=(_sds((rows, D), F32), _sds((1, D), F32)),
        in_specs=[row(wd) for _, wd in groups] + [pl.BlockSpec(memory_space=pl.ANY), row(D), row(D), vec, vec],
        out_specs=(row(D), vec),
        scratch_shapes=[pltpu.VMEM((D, IN_DIM), BF16), pltpu.SemaphoreType.DMA],
        compiler_params=_params(("arbitrary",), 56),
    )(*_pin(*d_groups, w_full, x2, dy, norm_g, dg_init))


def _wgrad(at, b, buf, slot, col_off, name, meta=None):
    rows, n = b.shape
    tn, tk = 512, min(2048, rows)
    nk = rows // tk
    j0 = col_off // tn

    def body(*refs):
        if meta is None:
            at_ref, b_ref, _, o_ref = refs
        else:
            at_ref, b_ref, xm_ref, dm_ref, _, o_ref = refs
        k = pl.program_id(1)

        @pl.when(k == 0)
        def _():
            if meta is None:
                o_ref[0] = jnp.zeros((D, tn), F32)
            else:
                dm = dm_ref[0]
                for e in range(1, dm_ref.shape[0]):
                    dm = dm + dm_ref[e]
                dm = jnp.concatenate([dm, jnp.zeros((128 - N_META, tn), F32)], axis=0)
                o_ref[0] = _dot(xm_ref[...], dm.astype(BF16))

        o_ref[0] += _dot(at_ref[...], b_ref[...].astype(BF16))

    in_specs = [pl.BlockSpec((D, tk), lambda j, k: (0, k)), pl.BlockSpec((tk, tn), lambda j, k: (k, j))]
    args = [at, b]
    if meta is not None:
        xmt, dm = meta
        in_specs += [pl.BlockSpec((D, 128), lambda j, k: (0, 0)),
                     pl.BlockSpec((dm.shape[0], N_META, tn), lambda j, k: (0, 0, j))]
        args += [xmt, dm]
    in_specs.append(pl.BlockSpec(memory_space=pl.ANY))
    args.append(buf)
    return pl.pallas_call(
        body, name=name, grid=(n // tn, nk),
        out_shape=_sds(buf.shape, F32),
        in_specs=in_specs,
        out_specs=pl.BlockSpec((1, D, tn), lambda j, k: (slot, 0, j0 + j)),
        input_output_aliases={len(args) - 1: 0},
        compiler_params=_params(("parallel", "arbitrary"), 32),
    )(*_pin(*args))


def _rope_tables(s_len):
    pos = jnp.arange(s_len, dtype=jnp.int32)
    row_ids = (pos // GRID_W).astype(F32)
    col_ids = (pos % GRID_W).astype(F32)
    inv_freq = ROPE_THETA ** (-jnp.arange(ROPE_FREQS, dtype=F32) / ROPE_FREQS)
    a_row = row_ids[:, None] * inv_freq[None, :]
    a_col = col_ids[:, None] * inv_freq[None, :]
    ang = jnp.concatenate([a_row, a_row, a_col, a_col], axis=-1)
    return jnp.tile(jnp.cos(ang), (1, GQA)), jnp.tile(jnp.sin(ang), (1, GQA))


def _local_step(x, loss_target, norm_g, conv_b, cn_g, cn_b, q_g, k_g, w_full, w3_full, conv_w_full, meta_full,
                reduce_start=None):
    nb, s_len, _ = x.shape
    rows = nb * s_len
    x2 = x.reshape(rows, D)
    t2 = loss_target.reshape(rows, D)
    cos, sin = _rope_tables(s_len)
    qg = jnp.tile(q_g, (1, GQA))
    kg = jnp.tile(k_g, (1, N_KV))

    xnmt, pm_conv, pm_kv = _meta_fwd(meta_full, norm_g, w_full)
    pconv, pcz, pq, pkv, pe, xnt = _in_proj(x2, norm_g, w_full, 256)
    c0 = _conv_fwd(pconv, pm_conv, conv_w_full, conv_b, nb, 256)
    tq = min(512, s_len)
    kv4 = _kv_prep(pkv, pm_kv, kg, cos, sin, nb)
    q2, qst = _q_prep(pq, qg, cos, sin, nb, 256)
    o, lse = _attn_fwd(q2, kv4, nb, tq)
    if callable(w3_full):
        w3_full = w3_full(o)
    dy, mt, c3t, o2t, dyc, dya, d_o, dc0, dcz, de, sums = _mid(x2, t2, c0, pcz, o, pe, w3_full, cn_g, cn_b, 256)
    doe, dot_ = _do_prep(d_o, o, 256)
    dqr, dkt, dvt, dkt2, dvt2 = _attn_bwd(q2, qst, kv4, doe, dot_, lse, nb, tq)
    dq, dqg = _q_post(dqr, pq, qg, cos, sin, nb, 256)
    dd, ddm, dkg = _kv_bwd(dkt.reshape(nb, GROUP_W, s_len), dvt.reshape(nb, GROUP_W, s_len),
                           dkt2.reshape(nb, GROUP_W, KEY_PAD), dvt2.reshape(nb, GROUP_W, KEY_PAD),
                           pkv, pm_kv, kg, cos, sin, nb)
    da, dam, gcw = _conv_bwd(dc0, pconv, pm_conv, conv_w_full, nb, 256)
    gmeta, dng_m = _meta_bwd(dam, ddm, w_full, meta_full, norm_g)

    gw3 = lax.empty((3, D, D), F32)
    gw3 = _wgrad(c3t, dyc, gw3, 0, 0, "wgrad_conv_out")
    gw3 = _wgrad(o2t, dya, gw3, 1, 0, "wgrad_attn_out")
    gw3 = _wgrad(mt, dy, gw3, 2, 0, "wgrad_out")
    gwin = lax.empty((1, D, IN_DIM), F32)
    gwin = _wgrad(xnt, da, gwin, 0, G_CONV[0], "wgrad_in_conv", meta=(xnmt, dam))
    gwin = _wgrad(xnt, dcz, gwin, 0, G_CZ[0], "wgrad_in_cz")
    gwin = _wgrad(xnt, dq, gwin, 0, G_Q[0], "wgrad_in_q")
    gwin = _wgrad(xnt, dd, gwin, 0, G_KV[0], "wgrad_in_kv", meta=(xnmt, ddm))
    gwin = _wgrad(xnt, de, gwin, 0, G_E[0], "wgrad_in_e")

    pending = None
    if reduce_start is not None:
        token, pending = reduce_start(gwin, gw3, gcw, gmeta)
        dng_m = dng_m + token[0:1, 0:1]
    gx, dng = _dxn((da, dcz, dq, dd, de), w_full, x2, dy, norm_g, dng_m, 512)

    zeros = jnp.zeros((1, D - 2 * GROUP_W), F32)
    smalls = jnp.concatenate([dng, sums[2:3], sums[0:1], sums[1:2], jnp.concatenate([dqg, dkg, zeros], axis=1),
                              sums[3:4], jnp.zeros((2, D), F32)], axis=0)
    return gx.reshape(nb, s_len, D), gwin, gw3, gcw, gmeta, smalls, pending


def _xyc():
    return lax.axis_index("x"), lax.axis_index("y"), lax.axis_index("c")


def _reduce_sibling(gwin, gw3v, gcm):
    def body(gwin_ref, gw3_ref, gcm_ref, r_win, r_w3, r_cm, send, recv):
        x, y, c = _xyc()
        o = 1 - c
        half = D // 2
        outs = ((gwin_ref.at[pl.ds(o * half, half), :], r_win), (gw3_ref.at[:, :, o], r_w3), (gcm_ref.at[o], r_cm))
        cps = []
        for a, (src, dst) in enumerate(outs):
            cp = pltpu.make_async_remote_copy(src_ref=src, dst_ref=dst, send_sem=send.at[a], recv_sem=recv.at[a],
                                              device_id=(x, y, o), device_id_type=MESH)
            cp.start()
            cps.append(cp)
        for cp in cps:
            cp.wait()

    any_spec = pl.BlockSpec(memory_space=pl.ANY)
    return pl.pallas_call(
        body, name="reduce_sibling",
        out_shape=(_sds((D // 2, IN_DIM), F32), _sds((3, 4, 128, D), F32),
                   _sds((24, D), F32)),
        in_specs=[any_spec] * 3, out_specs=(any_spec,) * 3,
        scratch_shapes=[pltpu.SemaphoreType.DMA((3,)), pltpu.SemaphoreType.DMA((3,))],
    )(*_pin(gwin, gw3v, gcm))


def _add_sibling(gwin, gw3v, gcm, r_win, r_w3, r_cm):
    c = lax.axis_index("c").astype(jnp.int32).reshape(1)
    half = D // 2
    tr = 64

    def body1(c_ref, a_ref, b_ref, o_ref):
        o_ref[...] = (a_ref[...] + b_ref[...]).astype(BF16)

    cs_win = pl.pallas_call(
        body1, name="add_sibling_w_in", out_shape=_sds((half, IN_DIM), BF16),
        grid_spec=pltpu.PrefetchScalarGridSpec(
            num_scalar_prefetch=1, grid=(half // tr,),
            in_specs=[pl.BlockSpec((tr, IN_DIM), lambda i, c_ref: (c_ref[0] * (half // tr) + i, 0)),
                      pl.BlockSpec((tr, IN_DIM), lambda i, c_ref: (i, 0))],
            out_specs=pl.BlockSpec((tr, IN_DIM), lambda i, c_ref: (i, 0))),
        compiler_params=_params(("parallel",), 32),
    )(c, *_pin(gwin, r_win))

    def body2(c_ref, a_ref, b_ref, o_ref):
        o_ref[0, 0] = (a_ref[0, 0, 0] + b_ref[0, 0]).astype(BF16)

    cs_w3 = pl.pallas_call(
        body2, name="add_sibling_w3", out_shape=_sds((3, 4, 128, D), BF16),
        grid_spec=pltpu.PrefetchScalarGridSpec(
            num_scalar_prefetch=1, grid=(3, 4),
            in_specs=[pl.BlockSpec((1, 1, 1, 128, D), lambda w, s, c_ref: (w, s, c_ref[0], 0, 0)),
                      pl.BlockSpec((1, 1, 128, D), lambda w, s, c_ref: (w, s, 0, 0))],
            out_specs=pl.BlockSpec((1, 1, 128, D), lambda w, s, c_ref: (w, s, 0, 0))),
        compiler_params=_params(("parallel", "parallel"), 32),
    )(c, *_pin(gw3v, r_w3))

    def body3(c_ref, a_ref, b_ref, o_ref):
        o_ref[...] = a_ref[0] + b_ref[...]

    cs_cm = pl.pallas_call(
        body3, name="add_sibling_cm", out_shape=_sds((24, D), F32),
        grid_spec=pltpu.PrefetchScalarGridSpec(
            num_scalar_prefetch=1, grid=(1,),
            in_specs=[pl.BlockSpec((1, 24, D), lambda i, c_ref: (c_ref[0], 0, 0)),
                      pl.BlockSpec((24, D), lambda i, c_ref: (0, 0))],
            out_specs=pl.BlockSpec((24, D), lambda i, c_ref: (0, 0))),
        compiler_params=_params(("arbitrary",), 32),
    )(c, *_pin(gcm, r_cm))
    return cs_win, cs_w3, cs_cm


def _reduce_chips_copies(srcs, lands, send, recv):
    win_ref, w3_ref, cm_ref = srcs
    r_win, r_w3, r_cm = lands
    x, y, c = _xyc()
    peers = ((1 - x, y), (x, 1 - y), (1 - x, 1 - y))
    cps = []
    for k, (px, py) in enumerate(peers):
        ps = 2 * px + py
        items = ((win_ref.at[:, pl.ds(ps * W_IN_SHARD, W_IN_SHARD)], r_win.at[k]),
                 (w3_ref.at[:, ps], r_w3.at[k]),
                 (cm_ref.at[:, pl.ds(ps * ROW_SHARD, ROW_SHARD)], r_cm.at[k]))
        for a, (src, dst) in enumerate(items):
            cps.append(pltpu.make_async_remote_copy(src_ref=src, dst_ref=dst, send_sem=send.at[3 * a + k],
                                                    recv_sem=recv.at[3 * a + k], device_id=(px, py, c),
                                                    device_id_type=MESH))
    return cps


_HBM = pl.BlockSpec(memory_space=pltpu.HBM)
_SEM = pl.BlockSpec(memory_space=pltpu.SEMAPHORE)
_EFFECT = pltpu.SideEffectType.DATAFLOW_SIDE_EFFECTING


def _reduce_chips_start(cs_win, cs_w3, cs_cm):
    srcs = (cs_win, cs_w3, cs_cm)
    lands = (lax.empty((3, D // 2, W_IN_SHARD), BF16), lax.empty((3, 3, 128, D), BF16),
             lax.empty((3, 24, ROW_SHARD), F32))

    def body(*refs):
        srcs_in, lands_in, send, recv, token = refs[0:3], refs[3:6], refs[6], refs[7], refs[14]
        for cp in _reduce_chips_copies(srcs_in, lands_in, send, recv):
            cp.start()
        token[...] = jnp.zeros_like(token)

    hbm = lambda a: pltpu.HBM(a.shape, a.dtype)
    outs = pl.pallas_call(
        body, name="reduce_chips_start",
        out_shape=(pltpu.SemaphoreType.DMA((9,)), pltpu.SemaphoreType.DMA((9,)),
                   *[hbm(a) for a in srcs], *[hbm(a) for a in lands], jax.ShapeDtypeStruct((8, 128), F32)),
        in_specs=[_HBM] * 6,
        out_specs=(_SEM, _SEM, *[_HBM] * 6, pl.BlockSpec(memory_space=pltpu.VMEM)),
        input_output_aliases={i: i + 2 for i in range(6)},
        compiler_params=pltpu.CompilerParams(has_side_effects=_EFFECT),
    )(*[pltpu.with_memory_space_constraint(a, pltpu.HBM) for a in srcs + lands])
    return outs[0], outs[1], outs[2:5], outs[5:8], outs[8]


def _reduce_chips_wait(send, recv, srcs, lands, after):
    def body(*refs):
        srcs_in, lands_in, send_ref, recv_ref = refs[0:3], refs[3:6], refs[6], refs[7]
        for cp in _reduce_chips_copies(srcs_in, lands_in, send_ref, recv_ref):
            cp.wait_send()
            cp.wait_recv()

    hbm = lambda a: pltpu.HBM(a.shape, a.dtype)
    outs = pl.pallas_call(
        body, name="reduce_chips_wait",
        out_shape=(*[hbm(a) for a in srcs], *[hbm(a) for a in lands]),
        in_specs=[_HBM] * 6 + [_SEM, _SEM, pl.BlockSpec(memory_space=pl.ANY)],
        out_specs=(_HBM,) * 6,
        input_output_aliases={i: i for i in range(6)},
        compiler_params=pltpu.CompilerParams(has_side_effects=_EFFECT),
    )(*srcs, *lands, send, recv, after)
    return outs[0:3], outs[3:6]


def _add_chips(cs_win, cs_w3, cs_cm, r_win, r_w3, r_cm):
    x, y, c = _xyc()
    idx = jnp.stack([2 * x + y, c]).astype(jnp.int32)
    half = D // 2
    tr = 128

    def body1(i_ref, a_ref, b_ref, o_ref):
        f = lambda v: v.astype(F32)
        o_ref[0] = (f(a_ref[...]) + f(b_ref[2])) + (f(b_ref[0]) + f(b_ref[1]))

    f_win = pl.pallas_call(
        body1, name="add_chips_w_in", out_shape=_sds((2, half, W_IN_SHARD), F32),
        grid_spec=pltpu.PrefetchScalarGridSpec(
            num_scalar_prefetch=1, grid=(half // tr,),
            in_specs=[pl.BlockSpec((tr, W_IN_SHARD), lambda i, r: (i, r[0])),
                      pl.BlockSpec((3, tr, W_IN_SHARD), lambda i, r: (0, i, 0))],
            out_specs=pl.BlockSpec((1, tr, W_IN_SHARD), lambda i, r: (r[1], i, 0))),
        compiler_params=_params(("parallel",), 32),
    )(idx, *_pin(cs_win, r_win))

    def body2(i_ref, a_ref, b_ref, o_ref):
        f = lambda v: v.astype(F32)
        o_ref[0, 0] = (f(a_ref[0, 0]) + f(b_ref[2, 0])) + (f(b_ref[0, 0]) + f(b_ref[1, 0]))

    f_w3 = pl.pallas_call(
        body2, name="add_chips_w3", out_shape=_sds((3, 2, 128, D), F32),
        grid_spec=pltpu.PrefetchScalarGridSpec(
            num_scalar_prefetch=1, grid=(3,),
            in_specs=[pl.BlockSpec((1, 1, 128, D), lambda w, r: (w, r[0], 0, 0)),
                      pl.BlockSpec((3, 1, 128, D), lambda w, r: (0, w, 0, 0))],
            out_specs=pl.BlockSpec((1, 1, 128, D), lambda w, r: (w, r[1], 0, 0))),
        compiler_params=_params(("parallel",), 32),
    )(idx, *_pin(cs_w3, r_w3))

    def body3(i_ref, a_ref, b_ref, o_ref):
        o_ref[0] = (a_ref[...] + b_ref[2]) + (b_ref[0] + b_ref[1])

    f_cm = pl.pallas_call(
        body3, name="add_chips_cm", out_shape=_sds((2, 24, ROW_SHARD), F32),
        grid_spec=pltpu.PrefetchScalarGridSpec(
            num_scalar_prefetch=1, grid=(1,),
            in_specs=[pl.BlockSpec((24, ROW_SHARD), lambda i, r: (0, r[0])),
                      pl.BlockSpec((3, 24, ROW_SHARD), lambda i, r: (0, 0, 0))],
            out_specs=pl.BlockSpec((1, 24, ROW_SHARD), lambda i, r: (r[1], 0, 0))),
        compiler_params=_params(("arbitrary",), 32),
    )(idx, *_pin(cs_cm, r_cm))
    return f_win, f_w3, f_cm


def _share_sibling(f_win, f_w3, f_cm, smalls):
    def body(win_in, w3_in, cm_in, sm_ref, win_ref, w3_ref, cm_ref, r_sm, send, recv, ssend, srecv, lsem):
        x, y, c = _xyc()
        o = 1 - c
        cps = []
        for a, (ref, sl) in enumerate(((win_ref, lambda h: win_ref.at[h]), (w3_ref, lambda h: w3_ref.at[:, h]),
                                       (cm_ref, lambda h: cm_ref.at[h]))):
            cp = pltpu.make_async_remote_copy(src_ref=sl(c), dst_ref=sl(c), send_sem=send.at[a], recv_sem=recv.at[a],
                                              device_id=(x, y, o), device_id_type=MESH)
            cp.start()
            cps.append((cp, sl))
        me = 4 * x + 2 * y + c
        loc = pltpu.make_async_copy(sm_ref, r_sm.at[me], lsem)
        loc.start()
        scps = []
        for d in range(1, 8):
            px, py, pc = (x + (d >> 2)) % 2, (y + ((d >> 1) & 1)) % 2, (c + (d & 1)) % 2
            cp = pltpu.make_async_remote_copy(src_ref=sm_ref, dst_ref=r_sm.at[me], send_sem=ssend.at[d - 1],
                                              recv_sem=srecv.at[d - 1], device_id=(px, py, pc), device_id_type=MESH)
            cp.start()
            scps.append((cp, 4 * px + 2 * py + pc))
        for a, (cp, sl) in enumerate(cps):
            pltpu.make_async_remote_copy(src_ref=sl(o), dst_ref=sl(o), send_sem=send.at[a], recv_sem=recv.at[a],
                                         device_id=(x, y, o), device_id_type=MESH).wait_recv()
            cp.wait_send()
        for d, (cp, pid) in enumerate(scps):
            pltpu.make_async_remote_copy(src_ref=sm_ref, dst_ref=r_sm.at[pid], send_sem=ssend.at[d],
                                         recv_sem=srecv.at[d], device_id=(x, y, c), device_id_type=MESH).wait_recv()
            cp.wait_send()
        loc.wait()

    any_spec = pl.BlockSpec(memory_space=pl.ANY)
    return pl.pallas_call(
        body, name="share_sibling",
        out_shape=(_sds(f_win.shape, F32), _sds(f_w3.shape, F32), _sds(f_cm.shape, F32), _sds((8, 8, D), F32)),
        in_specs=[any_spec] * 4, out_specs=(any_spec,) * 4,
        input_output_aliases={0: 0, 1: 1, 2: 2},
        scratch_shapes=[pltpu.SemaphoreType.DMA((3,)), pltpu.SemaphoreType.DMA((3,)),
                        pltpu.SemaphoreType.DMA((7,)), pltpu.SemaphoreType.DMA((7,)), pltpu.SemaphoreType.DMA],
    )(*_pin(f_win, f_w3, f_cm, smalls))


def _adamw_math(w, g, m, v):
    m = ADAM_B1 * m + (1.0 - ADAM_B1) * g
    v = ADAM_B2 * v + (1.0 - ADAM_B2) * (g * g)
    m_hat = m / (1.0 - ADAM_B1 ** ADAM_STEP)
    v_hat = v / (1.0 - ADAM_B2 ** ADAM_STEP)
    delta = -ADAM_LR * (m_hat / (jnp.sqrt(v_hat) + ADAM_EPS) + ADAM_WD * w)
    return delta, m, v


def _adamw(w, g, m, v, tr, name):
    rows, cols = w.shape

    def body(w_ref, g_ref, m_ref, v_ref, d_o, m_o, v_o):
        d_o[...], m_o[...], v_o[...] = _adamw_math(w_ref[...], g_ref[...], m_ref[...], v_ref[...])

    spec = pl.BlockSpec((tr, cols), lambda i: (i, 0))
    return pl.pallas_call(
        body, name=name, grid=(rows // tr,),
        out_shape=(_sds((rows, cols), F32),) * 3,
        in_specs=[spec] * 4, out_specs=(spec,) * 3,
        compiler_params=_params(("parallel",), 32),
    )(*_pin(w, g, m, v))


def _adamw3(g3, ws, ms, vs):
    def body(g_ref, *refs):
        w_refs, m_refs, v_refs, outs = refs[0:3], refs[3:6], refs[6:9], refs[9:]
        g_os, d_os, m_os, v_os = outs[0:3], outs[3:6], outs[6:9], outs[9:12]
        for i in range(3):
            g = g_ref[i]
            g_os[i][0] = g
            d_os[i][0], m_os[i][0], v_os[i][0] = _adamw_math(w_refs[i][0], g, m_refs[i][0], v_refs[i][0])

    return pl.pallas_call(
        body, name="adamw_w3", out_shape=(jax.ShapeDtypeStruct((1, ROW_SHARD, D), F32),) * 12,
        compiler_params=pltpu.CompilerParams(vmem_limit_bytes=48 << 20),
    )(g3, *ws, *ms, *vs)


def _adamw_cm(f_cm, ws, ms, vs):
    def body(f_ref, *refs):
        w_refs, m_refs, v_refs, outs = refs[0:2], refs[2:4], refs[4:6], refs[6:14]
        gcw, gmt = refs[14], refs[15]
        gcw[0:16] = f_ref[0, 0:16]
        gcw[16:32] = f_ref[1, 0:16]
        gmt[0:8] = f_ref[0, 16:24]
        gmt[8:16] = f_ref[1, 16:24]
        g_conv = gcw[0:CONV_K, :]
        g_meta = gmt[...]
        outs[0][0] = g_conv
        outs[1][...] = g_meta
        outs[2][0], outs[4][0], outs[6][0] = _adamw_math(w_refs[0][0], g_conv, m_refs[0][0], v_refs[0][0])
        outs[3][...], outs[5][...], outs[7][...] = _adamw_math(w_refs[1][...], g_meta, m_refs[1][...], v_refs[1][...])

    pair = (jax.ShapeDtypeStruct((1, CONV_K, ROW_SHARD), F32), jax.ShapeDtypeStruct((N_META, ROW_SHARD), F32))
    return pl.pallas_call(
        body, name="adamw_cm", out_shape=pair * 4,
        scratch_shapes=[pltpu.VMEM((32, ROW_SHARD), F32), pltpu.VMEM((N_META, ROW_SHARD), F32)],
    )(f_cm, *ws, *ms, *vs)


def _adamw_small(r_sm, ws, ms, vs):
    def body(s_ref, *refs):
        w_refs, m_refs, v_refs, outs = refs[0:6], refs[6:12], refs[12:18], refs[18:]
        loss_o, g_os, d_os, m_os, v_os = outs[0], outs[1:7], outs[7:13], outs[13:19], outs[19:25]
        g = s_ref[0]
        for dev in range(1, 8):
            g = g + s_ref[dev]
        qk = g[4:5, :]
        qg = qk[:, 0:HEAD_DIM]
        kg = qk[:, GROUP_W:GROUP_W + HEAD_DIM]
        for h in range(1, GQA):
            qg = qg + qk[:, HEAD_DIM * h:HEAD_DIM * (h + 1)]
            kg = kg + qk[:, GROUP_W + HEAD_DIM * h:GROUP_W + HEAD_DIM * (h + 1)]
        loss_o[...] = (0.5 / D) * jnp.sum(g[5:6, :], axis=-1, keepdims=True)
        for i, gi in enumerate((g[0:1], g[1:2], g[2:3], g[3:4], qg, kg)):
            g_os[i][...] = gi
            d_os[i][...], m_os[i][...], v_os[i][...] = _adamw_math(w_refs[i][...], gi, m_refs[i][...], v_refs[i][...])

    six = tuple(jax.ShapeDtypeStruct(w.shape, F32) for w in ws)
    return pl.pallas_call(
        body, name="adamw_small", out_shape=(jax.ShapeDtypeStruct((1, 1), F32),) + six * 4,
    )(r_sm, *ws, *ms, *vs)


def kernel(x, meta_tokens, norm_g, w_in, conv_w, conv_b, conv_norm_g, conv_norm_b, w_conv_out, q_norm_g, k_norm_g, w_attn_out, w_out, loss_target, m_meta_tokens, m_norm_g, m_w_in, m_conv_w, m_conv_b, m_conv_norm_g, m_conv_norm_b, m_w_conv_out, m_q_norm_g, m_k_norm_g, m_w_attn_out, m_w_out, v_meta_tokens, v_norm_g, v_w_in, v_conv_w, v_conv_b, v_conv_norm_g, v_conv_norm_b, v_w_conv_out, v_q_norm_g, v_k_norm_g, v_w_attn_out, v_w_out):
    pad_k = lambda a: jnp.pad(a[0], ((0, 32 - CONV_K), (0, 0)))
    w3_s = (w_conv_out, w_attn_out, w_out)
    w_full, conv_w_full, meta_full, w3b = _gather_weights(w_in[0], w3_s, pad_k(conv_w), meta_tokens)
    w3_pending = _w3_start(w3b, meta_full)

    def w3_full(after):
        return _w3_finish(*_w3_wait(*w3_pending, after))

    def reduce_start(gwin, gw3, gcw, gmeta):
        gwin2 = gwin.reshape(D, IN_DIM)
        gw3v = gw3.reshape(3, N_CHIPS, 2, 128, D)
        gcm = jnp.concatenate([gcw.reshape(2, 16, D), gmeta.reshape(2, 8, D)], axis=1)
        r_win, r_w3, r_cm = _reduce_sibling(gwin2, gw3v, gcm)
        cs = _add_sibling(gwin2, gw3v, gcm, r_win, r_w3, r_cm)
        send, recv, srcs, lands, token = _reduce_chips_start(*cs)
        return token, (send, recv, srcs, lands)

    gx, _, _, _, _, smalls, pending = _local_step(
        x, loss_target, norm_g, conv_b, conv_norm_g, conv_norm_b, q_norm_g, k_norm_g,
        w_full, w3_full, conv_w_full, meta_full, reduce_start)
    (cs_win, cs_w3, cs_cm), (r2_win, r2_w3, r2_cm) = _reduce_chips_wait(*pending, gx)
    f_win, f_w3, f_cm = _add_chips(cs_win, cs_w3, cs_cm, r2_win, r2_w3, r2_cm)
    f_win, f_w3, f_cm, r_sm = _share_sibling(f_win, f_w3, f_cm, smalls)

    g_w_in = f_win.reshape(D, W_IN_SHARD)
    d_w_in, nm_w_in, nv_w_in = _adamw(w_in[0], g_w_in, m_w_in[0], v_w_in[0], 128, "adamw_w_in")
    w3 = _adamw3(f_w3.reshape(3, ROW_SHARD, D), w3_s, (m_w_conv_out, m_w_attn_out, m_w_out),
                 (v_w_conv_out, v_w_attn_out, v_w_out))
    cm = _adamw_cm(f_cm, (conv_w, meta_tokens), (m_conv_w, m_meta_tokens), (v_conv_w, v_meta_tokens))
    small = _adamw_small(
        r_sm, (norm_g, conv_b, conv_norm_g, conv_norm_b, q_norm_g, k_norm_g),
        (m_norm_g, m_conv_b, m_conv_norm_g, m_conv_norm_b, m_q_norm_g, m_k_norm_g),
        (v_norm_g, v_conv_b, v_conv_norm_g, v_conv_norm_b, v_q_norm_g, v_k_norm_g))

    def assemble(big_in, w3x, cmx, s6):
        ng, cb, cng, cnb, qg, kg = s6
        return (cmx[1], ng, big_in[None], cmx[0], cb, cng, cnb, w3x[0], qg, kg, w3x[1], w3x[2])

    loss = small[0].reshape(())
    grads = assemble(g_w_in, w3[0:3], cm[0:2], small[1:7])
    deltas = assemble(d_w_in, w3[3:6], cm[2:4], small[7:13])
    new_m = assemble(nm_w_in, w3[6:9], cm[4:6], small[13:19])
    new_v = assemble(nv_w_in, w3[9:12], cm[6:8], small[19:25])
    return (loss, gx, *grads, *deltas, *new_m, *new_v)
```

```python
import functools
import math

import jax
import jax.numpy as jnp
from jax import lax
from jax.experimental import pallas as pl
from jax.experimental.pallas import tpu as pltpu

F32, BF16 = jnp.float32, jnp.bfloat16
MESH = pl.DeviceIdType.MESH

D = 1024
N_META = 16
CONV_K = 31
N_KV = 4
GQA = 4
HEAD_DIM = 64
GROUP_W = GQA * HEAD_DIM
GRID_W = 64
ROPE_FREQS = 16
ROPE_THETA = 10000.0
EPS = 1e-6
IN_DIM = 7680
KEY_PAD = 128
G_CONV, G_CZ, G_Q, G_KV, G_E = (0, 2048), (2048, 1024), (3072, 1024), (4096, 512), (4608, 3072)
N_CHIPS = 4
W_IN_SHARD = IN_DIM // N_CHIPS
ROW_SHARD = D // N_CHIPS

ADAM_LR, ADAM_B1, ADAM_B2, ADAM_EPS, ADAM_WD, ADAM_STEP = 0.001, 0.9, 0.999, 1e-08, 0.01, 10

NT_DIMS = (((1,), (1,)), ((), ()))


def _params(sem=None, vmem_mb=48):
    return pltpu.CompilerParams(dimension_semantics=sem, vmem_limit_bytes=vmem_mb << 20)


def _sds(shape, dtype):
    return pltpu.HBM(tuple(shape), dtype)


def _pin(*arrays):
    return [pltpu.with_memory_space_constraint(a, pltpu.HBM) for a in arrays]


def _sig(v):
    return jax.nn.sigmoid(v)


def _dsilu(v, s):
    return s * (1.0 + v * (1.0 - s))


def _dot(a, b):
    return jnp.dot(a, b, preferred_element_type=F32)


def _dot_nt(a, b):
    return lax.dot_general(a, b, NT_DIMS, preferred_element_type=F32)


def _qk_mats():
    i = lax.broadcasted_iota(jnp.int32, (GROUP_W, GROUP_W), 0)
    j = lax.broadcasted_iota(jnp.int32, (GROUP_W, GROUP_W), 1)
    mean = jnp.where((i >> 6) == (j >> 6), 1.0 / HEAD_DIM, 0.0).astype(BF16)
    turn = jnp.where((i == j + 16) & ((j & 16) == 0), -1.0,
                     jnp.where((i == j - 16) & ((j & 16) != 0), 1.0, 0.0)).astype(BF16)
    return mean, turn


def _apply(v, mat):
    hi = v.astype(BF16)
    lo = (v - hi.astype(F32)).astype(BF16)
    return _dot(hi, mat) + _dot(lo, mat)


def _qk_fwd(v, g, cos, sin, mats):
    mean, turn = mats
    r = lax.rsqrt(_apply(v * v, mean) + EPS)
    n = v * r * g
    return n * cos + _apply(n, turn) * sin, r


def _qk_bwd(dy, v, r, g, cos, sin, mats):
    mean, turn = mats
    dn = dy * cos - _apply(dy, turn) * sin
    dyg = dn * g
    dv = r * dyg - v * (r * r * r) * _apply(dyg * v, mean)
    return dv, dn * v * r


def _rms_bwd(dxn, v, r, g):
    dxg = dxn * g
    return r * dxg - v * (r * r * r) * jnp.mean(dxg * v, axis=-1, keepdims=True)


def _glu(a):
    return a[:, :D] * _sig(a[:, D:])


def _gather_weights(w_in_s, w3_s, conv_w_s, meta_s):
    def body(win_ref, wa_ref, wb_ref, wc_ref, cw_ref, mt_ref, win_o, cw_o, mt_o, w3b_o, w3_o, win_b, w3_b,
             send, recv, fsend, frecv, lsem, csem):
        x, y, c = _xyc()
        o = 1 - c
        me = 2 * x + y
        win_b[...] = win_ref[...].astype(BF16)
        for i, ref in enumerate((wa_ref, wb_ref, wc_ref)):
            w3_b[i] = ref[0].astype(BF16)
        cast = pltpu.make_async_copy(w3_b, w3b_o, csem.at[0])
        cast.start()
        own = pltpu.make_async_copy(w3_b, _w3_place(w3_o, me), csem.at[1])
        own.start()
        items = (
            (lambda h: win_b.at[pl.ds(h * 512, 512), :],
             lambda p, h: win_o.at[pl.ds(h * 512, 512), pl.ds(p * W_IN_SHARD, W_IN_SHARD)]),
            (lambda h: cw_ref.at[pl.ds(h * 16, 16), :],
             lambda p, h: cw_o.at[pl.ds(h * 16, 16), pl.ds(p * ROW_SHARD, ROW_SHARD)]),
            (lambda h: mt_ref.at[pl.ds(h * 8, 8), :],
             lambda p, h: mt_o.at[pl.ds(h * 8, 8), pl.ds(p * ROW_SHARD, ROW_SHARD)]),
        )
        peers = ((1 - x, y), (x, 1 - y), (1 - x, 1 - y))

        def remote(src, dst, s_sem, r_sem, to):
            return pltpu.make_async_remote_copy(src_ref=src, dst_ref=dst, send_sem=s_sem, recv_sem=r_sem,
                                                device_id=to, device_id_type=MESH)

        started = []
        for a, (half, place) in enumerate(items):
            for h in range(2):
                loc = pltpu.make_async_copy(half(h), place(me, h), lsem.at[a, h])
                loc.start()
                started.append(loc.wait)
            for k, (px, py) in enumerate(peers):
                cp = remote(half(c), place(me, c), send.at[a, k], recv.at[a, k], (px, py, c))
                cp.start()
                started.append(cp.wait_send)
        for k, (px, py) in enumerate(peers):
            for a, (half, place) in enumerate(items):
                got = place(2 * px + py, c)
                remote(got, got, send.at[a, k], recv.at[a, k], (px, py, c)).wait_recv()
                fw = remote(got, got, fsend.at[a, k], frecv.at[a, k], (x, y, o))
                fw.start()
                started.append(fw.wait_send)
        for k, (px, py) in enumerate(peers):
            for a, (half, place) in enumerate(items):
                theirs = place(2 * px + py, o)
                remote(theirs, theirs, fsend.at[a, k], frecv.at[a, k], (x, y, o)).wait_recv()
        for wait in started:
            wait()
        cast.wait()
        own.wait()

    any_spec = pl.BlockSpec(memory_space=pl.ANY)
    vmem = pl.BlockSpec(memory_space=pltpu.VMEM)
    return pl.pallas_call(
        body, name="gather_weights",
        out_shape=(_sds((D, IN_DIM), BF16), _sds((32, D), F32), _sds((N_META, D), F32),
                   _sds((3, ROW_SHARD, D), BF16), _sds((3, D, D), BF16)),
        in_specs=[vmem] * 6,
        out_specs=(any_spec,) * 5,
        scratch_shapes=[pltpu.VMEM((D, W_IN_SHARD), BF16), pltpu.VMEM((3, ROW_SHARD, D), BF16),
                        pltpu.SemaphoreType.DMA((3, 3)), pltpu.SemaphoreType.DMA((3, 3)),
                        pltpu.SemaphoreType.DMA((3, 3)), pltpu.SemaphoreType.DMA((3, 3)),
                        pltpu.SemaphoreType.DMA((3, 2)), pltpu.SemaphoreType.DMA((2,))],
        compiler_params=pltpu.CompilerParams(vmem_limit_bytes=40 << 20),
    )(w_in_s, *w3_s, conv_w_s, meta_s)


def _w3_place(ref, p):
    return ref.at[:, pl.ds(p * ROW_SHARD, ROW_SHARD), :]


def _w3_copies(w3b_ref, land_ref, send, recv):
    x, y, c = _xyc()
    me = 2 * x + y
    peers = ((1 - x, y), (x, 1 - y), (1 - x, 1 - y))
    return [pltpu.make_async_remote_copy(src_ref=w3b_ref, dst_ref=_w3_place(land_ref, me),
                                         send_sem=send.at[k], recv_sem=recv.at[k], device_id=(px, py, c),
                                         device_id_type=MESH)
            for k, (px, py) in enumerate(peers)]


def _w3_start(w3b, land):
    def body(w3b_ref, land_ref, send, recv, w3b_thru, land_thru, token):
        for cp in _w3_copies(w3b_ref, land_ref, send, recv):
            cp.start()
        token[...] = jnp.zeros_like(token)

    outs = pl.pallas_call(
        body, name="w3_start",
        out_shape=(pltpu.SemaphoreType.DMA((3,)), pltpu.SemaphoreType.DMA((3,)),
                   pltpu.HBM(w3b.shape, BF16), pltpu.HBM(land.shape, BF16), jax.ShapeDtypeStruct((8, 128), F32)),
        in_specs=[_HBM, _HBM],
        out_specs=(_SEM, _SEM, _HBM, _HBM, pl.BlockSpec(memory_space=pltpu.VMEM)),
        input_output_aliases={0: 2, 1: 3},
        compiler_params=pltpu.CompilerParams(has_side_effects=_EFFECT),
    )(*_pin(w3b, land))
    return outs[0:4], outs[4]


def _w3_wait(send, recv, w3b, land, after):
    def body(w3b_ref, land_ref, send_ref, recv_ref, after_ref, w3b_out, land_out):
        x, y, c = _xyc()
        peers = ((1 - x, y), (x, 1 - y), (1 - x, 1 - y))
        for k, (cp, (px, py)) in enumerate(zip(_w3_copies(w3b_ref, land_ref, send_ref, recv_ref), peers)):
            cp.wait_send()
            got = _w3_place(land_ref, 2 * px + py)
            pltpu.make_async_remote_copy(src_ref=got, dst_ref=got, send_sem=send_ref.at[k], recv_sem=recv_ref.at[k],
                                         device_id=(px, py, c), device_id_type=MESH).wait_recv()

    outs = pl.pallas_call(
        body, name="w3_wait",
        out_shape=(pltpu.HBM(w3b.shape, BF16), pltpu.HBM(land.shape, BF16)),
        in_specs=[_HBM, _HBM, _SEM, _SEM, pl.BlockSpec(memory_space=pl.ANY)],
        out_specs=(_HBM, _HBM),
        input_output_aliases={0: 0, 1: 1},
        compiler_params=pltpu.CompilerParams(has_side_effects=_EFFECT),
    )(w3b, land, send, recv, after)
    return outs[1]


def _meta_fwd(meta_full, norm_g, w_full):
    def body(m_ref, g_ref, wc_ref, wkv_ref, xnt_ref, pc_ref, pkv_ref):
        v = m_ref[...]
        r = lax.rsqrt(jnp.mean(v * v, axis=-1, keepdims=True) + EPS)
        xn = v * r * g_ref[...]
        xnb = xn.astype(BF16)
        pad = jnp.concatenate([xn, jnp.zeros((128 - N_META, D), F32)], axis=0)
        xnt_ref[...] = pad.T.astype(BF16)
        pc_ref[...] = _dot(xnb, wc_ref[...])
        pkv_ref[...] = _dot(xnb, wkv_ref[...])

    return pl.pallas_call(
        body, name="meta_fwd", grid=(1,),
        out_shape=(_sds((D, 128), BF16), _sds((N_META, 2048), F32),
                   _sds((N_META, 512), F32)),
        in_specs=[pl.BlockSpec((N_META, D), lambda i: (0, 0)), pl.BlockSpec((1, D), lambda i: (0, 0)),
                  pl.BlockSpec((D, 2048), lambda i: (0, 0)), pl.BlockSpec((D, 512), lambda i: (0, G_KV[0] // 512))],
        out_specs=(pl.BlockSpec((D, 128), lambda i: (0, 0)), pl.BlockSpec((N_META, 2048), lambda i: (0, 0)),
                   pl.BlockSpec((N_META, 512), lambda i: (0, 0))),
        compiler_params=_params(("arbitrary",), 32),
    )(*_pin(meta_full, norm_g, w_full, w_full))


def _in_proj(x2, norm_g, w_full, tm):
    rows = x2.shape[0]
    groups = (G_CONV, G_CZ, G_Q, G_KV, G_E)

    def body(x_ref, g_ref, w_hbm, *rest):
        outs, xnt_ref, w_vmem, sem = rest[:5], rest[5], rest[6], rest[7]

        @pl.when(pl.program_id(0) == 0)
        def _():
            cp = pltpu.make_async_copy(w_hbm, w_vmem, sem)
            cp.start()
            cp.wait()

        v = x_ref[...]
        r = lax.rsqrt(jnp.mean(v * v, axis=-1, keepdims=True) + EPS)
        xn = v * r * g_ref[...]
        xnb = xn.astype(BF16)
        xnt_ref[...] = xn.T.astype(BF16)
        for ref, (off, wd) in zip(outs, groups):
            for c0 in range(0, wd, 512):
                ref[:, c0:c0 + 512] = _dot(xnb, w_vmem[:, off + c0:off + c0 + 512])

    return pl.pallas_call(
        body, name="in_proj", grid=(rows // tm,),
        out_shape=tuple(_sds((rows, wd), F32) for _, wd in groups)
        + (_sds((D, rows), BF16),),
        in_specs=[pl.BlockSpec((tm, D), lambda i: (i, 0)), pl.BlockSpec((1, D), lambda i: (0, 0)),
                  pl.BlockSpec(memory_space=pl.ANY)],
        out_specs=tuple(pl.BlockSpec((tm, wd), lambda i: (i, 0)) for _, wd in groups)
        + (pl.BlockSpec((D, tm), lambda i: (0, i)),),
        scratch_shapes=[pltpu.VMEM((D, IN_DIM), BF16), pltpu.SemaphoreType.DMA],
        compiler_params=_params(("arbitrary",), 56),
    )(*_pin(x2, norm_g, w_full))


def _halo_specs(width, tm, nt, rows):
    h16 = tm // 16
    return [pl.BlockSpec((tm, width), lambda b, i: (b * nt + i, 0)),
            pl.BlockSpec((16, width), lambda b, i: (jnp.maximum((b * nt + i) * h16 - 1, 0), 0)),
            pl.BlockSpec((16, width), lambda b, i: (jnp.minimum((b * nt + i + 1) * h16, rows // 16 - 1), 0))]


def _fill_uext(uext, cur, prev, nxt, meta, i, nt, tm):
    uext[0:16] = jnp.where(i == 0, _glu(meta[...]), _glu(prev[...]))
    uext[16:16 + tm] = _glu(cur[...])
    uext[16 + tm:32 + tm] = jnp.where(i == nt - 1, 0.0, _glu(nxt[...]))


def _shifted_copies(dst, src, n):
    for r in range(1, 8):
        dst[r, 0:n] = src[r:r + n]


def _rows32(shifted, src, start, cols):
    q8, r = divmod(start, 8)
    if r == 0:
        return src[start:start + 32, cols]
    return shifted[r, 8 * q8:8 * q8 + 32, cols]


def _conv_fwd(pconv, pm_conv, conv_w, conv_b, nb, tm):
    rows = pconv.shape[0]
    nt = rows // nb // tm

    def body(cur, prev, nxt, meta, w_ref, b_ref, o_ref, uext, ush):
        i = pl.program_id(1)
        _fill_uext(uext, cur, prev, nxt, meta, i, nt, tm)
        _shifted_copies(ush, uext, tm + 24)
        for r0 in range(0, tm, 32):
            for c0 in range(0, D, 256):
                acc = jnp.zeros((32, 256), F32) + b_ref[:, c0:c0 + 256]
                for j in range(CONV_K):
                    acc = acc + _rows32(ush, uext, r0 + j + 1, slice(c0, c0 + 256)) * w_ref[j:j + 1, c0:c0 + 256]
                o_ref[r0:r0 + 32, c0:c0 + 256] = acc

    return pl.pallas_call(
        body, name="conv_fwd", grid=(nb, nt),
        out_shape=_sds((rows, D), F32),
        in_specs=_halo_specs(2048, tm, nt, rows)
        + [pl.BlockSpec((16, 2048), lambda b, i: (0, 0)), pl.BlockSpec((32, D), lambda b, i: (0, 0)),
           pl.BlockSpec((1, D), lambda b, i: (0, 0))],
        out_specs=pl.BlockSpec((tm, D), lambda b, i: (b * nt + i, 0)),
        scratch_shapes=[pltpu.VMEM((tm + 32, D), F32), pltpu.VMEM((8, tm + 24, D), F32)],
        compiler_params=_params(("parallel", "parallel"), 40),
    )(*_pin(pconv, pconv, pconv, pm_conv, conv_w, conv_b))


def _kv_prep(pkv, pm_kv, kg, cos, sin, nb):
    rows = pkv.shape[0]
    s_len = rows // nb
    tk = min(512, s_len)
    nt = s_len // tk

    def body(kv_ref, m_ref, g_ref, cos_ref, sin_ref, k_o, v_o, k2_o, v2_o):
        i = pl.program_id(1)
        mats = _qk_mats()
        kv = kv_ref[...]
        kr, _ = _qk_fwd(kv[:, :GROUP_W], g_ref[...], cos_ref[...], sin_ref[...], mats)
        ones = _ones_cols(tk, tk)
        for h in range(N_KV):
            k_o[0, h] = kr[:, HEAD_DIM * h:HEAD_DIM * (h + 1)].astype(BF16)
            vh = kv[:, GROUP_W + HEAD_DIM * h:GROUP_W + HEAD_DIM * (h + 1)]
            v_o[0, h] = jnp.concatenate([vh, ones], axis=1).astype(BF16)

        @pl.when(i == 0)
        def _():
            kvm = m_ref[...]
            km = kvm[:, :GROUP_W]
            kn = km * lax.rsqrt(_apply(km * km, mats[0]) + EPS) * g_ref[...]
            zeros = jnp.zeros((KEY_PAD - N_META, GROUP_W), F32)
            kfull = jnp.concatenate([kn, zeros], axis=0)
            vfull = jnp.concatenate([kvm[:, GROUP_W:], zeros], axis=0)
            ones_m = _ones_cols(KEY_PAD, N_META)
            for h in range(N_KV):
                k2_o[0, h] = kfull[:, HEAD_DIM * h:HEAD_DIM * (h + 1)].astype(BF16)
                v2_o[0, h] = jnp.concatenate([vfull[:, HEAD_DIM * h:HEAD_DIM * (h + 1)], ones_m], axis=1).astype(BF16)

    return pl.pallas_call(
        body, name="kv_prep", grid=(nb, nt),
        out_shape=(_sds((nb, N_KV, s_len, HEAD_DIM), BF16), _sds((nb, N_KV, s_len, 2 * HEAD_DIM), BF16),
                   _sds((nb, N_KV, KEY_PAD, HEAD_DIM), BF16), _sds((nb, N_KV, KEY_PAD, 2 * HEAD_DIM), BF16)),
        in_specs=[pl.BlockSpec((tk, 512), lambda b, i: (b * nt + i, 0)),
                  pl.BlockSpec((N_META, 512), lambda b, i: (0, 0)), pl.BlockSpec((1, GROUP_W), lambda b, i: (0, 0)),
                  pl.BlockSpec((tk, GROUP_W), lambda b, i: (i, 0)),
                  pl.BlockSpec((tk, GROUP_W), lambda b, i: (i, 0))],
        out_specs=(pl.BlockSpec((1, N_KV, tk, HEAD_DIM), lambda b, i: (b, 0, i, 0)),
                   pl.BlockSpec((1, N_KV, tk, 2 * HEAD_DIM), lambda b, i: (b, 0, i, 0)),
                   pl.BlockSpec((1, N_KV, KEY_PAD, HEAD_DIM), lambda b, i: (b, 0, 0, 0)),
                   pl.BlockSpec((1, N_KV, KEY_PAD, 2 * HEAD_DIM), lambda b, i: (b, 0, 0, 0))),
        compiler_params=_params(("parallel", "arbitrary"), 40),
    )(*_pin(pkv, pm_kv, kg, cos, sin))


def _ones_cols(rows, valid):
    r = lax.broadcasted_iota(jnp.int32, (rows, HEAD_DIM), 0)
    col = lax.broadcasted_iota(jnp.int32, (rows, HEAD_DIM), 1)
    return jnp.where((col < 2) & (r < valid), 1.0, 0.0).astype(F32)


def _tail_bias():
    col = lax.broadcasted_iota(jnp.int32, (1, KEY_PAD), 1)
    return jnp.where(col < N_META, 0.0, -1e30).astype(F32)


LOG2E = 1.4426950408889634


def _q_prep(pq, qg, cos, sin, nb, tm):
    rows = pq.shape[0]
    nt = rows // nb // tm
    scale = 1.0 / math.sqrt(HEAD_DIM)

    def body(q_ref, g_ref, cos_ref, sin_ref, q2_o, qt_o):
        gv, cosv, sinv = g_ref[...], cos_ref[...], sin_ref[...]
        mats = _qk_mats()
        for g in range(N_KV):
            gs = slice(GROUP_W * g, GROUP_W * (g + 1))
            qr, _ = _qk_fwd(q_ref[:, gs], gv, cosv, sinv, mats)
            q2_o[:, gs] = (qr * (scale * LOG2E)).astype(BF16)
            qt_o[gs, :] = (qr * scale).T.astype(BF16)

    row = pl.BlockSpec((tm, D), lambda b, i: (b * nt + i, 0))
    rope = pl.BlockSpec((tm, GROUP_W), lambda b, i: (i, 0))
    return pl.pallas_call(
        body, name="q_prep", grid=(nb, nt),
        out_shape=(_sds((rows, D), BF16), _sds((D, rows), BF16)),
        in_specs=[row, pl.BlockSpec((1, GROUP_W), lambda b, i: (0, 0)), rope, rope],
        out_specs=(row, pl.BlockSpec((D, tm), lambda b, i: (0, b * nt + i))),
        compiler_params=_params(("parallel", "parallel"), 32),
    )(*_pin(pq, qg, cos, sin))


def _kv_specs(s_len):
    return [pl.BlockSpec((1, 1, s_len, HEAD_DIM), lambda b, g, i: (b, g, 0, 0)),
            pl.BlockSpec((1, 1, s_len, 2 * HEAD_DIM), lambda b, g, i: (b, g, 0, 0)),
            pl.BlockSpec((1, 1, KEY_PAD, HEAD_DIM), lambda b, g, i: (b, g, 0, 0)),
            pl.BlockSpec((1, 1, KEY_PAD, 2 * HEAD_DIM), lambda b, g, i: (b, g, 0, 0))]


def _attn_fwd(q2, kv4, nb, tq):
    rows = q2.shape[0]
    s_len = rows // nb
    nq = s_len // tq

    def body(q_ref, k1_ref, v1_ref, k2_ref, v2_ref, o_ref, lse_ref):
        qs = q_ref[...]
        k1, k2, v1, v2 = k1_ref[0, 0], k2_ref[0, 0], v1_ref[0, 0], v2_ref[0, 0]
        bias = _tail_bias()
        outs, lses = [], []

        def scores(h):
            qh = qs[:, HEAD_DIM * h:HEAD_DIM * (h + 1)]
            return _dot_nt(qh, k1), _dot_nt(qh, k2) + bias

        ahead = scores(0)
        for h in range(GQA):
            s1, s2 = ahead
            if h + 1 < GQA:
                ahead = scores(h + 1)
            m = jnp.maximum(jnp.max(s1, axis=-1, keepdims=True), jnp.max(s2, axis=-1, keepdims=True))
            oe = _dot(jnp.exp2(s1 - m).astype(BF16), v1) + _dot(jnp.exp2(s2 - m).astype(BF16), v2)
            l = oe[:, HEAD_DIM:HEAD_DIM + 1]
            outs.append(oe[:, :HEAD_DIM] / l)
            lses.append(m + jnp.log2(l))
        o_ref[...] = jnp.concatenate(outs, axis=1)
        lse_ref[0, 0] = jnp.concatenate(lses, axis=1)

    return pl.pallas_call(
        body, name="attn_fwd", grid=(nb, N_KV, nq),
        out_shape=(_sds((rows, D), F32), _sds((nb, N_KV, s_len, GQA), F32)),
        in_specs=[pl.BlockSpec((tq, GROUP_W), lambda b, g, i: (b * nq + i, g))] + _kv_specs(s_len),
        out_specs=(pl.BlockSpec((tq, GROUP_W), lambda b, g, i: (b * nq + i, g)),
                   pl.BlockSpec((1, 1, tq, GQA), lambda b, g, i: (b, g, i, 0))),
        compiler_params=_params(("parallel", "parallel", "parallel"), 48),
    )(*_pin(q2, *kv4))


def _mid(x2, t2, c0, cz, o, e, w3, cn_g, cn_b, tm):
    rows = x2.shape[0]

    def body(x_ref, t_ref, c0_ref, cz_ref, o_ref, e_ref, w_ref, g_ref, b_ref,
             dy_o, mt_o, c3t_o, o2t_o, dyc_o, dya_o, do_o, dc0_o, dcz_o, de_o, sums_o):
        wco, wao, wo = w_ref[0], w_ref[1], w_ref[2]
        cn_g_v = g_ref[...]
        c0v = c0_ref[...]
        xc = c0v - jnp.mean(c0v, axis=-1, keepdims=True)
        rstd = lax.rsqrt(jnp.mean(xc * xc, axis=-1, keepdims=True) + EPS)
        n = xc * rstd
        c1 = n * cn_g_v + b_ref[...]
        s1 = _sig(c1)
        c2 = c1 * s1
        czv = cz_ref[...]
        sz = _sig(czv)
        gz = czv * sz
        c3 = c2 * gz
        yc = _dot(c3.astype(BF16), wco)
        az, gc, ga = e_ref[:, :D], e_ref[:, D:2 * D], e_ref[:, 2 * D:]
        saz = _sig(az)
        gaz = az * saz
        ov = o_ref[...]
        o2 = ov * gaz
        ya = _dot(o2.astype(BF16), wao)
        sc, sa = _sig(gc), _sig(ga)
        merged = sc * yc + sa * ya
        out = _dot(merged.astype(BF16), wo)
        err = x_ref[...] + out - t_ref[...]
        dy = err * (1.0 / D)
        dy_o[...] = dy
        dm = _dot_nt(dy.astype(BF16), wo)
        dyc = dm * sc
        dya = dm * sa
        dycb, dyab = dyc.astype(BF16), dya.astype(BF16)
        dyc_o[...] = dycb
        dya_o[...] = dyab
        de_o[:, D:2 * D] = (dyc * yc * (1.0 - sc)).astype(BF16)
        de_o[:, 2 * D:] = (dya * ya * (1.0 - sa)).astype(BF16)
        dc3 = _dot_nt(dycb, wco)
        do2 = _dot_nt(dyab, wao)
        do_o[...] = do2 * gaz
        de_o[:, :D] = (do2 * ov * _dsilu(az, saz)).astype(BF16)
        dcz_o[...] = (dc3 * c2 * _dsilu(czv, sz)).astype(BF16)
        dc1 = dc3 * gz * _dsilu(c1, s1)
        dn = dc1 * cn_g_v
        dc0 = rstd * (dn - jnp.mean(dn, axis=-1, keepdims=True) - n * jnp.mean(dn * n, axis=-1, keepdims=True))
        dc0_o[...] = dc0
        mt_o[...] = merged.T.astype(BF16)
        c3t_o[...] = c3.T.astype(BF16)
        o2t_o[...] = o2.T.astype(BF16)

        @pl.when(pl.program_id(0) == 0)
        def _():
            sums_o[...] = jnp.zeros_like(sums_o)

        sums_o[0:1, :] += jnp.sum(dc1 * n, axis=0, keepdims=True)
        sums_o[1:2, :] += jnp.sum(dc1, axis=0, keepdims=True)
        sums_o[2:3, :] += jnp.sum(dc0, axis=0, keepdims=True)
        sums_o[3:4, :] += jnp.sum(err * err, axis=0, keepdims=True)

    row = lambda wd: pl.BlockSpec((tm, wd), lambda i: (i, 0))
    col = pl.BlockSpec((D, tm), lambda i: (0, i))
    vec = pl.BlockSpec((1, D), lambda i: (0, 0))
    f32o = lambda wd: _sds((rows, wd), F32)
    b16o = lambda wd: _sds((rows, wd), BF16)
    tpo = _sds((D, rows), BF16)
    return pl.pallas_call(
        body, name="mid", grid=(rows // tm,),
        out_shape=(f32o(D), tpo, tpo, tpo, b16o(D), b16o(D), f32o(D), f32o(D), b16o(D), b16o(3 * D),
                   _sds((8, D), F32)),
        in_specs=[row(D), row(D), row(D), row(D), row(D), row(3 * D),
                  pl.BlockSpec((3, D, D), lambda i: (0, 0, 0)), vec, vec],
        out_specs=(row(D), col, col, col, row(D), row(D), row(D), row(D), row(D), row(3 * D),
                   pl.BlockSpec((8, D), lambda i: (0, 0))),
        compiler_params=_params(("arbitrary",), 60),
    )(*_pin(x2, t2, c0, cz, o, e, w3, cn_g, cn_b))


def _do_prep(d_o, o, tm):
    rows = d_o.shape[0]

    def body(do_ref, o_ref, doe_o, dot_o):
        dov = do_ref[...]
        prod = dov * o_ref[...]
        col = lax.broadcasted_iota(jnp.int32, (tm, HEAD_DIM), 1)
        for h in range(D // HEAD_DIM):
            hs = slice(HEAD_DIM * h, HEAD_DIM * (h + 1))
            delta = jnp.sum(prod[:, hs], axis=-1, keepdims=True)
            d_hi = delta.astype(BF16).astype(F32)
            tail = jnp.where(col == 0, -d_hi, jnp.where(col == 1, d_hi - delta, 0.0))
            doe_o[:, 2 * HEAD_DIM * h:2 * HEAD_DIM * (h + 1)] = jnp.concatenate([dov[:, hs], tail], axis=1).astype(BF16)
        dot_o[...] = dov.T.astype(BF16)

    row = pl.BlockSpec((tm, D), lambda i: (i, 0))
    return pl.pallas_call(
        body, name="do_prep", grid=(rows // tm,),
        out_shape=(_sds((rows, 2 * D), BF16), _sds((D, rows), BF16)),
        in_specs=[row, row],
        out_specs=(pl.BlockSpec((tm, 2 * D), lambda i: (i, 0)), pl.BlockSpec((D, tm), lambda i: (0, i))),
        compiler_params=_params(("parallel",), 32),
    )(*_pin(d_o, o))


def _q_post(dqr, pq, qg, cos, sin, nb, tm):
    rows = pq.shape[0]
    nt = rows // nb // tm

    def body(dq_ref, q_ref, g_ref, cos_ref, sin_ref, dq_o, dg_o):
        @pl.when((pl.program_id(0) == 0) & (pl.program_id(1) == 0))
        def _():
            dg_o[...] = jnp.zeros_like(dg_o)

        gv, cosv, sinv = g_ref[...], cos_ref[...], sin_ref[...]
        acc = jnp.zeros((1, GROUP_W), F32)
        mats = _qk_mats()
        for g in range(N_KV):
            gs = slice(GROUP_W * g, GROUP_W * (g + 1))
            qv = q_ref[:, gs]
            r = lax.rsqrt(_apply(qv * qv, mats[0]) + EPS)
            dq, dgr = _qk_bwd(dq_ref[:, gs], qv, r, gv, cosv, sinv, mats)
            dq_o[:, gs] = dq.astype(BF16)
            acc = acc + jnp.sum(dgr, axis=0, keepdims=True)
        dg_o[...] += acc

    row = pl.BlockSpec((tm, D), lambda b, i: (b * nt + i, 0))
    rope = pl.BlockSpec((tm, GROUP_W), lambda b, i: (i, 0))
    vec = pl.BlockSpec((1, GROUP_W), lambda b, i: (0, 0))
    return pl.pallas_call(
        body, name="q_post", grid=(nb, nt),
        out_shape=(_sds((rows, D), BF16), _sds((1, GROUP_W), F32)),
        in_specs=[row, row, vec, rope, rope], out_specs=(row, vec),
        compiler_params=_params(("arbitrary", "arbitrary"), 32),
    )(*_pin(dqr, pq, qg, cos, sin))


def _attn_bwd(q2, qst, kv4, doe, dot_, lse, nb, tq):
    rows = q2.shape[0]
    s_len = rows // nb
    nq = s_len // tq
    scale = 1.0 / math.sqrt(HEAD_DIM)

    def body(q_ref, qt_ref, k1_ref, v1_ref, k2_ref, v2_ref, doe_ref, dot_ref, lse_ref,
             dq_o, dkt_o, dvt_o, dkt2_o, dvt2_o):
        i = pl.program_id(2)
        lse = lse_ref[0, 0]
        k1, k2, v1, v2 = k1_ref[0, 0], k2_ref[0, 0], v1_ref[0, 0], v2_ref[0, 0]
        bias = _tail_bias()
        dkt1, dkt2 = jnp.zeros((HEAD_DIM, s_len), F32), jnp.zeros((HEAD_DIM, KEY_PAD), F32)
        dvt1, dvt2 = jnp.zeros((HEAD_DIM, s_len), F32), jnp.zeros((HEAD_DIM, KEY_PAD), F32)

        def products(h):
            qh = q_ref[:, HEAD_DIM * h:HEAD_DIM * (h + 1)]
            dh = doe_ref[:, 2 * HEAD_DIM * h:2 * HEAD_DIM * (h + 1)]
            return _dot_nt(qh, k1), _dot_nt(qh, k2) + bias, _dot_nt(dh, v1), _dot_nt(dh, v2)

        ahead = products(0)
        for h in range(GQA):
            hs = slice(HEAD_DIM * h, HEAD_DIM * (h + 1))
            s1, s2, dp1, dp2 = ahead
            if h + 1 < GQA:
                ahead = products(h + 1)
            lse_h = lse[:, h:h + 1]
            p1 = jnp.exp2(s1 - lse_h)
            p2 = jnp.exp2(s2 - lse_h)
            ds1 = (p1 * dp1).astype(BF16)
            ds2 = (p2 * dp2).astype(BF16)
            dq_o[:, hs] = (_dot(ds1, k1) + _dot(ds2, k2)) * scale
            dkt1 = dkt1 + _dot(qt_ref[hs, :], ds1)
            dkt2 = dkt2 + _dot(qt_ref[hs, :], ds2)
            dvt1 = dvt1 + _dot(dot_ref[hs, :], p1.astype(BF16))
            dvt2 = dvt2 + _dot(dot_ref[hs, :], p2.astype(BF16))

        @pl.when(i == 0)
        def _():
            dkt_o[0, 0], dkt2_o[0, 0], dvt_o[0, 0], dvt2_o[0, 0] = dkt1, dkt2, dvt1, dvt2

        @pl.when(i > 0)
        def _():
            dkt_o[0, 0] += dkt1
            dkt2_o[0, 0] += dkt2
            dvt_o[0, 0] += dvt1
            dvt2_o[0, 0] += dvt2

    qspec = pl.BlockSpec((tq, GROUP_W), lambda b, g, i: (b * nq + i, g))
    qtspec = pl.BlockSpec((GROUP_W, tq), lambda b, g, i: (g, b * nq + i))
    tspec = pl.BlockSpec((1, 1, HEAD_DIM, s_len), lambda b, g, i: (b, g, 0, 0))
    t2spec = pl.BlockSpec((1, 1, HEAD_DIM, KEY_PAD), lambda b, g, i: (b, g, 0, 0))
    tshape = _sds((nb, N_KV, HEAD_DIM, s_len), F32)
    t2shape = _sds((nb, N_KV, HEAD_DIM, KEY_PAD), F32)
    return pl.pallas_call(
        body, name="attn_bwd", grid=(nb, N_KV, nq),
        out_shape=(_sds((rows, D), F32), tshape, tshape, t2shape, t2shape),
        in_specs=[qspec, qtspec] + _kv_specs(s_len)
        + [pl.BlockSpec((tq, 2 * GROUP_W), lambda b, g, i: (b * nq + i, g)), qtspec,
           pl.BlockSpec((1, 1, tq, GQA), lambda b, g, i: (b, g, i, 0))],
        out_specs=(qspec, tspec, tspec, t2spec, t2spec),
        compiler_params=_params(("parallel", "parallel", "arbitrary"), 56),
    )(*_pin(q2, qst, *kv4, doe, dot_, lse))


def _kv_bwd(dkt, dvt, dkt2, dvt2, pkv, pm_kv, kg, cos, sin, nb):
    rows = pkv.shape[0]
    s_len = rows // nb
    tk = min(512, s_len)
    nt = s_len // tk

    def body(dk_ref, dv_ref, dk2_ref, dv2_ref, kv_ref, m_ref, g_ref, cos_ref, sin_ref, d_o, dm_o, dg_o):
        b, i = pl.program_id(0), pl.program_id(1)
        gv = g_ref[...]
        mats = _qk_mats()

        @pl.when((b == 0) & (i == 0))
        def _():
            dg_o[...] = jnp.zeros_like(dg_o)

        kx = kv_ref[:, :GROUP_W]
        r = lax.rsqrt(_apply(kx * kx, mats[0]) + EPS)
        dk, dgr = _qk_bwd(dk_ref[0].T, kx, r, gv, cos_ref[...], sin_ref[...], mats)
        d_o[:, :GROUP_W] = dk.astype(BF16)
        d_o[:, GROUP_W:] = dv_ref[0].T.astype(BF16)
        dg_o[...] += jnp.sum(dgr, axis=0, keepdims=True)

        @pl.when(i == 0)
        def _():
            kxm = m_ref[:, :GROUP_W]
            rm = lax.rsqrt(_apply(kxm * kxm, mats[0]) + EPS)
            dn = dk2_ref[0].T[0:N_META]
            dyg = dn * gv
            dm_o[0, :, :GROUP_W] = rm * dyg - kxm * (rm * rm * rm) * _apply(dyg * kxm, mats[0])
            dm_o[0, :, GROUP_W:] = dv2_ref[0].T[0:N_META]
            dg_o[...] += jnp.sum(dn * kxm * rm, axis=0, keepdims=True)

    tspec = pl.BlockSpec((1, GROUP_W, tk), lambda b, i: (b, 0, i))
    t2spec = pl.BlockSpec((1, GROUP_W, KEY_PAD), lambda b, i: (b, 0, 0))
    rope = pl.BlockSpec((tk, GROUP_W), lambda b, i: (i, 0))
    return pl.pallas_call(
        body, name="kv_bwd", grid=(nb, nt),
        out_shape=(_sds((rows, 512), BF16), _sds((nb, N_META, 512), F32),
                   _sds((1, GROUP_W), F32)),
        in_specs=[tspec, tspec, t2spec, t2spec, pl.BlockSpec((tk, 512), lambda b, i: (b * nt + i, 0)),
                  pl.BlockSpec((N_META, 512), lambda b, i: (0, 0)), pl.BlockSpec((1, GROUP_W), lambda b, i: (0, 0)),
                  rope, rope],
        out_specs=(pl.BlockSpec((tk, 512), lambda b, i: (b * nt + i, 0)),
                   pl.BlockSpec((1, N_META, 512), lambda b, i: (b, 0, 0)),
                   pl.BlockSpec((1, GROUP_W), lambda b, i: (0, 0))),
        compiler_params=_params(("arbitrary", "arbitrary"), 40),
    )(*_pin(dkt, dvt, dkt2, dvt2, pkv, pm_kv, kg, cos, sin))


def _conv_bwd(dc0, pconv, pm_conv, conv_w, nb, tm):
    rows = pconv.shape[0]
    nt = rows // nb // tm

    def body(dcur, dprev, dnxt, cur, meta, w_ref, da_o, dam_o, gw_o, ucur, dext, dsh):
        b, i = pl.program_id(0), pl.program_id(1)
        ucur[...] = _glu(cur[...])
        dext[0:16] = jnp.zeros((16, D), F32)
        dext[16:32] = jnp.where(i == 0, 0.0, dprev[...])
        dext[32:32 + tm] = dcur[...]
        dext[32 + tm:48 + tm] = jnp.where(i == nt - 1, 0.0, dnxt[...])
        _shifted_copies(dsh, dext, tm + 40)

        @pl.when((b == 0) & (i == 0))
        def _():
            gw_o[...] = jnp.zeros_like(gw_o)

        for c0 in range(0, D, 256):
            cs = slice(c0, c0 + 256)
            for r0 in range(0, tm, 32):
                acc = jnp.zeros((32, 256), F32)
                for j in range(CONV_K):
                    acc = acc + _rows32(dsh, dext, r0 + 47 - j, cs) * w_ref[j:j + 1, cs]
                cv = cur[r0:r0 + 32, c0:c0 + 256]
                sg = _sig(cur[r0:r0 + 32, D + c0:D + c0 + 256])
                da_o[r0:r0 + 32, cs] = (acc * sg).astype(BF16)
                da_o[r0:r0 + 32, D + c0:D + c0 + 256] = (acc * cv * sg * (1.0 - sg)).astype(BF16)
            for j in range(CONV_K):
                acc = jnp.zeros((32, 256), F32)
                for r0 in range(0, tm, 32):
                    acc = acc + _rows32(dsh, dext, r0 + 47 - j, cs) * ucur[r0:r0 + 32, cs]
                gw_o[j:j + 1, cs] += jnp.sum(acc, axis=0, keepdims=True)

        @pl.when(i == 0)
        def _():
            for c0 in range(0, D, 256):
                cs = slice(c0, c0 + 256)
                cv = meta[:, c0:c0 + 256]
                sg = _sig(meta[:, D + c0:D + c0 + 256])
                um = cv * sg
                acc = jnp.zeros((16, 256), F32)
                for j in range(CONV_K):
                    d = dext[31 - j:47 - j, cs]
                    acc = acc + d * w_ref[j:j + 1, cs]
                    gw_o[j:j + 1, cs] += jnp.sum(d * um, axis=0, keepdims=True)
                dam_o[0, :, cs] = acc * sg
                dam_o[0, :, D + c0:D + c0 + 256] = acc * cv * sg * (1.0 - sg)

    return pl.pallas_call(
        body, name="conv_bwd", grid=(nb, nt),
        out_shape=(_sds((rows, 2048), BF16), _sds((nb, N_META, 2048), F32),
                   _sds((32, D), F32)),
        in_specs=_halo_specs(D, tm, nt, rows)
        + [pl.BlockSpec((tm, 2048), lambda b, i: (b * nt + i, 0)),
           pl.BlockSpec((16, 2048), lambda b, i: (0, 0)), pl.BlockSpec((32, D), lambda b, i: (0, 0))],
        out_specs=(pl.BlockSpec((tm, 2048), lambda b, i: (b * nt + i, 0)),
                   pl.BlockSpec((1, N_META, 2048), lambda b, i: (b, 0, 0)),
                   pl.BlockSpec((32, D), lambda b, i: (0, 0))),
        scratch_shapes=[pltpu.VMEM((tm, D), F32), pltpu.VMEM((tm + 48, D), F32), pltpu.VMEM((8, tm + 40, D), F32)],
        compiler_params=_params(("arbitrary", "arbitrary"), 48),
    )(*_pin(dc0, dc0, dc0, pconv, pm_conv, conv_w))


def _meta_bwd(dam, ddm, w_full, meta_full, norm_g):
    nb = dam.shape[0]

    def body(a_ref, d_ref, wc_ref, wkv_ref, m_ref, g_ref, gm_o, dg_o):
        a, d = a_ref[0], d_ref[0]
        for b in range(1, nb):
            a = a + a_ref[b]
            d = d + d_ref[b]
        dxn = _dot_nt(a.astype(BF16), wc_ref[...]) + _dot_nt(d.astype(BF16), wkv_ref[...])
        v = m_ref[...]
        r = lax.rsqrt(jnp.mean(v * v, axis=-1, keepdims=True) + EPS)
        gm_o[...] = _rms_bwd(dxn, v, r, g_ref[...])
        dg_o[...] = jnp.sum(dxn * v * r, axis=0, keepdims=True)

    return pl.pallas_call(
        body, name="meta_bwd", grid=(1,),
        out_shape=(_sds((N_META, D), F32), _sds((1, D), F32)),
        in_specs=[pl.BlockSpec((nb, N_META, 2048), lambda i: (0, 0, 0)), pl.BlockSpec((nb, N_META, 512), lambda i: (0, 0, 0)),
                  pl.BlockSpec((D, 2048), lambda i: (0, 0)), pl.BlockSpec((D, 512), lambda i: (0, G_KV[0] // 512)),
                  pl.BlockSpec((N_META, D), lambda i: (0, 0)), pl.BlockSpec((1, D), lambda i: (0, 0))],
        out_specs=(pl.BlockSpec((N_META, D), lambda i: (0, 0)), pl.BlockSpec((1, D), lambda i: (0, 0))),
        compiler_params=_params(("arbitrary",), 32),
    )(*_pin(dam, ddm, w_full, w_full, meta_full, norm_g))


def _dxn(d_groups, w_full, x2, dy, norm_g, dg_init, tm):
    rows = x2.shape[0]
    groups = (G_CONV, G_CZ, G_Q, G_KV, G_E)

    def body(da, db, dq, dd, de, w_hbm, x_ref, dy_ref, g_ref, gi_ref, gx_o, dg_o, w_vmem, sem):
        @pl.when(pl.program_id(0) == 0)
        def _():
            cp = pltpu.make_async_copy(w_hbm, w_vmem, sem)
            cp.start()
            cp.wait()
            dg_o[...] = gi_ref[...]

        dxn = jnp.zeros((tm, D), F32)
        for ref, (off, wd) in zip((da, db, dq, dd, de), groups):
            for c0 in range(0, wd, 512):
                dxn = dxn + _dot_nt(ref[:, c0:c0 + 512], w_vmem[:, off + c0:off + c0 + 512])
        v = x_ref[...]
        r = lax.rsqrt(jnp.mean(v * v, axis=-1, keepdims=True) + EPS)
        gx_o[...] = dy_ref[...] + _rms_bwd(dxn, v, r, g_ref[...])
        dg_o[...] += jnp.sum(dxn * v * r, axis=0, keepdims=True)

    row = lambda wd: pl.BlockSpec((tm, wd), lambda i: (i, 0))
    vec = pl.BlockSpec((1, D), lambda i: (0, 0))
    return pl.pallas_call(
        body, name="dxn", grid=(rows // tm,),
        out_shape=(_sds((rows, D), F32), _sds((1, D), F32)),
        in_specs=[row(wd) for _, wd in groups] + [pl.BlockSpec(memory_space=pl.ANY), row(D), row(D), vec, vec],
        out_specs=(row(D), vec),
        scratch_shapes=[pltpu.VMEM((D, IN_DIM), BF16), pltpu.SemaphoreType.DMA],
        compiler_params=_params(("arbitrary",), 56),
    )(*_pin(*d_groups, w_full, x2, dy, norm_g, dg_init))


def _wgrad(at, b, buf, slot, col_off, name, meta=None):
    rows, n = b.shape
    tn, tk = 512, min(2048, rows)
    nk = rows // tk
    j0 = col_off // tn

    def body(*refs):
        if meta is None:
            at_ref, b_ref, _, o_ref = refs
        else:
            at_ref, b_ref, xm_ref, dm_ref, _, o_ref = refs
        k = pl.program_id(1)

        @pl.when(k == 0)
        def _():
            if meta is None:
                o_ref[0] = jnp.zeros((D, tn), F32)
            else:
                dm = dm_ref[0]
                for e in range(1, dm_ref.shape[0]):
                    dm = dm + dm_ref[e]
                dm = jnp.concatenate([dm, jnp.zeros((128 - N_META, tn), F32)], axis=0)
                o_ref[0] = _dot(xm_ref[...], dm.astype(BF16))

        o_ref[0] += _dot(at_ref[...], b_ref[...].astype(BF16))

    in_specs = [pl.BlockSpec((D, tk), lambda j, k: (0, k)), pl.BlockSpec((tk, tn), lambda j, k: (k, j))]
    args = [at, b]
    if meta is not None:
        xmt, dm = meta
        in_specs += [pl.BlockSpec((D, 128), lambda j, k: (0, 0)),
                     pl.BlockSpec((dm.shape[0], N_META, tn), lambda j, k: (0, 0, j))]
        args += [xmt, dm]
    in_specs.append(pl.BlockSpec(memory_space=pl.ANY))
    args.append(buf)
    return pl.pallas_call(
        body, name=name, grid=(n // tn, nk),
        out_shape=_sds(buf.shape, F32),
        in_specs=in_specs,
        out_specs=pl.BlockSpec((1, D, tn), lambda j, k: (slot, 0, j0 + j)),
        input_output_aliases={len(args) - 1: 0},
        compiler_params=_params(("parallel", "arbitrary"), 32),
    )(*_pin(*args))


def _rope_tables(s_len):
    pos = jnp.arange(s_len, dtype=jnp.int32)
    row_ids = (pos // GRID_W).astype(F32)
    col_ids = (pos % GRID_W).astype(F32)
    inv_freq = ROPE_THETA ** (-jnp.arange(ROPE_FREQS, dtype=F32) / ROPE_FREQS)
    a_row = row_ids[:, None] * inv_freq[None, :]
    a_col = col_ids[:, None] * inv_freq[None, :]
    ang = jnp.concatenate([a_row, a_row, a_col, a_col], axis=-1)
    return jnp.tile(jnp.cos(ang), (1, GQA)), jnp.tile(jnp.sin(ang), (1, GQA))


def _local_step(x, loss_target, norm_g, conv_b, cn_g, cn_b, q_g, k_g, w_full, w3_full, conv_w_full, meta_full,
                reduce_start=None):
    nb, s_len, _ = x.shape
    rows = nb * s_len
    x2 = x.reshape(rows, D)
    t2 = loss_target.reshape(rows, D)
    cos, sin = _rope_tables(s_len)
    qg = jnp.tile(q_g, (1, GQA))
    kg = jnp.tile(k_g, (1, N_KV))

    xnmt, pm_conv, pm_kv = _meta_fwd(meta_full, norm_g, w_full)
    pconv, pcz, pq, pkv, pe, xnt = _in_proj(x2, norm_g, w_full, 256)
    c0 = _conv_fwd(pconv, pm_conv, conv_w_full, conv_b, nb, 256)
    tq = min(512, s_len)
    kv4 = _kv_prep(pkv, pm_kv, kg, cos, sin, nb)
    q2, qst = _q_prep(pq, qg, cos, sin, nb, 256)
    o, lse = _attn_fwd(q2, kv4, nb, tq)
    if callable(w3_full):
        w3_full = w3_full(o)
    dy, mt, c3t, o2t, dyc, dya, d_o, dc0, dcz, de, sums = _mid(x2, t2, c0, pcz, o, pe, w3_full, cn_g, cn_b, 256)
    doe, dot_ = _do_prep(d_o, o, 256)
    dqr, dkt, dvt, dkt2, dvt2 = _attn_bwd(q2, qst, kv4, doe, dot_, lse, nb, tq)
    dq, dqg = _q_post(dqr, pq, qg, cos, sin, nb, 256)
    dd, ddm, dkg = _kv_bwd(dkt.reshape(nb, GROUP_W, s_len), dvt.reshape(nb, GROUP_W, s_len),
                           dkt2.reshape(nb, GROUP_W, KEY_PAD), dvt2.reshape(nb, GROUP_W, KEY_PAD),
                           pkv, pm_kv, kg, cos, sin, nb)
    da, dam, gcw = _conv_bwd(dc0, pconv, pm_conv, conv_w_full, nb, 256)
    gmeta, dng_m = _meta_bwd(dam, ddm, w_full, meta_full, norm_g)

    gw3 = lax.empty((3, D, D), F32)
    gw3 = _wgrad(c3t, dyc, gw3, 0, 0, "wgrad_conv_out")
    gw3 = _wgrad(o2t, dya, gw3, 1, 0, "wgrad_attn_out")
    gw3 = _wgrad(mt, dy, gw3, 2, 0, "wgrad_out")
    gwin = lax.empty((1, D, IN_DIM), F32)
    gwin = _wgrad(xnt, da, gwin, 0, G_CONV[0], "wgrad_in_conv", meta=(xnmt, dam))
    gwin = _wgrad(xnt, dcz, gwin, 0, G_CZ[0], "wgrad_in_cz")
    gwin = _wgrad(xnt, dq, gwin, 0, G_Q[0], "wgrad_in_q")
    gwin = _wgrad(xnt, dd, gwin, 0, G_KV[0], "wgrad_in_kv", meta=(xnmt, ddm))
    gwin = _wgrad(xnt, de, gwin, 0, G_E[0], "wgrad_in_e")

    pending = None
    if reduce_start is not None:
        token, pending = reduce_start(gwin, gw3, gcw, gmeta)
        dng_m = dng_m + token[0:1, 0:1]
    gx, dng = _dxn((da, dcz, dq, dd, de), w_full, x2, dy, norm_g, dng_m, 512)

    zeros = jnp.zeros((1, D - 2 * GROUP_W), F32)
    smalls = jnp.concatenate([dng, sums[2:3], sums[0:1], sums[1:2], jnp.concatenate([dqg, dkg, zeros], axis=1),
                              sums[3:4], jnp.zeros((2, D), F32)], axis=0)
    return gx.reshape(nb, s_len, D), gwin, gw3, gcw, gmeta, smalls, pending


def _xyc():
    return lax.axis_index("x"), lax.axis_index("y"), lax.axis_index("c")


def _reduce_sibling(gwin, gw3v, gcm):
    def body(gwin_ref, gw3_ref, gcm_ref, r_win, r_w3, r_cm, send, recv):
        x, y, c = _xyc()
        o = 1 - c
        half = D // 2
        outs = ((gwin_ref.at[pl.ds(o * half, half), :], r_win), (gw3_ref.at[:, :, o], r_w3), (gcm_ref.at[o], r_cm))
        cps = []
        for a, (src, dst) in enumerate(outs):
            cp = pltpu.make_async_remote_copy(src_ref=src, dst_ref=dst, send_sem=send.at[a], recv_sem=recv.at[a],
                                              device_id=(x, y, o), device_id_type=MESH)
            cp.start()
            cps.append(cp)
        for cp in cps:
            cp.wait()

    any_spec = pl.BlockSpec(memory_space=pl.ANY)
    return pl.pallas_call(
        body, name="reduce_sibling",
        out_shape=(_sds((D // 2, IN_DIM), F32), _sds((3, 4, 128, D), F32),
                   _sds((24, D), F32)),
        in_specs=[any_spec] * 3, out_specs=(any_spec,) * 3,
        scratch_shapes=[pltpu.SemaphoreType.DMA((3,)), pltpu.SemaphoreType.DMA((3,))],
    )(*_pin(gwin, gw3v, gcm))


def _add_sibling(gwin, gw3v, gcm, r_win, r_w3, r_cm):
    c = lax.axis_index("c").astype(jnp.int32).reshape(1)
    half = D // 2
    tr = 64

    def body1(c_ref, a_ref, b_ref, o_ref):
        o_ref[...] = (a_ref[...] + b_ref[...]).astype(BF16)

    cs_win = pl.pallas_call(
        body1, name="add_sibling_w_in", out_shape=_sds((half, IN_DIM), BF16),
        grid_spec=pltpu.PrefetchScalarGridSpec(
            num_scalar_prefetch=1, grid=(half // tr,),
            in_specs=[pl.BlockSpec((tr, IN_DIM), lambda i, c_ref: (c_ref[0] * (half // tr) + i, 0)),
                      pl.BlockSpec((tr, IN_DIM), lambda i, c_ref: (i, 0))],
            out_specs=pl.BlockSpec((tr, IN_DIM), lambda i, c_ref: (i, 0))),
        compiler_params=_params(("parallel",), 32),
    )(c, *_pin(gwin, r_win))

    def body2(c_ref, a_ref, b_ref, o_ref):
        o_ref[0, 0] = (a_ref[0, 0, 0] + b_ref[0, 0]).astype(BF16)

    cs_w3 = pl.pallas_call(
        body2, name="add_sibling_w3", out_shape=_sds((3, 4, 128, D), BF16),
        grid_spec=pltpu.PrefetchScalarGridSpec(
            num_scalar_prefetch=1, grid=(3, 4),
            in_specs=[pl.BlockSpec((1, 1, 1, 128, D), lambda w, s, c_ref: (w, s, c_ref[0], 0, 0)),
                      pl.BlockSpec((1, 1, 128, D), lambda w, s, c_ref: (w, s, 0, 0))],
            out_specs=pl.BlockSpec((1, 1, 128, D), lambda w, s, c_ref: (w, s, 0, 0))),
        compiler_params=_params(("parallel", "parallel"), 32),
    )(c, *_pin(gw3v, r_w3))

    def body3(c_ref, a_ref, b_ref, o_ref):
        o_ref[...] = a_ref[0] + b_ref[...]

    cs_cm = pl.pallas_call(
        body3, name="add_sibling_cm", out_shape=_sds((24, D), F32),
        grid_spec=pltpu.PrefetchScalarGridSpec(
            num_scalar_prefetch=1, grid=(1,),
            in_specs=[pl.BlockSpec((1, 24, D), lambda i, c_ref: (c_ref[0], 0, 0)),
                      pl.BlockSpec((24, D), lambda i, c_ref: (0, 0))],
            out_specs=pl.BlockSpec((24, D), lambda i, c_ref: (0, 0))),
        compiler_params=_params(("arbitrary",), 32),
    )(c, *_pin(gcm, r_cm))
    return cs_win, cs_w3, cs_cm


def _reduce_chips_copies(srcs, lands, send, recv):
    win_ref, w3_ref, cm_ref = srcs
    r_win, r_w3, r_cm = lands
    x, y, c = _xyc()
    peers = ((1 - x, y), (x, 1 - y), (1 - x, 1 - y))
    cps = []
    for k, (px, py) in enumerate(peers):
        ps = 2 * px + py
        items = ((win_ref.at[:, pl.ds(ps * W_IN_SHARD, W_IN_SHARD)], r_win.at[k]),
                 (w3_ref.at[:, ps], r_w3.at[k]),
                 (cm_ref.at[:, pl.ds(ps * ROW_SHARD, ROW_SHARD)], r_cm.at[k]))
        for a, (src, dst) in enumerate(items):
            cps.append(pltpu.make_async_remote_copy(src_ref=src, dst_ref=dst, send_sem=send.at[3 * a + k],
                                                    recv_sem=recv.at[3 * a + k], device_id=(px, py, c),
                                                    device_id_type=MESH))
    return cps


_HBM = pl.BlockSpec(memory_space=pltpu.HBM)
_SEM = pl.BlockSpec(memory_space=pltpu.SEMAPHORE)
_EFFECT = pltpu.SideEffectType.DATAFLOW_SIDE_EFFECTING


def _reduce_chips_start(cs_win, cs_w3, cs_cm):
    srcs = (cs_win, cs_w3, cs_cm)
    lands = (lax.empty((3, D // 2, W_IN_SHARD), BF16), lax.empty((3, 3, 128, D), BF16),
             lax.empty((3, 24, ROW_SHARD), F32))

    def body(*refs):
        srcs_in, lands_in, send, recv, token = refs[0:3], refs[3:6], refs[6], refs[7], refs[14]
        for cp in _reduce_chips_copies(srcs_in, lands_in, send, recv):
            cp.start()
        token[...] = jnp.zeros_like(token)

    hbm = lambda a: pltpu.HBM(a.shape, a.dtype)
    outs = pl.pallas_call(
        body, name="reduce_chips_start",
        out_shape=(pltpu.SemaphoreType.DMA((9,)), pltpu.SemaphoreType.DMA((9,)),
                   *[hbm(a) for a in srcs], *[hbm(a) for a in lands], jax.ShapeDtypeStruct((8, 128), F32)),
        in_specs=[_HBM] * 6,
        out_specs=(_SEM, _SEM, *[_HBM] * 6, pl.BlockSpec(memory_space=pltpu.VMEM)),
        input_output_aliases={i: i + 2 for i in range(6)},
        compiler_params=pltpu.CompilerParams(has_side_effects=_EFFECT),
    )(*[pltpu.with_memory_space_constraint(a, pltpu.HBM) for a in srcs + lands])
    return outs[0], outs[1], outs[2:5], outs[5:8], outs[8]


def _reduce_chips_wait(send, recv, srcs, lands, after):
    def body(*refs):
        srcs_in, lands_in, send_ref, recv_ref = refs[0:3], refs[3:6], refs[6], refs[7]
        for cp in _reduce_chips_copies(srcs_in, lands_in, send_ref, recv_ref):
            cp.wait_send()
            cp.wait_recv()

    hbm = lambda a: pltpu.HBM(a.shape, a.dtype)
    outs = pl.pallas_call(
        body, name="reduce_chips_wait",
        out_shape=(*[hbm(a) for a in srcs], *[hbm(a) for a in lands]),
        in_specs=[_HBM] * 6 + [_SEM, _SEM, pl.BlockSpec(memory_space=pl.ANY)],
        out_specs=(_HBM,) * 6,
        input_output_aliases={i: i for i in range(6)},
        compiler_params=pltpu.CompilerParams(has_side_effects=_EFFECT),
    )(*srcs, *lands, send, recv, after)
    return outs[0:3], outs[3:6]


def _add_chips(cs_win, cs_w3, cs_cm, r_win, r_w3, r_cm):
    x, y, c = _xyc()
    idx = jnp.stack([2 * x + y, c]).astype(jnp.int32)
    half = D // 2
    tr = 128

    def body1(i_ref, a_ref, b_ref, o_ref):
        f = lambda v: v.astype(F32)
        o_ref[0] = (f(a_ref[...]) + f(b_ref[2])) + (f(b_ref[0]) + f(b_ref[1]))

    f_win = pl.pallas_call(
        body1, name="add_chips_w_in", out_shape=_sds((2, half, W_IN_SHARD), F32),
        grid_spec=pltpu.PrefetchScalarGridSpec(
            num_scalar_prefetch=1, grid=(half // tr,),
            in_specs=[pl.BlockSpec((tr, W_IN_SHARD), lambda i, r: (i, r[0])),
                      pl.BlockSpec((3, tr, W_IN_SHARD), lambda i, r: (0, i, 0))],
            out_specs=pl.BlockSpec((1, tr, W_IN_SHARD), lambda i, r: (r[1], i, 0))),
        compiler_params=_params(("parallel",), 32),
    )(idx, *_pin(cs_win, r_win))

    def body2(i_ref, a_ref, b_ref, o_ref):
        f = lambda v: v.astype(F32)
        o_ref[0, 0] = (f(a_ref[0, 0]) + f(b_ref[2, 0])) + (f(b_ref[0, 0]) + f(b_ref[1, 0]))

    f_w3 = pl.pallas_call(
        body2, name="add_chips_w3", out_shape=_sds((3, 2, 128, D), F32),
        grid_spec=pltpu.PrefetchScalarGridSpec(
            num_scalar_prefetch=1, grid=(3,),
            in_specs=[pl.BlockSpec((1, 1, 128, D), lambda w, r: (w, r[0], 0, 0)),
                      pl.BlockSpec((3, 1, 128, D), lambda w, r: (0, w, 0, 0))],
            out_specs=pl.BlockSpec((1, 1, 128, D), lambda w, r: (w, r[1], 0, 0))),
        compiler_params=_params(("parallel",), 32),
    )(idx, *_pin(cs_w3, r_w3))

    def body3(i_ref, a_ref, b_ref, o_ref):
        o_ref[0] = (a_ref[...] + b_ref[2]) + (b_ref[0] + b_ref[1])

    f_cm = pl.pallas_call(
        body3, name="add_chips_cm", out_shape=_sds((2, 24, ROW_SHARD), F32),
        grid_spec=pltpu.PrefetchScalarGridSpec(
            num_scalar_prefetch=1, grid=(1,),
            in_specs=[pl.BlockSpec((24, ROW_SHARD), lambda i, r: (0, r[0])),
                      pl.BlockSpec((3, 24, ROW_SHARD), lambda i, r: (0, 0, 0))],
            out_specs=pl.BlockSpec((1, 24, ROW_SHARD), lambda i, r: (r[1], 0, 0))),
        compiler_params=_params(("arbitrary",), 32),
    )(idx, *_pin(cs_cm, r_cm))
    return f_win, f_w3, f_cm


def _share_sibling(f_win, f_w3, f_cm, smalls):
    def body(win_in, w3_in, cm_in, sm_ref, win_ref, w3_ref, cm_ref, r_sm, send, recv, ssend, srecv, lsem):
        x, y, c = _xyc()
        o = 1 - c
        cps = []
        for a, (ref, sl) in enumerate(((win_ref, lambda h: win_ref.at[h]), (w3_ref, lambda h: w3_ref.at[:, h]),
                                       (cm_ref, lambda h: cm_ref.at[h]))):
            cp = pltpu.make_async_remote_copy(src_ref=sl(c), dst_ref=sl(c), send_sem=send.at[a], recv_sem=recv.at[a],
                                              device_id=(x, y, o), device_id_type=MESH)
            cp.start()
            cps.append((cp, sl))
        me = 4 * x + 2 * y + c
        loc = pltpu.make_async_copy(sm_ref, r_sm.at[me], lsem)
        loc.start()
        scps = []
        for d in range(1, 8):
            px, py, pc = (x + (d >> 2)) % 2, (y + ((d >> 1) & 1)) % 2, (c + (d & 1)) % 2
            cp = pltpu.make_async_remote_copy(src_ref=sm_ref, dst_ref=r_sm.at[me], send_sem=ssend.at[d - 1],
                                              recv_sem=srecv.at[d - 1], device_id=(px, py, pc), device_id_type=MESH)
            cp.start()
            scps.append((cp, 4 * px + 2 * py + pc))
        for a, (cp, sl) in enumerate(cps):
            pltpu.make_async_remote_copy(src_ref=sl(o), dst_ref=sl(o), send_sem=send.at[a], recv_sem=recv.at[a],
                                         device_id=(x, y, o), device_id_type=MESH).wait_recv()
            cp.wait_send()
        for d, (cp, pid) in enumerate(scps):
            pltpu.make_async_remote_copy(src_ref=sm_ref, dst_ref=r_sm.at[pid], send_sem=ssend.at[d],
                                         recv_sem=srecv.at[d], device_id=(x, y, c), device_id_type=MESH).wait_recv()
            cp.wait_send()
        loc.wait()

    any_spec = pl.BlockSpec(memory_space=pl.ANY)
    return pl.pallas_call(
        body, name="share_sibling",
        out_shape=(_sds(f_win.shape, F32), _sds(f_w3.shape, F32), _sds(f_cm.shape, F32), _sds((8, 8, D), F32)),
        in_specs=[any_spec] * 4, out_specs=(any_spec,) * 4,
        input_output_aliases={0: 0, 1: 1, 2: 2},
        scratch_shapes=[pltpu.SemaphoreType.DMA((3,)), pltpu.SemaphoreType.DMA((3,)),
                        pltpu.SemaphoreType.DMA((7,)), pltpu.SemaphoreType.DMA((7,)), pltpu.SemaphoreType.DMA],
    )(*_pin(f_win, f_w3, f_cm, smalls))


def _adamw_math(w, g, m, v):
    m = ADAM_B1 * m + (1.0 - ADAM_B1) * g
    v = ADAM_B2 * v + (1.0 - ADAM_B2) * (g * g)
    m_hat = m / (1.0 - ADAM_B1 ** ADAM_STEP)
    v_hat = v / (1.0 - ADAM_B2 ** ADAM_STEP)
    delta = -ADAM_LR * (m_hat / (jnp.sqrt(v_hat) + ADAM_EPS) + ADAM_WD * w)
    return delta, m, v


def _adamw(w, g, m, v, tr, name):
    rows, cols = w.shape

    def body(w_ref, g_ref, m_ref, v_ref, d_o, m_o, v_o):
        d_o[...], m_o[...], v_o[...] = _adamw_math(w_ref[...], g_ref[...], m_ref[...], v_ref[...])

    spec = pl.BlockSpec((tr, cols), lambda i: (i, 0))
    return pl.pallas_call(
        body, name=name, grid=(rows // tr,),
        out_shape=(_sds((rows, cols), F32),) * 3,
        in_specs=[spec] * 4, out_specs=(spec,) * 3,
        compiler_params=_params(("parallel",), 32),
    )(*_pin(w, g, m, v))


def _adamw3(g3, ws, ms, vs):
    def body(g_ref, *refs):
        w_refs, m_refs, v_refs, outs = refs[0:3], refs[3:6], refs[6:9], refs[9:]
        g_os, d_os, m_os, v_os = outs[0:3], outs[3:6], outs[6:9], outs[9:12]
        for i in range(3):
            g = g_ref[i]
            g_os[i][0] = g
            d_os[i][0], m_os[i][0], v_os[i][0] = _adamw_math(w_refs[i][0], g, m_refs[i][0], v_refs[i][0])

    return pl.pallas_call(
        body, name="adamw_w3", out_shape=(jax.ShapeDtypeStruct((1, ROW_SHARD, D), F32),) * 12,
        compiler_params=pltpu.CompilerParams(vmem_limit_bytes=48 << 20),
    )(g3, *ws, *ms, *vs)


def _adamw_cm(f_cm, ws, ms, vs):
    def body(f_ref, *refs):
        w_refs, m_refs, v_refs, outs = refs[0:2], refs[2:4], refs[4:6], refs[6:14]
        gcw, gmt = refs[14], refs[15]
        gcw[0:16] = f_ref[0, 0:16]
        gcw[16:32] = f_ref[1, 0:16]
        gmt[0:8] = f_ref[0, 16:24]
        gmt[8:16] = f_ref[1, 16:24]
        g_conv = gcw[0:CONV_K, :]
        g_meta = gmt[...]
        outs[0][0] = g_conv
        outs[1][...] = g_meta
        outs[2][0], outs[4][0], outs[6][0] = _adamw_math(w_refs[0][0], g_conv, m_refs[0][0], v_refs[0][0])
        outs[3][...], outs[5][...], outs[7][...] = _adamw_math(w_refs[1][...], g_meta, m_refs[1][...], v_refs[1][...])

    pair = (jax.ShapeDtypeStruct((1, CONV_K, ROW_SHARD), F32), jax.ShapeDtypeStruct((N_META, ROW_SHARD), F32))
    return pl.pallas_call(
        body, name="adamw_cm", out_shape=pair * 4,
        scratch_shapes=[pltpu.VMEM((32, ROW_SHARD), F32), pltpu.VMEM((N_META, ROW_SHARD), F32)],
    )(f_cm, *ws, *ms, *vs)


def _adamw_small(r_sm, ws, ms, vs):
    def body(s_ref, *refs):
        w_refs, m_refs, v_refs, outs = refs[0:6], refs[6:12], refs[12:18], refs[18:]
        loss_o, g_os, d_os, m_os, v_os = outs[0], outs[1:7], outs[7:13], outs[13:19], outs[19:25]
        g = s_ref[0]
        for dev in range(1, 8):
            g = g + s_ref[dev]
        qk = g[4:5, :]
        qg = qk[:, 0:HEAD_DIM]
        kg = qk[:, GROUP_W:GROUP_W + HEAD_DIM]
        for h in range(1, GQA):
            qg = qg + qk[:, HEAD_DIM * h:HEAD_DIM * (h + 1)]
            kg = kg + qk[:, GROUP_W + HEAD_DIM * h:GROUP_W + HEAD_DIM * (h + 1)]
        loss_o[...] = (0.5 / D) * jnp.sum(g[5:6, :], axis=-1, keepdims=True)
        for i, gi in enumerate((g[0:1], g[1:2], g[2:3], g[3:4], qg, kg)):
            g_os[i][...] = gi
            d_os[i][...], m_os[i][...], v_os[i][...] = _adamw_math(w_refs[i][...], gi, m_refs[i][...], v_refs[i][...])

    six = tuple(jax.ShapeDtypeStruct(w.shape, F32) for w in ws)
    return pl.pallas_call(
        body, name="adamw_small", out_shape=(jax.ShapeDtypeStruct((1, 1), F32),) + six * 4,
    )(r_sm, *ws, *ms, *vs)


def kernel(x, meta_tokens, norm_g, w_in, conv_w, conv_b, conv_norm_g, conv_norm_b, w_conv_out, q_norm_g, k_norm_g, w_attn_out, w_out, loss_target, m_meta_tokens, m_norm_g, m_w_in, m_conv_w, m_conv_b, m_conv_norm_g, m_conv_norm_b, m_w_conv_out, m_q_norm_g, m_k_norm_g, m_w_attn_out, m_w_out, v_meta_tokens, v_norm_g, v_w_in, v_conv_w, v_conv_b, v_conv_norm_g, v_conv_norm_b, v_w_conv_out, v_q_norm_g, v_k_norm_g, v_w_attn_out, v_w_out):
    pad_k = lambda a: jnp.pad(a[0], ((0, 32 - CONV_K), (0, 0)))
    w3_s = (w_conv_out, w_attn_out, w_out)
    w_full, conv_w_full, meta_full, w3b, w3_land = _gather_weights(w_in[0], w3_s, pad_k(conv_w), meta_tokens)
    w3_pending, token = _w3_start(w3b, w3_land)
    norm_g_fwd = norm_g + token[0:1, 0:1]

    def w3_full(after):
        return _w3_wait(*w3_pending, after)

    def reduce_start(gwin, gw3, gcw, gmeta):
        gwin2 = gwin.reshape(D, IN_DIM)
        gw3v = gw3.reshape(3, N_CHIPS, 2, 128, D)
        gcm = jnp.concatenate([gcw.reshape(2, 16, D), gmeta.reshape(2, 8, D)], axis=1)
        r_win, r_w3, r_cm = _reduce_sibling(gwin2, gw3v, gcm)
        cs = _add_sibling(gwin2, gw3v, gcm, r_win, r_w3, r_cm)
        send, recv, srcs, lands, token = _reduce_chips_start(*cs)
        return token, (send, recv, srcs, lands)

    gx, _, _, _, _, smalls, pending = _local_step(
        x, loss_target, norm_g_fwd, conv_b, conv_norm_g, conv_norm_b, q_norm_g, k_norm_g,
        w_full, w3_full, conv_w_full, meta_full, reduce_start)
    (cs_win, cs_w3, cs_cm), (r2_win, r2_w3, r2_cm) = _reduce_chips_wait(*pending, gx)
    f_win, f_w3, f_cm = _add_chips(cs_win, cs_w3, cs_cm, r2_win, r2_w3, r2_cm)
    f_win, f_w3, f_cm, r_sm = _share_sibling(f_win, f_w3, f_cm, smalls)

    g_w_in = f_win.reshape(D, W_IN_SHARD)
    d_w_in, nm_w_in, nv_w_in = _adamw(w_in[0], g_w_in, m_w_in[0], v_w_in[0], 128, "adamw_w_in")
    w3 = _adamw3(f_w3.reshape(3, ROW_SHARD, D), w3_s, (m_w_conv_out, m_w_attn_out, m_w_out),
                 (v_w_conv_out, v_w_attn_out, v_w_out))
    cm = _adamw_cm(f_cm, (conv_w, meta_tokens), (m_conv_w, m_meta_tokens), (v_conv_w, v_meta_tokens))
    small = _adamw_small(
        r_sm, (norm_g, conv_b, conv_norm_g, conv_norm_b, q_norm_g, k_norm_g),
        (m_norm_g, m_conv_b, m_conv_norm_g, m_conv_norm_b, m_q_norm_g, m_k_norm_g),
        (v_norm_g, v_conv_b, v_conv_norm_g, v_conv_norm_b, v_q_norm_g, v_k_norm_g))

    def assemble(big_in, w3x, cmx, s6):
        ng, cb, cng, cnb, qg, kg = s6
        return (cmx[1], ng, big_in[None], cmx[0], cb, cng, cnb, w3x[0], qg, kg, w3x[1], w3x[2])

    loss = small[0].reshape(())
    grads = assemble(g_w_in, w3[0:3], cm[0:2], small[1:7])
    deltas = assemble(d_w_in, w3[3:6], cm[2:4], small[7:13])
    new_m = assemble(nm_w_in, w3[6:9], cm[4:6], small[13:19])
    new_v = assemble(nv_w_in, w3[9:12], cm[6:8], small[19:25])
    return (loss, gx, *grads, *deltas, *new_m, *new_v)
```

```python
import functools
import math

import jax
import jax.numpy as jnp
from jax import lax
from jax.experimental import pallas as pl
from jax.experimental.pallas import tpu as pltpu

F32, BF16 = jnp.float32, jnp.bfloat16
MESH = pl.DeviceIdType.MESH

D = 1024
N_META = 16
CONV_K = 31
N_KV = 4
GQA = 4
HEAD_DIM = 64
GROUP_W = GQA * HEAD_DIM
GRID_W = 64
ROPE_FREQS = 16
ROPE_THETA = 10000.0
EPS = 1e-6
IN_DIM = 7680
KEY_PAD = 128
G_CONV, G_CZ, G_Q, G_KV, G_E = (0, 2048), (2048, 1024), (3072, 1024), (4096, 512), (4608, 3072)
N_CHIPS = 4
W_IN_SHARD = IN_DIM // N_CHIPS
ROW_SHARD = D // N_CHIPS

ADAM_LR, ADAM_B1, ADAM_B2, ADAM_EPS, ADAM_WD, ADAM_STEP = 0.001, 0.9, 0.999, 1e-08, 0.01, 10

NT_DIMS = (((1,), (1,)), ((), ()))


def _params(sem=None, vmem_mb=48):
    return pltpu.CompilerParams(dimension_semantics=sem, vmem_limit_bytes=vmem_mb << 20)


def _sds(shape, dtype):
    return pltpu.HBM(tuple(shape), dtype)


def _pin(*arrays):
    return [pltpu.with_memory_space_constraint(a, pltpu.HBM) for a in arrays]


def _sig(v):
    return jax.nn.sigmoid(v)


def _dsilu(v, s):
    return s * (1.0 + v * (1.0 - s))


def _dot(a, b):
    return jnp.dot(a, b, preferred_element_type=F32)


def _dot_nt(a, b):
    return lax.dot_general(a, b, NT_DIMS, preferred_element_type=F32)


def _qk_mats():
    i = lax.broadcasted_iota(jnp.int32, (GROUP_W, GROUP_W), 0)
    j = lax.broadcasted_iota(jnp.int32, (GROUP_W, GROUP_W), 1)
    mean = jnp.where((i >> 6) == (j >> 6), 1.0 / HEAD_DIM, 0.0).astype(BF16)
    turn = jnp.where((i == j + 16) & ((j & 16) == 0), -1.0,
                     jnp.where((i == j - 16) & ((j & 16) != 0), 1.0, 0.0)).astype(BF16)
    return mean, turn


def _apply(v, mat):
    hi = v.astype(BF16)
    lo = (v - hi.astype(F32)).astype(BF16)
    return _dot(hi, mat) + _dot(lo, mat)


def _qk_fwd(v, g, cos, sin, mats):
    mean, turn = mats
    r = lax.rsqrt(_apply(v * v, mean) + EPS)
    n = v * r * g
    return n * cos + _apply(n, turn) * sin, r


def _qk_bwd(dy, v, r, g, cos, sin, mats):
    mean, turn = mats
    dn = dy * cos - _apply(dy, turn) * sin
    dyg = dn * g
    dv = r * dyg - v * (r * r * r) * _apply(dyg * v, mean)
    return dv, dn * v * r


def _rms_bwd(dxn, v, r, g):
    dxg = dxn * g
    return r * dxg - v * (r * r * r) * jnp.mean(dxg * v, axis=-1, keepdims=True)


def _glu(a):
    return a[:, :D] * _sig(a[:, D:])


def _gather_weights(w_in_s, w3_s, conv_w_s, meta_s):
    def body(win_ref, wa_ref, wb_ref, wc_ref, cw_ref, mt_ref, win_o, cw_o, mt_o, w3b_o, w3_o, win_b, w3_b,
             send, recv, fsend, frecv, lsem, csem):
        x, y, c = _xyc()
        o = 1 - c
        me = 2 * x + y
        win_b[...] = win_ref[...].astype(BF16)
        for i, ref in enumerate((wa_ref, wb_ref, wc_ref)):
            w3_b[i] = ref[0].astype(BF16)
        cast = pltpu.make_async_copy(w3_b, w3b_o, csem.at[0])
        cast.start()
        own = pltpu.make_async_copy(w3_b, _w3_place(w3_o, me), csem.at[1])
        own.start()
        items = (
            (lambda h: win_b.at[pl.ds(h * 512, 512), :],
             lambda p, h: win_o.at[pl.ds(h * 512, 512), pl.ds(p * W_IN_SHARD, W_IN_SHARD)]),
            (lambda h: cw_ref.at[pl.ds(h * 16, 16), :],
             lambda p, h: cw_o.at[pl.ds(h * 16, 16), pl.ds(p * ROW_SHARD, ROW_SHARD)]),
            (lambda h: mt_ref.at[pl.ds(h * 8, 8), :],
             lambda p, h: mt_o.at[pl.ds(h * 8, 8), pl.ds(p * ROW_SHARD, ROW_SHARD)]),
        )
        peers = ((1 - x, y), (x, 1 - y), (1 - x, 1 - y))

        def remote(src, dst, s_sem, r_sem, to):
            return pltpu.make_async_remote_copy(src_ref=src, dst_ref=dst, send_sem=s_sem, recv_sem=r_sem,
                                                device_id=to, device_id_type=MESH)

        started = []
        for a, (half, place) in enumerate(items):
            for h in range(2):
                loc = pltpu.make_async_copy(half(h), place(me, h), lsem.at[a, h])
                loc.start()
                started.append(loc.wait)
            for k, (px, py) in enumerate(peers):
                cp = remote(half(c), place(me, c), send.at[a, k], recv.at[a, k], (px, py, c))
                cp.start()
                started.append(cp.wait_send)
        for k, (px, py) in enumerate(peers):
            for a, (half, place) in enumerate(items):
                got = place(2 * px + py, c)
                remote(got, got, send.at[a, k], recv.at[a, k], (px, py, c)).wait_recv()
                fw = remote(got, got, fsend.at[a, k], frecv.at[a, k], (x, y, o))
                fw.start()
                started.append(fw.wait_send)
        for k, (px, py) in enumerate(peers):
            for a, (half, place) in enumerate(items):
                theirs = place(2 * px + py, o)
                remote(theirs, theirs, fsend.at[a, k], frecv.at[a, k], (x, y, o)).wait_recv()
        for wait in started:
            wait()
        cast.wait()
        own.wait()

    any_spec = pl.BlockSpec(memory_space=pl.ANY)
    vmem = pl.BlockSpec(memory_space=pltpu.VMEM)
    return pl.pallas_call(
        body, name="gather_weights",
        out_shape=(_sds((D, IN_DIM), BF16), _sds((32, D), F32), _sds((N_META, D), F32),
                   _sds((3, ROW_SHARD, D), BF16), _sds((3, D, D), BF16)),
        in_specs=[vmem] * 6,
        out_specs=(any_spec,) * 5,
        scratch_shapes=[pltpu.VMEM((D, W_IN_SHARD), BF16), pltpu.VMEM((3, ROW_SHARD, D), BF16),
                        pltpu.SemaphoreType.DMA((3, 3)), pltpu.SemaphoreType.DMA((3, 3)),
                        pltpu.SemaphoreType.DMA((3, 3)), pltpu.SemaphoreType.DMA((3, 3)),
                        pltpu.SemaphoreType.DMA((3, 2)), pltpu.SemaphoreType.DMA((2,))],
        compiler_params=pltpu.CompilerParams(vmem_limit_bytes=40 << 20),
    )(w_in_s, *w3_s, conv_w_s, meta_s)


def _w3_place(ref, p):
    return ref.at[:, pl.ds(p * ROW_SHARD, ROW_SHARD), :]


def _w3_copies(w3b_ref, land_ref, send, recv):
    x, y, c = _xyc()
    me = 2 * x + y
    peers = ((1 - x, y), (x, 1 - y), (1 - x, 1 - y))
    return [pltpu.make_async_remote_copy(src_ref=w3b_ref, dst_ref=_w3_place(land_ref, me),
                                         send_sem=send.at[k], recv_sem=recv.at[k], device_id=(px, py, c),
                                         device_id_type=MESH)
            for k, (px, py) in enumerate(peers)]


def _w3_start(w3b, land):
    def body(w3b_ref, land_ref, send, recv, w3b_thru, land_thru, token):
        for cp in _w3_copies(w3b_ref, land_ref, send, recv):
            cp.start()
        token[...] = jnp.zeros_like(token)

    outs = pl.pallas_call(
        body, name="w3_start",
        out_shape=(pltpu.SemaphoreType.DMA((3,)), pltpu.SemaphoreType.DMA((3,)),
                   pltpu.HBM(w3b.shape, BF16), pltpu.HBM(land.shape, BF16), jax.ShapeDtypeStruct((8, 128), F32)),
        in_specs=[_HBM, _HBM],
        out_specs=(_SEM, _SEM, _HBM, _HBM, pl.BlockSpec(memory_space=pltpu.VMEM)),
        input_output_aliases={0: 2, 1: 3},
        compiler_params=pltpu.CompilerParams(has_side_effects=_EFFECT),
    )(*_pin(w3b, land))
    return outs[0:4], outs[4]


def _w3_wait(send, recv, w3b, land, after):
    def body(w3b_ref, land_ref, send_ref, recv_ref, after_ref, w3b_out, land_out):
        x, y, c = _xyc()
        peers = ((1 - x, y), (x, 1 - y), (1 - x, 1 - y))
        for k, (cp, (px, py)) in enumerate(zip(_w3_copies(w3b_ref, land_ref, send_ref, recv_ref), peers)):
            cp.wait_send()
            got = _w3_place(land_ref, 2 * px + py)
            pltpu.make_async_remote_copy(src_ref=got, dst_ref=got, send_sem=send_ref.at[k], recv_sem=recv_ref.at[k],
                                         device_id=(px, py, c), device_id_type=MESH).wait_recv()

    outs = pl.pallas_call(
        body, name="w3_wait",
        out_shape=(pltpu.HBM(w3b.shape, BF16), pltpu.HBM(land.shape, BF16)),
        in_specs=[_HBM, _HBM, _SEM, _SEM, pl.BlockSpec(memory_space=pl.ANY)],
        out_specs=(_HBM, _HBM),
        input_output_aliases={0: 0, 1: 1},
        compiler_params=pltpu.CompilerParams(has_side_effects=_EFFECT),
    )(w3b, land, send, recv, after)
    return outs[1]


def _meta_fwd(meta_full, norm_g, w_full):
    def body(m_ref, g_ref, wc_ref, wkv_ref, xnt_ref, pc_ref, pkv_ref):
        v = m_ref[...]
        r = lax.rsqrt(jnp.mean(v * v, axis=-1, keepdims=True) + EPS)
        xn = v * r * g_ref[...]
        xnb = xn.astype(BF16)
        pad = jnp.concatenate([xn, jnp.zeros((128 - N_META, D), F32)], axis=0)
        xnt_ref[...] = pad.T.astype(BF16)
        pc_ref[...] = _dot(xnb, wc_ref[...])
        pkv_ref[...] = _dot(xnb, wkv_ref[...])

    return pl.pallas_call(
        body, name="meta_fwd", grid=(1,),
        out_shape=(_sds((D, 128), BF16), _sds((N_META, 2048), F32),
                   _sds((N_META, 512), F32)),
        in_specs=[pl.BlockSpec((N_META, D), lambda i: (0, 0)), pl.BlockSpec((1, D), lambda i: (0, 0)),
                  pl.BlockSpec((D, 2048), lambda i: (0, 0)), pl.BlockSpec((D, 512), lambda i: (0, G_KV[0] // 512))],
        out_specs=(pl.BlockSpec((D, 128), lambda i: (0, 0)), pl.BlockSpec((N_META, 2048), lambda i: (0, 0)),
                   pl.BlockSpec((N_META, 512), lambda i: (0, 0))),
        compiler_params=_params(("arbitrary",), 32),
    )(*_pin(meta_full, norm_g, w_full, w_full))


def _in_proj(x2, norm_g, w_full, qg, cos, sin, nb, tm):
    rows = x2.shape[0]
    nt = rows // nb // tm
    groups = (G_CONV, G_CZ, G_Q, G_KV, G_E)
    scale = 1.0 / math.sqrt(HEAD_DIM)

    def body(x_ref, g_ref, w_hbm, qg_ref, cos_ref, sin_ref, *rest):
        outs, xnt_ref, q2_o, qt_o, w_vmem, sem = rest[:5], rest[5], rest[6], rest[7], rest[8], rest[9]

        @pl.when(pl.program_id(0) == 0)
        def _():
            cp = pltpu.make_async_copy(w_hbm, w_vmem, sem)
            cp.start()
            cp.wait()

        v = x_ref[...]
        r = lax.rsqrt(jnp.mean(v * v, axis=-1, keepdims=True) + EPS)
        xn = v * r * g_ref[...]
        xnb = xn.astype(BF16)
        xnt_ref[...] = xn.T.astype(BF16)
        for ref, (off, wd) in zip(outs, groups):
            for c0 in range(0, wd, 512):
                ref[:, c0:c0 + 512] = _dot(xnb, w_vmem[:, off + c0:off + c0 + 512])
        gv, cosv, sinv = qg_ref[...], cos_ref[...], sin_ref[...]
        mats = _qk_mats()
        for g in range(N_KV):
            gs = slice(GROUP_W * g, GROUP_W * (g + 1))
            qr, _ = _qk_fwd(outs[2][:, gs], gv, cosv, sinv, mats)
            q2_o[:, gs] = (qr * (scale * LOG2E)).astype(BF16)
            qt_o[gs, :] = (qr * scale).T.astype(BF16)

    rope = pl.BlockSpec((tm, GROUP_W), lambda i: (lax.rem(i, nt), 0))
    return pl.pallas_call(
        body, name="in_proj", grid=(rows // tm,),
        out_shape=tuple(_sds((rows, wd), F32) for _, wd in groups)
        + (_sds((D, rows), BF16), _sds((rows, D), BF16), _sds((D, rows), BF16)),
        in_specs=[pl.BlockSpec((tm, D), lambda i: (i, 0)), pl.BlockSpec((1, D), lambda i: (0, 0)),
                  pl.BlockSpec(memory_space=pl.ANY), pl.BlockSpec((1, GROUP_W), lambda i: (0, 0)), rope, rope],
        out_specs=tuple(pl.BlockSpec((tm, wd), lambda i: (i, 0)) for _, wd in groups)
        + (pl.BlockSpec((D, tm), lambda i: (0, i)), pl.BlockSpec((tm, D), lambda i: (i, 0)),
           pl.BlockSpec((D, tm), lambda i: (0, i))),
        scratch_shapes=[pltpu.VMEM((D, IN_DIM), BF16), pltpu.SemaphoreType.DMA],
        compiler_params=_params(("arbitrary",), 58),
    )(*_pin(x2, norm_g, w_full, qg, cos, sin))


def _halo_specs(width, tm, nt, rows):
    h16 = tm // 16
    return [pl.BlockSpec((tm, width), lambda b, i: (b * nt + i, 0)),
            pl.BlockSpec((16, width), lambda b, i: (jnp.maximum((b * nt + i) * h16 - 1, 0), 0)),
            pl.BlockSpec((16, width), lambda b, i: (jnp.minimum((b * nt + i + 1) * h16, rows // 16 - 1), 0))]


def _fill_uext(uext, cur, prev, nxt, meta, i, nt, tm):
    uext[0:16] = jnp.where(i == 0, _glu(meta[...]), _glu(prev[...]))
    uext[16:16 + tm] = _glu(cur[...])
    uext[16 + tm:32 + tm] = jnp.where(i == nt - 1, 0.0, _glu(nxt[...]))


def _shifted_copies(dst, src, n):
    for r in range(1, 8):
        dst[r, 0:n] = src[r:r + n]


def _rows32(shifted, src, start, cols):
    q8, r = divmod(start, 8)
    if r == 0:
        return src[start:start + 32, cols]
    return shifted[r, 8 * q8:8 * q8 + 32, cols]


def _conv_fwd(pconv, pm_conv, conv_w, conv_b, nb, tm):
    rows = pconv.shape[0]
    nt = rows // nb // tm

    def body(cur, prev, nxt, meta, w_ref, b_ref, o_ref, uext, ush):
        i = pl.program_id(1)
        _fill_uext(uext, cur, prev, nxt, meta, i, nt, tm)
        _shifted_copies(ush, uext, tm + 24)
        for r0 in range(0, tm, 32):
            for c0 in range(0, D, 256):
                acc = jnp.zeros((32, 256), F32) + b_ref[:, c0:c0 + 256]
                for j in range(CONV_K):
                    acc = acc + _rows32(ush, uext, r0 + j + 1, slice(c0, c0 + 256)) * w_ref[j:j + 1, c0:c0 + 256]
                o_ref[r0:r0 + 32, c0:c0 + 256] = acc

    return pl.pallas_call(
        body, name="conv_fwd", grid=(nb, nt),
        out_shape=_sds((rows, D), F32),
        in_specs=_halo_specs(2048, tm, nt, rows)
        + [pl.BlockSpec((16, 2048), lambda b, i: (0, 0)), pl.BlockSpec((32, D), lambda b, i: (0, 0)),
           pl.BlockSpec((1, D), lambda b, i: (0, 0))],
        out_specs=pl.BlockSpec((tm, D), lambda b, i: (b * nt + i, 0)),
        scratch_shapes=[pltpu.VMEM((tm + 32, D), F32), pltpu.VMEM((8, tm + 24, D), F32)],
        compiler_params=_params(("parallel", "parallel"), 40),
    )(*_pin(pconv, pconv, pconv, pm_conv, conv_w, conv_b))


def _kv_prep(pkv, pm_kv, kg, cos, sin, nb):
    rows = pkv.shape[0]
    s_len = rows // nb
    tk = min(512, s_len)
    nt = s_len // tk

    def body(kv_ref, m_ref, g_ref, cos_ref, sin_ref, k_o, v_o, k2_o, v2_o):
        i = pl.program_id(1)
        mats = _qk_mats()
        kv = kv_ref[...]
        kr, _ = _qk_fwd(kv[:, :GROUP_W], g_ref[...], cos_ref[...], sin_ref[...], mats)
        ones = _ones_cols(tk, tk)
        for h in range(N_KV):
            k_o[0, h] = kr[:, HEAD_DIM * h:HEAD_DIM * (h + 1)].astype(BF16)
            vh = kv[:, GROUP_W + HEAD_DIM * h:GROUP_W + HEAD_DIM * (h + 1)]
            v_o[0, h] = jnp.concatenate([vh, ones], axis=1).astype(BF16)

        @pl.when(i == 0)
        def _():
            kvm = m_ref[...]
            km = kvm[:, :GROUP_W]
            kn = km * lax.rsqrt(_apply(km * km, mats[0]) + EPS) * g_ref[...]
            zeros = jnp.zeros((KEY_PAD - N_META, GROUP_W), F32)
            kfull = jnp.concatenate([kn, zeros], axis=0)
            vfull = jnp.concatenate([kvm[:, GROUP_W:], zeros], axis=0)
            ones_m = _ones_cols(KEY_PAD, N_META)
            for h in range(N_KV):
                k2_o[0, h] = kfull[:, HEAD_DIM * h:HEAD_DIM * (h + 1)].astype(BF16)
                v2_o[0, h] = jnp.concatenate([vfull[:, HEAD_DIM * h:HEAD_DIM * (h + 1)], ones_m], axis=1).astype(BF16)

    return pl.pallas_call(
        body, name="kv_prep", grid=(nb, nt),
        out_shape=(_sds((nb, N_KV, s_len, HEAD_DIM), BF16), _sds((nb, N_KV, s_len, 2 * HEAD_DIM), BF16),
                   _sds((nb, N_KV, KEY_PAD, HEAD_DIM), BF16), _sds((nb, N_KV, KEY_PAD, 2 * HEAD_DIM), BF16)),
        in_specs=[pl.BlockSpec((tk, 512), lambda b, i: (b * nt + i, 0)),
                  pl.BlockSpec((N_META, 512), lambda b, i: (0, 0)), pl.BlockSpec((1, GROUP_W), lambda b, i: (0, 0)),
                  pl.BlockSpec((tk, GROUP_W), lambda b, i: (i, 0)),
                  pl.BlockSpec((tk, GROUP_W), lambda b, i: (i, 0))],
        out_specs=(pl.BlockSpec((1, N_KV, tk, HEAD_DIM), lambda b, i: (b, 0, i, 0)),
                   pl.BlockSpec((1, N_KV, tk, 2 * HEAD_DIM), lambda b, i: (b, 0, i, 0)),
                   pl.BlockSpec((1, N_KV, KEY_PAD, HEAD_DIM), lambda b, i: (b, 0, 0, 0)),
                   pl.BlockSpec((1, N_KV, KEY_PAD, 2 * HEAD_DIM), lambda b, i: (b, 0, 0, 0))),
        compiler_params=_params(("parallel", "arbitrary"), 40),
    )(*_pin(pkv, pm_kv, kg, cos, sin))


def _ones_cols(rows, valid):
    r = lax.broadcasted_iota(jnp.int32, (rows, HEAD_DIM), 0)
    col = lax.broadcasted_iota(jnp.int32, (rows, HEAD_DIM), 1)
    return jnp.where((col < 2) & (r < valid), 1.0, 0.0).astype(F32)


def _tail_bias():
    col = lax.broadcasted_iota(jnp.int32, (1, KEY_PAD), 1)
    return jnp.where(col < N_META, 0.0, -1e30).astype(F32)


LOG2E = 1.4426950408889634


def _kv_specs(s_len):
    return [pl.BlockSpec((1, 1, s_len, HEAD_DIM), lambda b, g, i: (b, g, 0, 0)),
            pl.BlockSpec((1, 1, s_len, 2 * HEAD_DIM), lambda b, g, i: (b, g, 0, 0)),
            pl.BlockSpec((1, 1, KEY_PAD, HEAD_DIM), lambda b, g, i: (b, g, 0, 0)),
            pl.BlockSpec((1, 1, KEY_PAD, 2 * HEAD_DIM), lambda b, g, i: (b, g, 0, 0))]


def _attn_fwd(q2, kv4, nb, tq):
    rows = q2.shape[0]
    s_len = rows // nb
    nq = s_len // tq

    def body(q_ref, k1_ref, v1_ref, k2_ref, v2_ref, o_ref, lse_ref):
        qs = q_ref[...]
        k1, k2, v1, v2 = k1_ref[0, 0], k2_ref[0, 0], v1_ref[0, 0], v2_ref[0, 0]
        bias = _tail_bias()
        outs, lses = [], []

        def scores(h):
            qh = qs[:, HEAD_DIM * h:HEAD_DIM * (h + 1)]
            return _dot_nt(qh, k1), _dot_nt(qh, k2) + bias

        ahead = scores(0)
        for h in range(GQA):
            s1, s2 = ahead
            if h + 1 < GQA:
                ahead = scores(h + 1)
            m = jnp.maximum(jnp.max(s1, axis=-1, keepdims=True), jnp.max(s2, axis=-1, keepdims=True))
            oe = _dot(jnp.exp2(s1 - m).astype(BF16), v1) + _dot(jnp.exp2(s2 - m).astype(BF16), v2)
            l = oe[:, HEAD_DIM:HEAD_DIM + 1]
            outs.append(oe[:, :HEAD_DIM] / l)
            lses.append(m + jnp.log2(l))
        o_ref[...] = jnp.concatenate(outs, axis=1)
        lse_ref[0, 0] = jnp.concatenate(lses, axis=1)

    return pl.pallas_call(
        body, name="attn_fwd", grid=(nb, N_KV, nq),
        out_shape=(_sds((rows, D), F32), _sds((nb, N_KV, s_len, GQA), F32)),
        in_specs=[pl.BlockSpec((tq, GROUP_W), lambda b, g, i: (b * nq + i, g))] + _kv_specs(s_len),
        out_specs=(pl.BlockSpec((tq, GROUP_W), lambda b, g, i: (b * nq + i, g)),
                   pl.BlockSpec((1, 1, tq, GQA), lambda b, g, i: (b, g, i, 0))),
        compiler_params=_params(("parallel", "parallel", "parallel"), 48),
    )(*_pin(q2, *kv4))


def _mid(x2, t2, c0, cz, o, e, w3, cn_g, cn_b, tm):
    rows = x2.shape[0]

    def body(x_ref, t_ref, c0_ref, cz_ref, o_ref, e_ref, w_ref, g_ref, b_ref,
             dy_o, mt_o, c3t_o, o2t_o, dyc_o, dya_o, do_o, dc0_o, dcz_o, de_o, sums_o):
        wco, wao, wo = w_ref[0], w_ref[1], w_ref[2]
        cn_g_v = g_ref[...]
        c0v = c0_ref[...]
        xc = c0v - jnp.mean(c0v, axis=-1, keepdims=True)
        rstd = lax.rsqrt(jnp.mean(xc * xc, axis=-1, keepdims=True) + EPS)
        n = xc * rstd
        c1 = n * cn_g_v + b_ref[...]
        s1 = _sig(c1)
        c2 = c1 * s1
        czv = cz_ref[...]
        sz = _sig(czv)
        gz = czv * sz
        c3 = c2 * gz
        yc = _dot(c3.astype(BF16), wco)
        az, gc, ga = e_ref[:, :D], e_ref[:, D:2 * D], e_ref[:, 2 * D:]
        saz = _sig(az)
        gaz = az * saz
        ov = o_ref[...]
        o2 = ov * gaz
        ya = _dot(o2.astype(BF16), wao)
        sc, sa = _sig(gc), _sig(ga)
        merged = sc * yc + sa * ya
        out = _dot(merged.astype(BF16), wo)
        err = x_ref[...] + out - t_ref[...]
        dy = err * (1.0 / D)
        dy_o[...] = dy
        dm = _dot_nt(dy.astype(BF16), wo)
        dyc = dm * sc
        dya = dm * sa
        dycb, dyab = dyc.astype(BF16), dya.astype(BF16)
        dyc_o[...] = dycb
        dya_o[...] = dyab
        de_o[:, D:2 * D] = (dyc * yc * (1.0 - sc)).astype(BF16)
        de_o[:, 2 * D:] = (dya * ya * (1.0 - sa)).astype(BF16)
        dc3 = _dot_nt(dycb, wco)
        do2 = _dot_nt(dyab, wao)
        do_o[...] = do2 * gaz
        de_o[:, :D] = (do2 * ov * _dsilu(az, saz)).astype(BF16)
        dcz_o[...] = (dc3 * c2 * _dsilu(czv, sz)).astype(BF16)
        dc1 = dc3 * gz * _dsilu(c1, s1)
        dn = dc1 * cn_g_v
        dc0 = rstd * (dn - jnp.mean(dn, axis=-1, keepdims=True) - n * jnp.mean(dn * n, axis=-1, keepdims=True))
        dc0_o[...] = dc0
        mt_o[...] = merged.T.astype(BF16)
        c3t_o[...] = c3.T.astype(BF16)
        o2t_o[...] = o2.T.astype(BF16)

        @pl.when(pl.program_id(0) == 0)
        def _():
            sums_o[...] = jnp.zeros_like(sums_o)

        sums_o[0:1, :] += jnp.sum(dc1 * n, axis=0, keepdims=True)
        sums_o[1:2, :] += jnp.sum(dc1, axis=0, keepdims=True)
        sums_o[2:3, :] += jnp.sum(dc0, axis=0, keepdims=True)
        sums_o[3:4, :] += jnp.sum(err * err, axis=0, keepdims=True)

    row = lambda wd: pl.BlockSpec((tm, wd), lambda i: (i, 0))
    col = pl.BlockSpec((D, tm), lambda i: (0, i))
    vec = pl.BlockSpec((1, D), lambda i: (0, 0))
    f32o = lambda wd: _sds((rows, wd), F32)
    b16o = lambda wd: _sds((rows, wd), BF16)
    tpo = _sds((D, rows), BF16)
    return pl.pallas_call(
        body, name="mid", grid=(rows // tm,),
        out_shape=(f32o(D), tpo, tpo, tpo, b16o(D), b16o(D), f32o(D), f32o(D), b16o(D), b16o(3 * D),
                   _sds((8, D), F32)),
        in_specs=[row(D), row(D), row(D), row(D), row(D), row(3 * D),
                  pl.BlockSpec((3, D, D), lambda i: (0, 0, 0)), vec, vec],
        out_specs=(row(D), col, col, col, row(D), row(D), row(D), row(D), row(D), row(3 * D),
                   pl.BlockSpec((8, D), lambda i: (0, 0))),
        compiler_params=_params(("arbitrary",), 60),
    )(*_pin(x2, t2, c0, cz, o, e, w3, cn_g, cn_b))


def _do_prep(d_o, o, tm):
    rows = d_o.shape[0]

    ch = jnp.arange(D, dtype=jnp.int32)
    head, col2 = jnp.arange(128, dtype=jnp.int32), jnp.arange(2 * D, dtype=jnp.int32)
    ind = ((ch // HEAD_DIM)[:, None] == head[None, :]).astype(BF16)
    spread = (col2[None, :] == (128 * (ch // HEAD_DIM) + ch % HEAD_DIM)[:, None]).astype(BF16)
    place = jnp.concatenate([-(col2[None, :] == (128 * head + 64)[:, None]).astype(F32),
                             -(col2[None, :] == (128 * head + 65)[:, None]).astype(F32)], axis=0).astype(BF16)

    def body(do_ref, o_ref, ind_ref, spread_ref, place_ref, doe_o, dot_o):
        dov = do_ref[...]
        delta = _apply(dov * o_ref[...], ind_ref[...])
        d_hi = delta.astype(BF16)
        d_lo = (delta - d_hi.astype(F32)).astype(BF16)
        tails = _dot(jnp.concatenate([d_hi, d_lo], axis=1), place_ref[...])
        doe_o[...] = (_dot(dov.astype(BF16), spread_ref[...]) + tails).astype(BF16)
        dot_o[...] = dov.T.astype(BF16)

    row = pl.BlockSpec((tm, D), lambda i: (i, 0))
    const = lambda a: pl.BlockSpec(a.shape, lambda i: (0, 0))
    return pl.pallas_call(
        body, name="do_prep", grid=(rows // tm,),
        out_shape=(_sds((rows, 2 * D), BF16), _sds((D, rows), BF16)),
        in_specs=[row, row, const(ind), const(spread), const(place)],
        out_specs=(pl.BlockSpec((tm, 2 * D), lambda i: (i, 0)), pl.BlockSpec((D, tm), lambda i: (0, i))),
        compiler_params=_params(("parallel",), 40),
    )(*_pin(d_o, o, ind, spread, place))


def _q_post(dqr, pq, qg, cos, sin, nb, tm):
    rows = pq.shape[0]
    nt = rows // nb // tm

    def body(dq_ref, q_ref, g_ref, cos_ref, sin_ref, dq_o, dg_o):
        @pl.when((pl.program_id(0) == 0) & (pl.program_id(1) == 0))
        def _():
            dg_o[...] = jnp.zeros_like(dg_o)

        gv, cosv, sinv = g_ref[...], cos_ref[...], sin_ref[...]
        acc = jnp.zeros((1, GROUP_W), F32)
        mats = _qk_mats()
        for g in range(N_KV):
            gs = slice(GROUP_W * g, GROUP_W * (g + 1))
            qv = q_ref[:, gs]
            r = lax.rsqrt(_apply(qv * qv, mats[0]) + EPS)
            dq, dgr = _qk_bwd(dq_ref[:, gs], qv, r, gv, cosv, sinv, mats)
            dq_o[:, gs] = dq.astype(BF16)
            acc = acc + jnp.sum(dgr, axis=0, keepdims=True)
        dg_o[...] += acc

    row = pl.BlockSpec((tm, D), lambda b, i: (b * nt + i, 0))
    rope = pl.BlockSpec((tm, GROUP_W), lambda b, i: (i, 0))
    vec = pl.BlockSpec((1, GROUP_W), lambda b, i: (0, 0))
    return pl.pallas_call(
        body, name="q_post", grid=(nb, nt),
        out_shape=(_sds((rows, D), BF16), _sds((1, GROUP_W), F32)),
        in_specs=[row, row, vec, rope, rope], out_specs=(row, vec),
        compiler_params=_params(("arbitrary", "arbitrary"), 32),
    )(*_pin(dqr, pq, qg, cos, sin))


def _attn_bwd(q2, qst, kv4, doe, dot_, lse, nb, tq):
    rows = q2.shape[0]
    s_len = rows // nb
    nq = s_len // tq
    scale = 1.0 / math.sqrt(HEAD_DIM)

    def body(q_ref, qt_ref, k1_ref, v1_ref, k2_ref, v2_ref, doe_ref, dot_ref, lse_ref,
             dq_o, dkt_o, dvt_o, dkt2_o, dvt2_o):
        i = pl.program_id(2)
        lse = lse_ref[0, 0]
        k1, k2, v1, v2 = k1_ref[0, 0], k2_ref[0, 0], v1_ref[0, 0], v2_ref[0, 0]
        bias = _tail_bias()
        dkt1, dkt2 = jnp.zeros((HEAD_DIM, s_len), F32), jnp.zeros((HEAD_DIM, KEY_PAD), F32)
        dvt1, dvt2 = jnp.zeros((HEAD_DIM, s_len), F32), jnp.zeros((HEAD_DIM, KEY_PAD), F32)

        def products(h):
            qh = q_ref[:, HEAD_DIM * h:HEAD_DIM * (h + 1)]
            dh = doe_ref[:, 2 * HEAD_DIM * h:2 * HEAD_DIM * (h + 1)]
            return _dot_nt(qh, k1), _dot_nt(qh, k2) + bias, _dot_nt(dh, v1), _dot_nt(dh, v2)

        ahead = products(0)
        for h in range(GQA):
            hs = slice(HEAD_DIM * h, HEAD_DIM * (h + 1))
            s1, s2, dp1, dp2 = ahead
            if h + 1 < GQA:
                ahead = products(h + 1)
            lse_h = lse[:, h:h + 1]
            p1 = jnp.exp2(s1 - lse_h)
            p2 = jnp.exp2(s2 - lse_h)
            ds1 = (p1 * dp1).astype(BF16)
            ds2 = (p2 * dp2).astype(BF16)
            dq_o[:, hs] = (_dot(ds1, k1) + _dot(ds2, k2)) * scale
            dkt1 = dkt1 + _dot(qt_ref[hs, :], ds1)
            dkt2 = dkt2 + _dot(qt_ref[hs, :], ds2)
            dvt1 = dvt1 + _dot(dot_ref[hs, :], p1.astype(BF16))
            dvt2 = dvt2 + _dot(dot_ref[hs, :], p2.astype(BF16))

        @pl.when(i == 0)
        def _():
            dkt_o[0, 0], dkt2_o[0, 0], dvt_o[0, 0], dvt2_o[0, 0] = dkt1, dkt2, dvt1, dvt2

        @pl.when(i > 0)
        def _():
            dkt_o[0, 0] += dkt1
            dkt2_o[0, 0] += dkt2
            dvt_o[0, 0] += dvt1
            dvt2_o[0, 0] += dvt2

    qspec = pl.BlockSpec((tq, GROUP_W), lambda b, g, i: (b * nq + i, g))
    qtspec = pl.BlockSpec((GROUP_W, tq), lambda b, g, i: (g, b * nq + i))
    tspec = pl.BlockSpec((1, 1, HEAD_DIM, s_len), lambda b, g, i: (b, g, 0, 0))
    t2spec = pl.BlockSpec((1, 1, HEAD_DIM, KEY_PAD), lambda b, g, i: (b, g, 0, 0))
    tshape = _sds((nb, N_KV, HEAD_DIM, s_len), F32)
    t2shape = _sds((nb, N_KV, HEAD_DIM, KEY_PAD), F32)
    return pl.pallas_call(
        body, name="attn_bwd", grid=(nb, N_KV, nq),
        out_shape=(_sds((rows, D), F32), tshape, tshape, t2shape, t2shape),
        in_specs=[qspec, qtspec] + _kv_specs(s_len)
        + [pl.BlockSpec((tq, 2 * GROUP_W), lambda b, g, i: (b * nq + i, g)), qtspec,
           pl.BlockSpec((1, 1, tq, GQA), lambda b, g, i: (b, g, i, 0))],
        out_specs=(qspec, tspec, tspec, t2spec, t2spec),
        compiler_params=_params(("parallel", "parallel", "arbitrary"), 56),
    )(*_pin(q2, qst, *kv4, doe, dot_, lse))


def _kv_bwd(dkt, dvt, dkt2, dvt2, pkv, pm_kv, kg, cos, sin, nb):
    rows = pkv.shape[0]
    s_len = rows // nb
    tk = min(512, s_len)
    nt = s_len // tk

    def body(dk_ref, dv_ref, dk2_ref, dv2_ref, kv_ref, m_ref, g_ref, cos_ref, sin_ref, d_o, dm_o, dg_o):
        b, i = pl.program_id(0), pl.program_id(1)
        gv = g_ref[...]
        mats = _qk_mats()

        @pl.when((b == 0) & (i == 0))
        def _():
            dg_o[...] = jnp.zeros_like(dg_o)

        kx = kv_ref[:, :GROUP_W]
        r = lax.rsqrt(_apply(kx * kx, mats[0]) + EPS)
        dk, dgr = _qk_bwd(dk_ref[0].T, kx, r, gv, cos_ref[...], sin_ref[...], mats)
        d_o[:, :GROUP_W] = dk.astype(BF16)
        d_o[:, GROUP_W:] = dv_ref[0].T.astype(BF16)
        dg_o[...] += jnp.sum(dgr, axis=0, keepdims=True)

        @pl.when(i == 0)
        def _():
            kxm = m_ref[:, :GROUP_W]
            rm = lax.rsqrt(_apply(kxm * kxm, mats[0]) + EPS)
            dn = dk2_ref[0].T[0:N_META]
            dyg = dn * gv
            dm_o[0, :, :GROUP_W] = rm * dyg - kxm * (rm * rm * rm) * _apply(dyg * kxm, mats[0])
            dm_o[0, :, GROUP_W:] = dv2_ref[0].T[0:N_META]
            dg_o[...] += jnp.sum(dn * kxm * rm, axis=0, keepdims=True)

    tspec = pl.BlockSpec((1, GROUP_W, tk), lambda b, i: (b, 0, i))
    t2spec = pl.BlockSpec((1, GROUP_W, KEY_PAD), lambda b, i: (b, 0, 0))
    rope = pl.BlockSpec((tk, GROUP_W), lambda b, i: (i, 0))
    return pl.pallas_call(
        body, name="kv_bwd", grid=(nb, nt),
        out_shape=(_sds((rows, 512), BF16), _sds((nb, N_META, 512), F32),
                   _sds((1, GROUP_W), F32)),
        in_specs=[tspec, tspec, t2spec, t2spec, pl.BlockSpec((tk, 512), lambda b, i: (b * nt + i, 0)),
                  pl.BlockSpec((N_META, 512), lambda b, i: (0, 0)), pl.BlockSpec((1, GROUP_W), lambda b, i: (0, 0)),
                  rope, rope],
        out_specs=(pl.BlockSpec((tk, 512), lambda b, i: (b * nt + i, 0)),
                   pl.BlockSpec((1, N_META, 512), lambda b, i: (b, 0, 0)),
                   pl.BlockSpec((1, GROUP_W), lambda b, i: (0, 0))),
        compiler_params=_params(("arbitrary", "arbitrary"), 40),
    )(*_pin(dkt, dvt, dkt2, dvt2, pkv, pm_kv, kg, cos, sin))


def _conv_bwd(dc0, pconv, pm_conv, conv_w, nb, tm):
    rows = pconv.shape[0]
    nt = rows // nb // tm

    def body(dcur, dprev, dnxt, cur, meta, w_ref, da_o, dam_o, gw_o, ucur, dext, dsh):
        b, i = pl.program_id(0), pl.program_id(1)
        ucur[...] = _glu(cur[...])
        dext[0:16] = jnp.zeros((16, D), F32)
        dext[16:32] = jnp.where(i == 0, 0.0, dprev[...])
        dext[32:32 + tm] = dcur[...]
        dext[32 + tm:48 + tm] = jnp.where(i == nt - 1, 0.0, dnxt[...])
        _shifted_copies(dsh, dext, tm + 40)

        @pl.when((b == 0) & (i == 0))
        def _():
            gw_o[...] = jnp.zeros_like(gw_o)

        for c0 in range(0, D, 256):
            cs = slice(c0, c0 + 256)
            for r0 in range(0, tm, 32):
                acc = jnp.zeros((32, 256), F32)
                for j in range(CONV_K):
                    acc = acc + _rows32(dsh, dext, r0 + 47 - j, cs) * w_ref[j:j + 1, cs]
                cv = cur[r0:r0 + 32, c0:c0 + 256]
                sg = _sig(cur[r0:r0 + 32, D + c0:D + c0 + 256])
                da_o[r0:r0 + 32, cs] = (acc * sg).astype(BF16)
                da_o[r0:r0 + 32, D + c0:D + c0 + 256] = (acc * cv * sg * (1.0 - sg)).astype(BF16)
            for j in range(CONV_K):
                acc = jnp.zeros((32, 256), F32)
                for r0 in range(0, tm, 32):
                    acc = acc + _rows32(dsh, dext, r0 + 47 - j, cs) * ucur[r0:r0 + 32, cs]
                gw_o[j:j + 1, cs] += jnp.sum(acc, axis=0, keepdims=True)

        @pl.when(i == 0)
        def _():
            for c0 in range(0, D, 256):
                cs = slice(c0, c0 + 256)
                cv = meta[:, c0:c0 + 256]
                sg = _sig(meta[:, D + c0:D + c0 + 256])
                um = cv * sg
                acc = jnp.zeros((16, 256), F32)
                for j in range(CONV_K):
                    d = dext[31 - j:47 - j, cs]
                    acc = acc + d * w_ref[j:j + 1, cs]
                    gw_o[j:j + 1, cs] += jnp.sum(d * um, axis=0, keepdims=True)
                dam_o[0, :, cs] = acc * sg
                dam_o[0, :, D + c0:D + c0 + 256] = acc * cv * sg * (1.0 - sg)

    return pl.pallas_call(
        body, name="conv_bwd", grid=(nb, nt),
        out_shape=(_sds((rows, 2048), BF16), _sds((nb, N_META, 2048), F32),
                   _sds((32, D), F32)),
        in_specs=_halo_specs(D, tm, nt, rows)
        + [pl.BlockSpec((tm, 2048), lambda b, i: (b * nt + i, 0)),
           pl.BlockSpec((16, 2048), lambda b, i: (0, 0)), pl.BlockSpec((32, D), lambda b, i: (0, 0))],
        out_specs=(pl.BlockSpec((tm, 2048), lambda b, i: (b * nt + i, 0)),
                   pl.BlockSpec((1, N_META, 2048), lambda b, i: (b, 0, 0)),
                   pl.BlockSpec((32, D), lambda b, i: (0, 0))),
        scratch_shapes=[pltpu.VMEM((tm, D), F32), pltpu.VMEM((tm + 48, D), F32), pltpu.VMEM((8, tm + 40, D), F32)],
        compiler_params=_params(("arbitrary", "arbitrary"), 48),
    )(*_pin(dc0, dc0, dc0, pconv, pm_conv, conv_w))


def _meta_bwd(dam, ddm, w_full, meta_full, norm_g):
    nb = dam.shape[0]

    def body(a_ref, d_ref, wc_ref, wkv_ref, m_ref, g_ref, gm_o, dg_o):
        a, d = a_ref[0], d_ref[0]
        for b in range(1, nb):
            a = a + a_ref[b]
            d = d + d_ref[b]
        dxn = _dot_nt(a.astype(BF16), wc_ref[...]) + _dot_nt(d.astype(BF16), wkv_ref[...])
        v = m_ref[...]
        r = lax.rsqrt(jnp.mean(v * v, axis=-1, keepdims=True) + EPS)
        gm_o[...] = _rms_bwd(dxn, v, r, g_ref[...])
        dg_o[...] = jnp.sum(dxn * v * r, axis=0, keepdims=True)

    return pl.pallas_call(
        body, name="meta_bwd", grid=(1,),
        out_shape=(_sds((N_META, D), F32), _sds((1, D), F32)),
        in_specs=[pl.BlockSpec((nb, N_META, 2048), lambda i: (0, 0, 0)), pl.BlockSpec((nb, N_META, 512), lambda i: (0, 0, 0)),
                  pl.BlockSpec((D, 2048), lambda i: (0, 0)), pl.BlockSpec((D, 512), lambda i: (0, G_KV[0] // 512)),
                  pl.BlockSpec((N_META, D), lambda i: (0, 0)), pl.BlockSpec((1, D), lambda i: (0, 0))],
        out_specs=(pl.BlockSpec((N_META, D), lambda i: (0, 0)), pl.BlockSpec((1, D), lambda i: (0, 0))),
        compiler_params=_params(("arbitrary",), 32),
    )(*_pin(dam, ddm, w_full, w_full, meta_full, norm_g))


def _dxn(d_groups, w_full, x2, dy, norm_g, dg_init, tm):
    rows = x2.shape[0]
    groups = (G_CONV, G_CZ, G_Q, G_KV, G_E)

    def body(da, db, dq, dd, de, w_hbm, x_ref, dy_ref, g_ref, gi_ref, gx_o, dg_o, w_vmem, sem):
        @pl.when(pl.program_id(0) == 0)
        def _():
            cp = pltpu.make_async_copy(w_hbm, w_vmem, sem)
            cp.start()
            cp.wait()
            dg_o[...] = gi_ref[...]

        dxn = jnp.zeros((tm, D), F32)
        for ref, (off, wd) in zip((da, db, dq, dd, de), groups):
            for c0 in range(0, wd, 512):
                dxn = dxn + _dot_nt(ref[:, c0:c0 + 512], w_vmem[:, off + c0:off + c0 + 512])
        v = x_ref[...]
        r = lax.rsqrt(jnp.mean(v * v, axis=-1, keepdims=True) + EPS)
        gx_o[...] = dy_ref[...] + _rms_bwd(dxn, v, r, g_ref[...])
        dg_o[...] += jnp.sum(dxn * v * r, axis=0, keepdims=True)

    row = lambda wd: pl.BlockSpec((tm, wd), lambda i: (i, 0))
    vec = pl.BlockSpec((1, D), lambda i: (0, 0))
    return pl.pallas_call(
        body, name="dxn", grid=(rows // tm,),
        out_shape=(_sds((rows, D), F32), _sds((1, D), F32)),
        in_specs=[row(wd) for _, wd in groups] + [pl.BlockSpec(memory_space=pl.ANY), row(D), row(D), vec, vec],
        out_specs=(row(D), vec),
        scratch_shapes=[pltpu.VMEM((D, IN_DIM), BF16), pltpu.SemaphoreType.DMA],
        compiler_params=_params(("arbitrary",), 56),
    )(*_pin(*d_groups, w_full, x2, dy, norm_g, dg_init))


def _wgrad(at, b, buf, slot, col_off, name, meta=None):
    rows, n = b.shape
    tn = next(t for t in (1536, 1024, 512) if n % t == 0 and col_off % t == 0)
    tk = min(2048, rows)
    nk = rows // tk
    j0 = col_off // tn

    def body(*refs):
        if meta is None:
            at_ref, b_ref, _, o_ref = refs
        else:
            at_ref, b_ref, xm_ref, dm_ref, _, o_ref = refs
        k = pl.program_id(1)

        @pl.when(k == 0)
        def _():
            if meta is None:
                o_ref[0] = jnp.zeros((D, tn), F32)
            else:
                dm = dm_ref[0]
                for e in range(1, dm_ref.shape[0]):
                    dm = dm + dm_ref[e]
                dm = jnp.concatenate([dm, jnp.zeros((128 - N_META, tn), F32)], axis=0)
                o_ref[0] = _dot(xm_ref[...], dm.astype(BF16))

        o_ref[0] += _dot(at_ref[...], b_ref[...].astype(BF16))

    in_specs = [pl.BlockSpec((D, tk), lambda j, k: (0, k)), pl.BlockSpec((tk, tn), lambda j, k: (k, j))]
    args = [at, b]
    if meta is not None:
        xmt, dm = meta
        in_specs += [pl.BlockSpec((D, 128), lambda j, k: (0, 0)),
                     pl.BlockSpec((dm.shape[0], N_META, tn), lambda j, k: (0, 0, j))]
        args += [xmt, dm]
    in_specs.append(pl.BlockSpec(memory_space=pl.ANY))
    args.append(buf)
    return pl.pallas_call(
        body, name=name, grid=(n // tn, nk),
        out_shape=_sds(buf.shape, F32),
        in_specs=in_specs,
        out_specs=pl.BlockSpec((1, D, tn), lambda j, k: (slot, 0, j0 + j)),
        input_output_aliases={len(args) - 1: 0},
        compiler_params=_params(("parallel", "arbitrary"), 48),
    )(*_pin(*args))


def _rope_tables(s_len):
    pos = jnp.arange(s_len, dtype=jnp.int32)
    row_ids = (pos // GRID_W).astype(F32)
    col_ids = (pos % GRID_W).astype(F32)
    inv_freq = ROPE_THETA ** (-jnp.arange(ROPE_FREQS, dtype=F32) / ROPE_FREQS)
    a_row = row_ids[:, None] * inv_freq[None, :]
    a_col = col_ids[:, None] * inv_freq[None, :]
    ang = jnp.concatenate([a_row, a_row, a_col, a_col], axis=-1)
    return jnp.tile(jnp.cos(ang), (1, GQA)), jnp.tile(jnp.sin(ang), (1, GQA))


def _local_step(x, loss_target, norm_g, conv_b, cn_g, cn_b, q_g, k_g, w_full, w3_full, conv_w_full, meta_full,
                reduce_start=None):
    nb, s_len, _ = x.shape
    rows = nb * s_len
    x2 = x.reshape(rows, D)
    t2 = loss_target.reshape(rows, D)
    cos, sin = _rope_tables(s_len)
    qg = jnp.tile(q_g, (1, GQA))
    kg = jnp.tile(k_g, (1, N_KV))

    xnmt, pm_conv, pm_kv = _meta_fwd(meta_full, norm_g, w_full)
    pconv, pcz, pq, pkv, pe, xnt, q2, qst = _in_proj(x2, norm_g, w_full, qg, cos, sin, nb, 256)
    c0 = _conv_fwd(pconv, pm_conv, conv_w_full, conv_b, nb, 256)
    tq = min(512, s_len)
    kv4 = _kv_prep(pkv, pm_kv, kg, cos, sin, nb)
    o, lse = _attn_fwd(q2, kv4, nb, tq)
    if callable(w3_full):
        w3_full = w3_full(o)
    dy, mt, c3t, o2t, dyc, dya, d_o, dc0, dcz, de, sums = _mid(x2, t2, c0, pcz, o, pe, w3_full, cn_g, cn_b, 256)
    doe, dot_ = _do_prep(d_o, o, 256)
    dqr, dkt, dvt, dkt2, dvt2 = _attn_bwd(q2, qst, kv4, doe, dot_, lse, nb, tq)
    dq, dqg = _q_post(dqr, pq, qg, cos, sin, nb, 256)
    dd, ddm, dkg = _kv_bwd(dkt.reshape(nb, GROUP_W, s_len), dvt.reshape(nb, GROUP_W, s_len),
                           dkt2.reshape(nb, GROUP_W, KEY_PAD), dvt2.reshape(nb, GROUP_W, KEY_PAD),
                           pkv, pm_kv, kg, cos, sin, nb)
    da, dam, gcw = _conv_bwd(dc0, pconv, pm_conv, conv_w_full, nb, 256)
    gmeta, dng_m = _meta_bwd(dam, ddm, w_full, meta_full, norm_g)

    gw3 = lax.empty((3, D, D), F32)
    gw3 = _wgrad(c3t, dyc, gw3, 0, 0, "wgrad_conv_out")
    gw3 = _wgrad(o2t, dya, gw3, 1, 0, "wgrad_attn_out")
    gw3 = _wgrad(mt, dy, gw3, 2, 0, "wgrad_out")
    gwin = lax.empty((1, D, IN_DIM), F32)
    gwin = _wgrad(xnt, da, gwin, 0, G_CONV[0], "wgrad_in_conv", meta=(xnmt, dam))
    gwin = _wgrad(xnt, dcz, gwin, 0, G_CZ[0], "wgrad_in_cz")
    gwin = _wgrad(xnt, dq, gwin, 0, G_Q[0], "wgrad_in_q")
    gwin = _wgrad(xnt, dd, gwin, 0, G_KV[0], "wgrad_in_kv", meta=(xnmt, ddm))
    gwin = _wgrad(xnt, de, gwin, 0, G_E[0], "wgrad_in_e")

    pending = None
    if reduce_start is not None:
        token, pending = reduce_start(gwin, gw3, gcw, gmeta)
        dng_m = dng_m + token[0:1, 0:1]
    gx, dng = _dxn((da, dcz, dq, dd, de), w_full, x2, dy, norm_g, dng_m, 512)

    zeros = jnp.zeros((1, D - 2 * GROUP_W), F32)
    smalls = jnp.concatenate([dng, sums[2:3], sums[0:1], sums[1:2], jnp.concatenate([dqg, dkg, zeros], axis=1),
                              sums[3:4], jnp.zeros((2, D), F32)], axis=0)
    return gx.reshape(nb, s_len, D), gwin, gw3, gcw, gmeta, smalls, pending


def _xyc():
    return lax.axis_index("x"), lax.axis_index("y"), lax.axis_index("c")


def _reduce_sibling(gwin, gw3v, gcm):
    def body(gwin_ref, gw3_ref, gcm_ref, r_win, r_w3, r_cm, send, recv):
        x, y, c = _xyc()
        o = 1 - c
        half = D // 2
        outs = ((gwin_ref.at[pl.ds(o * half, half), :], r_win), (gw3_ref.at[:, :, o], r_w3), (gcm_ref.at[o], r_cm))
        cps = []
        for a, (src, dst) in enumerate(outs):
            cp = pltpu.make_async_remote_copy(src_ref=src, dst_ref=dst, send_sem=send.at[a], recv_sem=recv.at[a],
                                              device_id=(x, y, o), device_id_type=MESH)
            cp.start()
            cps.append(cp)
        for cp in cps:
            cp.wait()

    any_spec = pl.BlockSpec(memory_space=pl.ANY)
    return pl.pallas_call(
        body, name="reduce_sibling",
        out_shape=(_sds((D // 2, IN_DIM), F32), _sds((3, 4, 128, D), F32),
                   _sds((24, D), F32)),
        in_specs=[any_spec] * 3, out_specs=(any_spec,) * 3,
        scratch_shapes=[pltpu.SemaphoreType.DMA((3,)), pltpu.SemaphoreType.DMA((3,))],
    )(*_pin(gwin, gw3v, gcm))


def _add_sibling(gwin, gw3v, gcm, r_win, r_w3, r_cm):
    c = lax.axis_index("c").astype(jnp.int32).reshape(1)
    half = D // 2
    tr = 64

    def body1(c_ref, a_ref, b_ref, o_ref):
        o_ref[...] = (a_ref[...] + b_ref[...]).astype(BF16)

    cs_win = pl.pallas_call(
        body1, name="add_sibling_w_in", out_shape=_sds((half, IN_DIM), BF16),
        grid_spec=pltpu.PrefetchScalarGridSpec(
            num_scalar_prefetch=1, grid=(half // tr,),
            in_specs=[pl.BlockSpec((tr, IN_DIM), lambda i, c_ref: (c_ref[0] * (half // tr) + i, 0)),
                      pl.BlockSpec((tr, IN_DIM), lambda i, c_ref: (i, 0))],
            out_specs=pl.BlockSpec((tr, IN_DIM), lambda i, c_ref: (i, 0))),
        compiler_params=_params(("parallel",), 32),
    )(c, *_pin(gwin, r_win))

    def body2(c_ref, a_ref, b_ref, o_ref):
        o_ref[0, 0] = (a_ref[0, 0, 0] + b_ref[0, 0]).astype(BF16)

    cs_w3 = pl.pallas_call(
        body2, name="add_sibling_w3", out_shape=_sds((3, 4, 128, D), BF16),
        grid_spec=pltpu.PrefetchScalarGridSpec(
            num_scalar_prefetch=1, grid=(3, 4),
            in_specs=[pl.BlockSpec((1, 1, 1, 128, D), lambda w, s, c_ref: (w, s, c_ref[0], 0, 0)),
                      pl.BlockSpec((1, 1, 128, D), lambda w, s, c_ref: (w, s, 0, 0))],
            out_specs=pl.BlockSpec((1, 1, 128, D), lambda w, s, c_ref: (w, s, 0, 0))),
        compiler_params=_params(("parallel", "parallel"), 32),
    )(c, *_pin(gw3v, r_w3))

    def body3(c_ref, a_ref, b_ref, o_ref):
        o_ref[...] = a_ref[0] + b_ref[...]

    cs_cm = pl.pallas_call(
        body3, name="add_sibling_cm", out_shape=_sds((24, D), F32),
        grid_spec=pltpu.PrefetchScalarGridSpec(
            num_scalar_prefetch=1, grid=(1,),
            in_specs=[pl.BlockSpec((1, 24, D), lambda i, c_ref: (c_ref[0], 0, 0)),
                      pl.BlockSpec((24, D), lambda i, c_ref: (0, 0))],
            out_specs=pl.BlockSpec((24, D), lambda i, c_ref: (0, 0))),
        compiler_params=_params(("arbitrary",), 32),
    )(c, *_pin(gcm, r_cm))
    return cs_win, cs_w3, cs_cm


def _reduce_chips_copies(srcs, lands, send, recv):
    win_ref, w3_ref, cm_ref = srcs
    r_win, r_w3, r_cm = lands
    x, y, c = _xyc()
    peers = ((1 - x, y), (x, 1 - y), (1 - x, 1 - y))
    cps = []
    for k, (px, py) in enumerate(peers):
        ps = 2 * px + py
        items = ((win_ref.at[:, pl.ds(ps * W_IN_SHARD, W_IN_SHARD)], r_win.at[k]),
                 (w3_ref.at[:, ps], r_w3.at[k]),
                 (cm_ref.at[:, pl.ds(ps * ROW_SHARD, ROW_SHARD)], r_cm.at[k]))
        for a, (src, dst) in enumerate(items):
            cps.append(pltpu.make_async_remote_copy(src_ref=src, dst_ref=dst, send_sem=send.at[3 * a + k],
                                                    recv_sem=recv.at[3 * a + k], device_id=(px, py, c),
                                                    device_id_type=MESH))
    return cps


_HBM = pl.BlockSpec(memory_space=pltpu.HBM)
_SEM = pl.BlockSpec(memory_space=pltpu.SEMAPHORE)
_EFFECT = pltpu.SideEffectType.DATAFLOW_SIDE_EFFECTING


def _reduce_chips_start(cs_win, cs_w3, cs_cm):
    srcs = (cs_win, cs_w3, cs_cm)
    lands = (lax.empty((3, D // 2, W_IN_SHARD), BF16), lax.empty((3, 3, 128, D), BF16),
             lax.empty((3, 24, ROW_SHARD), F32))

    def body(*refs):
        srcs_in, lands_in, send, recv, token = refs[0:3], refs[3:6], refs[6], refs[7], refs[14]
        for cp in _reduce_chips_copies(srcs_in, lands_in, send, recv):
            cp.start()
        token[...] = jnp.zeros_like(token)

    hbm = lambda a: pltpu.HBM(a.shape, a.dtype)
    outs = pl.pallas_call(
        body, name="reduce_chips_start",
        out_shape=(pltpu.SemaphoreType.DMA((9,)), pltpu.SemaphoreType.DMA((9,)),
                   *[hbm(a) for a in srcs], *[hbm(a) for a in lands], jax.ShapeDtypeStruct((8, 128), F32)),
        in_specs=[_HBM] * 6,
        out_specs=(_SEM, _SEM, *[_HBM] * 6, pl.BlockSpec(memory_space=pltpu.VMEM)),
        input_output_aliases={i: i + 2 for i in range(6)},
        compiler_params=pltpu.CompilerParams(has_side_effects=_EFFECT),
    )(*[pltpu.with_memory_space_constraint(a, pltpu.HBM) for a in srcs + lands])
    return outs[0], outs[1], outs[2:5], outs[5:8], outs[8]


def _reduce_chips_wait(send, recv, srcs, lands, after):
    def body(*refs):
        srcs_in, lands_in, send_ref, recv_ref = refs[0:3], refs[3:6], refs[6], refs[7]
        for cp in _reduce_chips_copies(srcs_in, lands_in, send_ref, recv_ref):
            cp.wait_send()
            cp.wait_recv()

    hbm = lambda a: pltpu.HBM(a.shape, a.dtype)
    outs = pl.pallas_call(
        body, name="reduce_chips_wait",
        out_shape=(*[hbm(a) for a in srcs], *[hbm(a) for a in lands]),
        in_specs=[_HBM] * 6 + [_SEM, _SEM, pl.BlockSpec(memory_space=pl.ANY)],
        out_specs=(_HBM,) * 6,
        input_output_aliases={i: i for i in range(6)},
        compiler_params=pltpu.CompilerParams(has_side_effects=_EFFECT),
    )(*srcs, *lands, send, recv, after)
    return outs[0:3], outs[3:6]


def _add_chips(cs_win, cs_w3, cs_cm, r_win, r_w3, r_cm):
    x, y, c = _xyc()
    idx = jnp.stack([2 * x + y, c]).astype(jnp.int32)
    half = D // 2
    tr = 128

    def body1(i_ref, a_ref, b_ref, o_ref):
        f = lambda v: v.astype(F32)
        o_ref[0] = (f(a_ref[...]) + f(b_ref[2])) + (f(b_ref[0]) + f(b_ref[1]))

    f_win = pl.pallas_call(
        body1, name="add_chips_w_in", out_shape=_sds((2, half, W_IN_SHARD), F32),
        grid_spec=pltpu.PrefetchScalarGridSpec(
            num_scalar_prefetch=1, grid=(half // tr,),
            in_specs=[pl.BlockSpec((tr, W_IN_SHARD), lambda i, r: (i, r[0])),
                      pl.BlockSpec((3, tr, W_IN_SHARD), lambda i, r: (0, i, 0))],
            out_specs=pl.BlockSpec((1, tr, W_IN_SHARD), lambda i, r: (r[1], i, 0))),
        compiler_params=_params(("parallel",), 32),
    )(idx, *_pin(cs_win, r_win))

    def body2(i_ref, a_ref, b_ref, o_ref):
        f = lambda v: v.astype(F32)
        o_ref[0, 0] = (f(a_ref[0, 0]) + f(b_ref[2, 0])) + (f(b_ref[0, 0]) + f(b_ref[1, 0]))

    f_w3 = pl.pallas_call(
        body2, name="add_chips_w3", out_shape=_sds((3, 2, 128, D), F32),
        grid_spec=pltpu.PrefetchScalarGridSpec(
            num_scalar_prefetch=1, grid=(3,),
            in_specs=[pl.BlockSpec((1, 1, 128, D), lambda w, r: (w, r[0], 0, 0)),
                      pl.BlockSpec((3, 1, 128, D), lambda w, r: (0, w, 0, 0))],
            out_specs=pl.BlockSpec((1, 1, 128, D), lambda w, r: (w, r[1], 0, 0))),
        compiler_params=_params(("parallel",), 32),
    )(idx, *_pin(cs_w3, r_w3))

    def body3(i_ref, a_ref, b_ref, o_ref):
        o_ref[0] = (a_ref[...] + b_ref[2]) + (b_ref[0] + b_ref[1])

    f_cm = pl.pallas_call(
        body3, name="add_chips_cm", out_shape=_sds((2, 24, ROW_SHARD), F32),
        grid_spec=pltpu.PrefetchScalarGridSpec(
            num_scalar_prefetch=1, grid=(1,),
            in_specs=[pl.BlockSpec((24, ROW_SHARD), lambda i, r: (0, r[0])),
                      pl.BlockSpec((3, 24, ROW_SHARD), lambda i, r: (0, 0, 0))],
            out_specs=pl.BlockSpec((1, 24, ROW_SHARD), lambda i, r: (r[1], 0, 0))),
        compiler_params=_params(("arbitrary",), 32),
    )(idx, *_pin(cs_cm, r_cm))
    return f_win, f_w3, f_cm


def _share_sibling(f_win, f_w3, f_cm, smalls):
    def body(win_in, w3_in, cm_in, sm_ref, win_ref, w3_ref, cm_ref, r_sm, send, recv, ssend, srecv, lsem):
        x, y, c = _xyc()
        o = 1 - c
        cps = []
        for a, (ref, sl) in enumerate(((win_ref, lambda h: win_ref.at[h]), (w3_ref, lambda h: w3_ref.at[:, h]),
                                       (cm_ref, lambda h: cm_ref.at[h]))):
            cp = pltpu.make_async_remote_copy(src_ref=sl(c), dst_ref=sl(c), send_sem=send.at[a], recv_sem=recv.at[a],
                                              device_id=(x, y, o), device_id_type=MESH)
            cp.start()
            cps.append((cp, sl))
        me = 4 * x + 2 * y + c
        loc = pltpu.make_async_copy(sm_ref, r_sm.at[me], lsem)
        loc.start()
        scps = []
        for d in range(1, 8):
            px, py, pc = (x + (d >> 2)) % 2, (y + ((d >> 1) & 1)) % 2, (c + (d & 1)) % 2
            cp = pltpu.make_async_remote_copy(src_ref=sm_ref, dst_ref=r_sm.at[me], send_sem=ssend.at[d - 1],
                                              recv_sem=srecv.at[d - 1], device_id=(px, py, pc), device_id_type=MESH)
            cp.start()
            scps.append((cp, 4 * px + 2 * py + pc))
        for a, (cp, sl) in enumerate(cps):
            pltpu.make_async_remote_copy(src_ref=sl(o), dst_ref=sl(o), send_sem=send.at[a], recv_sem=recv.at[a],
                                         device_id=(x, y, o), device_id_type=MESH).wait_recv()
            cp.wait_send()
        for d, (cp, pid) in enumerate(scps):
            pltpu.make_async_remote_copy(src_ref=sm_ref, dst_ref=r_sm.at[pid], send_sem=ssend.at[d],
                                         recv_sem=srecv.at[d], device_id=(x, y, c), device_id_type=MESH).wait_recv()
            cp.wait_send()
        loc.wait()

    any_spec = pl.BlockSpec(memory_space=pl.ANY)
    return pl.pallas_call(
        body, name="share_sibling",
        out_shape=(_sds(f_win.shape, F32), _sds(f_w3.shape, F32), _sds(f_cm.shape, F32), _sds((8, 8, D), F32)),
        in_specs=[any_spec] * 4, out_specs=(any_spec,) * 4,
        input_output_aliases={0: 0, 1: 1, 2: 2},
        scratch_shapes=[pltpu.SemaphoreType.DMA((3,)), pltpu.SemaphoreType.DMA((3,)),
                        pltpu.SemaphoreType.DMA((7,)), pltpu.SemaphoreType.DMA((7,)), pltpu.SemaphoreType.DMA],
    )(*_pin(f_win, f_w3, f_cm, smalls))


def _adamw_math(w, g, m, v):
    m = ADAM_B1 * m + (1.0 - ADAM_B1) * g
    v = ADAM_B2 * v + (1.0 - ADAM_B2) * (g * g)
    m_hat = m / (1.0 - ADAM_B1 ** ADAM_STEP)
    v_hat = v / (1.0 - ADAM_B2 ** ADAM_STEP)
    delta = -ADAM_LR * (m_hat / (jnp.sqrt(v_hat) + ADAM_EPS) + ADAM_WD * w)
    return delta, m, v


def _adamw(w, g, m, v, tr, name):
    rows, cols = w.shape

    def body(w_ref, g_ref, m_ref, v_ref, d_o, m_o, v_o):
        d_o[...], m_o[...], v_o[...] = _adamw_math(w_ref[...], g_ref[...], m_ref[...], v_ref[...])

    spec = pl.BlockSpec((tr, cols), lambda i: (i, 0))
    return pl.pallas_call(
        body, name=name, grid=(rows // tr,),
        out_shape=(_sds((rows, cols), F32),) * 3,
        in_specs=[spec] * 4, out_specs=(spec,) * 3,
        compiler_params=_params(("parallel",), 32),
    )(*_pin(w, g, m, v))


def _adamw3(g3, ws, ms, vs):
    def body(g_ref, *refs):
        w_refs, m_refs, v_refs, outs = refs[0:3], refs[3:6], refs[6:9], refs[9:]
        g_os, d_os, m_os, v_os = outs[0:3], outs[3:6], outs[6:9], outs[9:12]
        for i in range(3):
            g = g_ref[i]
            g_os[i][0] = g
            d_os[i][0], m_os[i][0], v_os[i][0] = _adamw_math(w_refs[i][0], g, m_refs[i][0], v_refs[i][0])

    return pl.pallas_call(
        body, name="adamw_w3", out_shape=(jax.ShapeDtypeStruct((1, ROW_SHARD, D), F32),) * 12,
        compiler_params=pltpu.CompilerParams(vmem_limit_bytes=48 << 20),
    )(g3, *ws, *ms, *vs)


def _adamw_cm(f_cm, ws, ms, vs):
    def body(f_ref, *refs):
        w_refs, m_refs, v_refs, outs = refs[0:2], refs[2:4], refs[4:6], refs[6:14]
        gcw, gmt = refs[14], refs[15]
        gcw[0:16] = f_ref[0, 0:16]
        gcw[16:32] = f_ref[1, 0:16]
        gmt[0:8] = f_ref[0, 16:24]
        gmt[8:16] = f_ref[1, 16:24]
        g_conv = gcw[0:CONV_K, :]
        g_meta = gmt[...]
        outs[0][0] = g_conv
        outs[1][...] = g_meta
        outs[2][0], outs[4][0], outs[6][0] = _adamw_math(w_refs[0][0], g_conv, m_refs[0][0], v_refs[0][0])
        outs[3][...], outs[5][...], outs[7][...] = _adamw_math(w_refs[1][...], g_meta, m_refs[1][...], v_refs[1][...])

    pair = (jax.ShapeDtypeStruct((1, CONV_K, ROW_SHARD), F32), jax.ShapeDtypeStruct((N_META, ROW_SHARD), F32))
    return pl.pallas_call(
        body, name="adamw_cm", out_shape=pair * 4,
        scratch_shapes=[pltpu.VMEM((32, ROW_SHARD), F32), pltpu.VMEM((N_META, ROW_SHARD), F32)],
    )(f_cm, *ws, *ms, *vs)


def _adamw_small(r_sm, ws, ms, vs):
    def body(s_ref, *refs):
        w_refs, m_refs, v_refs, outs = refs[0:6], refs[6:12], refs[12:18], refs[18:]
        loss_o, g_os, d_os, m_os, v_os = outs[0], outs[1:7], outs[7:13], outs[13:19], outs[19:25]
        g = s_ref[0]
        for dev in range(1, 8):
            g = g + s_ref[dev]
        qk = g[4:5, :]
        qg = qk[:, 0:HEAD_DIM]
        kg = qk[:, GROUP_W:GROUP_W + HEAD_DIM]
        for h in range(1, GQA):
            qg = qg + qk[:, HEAD_DIM * h:HEAD_DIM * (h + 1)]
            kg = kg + qk[:, GROUP_W + HEAD_DIM * h:GROUP_W + HEAD_DIM * (h + 1)]
        loss_o[...] = (0.5 / D) * jnp.sum(g[5:6, :], axis=-1, keepdims=True)
        for i, gi in enumerate((g[0:1], g[1:2], g[2:3], g[3:4], qg, kg)):
            g_os[i][...] = gi
            d_os[i][...], m_os[i][...], v_os[i][...] = _adamw_math(w_refs[i][...], gi, m_refs[i][...], v_refs[i][...])

    six = tuple(jax.ShapeDtypeStruct(w.shape, F32) for w in ws)
    return pl.pallas_call(
        body, name="adamw_small", out_shape=(jax.ShapeDtypeStruct((1, 1), F32),) + six * 4,
    )(r_sm, *ws, *ms, *vs)


def kernel(x, meta_tokens, norm_g, w_in, conv_w, conv_b, conv_norm_g, conv_norm_b, w_conv_out, q_norm_g, k_norm_g, w_attn_out, w_out, loss_target, m_meta_tokens, m_norm_g, m_w_in, m_conv_w, m_conv_b, m_conv_norm_g, m_conv_norm_b, m_w_conv_out, m_q_norm_g, m_k_norm_g, m_w_attn_out, m_w_out, v_meta_tokens, v_norm_g, v_w_in, v_conv_w, v_conv_b, v_conv_norm_g, v_conv_norm_b, v_w_conv_out, v_q_norm_g, v_k_norm_g, v_w_attn_out, v_w_out):
    pad_k = lambda a: jnp.pad(a[0], ((0, 32 - CONV_K), (0, 0)))
    w3_s = (w_conv_out, w_attn_out, w_out)
    w_full, conv_w_full, meta_full, w3b, w3_land = _gather_weights(w_in[0], w3_s, pad_k(conv_w), meta_tokens)
    w3_pending, token = _w3_start(w3b, w3_land)
    norm_g_fwd = norm_g + token[0:1, 0:1]

    def w3_full(after):
        return _w3_wait(*w3_pending, after)

    def reduce_start(gwin, gw3, gcw, gmeta):
        gwin2 = gwin.reshape(D, IN_DIM)
        gw3v = gw3.reshape(3, N_CHIPS, 2, 128, D)
        gcm = jnp.concatenate([gcw.reshape(2, 16, D), gmeta.reshape(2, 8, D)], axis=1)
        r_win, r_w3, r_cm = _reduce_sibling(gwin2, gw3v, gcm)
        cs = _add_sibling(gwin2, gw3v, gcm, r_win, r_w3, r_cm)
        send, recv, srcs, lands, token = _reduce_chips_start(*cs)
        return token, (send, recv, srcs, lands)

    gx, _, _, _, _, smalls, pending = _local_step(
        x, loss_target, norm_g_fwd, conv_b, conv_norm_g, conv_norm_b, q_norm_g, k_norm_g,
        w_full, w3_full, conv_w_full, meta_full, reduce_start)
    (cs_win, cs_w3, cs_cm), (r2_win, r2_w3, r2_cm) = _reduce_chips_wait(*pending, gx)
    f_win, f_w3, f_cm = _add_chips(cs_win, cs_w3, cs_cm, r2_win, r2_w3, r2_cm)
    f_win, f_w3, f_cm, r_sm = _share_sibling(f_win, f_w3, f_cm, smalls)

    g_w_in = f_win.reshape(D, W_IN_SHARD)
    d_w_in, nm_w_in, nv_w_in = _adamw(w_in[0], g_w_in, m_w_in[0], v_w_in[0], 128, "adamw_w_in")
    w3 = _adamw3(f_w3.reshape(3, ROW_SHARD, D), w3_s, (m_w_conv_out, m_w_attn_out, m_w_out),
                 (v_w_conv_out, v_w_attn_out, v_w_out))
    cm = _adamw_cm(f_cm, (conv_w, meta_tokens), (m_conv_w, m_meta_tokens), (v_conv_w, v_meta_tokens))
    small = _adamw_small(
        r_sm, (norm_g, conv_b, conv_norm_g, conv_norm_b, q_norm_g, k_norm_g),
        (m_norm_g, m_conv_b, m_conv_norm_g, m_conv_norm_b, m_q_norm_g, m_k_norm_g),
        (v_norm_g, v_conv_b, v_conv_norm_g, v_conv_norm_b, v_q_norm_g, v_k_norm_g))

    def assemble(big_in, w3x, cmx, s6):
        ng, cb, cng, cnb, qg, kg = s6
        return (cmx[1], ng, big_in[None], cmx[0], cb, cng, cnb, w3x[0], qg, kg, w3x[1], w3x[2])

    loss = small[0].reshape(())
    grads = assemble(g_w_in, w3[0:3], cm[0:2], small[1:7])
    deltas = assemble(d_w_in, w3[3:6], cm[2:4], small[7:13])
    new_m = assemble(nm_w_in, w3[6:9], cm[4:6], small[13:19])
    new_v = assemble(nv_w_in, w3[9:12], cm[6:8], small[19:25])
    return (loss, gx, *grads, *deltas, *new_m, *new_v)
```

```python
import functools
import math

import jax
import jax.numpy as jnp
from jax import lax
from jax.experimental import pallas as pl
from jax.experimental.pallas import tpu as pltpu

F32, BF16 = jnp.float32, jnp.bfloat16
MESH = pl.DeviceIdType.MESH

D = 1024
N_META = 16
CONV_K = 31
N_KV = 4
GQA = 4
HEAD_DIM = 64
GROUP_W = GQA * HEAD_DIM
GRID_W = 64
ROPE_FREQS = 16
ROPE_THETA = 10000.0
EPS = 1e-6
IN_DIM = 7680
KEY_PAD = 128
G_CONV, G_CZ, G_Q, G_KV, G_E = (0, 2048), (2048, 1024), (3072, 1024), (4096, 512), (4608, 3072)
N_CHIPS = 4
W_IN_SHARD = IN_DIM // N_CHIPS
ROW_SHARD = D // N_CHIPS

ADAM_LR, ADAM_B1, ADAM_B2, ADAM_EPS, ADAM_WD, ADAM_STEP = 0.001, 0.9, 0.999, 1e-08, 0.01, 10

NT_DIMS = (((1,), (1,)), ((), ()))


def _params(sem=None, vmem_mb=48):
    return pltpu.CompilerParams(dimension_semantics=sem, vmem_limit_bytes=vmem_mb << 20)


def _sds(shape, dtype):
    return pltpu.HBM(tuple(shape), dtype)


def _pin(*arrays):
    return [pltpu.with_memory_space_constraint(a, pltpu.HBM) for a in arrays]


def _sig(v):
    return jax.nn.sigmoid(v)


def _dsilu(v, s):
    return s * (1.0 + v * (1.0 - s))


def _dot(a, b):
    return jnp.dot(a, b, preferred_element_type=F32)


def _dot_nt(a, b):
    return lax.dot_general(a, b, NT_DIMS, preferred_element_type=F32)


def _qk_mats():
    i = lax.broadcasted_iota(jnp.int32, (GROUP_W, GROUP_W), 0)
    j = lax.broadcasted_iota(jnp.int32, (GROUP_W, GROUP_W), 1)
    mean = jnp.where((i >> 6) == (j >> 6), 1.0 / HEAD_DIM, 0.0).astype(BF16)
    turn = jnp.where((i == j + 16) & ((j & 16) == 0), -1.0,
                     jnp.where((i == j - 16) & ((j & 16) != 0), 1.0, 0.0)).astype(BF16)
    return mean, turn


def _apply(v, mat):
    hi = v.astype(BF16)
    lo = (v - hi.astype(F32)).astype(BF16)
    return _dot(hi, mat) + _dot(lo, mat)


def _qk_fwd(v, g, cos, sin, mats):
    mean, turn = mats
    r = lax.rsqrt(_apply(v * v, mean) + EPS)
    n = v * r * g
    return n * cos + _apply(n, turn) * sin, r


def _qk_bwd(dy, v, r, g, cos, sin, mats):
    mean, turn = mats
    dn = dy * cos - _apply(dy, turn) * sin
    dyg = dn * g
    dv = r * dyg - v * (r * r * r) * _apply(dyg * v, mean)
    return dv, dn * v * r


def _rms_bwd(dxn, v, r, g):
    dxg = dxn * g
    return r * dxg - v * (r * r * r) * jnp.mean(dxg * v, axis=-1, keepdims=True)


def _glu(a):
    return a[:, :D] * _sig(a[:, D:])


def _gather_weights(w_in_s, w3_s, conv_w_s, meta_s):
    def body(win_ref, wa_ref, wb_ref, wc_ref, cw_ref, mt_ref, win_o, cw_o, mt_o, w3b_o, w3_o, win_b, w3_b,
             send, recv, fsend, frecv, lsem, csem):
        x, y, c = _xyc()
        o = 1 - c
        me = 2 * x + y
        win_b[...] = win_ref[...].astype(BF16)
        for i, ref in enumerate((wa_ref, wb_ref, wc_ref)):
            w3_b[i] = ref[0].astype(BF16)
        cast = pltpu.make_async_copy(w3_b, w3b_o, csem.at[0])
        cast.start()
        own = pltpu.make_async_copy(w3_b, _w3_place(w3_o, me), csem.at[1])
        own.start()
        items = (
            (lambda h: win_b.at[pl.ds(h * 512, 512), :],
             lambda p, h: win_o.at[pl.ds(h * 512, 512), pl.ds(p * W_IN_SHARD, W_IN_SHARD)]),
            (lambda h: cw_ref.at[pl.ds(h * 16, 16), :],
             lambda p, h: cw_o.at[pl.ds(h * 16, 16), pl.ds(p * ROW_SHARD, ROW_SHARD)]),
            (lambda h: mt_ref.at[pl.ds(h * 8, 8), :],
             lambda p, h: mt_o.at[pl.ds(h * 8, 8), pl.ds(p * ROW_SHARD, ROW_SHARD)]),
        )
        peers = ((1 - x, y), (x, 1 - y), (1 - x, 1 - y))

        def remote(src, dst, s_sem, r_sem, to):
            return pltpu.make_async_remote_copy(src_ref=src, dst_ref=dst, send_sem=s_sem, recv_sem=r_sem,
                                                device_id=to, device_id_type=MESH)

        started = []
        for a, (half, place) in enumerate(items):
            for h in range(2):
                loc = pltpu.make_async_copy(half(h), place(me, h), lsem.at[a, h])
                loc.start()
                started.append(loc.wait)
            for k, (px, py) in enumerate(peers):
                cp = remote(half(c), place(me, c), send.at[a, k], recv.at[a, k], (px, py, c))
                cp.start()
                started.append(cp.wait_send)
        for k, (px, py) in enumerate(peers):
            for a, (half, place) in enumerate(items):
                got = place(2 * px + py, c)
                remote(got, got, send.at[a, k], recv.at[a, k], (px, py, c)).wait_recv()
                fw = remote(got, got, fsend.at[a, k], frecv.at[a, k], (x, y, o))
                fw.start()
                started.append(fw.wait_send)
        for k, (px, py) in enumerate(peers):
            for a, (half, place) in enumerate(items):
                theirs = place(2 * px + py, o)
                remote(theirs, theirs, fsend.at[a, k], frecv.at[a, k], (x, y, o)).wait_recv()
        for wait in started:
            wait()
        cast.wait()
        own.wait()

    any_spec = pl.BlockSpec(memory_space=pl.ANY)
    vmem = pl.BlockSpec(memory_space=pltpu.VMEM)
    return pl.pallas_call(
        body, name="gather_weights",
        out_shape=(_sds((D, IN_DIM), BF16), _sds((32, D), F32), _sds((N_META, D), F32),
                   _sds((3, ROW_SHARD, D), BF16), _sds((3, D, D), BF16)),
        in_specs=[vmem] * 6,
        out_specs=(any_spec,) * 5,
        scratch_shapes=[pltpu.VMEM((D, W_IN_SHARD), BF16), pltpu.VMEM((3, ROW_SHARD, D), BF16),
                        pltpu.SemaphoreType.DMA((3, 3)), pltpu.SemaphoreType.DMA((3, 3)),
                        pltpu.SemaphoreType.DMA((3, 3)), pltpu.SemaphoreType.DMA((3, 3)),
                        pltpu.SemaphoreType.DMA((3, 2)), pltpu.SemaphoreType.DMA((2,))],
        compiler_params=pltpu.CompilerParams(vmem_limit_bytes=40 << 20),
    )(w_in_s, *w3_s, conv_w_s, meta_s)


def _w3_place(ref, p):
    return ref.at[:, pl.ds(p * ROW_SHARD, ROW_SHARD), :]


def _w3_copies(w3b_ref, land_ref, send, recv):
    x, y, c = _xyc()
    me = 2 * x + y
    peers = ((1 - x, y), (x, 1 - y), (1 - x, 1 - y))
    return [pltpu.make_async_remote_copy(src_ref=w3b_ref, dst_ref=_w3_place(land_ref, me),
                                         send_sem=send.at[k], recv_sem=recv.at[k], device_id=(px, py, c),
                                         device_id_type=MESH)
            for k, (px, py) in enumerate(peers)]


def _w3_start(w3b, land):
    def body(w3b_ref, land_ref, send, recv, w3b_thru, land_thru, token):
        for cp in _w3_copies(w3b_ref, land_ref, send, recv):
            cp.start()
        token[...] = jnp.zeros_like(token)

    outs = pl.pallas_call(
        body, name="w3_start",
        out_shape=(pltpu.SemaphoreType.DMA((3,)), pltpu.SemaphoreType.DMA((3,)),
                   pltpu.HBM(w3b.shape, BF16), pltpu.HBM(land.shape, BF16), jax.ShapeDtypeStruct((8, 128), F32)),
        in_specs=[_HBM, _HBM],
        out_specs=(_SEM, _SEM, _HBM, _HBM, pl.BlockSpec(memory_space=pltpu.VMEM)),
        input_output_aliases={0: 2, 1: 3},
        compiler_params=pltpu.CompilerParams(has_side_effects=_EFFECT),
    )(*_pin(w3b, land))
    return outs[0:4], outs[4]


def _w3_wait(send, recv, w3b, land, after):
    def body(w3b_ref, land_ref, send_ref, recv_ref, after_ref, w3b_out, land_out):
        x, y, c = _xyc()
        peers = ((1 - x, y), (x, 1 - y), (1 - x, 1 - y))
        for k, (cp, (px, py)) in enumerate(zip(_w3_copies(w3b_ref, land_ref, send_ref, recv_ref), peers)):
            cp.wait_send()
            got = _w3_place(land_ref, 2 * px + py)
            pltpu.make_async_remote_copy(src_ref=got, dst_ref=got, send_sem=send_ref.at[k], recv_sem=recv_ref.at[k],
                                         device_id=(px, py, c), device_id_type=MESH).wait_recv()

    outs = pl.pallas_call(
        body, name="w3_wait",
        out_shape=(pltpu.HBM(w3b.shape, BF16), pltpu.HBM(land.shape, BF16)),
        in_specs=[_HBM, _HBM, _SEM, _SEM, pl.BlockSpec(memory_space=pl.ANY)],
        out_specs=(_HBM, _HBM),
        input_output_aliases={0: 0, 1: 1},
        compiler_params=pltpu.CompilerParams(has_side_effects=_EFFECT),
    )(w3b, land, send, recv, after)
    return outs[1]


def _meta_fwd(meta_full, norm_g, w_full):
    def body(m_ref, g_ref, wc_ref, wkv_ref, xnt_ref, pc_ref, pkv_ref):
        v = m_ref[...]
        r = lax.rsqrt(jnp.mean(v * v, axis=-1, keepdims=True) + EPS)
        xn = v * r * g_ref[...]
        xnb = xn.astype(BF16)
        pad = jnp.concatenate([xn, jnp.zeros((128 - N_META, D), F32)], axis=0)
        xnt_ref[...] = pad.T.astype(BF16)
        pc_ref[...] = _dot(xnb, wc_ref[...])
        pkv_ref[...] = _dot(xnb, wkv_ref[...])

    return pl.pallas_call(
        body, name="meta_fwd", grid=(1,),
        out_shape=(_sds((D, 128), BF16), _sds((N_META, 2048), F32),
                   _sds((N_META, 512), F32)),
        in_specs=[pl.BlockSpec((N_META, D), lambda i: (0, 0)), pl.BlockSpec((1, D), lambda i: (0, 0)),
                  pl.BlockSpec((D, 2048), lambda i: (0, 0)), pl.BlockSpec((D, 512), lambda i: (0, G_KV[0] // 512))],
        out_specs=(pl.BlockSpec((D, 128), lambda i: (0, 0)), pl.BlockSpec((N_META, 2048), lambda i: (0, 0)),
                   pl.BlockSpec((N_META, 512), lambda i: (0, 0))),
        compiler_params=_params(("arbitrary",), 32),
    )(*_pin(meta_full, norm_g, w_full, w_full))


def _in_proj(x2, norm_g, w_full, qg, cos, sin, nb, tm):
    rows = x2.shape[0]
    nt = rows // nb // tm
    groups = (G_CONV, G_CZ, G_Q, G_KV, G_E)
    scale = 1.0 / math.sqrt(HEAD_DIM)

    def body(x_ref, g_ref, w_hbm, qg_ref, cos_ref, sin_ref, *rest):
        outs, xnt_ref, q2_o, qt_o, w_vmem, sem = rest[:5], rest[5], rest[6], rest[7], rest[8], rest[9]

        @pl.when(pl.program_id(0) == 0)
        def _():
            cp = pltpu.make_async_copy(w_hbm, w_vmem, sem)
            cp.start()
            cp.wait()

        v = x_ref[...]
        r = lax.rsqrt(jnp.mean(v * v, axis=-1, keepdims=True) + EPS)
        xn = v * r * g_ref[...]
        xnb = xn.astype(BF16)
        xnt_ref[...] = xn.T.astype(BF16)
        for ref, (off, wd) in zip(outs, groups):
            for c0 in range(0, wd, 512):
                ref[:, c0:c0 + 512] = _dot(xnb, w_vmem[:, off + c0:off + c0 + 512])
        gv, cosv, sinv = qg_ref[...], cos_ref[...], sin_ref[...]
        mats = _qk_mats()
        for g in range(N_KV):
            gs = slice(GROUP_W * g, GROUP_W * (g + 1))
            qr, _ = _qk_fwd(outs[2][:, gs], gv, cosv, sinv, mats)
            q2_o[:, gs] = (qr * (scale * LOG2E)).astype(BF16)
            qt_o[gs, :] = (qr * scale).T.astype(BF16)

    rope = pl.BlockSpec((tm, GROUP_W), lambda i: (lax.rem(i, nt), 0))
    return pl.pallas_call(
        body, name="in_proj", grid=(rows // tm,),
        out_shape=tuple(_sds((rows, wd), F32) for _, wd in groups)
        + (_sds((D, rows), BF16), _sds((rows, D), BF16), _sds((D, rows), BF16)),
        in_specs=[pl.BlockSpec((tm, D), lambda i: (i, 0)), pl.BlockSpec((1, D), lambda i: (0, 0)),
                  pl.BlockSpec(memory_space=pl.ANY), pl.BlockSpec((1, GROUP_W), lambda i: (0, 0)), rope, rope],
        out_specs=tuple(pl.BlockSpec((tm, wd), lambda i: (i, 0)) for _, wd in groups)
        + (pl.BlockSpec((D, tm), lambda i: (0, i)), pl.BlockSpec((tm, D), lambda i: (i, 0)),
           pl.BlockSpec((D, tm), lambda i: (0, i))),
        scratch_shapes=[pltpu.VMEM((D, IN_DIM), BF16), pltpu.SemaphoreType.DMA],
        compiler_params=_params(("arbitrary",), 58),
    )(*_pin(x2, norm_g, w_full, qg, cos, sin))


def _halo_specs(width, tm, nt, rows):
    h16 = tm // 16
    return [pl.BlockSpec((tm, width), lambda b, i: (b * nt + i, 0)),
            pl.BlockSpec((16, width), lambda b, i: (jnp.maximum((b * nt + i) * h16 - 1, 0), 0)),
            pl.BlockSpec((16, width), lambda b, i: (jnp.minimum((b * nt + i + 1) * h16, rows // 16 - 1), 0))]


def _fill_uext(uext, cur, prev, nxt, meta, i, nt, tm):
    uext[0:16] = jnp.where(i == 0, _glu(meta[...]), _glu(prev[...]))
    uext[16:16 + tm] = _glu(cur[...])
    uext[16 + tm:32 + tm] = jnp.where(i == nt - 1, 0.0, _glu(nxt[...]))


def _shifted_copies(dst, src, n):
    for r in range(1, 8):
        dst[r, 0:n] = src[r:r + n]


def _rows32(shifted, src, start, cols):
    q8, r = divmod(start, 8)
    if r == 0:
        return src[start:start + 32, cols]
    return shifted[r, 8 * q8:8 * q8 + 32, cols]


def _conv_fwd(pconv, pm_conv, conv_w, conv_b, nb, tm):
    rows = pconv.shape[0]
    nt = rows // nb // tm

    def body(cur, prev, nxt, meta, w_ref, b_ref, o_ref, uext, ush):
        i = pl.program_id(1)
        _fill_uext(uext, cur, prev, nxt, meta, i, nt, tm)
        _shifted_copies(ush, uext, tm + 24)
        for r0 in range(0, tm, 32):
            for c0 in range(0, D, 256):
                acc = jnp.zeros((32, 256), F32) + b_ref[:, c0:c0 + 256]
                for j in range(CONV_K):
                    acc = acc + _rows32(ush, uext, r0 + j + 1, slice(c0, c0 + 256)) * w_ref[j:j + 1, c0:c0 + 256]
                o_ref[r0:r0 + 32, c0:c0 + 256] = acc

    return pl.pallas_call(
        body, name="conv_fwd", grid=(nb, nt),
        out_shape=_sds((rows, D), F32),
        in_specs=_halo_specs(2048, tm, nt, rows)
        + [pl.BlockSpec((16, 2048), lambda b, i: (0, 0)), pl.BlockSpec((32, D), lambda b, i: (0, 0)),
           pl.BlockSpec((1, D), lambda b, i: (0, 0))],
        out_specs=pl.BlockSpec((tm, D), lambda b, i: (b * nt + i, 0)),
        scratch_shapes=[pltpu.VMEM((tm + 32, D), F32), pltpu.VMEM((8, tm + 24, D), F32)],
        compiler_params=_params(("parallel", "parallel"), 40),
    )(*_pin(pconv, pconv, pconv, pm_conv, conv_w, conv_b))


def _kv_prep(pkv, pm_kv, kg, cos, sin, nb):
    rows = pkv.shape[0]
    s_len = rows // nb
    tk = min(512, s_len)
    nt = s_len // tk

    def body(kv_ref, m_ref, g_ref, cos_ref, sin_ref, k_o, v_o, k2_o, v2_o):
        i = pl.program_id(1)
        mats = _qk_mats()
        kv = kv_ref[...]
        kr, _ = _qk_fwd(kv[:, :GROUP_W], g_ref[...], cos_ref[...], sin_ref[...], mats)
        ones = _ones_cols(tk, tk)
        for h in range(N_KV):
            k_o[0, h] = kr[:, HEAD_DIM * h:HEAD_DIM * (h + 1)].astype(BF16)
            vh = kv[:, GROUP_W + HEAD_DIM * h:GROUP_W + HEAD_DIM * (h + 1)]
            v_o[0, h] = jnp.concatenate([vh, ones], axis=1).astype(BF16)

        @pl.when(i == 0)
        def _():
            kvm = m_ref[...]
            km = kvm[:, :GROUP_W]
            kn = km * lax.rsqrt(_apply(km * km, mats[0]) + EPS) * g_ref[...]
            zeros = jnp.zeros((KEY_PAD - N_META, GROUP_W), F32)
            kfull = jnp.concatenate([kn, zeros], axis=0)
            vfull = jnp.concatenate([kvm[:, GROUP_W:], zeros], axis=0)
            ones_m = _ones_cols(KEY_PAD, N_META)
            for h in range(N_KV):
                k2_o[0, h] = kfull[:, HEAD_DIM * h:HEAD_DIM * (h + 1)].astype(BF16)
                v2_o[0, h] = jnp.concatenate([vfull[:, HEAD_DIM * h:HEAD_DIM * (h + 1)], ones_m], axis=1).astype(BF16)

    return pl.pallas_call(
        body, name="kv_prep", grid=(nb, nt),
        out_shape=(_sds((nb, N_KV, s_len, HEAD_DIM), BF16), _sds((nb, N_KV, s_len, 2 * HEAD_DIM), BF16),
                   _sds((nb, N_KV, KEY_PAD, HEAD_DIM), BF16), _sds((nb, N_KV, KEY_PAD, 2 * HEAD_DIM), BF16)),
        in_specs=[pl.BlockSpec((tk, 512), lambda b, i: (b * nt + i, 0)),
                  pl.BlockSpec((N_META, 512), lambda b, i: (0, 0)), pl.BlockSpec((1, GROUP_W), lambda b, i: (0, 0)),
                  pl.BlockSpec((tk, GROUP_W), lambda b, i: (i, 0)),
                  pl.BlockSpec((tk, GROUP_W), lambda b, i: (i, 0))],
        out_specs=(pl.BlockSpec((1, N_KV, tk, HEAD_DIM), lambda b, i: (b, 0, i, 0)),
                   pl.BlockSpec((1, N_KV, tk, 2 * HEAD_DIM), lambda b, i: (b, 0, i, 0)),
                   pl.BlockSpec((1, N_KV, KEY_PAD, HEAD_DIM), lambda b, i: (b, 0, 0, 0)),
                   pl.BlockSpec((1, N_KV, KEY_PAD, 2 * HEAD_DIM), lambda b, i: (b, 0, 0, 0))),
        compiler_params=_params(("parallel", "arbitrary"), 40),
    )(*_pin(pkv, pm_kv, kg, cos, sin))


def _ones_cols(rows, valid):
    r = lax.broadcasted_iota(jnp.int32, (rows, HEAD_DIM), 0)
    col = lax.broadcasted_iota(jnp.int32, (rows, HEAD_DIM), 1)
    return jnp.where((col < 2) & (r < valid), 1.0, 0.0).astype(F32)


def _tail_bias():
    col = lax.broadcasted_iota(jnp.int32, (1, KEY_PAD), 1)
    return jnp.where(col < N_META, 0.0, -1e30).astype(F32)


LOG2E = 1.4426950408889634


def _kv_specs(s_len):
    return [pl.BlockSpec((1, 1, s_len, HEAD_DIM), lambda b, g, i: (b, g, 0, 0)),
            pl.BlockSpec((1, 1, s_len, 2 * HEAD_DIM), lambda b, g, i: (b, g, 0, 0)),
            pl.BlockSpec((1, 1, KEY_PAD, HEAD_DIM), lambda b, g, i: (b, g, 0, 0)),
            pl.BlockSpec((1, 1, KEY_PAD, 2 * HEAD_DIM), lambda b, g, i: (b, g, 0, 0))]


def _attn_fwd(q2, kv4, nb, tq):
    rows = q2.shape[0]
    s_len = rows // nb
    nq = s_len // tq

    def body(q_ref, k1_ref, v1_ref, k2_ref, v2_ref, o_ref, lse_ref):
        qs = q_ref[...]
        k1, k2, v1, v2 = k1_ref[0, 0], k2_ref[0, 0], v1_ref[0, 0], v2_ref[0, 0]
        bias = _tail_bias()
        outs, lses = [], []

        def scores(h):
            qh = qs[:, HEAD_DIM * h:HEAD_DIM * (h + 1)]
            return _dot_nt(qh, k1), _dot_nt(qh, k2) + bias

        ahead = scores(0)
        for h in range(GQA):
            s1, s2 = ahead
            if h + 1 < GQA:
                ahead = scores(h + 1)
            m = jnp.maximum(jnp.max(s1, axis=-1, keepdims=True), jnp.max(s2, axis=-1, keepdims=True))
            oe = _dot(jnp.exp2(s1 - m).astype(BF16), v1) + _dot(jnp.exp2(s2 - m).astype(BF16), v2)
            l = oe[:, HEAD_DIM:HEAD_DIM + 1]
            outs.append(oe[:, :HEAD_DIM] / l)
            lses.append(m + jnp.log2(l))
        o_ref[...] = jnp.concatenate(outs, axis=1)
        lse_ref[0, 0] = jnp.concatenate(lses, axis=1)

    return pl.pallas_call(
        body, name="attn_fwd", grid=(nb, N_KV, nq),
        out_shape=(_sds((rows, D), F32), _sds((nb, N_KV, s_len, GQA), F32)),
        in_specs=[pl.BlockSpec((tq, GROUP_W), lambda b, g, i: (b * nq + i, g))] + _kv_specs(s_len),
        out_specs=(pl.BlockSpec((tq, GROUP_W), lambda b, g, i: (b * nq + i, g)),
                   pl.BlockSpec((1, 1, tq, GQA), lambda b, g, i: (b, g, i, 0))),
        compiler_params=_params(("parallel", "parallel", "parallel"), 48),
    )(*_pin(q2, *kv4))


def _mid(x2, t2, c0, cz, o, e, w3, cn_g, cn_b, tm):
    rows = x2.shape[0]

    def body(x_ref, t_ref, c0_ref, cz_ref, o_ref, e_ref, w_ref, g_ref, b_ref,
             dy_o, mt_o, c3t_o, o2t_o, dyc_o, dya_o, do_o, dc0_o, dcz_o, de_o, sums_o):
        wco, wao, wo = w_ref[0], w_ref[1], w_ref[2]
        cn_g_v = g_ref[...]
        c0v = c0_ref[...]
        xc = c0v - jnp.mean(c0v, axis=-1, keepdims=True)
        rstd = lax.rsqrt(jnp.mean(xc * xc, axis=-1, keepdims=True) + EPS)
        n = xc * rstd
        c1 = n * cn_g_v + b_ref[...]
        s1 = _sig(c1)
        c2 = c1 * s1
        czv = cz_ref[...]
        sz = _sig(czv)
        gz = czv * sz
        c3 = c2 * gz
        yc = _dot(c3.astype(BF16), wco)
        az, gc, ga = e_ref[:, :D], e_ref[:, D:2 * D], e_ref[:, 2 * D:]
        saz = _sig(az)
        gaz = az * saz
        ov = o_ref[...]
        o2 = ov * gaz
        ya = _dot(o2.astype(BF16), wao)
        sc, sa = _sig(gc), _sig(ga)
        merged = sc * yc + sa * ya
        out = _dot(merged.astype(BF16), wo)
        err = x_ref[...] + out - t_ref[...]
        dy = err * (1.0 / D)
        dy_o[...] = dy
        dm = _dot_nt(dy.astype(BF16), wo)
        dyc = dm * sc
        dya = dm * sa
        dycb, dyab = dyc.astype(BF16), dya.astype(BF16)
        dyc_o[...] = dycb
        dya_o[...] = dyab
        de_o[:, D:2 * D] = (dyc * yc * (1.0 - sc)).astype(BF16)
        de_o[:, 2 * D:] = (dya * ya * (1.0 - sa)).astype(BF16)
        dc3 = _dot_nt(dycb, wco)
        do2 = _dot_nt(dyab, wao)
        do_o[...] = do2 * gaz
        de_o[:, :D] = (do2 * ov * _dsilu(az, saz)).astype(BF16)
        dcz_o[...] = (dc3 * c2 * _dsilu(czv, sz)).astype(BF16)
        dc1 = dc3 * gz * _dsilu(c1, s1)
        dn = dc1 * cn_g_v
        dc0 = rstd * (dn - jnp.mean(dn, axis=-1, keepdims=True) - n * jnp.mean(dn * n, axis=-1, keepdims=True))
        dc0_o[...] = dc0
        mt_o[...] = merged.T.astype(BF16)
        c3t_o[...] = c3.T.astype(BF16)
        o2t_o[...] = o2.T.astype(BF16)

        @pl.when(pl.program_id(0) == 0)
        def _():
            sums_o[...] = jnp.zeros_like(sums_o)

        sums_o[0:1, :] += jnp.sum(dc1 * n, axis=0, keepdims=True)
        sums_o[1:2, :] += jnp.sum(dc1, axis=0, keepdims=True)
        sums_o[2:3, :] += jnp.sum(dc0, axis=0, keepdims=True)
        sums_o[3:4, :] += jnp.sum(err * err, axis=0, keepdims=True)

    row = lambda wd: pl.BlockSpec((tm, wd), lambda i: (i, 0))
    col = pl.BlockSpec((D, tm), lambda i: (0, i))
    vec = pl.BlockSpec((1, D), lambda i: (0, 0))
    f32o = lambda wd: _sds((rows, wd), F32)
    b16o = lambda wd: _sds((rows, wd), BF16)
    tpo = _sds((D, rows), BF16)
    return pl.pallas_call(
        body, name="mid", grid=(rows // tm,),
        out_shape=(f32o(D), tpo, tpo, tpo, b16o(D), b16o(D), f32o(D), f32o(D), b16o(D), b16o(3 * D),
                   _sds((8, D), F32)),
        in_specs=[row(D), row(D), row(D), row(D), row(D), row(3 * D),
                  pl.BlockSpec((3, D, D), lambda i: (0, 0, 0)), vec, vec],
        out_specs=(row(D), col, col, col, row(D), row(D), row(D), row(D), row(D), row(3 * D),
                   pl.BlockSpec((8, D), lambda i: (0, 0))),
        compiler_params=_params(("arbitrary",), 60),
    )(*_pin(x2, t2, c0, cz, o, e, w3, cn_g, cn_b))


def _do_prep(d_o, o, tm):
    rows = d_o.shape[0]

    ch = jnp.arange(D, dtype=jnp.int32)
    head, col2 = jnp.arange(128, dtype=jnp.int32), jnp.arange(2 * D, dtype=jnp.int32)
    ind = ((ch // HEAD_DIM)[:, None] == head[None, :]).astype(BF16)
    spread = (col2[None, :] == (128 * (ch // HEAD_DIM) + ch % HEAD_DIM)[:, None]).astype(BF16)
    place = jnp.concatenate([-(col2[None, :] == (128 * head + 64)[:, None]).astype(F32),
                             -(col2[None, :] == (128 * head + 65)[:, None]).astype(F32)], axis=0).astype(BF16)

    def body(do_ref, o_ref, ind_ref, spread_ref, place_ref, doe_o, dot_o):
        dov = do_ref[...]
        delta = _apply(dov * o_ref[...], ind_ref[...])
        d_hi = delta.astype(BF16)
        d_lo = (delta - d_hi.astype(F32)).astype(BF16)
        tails = _dot(jnp.concatenate([d_hi, d_lo], axis=1), place_ref[...])
        doe_o[...] = (_dot(dov.astype(BF16), spread_ref[...]) + tails).astype(BF16)
        dot_o[...] = dov.T.astype(BF16)

    row = pl.BlockSpec((tm, D), lambda i: (i, 0))
    const = lambda a: pl.BlockSpec(a.shape, lambda i: (0, 0))
    return pl.pallas_call(
        body, name="do_prep", grid=(rows // tm,),
        out_shape=(_sds((rows, 2 * D), BF16), _sds((D, rows), BF16)),
        in_specs=[row, row, const(ind), const(spread), const(place)],
        out_specs=(pl.BlockSpec((tm, 2 * D), lambda i: (i, 0)), pl.BlockSpec((D, tm), lambda i: (0, i))),
        compiler_params=_params(("parallel",), 40),
    )(*_pin(d_o, o, ind, spread, place))


def _q_post(dqr, pq, qg, cos, sin, nb, tm):
    rows = pq.shape[0]
    nt = rows // nb // tm

    def body(dq_ref, q_ref, g_ref, cos_ref, sin_ref, dq_o, dg_o):
        @pl.when((pl.program_id(0) == 0) & (pl.program_id(1) == 0))
        def _():
            dg_o[...] = jnp.zeros_like(dg_o)

        gv, cosv, sinv = g_ref[...], cos_ref[...], sin_ref[...]
        acc = jnp.zeros((1, GROUP_W), F32)
        mats = _qk_mats()
        for g in range(N_KV):
            gs = slice(GROUP_W * g, GROUP_W * (g + 1))
            qv = q_ref[:, gs]
            r = lax.rsqrt(_apply(qv * qv, mats[0]) + EPS)
            dq, dgr = _qk_bwd(dq_ref[:, gs], qv, r, gv, cosv, sinv, mats)
            dq_o[:, gs] = dq.astype(BF16)
            acc = acc + jnp.sum(dgr, axis=0, keepdims=True)
        dg_o[...] += acc

    row = pl.BlockSpec((tm, D), lambda b, i: (b * nt + i, 0))
    rope = pl.BlockSpec((tm, GROUP_W), lambda b, i: (i, 0))
    vec = pl.BlockSpec((1, GROUP_W), lambda b, i: (0, 0))
    return pl.pallas_call(
        body, name="q_post", grid=(nb, nt),
        out_shape=(_sds((rows, D), BF16), _sds((1, GROUP_W), F32)),
        in_specs=[row, row, vec, rope, rope], out_specs=(row, vec),
        compiler_params=_params(("arbitrary", "arbitrary"), 32),
    )(*_pin(dqr, pq, qg, cos, sin))


def _attn_bwd(q2, qst, kv4, doe, dot_, lse, nb, tq):
    rows = q2.shape[0]
    s_len = rows // nb
    nq = s_len // tq
    scale = 1.0 / math.sqrt(HEAD_DIM)

    def body(q_ref, qt_ref, k1_ref, v1_ref, k2_ref, v2_ref, doe_ref, dot_ref, lse_ref,
             dq_o, dkt_o, dvt_o, dkt2_o, dvt2_o):
        i = pl.program_id(2)
        lse = lse_ref[0, 0]
        k1, k2, v1, v2 = k1_ref[0, 0], k2_ref[0, 0], v1_ref[0, 0], v2_ref[0, 0]
        bias = _tail_bias()
        dkt1, dkt2 = jnp.zeros((HEAD_DIM, s_len), F32), jnp.zeros((HEAD_DIM, KEY_PAD), F32)
        dvt1, dvt2 = jnp.zeros((HEAD_DIM, s_len), F32), jnp.zeros((HEAD_DIM, KEY_PAD), F32)

        def products(h):
            qh = q_ref[:, HEAD_DIM * h:HEAD_DIM * (h + 1)]
            dh = doe_ref[:, 2 * HEAD_DIM * h:2 * HEAD_DIM * (h + 1)]
            return _dot_nt(qh, k1), _dot_nt(qh, k2) + bias, _dot_nt(dh, v1), _dot_nt(dh, v2)

        ahead = products(0)
        for h in range(GQA):
            hs = slice(HEAD_DIM * h, HEAD_DIM * (h + 1))
            s1, s2, dp1, dp2 = ahead
            if h + 1 < GQA:
                ahead = products(h + 1)
            lse_h = lse[:, h:h + 1]
            p1 = jnp.exp2(s1 - lse_h)
            p2 = jnp.exp2(s2 - lse_h)
            ds1 = (p1 * dp1).astype(BF16)
            ds2 = (p2 * dp2).astype(BF16)
            dq_o[:, hs] = (_dot(ds1, k1) + _dot(ds2, k2)) * scale
            dkt1 = dkt1 + _dot(qt_ref[hs, :], ds1)
            dkt2 = dkt2 + _dot(qt_ref[hs, :], ds2)
            dvt1 = dvt1 + _dot(dot_ref[hs, :], p1.astype(BF16))
            dvt2 = dvt2 + _dot(dot_ref[hs, :], p2.astype(BF16))

        @pl.when(i == 0)
        def _():
            dkt_o[0, 0], dkt2_o[0, 0], dvt_o[0, 0], dvt2_o[0, 0] = dkt1, dkt2, dvt1, dvt2

        @pl.when(i > 0)
        def _():
            dkt_o[0, 0] += dkt1
            dkt2_o[0, 0] += dkt2
            dvt_o[0, 0] += dvt1
            dvt2_o[0, 0] += dvt2

    qspec = pl.BlockSpec((tq, GROUP_W), lambda b, g, i: (b * nq + i, g))
    qtspec = pl.BlockSpec((GROUP_W, tq), lambda b, g, i: (g, b * nq + i))
    tspec = pl.BlockSpec((1, 1, HEAD_DIM, s_len), lambda b, g, i: (b, g, 0, 0))
    t2spec = pl.BlockSpec((1, 1, HEAD_DIM, KEY_PAD), lambda b, g, i: (b, g, 0, 0))
    tshape = _sds((nb, N_KV, HEAD_DIM, s_len), F32)
    t2shape = _sds((nb, N_KV, HEAD_DIM, KEY_PAD), F32)
    return pl.pallas_call(
        body, name="attn_bwd", grid=(nb, N_KV, nq),
        out_shape=(_sds((rows, D), F32), tshape, tshape, t2shape, t2shape),
        in_specs=[qspec, qtspec] + _kv_specs(s_len)
        + [pl.BlockSpec((tq, 2 * GROUP_W), lambda b, g, i: (b * nq + i, g)), qtspec,
           pl.BlockSpec((1, 1, tq, GQA), lambda b, g, i: (b, g, i, 0))],
        out_specs=(qspec, tspec, tspec, t2spec, t2spec),
        compiler_params=_params(("parallel", "parallel", "arbitrary"), 56),
    )(*_pin(q2, qst, *kv4, doe, dot_, lse))


def _kv_bwd(dkt, dvt, dkt2, dvt2, pkv, pm_kv, kg, cos, sin, nb):
    rows = pkv.shape[0]
    s_len = rows // nb
    tk = min(512, s_len)
    nt = s_len // tk

    def body(dk_ref, dv_ref, dk2_ref, dv2_ref, kv_ref, m_ref, g_ref, cos_ref, sin_ref, d_o, dm_o, dg_o):
        b, i = pl.program_id(0), pl.program_id(1)
        gv = g_ref[...]
        mats = _qk_mats()

        @pl.when((b == 0) & (i == 0))
        def _():
            dg_o[...] = jnp.zeros_like(dg_o)

        kx = kv_ref[:, :GROUP_W]
        r = lax.rsqrt(_apply(kx * kx, mats[0]) + EPS)
        dk, dgr = _qk_bwd(dk_ref[0].T, kx, r, gv, cos_ref[...], sin_ref[...], mats)
        d_o[:, :GROUP_W] = dk.astype(BF16)
        d_o[:, GROUP_W:] = dv_ref[0].T.astype(BF16)
        dg_o[...] += jnp.sum(dgr, axis=0, keepdims=True)

        @pl.when(i == 0)
        def _():
            kxm = m_ref[:, :GROUP_W]
            rm = lax.rsqrt(_apply(kxm * kxm, mats[0]) + EPS)
            dn = dk2_ref[0].T[0:N_META]
            dyg = dn * gv
            dm_o[0, :, :GROUP_W] = rm * dyg - kxm * (rm * rm * rm) * _apply(dyg * kxm, mats[0])
            dm_o[0, :, GROUP_W:] = dv2_ref[0].T[0:N_META]
            dg_o[...] += jnp.sum(dn * kxm * rm, axis=0, keepdims=True)

    tspec = pl.BlockSpec((1, GROUP_W, tk), lambda b, i: (b, 0, i))
    t2spec = pl.BlockSpec((1, GROUP_W, KEY_PAD), lambda b, i: (b, 0, 0))
    rope = pl.BlockSpec((tk, GROUP_W), lambda b, i: (i, 0))
    return pl.pallas_call(
        body, name="kv_bwd", grid=(nb, nt),
        out_shape=(_sds((rows, 512), BF16), _sds((nb, N_META, 512), F32),
                   _sds((1, GROUP_W), F32)),
        in_specs=[tspec, tspec, t2spec, t2spec, pl.BlockSpec((tk, 512), lambda b, i: (b * nt + i, 0)),
                  pl.BlockSpec((N_META, 512), lambda b, i: (0, 0)), pl.BlockSpec((1, GROUP_W), lambda b, i: (0, 0)),
                  rope, rope],
        out_specs=(pl.BlockSpec((tk, 512), lambda b, i: (b * nt + i, 0)),
                   pl.BlockSpec((1, N_META, 512), lambda b, i: (b, 0, 0)),
                   pl.BlockSpec((1, GROUP_W), lambda b, i: (0, 0))),
        compiler_params=_params(("arbitrary", "arbitrary"), 40),
    )(*_pin(dkt, dvt, dkt2, dvt2, pkv, pm_kv, kg, cos, sin))


def _conv_bwd(dc0, pconv, pm_conv, conv_w, nb, tm):
    rows = pconv.shape[0]
    nt = rows // nb // tm

    def body(dcur, dprev, dnxt, cur, meta, w_ref, da_o, dam_o, gw_o, ucur, dext, dsh):
        b, i = pl.program_id(0), pl.program_id(1)
        ucur[...] = _glu(cur[...])
        dext[0:16] = jnp.zeros((16, D), F32)
        dext[16:32] = jnp.where(i == 0, 0.0, dprev[...])
        dext[32:32 + tm] = dcur[...]
        dext[32 + tm:48 + tm] = jnp.where(i == nt - 1, 0.0, dnxt[...])
        _shifted_copies(dsh, dext, tm + 40)

        @pl.when((b == 0) & (i == 0))
        def _():
            gw_o[...] = jnp.zeros_like(gw_o)

        for c0 in range(0, D, 256):
            cs = slice(c0, c0 + 256)
            for r0 in range(0, tm, 32):
                acc = jnp.zeros((32, 256), F32)
                for j in range(CONV_K):
                    acc = acc + _rows32(dsh, dext, r0 + 47 - j, cs) * w_ref[j:j + 1, cs]
                cv = cur[r0:r0 + 32, c0:c0 + 256]
                sg = _sig(cur[r0:r0 + 32, D + c0:D + c0 + 256])
                da_o[r0:r0 + 32, cs] = (acc * sg).astype(BF16)
                da_o[r0:r0 + 32, D + c0:D + c0 + 256] = (acc * cv * sg * (1.0 - sg)).astype(BF16)
            for j in range(CONV_K):
                acc = jnp.zeros((32, 256), F32)
                for r0 in range(0, tm, 32):
                    acc = acc + _rows32(dsh, dext, r0 + 47 - j, cs) * ucur[r0:r0 + 32, cs]
                gw_o[j:j + 1, cs] += jnp.sum(acc, axis=0, keepdims=True)

        @pl.when(i == 0)
        def _():
            for c0 in range(0, D, 256):
                cs = slice(c0, c0 + 256)
                cv = meta[:, c0:c0 + 256]
                sg = _sig(meta[:, D + c0:D + c0 + 256])
                um = cv * sg
                acc = jnp.zeros((16, 256), F32)
                for j in range(CONV_K):
                    d = dext[31 - j:47 - j, cs]
                    acc = acc + d * w_ref[j:j + 1, cs]
                    gw_o[j:j + 1, cs] += jnp.sum(d * um, axis=0, keepdims=True)
                dam_o[0, :, cs] = acc * sg
                dam_o[0, :, D + c0:D + c0 + 256] = acc * cv * sg * (1.0 - sg)

    return pl.pallas_call(
        body, name="conv_bwd", grid=(nb, nt),
        out_shape=(_sds((rows, 2048), BF16), _sds((nb, N_META, 2048), F32),
                   _sds((32, D), F32)),
        in_specs=_halo_specs(D, tm, nt, rows)
        + [pl.BlockSpec((tm, 2048), lambda b, i: (b * nt + i, 0)),
           pl.BlockSpec((16, 2048), lambda b, i: (0, 0)), pl.BlockSpec((32, D), lambda b, i: (0, 0))],
        out_specs=(pl.BlockSpec((tm, 2048), lambda b, i: (b * nt + i, 0)),
                   pl.BlockSpec((1, N_META, 2048), lambda b, i: (b, 0, 0)),
                   pl.BlockSpec((32, D), lambda b, i: (0, 0))),
        scratch_shapes=[pltpu.VMEM((tm, D), F32), pltpu.VMEM((tm + 48, D), F32), pltpu.VMEM((8, tm + 40, D), F32)],
        compiler_params=_params(("arbitrary", "arbitrary"), 48),
    )(*_pin(dc0, dc0, dc0, pconv, pm_conv, conv_w))


def _meta_bwd(dam, ddm, w_full, meta_full, norm_g):
    nb = dam.shape[0]

    def body(a_ref, d_ref, wc_ref, wkv_ref, m_ref, g_ref, gm_o, dg_o):
        a, d = a_ref[0], d_ref[0]
        for b in range(1, nb):
            a = a + a_ref[b]
            d = d + d_ref[b]
        dxn = _dot_nt(a.astype(BF16), wc_ref[...]) + _dot_nt(d.astype(BF16), wkv_ref[...])
        v = m_ref[...]
        r = lax.rsqrt(jnp.mean(v * v, axis=-1, keepdims=True) + EPS)
        gm_o[...] = _rms_bwd(dxn, v, r, g_ref[...])
        dg_o[...] = jnp.sum(dxn * v * r, axis=0, keepdims=True)

    return pl.pallas_call(
        body, name="meta_bwd", grid=(1,),
        out_shape=(_sds((N_META, D), F32), _sds((1, D), F32)),
        in_specs=[pl.BlockSpec((nb, N_META, 2048), lambda i: (0, 0, 0)), pl.BlockSpec((nb, N_META, 512), lambda i: (0, 0, 0)),
                  pl.BlockSpec((D, 2048), lambda i: (0, 0)), pl.BlockSpec((D, 512), lambda i: (0, G_KV[0] // 512)),
                  pl.BlockSpec((N_META, D), lambda i: (0, 0)), pl.BlockSpec((1, D), lambda i: (0, 0))],
        out_specs=(pl.BlockSpec((N_META, D), lambda i: (0, 0)), pl.BlockSpec((1, D), lambda i: (0, 0))),
        compiler_params=_params(("arbitrary",), 32),
    )(*_pin(dam, ddm, w_full, w_full, meta_full, norm_g))


def _dxn(d_groups, w_full, x2, dy, norm_g, dg_init, tm):
    rows = x2.shape[0]
    groups = (G_CONV, G_CZ, G_Q, G_KV, G_E)

    def body(da, db, dq, dd, de, w_hbm, x_ref, dy_ref, g_ref, gi_ref, gx_o, dg_o, w_vmem, sem):
        @pl.when(pl.program_id(0) == 0)
        def _():
            cp = pltpu.make_async_copy(w_hbm, w_vmem, sem)
            cp.start()
            cp.wait()
            dg_o[...] = gi_ref[...]

        dxn = jnp.zeros((tm, D), F32)
        for ref, (off, wd) in zip((da, db, dq, dd, de), groups):
            for c0 in range(0, wd, 512):
                dxn = dxn + _dot_nt(ref[:, c0:c0 + 512], w_vmem[:, off + c0:off + c0 + 512])
        v = x_ref[...]
        r = lax.rsqrt(jnp.mean(v * v, axis=-1, keepdims=True) + EPS)
        gx_o[...] = dy_ref[...] + _rms_bwd(dxn, v, r, g_ref[...])
        dg_o[...] += jnp.sum(dxn * v * r, axis=0, keepdims=True)

    row = lambda wd: pl.BlockSpec((tm, wd), lambda i: (i, 0))
    vec = pl.BlockSpec((1, D), lambda i: (0, 0))
    return pl.pallas_call(
        body, name="dxn", grid=(rows // tm,),
        out_shape=(_sds((rows, D), F32), _sds((1, D), F32)),
        in_specs=[row(wd) for _, wd in groups] + [pl.BlockSpec(memory_space=pl.ANY), row(D), row(D), vec, vec],
        out_specs=(row(D), vec),
        scratch_shapes=[pltpu.VMEM((D, IN_DIM), BF16), pltpu.SemaphoreType.DMA],
        compiler_params=_params(("arbitrary",), 56),
    )(*_pin(*d_groups, w_full, x2, dy, norm_g, dg_init))


def _wgrad(at, b, bufs, slot, col_off, name, meta=None):
    buf, bufb = bufs
    rows, n = b.shape
    tn = next(t for t in (1536, 1024, 512) if n % t == 0 and col_off % t == 0)
    tk = min(2048, rows)
    nk = rows // tk
    j0 = col_off // tn

    def body(*refs):
        if meta is None:
            at_ref, b_ref, _, _, o_ref, ob_ref = refs
        else:
            at_ref, b_ref, xm_ref, dm_ref, _, _, o_ref, ob_ref = refs
        k = pl.program_id(1)

        @pl.when(k == 0)
        def _():
            if meta is None:
                o_ref[0] = jnp.zeros((D, tn), F32)
            else:
                dm = dm_ref[0]
                for e in range(1, dm_ref.shape[0]):
                    dm = dm + dm_ref[e]
                dm = jnp.concatenate([dm, jnp.zeros((128 - N_META, tn), F32)], axis=0)
                o_ref[0] = _dot(xm_ref[...], dm.astype(BF16))

        o_ref[0] += _dot(at_ref[...], b_ref[...].astype(BF16))

        @pl.when(k == nk - 1)
        def _():
            ob_ref[0] = o_ref[0].astype(BF16)

    in_specs = [pl.BlockSpec((D, tk), lambda j, k: (0, k)), pl.BlockSpec((tk, tn), lambda j, k: (k, j))]
    args = [at, b]
    if meta is not None:
        xmt, dm = meta
        in_specs += [pl.BlockSpec((D, 128), lambda j, k: (0, 0)),
                     pl.BlockSpec((dm.shape[0], N_META, tn), lambda j, k: (0, 0, j))]
        args += [xmt, dm]
    in_specs += [pl.BlockSpec(memory_space=pl.ANY)] * 2
    args += [buf, bufb]
    blk = pl.BlockSpec((1, D, tn), lambda j, k: (slot, 0, j0 + j))
    return pl.pallas_call(
        body, name=name, grid=(n // tn, nk),
        out_shape=(_sds(buf.shape, F32), _sds(buf.shape, BF16)),
        in_specs=in_specs,
        out_specs=(blk, blk),
        input_output_aliases={len(args) - 2: 0, len(args) - 1: 1},
        compiler_params=_params(("parallel", "arbitrary"), 56),
    )(*_pin(*args))


def _rope_tables(s_len):
    pos = jnp.arange(s_len, dtype=jnp.int32)
    row_ids = (pos // GRID_W).astype(F32)
    col_ids = (pos % GRID_W).astype(F32)
    inv_freq = ROPE_THETA ** (-jnp.arange(ROPE_FREQS, dtype=F32) / ROPE_FREQS)
    a_row = row_ids[:, None] * inv_freq[None, :]
    a_col = col_ids[:, None] * inv_freq[None, :]
    ang = jnp.concatenate([a_row, a_row, a_col, a_col], axis=-1)
    return jnp.tile(jnp.cos(ang), (1, GQA)), jnp.tile(jnp.sin(ang), (1, GQA))


def _local_step(x, loss_target, norm_g, conv_b, cn_g, cn_b, q_g, k_g, w_full, w3_full, conv_w_full, meta_full,
                reduce_start=None):
    nb, s_len, _ = x.shape
    rows = nb * s_len
    x2 = x.reshape(rows, D)
    t2 = loss_target.reshape(rows, D)
    cos, sin = _rope_tables(s_len)
    qg = jnp.tile(q_g, (1, GQA))
    kg = jnp.tile(k_g, (1, N_KV))

    xnmt, pm_conv, pm_kv = _meta_fwd(meta_full, norm_g, w_full)
    pconv, pcz, pq, pkv, pe, xnt, q2, qst = _in_proj(x2, norm_g, w_full, qg, cos, sin, nb, 256)
    c0 = _conv_fwd(pconv, pm_conv, conv_w_full, conv_b, nb, 256)
    tq = min(512, s_len)
    kv4 = _kv_prep(pkv, pm_kv, kg, cos, sin, nb)
    o, lse = _attn_fwd(q2, kv4, nb, min(1024, s_len))
    if callable(w3_full):
        w3_full = w3_full(o)
    dy, mt, c3t, o2t, dyc, dya, d_o, dc0, dcz, de, sums = _mid(x2, t2, c0, pcz, o, pe, w3_full, cn_g, cn_b, 256)
    doe, dot_ = _do_prep(d_o, o, 256)
    dqr, dkt, dvt, dkt2, dvt2 = _attn_bwd(q2, qst, kv4, doe, dot_, lse, nb, tq)
    dq, dqg = _q_post(dqr, pq, qg, cos, sin, nb, 256)
    dd, ddm, dkg = _kv_bwd(dkt.reshape(nb, GROUP_W, s_len), dvt.reshape(nb, GROUP_W, s_len),
                           dkt2.reshape(nb, GROUP_W, KEY_PAD), dvt2.reshape(nb, GROUP_W, KEY_PAD),
                           pkv, pm_kv, kg, cos, sin, nb)
    da, dam, gcw = _conv_bwd(dc0, pconv, pm_conv, conv_w_full, nb, 256)
    gmeta, dng_m = _meta_bwd(dam, ddm, w_full, meta_full, norm_g)

    gw3 = (lax.empty((3, D, D), F32), lax.empty((3, D, D), BF16))
    gw3 = _wgrad(c3t, dyc, gw3, 0, 0, "wgrad_conv_out")
    gw3 = _wgrad(o2t, dya, gw3, 1, 0, "wgrad_attn_out")
    gw3 = _wgrad(mt, dy, gw3, 2, 0, "wgrad_out")
    gwin = (lax.empty((1, D, IN_DIM), F32), lax.empty((1, D, IN_DIM), BF16))
    gwin = _wgrad(xnt, da, gwin, 0, G_CONV[0], "wgrad_in_conv", meta=(xnmt, dam))
    gwin = _wgrad(xnt, dcz, gwin, 0, G_CZ[0], "wgrad_in_cz")
    gwin = _wgrad(xnt, dq, gwin, 0, G_Q[0], "wgrad_in_q")
    gwin = _wgrad(xnt, dd, gwin, 0, G_KV[0], "wgrad_in_kv", meta=(xnmt, ddm))
    gwin = _wgrad(xnt, de, gwin, 0, G_E[0], "wgrad_in_e")

    pending = None
    if reduce_start is not None:
        token, pending = reduce_start(gwin, gw3, gcw, gmeta)
        dng_m = dng_m + token[0:1, 0:1]
    gx, dng = _dxn((da, dcz, dq, dd, de), w_full, x2, dy, norm_g, dng_m, 512)

    zeros = jnp.zeros((1, D - 2 * GROUP_W), F32)
    smalls = jnp.concatenate([dng, sums[2:3], sums[0:1], sums[1:2], jnp.concatenate([dqg, dkg, zeros], axis=1),
                              sums[3:4], jnp.zeros((2, D), F32)], axis=0)
    return gx.reshape(nb, s_len, D), gwin, gw3, gcw, gmeta, smalls, pending


def _xyc():
    return lax.axis_index("x"), lax.axis_index("y"), lax.axis_index("c")


def _reduce_sibling(gwin, gw3v, gcm):
    def body(gwin_ref, gw3_ref, gcm_ref, r_win, r_w3, r_cm, send, recv):
        x, y, c = _xyc()
        o = 1 - c
        half = D // 2
        outs = ((gwin_ref.at[pl.ds(o * half, half), :], r_win), (gw3_ref.at[:, :, o], r_w3), (gcm_ref.at[o], r_cm))
        cps = []
        for a, (src, dst) in enumerate(outs):
            cp = pltpu.make_async_remote_copy(src_ref=src, dst_ref=dst, send_sem=send.at[a], recv_sem=recv.at[a],
                                              device_id=(x, y, o), device_id_type=MESH)
            cp.start()
            cps.append(cp)
        for cp in cps:
            cp.wait()

    any_spec = pl.BlockSpec(memory_space=pl.ANY)
    return pl.pallas_call(
        body, name="reduce_sibling",
        out_shape=(_sds((D // 2, IN_DIM), BF16), _sds((3, 4, 128, D), BF16),
                   _sds((24, D), F32)),
        in_specs=[any_spec] * 3, out_specs=(any_spec,) * 3,
        scratch_shapes=[pltpu.SemaphoreType.DMA((3,)), pltpu.SemaphoreType.DMA((3,))],
    )(*_pin(gwin, gw3v, gcm))


def _add_sibling(gwin, gw3v, gcm, r_win, r_w3, r_cm):
    c = lax.axis_index("c").astype(jnp.int32).reshape(1)
    half = D // 2
    tr = 64

    def body1(c_ref, a_ref, b_ref, o_ref):
        o_ref[...] = (a_ref[...] + b_ref[...].astype(F32)).astype(BF16)

    cs_win = pl.pallas_call(
        body1, name="add_sibling_w_in", out_shape=_sds((half, IN_DIM), BF16),
        grid_spec=pltpu.PrefetchScalarGridSpec(
            num_scalar_prefetch=1, grid=(half // tr,),
            in_specs=[pl.BlockSpec((tr, IN_DIM), lambda i, c_ref: (c_ref[0] * (half // tr) + i, 0)),
                      pl.BlockSpec((tr, IN_DIM), lambda i, c_ref: (i, 0))],
            out_specs=pl.BlockSpec((tr, IN_DIM), lambda i, c_ref: (i, 0))),
        compiler_params=_params(("parallel",), 32),
    )(c, *_pin(gwin, r_win))

    def body2(c_ref, a_ref, b_ref, o_ref):
        o_ref[0, 0] = (a_ref[0, 0, 0] + b_ref[0, 0].astype(F32)).astype(BF16)

    cs_w3 = pl.pallas_call(
        body2, name="add_sibling_w3", out_shape=_sds((3, 4, 128, D), BF16),
        grid_spec=pltpu.PrefetchScalarGridSpec(
            num_scalar_prefetch=1, grid=(3, 4),
            in_specs=[pl.BlockSpec((1, 1, 1, 128, D), lambda w, s, c_ref: (w, s, c_ref[0], 0, 0)),
                      pl.BlockSpec((1, 1, 128, D), lambda w, s, c_ref: (w, s, 0, 0))],
            out_specs=pl.BlockSpec((1, 1, 128, D), lambda w, s, c_ref: (w, s, 0, 0))),
        compiler_params=_params(("parallel", "parallel"), 32),
    )(c, *_pin(gw3v, r_w3))

    def body3(c_ref, a_ref, b_ref, o_ref):
        o_ref[...] = a_ref[0] + b_ref[...]

    cs_cm = pl.pallas_call(
        body3, name="add_sibling_cm", out_shape=_sds((24, D), F32),
        grid_spec=pltpu.PrefetchScalarGridSpec(
            num_scalar_prefetch=1, grid=(1,),
            in_specs=[pl.BlockSpec((1, 24, D), lambda i, c_ref: (c_ref[0], 0, 0)),
                      pl.BlockSpec((24, D), lambda i, c_ref: (0, 0))],
            out_specs=pl.BlockSpec((24, D), lambda i, c_ref: (0, 0))),
        compiler_params=_params(("arbitrary",), 32),
    )(c, *_pin(gcm, r_cm))
    return cs_win, cs_w3, cs_cm


def _reduce_chips_copies(srcs, lands, send, recv):
    win_ref, w3_ref, cm_ref = srcs
    r_win, r_w3, r_cm = lands
    x, y, c = _xyc()
    peers = ((1 - x, y), (x, 1 - y), (1 - x, 1 - y))
    cps = []
    for k, (px, py) in enumerate(peers):
        ps = 2 * px + py
        items = ((win_ref.at[:, pl.ds(ps * W_IN_SHARD, W_IN_SHARD)], r_win.at[k]),
                 (w3_ref.at[:, ps], r_w3.at[k]),
                 (cm_ref.at[:, pl.ds(ps * ROW_SHARD, ROW_SHARD)], r_cm.at[k]))
        for a, (src, dst) in enumerate(items):
            cps.append(pltpu.make_async_remote_copy(src_ref=src, dst_ref=dst, send_sem=send.at[3 * a + k],
                                                    recv_sem=recv.at[3 * a + k], device_id=(px, py, c),
                                                    device_id_type=MESH))
    return cps


_HBM = pl.BlockSpec(memory_space=pltpu.HBM)
_SEM = pl.BlockSpec(memory_space=pltpu.SEMAPHORE)
_EFFECT = pltpu.SideEffectType.DATAFLOW_SIDE_EFFECTING


def _reduce_chips_start(cs_win, cs_w3, cs_cm):
    srcs = (cs_win, cs_w3, cs_cm)
    lands = (lax.empty((3, D // 2, W_IN_SHARD), BF16), lax.empty((3, 3, 128, D), BF16),
             lax.empty((3, 24, ROW_SHARD), F32))

    def body(*refs):
        srcs_in, lands_in, send, recv, token = refs[0:3], refs[3:6], refs[6], refs[7], refs[14]
        for cp in _reduce_chips_copies(srcs_in, lands_in, send, recv):
            cp.start()
        token[...] = jnp.zeros_like(token)

    hbm = lambda a: pltpu.HBM(a.shape, a.dtype)
    outs = pl.pallas_call(
        body, name="reduce_chips_start",
        out_shape=(pltpu.SemaphoreType.DMA((9,)), pltpu.SemaphoreType.DMA((9,)),
                   *[hbm(a) for a in srcs], *[hbm(a) for a in lands], jax.ShapeDtypeStruct((8, 128), F32)),
        in_specs=[_HBM] * 6,
        out_specs=(_SEM, _SEM, *[_HBM] * 6, pl.BlockSpec(memory_space=pltpu.VMEM)),
        input_output_aliases={i: i + 2 for i in range(6)},
        compiler_params=pltpu.CompilerParams(has_side_effects=_EFFECT),
    )(*[pltpu.with_memory_space_constraint(a, pltpu.HBM) for a in srcs + lands])
    return outs[0], outs[1], outs[2:5], outs[5:8], outs[8]


def _reduce_chips_wait(send, recv, srcs, lands, after):
    def body(*refs):
        srcs_in, lands_in, send_ref, recv_ref = refs[0:3], refs[3:6], refs[6], refs[7]
        for cp in _reduce_chips_copies(srcs_in, lands_in, send_ref, recv_ref):
            cp.wait_send()
            cp.wait_recv()

    hbm = lambda a: pltpu.HBM(a.shape, a.dtype)
    outs = pl.pallas_call(
        body, name="reduce_chips_wait",
        out_shape=(*[hbm(a) for a in srcs], *[hbm(a) for a in lands]),
        in_specs=[_HBM] * 6 + [_SEM, _SEM, pl.BlockSpec(memory_space=pl.ANY)],
        out_specs=(_HBM,) * 6,
        input_output_aliases={i: i for i in range(6)},
        compiler_params=pltpu.CompilerParams(has_side_effects=_EFFECT),
    )(*srcs, *lands, send, recv, after)
    return outs[0:3], outs[3:6]


def _add_chips(cs_win, cs_w3, cs_cm, r_win, r_w3, r_cm):
    x, y, c = _xyc()
    idx = jnp.stack([2 * x + y, c]).astype(jnp.int32)
    half = D // 2
    tr = 128

    def body1(i_ref, a_ref, b_ref, o_ref):
        f = lambda v: v.astype(F32)
        o_ref[0] = (f(a_ref[...]) + f(b_ref[2])) + (f(b_ref[0]) + f(b_ref[1]))

    f_win = pl.pallas_call(
        body1, name="add_chips_w_in", out_shape=_sds((2, half, W_IN_SHARD), F32),
        grid_spec=pltpu.PrefetchScalarGridSpec(
            num_scalar_prefetch=1, grid=(half // tr,),
            in_specs=[pl.BlockSpec((tr, W_IN_SHARD), lambda i, r: (i, r[0])),
                      pl.BlockSpec((3, tr, W_IN_SHARD), lambda i, r: (0, i, 0))],
            out_specs=pl.BlockSpec((1, tr, W_IN_SHARD), lambda i, r: (r[1], i, 0))),
        compiler_params=_params(("parallel",), 32),
    )(idx, *_pin(cs_win, r_win))

    def body2(i_ref, a_ref, b_ref, o_ref):
        f = lambda v: v.astype(F32)
        o_ref[0, 0] = (f(a_ref[0, 0]) + f(b_ref[2, 0])) + (f(b_ref[0, 0]) + f(b_ref[1, 0]))

    f_w3 = pl.pallas_call(
        body2, name="add_chips_w3", out_shape=_sds((3, 2, 128, D), F32),
        grid_spec=pltpu.PrefetchScalarGridSpec(
            num_scalar_prefetch=1, grid=(3,),
            in_specs=[pl.BlockSpec((1, 1, 128, D), lambda w, r: (w, r[0], 0, 0)),
                      pl.BlockSpec((3, 1, 128, D), lambda w, r: (0, w, 0, 0))],
            out_specs=pl.BlockSpec((1, 1, 128, D), lambda w, r: (w, r[1], 0, 0))),
        compiler_params=_params(("parallel",), 32),
    )(idx, *_pin(cs_w3, r_w3))

    def body3(i_ref, a_ref, b_ref, o_ref):
        o_ref[0] = (a_ref[...] + b_ref[2]) + (b_ref[0] + b_ref[1])

    f_cm = pl.pallas_call(
        body3, name="add_chips_cm", out_shape=_sds((2, 24, ROW_SHARD), F32),
        grid_spec=pltpu.PrefetchScalarGridSpec(
            num_scalar_prefetch=1, grid=(1,),
            in_specs=[pl.BlockSpec((24, ROW_SHARD), lambda i, r: (0, r[0])),
                      pl.BlockSpec((3, 24, ROW_SHARD), lambda i, r: (0, 0, 0))],
            out_specs=pl.BlockSpec((1, 24, ROW_SHARD), lambda i, r: (r[1], 0, 0))),
        compiler_params=_params(("arbitrary",), 32),
    )(idx, *_pin(cs_cm, r_cm))
    return f_win, f_w3, f_cm


def _share_sibling(f_win, f_w3, f_cm, smalls):
    def body(win_in, w3_in, cm_in, sm_ref, win_ref, w3_ref, cm_ref, r_sm, send, recv, ssend, srecv, lsem):
        x, y, c = _xyc()
        o = 1 - c
        cps = []
        for a, (ref, sl) in enumerate(((win_ref, lambda h: win_ref.at[h]), (w3_ref, lambda h: w3_ref.at[:, h]),
                                       (cm_ref, lambda h: cm_ref.at[h]))):
            cp = pltpu.make_async_remote_copy(src_ref=sl(c), dst_ref=sl(c), send_sem=send.at[a], recv_sem=recv.at[a],
                                              device_id=(x, y, o), device_id_type=MESH)
            cp.start()
            cps.append((cp, sl))
        me = 4 * x + 2 * y + c
        loc = pltpu.make_async_copy(sm_ref, r_sm.at[me], lsem)
        loc.start()
        scps = []
        for d in range(1, 8):
            px, py, pc = (x + (d >> 2)) % 2, (y + ((d >> 1) & 1)) % 2, (c + (d & 1)) % 2
            cp = pltpu.make_async_remote_copy(src_ref=sm_ref, dst_ref=r_sm.at[me], send_sem=ssend.at[d - 1],
                                              recv_sem=srecv.at[d - 1], device_id=(px, py, pc), device_id_type=MESH)
            cp.start()
            scps.append((cp, 4 * px + 2 * py + pc))
        for a, (cp, sl) in enumerate(cps):
            pltpu.make_async_remote_copy(src_ref=sl(o), dst_ref=sl(o), send_sem=send.at[a], recv_sem=recv.at[a],
                                         device_id=(x, y, o), device_id_type=MESH).wait_recv()
            cp.wait_send()
        for d, (cp, pid) in enumerate(scps):
            pltpu.make_async_remote_copy(src_ref=sm_ref, dst_ref=r_sm.at[pid], send_sem=ssend.at[d],
                                         recv_sem=srecv.at[d], device_id=(x, y, c), device_id_type=MESH).wait_recv()
            cp.wait_send()
        loc.wait()

    any_spec = pl.BlockSpec(memory_space=pl.ANY)
    return pl.pallas_call(
        body, name="share_sibling",
        out_shape=(_sds(f_win.shape, F32), _sds(f_w3.shape, F32), _sds(f_cm.shape, F32), _sds((8, 8, D), F32)),
        in_specs=[any_spec] * 4, out_specs=(any_spec,) * 4,
        input_output_aliases={0: 0, 1: 1, 2: 2},
        scratch_shapes=[pltpu.SemaphoreType.DMA((3,)), pltpu.SemaphoreType.DMA((3,)),
                        pltpu.SemaphoreType.DMA((7,)), pltpu.SemaphoreType.DMA((7,)), pltpu.SemaphoreType.DMA],
    )(*_pin(f_win, f_w3, f_cm, smalls))


def _adamw_math(w, g, m, v):
    m = ADAM_B1 * m + (1.0 - ADAM_B1) * g
    v = ADAM_B2 * v + (1.0 - ADAM_B2) * (g * g)
    m_hat = m / (1.0 - ADAM_B1 ** ADAM_STEP)
    v_hat = v / (1.0 - ADAM_B2 ** ADAM_STEP)
    delta = -ADAM_LR * (m_hat / (jnp.sqrt(v_hat) + ADAM_EPS) + ADAM_WD * w)
    return delta, m, v


def _adamw(w, g, m, v, tr, name):
    rows, cols = w.shape

    def body(w_ref, g_ref, m_ref, v_ref, d_o, m_o, v_o):
        d_o[...], m_o[...], v_o[...] = _adamw_math(w_ref[...], g_ref[...], m_ref[...], v_ref[...])

    spec = pl.BlockSpec((tr, cols), lambda i: (i, 0))
    return pl.pallas_call(
        body, name=name, grid=(rows // tr,),
        out_shape=(_sds((rows, cols), F32),) * 3,
        in_specs=[spec] * 4, out_specs=(spec,) * 3,
        compiler_params=_params(("parallel",), 32),
    )(*_pin(w, g, m, v))


def _adamw3(g3, ws, ms, vs):
    def body(g_ref, *refs):
        w_refs, m_refs, v_refs, outs = refs[0:3], refs[3:6], refs[6:9], refs[9:]
        g_os, d_os, m_os, v_os = outs[0:3], outs[3:6], outs[6:9], outs[9:12]
        for i in range(3):
            g = g_ref[i]
            g_os[i][0] = g
            d_os[i][0], m_os[i][0], v_os[i][0] = _adamw_math(w_refs[i][0], g, m_refs[i][0], v_refs[i][0])

    return pl.pallas_call(
        body, name="adamw_w3", out_shape=(jax.ShapeDtypeStruct((1, ROW_SHARD, D), F32),) * 12,
        compiler_params=pltpu.CompilerParams(vmem_limit_bytes=48 << 20),
    )(g3, *ws, *ms, *vs)


def _adamw_cm(f_cm, ws, ms, vs):
    def body(f_ref, *refs):
        w_refs, m_refs, v_refs, outs = refs[0:2], refs[2:4], refs[4:6], refs[6:14]
        gcw, gmt = refs[14], refs[15]
        gcw[0:16] = f_ref[0, 0:16]
        gcw[16:32] = f_ref[1, 0:16]
        gmt[0:8] = f_ref[0, 16:24]
        gmt[8:16] = f_ref[1, 16:24]
        g_conv = gcw[0:CONV_K, :]
        g_meta = gmt[...]
        outs[0][0] = g_conv
        outs[1][...] = g_meta
        outs[2][0], outs[4][0], outs[6][0] = _adamw_math(w_refs[0][0], g_conv, m_refs[0][0], v_refs[0][0])
        outs[3][...], outs[5][...], outs[7][...] = _adamw_math(w_refs[1][...], g_meta, m_refs[1][...], v_refs[1][...])

    pair = (jax.ShapeDtypeStruct((1, CONV_K, ROW_SHARD), F32), jax.ShapeDtypeStruct((N_META, ROW_SHARD), F32))
    return pl.pallas_call(
        body, name="adamw_cm", out_shape=pair * 4,
        scratch_shapes=[pltpu.VMEM((32, ROW_SHARD), F32), pltpu.VMEM((N_META, ROW_SHARD), F32)],
    )(f_cm, *ws, *ms, *vs)


def _adamw_small(r_sm, ws, ms, vs):
    def body(s_ref, *refs):
        w_refs, m_refs, v_refs, outs = refs[0:6], refs[6:12], refs[12:18], refs[18:]
        loss_o, g_os, d_os, m_os, v_os = outs[0], outs[1:7], outs[7:13], outs[13:19], outs[19:25]
        g = s_ref[0]
        for dev in range(1, 8):
            g = g + s_ref[dev]
        qk = g[4:5, :]
        qg = qk[:, 0:HEAD_DIM]
        kg = qk[:, GROUP_W:GROUP_W + HEAD_DIM]
        for h in range(1, GQA):
            qg = qg + qk[:, HEAD_DIM * h:HEAD_DIM * (h + 1)]
            kg = kg + qk[:, GROUP_W + HEAD_DIM * h:GROUP_W + HEAD_DIM * (h + 1)]
        loss_o[...] = (0.5 / D) * jnp.sum(g[5:6, :], axis=-1, keepdims=True)
        for i, gi in enumerate((g[0:1], g[1:2], g[2:3], g[3:4], qg, kg)):
            g_os[i][...] = gi
            d_os[i][...], m_os[i][...], v_os[i][...] = _adamw_math(w_refs[i][...], gi, m_refs[i][...], v_refs[i][...])

    six = tuple(jax.ShapeDtypeStruct(w.shape, F32) for w in ws)
    return pl.pallas_call(
        body, name="adamw_small", out_shape=(jax.ShapeDtypeStruct((1, 1), F32),) + six * 4,
    )(r_sm, *ws, *ms, *vs)


def kernel(x, meta_tokens, norm_g, w_in, conv_w, conv_b, conv_norm_g, conv_norm_b, w_conv_out, q_norm_g, k_norm_g, w_attn_out, w_out, loss_target, m_meta_tokens, m_norm_g, m_w_in, m_conv_w, m_conv_b, m_conv_norm_g, m_conv_norm_b, m_w_conv_out, m_q_norm_g, m_k_norm_g, m_w_attn_out, m_w_out, v_meta_tokens, v_norm_g, v_w_in, v_conv_w, v_conv_b, v_conv_norm_g, v_conv_norm_b, v_w_conv_out, v_q_norm_g, v_k_norm_g, v_w_attn_out, v_w_out):
    pad_k = lambda a: jnp.pad(a[0], ((0, 32 - CONV_K), (0, 0)))
    w3_s = (w_conv_out, w_attn_out, w_out)
    w_full, conv_w_full, meta_full, w3b, w3_land = _gather_weights(w_in[0], w3_s, pad_k(conv_w), meta_tokens)
    w3_pending, token = _w3_start(w3b, w3_land)
    norm_g_fwd = norm_g + token[0:1, 0:1]

    def w3_full(after):
        return _w3_wait(*w3_pending, after)

    def reduce_start(gwin, gw3, gcw, gmeta):
        gwin2, gwin2b = (a.reshape(D, IN_DIM) for a in gwin)
        gw3v, gw3vb = (a.reshape(3, N_CHIPS, 2, 128, D) for a in gw3)
        gcm = jnp.concatenate([gcw.reshape(2, 16, D), gmeta.reshape(2, 8, D)], axis=1)
        r_win, r_w3, r_cm = _reduce_sibling(gwin2b, gw3vb, gcm)
        cs = _add_sibling(gwin2, gw3v, gcm, r_win, r_w3, r_cm)
        send, recv, srcs, lands, token = _reduce_chips_start(*cs)
        return token, (send, recv, srcs, lands)

    gx, _, _, _, _, smalls, pending = _local_step(
        x, loss_target, norm_g_fwd, conv_b, conv_norm_g, conv_norm_b, q_norm_g, k_norm_g,
        w_full, w3_full, conv_w_full, meta_full, reduce_start)
    (cs_win, cs_w3, cs_cm), (r2_win, r2_w3, r2_cm) = _reduce_chips_wait(*pending, gx)
    f_win, f_w3, f_cm = _add_chips(cs_win, cs_w3, cs_cm, r2_win, r2_w3, r2_cm)
    f_win, f_w3, f_cm, r_sm = _share_sibling(f_win, f_w3, f_cm, smalls)

    g_w_in = f_win.reshape(D, W_IN_SHARD)
    d_w_in, nm_w_in, nv_w_in = _adamw(w_in[0], g_w_in, m_w_in[0], v_w_in[0], 128, "adamw_w_in")
    w3 = _adamw3(f_w3.reshape(3, ROW_SHARD, D), w3_s, (m_w_conv_out, m_w_attn_out, m_w_out),
                 (v_w_conv_out, v_w_attn_out, v_w_out))
    cm = _adamw_cm(f_cm, (conv_w, meta_tokens), (m_conv_w, m_meta_tokens), (v_conv_w, v_meta_tokens))
    small = _adamw_small(
        r_sm, (norm_g, conv_b, conv_norm_g, conv_norm_b, q_norm_g, k_norm_g),
        (m_norm_g, m_conv_b, m_conv_norm_g, m_conv_norm_b, m_q_norm_g, m_k_norm_g),
        (v_norm_g, v_conv_b, v_conv_norm_g, v_conv_norm_b, v_q_norm_g, v_k_norm_g))

    def assemble(big_in, w3x, cmx, s6):
        ng, cb, cng, cnb, qg, kg = s6
        return (cmx[1], ng, big_in[None], cmx[0], cb, cng, cnb, w3x[0], qg, kg, w3x[1], w3x[2])

    loss = small[0].reshape(())
    grads = assemble(g_w_in, w3[0:3], cm[0:2], small[1:7])
    deltas = assemble(d_w_in, w3[3:6], cm[2:4], small[7:13])
    new_m = assemble(nm_w_in, w3[6:9], cm[4:6], small[13:19])
    new_v = assemble(nv_w_in, w3[9:12], cm[6:8], small[19:25])
    return (loss, gx, *grads, *deltas, *new_m, *new_v)
```

```python
import functools
import math

import jax
import jax.numpy as jnp
from jax import lax
from jax.experimental import pallas as pl
from jax.experimental.pallas import tpu as pltpu

F32, BF16 = jnp.float32, jnp.bfloat16
MESH = pl.DeviceIdType.MESH

D = 1024
N_META = 16
CONV_K = 31
N_KV = 4
GQA = 4
HEAD_DIM = 64
GROUP_W = GQA * HEAD_DIM
GRID_W = 64
ROPE_FREQS = 16
ROPE_THETA = 10000.0
EPS = 1e-6
IN_DIM = 7680
KEY_PAD = 128
G_CONV, G_CZ, G_Q, G_KV, G_E = (0, 2048), (2048, 1024), (3072, 1024), (4096, 512), (4608, 3072)
N_CHIPS = 4
W_IN_SHARD = IN_DIM // N_CHIPS
ROW_SHARD = D // N_CHIPS

ADAM_LR, ADAM_B1, ADAM_B2, ADAM_EPS, ADAM_WD, ADAM_STEP = 0.001, 0.9, 0.999, 1e-08, 0.01, 10

NT_DIMS = (((1,), (1,)), ((), ()))


def _params(sem=None, vmem_mb=48):
    return pltpu.CompilerParams(dimension_semantics=sem, vmem_limit_bytes=vmem_mb << 20)


def _sds(shape, dtype):
    return pltpu.HBM(tuple(shape), dtype)


def _pin(*arrays):
    return [pltpu.with_memory_space_constraint(a, pltpu.HBM) for a in arrays]


def _sig(v):
    return jax.nn.sigmoid(v)


def _dsilu(v, s):
    return s * (1.0 + v * (1.0 - s))


def _dot(a, b):
    return jnp.dot(a, b, preferred_element_type=F32)


def _dot_nt(a, b):
    return lax.dot_general(a, b, NT_DIMS, preferred_element_type=F32)


def _qk_mats():
    i = lax.broadcasted_iota(jnp.int32, (GROUP_W, GROUP_W), 0)
    j = lax.broadcasted_iota(jnp.int32, (GROUP_W, GROUP_W), 1)
    mean = jnp.where((i >> 6) == (j >> 6), 1.0 / HEAD_DIM, 0.0).astype(BF16)
    turn = jnp.where((i == j + 16) & ((j & 16) == 0), -1.0,
                     jnp.where((i == j - 16) & ((j & 16) != 0), 1.0, 0.0)).astype(BF16)
    return mean, turn


def _apply(v, mat):
    hi = v.astype(BF16)
    lo = (v - hi.astype(F32)).astype(BF16)
    return _dot(hi, mat) + _dot(lo, mat)


def _qk_fwd(v, g, cos, sin, mats):
    mean, turn = mats
    r = lax.rsqrt(_apply(v * v, mean) + EPS)
    n = v * r * g
    return n * cos + _apply(n, turn) * sin, r


def _qk_bwd(dy, v, r, g, cos, sin, mats):
    mean, turn = mats
    dn = dy * cos - _apply(dy, turn) * sin
    dyg = dn * g
    dv = r * dyg - v * (r * r * r) * _apply(dyg * v, mean)
    return dv, dn * v * r


def _rms_bwd(dxn, v, r, g):
    dxg = dxn * g
    return r * dxg - v * (r * r * r) * jnp.mean(dxg * v, axis=-1, keepdims=True)


def _glu(a):
    return a[:, :D] * _sig(a[:, D:])


def _gather_weights(w_in_s, w3_s, conv_w_s, meta_s):
    def body(win_ref, wa_ref, wb_ref, wc_ref, cw_ref, mt_ref, win_o, cw_o, mt_o, w3b_o, w3_o, win_b, w3_b,
             send, recv, fsend, frecv, lsem, csem):
        x, y, c = _xyc()
        o = 1 - c
        me = 2 * x + y
        win_b[...] = win_ref[...].astype(BF16)
        for i, ref in enumerate((wa_ref, wb_ref, wc_ref)):
            w3_b[i] = ref[0].astype(BF16)
        cast = pltpu.make_async_copy(w3_b, w3b_o, csem.at[0])
        cast.start()
        own = pltpu.make_async_copy(w3_b, _w3_place(w3_o, me), csem.at[1])
        own.start()
        items = (
            (lambda h: win_b.at[pl.ds(h * 512, 512), :],
             lambda p, h: win_o.at[pl.ds(h * 512, 512), pl.ds(p * W_IN_SHARD, W_IN_SHARD)]),
            (lambda h: cw_ref.at[pl.ds(h * 16, 16), :],
             lambda p, h: cw_o.at[pl.ds(h * 16, 16), pl.ds(p * ROW_SHARD, ROW_SHARD)]),
            (lambda h: mt_ref.at[pl.ds(h * 8, 8), :],
             lambda p, h: mt_o.at[pl.ds(h * 8, 8), pl.ds(p * ROW_SHARD, ROW_SHARD)]),
        )
        peers = ((1 - x, y), (x, 1 - y), (1 - x, 1 - y))

        def remote(src, dst, s_sem, r_sem, to):
            return pltpu.make_async_remote_copy(src_ref=src, dst_ref=dst, send_sem=s_sem, recv_sem=r_sem,
                                                device_id=to, device_id_type=MESH)

        started = []
        for a, (half, place) in enumerate(items):
            for h in range(2):
                loc = pltpu.make_async_copy(half(h), place(me, h), lsem.at[a, h])
                loc.start()
                started.append(loc.wait)
            for k, (px, py) in enumerate(peers):
                cp = remote(half(c), place(me, c), send.at[a, k], recv.at[a, k], (px, py, c))
                cp.start()
                started.append(cp.wait_send)
        for k, (px, py) in enumerate(peers):
            for a, (half, place) in enumerate(items):
                got = place(2 * px + py, c)
                remote(got, got, send.at[a, k], recv.at[a, k], (px, py, c)).wait_recv()
                fw = remote(got, got, fsend.at[a, k], frecv.at[a, k], (x, y, o))
                fw.start()
                started.append(fw.wait_send)
        for k, (px, py) in enumerate(peers):
            for a, (half, place) in enumerate(items):
                theirs = place(2 * px + py, o)
                remote(theirs, theirs, fsend.at[a, k], frecv.at[a, k], (x, y, o)).wait_recv()
        for wait in started:
            wait()
        cast.wait()
        own.wait()

    any_spec = pl.BlockSpec(memory_space=pl.ANY)
    vmem = pl.BlockSpec(memory_space=pltpu.VMEM)
    return pl.pallas_call(
        body, name="gather_weights",
        out_shape=(_sds((D, IN_DIM), BF16), _sds((32, D), F32), _sds((N_META, D), F32),
                   _sds((3, ROW_SHARD, D), BF16), _sds((3, D, D), BF16)),
        in_specs=[vmem] * 6,
        out_specs=(any_spec,) * 5,
        scratch_shapes=[pltpu.VMEM((D, W_IN_SHARD), BF16), pltpu.VMEM((3, ROW_SHARD, D), BF16),
                        pltpu.SemaphoreType.DMA((3, 3)), pltpu.SemaphoreType.DMA((3, 3)),
                        pltpu.SemaphoreType.DMA((3, 3)), pltpu.SemaphoreType.DMA((3, 3)),
                        pltpu.SemaphoreType.DMA((3, 2)), pltpu.SemaphoreType.DMA((2,))],
        compiler_params=pltpu.CompilerParams(vmem_limit_bytes=40 << 20),
    )(w_in_s, *w3_s, conv_w_s, meta_s)


def _w3_place(ref, p):
    return ref.at[:, pl.ds(p * ROW_SHARD, ROW_SHARD), :]


def _w3_copies(w3b_ref, land_ref, send, recv):
    x, y, c = _xyc()
    me = 2 * x + y
    peers = ((1 - x, y), (x, 1 - y), (1 - x, 1 - y))
    return [pltpu.make_async_remote_copy(src_ref=w3b_ref, dst_ref=_w3_place(land_ref, me),
                                         send_sem=send.at[k], recv_sem=recv.at[k], device_id=(px, py, c),
                                         device_id_type=MESH)
            for k, (px, py) in enumerate(peers)]


def _w3_start(w3b, land):
    def body(w3b_ref, land_ref, send, recv, w3b_thru, land_thru, token):
        for cp in _w3_copies(w3b_ref, land_ref, send, recv):
            cp.start()
        token[...] = jnp.zeros_like(token)

    outs = pl.pallas_call(
        body, name="w3_start",
        out_shape=(pltpu.SemaphoreType.DMA((3,)), pltpu.SemaphoreType.DMA((3,)),
                   pltpu.HBM(w3b.shape, BF16), pltpu.HBM(land.shape, BF16), jax.ShapeDtypeStruct((8, 128), F32)),
        in_specs=[_HBM, _HBM],
        out_specs=(_SEM, _SEM, _HBM, _HBM, pl.BlockSpec(memory_space=pltpu.VMEM)),
        input_output_aliases={0: 2, 1: 3},
        compiler_params=pltpu.CompilerParams(has_side_effects=_EFFECT),
    )(*_pin(w3b, land))
    return outs[0:4], outs[4]


def _w3_wait(send, recv, w3b, land, after):
    def body(w3b_ref, land_ref, send_ref, recv_ref, after_ref, w3b_out, land_out):
        x, y, c = _xyc()
        peers = ((1 - x, y), (x, 1 - y), (1 - x, 1 - y))
        for k, (cp, (px, py)) in enumerate(zip(_w3_copies(w3b_ref, land_ref, send_ref, recv_ref), peers)):
            cp.wait_send()
            got = _w3_place(land_ref, 2 * px + py)
            pltpu.make_async_remote_copy(src_ref=got, dst_ref=got, send_sem=send_ref.at[k], recv_sem=recv_ref.at[k],
                                         device_id=(px, py, c), device_id_type=MESH).wait_recv()

    outs = pl.pallas_call(
        body, name="w3_wait",
        out_shape=(pltpu.HBM(w3b.shape, BF16), pltpu.HBM(land.shape, BF16)),
        in_specs=[_HBM, _HBM, _SEM, _SEM, pl.BlockSpec(memory_space=pl.ANY)],
        out_specs=(_HBM, _HBM),
        input_output_aliases={0: 0, 1: 1},
        compiler_params=pltpu.CompilerParams(has_side_effects=_EFFECT),
    )(w3b, land, send, recv, after)
    return outs[1]


def _meta_fwd(meta_full, norm_g, w_full):
    def body(m_ref, g_ref, wc_ref, wkv_ref, xnt_ref, pc_ref, pkv_ref):
        v = m_ref[...]
        r = lax.rsqrt(jnp.mean(v * v, axis=-1, keepdims=True) + EPS)
        xn = v * r * g_ref[...]
        xnb = xn.astype(BF16)
        pad = jnp.concatenate([xn, jnp.zeros((128 - N_META, D), F32)], axis=0)
        xnt_ref[...] = pad.T.astype(BF16)
        pc_ref[...] = _dot(xnb, wc_ref[...])
        pkv_ref[...] = _dot(xnb, wkv_ref[...])

    return pl.pallas_call(
        body, name="meta_fwd", grid=(1,),
        out_shape=(_sds((D, 128), BF16), _sds((N_META, 2048), F32),
                   _sds((N_META, 512), F32)),
        in_specs=[pl.BlockSpec((N_META, D), lambda i: (0, 0)), pl.BlockSpec((1, D), lambda i: (0, 0)),
                  pl.BlockSpec((D, 2048), lambda i: (0, 0)), pl.BlockSpec((D, 512), lambda i: (0, G_KV[0] // 512))],
        out_specs=(pl.BlockSpec((D, 128), lambda i: (0, 0)), pl.BlockSpec((N_META, 2048), lambda i: (0, 0)),
                   pl.BlockSpec((N_META, 512), lambda i: (0, 0))),
        compiler_params=_params(("arbitrary",), 32),
    )(*_pin(meta_full, norm_g, w_full, w_full))


def _in_proj(x2, norm_g, w_full, qg, cos, sin, nb, tm):
    rows = x2.shape[0]
    nt = rows // nb // tm
    groups = (G_CONV, G_CZ, G_Q, G_KV, G_E)
    scale = 1.0 / math.sqrt(HEAD_DIM)

    def body(x_ref, g_ref, w_hbm, qg_ref, cos_ref, sin_ref, *rest):
        outs, xnt_ref, q2_o, qt_o, w_vmem, sem = rest[:5], rest[5], rest[6], rest[7], rest[8], rest[9]

        @pl.when(pl.program_id(0) == 0)
        def _():
            cp = pltpu.make_async_copy(w_hbm, w_vmem, sem)
            cp.start()
            cp.wait()

        v = x_ref[...]
        r = lax.rsqrt(jnp.mean(v * v, axis=-1, keepdims=True) + EPS)
        xn = v * r * g_ref[...]
        xnb = xn.astype(BF16)
        xnt_ref[...] = xn.T.astype(BF16)
        for ref, (off, wd) in zip(outs, groups):
            for c0 in range(0, wd, 512):
                ref[:, c0:c0 + 512] = _dot(xnb, w_vmem[:, off + c0:off + c0 + 512])
        gv, cosv, sinv = qg_ref[...], cos_ref[...], sin_ref[...]
        mats = _qk_mats()
        for g in range(N_KV):
            gs = slice(GROUP_W * g, GROUP_W * (g + 1))
            qr, _ = _qk_fwd(outs[2][:, gs], gv, cosv, sinv, mats)
            q2_o[:, gs] = (qr * (scale * LOG2E)).astype(BF16)
            qt_o[gs, :] = (qr * scale).T.astype(BF16)

    rope = pl.BlockSpec((tm, GROUP_W), lambda i: (lax.rem(i, nt), 0))
    return pl.pallas_call(
        body, name="in_proj", grid=(rows // tm,),
        out_shape=tuple(_sds((rows, wd), F32) for _, wd in groups)
        + (_sds((D, rows), BF16), _sds((rows, D), BF16), _sds((D, rows), BF16)),
        in_specs=[pl.BlockSpec((tm, D), lambda i: (i, 0)), pl.BlockSpec((1, D), lambda i: (0, 0)),
                  pl.BlockSpec(memory_space=pl.ANY), pl.BlockSpec((1, GROUP_W), lambda i: (0, 0)), rope, rope],
        out_specs=tuple(pl.BlockSpec((tm, wd), lambda i: (i, 0)) for _, wd in groups)
        + (pl.BlockSpec((D, tm), lambda i: (0, i)), pl.BlockSpec((tm, D), lambda i: (i, 0)),
           pl.BlockSpec((D, tm), lambda i: (0, i))),
        scratch_shapes=[pltpu.VMEM((D, IN_DIM), BF16), pltpu.SemaphoreType.DMA],
        compiler_params=_params(("arbitrary",), 58),
    )(*_pin(x2, norm_g, w_full, qg, cos, sin))


def _halo_specs(width, tm, nt, rows):
    h16 = tm // 16
    return [pl.BlockSpec((tm, width), lambda b, i: (b * nt + i, 0)),
            pl.BlockSpec((16, width), lambda b, i: (jnp.maximum((b * nt + i) * h16 - 1, 0), 0)),
            pl.BlockSpec((16, width), lambda b, i: (jnp.minimum((b * nt + i + 1) * h16, rows // 16 - 1), 0))]


def _fill_uext(uext, cur, prev, nxt, meta, i, nt, tm):
    uext[0:16] = jnp.where(i == 0, _glu(meta[...]), _glu(prev[...]))
    uext[16:16 + tm] = _glu(cur[...])
    uext[16 + tm:32 + tm] = jnp.where(i == nt - 1, 0.0, _glu(nxt[...]))


def _shifted_copies(dst, src, n):
    for r in range(1, 8):
        dst[r, 0:n] = src[r:r + n]


def _rows32(shifted, src, start, cols):
    q8, r = divmod(start, 8)
    if r == 0:
        return src[start:start + 32, cols]
    return shifted[r, 8 * q8:8 * q8 + 32, cols]


def _conv_fwd(pconv, pm_conv, conv_w, conv_b, nb, tm):
    rows = pconv.shape[0]
    nt = rows // nb // tm

    def body(cur, prev, nxt, meta, w_ref, b_ref, o_ref, uext, ush):
        i = pl.program_id(1)
        _fill_uext(uext, cur, prev, nxt, meta, i, nt, tm)
        _shifted_copies(ush, uext, tm + 24)
        for r0 in range(0, tm, 32):
            for c0 in range(0, D, 256):
                acc = jnp.zeros((32, 256), F32) + b_ref[:, c0:c0 + 256]
                for j in range(CONV_K):
                    acc = acc + _rows32(ush, uext, r0 + j + 1, slice(c0, c0 + 256)) * w_ref[j:j + 1, c0:c0 + 256]
                o_ref[r0:r0 + 32, c0:c0 + 256] = acc

    return pl.pallas_call(
        body, name="conv_fwd", grid=(nb, nt),
        out_shape=_sds((rows, D), F32),
        in_specs=_halo_specs(2048, tm, nt, rows)
        + [pl.BlockSpec((16, 2048), lambda b, i: (0, 0)), pl.BlockSpec((32, D), lambda b, i: (0, 0)),
           pl.BlockSpec((1, D), lambda b, i: (0, 0))],
        out_specs=pl.BlockSpec((tm, D), lambda b, i: (b * nt + i, 0)),
        scratch_shapes=[pltpu.VMEM((tm + 32, D), F32), pltpu.VMEM((8, tm + 24, D), F32)],
        compiler_params=_params(("parallel", "parallel"), 40),
    )(*_pin(pconv, pconv, pconv, pm_conv, conv_w, conv_b))


def _kv_prep(pkv, pm_kv, kg, cos, sin, nb):
    rows = pkv.shape[0]
    s_len = rows // nb
    tk = min(512, s_len)
    nt = s_len // tk

    def body(kv_ref, m_ref, g_ref, cos_ref, sin_ref, k_o, v_o, k2_o, v2_o):
        i = pl.program_id(1)
        mats = _qk_mats()
        kv = kv_ref[...]
        kr, _ = _qk_fwd(kv[:, :GROUP_W], g_ref[...], cos_ref[...], sin_ref[...], mats)
        ones = _ones_cols(tk, tk)
        for h in range(N_KV):
            k_o[0, h] = kr[:, HEAD_DIM * h:HEAD_DIM * (h + 1)].astype(BF16)
            vh = kv[:, GROUP_W + HEAD_DIM * h:GROUP_W + HEAD_DIM * (h + 1)]
            v_o[0, h] = jnp.concatenate([vh, ones], axis=1).astype(BF16)

        @pl.when(i == 0)
        def _():
            kvm = m_ref[...]
            km = kvm[:, :GROUP_W]
            kn = km * lax.rsqrt(_apply(km * km, mats[0]) + EPS) * g_ref[...]
            zeros = jnp.zeros((KEY_PAD - N_META, GROUP_W), F32)
            kfull = jnp.concatenate([kn, zeros], axis=0)
            vfull = jnp.concatenate([kvm[:, GROUP_W:], zeros], axis=0)
            ones_m = _ones_cols(KEY_PAD, N_META)
            for h in range(N_KV):
                k2_o[0, h] = kfull[:, HEAD_DIM * h:HEAD_DIM * (h + 1)].astype(BF16)
                v2_o[0, h] = jnp.concatenate([vfull[:, HEAD_DIM * h:HEAD_DIM * (h + 1)], ones_m], axis=1).astype(BF16)

    return pl.pallas_call(
        body, name="kv_prep", grid=(nb, nt),
        out_shape=(_sds((nb, N_KV, s_len, HEAD_DIM), BF16), _sds((nb, N_KV, s_len, 2 * HEAD_DIM), BF16),
                   _sds((nb, N_KV, KEY_PAD, HEAD_DIM), BF16), _sds((nb, N_KV, KEY_PAD, 2 * HEAD_DIM), BF16)),
        in_specs=[pl.BlockSpec((tk, 512), lambda b, i: (b * nt + i, 0)),
                  pl.BlockSpec((N_META, 512), lambda b, i: (0, 0)), pl.BlockSpec((1, GROUP_W), lambda b, i: (0, 0)),
                  pl.BlockSpec((tk, GROUP_W), lambda b, i: (i, 0)),
                  pl.BlockSpec((tk, GROUP_W), lambda b, i: (i, 0))],
        out_specs=(pl.BlockSpec((1, N_KV, tk, HEAD_DIM), lambda b, i: (b, 0, i, 0)),
                   pl.BlockSpec((1, N_KV, tk, 2 * HEAD_DIM), lambda b, i: (b, 0, i, 0)),
                   pl.BlockSpec((1, N_KV, KEY_PAD, HEAD_DIM), lambda b, i: (b, 0, 0, 0)),
                   pl.BlockSpec((1, N_KV, KEY_PAD, 2 * HEAD_DIM), lambda b, i: (b, 0, 0, 0))),
        compiler_params=_params(("parallel", "arbitrary"), 40),
    )(*_pin(pkv, pm_kv, kg, cos, sin))


def _ones_cols(rows, valid):
    r = lax.broadcasted_iota(jnp.int32, (rows, HEAD_DIM), 0)
    col = lax.broadcasted_iota(jnp.int32, (rows, HEAD_DIM), 1)
    return jnp.where((col < 2) & (r < valid), 1.0, 0.0).astype(F32)


def _tail_bias():
    col = lax.broadcasted_iota(jnp.int32, (1, KEY_PAD), 1)
    return jnp.where(col < N_META, 0.0, -1e30).astype(F32)


LOG2E = 1.4426950408889634


def _kv_specs(s_len):
    return [pl.BlockSpec((1, 1, s_len, HEAD_DIM), lambda b, g, i: (b, g, 0, 0)),
            pl.BlockSpec((1, 1, s_len, 2 * HEAD_DIM), lambda b, g, i: (b, g, 0, 0)),
            pl.BlockSpec((1, 1, KEY_PAD, HEAD_DIM), lambda b, g, i: (b, g, 0, 0)),
            pl.BlockSpec((1, 1, KEY_PAD, 2 * HEAD_DIM), lambda b, g, i: (b, g, 0, 0))]


def _attn_fwd(q2, kv4, nb, tq):
    rows = q2.shape[0]
    s_len = rows // nb
    nq = s_len // tq

    def body(q_ref, k1_ref, v1_ref, k2_ref, v2_ref, o_ref, lse_ref):
        qs = q_ref[...]
        k1, k2, v1, v2 = k1_ref[0, 0], k2_ref[0, 0], v1_ref[0, 0], v2_ref[0, 0]
        bias = _tail_bias()
        outs, lses = [], []

        def scores(h):
            qh = qs[:, HEAD_DIM * h:HEAD_DIM * (h + 1)]
            return _dot_nt(qh, k1), _dot_nt(qh, k2) + bias

        ahead = scores(0)
        for h in range(GQA):
            s1, s2 = ahead
            if h + 1 < GQA:
                ahead = scores(h + 1)
            m = jnp.maximum(jnp.max(s1, axis=-1, keepdims=True), jnp.max(s2, axis=-1, keepdims=True))
            oe = _dot(jnp.exp2(s1 - m).astype(BF16), v1) + _dot(jnp.exp2(s2 - m).astype(BF16), v2)
            l = oe[:, HEAD_DIM:HEAD_DIM + 1]
            outs.append(oe[:, :HEAD_DIM] / l)
            lses.append(m + jnp.log2(l))
        o_ref[...] = jnp.concatenate(outs, axis=1)
        lse_ref[0, 0] = jnp.concatenate(lses, axis=1)

    return pl.pallas_call(
        body, name="attn_fwd", grid=(nb, N_KV, nq),
        out_shape=(_sds((rows, D), F32), _sds((nb, N_KV, s_len, GQA), F32)),
        in_specs=[pl.BlockSpec((tq, GROUP_W), lambda b, g, i: (b * nq + i, g))] + _kv_specs(s_len),
        out_specs=(pl.BlockSpec((tq, GROUP_W), lambda b, g, i: (b * nq + i, g)),
                   pl.BlockSpec((1, 1, tq, GQA), lambda b, g, i: (b, g, i, 0))),
        compiler_params=_params(("parallel", "parallel", "parallel"), 48),
    )(*_pin(q2, *kv4))


def _mid(x2, t2, c0, cz, o, e, w3, cn_g, cn_b, tm):
    rows = x2.shape[0]

    def body(x_ref, t_ref, c0_ref, cz_ref, o_ref, e_ref, w_ref, g_ref, b_ref,
             dy_o, mt_o, c3t_o, o2t_o, dyc_o, dya_o, do_o, dc0_o, dcz_o, de_o, sums_o):
        wco, wao, wo = w_ref[0], w_ref[1], w_ref[2]
        cn_g_v = g_ref[...]
        c0v = c0_ref[...]
        xc = c0v - jnp.mean(c0v, axis=-1, keepdims=True)
        rstd = lax.rsqrt(jnp.mean(xc * xc, axis=-1, keepdims=True) + EPS)
        n = xc * rstd
        c1 = n * cn_g_v + b_ref[...]
        s1 = _sig(c1)
        c2 = c1 * s1
        czv = cz_ref[...]
        sz = _sig(czv)
        gz = czv * sz
        c3 = c2 * gz
        yc = _dot(c3.astype(BF16), wco)
        az, gc, ga = e_ref[:, :D], e_ref[:, D:2 * D], e_ref[:, 2 * D:]
        saz = _sig(az)
        gaz = az * saz
        ov = o_ref[...]
        o2 = ov * gaz
        ya = _dot(o2.astype(BF16), wao)
        sc, sa = _sig(gc), _sig(ga)
        merged = sc * yc + sa * ya
        out = _dot(merged.astype(BF16), wo)
        err = x_ref[...] + out - t_ref[...]
        dy = err * (1.0 / D)
        dy_o[...] = dy
        dm = _dot_nt(dy.astype(BF16), wo)
        dyc = dm * sc
        dya = dm * sa
        dycb, dyab = dyc.astype(BF16), dya.astype(BF16)
        dyc_o[...] = dycb
        dya_o[...] = dyab
        de_o[:, D:2 * D] = (dyc * yc * (1.0 - sc)).astype(BF16)
        de_o[:, 2 * D:] = (dya * ya * (1.0 - sa)).astype(BF16)
        dc3 = _dot_nt(dycb, wco)
        do2 = _dot_nt(dyab, wao)
        do_o[...] = do2 * gaz
        de_o[:, :D] = (do2 * ov * _dsilu(az, saz)).astype(BF16)
        dcz_o[...] = (dc3 * c2 * _dsilu(czv, sz)).astype(BF16)
        dc1 = dc3 * gz * _dsilu(c1, s1)
        dn = dc1 * cn_g_v
        dc0 = rstd * (dn - jnp.mean(dn, axis=-1, keepdims=True) - n * jnp.mean(dn * n, axis=-1, keepdims=True))
        dc0_o[...] = dc0
        mt_o[...] = merged.T.astype(BF16)
        c3t_o[...] = c3.T.astype(BF16)
        o2t_o[...] = o2.T.astype(BF16)

        @pl.when(pl.program_id(0) == 0)
        def _():
            sums_o[...] = jnp.zeros_like(sums_o)

        sums_o[0:1, :] += jnp.sum(dc1 * n, axis=0, keepdims=True)
        sums_o[1:2, :] += jnp.sum(dc1, axis=0, keepdims=True)
        sums_o[2:3, :] += jnp.sum(dc0, axis=0, keepdims=True)
        sums_o[3:4, :] += jnp.sum(err * err, axis=0, keepdims=True)

    row = lambda wd: pl.BlockSpec((tm, wd), lambda i: (i, 0))
    col = pl.BlockSpec((D, tm), lambda i: (0, i))
    vec = pl.BlockSpec((1, D), lambda i: (0, 0))
    f32o = lambda wd: _sds((rows, wd), F32)
    b16o = lambda wd: _sds((rows, wd), BF16)
    tpo = _sds((D, rows), BF16)
    return pl.pallas_call(
        body, name="mid", grid=(rows // tm,),
        out_shape=(f32o(D), tpo, tpo, tpo, b16o(D), b16o(D), f32o(D), f32o(D), b16o(D), b16o(3 * D),
                   _sds((8, D), F32)),
        in_specs=[row(D), row(D), row(D), row(D), row(D), row(3 * D),
                  pl.BlockSpec((3, D, D), lambda i: (0, 0, 0)), vec, vec],
        out_specs=(row(D), col, col, col, row(D), row(D), row(D), row(D), row(D), row(3 * D),
                   pl.BlockSpec((8, D), lambda i: (0, 0))),
        compiler_params=_params(("arbitrary",), 60),
    )(*_pin(x2, t2, c0, cz, o, e, w3, cn_g, cn_b))


def _do_prep(d_o, o, tm):
    rows = d_o.shape[0]

    ch = jnp.arange(D, dtype=jnp.int32)
    head, col2 = jnp.arange(128, dtype=jnp.int32), jnp.arange(2 * D, dtype=jnp.int32)
    ind = ((ch // HEAD_DIM)[:, None] == head[None, :]).astype(BF16)
    spread = (col2[None, :] == (128 * (ch // HEAD_DIM) + ch % HEAD_DIM)[:, None]).astype(BF16)
    place = jnp.concatenate([-(col2[None, :] == (128 * head + 64)[:, None]).astype(F32),
                             -(col2[None, :] == (128 * head + 65)[:, None]).astype(F32)], axis=0).astype(BF16)

    def body(do_ref, o_ref, ind_ref, spread_ref, place_ref, doe_o, dot_o):
        dov = do_ref[...]
        delta = _apply(dov * o_ref[...], ind_ref[...])
        d_hi = delta.astype(BF16)
        d_lo = (delta - d_hi.astype(F32)).astype(BF16)
        tails = _dot(jnp.concatenate([d_hi, d_lo], axis=1), place_ref[...])
        doe_o[...] = (_dot(dov.astype(BF16), spread_ref[...]) + tails).astype(BF16)
        dot_o[...] = dov.T.astype(BF16)

    row = pl.BlockSpec((tm, D), lambda i: (i, 0))
    const = lambda a: pl.BlockSpec(a.shape, lambda i: (0, 0))
    return pl.pallas_call(
        body, name="do_prep", grid=(rows // tm,),
        out_shape=(_sds((rows, 2 * D), BF16), _sds((D, rows), BF16)),
        in_specs=[row, row, const(ind), const(spread), const(place)],
        out_specs=(pl.BlockSpec((tm, 2 * D), lambda i: (i, 0)), pl.BlockSpec((D, tm), lambda i: (0, i))),
        compiler_params=_params(("parallel",), 40),
    )(*_pin(d_o, o, ind, spread, place))


def _q_post(dqr, pq, qg, cos, sin, nb, tm):
    rows = pq.shape[0]
    nt = rows // nb // tm

    def body(dq_ref, q_ref, g_ref, cos_ref, sin_ref, dq_o, dg_o):
        @pl.when((pl.program_id(0) == 0) & (pl.program_id(1) == 0))
        def _():
            dg_o[...] = jnp.zeros_like(dg_o)

        gv, cosv, sinv = g_ref[...], cos_ref[...], sin_ref[...]
        acc = jnp.zeros((1, GROUP_W), F32)
        mats = _qk_mats()
        for g in range(N_KV):
            gs = slice(GROUP_W * g, GROUP_W * (g + 1))
            qv = q_ref[:, gs]
            r = lax.rsqrt(_apply(qv * qv, mats[0]) + EPS)
            dq, dgr = _qk_bwd(dq_ref[:, gs], qv, r, gv, cosv, sinv, mats)
            dq_o[:, gs] = dq.astype(BF16)
            acc = acc + jnp.sum(dgr, axis=0, keepdims=True)
        dg_o[...] += acc

    row = pl.BlockSpec((tm, D), lambda b, i: (b * nt + i, 0))
    rope = pl.BlockSpec((tm, GROUP_W), lambda b, i: (i, 0))
    vec = pl.BlockSpec((1, GROUP_W), lambda b, i: (0, 0))
    return pl.pallas_call(
        body, name="q_post", grid=(nb, nt),
        out_shape=(_sds((rows, D), BF16), _sds((1, GROUP_W), F32)),
        in_specs=[row, row, vec, rope, rope], out_specs=(row, vec),
        compiler_params=_params(("arbitrary", "arbitrary"), 32),
    )(*_pin(dqr, pq, qg, cos, sin))


def _attn_bwd(q2, qst, kv4, doe, dot_, lse, nb, tq):
    rows = q2.shape[0]
    s_len = rows // nb
    nq = s_len // tq
    scale = 1.0 / math.sqrt(HEAD_DIM)

    def body(q_ref, qt_ref, k1_ref, v1_ref, k2_ref, v2_ref, doe_ref, dot_ref, lse_ref,
             dq_o, dkt_o, dvt_o, dkt2_o, dvt2_o):
        i = pl.program_id(2)
        lse = lse_ref[0, 0]
        k1, k2, v1, v2 = k1_ref[0, 0], k2_ref[0, 0], v1_ref[0, 0], v2_ref[0, 0]
        bias = _tail_bias()
        dkt1, dkt2 = jnp.zeros((HEAD_DIM, s_len), F32), jnp.zeros((HEAD_DIM, KEY_PAD), F32)
        dvt1, dvt2 = jnp.zeros((HEAD_DIM, s_len), F32), jnp.zeros((HEAD_DIM, KEY_PAD), F32)

        def products(h):
            qh = q_ref[:, HEAD_DIM * h:HEAD_DIM * (h + 1)]
            dh = doe_ref[:, 2 * HEAD_DIM * h:2 * HEAD_DIM * (h + 1)]
            return _dot_nt(qh, k1), _dot_nt(qh, k2) + bias, _dot_nt(dh, v1), _dot_nt(dh, v2)

        ahead = products(0)
        for h in range(GQA):
            hs = slice(HEAD_DIM * h, HEAD_DIM * (h + 1))
            s1, s2, dp1, dp2 = ahead
            if h + 1 < GQA:
                ahead = products(h + 1)
            lse_h = lse[:, h:h + 1]
            p1 = jnp.exp2(s1 - lse_h)
            p2 = jnp.exp2(s2 - lse_h)
            ds1 = (p1 * dp1).astype(BF16)
            ds2 = (p2 * dp2).astype(BF16)
            dq_o[:, hs] = (_dot(ds1, k1) + _dot(ds2, k2)) * scale
            dkt1 = dkt1 + _dot(qt_ref[hs, :], ds1)
            dkt2 = dkt2 + _dot(qt_ref[hs, :], ds2)
            dvt1 = dvt1 + _dot(dot_ref[hs, :], p1.astype(BF16))
            dvt2 = dvt2 + _dot(dot_ref[hs, :], p2.astype(BF16))

        @pl.when(i == 0)
        def _():
            dkt_o[0, 0], dkt2_o[0, 0], dvt_o[0, 0], dvt2_o[0, 0] = dkt1, dkt2, dvt1, dvt2

        @pl.when(i > 0)
        def _():
            dkt_o[0, 0] += dkt1
            dkt2_o[0, 0] += dkt2
            dvt_o[0, 0] += dvt1
            dvt2_o[0, 0] += dvt2

    qspec = pl.BlockSpec((tq, GROUP_W), lambda b, g, i: (b * nq + i, g))
    qtspec = pl.BlockSpec((GROUP_W, tq), lambda b, g, i: (g, b * nq + i))
    tspec = pl.BlockSpec((1, 1, HEAD_DIM, s_len), lambda b, g, i: (b, g, 0, 0))
    t2spec = pl.BlockSpec((1, 1, HEAD_DIM, KEY_PAD), lambda b, g, i: (b, g, 0, 0))
    tshape = _sds((nb, N_KV, HEAD_DIM, s_len), F32)
    t2shape = _sds((nb, N_KV, HEAD_DIM, KEY_PAD), F32)
    return pl.pallas_call(
        body, name="attn_bwd", grid=(nb, N_KV, nq),
        out_shape=(_sds((rows, D), F32), tshape, tshape, t2shape, t2shape),
        in_specs=[qspec, qtspec] + _kv_specs(s_len)
        + [pl.BlockSpec((tq, 2 * GROUP_W), lambda b, g, i: (b * nq + i, g)), qtspec,
           pl.BlockSpec((1, 1, tq, GQA), lambda b, g, i: (b, g, i, 0))],
        out_specs=(qspec, tspec, tspec, t2spec, t2spec),
        compiler_params=_params(("parallel", "parallel", "arbitrary"), 56),
    )(*_pin(q2, qst, *kv4, doe, dot_, lse))


def _kv_bwd(dkt, dvt, dkt2, dvt2, pkv, pm_kv, kg, cos, sin, nb):
    rows = pkv.shape[0]
    s_len = rows // nb
    tk = min(512, s_len)
    nt = s_len // tk

    def body(dk_ref, dv_ref, dk2_ref, dv2_ref, kv_ref, m_ref, g_ref, cos_ref, sin_ref, d_o, dm_o, dg_o):
        b, i = pl.program_id(0), pl.program_id(1)
        gv = g_ref[...]
        mats = _qk_mats()

        @pl.when((b == 0) & (i == 0))
        def _():
            dg_o[...] = jnp.zeros_like(dg_o)

        kx = kv_ref[:, :GROUP_W]
        r = lax.rsqrt(_apply(kx * kx, mats[0]) + EPS)
        dk, dgr = _qk_bwd(dk_ref[0].T, kx, r, gv, cos_ref[...], sin_ref[...], mats)
        d_o[:, :GROUP_W] = dk.astype(BF16)
        d_o[:, GROUP_W:] = dv_ref[0].T.astype(BF16)
        dg_o[...] += jnp.sum(dgr, axis=0, keepdims=True)

        @pl.when(i == 0)
        def _():
            kxm = m_ref[:, :GROUP_W]
            rm = lax.rsqrt(_apply(kxm * kxm, mats[0]) + EPS)
            dn = dk2_ref[0].T[0:N_META]
            dyg = dn * gv
            dm_o[0, :, :GROUP_W] = rm * dyg - kxm * (rm * rm * rm) * _apply(dyg * kxm, mats[0])
            dm_o[0, :, GROUP_W:] = dv2_ref[0].T[0:N_META]
            dg_o[...] += jnp.sum(dn * kxm * rm, axis=0, keepdims=True)

    tspec = pl.BlockSpec((1, GROUP_W, tk), lambda b, i: (b, 0, i))
    t2spec = pl.BlockSpec((1, GROUP_W, KEY_PAD), lambda b, i: (b, 0, 0))
    rope = pl.BlockSpec((tk, GROUP_W), lambda b, i: (i, 0))
    return pl.pallas_call(
        body, name="kv_bwd", grid=(nb, nt),
        out_shape=(_sds((rows, 512), BF16), _sds((nb, N_META, 512), F32),
                   _sds((1, GROUP_W), F32)),
        in_specs=[tspec, tspec, t2spec, t2spec, pl.BlockSpec((tk, 512), lambda b, i: (b * nt + i, 0)),
                  pl.BlockSpec((N_META, 512), lambda b, i: (0, 0)), pl.BlockSpec((1, GROUP_W), lambda b, i: (0, 0)),
                  rope, rope],
        out_specs=(pl.BlockSpec((tk, 512), lambda b, i: (b * nt + i, 0)),
                   pl.BlockSpec((1, N_META, 512), lambda b, i: (b, 0, 0)),
                   pl.BlockSpec((1, GROUP_W), lambda b, i: (0, 0))),
        compiler_params=_params(("arbitrary", "arbitrary"), 40),
    )(*_pin(dkt, dvt, dkt2, dvt2, pkv, pm_kv, kg, cos, sin))


def _conv_bwd(dc0, pconv, pm_conv, conv_w, nb, tm):
    rows = pconv.shape[0]
    nt = rows // nb // tm

    def body(dcur, dprev, dnxt, cur, meta, w_ref, da_o, dam_o, gw_o, dext, dsh, accw):
        b, i = pl.program_id(0), pl.program_id(1)
        dext[0:16] = jnp.zeros((16, D), F32)
        dext[16:32] = jnp.where(i == 0, 0.0, dprev[...])
        dext[32:32 + tm] = dcur[...]
        dext[32 + tm:48 + tm] = jnp.where(i == nt - 1, 0.0, dnxt[...])
        _shifted_copies(dsh, dext, tm + 40)
        accw[...] = jnp.zeros_like(accw)

        @pl.when((b == 0) & (i == 0))
        def _():
            gw_o[...] = jnp.zeros_like(gw_o)

        for c0 in range(0, D, 256):
            cs = slice(c0, c0 + 256)
            for r0 in range(0, tm, 32):
                cv = cur[r0:r0 + 32, c0:c0 + 256]
                sg = _sig(cur[r0:r0 + 32, D + c0:D + c0 + 256])
                u = cv * sg
                acc = jnp.zeros((32, 256), F32)
                for j in range(CONV_K):
                    d = _rows32(dsh, dext, r0 + 47 - j, cs)
                    acc = acc + d * w_ref[j:j + 1, cs]
                    p = d * u
                    accw[8 * j:8 * j + 8, cs] += (p[0:8] + p[8:16]) + (p[16:24] + p[24:32])
                da_o[r0:r0 + 32, cs] = (acc * sg).astype(BF16)
                da_o[r0:r0 + 32, D + c0:D + c0 + 256] = (acc * cv * sg * (1.0 - sg)).astype(BF16)
        for j in range(CONV_K):
            gw_o[j:j + 1, :] += jnp.sum(accw[8 * j:8 * j + 8, :], axis=0, keepdims=True)

        @pl.when(i == 0)
        def _():
            for c0 in range(0, D, 256):
                cs = slice(c0, c0 + 256)
                cv = meta[:, c0:c0 + 256]
                sg = _sig(meta[:, D + c0:D + c0 + 256])
                um = cv * sg
                acc = jnp.zeros((16, 256), F32)
                for j in range(CONV_K):
                    d = dext[31 - j:47 - j, cs]
                    acc = acc + d * w_ref[j:j + 1, cs]
                    gw_o[j:j + 1, cs] += jnp.sum(d * um, axis=0, keepdims=True)
                dam_o[0, :, cs] = acc * sg
                dam_o[0, :, D + c0:D + c0 + 256] = acc * cv * sg * (1.0 - sg)

    return pl.pallas_call(
        body, name="conv_bwd", grid=(nb, nt),
        out_shape=(_sds((rows, 2048), BF16), _sds((nb, N_META, 2048), F32),
                   _sds((32, D), F32)),
        in_specs=_halo_specs(D, tm, nt, rows)
        + [pl.BlockSpec((tm, 2048), lambda b, i: (b * nt + i, 0)),
           pl.BlockSpec((16, 2048), lambda b, i: (0, 0)), pl.BlockSpec((32, D), lambda b, i: (0, 0))],
        out_specs=(pl.BlockSpec((tm, 2048), lambda b, i: (b * nt + i, 0)),
                   pl.BlockSpec((1, N_META, 2048), lambda b, i: (b, 0, 0)),
                   pl.BlockSpec((32, D), lambda b, i: (0, 0))),
        scratch_shapes=[pltpu.VMEM((tm + 48, D), F32), pltpu.VMEM((8, tm + 40, D), F32),
                        pltpu.VMEM((8 * CONV_K, D), F32)],
        compiler_params=_params(("arbitrary", "arbitrary"), 48),
    )(*_pin(dc0, dc0, dc0, pconv, pm_conv, conv_w))


def _meta_bwd(dam, ddm, w_full, meta_full, norm_g):
    nb = dam.shape[0]

    def body(a_ref, d_ref, wc_ref, wkv_ref, m_ref, g_ref, gm_o, dg_o):
        a, d = a_ref[0], d_ref[0]
        for b in range(1, nb):
            a = a + a_ref[b]
            d = d + d_ref[b]
        dxn = _dot_nt(a.astype(BF16), wc_ref[...]) + _dot_nt(d.astype(BF16), wkv_ref[...])
        v = m_ref[...]
        r = lax.rsqrt(jnp.mean(v * v, axis=-1, keepdims=True) + EPS)
        gm_o[...] = _rms_bwd(dxn, v, r, g_ref[...])
        dg_o[...] = jnp.sum(dxn * v * r, axis=0, keepdims=True)

    return pl.pallas_call(
        body, name="meta_bwd", grid=(1,),
        out_shape=(_sds((N_META, D), F32), _sds((1, D), F32)),
        in_specs=[pl.BlockSpec((nb, N_META, 2048), lambda i: (0, 0, 0)), pl.BlockSpec((nb, N_META, 512), lambda i: (0, 0, 0)),
                  pl.BlockSpec((D, 2048), lambda i: (0, 0)), pl.BlockSpec((D, 512), lambda i: (0, G_KV[0] // 512)),
                  pl.BlockSpec((N_META, D), lambda i: (0, 0)), pl.BlockSpec((1, D), lambda i: (0, 0))],
        out_specs=(pl.BlockSpec((N_META, D), lambda i: (0, 0)), pl.BlockSpec((1, D), lambda i: (0, 0))),
        compiler_params=_params(("arbitrary",), 32),
    )(*_pin(dam, ddm, w_full, w_full, meta_full, norm_g))


def _dxn(d_groups, w_full, x2, dy, norm_g, dg_init, tm):
    rows = x2.shape[0]
    groups = (G_CONV, G_CZ, G_Q, G_KV, G_E)

    def body(da, db, dq, dd, de, w_hbm, x_ref, dy_ref, g_ref, gi_ref, gx_o, dg_o, w_vmem, sem):
        @pl.when(pl.program_id(0) == 0)
        def _():
            cp = pltpu.make_async_copy(w_hbm, w_vmem, sem)
            cp.start()
            cp.wait()
            dg_o[...] = gi_ref[...]

        dxn = jnp.zeros((tm, D), F32)
        for ref, (off, wd) in zip((da, db, dq, dd, de), groups):
            for c0 in range(0, wd, 512):
                dxn = dxn + _dot_nt(ref[:, c0:c0 + 512], w_vmem[:, off + c0:off + c0 + 512])
        v = x_ref[...]
        r = lax.rsqrt(jnp.mean(v * v, axis=-1, keepdims=True) + EPS)
        gx_o[...] = dy_ref[...] + _rms_bwd(dxn, v, r, g_ref[...])
        dg_o[...] += jnp.sum(dxn * v * r, axis=0, keepdims=True)

    row = lambda wd: pl.BlockSpec((tm, wd), lambda i: (i, 0))
    vec = pl.BlockSpec((1, D), lambda i: (0, 0))
    return pl.pallas_call(
        body, name="dxn", grid=(rows // tm,),
        out_shape=(_sds((rows, D), F32), _sds((1, D), F32)),
        in_specs=[row(wd) for _, wd in groups] + [pl.BlockSpec(memory_space=pl.ANY), row(D), row(D), vec, vec],
        out_specs=(row(D), vec),
        scratch_shapes=[pltpu.VMEM((D, IN_DIM), BF16), pltpu.SemaphoreType.DMA],
        compiler_params=_params(("arbitrary",), 56),
    )(*_pin(*d_groups, w_full, x2, dy, norm_g, dg_init))


def _wgrad(at, b, bufs, slot, col_off, name, meta=None):
    buf, bufb = bufs
    rows, n = b.shape
    tn = next(t for t in (1536, 1024, 512) if n % t == 0 and col_off % t == 0)
    tk = min(2048, rows)
    nk = rows // tk
    j0 = col_off // tn

    def body(*refs):
        if meta is None:
            at_ref, b_ref, _, _, o_ref, ob_ref = refs
        else:
            at_ref, b_ref, xm_ref, dm_ref, _, _, o_ref, ob_ref = refs
        k = pl.program_id(1)

        @pl.when(k == 0)
        def _():
            if meta is None:
                o_ref[0] = jnp.zeros((D, tn), F32)
            else:
                dm = dm_ref[0]
                for e in range(1, dm_ref.shape[0]):
                    dm = dm + dm_ref[e]
                dm = jnp.concatenate([dm, jnp.zeros((128 - N_META, tn), F32)], axis=0)
                o_ref[0] = _dot(xm_ref[...], dm.astype(BF16))

        o_ref[0] += _dot(at_ref[...], b_ref[...].astype(BF16))

        @pl.when(k == nk - 1)
        def _():
            ob_ref[0] = o_ref[0].astype(BF16)

    in_specs = [pl.BlockSpec((D, tk), lambda j, k: (0, k)), pl.BlockSpec((tk, tn), lambda j, k: (k, j))]
    args = [at, b]
    if meta is not None:
        xmt, dm = meta
        in_specs += [pl.BlockSpec((D, 128), lambda j, k: (0, 0)),
                     pl.BlockSpec((dm.shape[0], N_META, tn), lambda j, k: (0, 0, j))]
        args += [xmt, dm]
    in_specs += [pl.BlockSpec(memory_space=pl.ANY)] * 2
    args += [buf, bufb]
    blk = pl.BlockSpec((1, D, tn), lambda j, k: (slot, 0, j0 + j))
    return pl.pallas_call(
        body, name=name, grid=(n // tn, nk),
        out_shape=(_sds(buf.shape, F32), _sds(buf.shape, BF16)),
        in_specs=in_specs,
        out_specs=(blk, blk),
        input_output_aliases={len(args) - 2: 0, len(args) - 1: 1},
        compiler_params=_params(("parallel", "arbitrary"), 56),
    )(*_pin(*args))


def _rope_tables(s_len):
    pos = jnp.arange(s_len, dtype=jnp.int32)
    row_ids = (pos // GRID_W).astype(F32)
    col_ids = (pos % GRID_W).astype(F32)
    inv_freq = ROPE_THETA ** (-jnp.arange(ROPE_FREQS, dtype=F32) / ROPE_FREQS)
    a_row = row_ids[:, None] * inv_freq[None, :]
    a_col = col_ids[:, None] * inv_freq[None, :]
    ang = jnp.concatenate([a_row, a_row, a_col, a_col], axis=-1)
    return jnp.tile(jnp.cos(ang), (1, GQA)), jnp.tile(jnp.sin(ang), (1, GQA))


def _local_step(x, loss_target, norm_g, conv_b, cn_g, cn_b, q_g, k_g, w_full, w3_full, conv_w_full, meta_full,
                reduce_start=None):
    nb, s_len, _ = x.shape
    rows = nb * s_len
    x2 = x.reshape(rows, D)
    t2 = loss_target.reshape(rows, D)
    cos, sin = _rope_tables(s_len)
    qg = jnp.tile(q_g, (1, GQA))
    kg = jnp.tile(k_g, (1, N_KV))

    xnmt, pm_conv, pm_kv = _meta_fwd(meta_full, norm_g, w_full)
    pconv, pcz, pq, pkv, pe, xnt, q2, qst = _in_proj(x2, norm_g, w_full, qg, cos, sin, nb, 256)
    c0 = _conv_fwd(pconv, pm_conv, conv_w_full, conv_b, nb, 256)
    tq = min(512, s_len)
    kv4 = _kv_prep(pkv, pm_kv, kg, cos, sin, nb)
    o, lse = _attn_fwd(q2, kv4, nb, min(1024, s_len))
    if callable(w3_full):
        w3_full = w3_full(o)
    dy, mt, c3t, o2t, dyc, dya, d_o, dc0, dcz, de, sums = _mid(x2, t2, c0, pcz, o, pe, w3_full, cn_g, cn_b, 256)
    doe, dot_ = _do_prep(d_o, o, 256)
    dqr, dkt, dvt, dkt2, dvt2 = _attn_bwd(q2, qst, kv4, doe, dot_, lse, nb, min(1024, s_len))
    dq, dqg = _q_post(dqr, pq, qg, cos, sin, nb, 256)
    dd, ddm, dkg = _kv_bwd(dkt.reshape(nb, GROUP_W, s_len), dvt.reshape(nb, GROUP_W, s_len),
                           dkt2.reshape(nb, GROUP_W, KEY_PAD), dvt2.reshape(nb, GROUP_W, KEY_PAD),
                           pkv, pm_kv, kg, cos, sin, nb)
    da, dam, gcw = _conv_bwd(dc0, pconv, pm_conv, conv_w_full, nb, 256)
    gmeta, dng_m = _meta_bwd(dam, ddm, w_full, meta_full, norm_g)

    gw3 = (lax.empty((3, D, D), F32), lax.empty((3, D, D), BF16))
    gw3 = _wgrad(c3t, dyc, gw3, 0, 0, "wgrad_conv_out")
    gw3 = _wgrad(o2t, dya, gw3, 1, 0, "wgrad_attn_out")
    gw3 = _wgrad(mt, dy, gw3, 2, 0, "wgrad_out")
    gwin = (lax.empty((1, D, IN_DIM), F32), lax.empty((1, D, IN_DIM), BF16))
    gwin = _wgrad(xnt, da, gwin, 0, G_CONV[0], "wgrad_in_conv", meta=(xnmt, dam))
    gwin = _wgrad(xnt, dcz, gwin, 0, G_CZ[0], "wgrad_in_cz")
    gwin = _wgrad(xnt, dq, gwin, 0, G_Q[0], "wgrad_in_q")
    gwin = _wgrad(xnt, dd, gwin, 0, G_KV[0], "wgrad_in_kv", meta=(xnmt, ddm))
    gwin = _wgrad(xnt, de, gwin, 0, G_E[0], "wgrad_in_e")

    pending = None
    if reduce_start is not None:
        token, pending = reduce_start(gwin, gw3, gcw, gmeta)
        dng_m = dng_m + token[0:1, 0:1]
    gx, dng = _dxn((da, dcz, dq, dd, de), w_full, x2, dy, norm_g, dng_m, 512)

    zeros = jnp.zeros((1, D - 2 * GROUP_W), F32)
    smalls = jnp.concatenate([dng, sums[2:3], sums[0:1], sums[1:2], jnp.concatenate([dqg, dkg, zeros], axis=1),
                              sums[3:4], jnp.zeros((2, D), F32)], axis=0)
    return gx.reshape(nb, s_len, D), gwin, gw3, gcw, gmeta, smalls, pending


def _xyc():
    return lax.axis_index("x"), lax.axis_index("y"), lax.axis_index("c")


def _reduce_sibling(gwin, gw3v, gcm):
    def body(gwin_ref, gw3_ref, gcm_ref, r_win, r_w3, r_cm, send, recv):
        x, y, c = _xyc()
        o = 1 - c
        half = D // 2
        outs = ((gwin_ref.at[pl.ds(o * half, half), :], r_win), (gw3_ref.at[:, :, o], r_w3), (gcm_ref.at[o], r_cm))
        cps = []
        for a, (src, dst) in enumerate(outs):
            cp = pltpu.make_async_remote_copy(src_ref=src, dst_ref=dst, send_sem=send.at[a], recv_sem=recv.at[a],
                                              device_id=(x, y, o), device_id_type=MESH)
            cp.start()
            cps.append(cp)
        for cp in cps:
            cp.wait()

    any_spec = pl.BlockSpec(memory_space=pl.ANY)
    return pl.pallas_call(
        body, name="reduce_sibling",
        out_shape=(_sds((D // 2, IN_DIM), BF16), _sds((3, 4, 128, D), BF16),
                   _sds((24, D), F32)),
        in_specs=[any_spec] * 3, out_specs=(any_spec,) * 3,
        scratch_shapes=[pltpu.SemaphoreType.DMA((3,)), pltpu.SemaphoreType.DMA((3,))],
    )(*_pin(gwin, gw3v, gcm))


def _add_sibling(gwin, gw3v, gcm, r_win, r_w3, r_cm):
    c = lax.axis_index("c").astype(jnp.int32).reshape(1)
    half = D // 2
    tr = 64

    def body1(c_ref, a_ref, b_ref, o_ref):
        o_ref[...] = (a_ref[...] + b_ref[...].astype(F32)).astype(BF16)

    cs_win = pl.pallas_call(
        body1, name="add_sibling_w_in", out_shape=_sds((half, IN_DIM), BF16),
        grid_spec=pltpu.PrefetchScalarGridSpec(
            num_scalar_prefetch=1, grid=(half // tr,),
            in_specs=[pl.BlockSpec((tr, IN_DIM), lambda i, c_ref: (c_ref[0] * (half // tr) + i, 0)),
                      pl.BlockSpec((tr, IN_DIM), lambda i, c_ref: (i, 0))],
            out_specs=pl.BlockSpec((tr, IN_DIM), lambda i, c_ref: (i, 0))),
        compiler_params=_params(("parallel",), 32),
    )(c, *_pin(gwin, r_win))

    def body2(c_ref, a_ref, b_ref, o_ref):
        o_ref[0, 0] = (a_ref[0, 0, 0] + b_ref[0, 0].astype(F32)).astype(BF16)

    cs_w3 = pl.pallas_call(
        body2, name="add_sibling_w3", out_shape=_sds((3, 4, 128, D), BF16),
        grid_spec=pltpu.PrefetchScalarGridSpec(
            num_scalar_prefetch=1, grid=(3, 4),
            in_specs=[pl.BlockSpec((1, 1, 1, 128, D), lambda w, s, c_ref: (w, s, c_ref[0], 0, 0)),
                      pl.BlockSpec((1, 1, 128, D), lambda w, s, c_ref: (w, s, 0, 0))],
            out_specs=pl.BlockSpec((1, 1, 128, D), lambda w, s, c_ref: (w, s, 0, 0))),
        compiler_params=_params(("parallel", "parallel"), 32),
    )(c, *_pin(gw3v, r_w3))

    def body3(c_ref, a_ref, b_ref, o_ref):
        o_ref[...] = a_ref[0] + b_ref[...]

    cs_cm = pl.pallas_call(
        body3, name="add_sibling_cm", out_shape=_sds((24, D), F32),
        grid_spec=pltpu.PrefetchScalarGridSpec(
            num_scalar_prefetch=1, grid=(1,),
            in_specs=[pl.BlockSpec((1, 24, D), lambda i, c_ref: (c_ref[0], 0, 0)),
                      pl.BlockSpec((24, D), lambda i, c_ref: (0, 0))],
            out_specs=pl.BlockSpec((24, D), lambda i, c_ref: (0, 0))),
        compiler_params=_params(("arbitrary",), 32),
    )(c, *_pin(gcm, r_cm))
    return cs_win, cs_w3, cs_cm


def _reduce_chips_copies(srcs, lands, send, recv):
    win_ref, w3_ref, cm_ref = srcs
    r_win, r_w3, r_cm = lands
    x, y, c = _xyc()
    peers = ((1 - x, y), (x, 1 - y), (1 - x, 1 - y))
    cps = []
    for k, (px, py) in enumerate(peers):
        ps = 2 * px + py
        items = ((win_ref.at[:, pl.ds(ps * W_IN_SHARD, W_IN_SHARD)], r_win.at[k]),
                 (w3_ref.at[:, ps], r_w3.at[k]),
                 (cm_ref.at[:, pl.ds(ps * ROW_SHARD, ROW_SHARD)], r_cm.at[k]))
        for a, (src, dst) in enumerate(items):
            cps.append(pltpu.make_async_remote_copy(src_ref=src, dst_ref=dst, send_sem=send.at[3 * a + k],
                                                    recv_sem=recv.at[3 * a + k], device_id=(px, py, c),
                                                    device_id_type=MESH))
    return cps


_HBM = pl.BlockSpec(memory_space=pltpu.HBM)
_SEM = pl.BlockSpec(memory_space=pltpu.SEMAPHORE)
_EFFECT = pltpu.SideEffectType.DATAFLOW_SIDE_EFFECTING


def _reduce_chips_start(cs_win, cs_w3, cs_cm):
    srcs = (cs_win, cs_w3, cs_cm)
    lands = (lax.empty((3, D // 2, W_IN_SHARD), BF16), lax.empty((3, 3, 128, D), BF16),
             lax.empty((3, 24, ROW_SHARD), F32))

    def body(*refs):
        srcs_in, lands_in, send, recv, token = refs[0:3], refs[3:6], refs[6], refs[7], refs[14]
        for cp in _reduce_chips_copies(srcs_in, lands_in, send, recv):
            cp.start()
        token[...] = jnp.zeros_like(token)

    hbm = lambda a: pltpu.HBM(a.shape, a.dtype)
    outs = pl.pallas_call(
        body, name="reduce_chips_start",
        out_shape=(pltpu.SemaphoreType.DMA((9,)), pltpu.SemaphoreType.DMA((9,)),
                   *[hbm(a) for a in srcs], *[hbm(a) for a in lands], jax.ShapeDtypeStruct((8, 128), F32)),
        in_specs=[_HBM] * 6,
        out_specs=(_SEM, _SEM, *[_HBM] * 6, pl.BlockSpec(memory_space=pltpu.VMEM)),
        input_output_aliases={i: i + 2 for i in range(6)},
        compiler_params=pltpu.CompilerParams(has_side_effects=_EFFECT),
    )(*[pltpu.with_memory_space_constraint(a, pltpu.HBM) for a in srcs + lands])
    return outs[0], outs[1], outs[2:5], outs[5:8], outs[8]


def _reduce_chips_wait(send, recv, srcs, lands, after):
    def body(*refs):
        srcs_in, lands_in, send_ref, recv_ref = refs[0:3], refs[3:6], refs[6], refs[7]
        for cp in _reduce_chips_copies(srcs_in, lands_in, send_ref, recv_ref):
            cp.wait_send()
            cp.wait_recv()

    hbm = lambda a: pltpu.HBM(a.shape, a.dtype)
    outs = pl.pallas_call(
        body, name="reduce_chips_wait",
        out_shape=(*[hbm(a) for a in srcs], *[hbm(a) for a in lands]),
        in_specs=[_HBM] * 6 + [_SEM, _SEM, pl.BlockSpec(memory_space=pl.ANY)],
        out_specs=(_HBM,) * 6,
        input_output_aliases={i: i for i in range(6)},
        compiler_params=pltpu.CompilerParams(has_side_effects=_EFFECT),
    )(*srcs, *lands, send, recv, after)
    return outs[0:3], outs[3:6]


def _add_chips(cs_win, cs_w3, cs_cm, r_win, r_w3, r_cm):
    x, y, c = _xyc()
    idx = jnp.stack([2 * x + y, c]).astype(jnp.int32)
    half = D // 2
    tr = 128

    def body1(i_ref, a_ref, b_ref, o_ref):
        f = lambda v: v.astype(F32)
        o_ref[0] = (f(a_ref[...]) + f(b_ref[2])) + (f(b_ref[0]) + f(b_ref[1]))

    f_win = pl.pallas_call(
        body1, name="add_chips_w_in", out_shape=_sds((2, half, W_IN_SHARD), F32),
        grid_spec=pltpu.PrefetchScalarGridSpec(
            num_scalar_prefetch=1, grid=(half // tr,),
            in_specs=[pl.BlockSpec((tr, W_IN_SHARD), lambda i, r: (i, r[0])),
                      pl.BlockSpec((3, tr, W_IN_SHARD), lambda i, r: (0, i, 0))],
            out_specs=pl.BlockSpec((1, tr, W_IN_SHARD), lambda i, r: (r[1], i, 0))),
        compiler_params=_params(("parallel",), 32),
    )(idx, *_pin(cs_win, r_win))

    def body2(i_ref, a_ref, b_ref, o_ref):
        f = lambda v: v.astype(F32)
        o_ref[0, 0] = (f(a_ref[0, 0]) + f(b_ref[2, 0])) + (f(b_ref[0, 0]) + f(b_ref[1, 0]))

    f_w3 = pl.pallas_call(
        body2, name="add_chips_w3", out_shape=_sds((3, 2, 128, D), F32),
        grid_spec=pltpu.PrefetchScalarGridSpec(
            num_scalar_prefetch=1, grid=(3,),
            in_specs=[pl.BlockSpec((1, 1, 128, D), lambda w, r: (w, r[0], 0, 0)),
                      pl.BlockSpec((3, 1, 128, D), lambda w, r: (0, w, 0, 0))],
            out_specs=pl.BlockSpec((1, 1, 128, D), lambda w, r: (w, r[1], 0, 0))),
        compiler_params=_params(("parallel",), 32),
    )(idx, *_pin(cs_w3, r_w3))

    def body3(i_ref, a_ref, b_ref, o_ref):
        o_ref[0] = (a_ref[...] + b_ref[2]) + (b_ref[0] + b_ref[1])

    f_cm = pl.pallas_call(
        body3, name="add_chips_cm", out_shape=_sds((2, 24, ROW_SHARD), F32),
        grid_spec=pltpu.PrefetchScalarGridSpec(
            num_scalar_prefetch=1, grid=(1,),
            in_specs=[pl.BlockSpec((24, ROW_SHARD), lambda i, r: (0, r[0])),
                      pl.BlockSpec((3, 24, ROW_SHARD), lambda i, r: (0, 0, 0))],
            out_specs=pl.BlockSpec((1, 24, ROW_SHARD), lambda i, r: (r[1], 0, 0))),
        compiler_params=_params(("arbitrary",), 32),
    )(idx, *_pin(cs_cm, r_cm))
    return f_win, f_w3, f_cm


def _share_sibling(f_win, f_w3, f_cm, smalls):
    def body(win_in, w3_in, cm_in, sm_ref, win_ref, w3_ref, cm_ref, r_sm, send, recv, ssend, srecv, lsem):
        x, y, c = _xyc()
        o = 1 - c
        cps = []
        for a, (ref, sl) in enumerate(((win_ref, lambda h: win_ref.at[h]), (w3_ref, lambda h: w3_ref.at[:, h]),
                                       (cm_ref, lambda h: cm_ref.at[h]))):
            cp = pltpu.make_async_remote_copy(src_ref=sl(c), dst_ref=sl(c), send_sem=send.at[a], recv_sem=recv.at[a],
                                              device_id=(x, y, o), device_id_type=MESH)
            cp.start()
            cps.append((cp, sl))
        me = 4 * x + 2 * y + c
        loc = pltpu.make_async_copy(sm_ref, r_sm.at[me], lsem)
        loc.start()
        scps = []
        for d in range(1, 8):
            px, py, pc = (x + (d >> 2)) % 2, (y + ((d >> 1) & 1)) % 2, (c + (d & 1)) % 2
            cp = pltpu.make_async_remote_copy(src_ref=sm_ref, dst_ref=r_sm.at[me], send_sem=ssend.at[d - 1],
                                              recv_sem=srecv.at[d - 1], device_id=(px, py, pc), device_id_type=MESH)
            cp.start()
            scps.append((cp, 4 * px + 2 * py + pc))
        for a, (cp, sl) in enumerate(cps):
            pltpu.make_async_remote_copy(src_ref=sl(o), dst_ref=sl(o), send_sem=send.at[a], recv_sem=recv.at[a],
                                         device_id=(x, y, o), device_id_type=MESH).wait_recv()
            cp.wait_send()
        for d, (cp, pid) in enumerate(scps):
            pltpu.make_async_remote_copy(src_ref=sm_ref, dst_ref=r_sm.at[pid], send_sem=ssend.at[d],
                                         recv_sem=srecv.at[d], device_id=(x, y, c), device_id_type=MESH).wait_recv()
            cp.wait_send()
        loc.wait()

    any_spec = pl.BlockSpec(memory_space=pl.ANY)
    return pl.pallas_call(
        body, name="share_sibling",
        out_shape=(_sds(f_win.shape, F32), _sds(f_w3.shape, F32), _sds(f_cm.shape, F32), _sds((8, 8, D), F32)),
        in_specs=[any_spec] * 4, out_specs=(any_spec,) * 4,
        input_output_aliases={0: 0, 1: 1, 2: 2},
        scratch_shapes=[pltpu.SemaphoreType.DMA((3,)), pltpu.SemaphoreType.DMA((3,)),
                        pltpu.SemaphoreType.DMA((7,)), pltpu.SemaphoreType.DMA((7,)), pltpu.SemaphoreType.DMA],
    )(*_pin(f_win, f_w3, f_cm, smalls))


def _adamw_math(w, g, m, v):
    m = ADAM_B1 * m + (1.0 - ADAM_B1) * g
    v = ADAM_B2 * v + (1.0 - ADAM_B2) * (g * g)
    m_hat = m / (1.0 - ADAM_B1 ** ADAM_STEP)
    v_hat = v / (1.0 - ADAM_B2 ** ADAM_STEP)
    delta = -ADAM_LR * (m_hat / (jnp.sqrt(v_hat) + ADAM_EPS) + ADAM_WD * w)
    return delta, m, v


def _adamw(w, g, m, v, tr, name):
    rows, cols = w.shape

    def body(w_ref, g_ref, m_ref, v_ref, d_o, m_o, v_o):
        d_o[...], m_o[...], v_o[...] = _adamw_math(w_ref[...], g_ref[...], m_ref[...], v_ref[...])

    spec = pl.BlockSpec((tr, cols), lambda i: (i, 0))
    return pl.pallas_call(
        body, name=name, grid=(rows // tr,),
        out_shape=(_sds((rows, cols), F32),) * 3,
        in_specs=[spec] * 4, out_specs=(spec,) * 3,
        compiler_params=_params(("parallel",), 32),
    )(*_pin(w, g, m, v))


def _adamw3(g3, ws, ms, vs):
    def body(g_ref, *refs):
        w_refs, m_refs, v_refs, outs = refs[0:3], refs[3:6], refs[6:9], refs[9:]
        g_os, d_os, m_os, v_os = outs[0:3], outs[3:6], outs[6:9], outs[9:12]
        for i in range(3):
            g = g_ref[i]
            g_os[i][0] = g
            d_os[i][0], m_os[i][0], v_os[i][0] = _adamw_math(w_refs[i][0], g, m_refs[i][0], v_refs[i][0])

    return pl.pallas_call(
        body, name="adamw_w3", out_shape=(jax.ShapeDtypeStruct((1, ROW_SHARD, D), F32),) * 12,
        compiler_params=pltpu.CompilerParams(vmem_limit_bytes=48 << 20),
    )(g3, *ws, *ms, *vs)


def _adamw_cm(f_cm, ws, ms, vs):
    def body(f_ref, *refs):
        w_refs, m_refs, v_refs, outs = refs[0:2], refs[2:4], refs[4:6], refs[6:14]
        gcw, gmt = refs[14], refs[15]
        gcw[0:16] = f_ref[0, 0:16]
        gcw[16:32] = f_ref[1, 0:16]
        gmt[0:8] = f_ref[0, 16:24]
        gmt[8:16] = f_ref[1, 16:24]
        g_conv = gcw[0:CONV_K, :]
        g_meta = gmt[...]
        outs[0][0] = g_conv
        outs[1][...] = g_meta
        outs[2][0], outs[4][0], outs[6][0] = _adamw_math(w_refs[0][0], g_conv, m_refs[0][0], v_refs[0][0])
        outs[3][...], outs[5][...], outs[7][...] = _adamw_math(w_refs[1][...], g_meta, m_refs[1][...], v_refs[1][...])

    pair = (jax.ShapeDtypeStruct((1, CONV_K, ROW_SHARD), F32), jax.ShapeDtypeStruct((N_META, ROW_SHARD), F32))
    return pl.pallas_call(
        body, name="adamw_cm", out_shape=pair * 4,
        scratch_shapes=[pltpu.VMEM((32, ROW_SHARD), F32), pltpu.VMEM((N_META, ROW_SHARD), F32)],
    )(f_cm, *ws, *ms, *vs)


def _adamw_small(r_sm, ws, ms, vs):
    def body(s_ref, *refs):
        w_refs, m_refs, v_refs, outs = refs[0:6], refs[6:12], refs[12:18], refs[18:]
        loss_o, g_os, d_os, m_os, v_os = outs[0], outs[1:7], outs[7:13], outs[13:19], outs[19:25]
        g = s_ref[0]
        for dev in range(1, 8):
            g = g + s_ref[dev]
        qk = g[4:5, :]
        qg = qk[:, 0:HEAD_DIM]
        kg = qk[:, GROUP_W:GROUP_W + HEAD_DIM]
        for h in range(1, GQA):
            qg = qg + qk[:, HEAD_DIM * h:HEAD_DIM * (h + 1)]
            kg = kg + qk[:, GROUP_W + HEAD_DIM * h:GROUP_W + HEAD_DIM * (h + 1)]
        loss_o[...] = (0.5 / D) * jnp.sum(g[5:6, :], axis=-1, keepdims=True)
        for i, gi in enumerate((g[0:1], g[1:2], g[2:3], g[3:4], qg, kg)):
            g_os[i][...] = gi
            d_os[i][...], m_os[i][...], v_os[i][...] = _adamw_math(w_refs[i][...], gi, m_refs[i][...], v_refs[i][...])

    six = tuple(jax.ShapeDtypeStruct(w.shape, F32) for w in ws)
    return pl.pallas_call(
        body, name="adamw_small", out_shape=(jax.ShapeDtypeStruct((1, 1), F32),) + six * 4,
    )(r_sm, *ws, *ms, *vs)


def kernel(x, meta_tokens, norm_g, w_in, conv_w, conv_b, conv_norm_g, conv_norm_b, w_conv_out, q_norm_g, k_norm_g, w_attn_out, w_out, loss_target, m_meta_tokens, m_norm_g, m_w_in, m_conv_w, m_conv_b, m_conv_norm_g, m_conv_norm_b, m_w_conv_out, m_q_norm_g, m_k_norm_g, m_w_attn_out, m_w_out, v_meta_tokens, v_norm_g, v_w_in, v_conv_w, v_conv_b, v_conv_norm_g, v_conv_norm_b, v_w_conv_out, v_q_norm_g, v_k_norm_g, v_w_attn_out, v_w_out):
    pad_k = lambda a: jnp.pad(a[0], ((0, 32 - CONV_K), (0, 0)))
    w3_s = (w_conv_out, w_attn_out, w_out)
    w_full, conv_w_full, meta_full, w3b, w3_land = _gather_weights(w_in[0], w3_s, pad_k(conv_w), meta_tokens)
    w3_pending, token = _w3_start(w3b, w3_land)
    norm_g_fwd = norm_g + token[0:1, 0:1]

    def w3_full(after):
        return _w3_wait(*w3_pending, after)

    def reduce_start(gwin, gw3, gcw, gmeta):
        gwin2, gwin2b = (a.reshape(D, IN_DIM) for a in gwin)
        gw3v, gw3vb = (a.reshape(3, N_CHIPS, 2, 128, D) for a in gw3)
        gcm = jnp.concatenate([gcw.reshape(2, 16, D), gmeta.reshape(2, 8, D)], axis=1)
        r_win, r_w3, r_cm = _reduce_sibling(gwin2b, gw3vb, gcm)
        cs = _add_sibling(gwin2, gw3v, gcm, r_win, r_w3, r_cm)
        send, recv, srcs, lands, token = _reduce_chips_start(*cs)
        return token, (send, recv, srcs, lands)

    gx, _, _, _, _, smalls, pending = _local_step(
        x, loss_target, norm_g_fwd, conv_b, conv_norm_g, conv_norm_b, q_norm_g, k_norm_g,
        w_full, w3_full, conv_w_full, meta_full, reduce_start)
    (cs_win, cs_w3, cs_cm), (r2_win, r2_w3, r2_cm) = _reduce_chips_wait(*pending, gx)
    f_win, f_w3, f_cm = _add_chips(cs_win, cs_w3, cs_cm, r2_win, r2_w3, r2_cm)
    f_win, f_w3, f_cm, r_sm = _share_sibling(f_win, f_w3, f_cm, smalls)

    g_w_in = f_win.reshape(D, W_IN_SHARD)
    d_w_in, nm_w_in, nv_w_in = _adamw(w_in[0], g_w_in, m_w_in[0], v_w_in[0], 128, "adamw_w_in")
    w3 = _adamw3(f_w3.reshape(3, ROW_SHARD, D), w3_s, (m_w_conv_out, m_w_attn_out, m_w_out),
                 (v_w_conv_out, v_w_attn_out, v_w_out))
    cm = _adamw_cm(f_cm, (conv_w, meta_tokens), (m_conv_w, m_meta_tokens), (v_conv_w, v_meta_tokens))
    small = _adamw_small(
        r_sm, (norm_g, conv_b, conv_norm_g, conv_norm_b, q_norm_g, k_norm_g),
        (m_norm_g, m_conv_b, m_conv_norm_g, m_conv_norm_b, m_q_norm_g, m_k_norm_g),
        (v_norm_g, v_conv_b, v_conv_norm_g, v_conv_norm_b, v_q_norm_g, v_k_norm_g))

    def assemble(big_in, w3x, cmx, s6):
        ng, cb, cng, cnb, qg, kg = s6
        return (cmx[1], ng, big_in[None], cmx[0], cb, cng, cnb, w3x[0], qg, kg, w3x[1], w3x[2])

    loss = small[0].reshape(())
    grads = assemble(g_w_in, w3[0:3], cm[0:2], small[1:7])
    deltas = assemble(d_w_in, w3[3:6], cm[2:4], small[7:13])
    new_m = assemble(nm_w_in, w3[6:9], cm[4:6], small[13:19])
    new_v = assemble(nv_w_in, w3[9:12], cm[6:8], small[19:25])
    return (loss, gx, *grads, *deltas, *new_m, *new_v)
```

```python
import math

import jax
import jax.numpy as jnp
from jax import lax
from jax.experimental import pallas as pl
from jax.experimental.pallas import tpu as pltpu

F32, BF16 = jnp.float32, jnp.bfloat16
MESH = pl.DeviceIdType.MESH

D = 1024
N_META = 16
CONV_K = 31
N_KV = 4
GQA = 4
HEAD_DIM = 64
GROUP_W = GQA * HEAD_DIM
GRID_W = 64
ROPE_FREQS = 16
ROPE_THETA = 10000.0
EPS = 1e-6
IN_DIM = 7680
KEY_PAD = 128
G_CONV, G_CZ, G_Q, G_KV, G_E = (0, 2048), (2048, 1024), (3072, 1024), (4096, 512), (4608, 3072)
N_CHIPS = 4
W_IN_SHARD = IN_DIM // N_CHIPS
ROW_SHARD = D // N_CHIPS

ADAM_LR, ADAM_B1, ADAM_B2, ADAM_EPS, ADAM_WD, ADAM_STEP = 0.001, 0.9, 0.999, 1e-08, 0.01, 10

NT_DIMS = (((1,), (1,)), ((), ()))


def _params(sem=None, vmem_mb=48):
    return pltpu.CompilerParams(dimension_semantics=sem, vmem_limit_bytes=vmem_mb << 20)


def _sds(shape, dtype):
    return pltpu.HBM(tuple(shape), dtype)


def _pin(*arrays):
    return [pltpu.with_memory_space_constraint(a, pltpu.HBM) for a in arrays]


def _sig(v):
    return jax.nn.sigmoid(v)


def _dsilu(v, s):
    return s * (1.0 + v * (1.0 - s))


def _dot(a, b):
    return jnp.dot(a, b, preferred_element_type=F32)


def _dot_nt(a, b):
    return lax.dot_general(a, b, NT_DIMS, preferred_element_type=F32)


def _qk_mats():
    i = lax.broadcasted_iota(jnp.int32, (GROUP_W, GROUP_W), 0)
    j = lax.broadcasted_iota(jnp.int32, (GROUP_W, GROUP_W), 1)
    mean = jnp.where((i >> 6) == (j >> 6), 1.0 / HEAD_DIM, 0.0).astype(BF16)
    turn = jnp.where((i == j + 16) & ((j & 16) == 0), -1.0,
                     jnp.where((i == j - 16) & ((j & 16) != 0), 1.0, 0.0)).astype(BF16)
    return mean, turn


def _apply(v, mat):
    hi = v.astype(BF16)
    lo = (v - hi.astype(F32)).astype(BF16)
    return _dot(hi, mat) + _dot(lo, mat)


def _qk_fwd(v, g, cos, sin, mats):
    mean, turn = mats
    r = lax.rsqrt(_apply(v * v, mean) + EPS)
    n = v * r * g
    return n * cos + _apply(n, turn) * sin, r


def _qk_bwd(dy, v, r, g, cos, sin, mats):
    mean, turn = mats
    dn = dy * cos - _apply(dy, turn) * sin
    dyg = dn * g
    dv = r * dyg - v * (r * r * r) * _apply(dyg * v, mean)
    return dv, dn * v * r


def _rms_bwd(dxn, v, r, g):
    dxg = dxn * g
    return r * dxg - v * (r * r * r) * jnp.mean(dxg * v, axis=-1, keepdims=True)


def _glu(a):
    return a[:, :D] * _sig(a[:, D:])


def _gather_weights(w_in_s, w3_s, conv_w_s, meta_s):
    def body(win_ref, wa_ref, wb_ref, wc_ref, cw_ref, mt_ref, win_o, cw_o, mt_o, w3b_o, w3_o, win_b, w3_b,
             send, recv, fsend, frecv, lsem, csem):
        x, y, c = _xyc()
        o = 1 - c
        me = 2 * x + y
        win_b[...] = win_ref[...].astype(BF16)
        for i, ref in enumerate((wa_ref, wb_ref, wc_ref)):
            w3_b[i] = ref[0].astype(BF16)
        cast = pltpu.make_async_copy(w3_b, w3b_o, csem.at[0])
        cast.start()
        own = pltpu.make_async_copy(w3_b, _w3_place(w3_o, me), csem.at[1])
        own.start()
        items = (
            (lambda h: win_b.at[pl.ds(h * 512, 512), :],
             lambda p, h: win_o.at[pl.ds(h * 512, 512), pl.ds(p * W_IN_SHARD, W_IN_SHARD)]),
            (lambda h: cw_ref.at[pl.ds(h * 16, 16), :],
             lambda p, h: cw_o.at[pl.ds(h * 16, 16), pl.ds(p * ROW_SHARD, ROW_SHARD)]),
            (lambda h: mt_ref.at[pl.ds(h * 8, 8), :],
             lambda p, h: mt_o.at[pl.ds(h * 8, 8), pl.ds(p * ROW_SHARD, ROW_SHARD)]),
        )
        peers = ((1 - x, y), (x, 1 - y), (1 - x, 1 - y))

        def remote(src, dst, s_sem, r_sem, to):
            return pltpu.make_async_remote_copy(src_ref=src, dst_ref=dst, send_sem=s_sem, recv_sem=r_sem,
                                                device_id=to, device_id_type=MESH)

        started = []
        for a, (half, place) in enumerate(items):
            for h in range(2):
                loc = pltpu.make_async_copy(half(h), place(me, h), lsem.at[a, h])
                loc.start()
                started.append(loc.wait)
            for k, (px, py) in enumerate(peers):
                cp = remote(half(c), place(me, c), send.at[a, k], recv.at[a, k], (px, py, c))
                cp.start()
                started.append(cp.wait_send)
        for k, (px, py) in enumerate(peers):
            for a, (half, place) in enumerate(items):
                got = place(2 * px + py, c)
                remote(got, got, send.at[a, k], recv.at[a, k], (px, py, c)).wait_recv()
                fw = remote(got, got, fsend.at[a, k], frecv.at[a, k], (x, y, o))
                fw.start()
                started.append(fw.wait_send)
        for k, (px, py) in enumerate(peers):
            for a, (half, place) in enumerate(items):
                theirs = place(2 * px + py, o)
                remote(theirs, theirs, fsend.at[a, k], frecv.at[a, k], (x, y, o)).wait_recv()
        for wait in started:
            wait()
        cast.wait()
        own.wait()

    any_spec = pl.BlockSpec(memory_space=pl.ANY)
    vmem = pl.BlockSpec(memory_space=pltpu.VMEM)
    return pl.pallas_call(
        body, name="gather_weights",
        out_shape=(_sds((D, IN_DIM), BF16), _sds((32, D), F32), _sds((N_META, D), F32),
                   _sds((3, ROW_SHARD, D), BF16), _sds((3, D, D), BF16)),
        in_specs=[vmem] * 6,
        out_specs=(any_spec,) * 5,
        scratch_shapes=[pltpu.VMEM((D, W_IN_SHARD), BF16), pltpu.VMEM((3, ROW_SHARD, D), BF16),
                        pltpu.SemaphoreType.DMA((3, 3)), pltpu.SemaphoreType.DMA((3, 3)),
                        pltpu.SemaphoreType.DMA((3, 3)), pltpu.SemaphoreType.DMA((3, 3)),
                        pltpu.SemaphoreType.DMA((3, 2)), pltpu.SemaphoreType.DMA((2,))],
        compiler_params=pltpu.CompilerParams(vmem_limit_bytes=40 << 20),
    )(w_in_s, *w3_s, conv_w_s, meta_s)


def _w3_place(ref, p):
    return ref.at[:, pl.ds(p * ROW_SHARD, ROW_SHARD), :]


def _w3_copies(w3b_ref, land_ref, send, recv):
    x, y, c = _xyc()
    me = 2 * x + y
    peers = ((1 - x, y), (x, 1 - y), (1 - x, 1 - y))
    return [pltpu.make_async_remote_copy(src_ref=w3b_ref, dst_ref=_w3_place(land_ref, me),
                                         send_sem=send.at[k], recv_sem=recv.at[k], device_id=(px, py, c),
                                         device_id_type=MESH)
            for k, (px, py) in enumerate(peers)]


def _w3_start(w3b, land):
    def body(w3b_ref, land_ref, send, recv, w3b_thru, land_thru, token):
        for cp in _w3_copies(w3b_ref, land_ref, send, recv):
            cp.start()
        token[...] = jnp.zeros_like(token)

    outs = pl.pallas_call(
        body, name="w3_start",
        out_shape=(pltpu.SemaphoreType.DMA((3,)), pltpu.SemaphoreType.DMA((3,)),
                   pltpu.HBM(w3b.shape, BF16), pltpu.HBM(land.shape, BF16), jax.ShapeDtypeStruct((8, 128), F32)),
        in_specs=[_HBM, _HBM],
        out_specs=(_SEM, _SEM, _HBM, _HBM, pl.BlockSpec(memory_space=pltpu.VMEM)),
        input_output_aliases={0: 2, 1: 3},
        compiler_params=pltpu.CompilerParams(has_side_effects=_EFFECT),
    )(*_pin(w3b, land))
    return outs[0:4], outs[4]


def _w3_wait(send, recv, w3b, land, after):
    def body(w3b_ref, land_ref, send_ref, recv_ref, after_ref, w3b_out, land_out):
        x, y, c = _xyc()
        peers = ((1 - x, y), (x, 1 - y), (1 - x, 1 - y))
        for k, (cp, (px, py)) in enumerate(zip(_w3_copies(w3b_ref, land_ref, send_ref, recv_ref), peers)):
            cp.wait_send()
            got = _w3_place(land_ref, 2 * px + py)
            pltpu.make_async_remote_copy(src_ref=got, dst_ref=got, send_sem=send_ref.at[k], recv_sem=recv_ref.at[k],
                                         device_id=(px, py, c), device_id_type=MESH).wait_recv()

    outs = pl.pallas_call(
        body, name="w3_wait",
        out_shape=(pltpu.HBM(w3b.shape, BF16), pltpu.HBM(land.shape, BF16)),
        in_specs=[_HBM, _HBM, _SEM, _SEM, pl.BlockSpec(memory_space=pl.ANY)],
        out_specs=(_HBM, _HBM),
        input_output_aliases={0: 0, 1: 1},
        compiler_params=pltpu.CompilerParams(has_side_effects=_EFFECT),
    )(w3b, land, send, recv, after)
    return outs[1]


def _meta_fwd(meta_full, norm_g, w_full):
    def body(m_ref, g_ref, wc_ref, wkv_ref, xnt_ref, pc_ref, pkv_ref):
        v = m_ref[...]
        r = lax.rsqrt(jnp.mean(v * v, axis=-1, keepdims=True) + EPS)
        xn = v * r * g_ref[...]
        xnb = xn.astype(BF16)
        pad = jnp.concatenate([xn, jnp.zeros((128 - N_META, D), F32)], axis=0)
        xnt_ref[...] = pad.T.astype(BF16)
        pc_ref[...] = _dot(xnb, wc_ref[...])
        pkv_ref[...] = _dot(xnb, wkv_ref[...])

    return pl.pallas_call(
        body, name="meta_fwd", grid=(1,),
        out_shape=(_sds((D, 128), BF16), _sds((N_META, 2048), F32),
                   _sds((N_META, 512), F32)),
        in_specs=[pl.BlockSpec((N_META, D), lambda i: (0, 0)), pl.BlockSpec((1, D), lambda i: (0, 0)),
                  pl.BlockSpec((D, 2048), lambda i: (0, 0)), pl.BlockSpec((D, 512), lambda i: (0, G_KV[0] // 512))],
        out_specs=(pl.BlockSpec((D, 128), lambda i: (0, 0)), pl.BlockSpec((N_META, 2048), lambda i: (0, 0)),
                   pl.BlockSpec((N_META, 512), lambda i: (0, 0))),
        compiler_params=_params(("arbitrary",), 32),
    )(*_pin(meta_full, norm_g, w_full, w_full))


def _in_proj(x2, norm_g, w_full, qg, cos, sin, nb, tm):
    rows = x2.shape[0]
    nt = rows // nb // tm
    groups = (G_CONV, G_CZ, G_Q, G_KV, G_E)
    scale = 1.0 / math.sqrt(HEAD_DIM)

    def body(x_ref, g_ref, w_hbm, qg_ref, cos_ref, sin_ref, *rest):
        outs, xnt_ref, q2_o, qt_o, w_vmem, sem = rest[:5], rest[5], rest[6], rest[7], rest[8], rest[9]

        @pl.when(pl.program_id(0) == 0)
        def _():
            cp = pltpu.make_async_copy(w_hbm, w_vmem, sem)
            cp.start()
            cp.wait()

        v = x_ref[...]
        r = lax.rsqrt(jnp.mean(v * v, axis=-1, keepdims=True) + EPS)
        xn = v * r * g_ref[...]
        xnb = xn.astype(BF16)
        xnt_ref[...] = xn.T.astype(BF16)
        for ref, (off, wd) in zip(outs, groups):
            for c0 in range(0, wd, 512):
                ref[:, c0:c0 + 512] = _dot(xnb, w_vmem[:, off + c0:off + c0 + 512])
        gv, cosv, sinv = qg_ref[...], cos_ref[...], sin_ref[...]
        mats = _qk_mats()
        for g in range(N_KV):
            gs = slice(GROUP_W * g, GROUP_W * (g + 1))
            qr, _ = _qk_fwd(outs[2][:, gs], gv, cosv, sinv, mats)
            q2_o[:, gs] = (qr * (scale * LOG2E)).astype(BF16)
            qt_o[gs, :] = (qr * scale).T.astype(BF16)

    rope = pl.BlockSpec((tm, GROUP_W), lambda i: (lax.rem(i, nt), 0))
    return pl.pallas_call(
        body, name="in_proj", grid=(rows // tm,),
        out_shape=tuple(_sds((rows, wd), F32) for _, wd in groups)
        + (_sds((D, rows), BF16), _sds((rows, D), BF16), _sds((D, rows), BF16)),
        in_specs=[pl.BlockSpec((tm, D), lambda i: (i, 0)), pl.BlockSpec((1, D), lambda i: (0, 0)),
                  pl.BlockSpec(memory_space=pl.ANY), pl.BlockSpec((1, GROUP_W), lambda i: (0, 0)), rope, rope],
        out_specs=tuple(pl.BlockSpec((tm, wd), lambda i: (i, 0)) for _, wd in groups)
        + (pl.BlockSpec((D, tm), lambda i: (0, i)), pl.BlockSpec((tm, D), lambda i: (i, 0)),
           pl.BlockSpec((D, tm), lambda i: (0, i))),
        scratch_shapes=[pltpu.VMEM((D, IN_DIM), BF16), pltpu.SemaphoreType.DMA],
        compiler_params=_params(("arbitrary",), 58),
    )(*_pin(x2, norm_g, w_full, qg, cos, sin))


def _halo_specs(width, tm, nt, rows):
    h16 = tm // 16
    return [pl.BlockSpec((tm, width), lambda b, i: (b * nt + i, 0)),
            pl.BlockSpec((16, width), lambda b, i: (jnp.maximum((b * nt + i) * h16 - 1, 0), 0)),
            pl.BlockSpec((16, width), lambda b, i: (jnp.minimum((b * nt + i + 1) * h16, rows // 16 - 1), 0))]


def _fill_uext(uext, cur, prev, nxt, meta, i, nt, tm):
    uext[0:16] = jnp.where(i == 0, _glu(meta[...]), _glu(prev[...]))
    uext[16:16 + tm] = _glu(cur[...])
    uext[16 + tm:32 + tm] = jnp.where(i == nt - 1, 0.0, _glu(nxt[...]))


def _shifted_copies(dst, src, n):
    for r in range(1, 8):
        dst[r, 0:n] = src[r:r + n]


def _rows32(shifted, src, start, cols):
    q8, r = divmod(start, 8)
    if r == 0:
        return src[start:start + 32, cols]
    return shifted[r, 8 * q8:8 * q8 + 32, cols]


def _conv_fwd(pconv, pm_conv, conv_w, conv_b, nb, tm):
    rows = pconv.shape[0]
    nt = rows // nb // tm

    def body(cur, prev, nxt, meta, w_ref, b_ref, o_ref, uext, ush):
        i = pl.program_id(1)
        _fill_uext(uext, cur, prev, nxt, meta, i, nt, tm)
        _shifted_copies(ush, uext, tm + 24)
        for r0 in range(0, tm, 32):
            for c0 in range(0, D, 256):
                acc = jnp.zeros((32, 256), F32) + b_ref[:, c0:c0 + 256]
                for j in range(CONV_K):
                    acc = acc + _rows32(ush, uext, r0 + j + 1, slice(c0, c0 + 256)) * w_ref[j:j + 1, c0:c0 + 256]
                o_ref[r0:r0 + 32, c0:c0 + 256] = acc

    return pl.pallas_call(
        body, name="conv_fwd", grid=(nb, nt),
        out_shape=_sds((rows, D), F32),
        in_specs=_halo_specs(2048, tm, nt, rows)
        + [pl.BlockSpec((16, 2048), lambda b, i: (0, 0)), pl.BlockSpec((32, D), lambda b, i: (0, 0)),
           pl.BlockSpec((1, D), lambda b, i: (0, 0))],
        out_specs=pl.BlockSpec((tm, D), lambda b, i: (b * nt + i, 0)),
        scratch_shapes=[pltpu.VMEM((tm + 32, D), F32), pltpu.VMEM((8, tm + 24, D), F32)],
        compiler_params=_params(("parallel", "parallel"), 40),
    )(*_pin(pconv, pconv, pconv, pm_conv, conv_w, conv_b))


def _kv_prep(pkv, pm_kv, kg, cos, sin, nb):
    rows = pkv.shape[0]
    s_len = rows // nb
    tk = min(512, s_len)
    nt = s_len // tk

    def body(kv_ref, m_ref, g_ref, cos_ref, sin_ref, k_o, v_o, k2_o, v2_o):
        i = pl.program_id(1)
        mats = _qk_mats()
        kv = kv_ref[...]
        kr, _ = _qk_fwd(kv[:, :GROUP_W], g_ref[...], cos_ref[...], sin_ref[...], mats)
        ones = _ones_cols(tk, tk)
        for h in range(N_KV):
            k_o[0, h] = kr[:, HEAD_DIM * h:HEAD_DIM * (h + 1)].astype(BF16)
            vh = kv[:, GROUP_W + HEAD_DIM * h:GROUP_W + HEAD_DIM * (h + 1)]
            v_o[0, h] = jnp.concatenate([vh, ones], axis=1).astype(BF16)

        @pl.when(i == 0)
        def _():
            kvm = m_ref[...]
            km = kvm[:, :GROUP_W]
            kn = km * lax.rsqrt(_apply(km * km, mats[0]) + EPS) * g_ref[...]
            zeros = jnp.zeros((KEY_PAD - N_META, GROUP_W), F32)
            kfull = jnp.concatenate([kn, zeros], axis=0)
            vfull = jnp.concatenate([kvm[:, GROUP_W:], zeros], axis=0)
            ones_m = _ones_cols(KEY_PAD, N_META)
            for h in range(N_KV):
                k2_o[0, h] = kfull[:, HEAD_DIM * h:HEAD_DIM * (h + 1)].astype(BF16)
                v2_o[0, h] = jnp.concatenate([vfull[:, HEAD_DIM * h:HEAD_DIM * (h + 1)], ones_m], axis=1).astype(BF16)

    return pl.pallas_call(
        body, name="kv_prep", grid=(nb, nt),
        out_shape=(_sds((nb, N_KV, s_len, HEAD_DIM), BF16), _sds((nb, N_KV, s_len, 2 * HEAD_DIM), BF16),
                   _sds((nb, N_KV, KEY_PAD, HEAD_DIM), BF16), _sds((nb, N_KV, KEY_PAD, 2 * HEAD_DIM), BF16)),
        in_specs=[pl.BlockSpec((tk, 512), lambda b, i: (b * nt + i, 0)),
                  pl.BlockSpec((N_META, 512), lambda b, i: (0, 0)), pl.BlockSpec((1, GROUP_W), lambda b, i: (0, 0)),
                  pl.BlockSpec((tk, GROUP_W), lambda b, i: (i, 0)),
                  pl.BlockSpec((tk, GROUP_W), lambda b, i: (i, 0))],
        out_specs=(pl.BlockSpec((1, N_KV, tk, HEAD_DIM), lambda b, i: (b, 0, i, 0)),
                   pl.BlockSpec((1, N_KV, tk, 2 * HEAD_DIM), lambda b, i: (b, 0, i, 0)),
                   pl.BlockSpec((1, N_KV, KEY_PAD, HEAD_DIM), lambda b, i: (b, 0, 0, 0)),
                   pl.BlockSpec((1, N_KV, KEY_PAD, 2 * HEAD_DIM), lambda b, i: (b, 0, 0, 0))),
        compiler_params=_params(("parallel", "arbitrary"), 40),
    )(*_pin(pkv, pm_kv, kg, cos, sin))


def _ones_cols(rows, valid):
    r = lax.broadcasted_iota(jnp.int32, (rows, HEAD_DIM), 0)
    col = lax.broadcasted_iota(jnp.int32, (rows, HEAD_DIM), 1)
    return jnp.where((col < 2) & (r < valid), 1.0, 0.0).astype(F32)


def _tail_bias():
    col = lax.broadcasted_iota(jnp.int32, (1, KEY_PAD), 1)
    return jnp.where(col < N_META, 0.0, -1e30).astype(F32)


LOG2E = 1.4426950408889634


def _kv_specs(s_len):
    return [pl.BlockSpec((1, 1, s_len, HEAD_DIM), lambda b, g, i: (b, g, 0, 0)),
            pl.BlockSpec((1, 1, s_len, 2 * HEAD_DIM), lambda b, g, i: (b, g, 0, 0)),
            pl.BlockSpec((1, 1, KEY_PAD, HEAD_DIM), lambda b, g, i: (b, g, 0, 0)),
            pl.BlockSpec((1, 1, KEY_PAD, 2 * HEAD_DIM), lambda b, g, i: (b, g, 0, 0))]


def _attn_fwd(q2, kv4, nb, tq):
    rows = q2.shape[0]
    s_len = rows // nb
    nq = s_len // tq

    def body(q_ref, k1_ref, v1_ref, k2_ref, v2_ref, o_ref, lse_ref):
        qs = q_ref[...]
        k1, k2, v1, v2 = k1_ref[0, 0], k2_ref[0, 0], v1_ref[0, 0], v2_ref[0, 0]
        bias = _tail_bias()
        outs, lses = [], []

        def scores(h):
            qh = qs[:, HEAD_DIM * h:HEAD_DIM * (h + 1)]
            return _dot_nt(qh, k1), _dot_nt(qh, k2) + bias

        ahead = scores(0)
        for h in range(GQA):
            s1, s2 = ahead
            if h + 1 < GQA:
                ahead = scores(h + 1)
            m = jnp.maximum(jnp.max(s1, axis=-1, keepdims=True), jnp.max(s2, axis=-1, keepdims=True))
            oe = _dot(jnp.exp2(s1 - m).astype(BF16), v1) + _dot(jnp.exp2(s2 - m).astype(BF16), v2)
            l = oe[:, HEAD_DIM:HEAD_DIM + 1]
            outs.append(oe[:, :HEAD_DIM] / l)
            lses.append(m + jnp.log2(l))
        o_ref[...] = jnp.concatenate(outs, axis=1)
        lse_ref[0, 0] = jnp.concatenate(lses, axis=1)

    return pl.pallas_call(
        body, name="attn_fwd", grid=(nb, N_KV, nq),
        out_shape=(_sds((rows, D), F32), _sds((nb, N_KV, s_len, GQA), F32)),
        in_specs=[pl.BlockSpec((tq, GROUP_W), lambda b, g, i: (b * nq + i, g))] + _kv_specs(s_len),
        out_specs=(pl.BlockSpec((tq, GROUP_W), lambda b, g, i: (b * nq + i, g)),
                   pl.BlockSpec((1, 1, tq, GQA), lambda b, g, i: (b, g, i, 0))),
        compiler_params=_params(("parallel", "parallel", "parallel"), 48),
    )(*_pin(q2, *kv4))


def _mid(x2, t2, c0, cz, o, e, w3, cn_g, cn_b, tm):
    rows = x2.shape[0]

    def body(x_ref, t_ref, c0_ref, cz_ref, o_ref, e_ref, w_ref, g_ref, b_ref,
             dy_o, mt_o, c3t_o, o2t_o, dyc_o, dya_o, do_o, dc0_o, dcz_o, de_o, sums_o):
        wco, wao, wo = w_ref[0], w_ref[1], w_ref[2]
        cn_g_v = g_ref[...]
        c0v = c0_ref[...]
        xc = c0v - jnp.mean(c0v, axis=-1, keepdims=True)
        rstd = lax.rsqrt(jnp.mean(xc * xc, axis=-1, keepdims=True) + EPS)
        n = xc * rstd
        c1 = n * cn_g_v + b_ref[...]
        s1 = _sig(c1)
        c2 = c1 * s1
        czv = cz_ref[...]
        sz = _sig(czv)
        gz = czv * sz
        c3 = c2 * gz
        yc = _dot(c3.astype(BF16), wco)
        az, gc, ga = e_ref[:, :D], e_ref[:, D:2 * D], e_ref[:, 2 * D:]
        saz = _sig(az)
        gaz = az * saz
        ov = o_ref[...]
        o2 = ov * gaz
        ya = _dot(o2.astype(BF16), wao)
        sc, sa = _sig(gc), _sig(ga)
        merged = sc * yc + sa * ya
        out = _dot(merged.astype(BF16), wo)
        err = x_ref[...] + out - t_ref[...]
        dy = err * (1.0 / D)
        dy_o[...] = dy
        dm = _dot_nt(dy.astype(BF16), wo)
        dyc = dm * sc
        dya = dm * sa
        dycb, dyab = dyc.astype(BF16), dya.astype(BF16)
        dyc_o[...] = dycb
        dya_o[...] = dyab
        de_o[:, D:2 * D] = (dyc * yc * (1.0 - sc)).astype(BF16)
        de_o[:, 2 * D:] = (dya * ya * (1.0 - sa)).astype(BF16)
        dc3 = _dot_nt(dycb, wco)
        do2 = _dot_nt(dyab, wao)
        do_o[...] = do2 * gaz
        de_o[:, :D] = (do2 * ov * _dsilu(az, saz)).astype(BF16)
        dcz_o[...] = (dc3 * c2 * _dsilu(czv, sz)).astype(BF16)
        dc1 = dc3 * gz * _dsilu(c1, s1)
        dn = dc1 * cn_g_v
        dc0 = rstd * (dn - jnp.mean(dn, axis=-1, keepdims=True) - n * jnp.mean(dn * n, axis=-1, keepdims=True))
        dc0_o[...] = dc0
        mt_o[...] = merged.T.astype(BF16)
        c3t_o[...] = c3.T.astype(BF16)
        o2t_o[...] = o2.T.astype(BF16)

        @pl.when(pl.program_id(0) == 0)
        def _():
            sums_o[...] = jnp.zeros_like(sums_o)

        sums_o[0:1, :] += jnp.sum(dc1 * n, axis=0, keepdims=True)
        sums_o[1:2, :] += jnp.sum(dc1, axis=0, keepdims=True)
        sums_o[2:3, :] += jnp.sum(dc0, axis=0, keepdims=True)
        sums_o[3:4, :] += jnp.sum(err * err, axis=0, keepdims=True)

    row = lambda wd: pl.BlockSpec((tm, wd), lambda i: (i, 0))
    col = pl.BlockSpec((D, tm), lambda i: (0, i))
    vec = pl.BlockSpec((1, D), lambda i: (0, 0))
    f32o = lambda wd: _sds((rows, wd), F32)
    b16o = lambda wd: _sds((rows, wd), BF16)
    tpo = _sds((D, rows), BF16)
    return pl.pallas_call(
        body, name="mid", grid=(rows // tm,),
        out_shape=(f32o(D), tpo, tpo, tpo, b16o(D), b16o(D), f32o(D), f32o(D), b16o(D), b16o(3 * D),
                   _sds((8, D), F32)),
        in_specs=[row(D), row(D), row(D), row(D), row(D), row(3 * D),
                  pl.BlockSpec((3, D, D), lambda i: (0, 0, 0)), vec, vec],
        out_specs=(row(D), col, col, col, row(D), row(D), row(D), row(D), row(D), row(3 * D),
                   pl.BlockSpec((8, D), lambda i: (0, 0))),
        compiler_params=_params(("arbitrary",), 60),
    )(*_pin(x2, t2, c0, cz, o, e, w3, cn_g, cn_b))


def _do_prep(d_o, o, tm):
    rows = d_o.shape[0]

    def body(do_ref, o_ref, doe_o, dot_o, ind_ref, spread_ref, place_ref):
        @pl.when(pl.program_id(0) == 0)
        def _():
            def grid(shape):
                return (lax.broadcasted_iota(jnp.int32, shape, 0), lax.broadcasted_iota(jnp.int32, shape, 1))

            r, c = grid((D, 128))
            ind_ref[...] = jnp.where((r >> 6) == c, 1.0, 0.0).astype(BF16)
            r, c = grid((D, 2 * D))
            spread_ref[...] = jnp.where(c == 128 * (r >> 6) + (r & 63), 1.0, 0.0).astype(BF16)
            r, c = grid((2 * 128, 2 * D))
            place_ref[...] = jnp.where(c == 128 * (r & 127) + 64 + (r >> 7), -1.0, 0.0).astype(BF16)

        dov = do_ref[...]
        delta = _apply(dov * o_ref[...], ind_ref[...])
        d_hi = delta.astype(BF16)
        d_lo = (delta - d_hi.astype(F32)).astype(BF16)
        tails = _dot(jnp.concatenate([d_hi, d_lo], axis=1), place_ref[...])
        doe_o[...] = (_dot(dov.astype(BF16), spread_ref[...]) + tails).astype(BF16)
        dot_o[...] = dov.T.astype(BF16)

    row = pl.BlockSpec((tm, D), lambda i: (i, 0))
    return pl.pallas_call(
        body, name="do_prep", grid=(rows // tm,),
        out_shape=(_sds((rows, 2 * D), BF16), _sds((D, rows), BF16)),
        in_specs=[row, row],
        out_specs=(pl.BlockSpec((tm, 2 * D), lambda i: (i, 0)), pl.BlockSpec((D, tm), lambda i: (0, i))),
        scratch_shapes=[pltpu.VMEM((D, 128), BF16), pltpu.VMEM((D, 2 * D), BF16), pltpu.VMEM((2 * 128, 2 * D), BF16)],
        compiler_params=_params(("arbitrary",), 40),
    )(*_pin(d_o, o))


def _q_post(dqr, pq, qg, cos, sin, nb, tm):
    rows = pq.shape[0]
    nt = rows // nb // tm

    def body(dq_ref, q_ref, g_ref, cos_ref, sin_ref, dq_o, dg_o):
        @pl.when((pl.program_id(0) == 0) & (pl.program_id(1) == 0))
        def _():
            dg_o[...] = jnp.zeros_like(dg_o)

        gv, cosv, sinv = g_ref[...], cos_ref[...], sin_ref[...]
        acc = jnp.zeros((1, GROUP_W), F32)
        mats = _qk_mats()
        for g in range(N_KV):
            gs = slice(GROUP_W * g, GROUP_W * (g + 1))
            qv = q_ref[:, gs]
            r = lax.rsqrt(_apply(qv * qv, mats[0]) + EPS)
            dq, dgr = _qk_bwd(dq_ref[:, gs], qv, r, gv, cosv, sinv, mats)
            dq_o[:, gs] = dq.astype(BF16)
            acc = acc + jnp.sum(dgr, axis=0, keepdims=True)
        dg_o[...] += acc

    row = pl.BlockSpec((tm, D), lambda b, i: (b * nt + i, 0))
    rope = pl.BlockSpec((tm, GROUP_W), lambda b, i: (i, 0))
    vec = pl.BlockSpec((1, GROUP_W), lambda b, i: (0, 0))
    return pl.pallas_call(
        body, name="q_post", grid=(nb, nt),
        out_shape=(_sds((rows, D), BF16), _sds((1, GROUP_W), F32)),
        in_specs=[row, row, vec, rope, rope], out_specs=(row, vec),
        compiler_params=_params(("arbitrary", "arbitrary"), 32),
    )(*_pin(dqr, pq, qg, cos, sin))


def _attn_bwd(q2, qst, kv4, doe, dot_, lse, nb, tq):
    rows = q2.shape[0]
    s_len = rows // nb
    nq = s_len // tq
    scale = 1.0 / math.sqrt(HEAD_DIM)

    def body(q_ref, qt_ref, k1_ref, v1_ref, k2_ref, v2_ref, doe_ref, dot_ref, lse_ref,
             dq_o, dkt_o, dvt_o, dkt2_o, dvt2_o):
        i = pl.program_id(2)
        lse = lse_ref[0, 0]
        k1, k2, v1, v2 = k1_ref[0, 0], k2_ref[0, 0], v1_ref[0, 0], v2_ref[0, 0]
        bias = _tail_bias()
        dkt1, dkt2 = jnp.zeros((HEAD_DIM, s_len), F32), jnp.zeros((HEAD_DIM, KEY_PAD), F32)
        dvt1, dvt2 = jnp.zeros((HEAD_DIM, s_len), F32), jnp.zeros((HEAD_DIM, KEY_PAD), F32)

        def products(h):
            qh = q_ref[:, HEAD_DIM * h:HEAD_DIM * (h + 1)]
            dh = doe_ref[:, 2 * HEAD_DIM * h:2 * HEAD_DIM * (h + 1)]
            return _dot_nt(qh, k1), _dot_nt(qh, k2) + bias, _dot_nt(dh, v1), _dot_nt(dh, v2)

        ahead = products(0)
        for h in range(GQA):
            hs = slice(HEAD_DIM * h, HEAD_DIM * (h + 1))
            s1, s2, dp1, dp2 = ahead
            if h + 1 < GQA:
                ahead = products(h + 1)
            lse_h = lse[:, h:h + 1]
            p1 = jnp.exp2(s1 - lse_h)
            p2 = jnp.exp2(s2 - lse_h)
            ds1 = (p1 * dp1).astype(BF16)
            ds2 = (p2 * dp2).astype(BF16)
            dq_o[:, hs] = (_dot(ds1, k1) + _dot(ds2, k2)) * scale
            dkt1 = dkt1 + _dot(qt_ref[hs, :], ds1)
            dkt2 = dkt2 + _dot(qt_ref[hs, :], ds2)
            dvt1 = dvt1 + _dot(dot_ref[hs, :], p1.astype(BF16))
            dvt2 = dvt2 + _dot(dot_ref[hs, :], p2.astype(BF16))

        @pl.when(i == 0)
        def _():
            dkt_o[0, 0], dkt2_o[0, 0], dvt_o[0, 0], dvt2_o[0, 0] = dkt1, dkt2, dvt1, dvt2

        @pl.when(i > 0)
        def _():
            dkt_o[0, 0] += dkt1
            dkt2_o[0, 0] += dkt2
            dvt_o[0, 0] += dvt1
            dvt2_o[0, 0] += dvt2

    qspec = pl.BlockSpec((tq, GROUP_W), lambda b, g, i: (b * nq + i, g))
    qtspec = pl.BlockSpec((GROUP_W, tq), lambda b, g, i: (g, b * nq + i))
    tspec = pl.BlockSpec((1, 1, HEAD_DIM, s_len), lambda b, g, i: (b, g, 0, 0))
    t2spec = pl.BlockSpec((1, 1, HEAD_DIM, KEY_PAD), lambda b, g, i: (b, g, 0, 0))
    tshape = _sds((nb, N_KV, HEAD_DIM, s_len), F32)
    t2shape = _sds((nb, N_KV, HEAD_DIM, KEY_PAD), F32)
    return pl.pallas_call(
        body, name="attn_bwd", grid=(nb, N_KV, nq),
        out_shape=(_sds((rows, D), F32), tshape, tshape, t2shape, t2shape),
        in_specs=[qspec, qtspec] + _kv_specs(s_len)
        + [pl.BlockSpec((tq, 2 * GROUP_W), lambda b, g, i: (b * nq + i, g)), qtspec,
           pl.BlockSpec((1, 1, tq, GQA), lambda b, g, i: (b, g, i, 0))],
        out_specs=(qspec, tspec, tspec, t2spec, t2spec),
        compiler_params=_params(("parallel", "parallel", "arbitrary"), 56),
    )(*_pin(q2, qst, *kv4, doe, dot_, lse))


def _kv_bwd(dkt, dvt, dkt2, dvt2, pkv, pm_kv, kg, cos, sin, nb):
    rows = pkv.shape[0]
    s_len = rows // nb
    tk = min(512, s_len)
    nt = s_len // tk

    def body(dk_ref, dv_ref, dk2_ref, dv2_ref, kv_ref, m_ref, g_ref, cos_ref, sin_ref, d_o, dm_o, dg_o):
        b, i = pl.program_id(0), pl.program_id(1)
        gv = g_ref[...]
        mats = _qk_mats()

        @pl.when((b == 0) & (i == 0))
        def _():
            dg_o[...] = jnp.zeros_like(dg_o)

        kx = kv_ref[:, :GROUP_W]
        r = lax.rsqrt(_apply(kx * kx, mats[0]) + EPS)
        dk, dgr = _qk_bwd(dk_ref[0].T, kx, r, gv, cos_ref[...], sin_ref[...], mats)
        d_o[:, :GROUP_W] = dk.astype(BF16)
        d_o[:, GROUP_W:] = dv_ref[0].T.astype(BF16)
        dg_o[...] += jnp.sum(dgr, axis=0, keepdims=True)

        @pl.when(i == 0)
        def _():
            kxm = m_ref[:, :GROUP_W]
            rm = lax.rsqrt(_apply(kxm * kxm, mats[0]) + EPS)
            dn = dk2_ref[0].T[0:N_META]
            dyg = dn * gv
            dm_o[0, :, :GROUP_W] = rm * dyg - kxm * (rm * rm * rm) * _apply(dyg * kxm, mats[0])
            dm_o[0, :, GROUP_W:] = dv2_ref[0].T[0:N_META]
            dg_o[...] += jnp.sum(dn * kxm * rm, axis=0, keepdims=True)

    tspec = pl.BlockSpec((1, GROUP_W, tk), lambda b, i: (b, 0, i))
    t2spec = pl.BlockSpec((1, GROUP_W, KEY_PAD), lambda b, i: (b, 0, 0))
    rope = pl.BlockSpec((tk, GROUP_W), lambda b, i: (i, 0))
    return pl.pallas_call(
        body, name="kv_bwd", grid=(nb, nt),
        out_shape=(_sds((rows, 512), BF16), _sds((nb, N_META, 512), F32),
                   _sds((1, GROUP_W), F32)),
        in_specs=[tspec, tspec, t2spec, t2spec, pl.BlockSpec((tk, 512), lambda b, i: (b * nt + i, 0)),
                  pl.BlockSpec((N_META, 512), lambda b, i: (0, 0)), pl.BlockSpec((1, GROUP_W), lambda b, i: (0, 0)),
                  rope, rope],
        out_specs=(pl.BlockSpec((tk, 512), lambda b, i: (b * nt + i, 0)),
                   pl.BlockSpec((1, N_META, 512), lambda b, i: (b, 0, 0)),
                   pl.BlockSpec((1, GROUP_W), lambda b, i: (0, 0))),
        compiler_params=_params(("arbitrary", "arbitrary"), 40),
    )(*_pin(dkt, dvt, dkt2, dvt2, pkv, pm_kv, kg, cos, sin))


def _conv_bwd(dc0, pconv, pm_conv, conv_w, nb, tm):
    rows = pconv.shape[0]
    nt = rows // nb // tm

    def body(dcur, dprev, dnxt, cur, meta, w_ref, da_o, dam_o, gw_o, dext, dsh, accw):
        b, i = pl.program_id(0), pl.program_id(1)
        dext[0:16] = jnp.zeros((16, D), F32)
        dext[16:32] = jnp.where(i == 0, 0.0, dprev[...])
        dext[32:32 + tm] = dcur[...]
        dext[32 + tm:48 + tm] = jnp.where(i == nt - 1, 0.0, dnxt[...])
        _shifted_copies(dsh, dext, tm + 40)
        accw[...] = jnp.zeros_like(accw)

        @pl.when((b == 0) & (i == 0))
        def _():
            gw_o[...] = jnp.zeros_like(gw_o)

        for c0 in range(0, D, 256):
            cs = slice(c0, c0 + 256)
            for r0 in range(0, tm, 32):
                cv = cur[r0:r0 + 32, c0:c0 + 256]
                sg = _sig(cur[r0:r0 + 32, D + c0:D + c0 + 256])
                u = cv * sg
                acc = jnp.zeros((32, 256), F32)
                for j in range(CONV_K):
                    d = _rows32(dsh, dext, r0 + 47 - j, cs)
                    acc = acc + d * w_ref[j:j + 1, cs]
                    p = d * u
                    accw[8 * j:8 * j + 8, cs] += (p[0:8] + p[8:16]) + (p[16:24] + p[24:32])
                da_o[r0:r0 + 32, cs] = (acc * sg).astype(BF16)
                da_o[r0:r0 + 32, D + c0:D + c0 + 256] = (acc * cv * sg * (1.0 - sg)).astype(BF16)
        for j in range(CONV_K):
            gw_o[j:j + 1, :] += jnp.sum(accw[8 * j:8 * j + 8, :], axis=0, keepdims=True)

        @pl.when(i == 0)
        def _():
            for c0 in range(0, D, 256):
                cs = slice(c0, c0 + 256)
                cv = meta[:, c0:c0 + 256]
                sg = _sig(meta[:, D + c0:D + c0 + 256])
                um = cv * sg
                acc = jnp.zeros((16, 256), F32)
                for j in range(CONV_K):
                    d = dext[31 - j:47 - j, cs]
                    acc = acc + d * w_ref[j:j + 1, cs]
                    gw_o[j:j + 1, cs] += jnp.sum(d * um, axis=0, keepdims=True)
                dam_o[0, :, cs] = acc * sg
                dam_o[0, :, D + c0:D + c0 + 256] = acc * cv * sg * (1.0 - sg)

    return pl.pallas_call(
        body, name="conv_bwd", grid=(nb, nt),
        out_shape=(_sds((rows, 2048), BF16), _sds((nb, N_META, 2048), F32),
                   _sds((32, D), F32)),
        in_specs=_halo_specs(D, tm, nt, rows)
        + [pl.BlockSpec((tm, 2048), lambda b, i: (b * nt + i, 0)),
           pl.BlockSpec((16, 2048), lambda b, i: (0, 0)), pl.BlockSpec((32, D), lambda b, i: (0, 0))],
        out_specs=(pl.BlockSpec((tm, 2048), lambda b, i: (b * nt + i, 0)),
                   pl.BlockSpec((1, N_META, 2048), lambda b, i: (b, 0, 0)),
                   pl.BlockSpec((32, D), lambda b, i: (0, 0))),
        scratch_shapes=[pltpu.VMEM((tm + 48, D), F32), pltpu.VMEM((8, tm + 40, D), F32),
                        pltpu.VMEM((8 * CONV_K, D), F32)],
        compiler_params=_params(("arbitrary", "arbitrary"), 48),
    )(*_pin(dc0, dc0, dc0, pconv, pm_conv, conv_w))


def _meta_bwd(dam, ddm, w_full, meta_full, norm_g):
    nb = dam.shape[0]

    def body(a_ref, d_ref, wc_ref, wkv_ref, m_ref, g_ref, gm_o, dg_o):
        a, d = a_ref[0], d_ref[0]
        for b in range(1, nb):
            a = a + a_ref[b]
            d = d + d_ref[b]
        dxn = _dot_nt(a.astype(BF16), wc_ref[...]) + _dot_nt(d.astype(BF16), wkv_ref[...])
        v = m_ref[...]
        r = lax.rsqrt(jnp.mean(v * v, axis=-1, keepdims=True) + EPS)
        gm_o[...] = _rms_bwd(dxn, v, r, g_ref[...])
        dg_o[...] = jnp.sum(dxn * v * r, axis=0, keepdims=True)

    return pl.pallas_call(
        body, name="meta_bwd", grid=(1,),
        out_shape=(_sds((N_META, D), F32), _sds((1, D), F32)),
        in_specs=[pl.BlockSpec((nb, N_META, 2048), lambda i: (0, 0, 0)), pl.BlockSpec((nb, N_META, 512), lambda i: (0, 0, 0)),
                  pl.BlockSpec((D, 2048), lambda i: (0, 0)), pl.BlockSpec((D, 512), lambda i: (0, G_KV[0] // 512)),
                  pl.BlockSpec((N_META, D), lambda i: (0, 0)), pl.BlockSpec((1, D), lambda i: (0, 0))],
        out_specs=(pl.BlockSpec((N_META, D), lambda i: (0, 0)), pl.BlockSpec((1, D), lambda i: (0, 0))),
        compiler_params=_params(("arbitrary",), 32),
    )(*_pin(dam, ddm, w_full, w_full, meta_full, norm_g))


def _dxn(d_groups, w_full, x2, dy, norm_g, dg_init, tm):
    rows = x2.shape[0]
    groups = (G_CONV, G_CZ, G_Q, G_KV, G_E)

    def body(da, db, dq, dd, de, w_hbm, x_ref, dy_ref, g_ref, gi_ref, gx_o, dg_o, w_vmem, sem):
        @pl.when(pl.program_id(0) == 0)
        def _():
            cp = pltpu.make_async_copy(w_hbm, w_vmem, sem)
            cp.start()
            cp.wait()
            dg_o[...] = gi_ref[...]

        dxn = jnp.zeros((tm, D), F32)
        for ref, (off, wd) in zip((da, db, dq, dd, de), groups):
            for c0 in range(0, wd, 512):
                dxn = dxn + _dot_nt(ref[:, c0:c0 + 512], w_vmem[:, off + c0:off + c0 + 512])
        v = x_ref[...]
        r = lax.rsqrt(jnp.mean(v * v, axis=-1, keepdims=True) + EPS)
        gx_o[...] = dy_ref[...] + _rms_bwd(dxn, v, r, g_ref[...])
        dg_o[...] += jnp.sum(dxn * v * r, axis=0, keepdims=True)

    row = lambda wd: pl.BlockSpec((tm, wd), lambda i: (i, 0))
    vec = pl.BlockSpec((1, D), lambda i: (0, 0))
    return pl.pallas_call(
        body, name="dxn", grid=(rows // tm,),
        out_shape=(_sds((rows, D), F32), _sds((1, D), F32)),
        in_specs=[row(wd) for _, wd in groups] + [pl.BlockSpec(memory_space=pl.ANY), row(D), row(D), vec, vec],
        out_specs=(row(D), vec),
        scratch_shapes=[pltpu.VMEM((D, IN_DIM), BF16), pltpu.SemaphoreType.DMA],
        compiler_params=_params(("arbitrary",), 56),
    )(*_pin(*d_groups, w_full, x2, dy, norm_g, dg_init))


def _wgrad(at, b, bufs, slot, col_off, name, meta=None):
    buf, bufb = bufs
    rows, n = b.shape
    tn = next(t for t in (1536, 1024, 512) if n % t == 0 and col_off % t == 0)
    tk = min(2048, rows)
    nk = rows // tk
    j0 = col_off // tn

    def body(*refs):
        if meta is None:
            at_ref, b_ref, _, _, o_ref, ob_ref = refs
        else:
            at_ref, b_ref, xm_ref, dm_ref, _, _, o_ref, ob_ref = refs
        k = pl.program_id(1)

        @pl.when(k == 0)
        def _():
            if meta is None:
                o_ref[0] = jnp.zeros((D, tn), F32)
            else:
                dm = dm_ref[0]
                for e in range(1, dm_ref.shape[0]):
                    dm = dm + dm_ref[e]
                dm = jnp.concatenate([dm, jnp.zeros((128 - N_META, tn), F32)], axis=0)
                o_ref[0] = _dot(xm_ref[...], dm.astype(BF16))

        o_ref[0] += _dot(at_ref[...], b_ref[...].astype(BF16))

        @pl.when(k == nk - 1)
        def _():
            ob_ref[0] = o_ref[0].astype(BF16)

    in_specs = [pl.BlockSpec((D, tk), lambda j, k: (0, k)), pl.BlockSpec((tk, tn), lambda j, k: (k, j))]
    args = [at, b]
    if meta is not None:
        xmt, dm = meta
        in_specs += [pl.BlockSpec((D, 128), lambda j, k: (0, 0)),
                     pl.BlockSpec((dm.shape[0], N_META, tn), lambda j, k: (0, 0, j))]
        args += [xmt, dm]
    in_specs += [pl.BlockSpec(memory_space=pl.ANY)] * 2
    args += [buf, bufb]
    blk = pl.BlockSpec((1, D, tn), lambda j, k: (slot, 0, j0 + j))
    return pl.pallas_call(
        body, name=name, grid=(n // tn, nk),
        out_shape=(_sds(buf.shape, F32), _sds(buf.shape, BF16)),
        in_specs=in_specs,
        out_specs=(blk, blk),
        input_output_aliases={len(args) - 2: 0, len(args) - 1: 1},
        compiler_params=_params(("parallel", "arbitrary"), 56),
    )(*_pin(*args))


def _rope_tables(s_len):
    pos = jnp.arange(s_len, dtype=jnp.int32)
    row_ids = (pos // GRID_W).astype(F32)
    col_ids = (pos % GRID_W).astype(F32)
    inv_freq = ROPE_THETA ** (-jnp.arange(ROPE_FREQS, dtype=F32) / ROPE_FREQS)
    a_row = row_ids[:, None] * inv_freq[None, :]
    a_col = col_ids[:, None] * inv_freq[None, :]
    ang = jnp.concatenate([a_row, a_row, a_col, a_col], axis=-1)
    return jnp.tile(jnp.cos(ang), (1, GQA)), jnp.tile(jnp.sin(ang), (1, GQA))


def _local_step(x, loss_target, norm_g, conv_b, cn_g, cn_b, q_g, k_g, w_full, w3_full, conv_w_full, meta_full,
                reduce_start=None):
    nb, s_len, _ = x.shape
    rows = nb * s_len
    x2 = x.reshape(rows, D)
    t2 = loss_target.reshape(rows, D)
    cos, sin = _rope_tables(s_len)
    qg = jnp.tile(q_g, (1, GQA))
    kg = jnp.tile(k_g, (1, N_KV))

    xnmt, pm_conv, pm_kv = _meta_fwd(meta_full, norm_g, w_full)
    pconv, pcz, pq, pkv, pe, xnt, q2, qst = _in_proj(x2, norm_g, w_full, qg, cos, sin, nb, 256)
    c0 = _conv_fwd(pconv, pm_conv, conv_w_full, conv_b, nb, 256)
    tq = min(512, s_len)
    kv4 = _kv_prep(pkv, pm_kv, kg, cos, sin, nb)
    o, lse = _attn_fwd(q2, kv4, nb, min(1024, s_len))
    if callable(w3_full):
        w3_full = w3_full(o)
    dy, mt, c3t, o2t, dyc, dya, d_o, dc0, dcz, de, sums = _mid(x2, t2, c0, pcz, o, pe, w3_full, cn_g, cn_b, 256)
    doe, dot_ = _do_prep(d_o, o, 256)
    dqr, dkt, dvt, dkt2, dvt2 = _attn_bwd(q2, qst, kv4, doe, dot_, lse, nb, min(1024, s_len))
    dq, dqg = _q_post(dqr, pq, qg, cos, sin, nb, 256)
    dd, ddm, dkg = _kv_bwd(dkt.reshape(nb, GROUP_W, s_len), dvt.reshape(nb, GROUP_W, s_len),
                           dkt2.reshape(nb, GROUP_W, KEY_PAD), dvt2.reshape(nb, GROUP_W, KEY_PAD),
                           pkv, pm_kv, kg, cos, sin, nb)
    da, dam, gcw = _conv_bwd(dc0, pconv, pm_conv, conv_w_full, nb, 256)
    gmeta, dng_m = _meta_bwd(dam, ddm, w_full, meta_full, norm_g)

    gw3 = (lax.empty((3, D, D), F32), lax.empty((3, D, D), BF16))
    gw3 = _wgrad(c3t, dyc, gw3, 0, 0, "wgrad_conv_out")
    gw3 = _wgrad(o2t, dya, gw3, 1, 0, "wgrad_attn_out")
    gw3 = _wgrad(mt, dy, gw3, 2, 0, "wgrad_out")
    gwin = (lax.empty((1, D, IN_DIM), F32), lax.empty((1, D, IN_DIM), BF16))
    gwin = _wgrad(xnt, da, gwin, 0, G_CONV[0], "wgrad_in_conv", meta=(xnmt, dam))
    gwin = _wgrad(xnt, dcz, gwin, 0, G_CZ[0], "wgrad_in_cz")
    gwin = _wgrad(xnt, dq, gwin, 0, G_Q[0], "wgrad_in_q")
    gwin = _wgrad(xnt, dd, gwin, 0, G_KV[0], "wgrad_in_kv", meta=(xnmt, ddm))
    gwin = _wgrad(xnt, de, gwin, 0, G_E[0], "wgrad_in_e")

    pending = None
    if reduce_start is not None:
        token, pending = reduce_start(gwin, gw3, gcw, gmeta)
        dng_m = dng_m + token[0:1, 0:1]
    gx, dng = _dxn((da, dcz, dq, dd, de), w_full, x2, dy, norm_g, dng_m, 512)

    zeros = jnp.zeros((1, D - 2 * GROUP_W), F32)
    smalls = jnp.concatenate([dng, sums[2:3], sums[0:1], sums[1:2], jnp.concatenate([dqg, dkg, zeros], axis=1),
                              sums[3:4], jnp.zeros((2, D), F32)], axis=0)
    return gx.reshape(nb, s_len, D), gwin, gw3, gcw, gmeta, smalls, pending


def _xyc():
    return lax.axis_index("x"), lax.axis_index("y"), lax.axis_index("c")


def _reduce_sibling(gwin, gw3v, gcm):
    def body(gwin_ref, gw3_ref, gcm_ref, r_win, r_w3, r_cm, send, recv):
        x, y, c = _xyc()
        o = 1 - c
        half = D // 2
        outs = ((gwin_ref.at[pl.ds(o * half, half), :], r_win), (gw3_ref.at[:, :, o], r_w3), (gcm_ref.at[o], r_cm))
        cps = []
        for a, (src, dst) in enumerate(outs):
            cp = pltpu.make_async_remote_copy(src_ref=src, dst_ref=dst, send_sem=send.at[a], recv_sem=recv.at[a],
                                              device_id=(x, y, o), device_id_type=MESH)
            cp.start()
            cps.append(cp)
        for cp in cps:
            cp.wait()

    any_spec = pl.BlockSpec(memory_space=pl.ANY)
    return pl.pallas_call(
        body, name="reduce_sibling",
        out_shape=(_sds((D // 2, IN_DIM), BF16), _sds((3, 4, 128, D), BF16),
                   _sds((24, D), F32)),
        in_specs=[any_spec] * 3, out_specs=(any_spec,) * 3,
        scratch_shapes=[pltpu.SemaphoreType.DMA((3,)), pltpu.SemaphoreType.DMA((3,))],
    )(*_pin(gwin, gw3v, gcm))


def _add_sibling(gwin, gw3v, gcm, r_win, r_w3, r_cm):
    c = lax.axis_index("c").astype(jnp.int32).reshape(1)
    half = D // 2
    tr = 64

    def body1(c_ref, a_ref, b_ref, o_ref):
        o_ref[...] = (a_ref[...] + b_ref[...].astype(F32)).astype(BF16)

    cs_win = pl.pallas_call(
        body1, name="add_sibling_w_in", out_shape=_sds((half, IN_DIM), BF16),
        grid_spec=pltpu.PrefetchScalarGridSpec(
            num_scalar_prefetch=1, grid=(half // tr,),
            in_specs=[pl.BlockSpec((tr, IN_DIM), lambda i, c_ref: (c_ref[0] * (half // tr) + i, 0)),
                      pl.BlockSpec((tr, IN_DIM), lambda i, c_ref: (i, 0))],
            out_specs=pl.BlockSpec((tr, IN_DIM), lambda i, c_ref: (i, 0))),
        compiler_params=_params(("parallel",), 32),
    )(c, *_pin(gwin, r_win))

    def body2(c_ref, a_ref, b_ref, o_ref):
        o_ref[0, 0] = (a_ref[0, 0, 0] + b_ref[0, 0].astype(F32)).astype(BF16)

    cs_w3 = pl.pallas_call(
        body2, name="add_sibling_w3", out_shape=_sds((3, 4, 128, D), BF16),
        grid_spec=pltpu.PrefetchScalarGridSpec(
            num_scalar_prefetch=1, grid=(3, 4),
            in_specs=[pl.BlockSpec((1, 1, 1, 128, D), lambda w, s, c_ref: (w, s, c_ref[0], 0, 0)),
                      pl.BlockSpec((1, 1, 128, D), lambda w, s, c_ref: (w, s, 0, 0))],
            out_specs=pl.BlockSpec((1, 1, 128, D), lambda w, s, c_ref: (w, s, 0, 0))),
        compiler_params=_params(("parallel", "parallel"), 32),
    )(c, *_pin(gw3v, r_w3))

    def body3(c_ref, a_ref, b_ref, o_ref):
        o_ref[...] = a_ref[0] + b_ref[...]

    cs_cm = pl.pallas_call(
        body3, name="add_sibling_cm", out_shape=_sds((24, D), F32),
        grid_spec=pltpu.PrefetchScalarGridSpec(
            num_scalar_prefetch=1, grid=(1,),
            in_specs=[pl.BlockSpec((1, 24, D), lambda i, c_ref: (c_ref[0], 0, 0)),
                      pl.BlockSpec((24, D), lambda i, c_ref: (0, 0))],
            out_specs=pl.BlockSpec((24, D), lambda i, c_ref: (0, 0))),
        compiler_params=_params(("arbitrary",), 32),
    )(c, *_pin(gcm, r_cm))
    return cs_win, cs_w3, cs_cm


def _reduce_chips_copies(srcs, lands, send, recv):
    win_ref, w3_ref, cm_ref = srcs
    r_win, r_w3, r_cm = lands
    x, y, c = _xyc()
    peers = ((1 - x, y), (x, 1 - y), (1 - x, 1 - y))
    cps = []
    for k, (px, py) in enumerate(peers):
        ps = 2 * px + py
        items = ((win_ref.at[:, pl.ds(ps * W_IN_SHARD, W_IN_SHARD)], r_win.at[k]),
                 (w3_ref.at[:, ps], r_w3.at[k]),
                 (cm_ref.at[:, pl.ds(ps * ROW_SHARD, ROW_SHARD)], r_cm.at[k]))
        for a, (src, dst) in enumerate(items):
            cps.append(pltpu.make_async_remote_copy(src_ref=src, dst_ref=dst, send_sem=send.at[3 * a + k],
                                                    recv_sem=recv.at[3 * a + k], device_id=(px, py, c),
                                                    device_id_type=MESH))
    return cps


_HBM = pl.BlockSpec(memory_space=pltpu.HBM)
_SEM = pl.BlockSpec(memory_space=pltpu.SEMAPHORE)
_EFFECT = pltpu.SideEffectType.DATAFLOW_SIDE_EFFECTING


def _reduce_chips_start(cs_win, cs_w3, cs_cm):
    srcs = (cs_win, cs_w3, cs_cm)
    lands = (lax.empty((3, D // 2, W_IN_SHARD), BF16), lax.empty((3, 3, 128, D), BF16),
             lax.empty((3, 24, ROW_SHARD), F32))

    def body(*refs):
        srcs_in, lands_in, send, recv, token = refs[0:3], refs[3:6], refs[6], refs[7], refs[14]
        for cp in _reduce_chips_copies(srcs_in, lands_in, send, recv):
            cp.start()
        token[...] = jnp.zeros_like(token)

    hbm = lambda a: pltpu.HBM(a.shape, a.dtype)
    outs = pl.pallas_call(
        body, name="reduce_chips_start",
        out_shape=(pltpu.SemaphoreType.DMA((9,)), pltpu.SemaphoreType.DMA((9,)),
                   *[hbm(a) for a in srcs], *[hbm(a) for a in lands], jax.ShapeDtypeStruct((8, 128), F32)),
        in_specs=[_HBM] * 6,
        out_specs=(_SEM, _SEM, *[_HBM] * 6, pl.BlockSpec(memory_space=pltpu.VMEM)),
        input_output_aliases={i: i + 2 for i in range(6)},
        compiler_params=pltpu.CompilerParams(has_side_effects=_EFFECT),
    )(*[pltpu.with_memory_space_constraint(a, pltpu.HBM) for a in srcs + lands])
    return outs[0], outs[1], outs[2:5], outs[5:8], outs[8]


def _reduce_chips_wait(send, recv, srcs, lands, after):
    def body(*refs):
        srcs_in, lands_in, send_ref, recv_ref = refs[0:3], refs[3:6], refs[6], refs[7]
        for cp in _reduce_chips_copies(srcs_in, lands_in, send_ref, recv_ref):
            cp.wait_send()
            cp.wait_recv()

    hbm = lambda a: pltpu.HBM(a.shape, a.dtype)
    outs = pl.pallas_call(
        body, name="reduce_chips_wait",
        out_shape=(*[hbm(a) for a in srcs], *[hbm(a) for a in lands]),
        in_specs=[_HBM] * 6 + [_SEM, _SEM, pl.BlockSpec(memory_space=pl.ANY)],
        out_specs=(_HBM,) * 6,
        input_output_aliases={i: i for i in range(6)},
        compiler_params=pltpu.CompilerParams(has_side_effects=_EFFECT),
    )(*srcs, *lands, send, recv, after)
    return outs[0:3], outs[3:6]


def _add_chips(cs_win, cs_w3, cs_cm, r_win, r_w3, r_cm):
    x, y, c = _xyc()
    idx = jnp.stack([2 * x + y, c]).astype(jnp.int32)
    half = D // 2
    tr = 128

    def body1(i_ref, a_ref, b_ref, o_ref):
        f = lambda v: v.astype(F32)
        o_ref[0] = (f(a_ref[...]) + f(b_ref[2])) + (f(b_ref[0]) + f(b_ref[1]))

    f_win = pl.pallas_call(
        body1, name="add_chips_w_in", out_shape=_sds((2, half, W_IN_SHARD), F32),
        grid_spec=pltpu.PrefetchScalarGridSpec(
            num_scalar_prefetch=1, grid=(half // tr,),
            in_specs=[pl.BlockSpec((tr, W_IN_SHARD), lambda i, r: (i, r[0])),
                      pl.BlockSpec((3, tr, W_IN_SHARD), lambda i, r: (0, i, 0))],
            out_specs=pl.BlockSpec((1, tr, W_IN_SHARD), lambda i, r: (r[1], i, 0))),
        compiler_params=_params(("parallel",), 32),
    )(idx, *_pin(cs_win, r_win))

    def body2(i_ref, a_ref, b_ref, o_ref):
        f = lambda v: v.astype(F32)
        o_ref[0, 0] = (f(a_ref[0, 0]) + f(b_ref[2, 0])) + (f(b_ref[0, 0]) + f(b_ref[1, 0]))

    f_w3 = pl.pallas_call(
        body2, name="add_chips_w3", out_shape=_sds((3, 2, 128, D), F32),
        grid_spec=pltpu.PrefetchScalarGridSpec(
            num_scalar_prefetch=1, grid=(3,),
            in_specs=[pl.BlockSpec((1, 1, 128, D), lambda w, r: (w, r[0], 0, 0)),
                      pl.BlockSpec((3, 1, 128, D), lambda w, r: (0, w, 0, 0))],
            out_specs=pl.BlockSpec((1, 1, 128, D), lambda w, r: (w, r[1], 0, 0))),
        compiler_params=_params(("parallel",), 32),
    )(idx, *_pin(cs_w3, r_w3))

    def body3(i_ref, a_ref, b_ref, o_ref):
        o_ref[0] = (a_ref[...] + b_ref[2]) + (b_ref[0] + b_ref[1])

    f_cm = pl.pallas_call(
        body3, name="add_chips_cm", out_shape=_sds((2, 24, ROW_SHARD), F32),
        grid_spec=pltpu.PrefetchScalarGridSpec(
            num_scalar_prefetch=1, grid=(1,),
            in_specs=[pl.BlockSpec((24, ROW_SHARD), lambda i, r: (0, r[0])),
                      pl.BlockSpec((3, 24, ROW_SHARD), lambda i, r: (0, 0, 0))],
            out_specs=pl.BlockSpec((1, 24, ROW_SHARD), lambda i, r: (r[1], 0, 0))),
        compiler_params=_params(("arbitrary",), 32),
    )(idx, *_pin(cs_cm, r_cm))
    return f_win, f_w3, f_cm


def _share_sibling(f_win, f_w3, f_cm, smalls):
    def body(win_in, w3_in, cm_in, sm_ref, win_ref, w3_ref, cm_ref, r_sm, send, recv, ssend, srecv, lsem):
        x, y, c = _xyc()
        o = 1 - c
        cps = []
        for a, (ref, sl) in enumerate(((win_ref, lambda h: win_ref.at[h]), (w3_ref, lambda h: w3_ref.at[:, h]),
                                       (cm_ref, lambda h: cm_ref.at[h]))):
            cp = pltpu.make_async_remote_copy(src_ref=sl(c), dst_ref=sl(c), send_sem=send.at[a], recv_sem=recv.at[a],
                                              device_id=(x, y, o), device_id_type=MESH)
            cp.start()
            cps.append((cp, sl))
        me = 4 * x + 2 * y + c
        loc = pltpu.make_async_copy(sm_ref, r_sm.at[me], lsem)
        loc.start()
        scps = []
        for d in range(1, 8):
            px, py, pc = (x + (d >> 2)) % 2, (y + ((d >> 1) & 1)) % 2, (c + (d & 1)) % 2
            cp = pltpu.make_async_remote_copy(src_ref=sm_ref, dst_ref=r_sm.at[me], send_sem=ssend.at[d - 1],
                                              recv_sem=srecv.at[d - 1], device_id=(px, py, pc), device_id_type=MESH)
            cp.start()
            scps.append((cp, 4 * px + 2 * py + pc))
        for a, (cp, sl) in enumerate(cps):
            pltpu.make_async_remote_copy(src_ref=sl(o), dst_ref=sl(o), send_sem=send.at[a], recv_sem=recv.at[a],
                                         device_id=(x, y, o), device_id_type=MESH).wait_recv()
            cp.wait_send()
        for d, (cp, pid) in enumerate(scps):
            pltpu.make_async_remote_copy(src_ref=sm_ref, dst_ref=r_sm.at[pid], send_sem=ssend.at[d],
                                         recv_sem=srecv.at[d], device_id=(x, y, c), device_id_type=MESH).wait_recv()
            cp.wait_send()
        loc.wait()

    any_spec = pl.BlockSpec(memory_space=pl.ANY)
    return pl.pallas_call(
        body, name="share_sibling",
        out_shape=(_sds(f_win.shape, F32), _sds(f_w3.shape, F32), _sds(f_cm.shape, F32), _sds((8, 8, D), F32)),
        in_specs=[any_spec] * 4, out_specs=(any_spec,) * 4,
        input_output_aliases={0: 0, 1: 1, 2: 2},
        scratch_shapes=[pltpu.SemaphoreType.DMA((3,)), pltpu.SemaphoreType.DMA((3,)),
                        pltpu.SemaphoreType.DMA((7,)), pltpu.SemaphoreType.DMA((7,)), pltpu.SemaphoreType.DMA],
    )(*_pin(f_win, f_w3, f_cm, smalls))


def _adamw_math(w, g, m, v):
    m = ADAM_B1 * m + (1.0 - ADAM_B1) * g
    v = ADAM_B2 * v + (1.0 - ADAM_B2) * (g * g)
    m_hat = m / (1.0 - ADAM_B1 ** ADAM_STEP)
    v_hat = v / (1.0 - ADAM_B2 ** ADAM_STEP)
    delta = -ADAM_LR * (m_hat / (jnp.sqrt(v_hat) + ADAM_EPS) + ADAM_WD * w)
    return delta, m, v


def _adamw(w, g, m, v, tr, name):
    rows, cols = w.shape

    def body(w_ref, g_ref, m_ref, v_ref, g_o, d_o, m_o, v_o):
        g = g_ref[...]
        g_o[...] = g
        d_o[...], m_o[...], v_o[...] = _adamw_math(w_ref[...], g, m_ref[...], v_ref[...])

    spec = pl.BlockSpec((tr, cols), lambda i: (i, 0))
    return pl.pallas_call(
        body, name=name, grid=(rows // tr,),
        out_shape=(_sds((rows, cols), F32),) * 4,
        in_specs=[spec] * 4, out_specs=(spec,) * 4,
        compiler_params=_params(("parallel",), 32),
    )(*_pin(w, g, m, v))


def _adamw3(g3, ws, ms, vs):
    def body(g_ref, *refs):
        w_refs, m_refs, v_refs, outs = refs[0:3], refs[3:6], refs[6:9], refs[9:]
        g_os, d_os, m_os, v_os = outs[0:3], outs[3:6], outs[6:9], outs[9:12]
        for i in range(3):
            g = g_ref[i]
            g_os[i][0] = g
            d_os[i][0], m_os[i][0], v_os[i][0] = _adamw_math(w_refs[i][0], g, m_refs[i][0], v_refs[i][0])

    return pl.pallas_call(
        body, name="adamw_w3", out_shape=(jax.ShapeDtypeStruct((1, ROW_SHARD, D), F32),) * 12,
        compiler_params=pltpu.CompilerParams(vmem_limit_bytes=48 << 20),
    )(g3, *ws, *ms, *vs)


def _adamw_cm(f_cm, ws, ms, vs):
    def body(f_ref, *refs):
        w_refs, m_refs, v_refs, outs = refs[0:2], refs[2:4], refs[4:6], refs[6:14]
        gcw, gmt = refs[14], refs[15]
        gcw[0:16] = f_ref[0, 0:16]
        gcw[16:32] = f_ref[1, 0:16]
        gmt[0:8] = f_ref[0, 16:24]
        gmt[8:16] = f_ref[1, 16:24]
        g_conv = gcw[0:CONV_K, :]
        g_meta = gmt[...]
        outs[0][0] = g_conv
        outs[1][...] = g_meta
        outs[2][0], outs[4][0], outs[6][0] = _adamw_math(w_refs[0][0], g_conv, m_refs[0][0], v_refs[0][0])
        outs[3][...], outs[5][...], outs[7][...] = _adamw_math(w_refs[1][...], g_meta, m_refs[1][...], v_refs[1][...])

    pair = (jax.ShapeDtypeStruct((1, CONV_K, ROW_SHARD), F32), jax.ShapeDtypeStruct((N_META, ROW_SHARD), F32))
    return pl.pallas_call(
        body, name="adamw_cm", out_shape=pair * 4,
        scratch_shapes=[pltpu.VMEM((32, ROW_SHARD), F32), pltpu.VMEM((N_META, ROW_SHARD), F32)],
    )(f_cm, *ws, *ms, *vs)


def _adamw_small(r_sm, ws, ms, vs):
    def body(s_ref, *refs):
        w_refs, m_refs, v_refs, outs = refs[0:6], refs[6:12], refs[12:18], refs[18:]
        loss_o, g_os, d_os, m_os, v_os = outs[0], outs[1:7], outs[7:13], outs[13:19], outs[19:25]
        g = s_ref[0]
        for dev in range(1, 8):
            g = g + s_ref[dev]
        qk = g[4:5, :]
        qg = qk[:, 0:HEAD_DIM]
        kg = qk[:, GROUP_W:GROUP_W + HEAD_DIM]
        for h in range(1, GQA):
            qg = qg + qk[:, HEAD_DIM * h:HEAD_DIM * (h + 1)]
            kg = kg + qk[:, GROUP_W + HEAD_DIM * h:GROUP_W + HEAD_DIM * (h + 1)]
        loss_o[...] = (0.5 / D) * jnp.sum(g[5:6, :], axis=-1, keepdims=True)
        for i, gi in enumerate((g[0:1], g[1:2], g[2:3], g[3:4], qg, kg)):
            g_os[i][...] = gi
            d_os[i][...], m_os[i][...], v_os[i][...] = _adamw_math(w_refs[i][...], gi, m_refs[i][...], v_refs[i][...])

    six = tuple(jax.ShapeDtypeStruct(w.shape, F32) for w in ws)
    return pl.pallas_call(
        body, name="adamw_small", out_shape=(jax.ShapeDtypeStruct((1, 1), F32),) + six * 4,
    )(r_sm, *ws, *ms, *vs)


def kernel(x, meta_tokens, norm_g, w_in, conv_w, conv_b, conv_norm_g, conv_norm_b, w_conv_out, q_norm_g, k_norm_g, w_attn_out, w_out, loss_target, m_meta_tokens, m_norm_g, m_w_in, m_conv_w, m_conv_b, m_conv_norm_g, m_conv_norm_b, m_w_conv_out, m_q_norm_g, m_k_norm_g, m_w_attn_out, m_w_out, v_meta_tokens, v_norm_g, v_w_in, v_conv_w, v_conv_b, v_conv_norm_g, v_conv_norm_b, v_w_conv_out, v_q_norm_g, v_k_norm_g, v_w_attn_out, v_w_out):
    pad_k = lambda a: jnp.pad(a[0], ((0, 32 - CONV_K), (0, 0)))
    w3_s = (w_conv_out, w_attn_out, w_out)
    w_full, conv_w_full, meta_full, w3b, w3_land = _gather_weights(w_in[0], w3_s, pad_k(conv_w), meta_tokens)
    w3_pending, token = _w3_start(w3b, w3_land)
    norm_g_fwd = norm_g + token[0:1, 0:1]

    def w3_full(after):
        return _w3_wait(*w3_pending, after)

    def reduce_start(gwin, gw3, gcw, gmeta):
        gwin2, gwin2b = (a.reshape(D, IN_DIM) for a in gwin)
        gw3v, gw3vb = (a.reshape(3, N_CHIPS, 2, 128, D) for a in gw3)
        gcm = jnp.concatenate([gcw.reshape(2, 16, D), gmeta.reshape(2, 8, D)], axis=1)
        r_win, r_w3, r_cm = _reduce_sibling(gwin2b, gw3vb, gcm)
        cs = _add_sibling(gwin2, gw3v, gcm, r_win, r_w3, r_cm)
        send, recv, srcs, lands, token = _reduce_chips_start(*cs)
        return token, (send, recv, srcs, lands)

    gx, _, _, _, _, smalls, pending = _local_step(
        x, loss_target, norm_g_fwd, conv_b, conv_norm_g, conv_norm_b, q_norm_g, k_norm_g,
        w_full, w3_full, conv_w_full, meta_full, reduce_start)
    (cs_win, cs_w3, cs_cm), (r2_win, r2_w3, r2_cm) = _reduce_chips_wait(*pending, gx)
    f_win, f_w3, f_cm = _add_chips(cs_win, cs_w3, cs_cm, r2_win, r2_w3, r2_cm)
    f_win, f_w3, f_cm, r_sm = _share_sibling(f_win, f_w3, f_cm, smalls)

    g_w_in, d_w_in, nm_w_in, nv_w_in = _adamw(w_in[0], f_win.reshape(D, W_IN_SHARD), m_w_in[0], v_w_in[0], 128,
                                              "adamw_w_in")
    w3 = _adamw3(f_w3.reshape(3, ROW_SHARD, D), w3_s, (m_w_conv_out, m_w_attn_out, m_w_out),
                 (v_w_conv_out, v_w_attn_out, v_w_out))
    cm = _adamw_cm(f_cm, (conv_w, meta_tokens), (m_conv_w, m_meta_tokens), (v_conv_w, v_meta_tokens))
    small = _adamw_small(
        r_sm, (norm_g, conv_b, conv_norm_g, conv_norm_b, q_norm_g, k_norm_g),
        (m_norm_g, m_conv_b, m_conv_norm_g, m_conv_norm_b, m_q_norm_g, m_k_norm_g),
        (v_norm_g, v_conv_b, v_conv_norm_g, v_conv_norm_b, v_q_norm_g, v_k_norm_g))

    def assemble(big_in, w3x, cmx, s6):
        ng, cb, cng, cnb, qg, kg = s6
        return (cmx[1], ng, big_in[None], cmx[0], cb, cng, cnb, w3x[0], qg, kg, w3x[1], w3x[2])

    loss = small[0].reshape(())
    grads = assemble(g_w_in, w3[0:3], cm[0:2], small[1:7])
    deltas = assemble(d_w_in, w3[3:6], cm[2:4], small[7:13])
    new_m = assemble(nm_w_in, w3[6:9], cm[4:6], small[13:19])
    new_v = assemble(nv_w_in, w3[9:12], cm[6:8], small[19:25])
    return (loss, gx, *grads, *deltas, *new_m, *new_v)
```

```python
import math

import jax
import jax.numpy as jnp
from jax import lax
from jax.experimental import pallas as pl
from jax.experimental.pallas import tpu as pltpu

F32, BF16 = jnp.float32, jnp.bfloat16
MESH = pl.DeviceIdType.MESH

D = 1024
N_META = 16
CONV_K = 31
N_KV = 4
GQA = 4
HEAD_DIM = 64
GROUP_W = GQA * HEAD_DIM
GRID_W = 64
ROPE_FREQS = 16
ROPE_THETA = 10000.0
EPS = 1e-6
IN_DIM = 7680
KEY_PAD = 128
G_CONV, G_CZ, G_Q, G_KV, G_E = (0, 2048), (2048, 1024), (3072, 1024), (4096, 512), (4608, 3072)
N_CHIPS = 4
W_IN_SHARD = IN_DIM // N_CHIPS
ROW_SHARD = D // N_CHIPS

ADAM_LR, ADAM_B1, ADAM_B2, ADAM_EPS, ADAM_WD, ADAM_STEP = 0.001, 0.9, 0.999, 1e-08, 0.01, 10

NT_DIMS = (((1,), (1,)), ((), ()))


def _params(sem=None, vmem_mb=48):
    return pltpu.CompilerParams(dimension_semantics=sem, vmem_limit_bytes=vmem_mb << 20)


def _sds(shape, dtype):
    return pltpu.HBM(tuple(shape), dtype)


def _pin(*arrays):
    return [pltpu.with_memory_space_constraint(a, pltpu.HBM) for a in arrays]


def _sig(v):
    return jax.nn.sigmoid(v)


def _dsilu(v, s):
    return s * (1.0 + v * (1.0 - s))


def _dot(a, b):
    return jnp.dot(a, b, preferred_element_type=F32)


def _dot_nt(a, b):
    return lax.dot_general(a, b, NT_DIMS, preferred_element_type=F32)


def _qk_mats():
    i = lax.broadcasted_iota(jnp.int32, (GROUP_W, GROUP_W), 0)
    j = lax.broadcasted_iota(jnp.int32, (GROUP_W, GROUP_W), 1)
    mean = jnp.where((i >> 6) == (j >> 6), 1.0 / HEAD_DIM, 0.0).astype(BF16)
    turn = jnp.where((i == j + 16) & ((j & 16) == 0), -1.0,
                     jnp.where((i == j - 16) & ((j & 16) != 0), 1.0, 0.0)).astype(BF16)
    return mean, turn


def _apply(v, mat):
    hi = v.astype(BF16)
    lo = (v - hi.astype(F32)).astype(BF16)
    return _dot(hi, mat) + _dot(lo, mat)


def _qk_fwd(v, g, cos, sin, mats):
    mean, turn = mats
    r = lax.rsqrt(_apply(v * v, mean) + EPS)
    n = v * r * g
    return n * cos + _apply(n, turn) * sin, r


def _qk_bwd(dy, v, r, g, cos, sin, mats):
    mean, turn = mats
    dn = dy * cos - _apply(dy, turn) * sin
    dyg = dn * g
    dv = r * dyg - v * (r * r * r) * _apply(dyg * v, mean)
    return dv, dn * v * r


def _rms_bwd(dxn, v, r, g):
    dxg = dxn * g
    return r * dxg - v * (r * r * r) * jnp.mean(dxg * v, axis=-1, keepdims=True)


def _glu(a):
    return a[:, :D] * _sig(a[:, D:])


def _gather_weights(w_in_s, w3_s, conv_w_s, meta_s):
    def body(win_ref, wa_ref, wb_ref, wc_ref, cw_ref, mt_ref, win_o, cw_o, mt_o, w3b_o, w3_o, win_b, w3_b,
             send, recv, fsend, frecv, lsem, csem):
        x, y, c = _xyc()
        o = 1 - c
        me = 2 * x + y
        win_b[...] = win_ref[...].astype(BF16)
        for i, ref in enumerate((wa_ref, wb_ref, wc_ref)):
            w3_b[i] = ref[0].astype(BF16)
        cast = pltpu.make_async_copy(w3_b, w3b_o, csem.at[0])
        cast.start()
        own = pltpu.make_async_copy(w3_b, _w3_place(w3_o, me), csem.at[1])
        own.start()
        items = (
            (lambda h: win_b.at[pl.ds(h * 512, 512), :],
             lambda p, h: win_o.at[pl.ds(h * 512, 512), pl.ds(p * W_IN_SHARD, W_IN_SHARD)]),
            (lambda h: cw_ref.at[pl.ds(h * 16, 16), :],
             lambda p, h: cw_o.at[pl.ds(h * 16, 16), pl.ds(p * ROW_SHARD, ROW_SHARD)]),
            (lambda h: mt_ref.at[pl.ds(h * 8, 8), :],
             lambda p, h: mt_o.at[pl.ds(h * 8, 8), pl.ds(p * ROW_SHARD, ROW_SHARD)]),
        )
        peers = ((1 - x, y), (x, 1 - y), (1 - x, 1 - y))

        def remote(src, dst, s_sem, r_sem, to):
            return pltpu.make_async_remote_copy(src_ref=src, dst_ref=dst, send_sem=s_sem, recv_sem=r_sem,
                                                device_id=to, device_id_type=MESH)

        started = []
        for a, (half, place) in enumerate(items):
            for h in range(2):
                loc = pltpu.make_async_copy(half(h), place(me, h), lsem.at[a, h])
                loc.start()
                started.append(loc.wait)
            for k, (px, py) in enumerate(peers):
                cp = remote(half(c), place(me, c), send.at[a, k], recv.at[a, k], (px, py, c))
                cp.start()
                started.append(cp.wait_send)
        for k, (px, py) in enumerate(peers):
            for a, (half, place) in enumerate(items):
                got = place(2 * px + py, c)
                remote(got, got, send.at[a, k], recv.at[a, k], (px, py, c)).wait_recv()
                fw = remote(got, got, fsend.at[a, k], frecv.at[a, k], (x, y, o))
                fw.start()
                started.append(fw.wait_send)
        for k, (px, py) in enumerate(peers):
            for a, (half, place) in enumerate(items):
                theirs = place(2 * px + py, o)
                remote(theirs, theirs, fsend.at[a, k], frecv.at[a, k], (x, y, o)).wait_recv()
        for wait in started:
            wait()
        cast.wait()
        own.wait()

    any_spec = pl.BlockSpec(memory_space=pl.ANY)
    vmem = pl.BlockSpec(memory_space=pltpu.VMEM)
    return pl.pallas_call(
        body, name="gather_weights",
        out_shape=(_sds((D, IN_DIM), BF16), _sds((32, D), F32), _sds((N_META, D), F32),
                   _sds((3, ROW_SHARD, D), BF16), _sds((3, D, D), BF16)),
        in_specs=[vmem] * 6,
        out_specs=(any_spec,) * 5,
        scratch_shapes=[pltpu.VMEM((D, W_IN_SHARD), BF16), pltpu.VMEM((3, ROW_SHARD, D), BF16),
                        pltpu.SemaphoreType.DMA((3, 3)), pltpu.SemaphoreType.DMA((3, 3)),
                        pltpu.SemaphoreType.DMA((3, 3)), pltpu.SemaphoreType.DMA((3, 3)),
                        pltpu.SemaphoreType.DMA((3, 2)), pltpu.SemaphoreType.DMA((2,))],
        compiler_params=pltpu.CompilerParams(vmem_limit_bytes=40 << 20),
    )(w_in_s, *w3_s, conv_w_s, meta_s)


def _w3_place(ref, p):
    return ref.at[:, pl.ds(p * ROW_SHARD, ROW_SHARD), :]


def _w3_copies(w3b_ref, land_ref, send, recv):
    x, y, c = _xyc()
    me = 2 * x + y
    peers = ((1 - x, y), (x, 1 - y), (1 - x, 1 - y))
    return [pltpu.make_async_remote_copy(src_ref=w3b_ref, dst_ref=_w3_place(land_ref, me),
                                         send_sem=send.at[k], recv_sem=recv.at[k], device_id=(px, py, c),
                                         device_id_type=MESH)
            for k, (px, py) in enumerate(peers)]


def _w3_start(w3b, land):
    def body(w3b_ref, land_ref, send, recv, w3b_thru, land_thru, token):
        for cp in _w3_copies(w3b_ref, land_ref, send, recv):
            cp.start()
        token[...] = jnp.zeros_like(token)

    outs = pl.pallas_call(
        body, name="w3_start",
        out_shape=(pltpu.SemaphoreType.DMA((3,)), pltpu.SemaphoreType.DMA((3,)),
                   pltpu.HBM(w3b.shape, BF16), pltpu.HBM(land.shape, BF16), jax.ShapeDtypeStruct((8, 128), F32)),
        in_specs=[_HBM, _HBM],
        out_specs=(_SEM, _SEM, _HBM, _HBM, pl.BlockSpec(memory_space=pltpu.VMEM)),
        input_output_aliases={0: 2, 1: 3},
        compiler_params=pltpu.CompilerParams(has_side_effects=_EFFECT),
    )(*_pin(w3b, land))
    return outs[0:4], outs[4]


def _w3_wait(send, recv, w3b, land, after):
    def body(w3b_ref, land_ref, send_ref, recv_ref, after_ref, w3b_out, land_out):
        x, y, c = _xyc()
        peers = ((1 - x, y), (x, 1 - y), (1 - x, 1 - y))
        for k, (cp, (px, py)) in enumerate(zip(_w3_copies(w3b_ref, land_ref, send_ref, recv_ref), peers)):
            cp.wait_send()
            got = _w3_place(land_ref, 2 * px + py)
            pltpu.make_async_remote_copy(src_ref=got, dst_ref=got, send_sem=send_ref.at[k], recv_sem=recv_ref.at[k],
                                         device_id=(px, py, c), device_id_type=MESH).wait_recv()

    outs = pl.pallas_call(
        body, name="w3_wait",
        out_shape=(pltpu.HBM(w3b.shape, BF16), pltpu.HBM(land.shape, BF16)),
        in_specs=[_HBM, _HBM, _SEM, _SEM, pl.BlockSpec(memory_space=pl.ANY)],
        out_specs=(_HBM, _HBM),
        input_output_aliases={0: 0, 1: 1},
        compiler_params=pltpu.CompilerParams(has_side_effects=_EFFECT),
    )(w3b, land, send, recv, after)
    return outs[1]


def _meta_fwd(meta_full, norm_g, w_full):
    def body(m_ref, g_ref, wc_ref, wkv_ref, xnt_ref, pc_ref, pkv_ref):
        v = m_ref[...]
        r = lax.rsqrt(jnp.mean(v * v, axis=-1, keepdims=True) + EPS)
        xn = v * r * g_ref[...]
        xnb = xn.astype(BF16)
        pad = jnp.concatenate([xn, jnp.zeros((128 - N_META, D), F32)], axis=0)
        xnt_ref[...] = pad.T.astype(BF16)
        pc_ref[...] = _dot(xnb, wc_ref[...])
        pkv_ref[...] = _dot(xnb, wkv_ref[...])

    return pl.pallas_call(
        body, name="meta_fwd", grid=(1,),
        out_shape=(_sds((D, 128), BF16), _sds((N_META, 2048), F32),
                   _sds((N_META, 512), F32)),
        in_specs=[pl.BlockSpec((N_META, D), lambda i: (0, 0)), pl.BlockSpec((1, D), lambda i: (0, 0)),
                  pl.BlockSpec((D, 2048), lambda i: (0, 0)), pl.BlockSpec((D, 512), lambda i: (0, G_KV[0] // 512))],
        out_specs=(pl.BlockSpec((D, 128), lambda i: (0, 0)), pl.BlockSpec((N_META, 2048), lambda i: (0, 0)),
                   pl.BlockSpec((N_META, 512), lambda i: (0, 0))),
        compiler_params=_params(("arbitrary",), 32),
    )(*_pin(meta_full, norm_g, w_full, w_full))


def _in_proj(x2, norm_g, w_full, qg, cos, sin, nb, tm):
    rows = x2.shape[0]
    nt = rows // nb // tm
    groups = (G_CONV, G_CZ, G_Q, G_KV, G_E)
    scale = 1.0 / math.sqrt(HEAD_DIM)

    def body(x_ref, g_ref, w_hbm, qg_ref, cos_ref, sin_ref, *rest):
        outs, xnt_ref, q2_o, qt_o, w_vmem, sem = rest[:5], rest[5], rest[6], rest[7], rest[8], rest[9]

        @pl.when(pl.program_id(0) == 0)
        def _():
            cp = pltpu.make_async_copy(w_hbm, w_vmem, sem)
            cp.start()
            cp.wait()

        v = x_ref[...]
        r = lax.rsqrt(jnp.mean(v * v, axis=-1, keepdims=True) + EPS)
        xn = v * r * g_ref[...]
        xnb = xn.astype(BF16)
        xnt_ref[...] = xn.T.astype(BF16)
        for ref, (off, wd) in zip(outs, groups):
            for c0 in range(0, wd, 512):
                ref[:, c0:c0 + 512] = _dot(xnb, w_vmem[:, off + c0:off + c0 + 512])
        gv, cosv, sinv = qg_ref[...], cos_ref[...], sin_ref[...]
        mats = _qk_mats()
        for g in range(N_KV):
            gs = slice(GROUP_W * g, GROUP_W * (g + 1))
            qr, _ = _qk_fwd(outs[2][:, gs], gv, cosv, sinv, mats)
            q2_o[:, gs] = (qr * (scale * LOG2E)).astype(BF16)
            qt_o[gs, :] = (qr * scale).T.astype(BF16)

    rope = pl.BlockSpec((tm, GROUP_W), lambda i: (lax.rem(i, nt), 0))
    return pl.pallas_call(
        body, name="in_proj", grid=(rows // tm,),
        out_shape=tuple(_sds((rows, wd), F32) for _, wd in groups)
        + (_sds((D, rows), BF16), _sds((rows, D), BF16), _sds((D, rows), BF16)),
        in_specs=[pl.BlockSpec((tm, D), lambda i: (i, 0)), pl.BlockSpec((1, D), lambda i: (0, 0)),
                  pl.BlockSpec(memory_space=pl.ANY), pl.BlockSpec((1, GROUP_W), lambda i: (0, 0)), rope, rope],
        out_specs=tuple(pl.BlockSpec((tm, wd), lambda i: (i, 0)) for _, wd in groups)
        + (pl.BlockSpec((D, tm), lambda i: (0, i)), pl.BlockSpec((tm, D), lambda i: (i, 0)),
           pl.BlockSpec((D, tm), lambda i: (0, i))),
        scratch_shapes=[pltpu.VMEM((D, IN_DIM), BF16), pltpu.SemaphoreType.DMA],
        compiler_params=_params(("arbitrary",), 58),
    )(*_pin(x2, norm_g, w_full, qg, cos, sin))


def _halo_specs(width, tm, nt, rows):
    h16 = tm // 16
    return [pl.BlockSpec((tm, width), lambda b, i: (b * nt + i, 0)),
            pl.BlockSpec((16, width), lambda b, i: (jnp.maximum((b * nt + i) * h16 - 1, 0), 0)),
            pl.BlockSpec((16, width), lambda b, i: (jnp.minimum((b * nt + i + 1) * h16, rows // 16 - 1), 0))]


def _fill_uext(uext, cur, prev, nxt, meta, i, nt, tm):
    uext[0:16] = jnp.where(i == 0, _glu(meta[...]), _glu(prev[...]))
    uext[16:16 + tm] = _glu(cur[...])
    uext[16 + tm:32 + tm] = jnp.where(i == nt - 1, 0.0, _glu(nxt[...]))


def _shifted_copies(dst, src, n):
    for r in range(1, 8):
        dst[r, 0:n] = src[r:r + n]


def _rows32(shifted, src, start, cols):
    q8, r = divmod(start, 8)
    if r == 0:
        return src[start:start + 32, cols]
    return shifted[r, 8 * q8:8 * q8 + 32, cols]


def _conv_fwd(pconv, pm_conv, conv_w, conv_b, nb, tm):
    rows = pconv.shape[0]
    nt = rows // nb // tm

    def body(cur, prev, nxt, meta, w_ref, b_ref, o_ref, uext, ush):
        i = pl.program_id(1)
        _fill_uext(uext, cur, prev, nxt, meta, i, nt, tm)
        _shifted_copies(ush, uext, tm + 24)
        for r0 in range(0, tm, 32):
            for c0 in range(0, D, 256):
                acc = jnp.zeros((32, 256), F32) + b_ref[:, c0:c0 + 256]
                for j in range(CONV_K):
                    acc = acc + _rows32(ush, uext, r0 + j + 1, slice(c0, c0 + 256)) * w_ref[j:j + 1, c0:c0 + 256]
                o_ref[r0:r0 + 32, c0:c0 + 256] = acc

    return pl.pallas_call(
        body, name="conv_fwd", grid=(nb, nt),
        out_shape=_sds((rows, D), F32),
        in_specs=_halo_specs(2048, tm, nt, rows)
        + [pl.BlockSpec((16, 2048), lambda b, i: (0, 0)), pl.BlockSpec((32, D), lambda b, i: (0, 0)),
           pl.BlockSpec((1, D), lambda b, i: (0, 0))],
        out_specs=pl.BlockSpec((tm, D), lambda b, i: (b * nt + i, 0)),
        scratch_shapes=[pltpu.VMEM((tm + 32, D), F32), pltpu.VMEM((8, tm + 24, D), F32)],
        compiler_params=_params(("parallel", "parallel"), 40),
    )(*_pin(pconv, pconv, pconv, pm_conv, conv_w, conv_b))


def _kv_prep(pkv, pm_kv, kg, cos, sin, nb):
    rows = pkv.shape[0]
    s_len = rows // nb
    tk = min(512, s_len)
    nt = s_len // tk

    def body(kv_ref, m_ref, g_ref, cos_ref, sin_ref, k_o, v_o, k2_o, v2_o):
        i = pl.program_id(1)
        mats = _qk_mats()
        kv = kv_ref[...]
        kr, _ = _qk_fwd(kv[:, :GROUP_W], g_ref[...], cos_ref[...], sin_ref[...], mats)
        ones = _ones_cols(tk, tk)
        for h in range(N_KV):
            k_o[0, h] = kr[:, HEAD_DIM * h:HEAD_DIM * (h + 1)].astype(BF16)
            vh = kv[:, GROUP_W + HEAD_DIM * h:GROUP_W + HEAD_DIM * (h + 1)]
            v_o[0, h] = jnp.concatenate([vh, ones], axis=1).astype(BF16)

        @pl.when(i == 0)
        def _():
            kvm = m_ref[...]
            km = kvm[:, :GROUP_W]
            kn = km * lax.rsqrt(_apply(km * km, mats[0]) + EPS) * g_ref[...]
            zeros = jnp.zeros((KEY_PAD - N_META, GROUP_W), F32)
            kfull = jnp.concatenate([kn, zeros], axis=0)
            vfull = jnp.concatenate([kvm[:, GROUP_W:], zeros], axis=0)
            ones_m = _ones_cols(KEY_PAD, N_META)
            for h in range(N_KV):
                k2_o[0, h] = kfull[:, HEAD_DIM * h:HEAD_DIM * (h + 1)].astype(BF16)
                v2_o[0, h] = jnp.concatenate([vfull[:, HEAD_DIM * h:HEAD_DIM * (h + 1)], ones_m], axis=1).astype(BF16)

    return pl.pallas_call(
        body, name="kv_prep", grid=(nb, nt),
        out_shape=(_sds((nb, N_KV, s_len, HEAD_DIM), BF16), _sds((nb, N_KV, s_len, 2 * HEAD_DIM), BF16),
                   _sds((nb, N_KV, KEY_PAD, HEAD_DIM), BF16), _sds((nb, N_KV, KEY_PAD, 2 * HEAD_DIM), BF16)),
        in_specs=[pl.BlockSpec((tk, 512), lambda b, i: (b * nt + i, 0)),
                  pl.BlockSpec((N_META, 512), lambda b, i: (0, 0)), pl.BlockSpec((1, GROUP_W), lambda b, i: (0, 0)),
                  pl.BlockSpec((tk, GROUP_W), lambda b, i: (i, 0)),
                  pl.BlockSpec((tk, GROUP_W), lambda b, i: (i, 0))],
        out_specs=(pl.BlockSpec((1, N_KV, tk, HEAD_DIM), lambda b, i: (b, 0, i, 0)),
                   pl.BlockSpec((1, N_KV, tk, 2 * HEAD_DIM), lambda b, i: (b, 0, i, 0)),
                   pl.BlockSpec((1, N_KV, KEY_PAD, HEAD_DIM), lambda b, i: (b, 0, 0, 0)),
                   pl.BlockSpec((1, N_KV, KEY_PAD, 2 * HEAD_DIM), lambda b, i: (b, 0, 0, 0))),
        compiler_params=_params(("parallel", "arbitrary"), 40),
    )(*_pin(pkv, pm_kv, kg, cos, sin))


def _ones_cols(rows, valid):
    r = lax.broadcasted_iota(jnp.int32, (rows, HEAD_DIM), 0)
    col = lax.broadcasted_iota(jnp.int32, (rows, HEAD_DIM), 1)
    return jnp.where((col < 2) & (r < valid), 1.0, 0.0).astype(F32)


def _tail_bias():
    col = lax.broadcasted_iota(jnp.int32, (1, KEY_PAD), 1)
    return jnp.where(col < N_META, 0.0, -1e30).astype(F32)


LOG2E = 1.4426950408889634


def _kv_specs(s_len):
    return [pl.BlockSpec((1, 1, s_len, HEAD_DIM), lambda b, g, i: (b, g, 0, 0)),
            pl.BlockSpec((1, 1, s_len, 2 * HEAD_DIM), lambda b, g, i: (b, g, 0, 0)),
            pl.BlockSpec((1, 1, KEY_PAD, HEAD_DIM), lambda b, g, i: (b, g, 0, 0)),
            pl.BlockSpec((1, 1, KEY_PAD, 2 * HEAD_DIM), lambda b, g, i: (b, g, 0, 0))]


def _attn_fwd(q2, kv4, nb, tq):
    rows = q2.shape[0]
    s_len = rows // nb
    nq = s_len // tq

    def body(q_ref, k1_ref, v1_ref, k2_ref, v2_ref, o_ref, lse_ref):
        qs = q_ref[...]
        k1, k2, v1, v2 = k1_ref[0, 0], k2_ref[0, 0], v1_ref[0, 0], v2_ref[0, 0]
        bias = _tail_bias()
        outs, lses = [], []

        def scores(h):
            qh = qs[:, HEAD_DIM * h:HEAD_DIM * (h + 1)]
            return _dot_nt(qh, k1), _dot_nt(qh, k2) + bias

        ahead = scores(0)
        for h in range(GQA):
            s1, s2 = ahead
            if h + 1 < GQA:
                ahead = scores(h + 1)
            m = jnp.maximum(jnp.max(s1, axis=-1, keepdims=True), jnp.max(s2, axis=-1, keepdims=True))
            oe = _dot(jnp.exp2(s1 - m).astype(BF16), v1) + _dot(jnp.exp2(s2 - m).astype(BF16), v2)
            l = oe[:, HEAD_DIM:HEAD_DIM + 1]
            outs.append(oe[:, :HEAD_DIM] / l)
            lses.append(m + jnp.log2(l))
        o_ref[...] = jnp.concatenate(outs, axis=1)
        lse_ref[0, 0] = jnp.concatenate(lses, axis=1)

    return pl.pallas_call(
        body, name="attn_fwd", grid=(nb, N_KV, nq),
        out_shape=(_sds((rows, D), F32), _sds((nb, N_KV, s_len, GQA), F32)),
        in_specs=[pl.BlockSpec((tq, GROUP_W), lambda b, g, i: (b * nq + i, g))] + _kv_specs(s_len),
        out_specs=(pl.BlockSpec((tq, GROUP_W), lambda b, g, i: (b * nq + i, g)),
                   pl.BlockSpec((1, 1, tq, GQA), lambda b, g, i: (b, g, i, 0))),
        compiler_params=_params(("parallel", "parallel", "parallel"), 48),
    )(*_pin(q2, *kv4))


def _mid(x2, t2, c0, cz, o, e, w3, cn_g, cn_b, tm):
    rows = x2.shape[0]

    def body(x_ref, t_ref, c0_ref, cz_ref, o_ref, e_ref, w_ref, g_ref, b_ref,
             dy_o, mt_o, c3t_o, o2t_o, dyc_o, dya_o, do_o, dc0_o, dcz_o, de_o, sums_o):
        wco, wao, wo = w_ref[0], w_ref[1], w_ref[2]
        cn_g_v = g_ref[...]

        @pl.when(pl.program_id(0) == 0)
        def _():
            sums_o[...] = jnp.zeros_like(sums_o)

        az, gc, ga = e_ref[:, :D], e_ref[:, D:2 * D], e_ref[:, 2 * D:]
        saz = _sig(az)
        gaz = az * saz
        ov = o_ref[...]
        o2 = ov * gaz
        ya = _dot(o2.astype(BF16), wao)
        o2t_o[...] = o2.T.astype(BF16)
        c0v = c0_ref[...]
        xc = c0v - jnp.mean(c0v, axis=-1, keepdims=True)
        rstd = lax.rsqrt(jnp.mean(xc * xc, axis=-1, keepdims=True) + EPS)
        n = xc * rstd
        c1 = n * cn_g_v + b_ref[...]
        s1 = _sig(c1)
        c2 = c1 * s1
        czv = cz_ref[...]
        sz = _sig(czv)
        gz = czv * sz
        c3 = c2 * gz
        yc = _dot(c3.astype(BF16), wco)
        c3t_o[...] = c3.T.astype(BF16)
        sc, sa = _sig(gc), _sig(ga)
        merged = sc * yc + sa * ya
        out = _dot(merged.astype(BF16), wo)
        mt_o[...] = merged.T.astype(BF16)
        err = x_ref[...] + out - t_ref[...]
        dy = err * (1.0 / D)
        dy_o[...] = dy
        dm = _dot_nt(dy.astype(BF16), wo)
        sums_o[3:4, :] += jnp.sum(err * err, axis=0, keepdims=True)
        dyc = dm * sc
        dya = dm * sa
        dycb, dyab = dyc.astype(BF16), dya.astype(BF16)
        dc3 = _dot_nt(dycb, wco)
        do2 = _dot_nt(dyab, wao)
        dyc_o[...] = dycb
        dya_o[...] = dyab
        de_o[:, D:2 * D] = (dyc * yc * (1.0 - sc)).astype(BF16)
        de_o[:, 2 * D:] = (dya * ya * (1.0 - sa)).astype(BF16)
        dcz_o[...] = (dc3 * c2 * _dsilu(czv, sz)).astype(BF16)
        dc1 = dc3 * gz * _dsilu(c1, s1)
        dn = dc1 * cn_g_v
        dc0 = rstd * (dn - jnp.mean(dn, axis=-1, keepdims=True) - n * jnp.mean(dn * n, axis=-1, keepdims=True))
        dc0_o[...] = dc0
        sums_o[0:1, :] += jnp.sum(dc1 * n, axis=0, keepdims=True)
        sums_o[1:2, :] += jnp.sum(dc1, axis=0, keepdims=True)
        sums_o[2:3, :] += jnp.sum(dc0, axis=0, keepdims=True)
        do_o[...] = do2 * gaz
        de_o[:, :D] = (do2 * ov * _dsilu(az, saz)).astype(BF16)

    row = lambda wd: pl.BlockSpec((tm, wd), lambda i: (i, 0))
    col = pl.BlockSpec((D, tm), lambda i: (0, i))
    vec = pl.BlockSpec((1, D), lambda i: (0, 0))
    f32o = lambda wd: _sds((rows, wd), F32)
    b16o = lambda wd: _sds((rows, wd), BF16)
    tpo = _sds((D, rows), BF16)
    return pl.pallas_call(
        body, name="mid", grid=(rows // tm,),
        out_shape=(f32o(D), tpo, tpo, tpo, b16o(D), b16o(D), f32o(D), f32o(D), b16o(D), b16o(3 * D),
                   _sds((8, D), F32)),
        in_specs=[row(D), row(D), row(D), row(D), row(D), row(3 * D),
                  pl.BlockSpec((3, D, D), lambda i: (0, 0, 0)), vec, vec],
        out_specs=(row(D), col, col, col, row(D), row(D), row(D), row(D), row(D), row(3 * D),
                   pl.BlockSpec((8, D), lambda i: (0, 0))),
        compiler_params=_params(("arbitrary",), 60),
    )(*_pin(x2, t2, c0, cz, o, e, w3, cn_g, cn_b))


def _do_prep(d_o, o, tm):
    rows = d_o.shape[0]

    def body(do_ref, o_ref, doe_o, dot_o, ind_ref, spread_ref, place_ref):
        @pl.when(pl.program_id(0) == 0)
        def _():
            def grid(shape):
                return (lax.broadcasted_iota(jnp.int32, shape, 0), lax.broadcasted_iota(jnp.int32, shape, 1))

            r, c = grid((D, 128))
            ind_ref[...] = jnp.where((r >> 6) == c, 1.0, 0.0).astype(BF16)
            r, c = grid((D, 2 * D))
            spread_ref[...] = jnp.where(c == 128 * (r >> 6) + (r & 63), 1.0, 0.0).astype(BF16)
            r, c = grid((2 * 128, 2 * D))
            place_ref[...] = jnp.where(c == 128 * (r & 127) + 64 + (r >> 7), -1.0, 0.0).astype(BF16)

        dov = do_ref[...]
        delta = _apply(dov * o_ref[...], ind_ref[...])
        d_hi = delta.astype(BF16)
        d_lo = (delta - d_hi.astype(F32)).astype(BF16)
        tails = _dot(jnp.concatenate([d_hi, d_lo], axis=1), place_ref[...])
        doe_o[...] = (_dot(dov.astype(BF16), spread_ref[...]) + tails).astype(BF16)
        dot_o[...] = dov.T.astype(BF16)

    row = pl.BlockSpec((tm, D), lambda i: (i, 0))
    return pl.pallas_call(
        body, name="do_prep", grid=(rows // tm,),
        out_shape=(_sds((rows, 2 * D), BF16), _sds((D, rows), BF16)),
        in_specs=[row, row],
        out_specs=(pl.BlockSpec((tm, 2 * D), lambda i: (i, 0)), pl.BlockSpec((D, tm), lambda i: (0, i))),
        scratch_shapes=[pltpu.VMEM((D, 128), BF16), pltpu.VMEM((D, 2 * D), BF16), pltpu.VMEM((2 * 128, 2 * D), BF16)],
        compiler_params=_params(("arbitrary",), 40),
    )(*_pin(d_o, o))


def _q_post(dqr, pq, qg, cos, sin, nb, tm):
    rows = pq.shape[0]
    nt = rows // nb // tm

    def body(dq_ref, q_ref, g_ref, cos_ref, sin_ref, dq_o, dg_o):
        @pl.when((pl.program_id(0) == 0) & (pl.program_id(1) == 0))
        def _():
            dg_o[...] = jnp.zeros_like(dg_o)

        gv, cosv, sinv = g_ref[...], cos_ref[...], sin_ref[...]
        acc = jnp.zeros((1, GROUP_W), F32)
        mats = _qk_mats()
        for g in range(N_KV):
            gs = slice(GROUP_W * g, GROUP_W * (g + 1))
            qv = q_ref[:, gs]
            r = lax.rsqrt(_apply(qv * qv, mats[0]) + EPS)
            dq, dgr = _qk_bwd(dq_ref[:, gs], qv, r, gv, cosv, sinv, mats)
            dq_o[:, gs] = dq.astype(BF16)
            acc = acc + jnp.sum(dgr, axis=0, keepdims=True)
        dg_o[...] += acc

    row = pl.BlockSpec((tm, D), lambda b, i: (b * nt + i, 0))
    rope = pl.BlockSpec((tm, GROUP_W), lambda b, i: (i, 0))
    vec = pl.BlockSpec((1, GROUP_W), lambda b, i: (0, 0))
    return pl.pallas_call(
        body, name="q_post", grid=(nb, nt),
        out_shape=(_sds((rows, D), BF16), _sds((1, GROUP_W), F32)),
        in_specs=[row, row, vec, rope, rope], out_specs=(row, vec),
        compiler_params=_params(("arbitrary", "arbitrary"), 32),
    )(*_pin(dqr, pq, qg, cos, sin))


def _attn_bwd(q2, qst, kv4, doe, dot_, lse, nb, tq):
    rows = q2.shape[0]
    s_len = rows // nb
    nq = s_len // tq
    scale = 1.0 / math.sqrt(HEAD_DIM)

    def body(q_ref, qt_ref, k1_ref, v1_ref, k2_ref, v2_ref, doe_ref, dot_ref, lse_ref,
             dq_o, dkt_o, dvt_o, dkt2_o, dvt2_o):
        i = pl.program_id(2)
        lse = lse_ref[0, 0]
        k1, k2, v1, v2 = k1_ref[0, 0], k2_ref[0, 0], v1_ref[0, 0], v2_ref[0, 0]
        bias = _tail_bias()
        dkt1, dkt2 = jnp.zeros((HEAD_DIM, s_len), F32), jnp.zeros((HEAD_DIM, KEY_PAD), F32)
        dvt1, dvt2 = jnp.zeros((HEAD_DIM, s_len), F32), jnp.zeros((HEAD_DIM, KEY_PAD), F32)

        def products(h):
            qh = q_ref[:, HEAD_DIM * h:HEAD_DIM * (h + 1)]
            dh = doe_ref[:, 2 * HEAD_DIM * h:2 * HEAD_DIM * (h + 1)]
            return _dot_nt(qh, k1), _dot_nt(qh, k2) + bias, _dot_nt(dh, v1), _dot_nt(dh, v2)

        ahead = products(0)
        for h in range(GQA):
            hs = slice(HEAD_DIM * h, HEAD_DIM * (h + 1))
            s1, s2, dp1, dp2 = ahead
            if h + 1 < GQA:
                ahead = products(h + 1)
            lse_h = lse[:, h:h + 1]
            p1 = jnp.exp2(s1 - lse_h)
            p2 = jnp.exp2(s2 - lse_h)
            ds1 = (p1 * dp1).astype(BF16)
            ds2 = (p2 * dp2).astype(BF16)
            dq_o[:, hs] = (_dot(ds1, k1) + _dot(ds2, k2)) * scale
            dkt1 = dkt1 + _dot(qt_ref[hs, :], ds1)
            dkt2 = dkt2 + _dot(qt_ref[hs, :], ds2)
            dvt1 = dvt1 + _dot(dot_ref[hs, :], p1.astype(BF16))
            dvt2 = dvt2 + _dot(dot_ref[hs, :], p2.astype(BF16))

        @pl.when(i == 0)
        def _():
            dkt_o[0, 0], dkt2_o[0, 0], dvt_o[0, 0], dvt2_o[0, 0] = dkt1, dkt2, dvt1, dvt2

        @pl.when(i > 0)
        def _():
            dkt_o[0, 0] += dkt1
            dkt2_o[0, 0] += dkt2
            dvt_o[0, 0] += dvt1
            dvt2_o[0, 0] += dvt2

    qspec = pl.BlockSpec((tq, GROUP_W), lambda b, g, i: (b * nq + i, g))
    qtspec = pl.BlockSpec((GROUP_W, tq), lambda b, g, i: (g, b * nq + i))
    tspec = pl.BlockSpec((1, 1, HEAD_DIM, s_len), lambda b, g, i: (b, g, 0, 0))
    t2spec = pl.BlockSpec((1, 1, HEAD_DIM, KEY_PAD), lambda b, g, i: (b, g, 0, 0))
    tshape = _sds((nb, N_KV, HEAD_DIM, s_len), F32)
    t2shape = _sds((nb, N_KV, HEAD_DIM, KEY_PAD), F32)
    return pl.pallas_call(
        body, name="attn_bwd", grid=(nb, N_KV, nq),
        out_shape=(_sds((rows, D), F32), tshape, tshape, t2shape, t2shape),
        in_specs=[qspec, qtspec] + _kv_specs(s_len)
        + [pl.BlockSpec((tq, 2 * GROUP_W), lambda b, g, i: (b * nq + i, g)), qtspec,
           pl.BlockSpec((1, 1, tq, GQA), lambda b, g, i: (b, g, i, 0))],
        out_specs=(qspec, tspec, tspec, t2spec, t2spec),
        compiler_params=_params(("parallel", "parallel", "arbitrary"), 56),
    )(*_pin(q2, qst, *kv4, doe, dot_, lse))


def _kv_bwd(dkt, dvt, dkt2, dvt2, pkv, pm_kv, kg, cos, sin, nb):
    rows = pkv.shape[0]
    s_len = rows // nb
    tk = min(512, s_len)
    nt = s_len // tk

    def body(dk_ref, dv_ref, dk2_ref, dv2_ref, kv_ref, m_ref, g_ref, cos_ref, sin_ref, d_o, dm_o, dg_o):
        b, i = pl.program_id(0), pl.program_id(1)
        gv = g_ref[...]
        mats = _qk_mats()

        @pl.when((b == 0) & (i == 0))
        def _():
            dg_o[...] = jnp.zeros_like(dg_o)

        kx = kv_ref[:, :GROUP_W]
        r = lax.rsqrt(_apply(kx * kx, mats[0]) + EPS)
        dk, dgr = _qk_bwd(dk_ref[0].T, kx, r, gv, cos_ref[...], sin_ref[...], mats)
        d_o[:, :GROUP_W] = dk.astype(BF16)
        d_o[:, GROUP_W:] = dv_ref[0].T.astype(BF16)
        dg_o[...] += jnp.sum(dgr, axis=0, keepdims=True)

        @pl.when(i == 0)
        def _():
            kxm = m_ref[:, :GROUP_W]
            rm = lax.rsqrt(_apply(kxm * kxm, mats[0]) + EPS)
            dn = dk2_ref[0].T[0:N_META]
            dyg = dn * gv
            dm_o[0, :, :GROUP_W] = rm * dyg - kxm * (rm * rm * rm) * _apply(dyg * kxm, mats[0])
            dm_o[0, :, GROUP_W:] = dv2_ref[0].T[0:N_META]
            dg_o[...] += jnp.sum(dn * kxm * rm, axis=0, keepdims=True)

    tspec = pl.BlockSpec((1, GROUP_W, tk), lambda b, i: (b, 0, i))
    t2spec = pl.BlockSpec((1, GROUP_W, KEY_PAD), lambda b, i: (b, 0, 0))
    rope = pl.BlockSpec((tk, GROUP_W), lambda b, i: (i, 0))
    return pl.pallas_call(
        body, name="kv_bwd", grid=(nb, nt),
        out_shape=(_sds((rows, 512), BF16), _sds((nb, N_META, 512), F32),
                   _sds((1, GROUP_W), F32)),
        in_specs=[tspec, tspec, t2spec, t2spec, pl.BlockSpec((tk, 512), lambda b, i: (b * nt + i, 0)),
                  pl.BlockSpec((N_META, 512), lambda b, i: (0, 0)), pl.BlockSpec((1, GROUP_W), lambda b, i: (0, 0)),
                  rope, rope],
        out_specs=(pl.BlockSpec((tk, 512), lambda b, i: (b * nt + i, 0)),
                   pl.BlockSpec((1, N_META, 512), lambda b, i: (b, 0, 0)),
                   pl.BlockSpec((1, GROUP_W), lambda b, i: (0, 0))),
        compiler_params=_params(("arbitrary", "arbitrary"), 40),
    )(*_pin(dkt, dvt, dkt2, dvt2, pkv, pm_kv, kg, cos, sin))


def _conv_bwd(dc0, pconv, pm_conv, conv_w, nb, tm):
    rows = pconv.shape[0]
    nt = rows // nb // tm

    def body(dcur, dprev, dnxt, cur, meta, w_ref, da_o, dam_o, gw_o, dext, dsh, accw):
        b, i = pl.program_id(0), pl.program_id(1)
        dext[0:16] = jnp.zeros((16, D), F32)
        dext[16:32] = jnp.where(i == 0, 0.0, dprev[...])
        dext[32:32 + tm] = dcur[...]
        dext[32 + tm:48 + tm] = jnp.where(i == nt - 1, 0.0, dnxt[...])
        _shifted_copies(dsh, dext, tm + 40)
        accw[...] = jnp.zeros_like(accw)

        @pl.when((b == 0) & (i == 0))
        def _():
            gw_o[...] = jnp.zeros_like(gw_o)

        for c0 in range(0, D, 256):
            cs = slice(c0, c0 + 256)
            for r0 in range(0, tm, 32):
                cv = cur[r0:r0 + 32, c0:c0 + 256]
                sg = _sig(cur[r0:r0 + 32, D + c0:D + c0 + 256])
                u = cv * sg
                acc = jnp.zeros((32, 256), F32)
                for j in range(CONV_K):
                    d = _rows32(dsh, dext, r0 + 47 - j, cs)
                    acc = acc + d * w_ref[j:j + 1, cs]
                    p = d * u
                    accw[8 * j:8 * j + 8, cs] += (p[0:8] + p[8:16]) + (p[16:24] + p[24:32])
                da_o[r0:r0 + 32, cs] = (acc * sg).astype(BF16)
                da_o[r0:r0 + 32, D + c0:D + c0 + 256] = (acc * cv * sg * (1.0 - sg)).astype(BF16)
        for j in range(CONV_K):
            gw_o[j:j + 1, :] += jnp.sum(accw[8 * j:8 * j + 8, :], axis=0, keepdims=True)

        @pl.when(i == 0)
        def _():
            for c0 in range(0, D, 256):
                cs = slice(c0, c0 + 256)
                cv = meta[:, c0:c0 + 256]
                sg = _sig(meta[:, D + c0:D + c0 + 256])
                um = cv * sg
                acc = jnp.zeros((16, 256), F32)
                for j in range(CONV_K):
                    d = dext[31 - j:47 - j, cs]
                    acc = acc + d * w_ref[j:j + 1, cs]
                    gw_o[j:j + 1, cs] += jnp.sum(d * um, axis=0, keepdims=True)
                dam_o[0, :, cs] = acc * sg
                dam_o[0, :, D + c0:D + c0 + 256] = acc * cv * sg * (1.0 - sg)

    return pl.pallas_call(
        body, name="conv_bwd", grid=(nb, nt),
        out_shape=(_sds((rows, 2048), BF16), _sds((nb, N_META, 2048), F32),
                   _sds((32, D), F32)),
        in_specs=_halo_specs(D, tm, nt, rows)
        + [pl.BlockSpec((tm, 2048), lambda b, i: (b * nt + i, 0)),
           pl.BlockSpec((16, 2048), lambda b, i: (0, 0)), pl.BlockSpec((32, D), lambda b, i: (0, 0))],
        out_specs=(pl.BlockSpec((tm, 2048), lambda b, i: (b * nt + i, 0)),
                   pl.BlockSpec((1, N_META, 2048), lambda b, i: (b, 0, 0)),
                   pl.BlockSpec((32, D), lambda b, i: (0, 0))),
        scratch_shapes=[pltpu.VMEM((tm + 48, D), F32), pltpu.VMEM((8, tm + 40, D), F32),
                        pltpu.VMEM((8 * CONV_K, D), F32)],
        compiler_params=_params(("arbitrary", "arbitrary"), 48),
    )(*_pin(dc0, dc0, dc0, pconv, pm_conv, conv_w))


def _meta_bwd(dam, ddm, w_full, meta_full, norm_g):
    nb = dam.shape[0]

    def body(a_ref, d_ref, wc_ref, wkv_ref, m_ref, g_ref, gm_o, dg_o):
        a, d = a_ref[0], d_ref[0]
        for b in range(1, nb):
            a = a + a_ref[b]
            d = d + d_ref[b]
        dxn = _dot_nt(a.astype(BF16), wc_ref[...]) + _dot_nt(d.astype(BF16), wkv_ref[...])
        v = m_ref[...]
        r = lax.rsqrt(jnp.mean(v * v, axis=-1, keepdims=True) + EPS)
        gm_o[...] = _rms_bwd(dxn, v, r, g_ref[...])
        dg_o[...] = jnp.sum(dxn * v * r, axis=0, keepdims=True)

    return pl.pallas_call(
        body, name="meta_bwd", grid=(1,),
        out_shape=(_sds((N_META, D), F32), _sds((1, D), F32)),
        in_specs=[pl.BlockSpec((nb, N_META, 2048), lambda i: (0, 0, 0)), pl.BlockSpec((nb, N_META, 512), lambda i: (0, 0, 0)),
                  pl.BlockSpec((D, 2048), lambda i: (0, 0)), pl.BlockSpec((D, 512), lambda i: (0, G_KV[0] // 512)),
                  pl.BlockSpec((N_META, D), lambda i: (0, 0)), pl.BlockSpec((1, D), lambda i: (0, 0))],
        out_specs=(pl.BlockSpec((N_META, D), lambda i: (0, 0)), pl.BlockSpec((1, D), lambda i: (0, 0))),
        compiler_params=_params(("arbitrary",), 32),
    )(*_pin(dam, ddm, w_full, w_full, meta_full, norm_g))


def _dxn(d_groups, w_full, x2, dy, norm_g, dg_init, tm):
    rows = x2.shape[0]
    groups = (G_CONV, G_CZ, G_Q, G_KV, G_E)

    def body(da, db, dq, dd, de, w_hbm, x_ref, dy_ref, g_ref, gi_ref, gx_o, dg_o, w_vmem, sem):
        @pl.when(pl.program_id(0) == 0)
        def _():
            cp = pltpu.make_async_copy(w_hbm, w_vmem, sem)
            cp.start()
            cp.wait()
            dg_o[...] = gi_ref[...]

        dxn = jnp.zeros((tm, D), F32)
        for ref, (off, wd) in zip((da, db, dq, dd, de), groups):
            for c0 in range(0, wd, 512):
                dxn = dxn + _dot_nt(ref[:, c0:c0 + 512], w_vmem[:, off + c0:off + c0 + 512])
        v = x_ref[...]
        r = lax.rsqrt(jnp.mean(v * v, axis=-1, keepdims=True) + EPS)
        gx_o[...] = dy_ref[...] + _rms_bwd(dxn, v, r, g_ref[...])
        dg_o[...] += jnp.sum(dxn * v * r, axis=0, keepdims=True)

    row = lambda wd: pl.BlockSpec((tm, wd), lambda i: (i, 0))
    vec = pl.BlockSpec((1, D), lambda i: (0, 0))
    return pl.pallas_call(
        body, name="dxn", grid=(rows // tm,),
        out_shape=(_sds((rows, D), F32), _sds((1, D), F32)),
        in_specs=[row(wd) for _, wd in groups] + [pl.BlockSpec(memory_space=pl.ANY), row(D), row(D), vec, vec],
        out_specs=(row(D), vec),
        scratch_shapes=[pltpu.VMEM((D, IN_DIM), BF16), pltpu.SemaphoreType.DMA],
        compiler_params=_params(("arbitrary",), 56),
    )(*_pin(*d_groups, w_full, x2, dy, norm_g, dg_init))


def _wgrad(at, b, bufs, slot, col_off, name, meta=None):
    buf, bufb = bufs
    rows, n = b.shape
    tn = next(t for t in (1536, 1024, 512) if n % t == 0 and col_off % t == 0)
    tk = min(2048, rows)
    nk = rows // tk
    j0 = col_off // tn

    def body(*refs):
        if meta is None:
            at_ref, b_ref, _, _, o_ref, ob_ref = refs
        else:
            at_ref, b_ref, xm_ref, dm_ref, _, _, o_ref, ob_ref = refs
        k = pl.program_id(1)

        @pl.when(k == 0)
        def _():
            if meta is None:
                o_ref[0] = jnp.zeros((D, tn), F32)
            else:
                dm = dm_ref[0]
                for e in range(1, dm_ref.shape[0]):
                    dm = dm + dm_ref[e]
                dm = jnp.concatenate([dm, jnp.zeros((128 - N_META, tn), F32)], axis=0)
                o_ref[0] = _dot(xm_ref[...], dm.astype(BF16))

        o_ref[0] += _dot(at_ref[...], b_ref[...].astype(BF16))

        @pl.when(k == nk - 1)
        def _():
            ob_ref[0] = o_ref[0].astype(BF16)

    in_specs = [pl.BlockSpec((D, tk), lambda j, k: (0, k)), pl.BlockSpec((tk, tn), lambda j, k: (k, j))]
    args = [at, b]
    if meta is not None:
        xmt, dm = meta
        in_specs += [pl.BlockSpec((D, 128), lambda j, k: (0, 0)),
                     pl.BlockSpec((dm.shape[0], N_META, tn), lambda j, k: (0, 0, j))]
        args += [xmt, dm]
    in_specs += [pl.BlockSpec(memory_space=pl.ANY)] * 2
    args += [buf, bufb]
    blk = pl.BlockSpec((1, D, tn), lambda j, k: (slot, 0, j0 + j))
    return pl.pallas_call(
        body, name=name, grid=(n // tn, nk),
        out_shape=(_sds(buf.shape, F32), _sds(buf.shape, BF16)),
        in_specs=in_specs,
        out_specs=(blk, blk),
        input_output_aliases={len(args) - 2: 0, len(args) - 1: 1},
        compiler_params=_params(("parallel", "arbitrary"), 56),
    )(*_pin(*args))


def _rope_tables(s_len):
    pos = jnp.arange(s_len, dtype=jnp.int32)
    row_ids = (pos // GRID_W).astype(F32)
    col_ids = (pos % GRID_W).astype(F32)
    inv_freq = ROPE_THETA ** (-jnp.arange(ROPE_FREQS, dtype=F32) / ROPE_FREQS)
    a_row = row_ids[:, None] * inv_freq[None, :]
    a_col = col_ids[:, None] * inv_freq[None, :]
    ang = jnp.concatenate([a_row, a_row, a_col, a_col], axis=-1)
    return jnp.tile(jnp.cos(ang), (1, GQA)), jnp.tile(jnp.sin(ang), (1, GQA))


def _local_step(x, loss_target, norm_g, conv_b, cn_g, cn_b, q_g, k_g, w_full, w3_full, conv_w_full, meta_full,
                hooks=None):
    nb, s_len, _ = x.shape
    rows = nb * s_len
    x2 = x.reshape(rows, D)
    t2 = loss_target.reshape(rows, D)
    cos, sin = _rope_tables(s_len)
    qg = jnp.tile(q_g, (1, GQA))
    kg = jnp.tile(k_g, (1, N_KV))

    xnmt, pm_conv, pm_kv = _meta_fwd(meta_full, norm_g, w_full)
    pconv, pcz, pq, pkv, pe, xnt, q2, qst = _in_proj(x2, norm_g, w_full, qg, cos, sin, nb, 256)
    c0 = _conv_fwd(pconv, pm_conv, conv_w_full, conv_b, nb, 256)
    tq = min(512, s_len)
    kv4 = _kv_prep(pkv, pm_kv, kg, cos, sin, nb)
    o, lse = _attn_fwd(q2, kv4, nb, min(1024, s_len))
    if callable(w3_full):
        w3_full = w3_full(o)
    dy, mt, c3t, o2t, dyc, dya, d_o, dc0, dcz, de, sums = _mid(x2, t2, c0, pcz, o, pe, w3_full, cn_g, cn_b, 256)
    gw3 = (lax.empty((3, D, D), F32), lax.empty((3, D, D), BF16))
    gw3 = _wgrad(c3t, dyc, gw3, 0, 0, "wgrad_conv_out")
    gw3 = _wgrad(o2t, dya, gw3, 1, 0, "wgrad_attn_out")
    gw3 = _wgrad(mt, dy, gw3, 2, 0, "wgrad_out")
    if hooks is not None:
        lse = lse + hooks[0](gw3)[0, 0]
    doe, dot_ = _do_prep(d_o, o, 256)
    dqr, dkt, dvt, dkt2, dvt2 = _attn_bwd(q2, qst, kv4, doe, dot_, lse, nb, min(1024, s_len))
    qg_post = qg if hooks is None else qg + hooks[1](dqr)[0:1, 0:1]
    dq, dqg = _q_post(dqr, pq, qg_post, cos, sin, nb, 256)
    dd, ddm, dkg = _kv_bwd(dkt.reshape(nb, GROUP_W, s_len), dvt.reshape(nb, GROUP_W, s_len),
                           dkt2.reshape(nb, GROUP_W, KEY_PAD), dvt2.reshape(nb, GROUP_W, KEY_PAD),
                           pkv, pm_kv, kg, cos, sin, nb)
    da, dam, gcw = _conv_bwd(dc0, pconv, pm_conv, conv_w_full, nb, 256)
    gmeta, dng_m = _meta_bwd(dam, ddm, w_full, meta_full, norm_g)

    gwin = (lax.empty((1, D, IN_DIM), F32), lax.empty((1, D, IN_DIM), BF16))
    gwin = _wgrad(xnt, da, gwin, 0, G_CONV[0], "wgrad_in_conv", meta=(xnmt, dam))
    gwin = _wgrad(xnt, dcz, gwin, 0, G_CZ[0], "wgrad_in_cz")
    gwin = _wgrad(xnt, dq, gwin, 0, G_Q[0], "wgrad_in_q")
    gwin = _wgrad(xnt, dd, gwin, 0, G_KV[0], "wgrad_in_kv", meta=(xnmt, ddm))
    gwin = _wgrad(xnt, de, gwin, 0, G_E[0], "wgrad_in_e")

    pending = None
    if hooks is not None:
        token, pending = hooks[2](gwin, gcw, gmeta)
        dng_m = dng_m + token[0:1, 0:1]
    gx, dng = _dxn((da, dcz, dq, dd, de), w_full, x2, dy, norm_g, dng_m, 512)

    zeros = jnp.zeros((1, D - 2 * GROUP_W), F32)
    smalls = jnp.concatenate([dng, sums[2:3], sums[0:1], sums[1:2], jnp.concatenate([dqg, dkg, zeros], axis=1),
                              sums[3:4], jnp.zeros((2, D), F32)], axis=0)
    return gx.reshape(nb, s_len, D), gwin, gw3, gcw, gmeta, smalls, pending


def _xyc():
    return lax.axis_index("x"), lax.axis_index("y"), lax.axis_index("c")


_HBM = pl.BlockSpec(memory_space=pltpu.HBM)
_SEM = pl.BlockSpec(memory_space=pltpu.SEMAPHORE)
_EFFECT = pltpu.SideEffectType.DATAFLOW_SIDE_EFFECTING


def _reduce_sibling(gwin, gcm):
    def body(gwin_ref, gcm_ref, r_win, r_cm, send, recv):
        x, y, c = _xyc()
        o = 1 - c
        half = D // 2
        outs = ((gwin_ref.at[pl.ds(o * half, half), :], r_win), (gcm_ref.at[o], r_cm))
        cps = []
        for a, (src, dst) in enumerate(outs):
            cp = pltpu.make_async_remote_copy(src_ref=src, dst_ref=dst, send_sem=send.at[a], recv_sem=recv.at[a],
                                              device_id=(x, y, o), device_id_type=MESH)
            cp.start()
            cps.append(cp)
        for cp in cps:
            cp.wait()

    any_spec = pl.BlockSpec(memory_space=pl.ANY)
    return pl.pallas_call(
        body, name="reduce_sibling",
        out_shape=(_sds((D // 2, IN_DIM), BF16), _sds((24, D), F32)),
        in_specs=[any_spec] * 2, out_specs=(any_spec,) * 2,
        scratch_shapes=[pltpu.SemaphoreType.DMA((2,)), pltpu.SemaphoreType.DMA((2,))],
    )(*_pin(gwin, gcm))


def _w3_sibling_copy(gw3_ref, land_ref, send, recv):
    x, y, c = _xyc()
    return pltpu.make_async_remote_copy(src_ref=gw3_ref.at[:, :, 1 - c], dst_ref=land_ref, send_sem=send.at[0],
                                        recv_sem=recv.at[0], device_id=(x, y, 1 - c), device_id_type=MESH)


def _w3_sibling_start(gw3vb):
    land = lax.empty((3, N_CHIPS, 128, D), BF16)

    def body(src_ref, land_ref, send, recv, src_thru, land_thru, token):
        _w3_sibling_copy(src_ref, land_ref, send, recv).start()
        token[...] = jnp.zeros_like(token)

    outs = pl.pallas_call(
        body, name="w3_sibling_start",
        out_shape=(pltpu.SemaphoreType.DMA((1,)), pltpu.SemaphoreType.DMA((1,)),
                   pltpu.HBM(gw3vb.shape, BF16), pltpu.HBM(land.shape, BF16), jax.ShapeDtypeStruct((8, 128), F32)),
        in_specs=[_HBM, _HBM],
        out_specs=(_SEM, _SEM, _HBM, _HBM, pl.BlockSpec(memory_space=pltpu.VMEM)),
        input_output_aliases={0: 2, 1: 3},
        compiler_params=pltpu.CompilerParams(has_side_effects=_EFFECT),
    )(*_pin(gw3vb, land))
    return outs[0:4], outs[4]


def _w3_sibling_wait(send, recv, src, land, after):
    def body(src_ref, land_ref, send_ref, recv_ref, after_ref, src_out, land_out):
        cp = _w3_sibling_copy(src_ref, land_ref, send_ref, recv_ref)
        cp.wait_send()
        cp.wait_recv()

    outs = pl.pallas_call(
        body, name="w3_sibling_wait",
        out_shape=(pltpu.HBM(src.shape, BF16), pltpu.HBM(land.shape, BF16)),
        in_specs=[_HBM, _HBM, _SEM, _SEM, pl.BlockSpec(memory_space=pl.ANY)],
        out_specs=(_HBM, _HBM),
        input_output_aliases={0: 0, 1: 1},
        compiler_params=pltpu.CompilerParams(has_side_effects=_EFFECT),
    )(src, land, send, recv, after)
    return outs[1]


def _add_sibling_w3(gw3v, r_w3):
    c = lax.axis_index("c").astype(jnp.int32).reshape(1)

    def body(c_ref, a_ref, b_ref, o_ref):
        o_ref[0, 0] = (a_ref[0, 0, 0] + b_ref[0, 0].astype(F32)).astype(BF16)

    return pl.pallas_call(
        body, name="add_sibling_w3", out_shape=_sds((3, 4, 128, D), BF16),
        grid_spec=pltpu.PrefetchScalarGridSpec(
            num_scalar_prefetch=1, grid=(3, 4),
            in_specs=[pl.BlockSpec((1, 1, 1, 128, D), lambda w, s, c_ref: (w, s, c_ref[0], 0, 0)),
                      pl.BlockSpec((1, 1, 128, D), lambda w, s, c_ref: (w, s, 0, 0))],
            out_specs=pl.BlockSpec((1, 1, 128, D), lambda w, s, c_ref: (w, s, 0, 0))),
        compiler_params=_params(("parallel", "parallel"), 32),
    )(c, *_pin(gw3v, r_w3))


def _add_sibling(gwin, gcm, r_win, r_cm):
    c = lax.axis_index("c").astype(jnp.int32).reshape(1)
    half = D // 2
    tr = 64

    def body1(c_ref, a_ref, b_ref, o_ref):
        o_ref[...] = (a_ref[...] + b_ref[...].astype(F32)).astype(BF16)

    cs_win = pl.pallas_call(
        body1, name="add_sibling_w_in", out_shape=_sds((half, IN_DIM), BF16),
        grid_spec=pltpu.PrefetchScalarGridSpec(
            num_scalar_prefetch=1, grid=(half // tr,),
            in_specs=[pl.BlockSpec((tr, IN_DIM), lambda i, c_ref: (c_ref[0] * (half // tr) + i, 0)),
                      pl.BlockSpec((tr, IN_DIM), lambda i, c_ref: (i, 0))],
            out_specs=pl.BlockSpec((tr, IN_DIM), lambda i, c_ref: (i, 0))),
        compiler_params=_params(("parallel",), 32),
    )(c, *_pin(gwin, r_win))

    def body3(c_ref, a_ref, b_ref, o_ref):
        o_ref[...] = a_ref[0] + b_ref[...]

    cs_cm = pl.pallas_call(
        body3, name="add_sibling_cm", out_shape=_sds((24, D), F32),
        grid_spec=pltpu.PrefetchScalarGridSpec(
            num_scalar_prefetch=1, grid=(1,),
            in_specs=[pl.BlockSpec((1, 24, D), lambda i, c_ref: (c_ref[0], 0, 0)),
                      pl.BlockSpec((24, D), lambda i, c_ref: (0, 0))],
            out_specs=pl.BlockSpec((24, D), lambda i, c_ref: (0, 0))),
        compiler_params=_params(("arbitrary",), 32),
    )(c, *_pin(gcm, r_cm))
    return cs_win, cs_cm


_CHIP_PARTS = {
    "win": (lambda ref, p: ref.at[:, pl.ds(p * W_IN_SHARD, W_IN_SHARD)], (D // 2, W_IN_SHARD), BF16),
    "w3": (lambda ref, p: ref.at[:, p], (3, 128, D), BF16),
    "cm": (lambda ref, p: ref.at[:, pl.ds(p * ROW_SHARD, ROW_SHARD)], (24, ROW_SHARD), F32),
}


def _reduce_chips_copies(kinds, srcs, lands, send, recv):
    x, y, c = _xyc()
    peers = ((1 - x, y), (x, 1 - y), (1 - x, 1 - y))
    cps = []
    for k, (px, py) in enumerate(peers):
        for a, (kind, src, land) in enumerate(zip(kinds, srcs, lands)):
            cps.append(pltpu.make_async_remote_copy(
                src_ref=_CHIP_PARTS[kind][0](src, 2 * px + py), dst_ref=land.at[k], send_sem=send.at[3 * a + k],
                recv_sem=recv.at[3 * a + k], device_id=(px, py, c), device_id_type=MESH))
    return cps


def _reduce_chips_start(kinds, srcs, name):
    n = len(kinds)
    lands = tuple(lax.empty((3,) + _CHIP_PARTS[kind][1], _CHIP_PARTS[kind][2]) for kind in kinds)

    def body(*refs):
        srcs_in, lands_in, send, recv, token = refs[0:n], refs[n:2 * n], refs[2 * n], refs[2 * n + 1], refs[-1]
        for cp in _reduce_chips_copies(kinds, srcs_in, lands_in, send, recv):
            cp.start()
        token[...] = jnp.zeros_like(token)

    hbm = lambda a: pltpu.HBM(a.shape, a.dtype)
    outs = pl.pallas_call(
        body, name=name,
        out_shape=(pltpu.SemaphoreType.DMA((3 * n,)), pltpu.SemaphoreType.DMA((3 * n,)),
                   *[hbm(a) for a in srcs], *[hbm(a) for a in lands], jax.ShapeDtypeStruct((8, 128), F32)),
        in_specs=[_HBM] * (2 * n),
        out_specs=(_SEM, _SEM, *[_HBM] * (2 * n), pl.BlockSpec(memory_space=pltpu.VMEM)),
        input_output_aliases={i: i + 2 for i in range(2 * n)},
        compiler_params=pltpu.CompilerParams(has_side_effects=_EFFECT),
    )(*_pin(*srcs, *lands))
    return (outs[0], outs[1], outs[2:2 + n], outs[2 + n:2 + 2 * n]), outs[-1]


def _reduce_chips_wait(kinds, pending, after, name):
    send, recv, srcs, lands = pending
    n = len(kinds)

    def body(*refs):
        srcs_in, lands_in, send_ref, recv_ref = refs[0:n], refs[n:2 * n], refs[2 * n], refs[2 * n + 1]
        for cp in _reduce_chips_copies(kinds, srcs_in, lands_in, send_ref, recv_ref):
            cp.wait_send()
            cp.wait_recv()

    hbm = lambda a: pltpu.HBM(a.shape, a.dtype)
    outs = pl.pallas_call(
        body, name=name,
        out_shape=(*[hbm(a) for a in srcs], *[hbm(a) for a in lands]),
        in_specs=[_HBM] * (2 * n) + [_SEM, _SEM, pl.BlockSpec(memory_space=pl.ANY)],
        out_specs=(_HBM,) * (2 * n),
        input_output_aliases={i: i for i in range(2 * n)},
        compiler_params=pltpu.CompilerParams(has_side_effects=_EFFECT),
    )(*srcs, *lands, send, recv, after)
    return outs[0:n], outs[n:2 * n]


def _add_chips(cs_win, cs_w3, cs_cm, r_win, r_w3, r_cm):
    x, y, c = _xyc()
    idx = jnp.stack([2 * x + y, c]).astype(jnp.int32)
    half = D // 2
    tr = 128

    def body1(i_ref, a_ref, b_ref, o_ref):
        f = lambda v: v.astype(F32)
        o_ref[0] = (f(a_ref[...]) + f(b_ref[2])) + (f(b_ref[0]) + f(b_ref[1]))

    f_win = pl.pallas_call(
        body1, name="add_chips_w_in", out_shape=_sds((2, half, W_IN_SHARD), F32),
        grid_spec=pltpu.PrefetchScalarGridSpec(
            num_scalar_prefetch=1, grid=(half // tr,),
            in_specs=[pl.BlockSpec((tr, W_IN_SHARD), lambda i, r: (i, r[0])),
                      pl.BlockSpec((3, tr, W_IN_SHARD), lambda i, r: (0, i, 0))],
            out_specs=pl.BlockSpec((1, tr, W_IN_SHARD), lambda i, r: (r[1], i, 0))),
        compiler_params=_params(("parallel",), 32),
    )(idx, *_pin(cs_win, r_win))

    def body2(i_ref, a_ref, b_ref, o_ref):
        f = lambda v: v.astype(F32)
        o_ref[0, 0] = (f(a_ref[0, 0]) + f(b_ref[2, 0])) + (f(b_ref[0, 0]) + f(b_ref[1, 0]))

    f_w3 = pl.pallas_call(
        body2, name="add_chips_w3", out_shape=_sds((3, 2, 128, D), F32),
        grid_spec=pltpu.PrefetchScalarGridSpec(
            num_scalar_prefetch=1, grid=(3,),
            in_specs=[pl.BlockSpec((1, 1, 128, D), lambda w, r: (w, r[0], 0, 0)),
                      pl.BlockSpec((3, 1, 128, D), lambda w, r: (0, w, 0, 0))],
            out_specs=pl.BlockSpec((1, 1, 128, D), lambda w, r: (w, r[1], 0, 0))),
        compiler_params=_params(("parallel",), 32),
    )(idx, *_pin(cs_w3, r_w3))

    def body3(i_ref, a_ref, b_ref, o_ref):
        o_ref[0] = (a_ref[...] + b_ref[2]) + (b_ref[0] + b_ref[1])

    f_cm = pl.pallas_call(
        body3, name="add_chips_cm", out_shape=_sds((2, 24, ROW_SHARD), F32),
        grid_spec=pltpu.PrefetchScalarGridSpec(
            num_scalar_prefetch=1, grid=(1,),
            in_specs=[pl.BlockSpec((24, ROW_SHARD), lambda i, r: (0, r[0])),
                      pl.BlockSpec((3, 24, ROW_SHARD), lambda i, r: (0, 0, 0))],
            out_specs=pl.BlockSpec((1, 24, ROW_SHARD), lambda i, r: (r[1], 0, 0))),
        compiler_params=_params(("arbitrary",), 32),
    )(idx, *_pin(cs_cm, r_cm))
    return f_win, f_w3, f_cm


def _share_sibling(f_win, f_w3, f_cm, smalls):
    def body(win_in, w3_in, cm_in, sm_ref, win_ref, w3_ref, cm_ref, r_sm, send, recv, ssend, srecv, lsem):
        x, y, c = _xyc()
        o = 1 - c
        cps = []
        for a, (ref, sl) in enumerate(((win_ref, lambda h: win_ref.at[h]), (w3_ref, lambda h: w3_ref.at[:, h]),
                                       (cm_ref, lambda h: cm_ref.at[h]))):
            cp = pltpu.make_async_remote_copy(src_ref=sl(c), dst_ref=sl(c), send_sem=send.at[a], recv_sem=recv.at[a],
                                              device_id=(x, y, o), device_id_type=MESH)
            cp.start()
            cps.append((cp, sl))
        me = 4 * x + 2 * y + c
        loc = pltpu.make_async_copy(sm_ref, r_sm.at[me], lsem)
        loc.start()
        scps = []
        for d in range(1, 8):
            px, py, pc = (x + (d >> 2)) % 2, (y + ((d >> 1) & 1)) % 2, (c + (d & 1)) % 2
            cp = pltpu.make_async_remote_copy(src_ref=sm_ref, dst_ref=r_sm.at[me], send_sem=ssend.at[d - 1],
                                              recv_sem=srecv.at[d - 1], device_id=(px, py, pc), device_id_type=MESH)
            cp.start()
            scps.append((cp, 4 * px + 2 * py + pc))
        for a, (cp, sl) in enumerate(cps):
            pltpu.make_async_remote_copy(src_ref=sl(o), dst_ref=sl(o), send_sem=send.at[a], recv_sem=recv.at[a],
                                         device_id=(x, y, o), device_id_type=MESH).wait_recv()
            cp.wait_send()
        for d, (cp, pid) in enumerate(scps):
            pltpu.make_async_remote_copy(src_ref=sm_ref, dst_ref=r_sm.at[pid], send_sem=ssend.at[d],
                                         recv_sem=srecv.at[d], device_id=(x, y, c), device_id_type=MESH).wait_recv()
            cp.wait_send()
        loc.wait()

    any_spec = pl.BlockSpec(memory_space=pl.ANY)
    return pl.pallas_call(
        body, name="share_sibling",
        out_shape=(_sds(f_win.shape, F32), _sds(f_w3.shape, F32), _sds(f_cm.shape, F32), _sds((8, 8, D), F32)),
        in_specs=[any_spec] * 4, out_specs=(any_spec,) * 4,
        input_output_aliases={0: 0, 1: 1, 2: 2},
        scratch_shapes=[pltpu.SemaphoreType.DMA((3,)), pltpu.SemaphoreType.DMA((3,)),
                        pltpu.SemaphoreType.DMA((7,)), pltpu.SemaphoreType.DMA((7,)), pltpu.SemaphoreType.DMA],
    )(*_pin(f_win, f_w3, f_cm, smalls))


def _adamw_math(w, g, m, v):
    m = ADAM_B1 * m + (1.0 - ADAM_B1) * g
    v = ADAM_B2 * v + (1.0 - ADAM_B2) * (g * g)
    m_hat = m / (1.0 - ADAM_B1 ** ADAM_STEP)
    v_hat = v / (1.0 - ADAM_B2 ** ADAM_STEP)
    delta = -ADAM_LR * (m_hat / (jnp.sqrt(v_hat) + ADAM_EPS) + ADAM_WD * w)
    return delta, m, v


def _adamw(w, g, m, v, tr, name):
    rows, cols = w.shape

    def body(w_ref, g_ref, m_ref, v_ref, g_o, d_o, m_o, v_o):
        g = g_ref[...]
        g_o[...] = g
        d_o[...], m_o[...], v_o[...] = _adamw_math(w_ref[...], g, m_ref[...], v_ref[...])

    spec = pl.BlockSpec((tr, cols), lambda i: (i, 0))
    return pl.pallas_call(
        body, name=name, grid=(rows // tr,),
        out_shape=(_sds((rows, cols), F32),) * 4,
        in_specs=[spec] * 4, out_specs=(spec,) * 4,
        compiler_params=_params(("parallel",), 32),
    )(*_pin(w, g, m, v))


def _adamw3(g3, ws, ms, vs):
    def body(g_ref, *refs):
        w_refs, m_refs, v_refs, outs = refs[0:3], refs[3:6], refs[6:9], refs[9:]
        g_os, d_os, m_os, v_os = outs[0:3], outs[3:6], outs[6:9], outs[9:12]
        for i in range(3):
            g = g_ref[i]
            g_os[i][0] = g
            d_os[i][0], m_os[i][0], v_os[i][0] = _adamw_math(w_refs[i][0], g, m_refs[i][0], v_refs[i][0])

    return pl.pallas_call(
        body, name="adamw_w3", out_shape=(jax.ShapeDtypeStruct((1, ROW_SHARD, D), F32),) * 12,
        compiler_params=pltpu.CompilerParams(vmem_limit_bytes=48 << 20),
    )(g3, *ws, *ms, *vs)


def _adamw_cm(f_cm, ws, ms, vs):
    def body(f_ref, *refs):
        w_refs, m_refs, v_refs, outs = refs[0:2], refs[2:4], refs[4:6], refs[6:14]
        gcw, gmt = refs[14], refs[15]
        gcw[0:16] = f_ref[0, 0:16]
        gcw[16:32] = f_ref[1, 0:16]
        gmt[0:8] = f_ref[0, 16:24]
        gmt[8:16] = f_ref[1, 16:24]
        g_conv = gcw[0:CONV_K, :]
        g_meta = gmt[...]
        outs[0][0] = g_conv
        outs[1][...] = g_meta
        outs[2][0], outs[4][0], outs[6][0] = _adamw_math(w_refs[0][0], g_conv, m_refs[0][0], v_refs[0][0])
        outs[3][...], outs[5][...], outs[7][...] = _adamw_math(w_refs[1][...], g_meta, m_refs[1][...], v_refs[1][...])

    pair = (jax.ShapeDtypeStruct((1, CONV_K, ROW_SHARD), F32), jax.ShapeDtypeStruct((N_META, ROW_SHARD), F32))
    return pl.pallas_call(
        body, name="adamw_cm", out_shape=pair * 4,
        scratch_shapes=[pltpu.VMEM((32, ROW_SHARD), F32), pltpu.VMEM((N_META, ROW_SHARD), F32)],
    )(f_cm, *ws, *ms, *vs)


def _adamw_small(r_sm, ws, ms, vs):
    def body(s_ref, *refs):
        w_refs, m_refs, v_refs, outs = refs[0:6], refs[6:12], refs[12:18], refs[18:]
        loss_o, g_os, d_os, m_os, v_os = outs[0], outs[1:7], outs[7:13], outs[13:19], outs[19:25]
        g = s_ref[0]
        for dev in range(1, 8):
            g = g + s_ref[dev]
        qk = g[4:5, :]
        qg = qk[:, 0:HEAD_DIM]
        kg = qk[:, GROUP_W:GROUP_W + HEAD_DIM]
        for h in range(1, GQA):
            qg = qg + qk[:, HEAD_DIM * h:HEAD_DIM * (h + 1)]
            kg = kg + qk[:, GROUP_W + HEAD_DIM * h:GROUP_W + HEAD_DIM * (h + 1)]
        loss_o[...] = (0.5 / D) * jnp.sum(g[5:6, :], axis=-1, keepdims=True)
        for i, gi in enumerate((g[0:1], g[1:2], g[2:3], g[3:4], qg, kg)):
            g_os[i][...] = gi
            d_os[i][...], m_os[i][...], v_os[i][...] = _adamw_math(w_refs[i][...], gi, m_refs[i][...], v_refs[i][...])

    six = tuple(jax.ShapeDtypeStruct(w.shape, F32) for w in ws)
    return pl.pallas_call(
        body, name="adamw_small", out_shape=(jax.ShapeDtypeStruct((1, 1), F32),) + six * 4,
    )(r_sm, *ws, *ms, *vs)


def kernel(x, meta_tokens, norm_g, w_in, conv_w, conv_b, conv_norm_g, conv_norm_b, w_conv_out, q_norm_g, k_norm_g, w_attn_out, w_out, loss_target, m_meta_tokens, m_norm_g, m_w_in, m_conv_w, m_conv_b, m_conv_norm_g, m_conv_norm_b, m_w_conv_out, m_q_norm_g, m_k_norm_g, m_w_attn_out, m_w_out, v_meta_tokens, v_norm_g, v_w_in, v_conv_w, v_conv_b, v_conv_norm_g, v_conv_norm_b, v_w_conv_out, v_q_norm_g, v_k_norm_g, v_w_attn_out, v_w_out):
    pad_k = lambda a: jnp.pad(a[0], ((0, 32 - CONV_K), (0, 0)))
    w3_s = (w_conv_out, w_attn_out, w_out)
    w_full, conv_w_full, meta_full, w3b, w3_land = _gather_weights(w_in[0], w3_s, pad_k(conv_w), meta_tokens)
    w3_pending, token = _w3_start(w3b, w3_land)
    norm_g_fwd = norm_g + token[0:1, 0:1]

    def w3_full(after):
        return _w3_wait(*w3_pending, after)

    early = {}

    def w3_sent(gw3):
        gw3v, gw3vb = (a.reshape(3, N_CHIPS, 2, 128, D) for a in gw3)
        early["sibling"], token = _w3_sibling_start(gw3vb)
        early["mine"] = gw3v
        return token

    def w3_reduced(after):
        r_w3 = _w3_sibling_wait(*early["sibling"], after)
        cs_w3 = _add_sibling_w3(early["mine"], r_w3)
        early["chips"], token = _reduce_chips_start(("w3",), (cs_w3,), "reduce_w3_start")
        return token

    def reduce_start(gwin, gcw, gmeta):
        gwin2, gwin2b = (a.reshape(D, IN_DIM) for a in gwin)
        gcm = jnp.concatenate([gcw.reshape(2, 16, D), gmeta.reshape(2, 8, D)], axis=1)
        r_win, r_cm = _reduce_sibling(gwin2b, gcm)
        cs = _add_sibling(gwin2, gcm, r_win, r_cm)
        pending, token = _reduce_chips_start(("win", "cm"), cs, "reduce_chips_start")
        return token, pending

    gx, _, _, _, _, smalls, pending = _local_step(
        x, loss_target, norm_g_fwd, conv_b, conv_norm_g, conv_norm_b, q_norm_g, k_norm_g,
        w_full, w3_full, conv_w_full, meta_full, (w3_sent, w3_reduced, reduce_start))
    (cs_win, cs_cm), (r2_win, r2_cm) = _reduce_chips_wait(("win", "cm"), pending, gx, "reduce_chips_wait")
    (cs_w3,), (r2_w3,) = _reduce_chips_wait(("w3",), early["chips"], gx, "reduce_w3_wait")
    f_win, f_w3, f_cm = _add_chips(cs_win, cs_w3, cs_cm, r2_win, r2_w3, r2_cm)
    f_win, f_w3, f_cm, r_sm = _share_sibling(f_win, f_w3, f_cm, smalls)

    g_w_in, d_w_in, nm_w_in, nv_w_in = _adamw(w_in[0], f_win.reshape(D, W_IN_SHARD), m_w_in[0], v_w_in[0], 128,
                                              "adamw_w_in")
    w3 = _adamw3(f_w3.reshape(3, ROW_SHARD, D), w3_s, (m_w_conv_out, m_w_attn_out, m_w_out),
                 (v_w_conv_out, v_w_attn_out, v_w_out))
    cm = _adamw_cm(f_cm, (conv_w, meta_tokens), (m_conv_w, m_meta_tokens), (v_conv_w, v_meta_tokens))
    small = _adamw_small(
        r_sm, (norm_g, conv_b, conv_norm_g, conv_norm_b, q_norm_g, k_norm_g),
        (m_norm_g, m_conv_b, m_conv_norm_g, m_conv_norm_b, m_q_norm_g, m_k_norm_g),
        (v_norm_g, v_conv_b, v_conv_norm_g, v_conv_norm_b, v_q_norm_g, v_k_norm_g))

    def assemble(big_in, w3x, cmx, s6):
        ng, cb, cng, cnb, qg, kg = s6
        return (cmx[1], ng, big_in[None], cmx[0], cb, cng, cnb, w3x[0], qg, kg, w3x[1], w3x[2])

    loss = small[0].reshape(())
    grads = assemble(g_w_in, w3[0:3], cm[0:2], small[1:7])
    deltas = assemble(d_w_in, w3[3:6], cm[2:4], small[7:13])
    new_m = assemble(nm_w_in, w3[6:9], cm[4:6], small[13:19])
    new_v = assemble(nv_w_in, w3[9:12], cm[6:8], small[19:25])
    return (loss, gx, *grads, *deltas, *new_m, *new_v)
```

```python
import math

import jax
import jax.numpy as jnp
from jax import lax
from jax.experimental import pallas as pl
from jax.experimental.pallas import tpu as pltpu

F32, BF16 = jnp.float32, jnp.bfloat16
MESH = pl.DeviceIdType.MESH

D = 1024
N_META = 16
CONV_K = 31
N_KV = 4
GQA = 4
HEAD_DIM = 64
GROUP_W = GQA * HEAD_DIM
GRID_W = 64
ROPE_FREQS = 16
ROPE_THETA = 10000.0
EPS = 1e-6
IN_DIM = 7680
KEY_PAD = 128
G_CONV, G_CZ, G_Q, G_KV, G_E = (0, 2048), (2048, 1024), (3072, 1024), (4096, 512), (4608, 3072)
N_CHIPS = 4
W_IN_SHARD = IN_DIM // N_CHIPS
ROW_SHARD = D // N_CHIPS

ADAM_LR, ADAM_B1, ADAM_B2, ADAM_EPS, ADAM_WD, ADAM_STEP = 0.001, 0.9, 0.999, 1e-08, 0.01, 10

NT_DIMS = (((1,), (1,)), ((), ()))


def _params(sem=None, vmem_mb=48):
    return pltpu.CompilerParams(dimension_semantics=sem, vmem_limit_bytes=vmem_mb << 20)


def _sds(shape, dtype):
    return pltpu.HBM(tuple(shape), dtype)


def _pin(*arrays):
    return [pltpu.with_memory_space_constraint(a, pltpu.HBM) for a in arrays]


def _sig(v):
    return jax.nn.sigmoid(v)


def _dsilu(v, s):
    return s * (1.0 + v * (1.0 - s))


def _dot(a, b):
    return jnp.dot(a, b, preferred_element_type=F32)


def _dot_nt(a, b):
    return lax.dot_general(a, b, NT_DIMS, preferred_element_type=F32)


def _qk_mats():
    i = lax.broadcasted_iota(jnp.int32, (GROUP_W, GROUP_W), 0)
    j = lax.broadcasted_iota(jnp.int32, (GROUP_W, GROUP_W), 1)
    mean = jnp.where((i >> 6) == (j >> 6), 1.0 / HEAD_DIM, 0.0).astype(BF16)
    turn = jnp.where((i == j + 16) & ((j & 16) == 0), -1.0,
                     jnp.where((i == j - 16) & ((j & 16) != 0), 1.0, 0.0)).astype(BF16)
    return mean, turn


def _apply(v, mat):
    hi = v.astype(BF16)
    lo = (v - hi.astype(F32)).astype(BF16)
    return _dot(hi, mat) + _dot(lo, mat)


def _qk_fwd(v, g, cos, sin, mats):
    mean, turn = mats
    r = lax.rsqrt(_apply(v * v, mean) + EPS)
    n = v * r * g
    return n * cos + _apply(n, turn) * sin, r


def _qk_bwd(dy, v, r, g, cos, sin, mats):
    mean, turn = mats
    dn = dy * cos - _apply(dy, turn) * sin
    dyg = dn * g
    dv = r * dyg - v * (r * r * r) * _apply(dyg * v, mean)
    return dv, dn * v * r


def _rms_bwd(dxn, v, r, g):
    dxg = dxn * g
    return r * dxg - v * (r * r * r) * jnp.mean(dxg * v, axis=-1, keepdims=True)


def _glu(a):
    return a[:, :D] * _sig(a[:, D:])


def _gather_weights(w_in_s, w3_s, conv_w_s, meta_s):
    def body(win_ref, wa_ref, wb_ref, wc_ref, cw_ref, mt_ref, win_o, cw_o, mt_o, w3b_o, w3_o, win_b, w3_b,
             send, recv, fsend, frecv, lsem, csem):
        x, y, c = _xyc()
        o = 1 - c
        me = 2 * x + y
        win_b[...] = win_ref[...].astype(BF16)
        for i, ref in enumerate((wa_ref, wb_ref, wc_ref)):
            w3_b[i] = ref[0].astype(BF16)
        cast = pltpu.make_async_copy(w3_b, w3b_o, csem.at[0])
        cast.start()
        own = pltpu.make_async_copy(w3_b, _w3_place(w3_o, me), csem.at[1])
        own.start()
        items = (
            (lambda h: win_b.at[pl.ds(h * 512, 512), :],
             lambda p, h: win_o.at[pl.ds(h * 512, 512), pl.ds(p * W_IN_SHARD, W_IN_SHARD)]),
            (lambda h: cw_ref.at[pl.ds(h * 16, 16), :],
             lambda p, h: cw_o.at[pl.ds(h * 16, 16), pl.ds(p * ROW_SHARD, ROW_SHARD)]),
            (lambda h: mt_ref.at[pl.ds(h * 8, 8), :],
             lambda p, h: mt_o.at[pl.ds(h * 8, 8), pl.ds(p * ROW_SHARD, ROW_SHARD)]),
        )
        peers = ((1 - x, y), (x, 1 - y), (1 - x, 1 - y))

        def remote(src, dst, s_sem, r_sem, to):
            return pltpu.make_async_remote_copy(src_ref=src, dst_ref=dst, send_sem=s_sem, recv_sem=r_sem,
                                                device_id=to, device_id_type=MESH)

        started = []
        for a, (half, place) in enumerate(items):
            for h in range(2):
                loc = pltpu.make_async_copy(half(h), place(me, h), lsem.at[a, h])
                loc.start()
                started.append(loc.wait)
            for k, (px, py) in enumerate(peers):
                cp = remote(half(c), place(me, c), send.at[a, k], recv.at[a, k], (px, py, c))
                cp.start()
                started.append(cp.wait_send)
        for k, (px, py) in enumerate(peers):
            for a, (half, place) in enumerate(items):
                got = place(2 * px + py, c)
                remote(got, got, send.at[a, k], recv.at[a, k], (px, py, c)).wait_recv()
                fw = remote(got, got, fsend.at[a, k], frecv.at[a, k], (x, y, o))
                fw.start()
                started.append(fw.wait_send)
        for k, (px, py) in enumerate(peers):
            for a, (half, place) in enumerate(items):
                theirs = place(2 * px + py, o)
                remote(theirs, theirs, fsend.at[a, k], frecv.at[a, k], (x, y, o)).wait_recv()
        for wait in started:
            wait()
        cast.wait()
        own.wait()

    any_spec = pl.BlockSpec(memory_space=pl.ANY)
    vmem = pl.BlockSpec(memory_space=pltpu.VMEM)
    return pl.pallas_call(
        body, name="gather_weights",
        out_shape=(_sds((D, IN_DIM), BF16), _sds((32, D), F32), _sds((N_META, D), F32),
                   _sds((3, ROW_SHARD, D), BF16), _sds((3, D, D), BF16)),
        in_specs=[vmem] * 6,
        out_specs=(any_spec,) * 5,
        scratch_shapes=[pltpu.VMEM((D, W_IN_SHARD), BF16), pltpu.VMEM((3, ROW_SHARD, D), BF16),
                        pltpu.SemaphoreType.DMA((3, 3)), pltpu.SemaphoreType.DMA((3, 3)),
                        pltpu.SemaphoreType.DMA((3, 3)), pltpu.SemaphoreType.DMA((3, 3)),
                        pltpu.SemaphoreType.DMA((3, 2)), pltpu.SemaphoreType.DMA((2,))],
        compiler_params=pltpu.CompilerParams(vmem_limit_bytes=40 << 20),
    )(w_in_s, *w3_s, conv_w_s, meta_s)


def _w3_place(ref, p):
    return ref.at[:, pl.ds(p * ROW_SHARD, ROW_SHARD), :]


def _w3_copies(w3b_ref, land_ref, send, recv):
    x, y, c = _xyc()
    me = 2 * x + y
    peers = ((1 - x, y), (x, 1 - y), (1 - x, 1 - y))
    return [pltpu.make_async_remote_copy(src_ref=w3b_ref, dst_ref=_w3_place(land_ref, me),
                                         send_sem=send.at[k], recv_sem=recv.at[k], device_id=(px, py, c),
                                         device_id_type=MESH)
            for k, (px, py) in enumerate(peers)]


def _w3_start(w3b, land):
    def body(w3b_ref, land_ref, send, recv, w3b_thru, land_thru, token):
        for cp in _w3_copies(w3b_ref, land_ref, send, recv):
            cp.start()
        token[...] = jnp.zeros_like(token)

    outs = pl.pallas_call(
        body, name="w3_start",
        out_shape=(pltpu.SemaphoreType.DMA((3,)), pltpu.SemaphoreType.DMA((3,)),
                   pltpu.HBM(w3b.shape, BF16), pltpu.HBM(land.shape, BF16), jax.ShapeDtypeStruct((8, 128), F32)),
        in_specs=[_HBM, _HBM],
        out_specs=(_SEM, _SEM, _HBM, _HBM, pl.BlockSpec(memory_space=pltpu.VMEM)),
        input_output_aliases={0: 2, 1: 3},
        compiler_params=pltpu.CompilerParams(has_side_effects=_EFFECT),
    )(*_pin(w3b, land))
    return outs[0:4], outs[4]


def _w3_wait(send, recv, w3b, land, after):
    def body(w3b_ref, land_ref, send_ref, recv_ref, after_ref, w3b_out, land_out):
        x, y, c = _xyc()
        peers = ((1 - x, y), (x, 1 - y), (1 - x, 1 - y))
        for k, (cp, (px, py)) in enumerate(zip(_w3_copies(w3b_ref, land_ref, send_ref, recv_ref), peers)):
            cp.wait_send()
            got = _w3_place(land_ref, 2 * px + py)
            pltpu.make_async_remote_copy(src_ref=got, dst_ref=got, send_sem=send_ref.at[k], recv_sem=recv_ref.at[k],
                                         device_id=(px, py, c), device_id_type=MESH).wait_recv()

    outs = pl.pallas_call(
        body, name="w3_wait",
        out_shape=(pltpu.HBM(w3b.shape, BF16), pltpu.HBM(land.shape, BF16)),
        in_specs=[_HBM, _HBM, _SEM, _SEM, pl.BlockSpec(memory_space=pl.ANY)],
        out_specs=(_HBM, _HBM),
        input_output_aliases={0: 0, 1: 1},
        compiler_params=pltpu.CompilerParams(has_side_effects=_EFFECT),
    )(w3b, land, send, recv, after)
    return outs[1]


def _meta_fwd(meta_full, norm_g, w_full):
    def body(m_ref, g_ref, wc_ref, wkv_ref, xnt_ref, pc_ref, pkv_ref):
        v = m_ref[...]
        r = lax.rsqrt(jnp.mean(v * v, axis=-1, keepdims=True) + EPS)
        xn = v * r * g_ref[...]
        xnb = xn.astype(BF16)
        pad = jnp.concatenate([xn, jnp.zeros((128 - N_META, D), F32)], axis=0)
        xnt_ref[...] = pad.T.astype(BF16)
        pc_ref[...] = _dot(xnb, wc_ref[...])
        pkv_ref[...] = _dot(xnb, wkv_ref[...])

    return pl.pallas_call(
        body, name="meta_fwd", grid=(1,),
        out_shape=(_sds((D, 128), BF16), _sds((N_META, 2048), F32),
                   _sds((N_META, 512), F32)),
        in_specs=[pl.BlockSpec((N_META, D), lambda i: (0, 0)), pl.BlockSpec((1, D), lambda i: (0, 0)),
                  pl.BlockSpec((D, 2048), lambda i: (0, 0)), pl.BlockSpec((D, 512), lambda i: (0, G_KV[0] // 512))],
        out_specs=(pl.BlockSpec((D, 128), lambda i: (0, 0)), pl.BlockSpec((N_META, 2048), lambda i: (0, 0)),
                   pl.BlockSpec((N_META, 512), lambda i: (0, 0))),
        compiler_params=_params(("arbitrary",), 32),
    )(*_pin(meta_full, norm_g, w_full, w_full))


def _in_proj(x2, norm_g, w_full, qg, cos, sin, nb, tm):
    rows = x2.shape[0]
    nt = rows // nb // tm
    groups = (G_CONV, G_CZ, G_Q, G_KV, G_E)
    scale = 1.0 / math.sqrt(HEAD_DIM)

    def body(x_ref, g_ref, w_hbm, qg_ref, cos_ref, sin_ref, *rest):
        outs, xnt_ref, q2_o, qt_o, w_vmem, sem = rest[:5], rest[5], rest[6], rest[7], rest[8], rest[9]

        @pl.when(pl.program_id(0) == 0)
        def _():
            cp = pltpu.make_async_copy(w_hbm, w_vmem, sem)
            cp.start()
            cp.wait()

        v = x_ref[...]
        r = lax.rsqrt(jnp.mean(v * v, axis=-1, keepdims=True) + EPS)
        xn = v * r * g_ref[...]
        xnb = xn.astype(BF16)
        xnt_ref[...] = xn.T.astype(BF16)
        for ref, (off, wd) in zip(outs, groups):
            for c0 in range(0, wd, 512):
                ref[:, c0:c0 + 512] = _dot(xnb, w_vmem[:, off + c0:off + c0 + 512])
        gv, cosv, sinv = qg_ref[...], cos_ref[...], sin_ref[...]
        mats = _qk_mats()
        for g in range(N_KV):
            gs = slice(GROUP_W * g, GROUP_W * (g + 1))
            qr, _ = _qk_fwd(outs[2][:, gs], gv, cosv, sinv, mats)
            q2_o[:, gs] = (qr * (scale * LOG2E)).astype(BF16)
            qt_o[gs, :] = (qr * scale).T.astype(BF16)

    rope = pl.BlockSpec((tm, GROUP_W), lambda i: (lax.rem(i, nt), 0))
    return pl.pallas_call(
        body, name="in_proj", grid=(rows // tm,),
        out_shape=tuple(_sds((rows, wd), F32) for _, wd in groups)
        + (_sds((D, rows), BF16), _sds((rows, D), BF16), _sds((D, rows), BF16)),
        in_specs=[pl.BlockSpec((tm, D), lambda i: (i, 0)), pl.BlockSpec((1, D), lambda i: (0, 0)),
                  pl.BlockSpec(memory_space=pl.ANY), pl.BlockSpec((1, GROUP_W), lambda i: (0, 0)), rope, rope],
        out_specs=tuple(pl.BlockSpec((tm, wd), lambda i: (i, 0)) for _, wd in groups)
        + (pl.BlockSpec((D, tm), lambda i: (0, i)), pl.BlockSpec((tm, D), lambda i: (i, 0)),
           pl.BlockSpec((D, tm), lambda i: (0, i))),
        scratch_shapes=[pltpu.VMEM((D, IN_DIM), BF16), pltpu.SemaphoreType.DMA],
        compiler_params=_params(("arbitrary",), 58),
    )(*_pin(x2, norm_g, w_full, qg, cos, sin))


def _halo_specs(width, tm, nt, rows):
    h16 = tm // 16
    return [pl.BlockSpec((tm, width), lambda b, i: (b * nt + i, 0)),
            pl.BlockSpec((16, width), lambda b, i: (jnp.maximum((b * nt + i) * h16 - 1, 0), 0)),
            pl.BlockSpec((16, width), lambda b, i: (jnp.minimum((b * nt + i + 1) * h16, rows // 16 - 1), 0))]


def _fill_uext(uext, cur, prev, nxt, meta, i, nt, tm):
    uext[0:16] = jnp.where(i == 0, _glu(meta[...]), _glu(prev[...]))
    uext[16:16 + tm] = _glu(cur[...])
    uext[16 + tm:32 + tm] = jnp.where(i == nt - 1, 0.0, _glu(nxt[...]))


def _shifted_copies(dst, src, n):
    for r in range(1, 8):
        dst[r, 0:n] = src[r:r + n]


def _rows32(shifted, src, start, cols):
    q8, r = divmod(start, 8)
    if r == 0:
        return src[start:start + 32, cols]
    return shifted[r, 8 * q8:8 * q8 + 32, cols]


def _conv_fwd(pconv, pm_conv, conv_w, conv_b, nb, tm):
    rows = pconv.shape[0]
    nt = rows // nb // tm

    def body(cur, prev, nxt, meta, w_ref, b_ref, o_ref, uext, ush):
        i = pl.program_id(1)
        _fill_uext(uext, cur, prev, nxt, meta, i, nt, tm)
        _shifted_copies(ush, uext, tm + 24)
        for r0 in range(0, tm, 32):
            for c0 in range(0, D, 256):
                acc = jnp.zeros((32, 256), F32) + b_ref[:, c0:c0 + 256]
                for j in range(CONV_K):
                    acc = acc + _rows32(ush, uext, r0 + j + 1, slice(c0, c0 + 256)) * w_ref[j:j + 1, c0:c0 + 256]
                o_ref[r0:r0 + 32, c0:c0 + 256] = acc

    return pl.pallas_call(
        body, name="conv_fwd", grid=(nb, nt),
        out_shape=_sds((rows, D), F32),
        in_specs=_halo_specs(2048, tm, nt, rows)
        + [pl.BlockSpec((16, 2048), lambda b, i: (0, 0)), pl.BlockSpec((32, D), lambda b, i: (0, 0)),
           pl.BlockSpec((1, D), lambda b, i: (0, 0))],
        out_specs=pl.BlockSpec((tm, D), lambda b, i: (b * nt + i, 0)),
        scratch_shapes=[pltpu.VMEM((tm + 32, D), F32), pltpu.VMEM((8, tm + 24, D), F32)],
        compiler_params=_params(("parallel", "parallel"), 48),
    )(*_pin(pconv, pconv, pconv, pm_conv, conv_w, conv_b))


def _kv_prep(pkv, pm_kv, kg, cos, sin, nb):
    rows = pkv.shape[0]
    s_len = rows // nb
    tk = min(512, s_len)
    nt = s_len // tk

    def body(kv_ref, m_ref, g_ref, cos_ref, sin_ref, k_o, v_o, k2_o, v2_o):
        i = pl.program_id(1)
        mats = _qk_mats()
        kv = kv_ref[...]
        kr, _ = _qk_fwd(kv[:, :GROUP_W], g_ref[...], cos_ref[...], sin_ref[...], mats)
        ones = _ones_cols(tk, tk)
        for h in range(N_KV):
            k_o[0, h] = kr[:, HEAD_DIM * h:HEAD_DIM * (h + 1)].astype(BF16)
            vh = kv[:, GROUP_W + HEAD_DIM * h:GROUP_W + HEAD_DIM * (h + 1)]
            v_o[0, h] = jnp.concatenate([vh, ones], axis=1).astype(BF16)

        @pl.when(i == 0)
        def _():
            kvm = m_ref[...]
            km = kvm[:, :GROUP_W]
            kn = km * lax.rsqrt(_apply(km * km, mats[0]) + EPS) * g_ref[...]
            zeros = jnp.zeros((KEY_PAD - N_META, GROUP_W), F32)
            kfull = jnp.concatenate([kn, zeros], axis=0)
            vfull = jnp.concatenate([kvm[:, GROUP_W:], zeros], axis=0)
            ones_m = _ones_cols(KEY_PAD, N_META)
            for h in range(N_KV):
                k2_o[0, h] = kfull[:, HEAD_DIM * h:HEAD_DIM * (h + 1)].astype(BF16)
                v2_o[0, h] = jnp.concatenate([vfull[:, HEAD_DIM * h:HEAD_DIM * (h + 1)], ones_m], axis=1).astype(BF16)

    return pl.pallas_call(
        body, name="kv_prep", grid=(nb, nt),
        out_shape=(_sds((nb, N_KV, s_len, HEAD_DIM), BF16), _sds((nb, N_KV, s_len, 2 * HEAD_DIM), BF16),
                   _sds((nb, N_KV, KEY_PAD, HEAD_DIM), BF16), _sds((nb, N_KV, KEY_PAD, 2 * HEAD_DIM), BF16)),
        in_specs=[pl.BlockSpec((tk, 512), lambda b, i: (b * nt + i, 0)),
                  pl.BlockSpec((N_META, 512), lambda b, i: (0, 0)), pl.BlockSpec((1, GROUP_W), lambda b, i: (0, 0)),
                  pl.BlockSpec((tk, GROUP_W), lambda b, i: (i, 0)),
                  pl.BlockSpec((tk, GROUP_W), lambda b, i: (i, 0))],
        out_specs=(pl.BlockSpec((1, N_KV, tk, HEAD_DIM), lambda b, i: (b, 0, i, 0)),
                   pl.BlockSpec((1, N_KV, tk, 2 * HEAD_DIM), lambda b, i: (b, 0, i, 0)),
                   pl.BlockSpec((1, N_KV, KEY_PAD, HEAD_DIM), lambda b, i: (b, 0, 0, 0)),
                   pl.BlockSpec((1, N_KV, KEY_PAD, 2 * HEAD_DIM), lambda b, i: (b, 0, 0, 0))),
        compiler_params=_params(("parallel", "arbitrary"), 40),
    )(*_pin(pkv, pm_kv, kg, cos, sin))


def _ones_cols(rows, valid):
    r = lax.broadcasted_iota(jnp.int32, (rows, HEAD_DIM), 0)
    col = lax.broadcasted_iota(jnp.int32, (rows, HEAD_DIM), 1)
    return jnp.where((col < 2) & (r < valid), 1.0, 0.0).astype(F32)


def _tail_bias():
    col = lax.broadcasted_iota(jnp.int32, (1, KEY_PAD), 1)
    return jnp.where(col < N_META, 0.0, -1e30).astype(F32)


LOG2E = 1.4426950408889634


def _kv_specs(s_len):
    return [pl.BlockSpec((1, 1, s_len, HEAD_DIM), lambda b, g, i: (b, g, 0, 0)),
            pl.BlockSpec((1, 1, s_len, 2 * HEAD_DIM), lambda b, g, i: (b, g, 0, 0)),
            pl.BlockSpec((1, 1, KEY_PAD, HEAD_DIM), lambda b, g, i: (b, g, 0, 0)),
            pl.BlockSpec((1, 1, KEY_PAD, 2 * HEAD_DIM), lambda b, g, i: (b, g, 0, 0))]


def _attn_fwd(q2, kv4, nb, tq):
    rows = q2.shape[0]
    s_len = rows // nb
    nq = s_len // tq

    def body(q_ref, k1_ref, v1_ref, k2_ref, v2_ref, o_ref, lse_ref):
        qs = q_ref[...]
        k1, k2, v1, v2 = k1_ref[0, 0], k2_ref[0, 0], v1_ref[0, 0], v2_ref[0, 0]
        bias = _tail_bias()
        outs, lses = [], []

        def scores(h):
            qh = qs[:, HEAD_DIM * h:HEAD_DIM * (h + 1)]
            return _dot_nt(qh, k1), _dot_nt(qh, k2) + bias

        ahead = scores(0)
        for h in range(GQA):
            s1, s2 = ahead
            if h + 1 < GQA:
                ahead = scores(h + 1)
            m = jnp.maximum(jnp.max(s1, axis=-1, keepdims=True), jnp.max(s2, axis=-1, keepdims=True))
            oe = _dot(jnp.exp2(s1 - m).astype(BF16), v1) + _dot(jnp.exp2(s2 - m).astype(BF16), v2)
            l = oe[:, HEAD_DIM:HEAD_DIM + 1]
            outs.append(oe[:, :HEAD_DIM] / l)
            lses.append(m + jnp.log2(l))
        o_ref[...] = jnp.concatenate(outs, axis=1)
        lse_ref[0, 0] = jnp.concatenate(lses, axis=1)

    return pl.pallas_call(
        body, name="attn_fwd", grid=(nb, N_KV, nq),
        out_shape=(_sds((rows, D), F32), _sds((nb, N_KV, s_len, GQA), F32)),
        in_specs=[pl.BlockSpec((tq, GROUP_W), lambda b, g, i: (b * nq + i, g))] + _kv_specs(s_len),
        out_specs=(pl.BlockSpec((tq, GROUP_W), lambda b, g, i: (b * nq + i, g)),
                   pl.BlockSpec((1, 1, tq, GQA), lambda b, g, i: (b, g, i, 0))),
        compiler_params=_params(("parallel", "parallel", "parallel"), 48),
    )(*_pin(q2, *kv4))


def _mid(x2, t2, c0, cz, o, e, w3, cn_g, cn_b, tm):
    rows = x2.shape[0]

    def body(x_ref, t_ref, c0_ref, cz_ref, o_ref, e_ref, w_ref, g_ref, b_ref,
             dy_o, mt_o, c3t_o, o2t_o, dyc_o, dya_o, do_o, dc0_o, dcz_o, de_o, sums_o):
        wco, wao, wo = w_ref[0], w_ref[1], w_ref[2]
        cn_g_v = g_ref[...]

        @pl.when(pl.program_id(0) == 0)
        def _():
            sums_o[...] = jnp.zeros_like(sums_o)

        az, gc, ga = e_ref[:, :D], e_ref[:, D:2 * D], e_ref[:, 2 * D:]
        saz = _sig(az)
        gaz = az * saz
        ov = o_ref[...]
        o2 = ov * gaz
        ya = _dot(o2.astype(BF16), wao)
        o2t_o[...] = o2.T.astype(BF16)
        c0v = c0_ref[...]
        xc = c0v - jnp.mean(c0v, axis=-1, keepdims=True)
        rstd = lax.rsqrt(jnp.mean(xc * xc, axis=-1, keepdims=True) + EPS)
        n = xc * rstd
        c1 = n * cn_g_v + b_ref[...]
        s1 = _sig(c1)
        c2 = c1 * s1
        czv = cz_ref[...]
        sz = _sig(czv)
        gz = czv * sz
        c3 = c2 * gz
        yc = _dot(c3.astype(BF16), wco)
        c3t_o[...] = c3.T.astype(BF16)
        sc, sa = _sig(gc), _sig(ga)
        merged = sc * yc + sa * ya
        out = _dot(merged.astype(BF16), wo)
        mt_o[...] = merged.T.astype(BF16)
        err = x_ref[...] + out - t_ref[...]
        dy = err * (1.0 / D)
        dy_o[...] = dy
        dm = _dot_nt(dy.astype(BF16), wo)
        sums_o[3:4, :] += jnp.sum(err * err, axis=0, keepdims=True)
        dyc = dm * sc
        dya = dm * sa
        dycb, dyab = dyc.astype(BF16), dya.astype(BF16)
        dc3 = _dot_nt(dycb, wco)
        do2 = _dot_nt(dyab, wao)
        dyc_o[...] = dycb
        dya_o[...] = dyab
        de_o[:, D:2 * D] = (dyc * yc * (1.0 - sc)).astype(BF16)
        de_o[:, 2 * D:] = (dya * ya * (1.0 - sa)).astype(BF16)
        dcz_o[...] = (dc3 * c2 * _dsilu(czv, sz)).astype(BF16)
        dc1 = dc3 * gz * _dsilu(c1, s1)
        dn = dc1 * cn_g_v
        dc0 = rstd * (dn - jnp.mean(dn, axis=-1, keepdims=True) - n * jnp.mean(dn * n, axis=-1, keepdims=True))
        dc0_o[...] = dc0
        sums_o[0:1, :] += jnp.sum(dc1 * n, axis=0, keepdims=True)
        sums_o[1:2, :] += jnp.sum(dc1, axis=0, keepdims=True)
        sums_o[2:3, :] += jnp.sum(dc0, axis=0, keepdims=True)
        do_o[...] = do2 * gaz
        de_o[:, :D] = (do2 * ov * _dsilu(az, saz)).astype(BF16)

    row = lambda wd: pl.BlockSpec((tm, wd), lambda i: (i, 0))
    col = pl.BlockSpec((D, tm), lambda i: (0, i))
    vec = pl.BlockSpec((1, D), lambda i: (0, 0))
    f32o = lambda wd: _sds((rows, wd), F32)
    b16o = lambda wd: _sds((rows, wd), BF16)
    tpo = _sds((D, rows), BF16)
    return pl.pallas_call(
        body, name="mid", grid=(rows // tm,),
        out_shape=(f32o(D), tpo, tpo, tpo, b16o(D), b16o(D), f32o(D), f32o(D), b16o(D), b16o(3 * D),
                   _sds((8, D), F32)),
        in_specs=[row(D), row(D), row(D), row(D), row(D), row(3 * D),
                  pl.BlockSpec((3, D, D), lambda i: (0, 0, 0)), vec, vec],
        out_specs=(row(D), col, col, col, row(D), row(D), row(D), row(D), row(D), row(3 * D),
                   pl.BlockSpec((8, D), lambda i: (0, 0))),
        compiler_params=_params(("arbitrary",), 60),
    )(*_pin(x2, t2, c0, cz, o, e, w3, cn_g, cn_b))


def _do_prep(d_o, o, tm):
    rows = d_o.shape[0]

    def body(do_ref, o_ref, doe_o, dot_o, ind_ref, spread_ref, place_ref):
        @pl.when(pl.program_id(0) == 0)
        def _():
            def grid(shape):
                return (lax.broadcasted_iota(jnp.int32, shape, 0), lax.broadcasted_iota(jnp.int32, shape, 1))

            r, c = grid((D, 128))
            ind_ref[...] = jnp.where((r >> 6) == c, 1.0, 0.0).astype(BF16)
            r, c = grid((D, 2 * D))
            spread_ref[...] = jnp.where(c == 128 * (r >> 6) + (r & 63), 1.0, 0.0).astype(BF16)
            r, c = grid((2 * 128, 2 * D))
            place_ref[...] = jnp.where(c == 128 * (r & 127) + 64 + (r >> 7), -1.0, 0.0).astype(BF16)

        dov = do_ref[...]
        delta = _apply(dov * o_ref[...], ind_ref[...])
        d_hi = delta.astype(BF16)
        d_lo = (delta - d_hi.astype(F32)).astype(BF16)
        tails = _dot(jnp.concatenate([d_hi, d_lo], axis=1), place_ref[...])
        doe_o[...] = (_dot(dov.astype(BF16), spread_ref[...]) + tails).astype(BF16)
        dot_o[...] = dov.T.astype(BF16)

    row = pl.BlockSpec((tm, D), lambda i: (i, 0))
    return pl.pallas_call(
        body, name="do_prep", grid=(rows // tm,),
        out_shape=(_sds((rows, 2 * D), BF16), _sds((D, rows), BF16)),
        in_specs=[row, row],
        out_specs=(pl.BlockSpec((tm, 2 * D), lambda i: (i, 0)), pl.BlockSpec((D, tm), lambda i: (0, i))),
        scratch_shapes=[pltpu.VMEM((D, 128), BF16), pltpu.VMEM((D, 2 * D), BF16), pltpu.VMEM((2 * 128, 2 * D), BF16)],
        compiler_params=_params(("arbitrary",), 40),
    )(*_pin(d_o, o))


def _q_post(dqr, pq, qg, cos, sin, nb, tm):
    rows = pq.shape[0]
    nt = rows // nb // tm

    def body(dq_ref, q_ref, g_ref, cos_ref, sin_ref, dq_o, dg_o):
        @pl.when((pl.program_id(0) == 0) & (pl.program_id(1) == 0))
        def _():
            dg_o[...] = jnp.zeros_like(dg_o)

        gv, cosv, sinv = g_ref[...], cos_ref[...], sin_ref[...]
        acc = jnp.zeros((1, GROUP_W), F32)
        mats = _qk_mats()
        for g in range(N_KV):
            gs = slice(GROUP_W * g, GROUP_W * (g + 1))
            qv = q_ref[:, gs]
            r = lax.rsqrt(_apply(qv * qv, mats[0]) + EPS)
            dq, dgr = _qk_bwd(dq_ref[:, gs], qv, r, gv, cosv, sinv, mats)
            dq_o[:, gs] = dq.astype(BF16)
            acc = acc + jnp.sum(dgr, axis=0, keepdims=True)
        dg_o[...] += acc

    row = pl.BlockSpec((tm, D), lambda b, i: (b * nt + i, 0))
    rope = pl.BlockSpec((tm, GROUP_W), lambda b, i: (i, 0))
    vec = pl.BlockSpec((1, GROUP_W), lambda b, i: (0, 0))
    return pl.pallas_call(
        body, name="q_post", grid=(nb, nt),
        out_shape=(_sds((rows, D), BF16), _sds((1, GROUP_W), F32)),
        in_specs=[row, row, vec, rope, rope], out_specs=(row, vec),
        compiler_params=_params(("arbitrary", "arbitrary"), 32),
    )(*_pin(dqr, pq, qg, cos, sin))


def _attn_bwd(q2, qst, kv4, doe, dot_, lse, nb, tq):
    rows = q2.shape[0]
    s_len = rows // nb
    nq = s_len // tq
    scale = 1.0 / math.sqrt(HEAD_DIM)

    def body(q_ref, qt_ref, k1_ref, v1_ref, k2_ref, v2_ref, doe_ref, dot_ref, lse_ref,
             dq_o, dkt_o, dvt_o, dkt2_o, dvt2_o):
        i = pl.program_id(2)
        lse = lse_ref[0, 0]
        k1, k2, v1, v2 = k1_ref[0, 0], k2_ref[0, 0], v1_ref[0, 0], v2_ref[0, 0]
        bias = _tail_bias()
        dkt1, dkt2 = jnp.zeros((HEAD_DIM, s_len), F32), jnp.zeros((HEAD_DIM, KEY_PAD), F32)
        dvt1, dvt2 = jnp.zeros((HEAD_DIM, s_len), F32), jnp.zeros((HEAD_DIM, KEY_PAD), F32)

        def products(h):
            qh = q_ref[:, HEAD_DIM * h:HEAD_DIM * (h + 1)]
            dh = doe_ref[:, 2 * HEAD_DIM * h:2 * HEAD_DIM * (h + 1)]
            return _dot_nt(qh, k1), _dot_nt(qh, k2) + bias, _dot_nt(dh, v1), _dot_nt(dh, v2)

        ahead = products(0)
        for h in range(GQA):
            hs = slice(HEAD_DIM * h, HEAD_DIM * (h + 1))
            s1, s2, dp1, dp2 = ahead
            if h + 1 < GQA:
                ahead = products(h + 1)
            lse_h = lse[:, h:h + 1]
            p1 = jnp.exp2(s1 - lse_h)
            p2 = jnp.exp2(s2 - lse_h)
            ds1 = (p1 * dp1).astype(BF16)
            ds2 = (p2 * dp2).astype(BF16)
            dq_o[:, hs] = (_dot(ds1, k1) + _dot(ds2, k2)) * scale
            dkt1 = dkt1 + _dot(qt_ref[hs, :], ds1)
            dkt2 = dkt2 + _dot(qt_ref[hs, :], ds2)
            dvt1 = dvt1 + _dot(dot_ref[hs, :], p1.astype(BF16))
            dvt2 = dvt2 + _dot(dot_ref[hs, :], p2.astype(BF16))

        @pl.when(i == 0)
        def _():
            dkt_o[0, 0], dkt2_o[0, 0], dvt_o[0, 0], dvt2_o[0, 0] = dkt1, dkt2, dvt1, dvt2

        @pl.when(i > 0)
        def _():
            dkt_o[0, 0] += dkt1
            dkt2_o[0, 0] += dkt2
            dvt_o[0, 0] += dvt1
            dvt2_o[0, 0] += dvt2

    qspec = pl.BlockSpec((tq, GROUP_W), lambda b, g, i: (b * nq + i, g))
    qtspec = pl.BlockSpec((GROUP_W, tq), lambda b, g, i: (g, b * nq + i))
    tspec = pl.BlockSpec((1, 1, HEAD_DIM, s_len), lambda b, g, i: (b, g, 0, 0))
    t2spec = pl.BlockSpec((1, 1, HEAD_DIM, KEY_PAD), lambda b, g, i: (b, g, 0, 0))
    tshape = _sds((nb, N_KV, HEAD_DIM, s_len), F32)
    t2shape = _sds((nb, N_KV, HEAD_DIM, KEY_PAD), F32)
    return pl.pallas_call(
        body, name="attn_bwd", grid=(nb, N_KV, nq),
        out_shape=(_sds((rows, D), F32), tshape, tshape, t2shape, t2shape),
        in_specs=[qspec, qtspec] + _kv_specs(s_len)
        + [pl.BlockSpec((tq, 2 * GROUP_W), lambda b, g, i: (b * nq + i, g)), qtspec,
           pl.BlockSpec((1, 1, tq, GQA), lambda b, g, i: (b, g, i, 0))],
        out_specs=(qspec, tspec, tspec, t2spec, t2spec),
        compiler_params=_params(("parallel", "parallel", "arbitrary"), 56),
    )(*_pin(q2, qst, *kv4, doe, dot_, lse))


def _kv_bwd(dkt, dvt, dkt2, dvt2, pkv, pm_kv, kg, cos, sin, nb):
    rows = pkv.shape[0]
    s_len = rows // nb
    tk = min(512, s_len)
    nt = s_len // tk

    def body(dk_ref, dv_ref, dk2_ref, dv2_ref, kv_ref, m_ref, g_ref, cos_ref, sin_ref, d_o, dm_o, dg_o):
        b, i = pl.program_id(0), pl.program_id(1)
        gv = g_ref[...]
        mats = _qk_mats()

        @pl.when((b == 0) & (i == 0))
        def _():
            dg_o[...] = jnp.zeros_like(dg_o)

        kx = kv_ref[:, :GROUP_W]
        r = lax.rsqrt(_apply(kx * kx, mats[0]) + EPS)
        dk, dgr = _qk_bwd(dk_ref[0].T, kx, r, gv, cos_ref[...], sin_ref[...], mats)
        d_o[:, :GROUP_W] = dk.astype(BF16)
        d_o[:, GROUP_W:] = dv_ref[0].T.astype(BF16)
        dg_o[...] += jnp.sum(dgr, axis=0, keepdims=True)

        @pl.when(i == 0)
        def _():
            kxm = m_ref[:, :GROUP_W]
            rm = lax.rsqrt(_apply(kxm * kxm, mats[0]) + EPS)
            dn = dk2_ref[0].T[0:N_META]
            dyg = dn * gv
            dm_o[0, :, :GROUP_W] = rm * dyg - kxm * (rm * rm * rm) * _apply(dyg * kxm, mats[0])
            dm_o[0, :, GROUP_W:] = dv2_ref[0].T[0:N_META]
            dg_o[...] += jnp.sum(dn * kxm * rm, axis=0, keepdims=True)

    tspec = pl.BlockSpec((1, GROUP_W, tk), lambda b, i: (b, 0, i))
    t2spec = pl.BlockSpec((1, GROUP_W, KEY_PAD), lambda b, i: (b, 0, 0))
    rope = pl.BlockSpec((tk, GROUP_W), lambda b, i: (i, 0))
    return pl.pallas_call(
        body, name="kv_bwd", grid=(nb, nt),
        out_shape=(_sds((rows, 512), BF16), _sds((nb, N_META, 512), F32),
                   _sds((1, GROUP_W), F32)),
        in_specs=[tspec, tspec, t2spec, t2spec, pl.BlockSpec((tk, 512), lambda b, i: (b * nt + i, 0)),
                  pl.BlockSpec((N_META, 512), lambda b, i: (0, 0)), pl.BlockSpec((1, GROUP_W), lambda b, i: (0, 0)),
                  rope, rope],
        out_specs=(pl.BlockSpec((tk, 512), lambda b, i: (b * nt + i, 0)),
                   pl.BlockSpec((1, N_META, 512), lambda b, i: (b, 0, 0)),
                   pl.BlockSpec((1, GROUP_W), lambda b, i: (0, 0))),
        compiler_params=_params(("arbitrary", "arbitrary"), 40),
    )(*_pin(dkt, dvt, dkt2, dvt2, pkv, pm_kv, kg, cos, sin))


def _conv_bwd(dc0, pconv, pm_conv, conv_w, nb, tm):
    rows = pconv.shape[0]
    nt = rows // nb // tm

    def body(dcur, dprev, dnxt, cur, meta, w_ref, da_o, dam_o, gw_o, dext, dsh, accw):
        b, i = pl.program_id(0), pl.program_id(1)
        dext[0:16] = jnp.zeros((16, D), F32)
        dext[16:32] = jnp.where(i == 0, 0.0, dprev[...])
        dext[32:32 + tm] = dcur[...]
        dext[32 + tm:48 + tm] = jnp.where(i == nt - 1, 0.0, dnxt[...])
        _shifted_copies(dsh, dext, tm + 40)
        accw[...] = jnp.zeros_like(accw)

        @pl.when((b == 0) & (i == 0))
        def _():
            gw_o[...] = jnp.zeros_like(gw_o)

        for c0 in range(0, D, 256):
            cs = slice(c0, c0 + 256)
            for r0 in range(0, tm, 32):
                cv = cur[r0:r0 + 32, c0:c0 + 256]
                sg = _sig(cur[r0:r0 + 32, D + c0:D + c0 + 256])
                u = cv * sg
                acc = jnp.zeros((32, 256), F32)
                for j in range(CONV_K):
                    d = _rows32(dsh, dext, r0 + 47 - j, cs)
                    acc = acc + d * w_ref[j:j + 1, cs]
                    p = d * u
                    accw[8 * j:8 * j + 8, cs] += (p[0:8] + p[8:16]) + (p[16:24] + p[24:32])
                da_o[r0:r0 + 32, cs] = (acc * sg).astype(BF16)
                da_o[r0:r0 + 32, D + c0:D + c0 + 256] = (acc * cv * sg * (1.0 - sg)).astype(BF16)
        for j in range(CONV_K):
            gw_o[j:j + 1, :] += jnp.sum(accw[8 * j:8 * j + 8, :], axis=0, keepdims=True)

        @pl.when(i == 0)
        def _():
            for c0 in range(0, D, 256):
                cs = slice(c0, c0 + 256)
                cv = meta[:, c0:c0 + 256]
                sg = _sig(meta[:, D + c0:D + c0 + 256])
                um = cv * sg
                acc = jnp.zeros((16, 256), F32)
                for j in range(CONV_K):
                    d = dext[31 - j:47 - j, cs]
                    acc = acc + d * w_ref[j:j + 1, cs]
                    gw_o[j:j + 1, cs] += jnp.sum(d * um, axis=0, keepdims=True)
                dam_o[0, :, cs] = acc * sg
                dam_o[0, :, D + c0:D + c0 + 256] = acc * cv * sg * (1.0 - sg)

    return pl.pallas_call(
        body, name="conv_bwd", grid=(nb, nt),
        out_shape=(_sds((rows, 2048), BF16), _sds((nb, N_META, 2048), F32),
                   _sds((32, D), F32)),
        in_specs=_halo_specs(D, tm, nt, rows)
        + [pl.BlockSpec((tm, 2048), lambda b, i: (b * nt + i, 0)),
           pl.BlockSpec((16, 2048), lambda b, i: (0, 0)), pl.BlockSpec((32, D), lambda b, i: (0, 0))],
        out_specs=(pl.BlockSpec((tm, 2048), lambda b, i: (b * nt + i, 0)),
                   pl.BlockSpec((1, N_META, 2048), lambda b, i: (b, 0, 0)),
                   pl.BlockSpec((32, D), lambda b, i: (0, 0))),
        scratch_shapes=[pltpu.VMEM((tm + 48, D), F32), pltpu.VMEM((8, tm + 40, D), F32),
                        pltpu.VMEM((8 * CONV_K, D), F32)],
        compiler_params=_params(("arbitrary", "arbitrary"), 48),
    )(*_pin(dc0, dc0, dc0, pconv, pm_conv, conv_w))


def _meta_bwd(dam, ddm, w_full, meta_full, norm_g):
    nb = dam.shape[0]

    def body(a_ref, d_ref, wc_ref, wkv_ref, m_ref, g_ref, gm_o, dg_o):
        a, d = a_ref[0], d_ref[0]
        for b in range(1, nb):
            a = a + a_ref[b]
            d = d + d_ref[b]
        dxn = _dot_nt(a.astype(BF16), wc_ref[...]) + _dot_nt(d.astype(BF16), wkv_ref[...])
        v = m_ref[...]
        r = lax.rsqrt(jnp.mean(v * v, axis=-1, keepdims=True) + EPS)
        gm_o[...] = _rms_bwd(dxn, v, r, g_ref[...])
        dg_o[...] = jnp.sum(dxn * v * r, axis=0, keepdims=True)

    return pl.pallas_call(
        body, name="meta_bwd", grid=(1,),
        out_shape=(_sds((N_META, D), F32), _sds((1, D), F32)),
        in_specs=[pl.BlockSpec((nb, N_META, 2048), lambda i: (0, 0, 0)), pl.BlockSpec((nb, N_META, 512), lambda i: (0, 0, 0)),
                  pl.BlockSpec((D, 2048), lambda i: (0, 0)), pl.BlockSpec((D, 512), lambda i: (0, G_KV[0] // 512)),
                  pl.BlockSpec((N_META, D), lambda i: (0, 0)), pl.BlockSpec((1, D), lambda i: (0, 0))],
        out_specs=(pl.BlockSpec((N_META, D), lambda i: (0, 0)), pl.BlockSpec((1, D), lambda i: (0, 0))),
        compiler_params=_params(("arbitrary",), 32),
    )(*_pin(dam, ddm, w_full, w_full, meta_full, norm_g))


def _dxn(d_groups, w_full, x2, dy, norm_g, dg_init, tm):
    rows = x2.shape[0]
    groups = (G_CONV, G_CZ, G_Q, G_KV, G_E)

    def body(da, db, dq, dd, de, w_hbm, x_ref, dy_ref, g_ref, gi_ref, gx_o, dg_o, w_vmem, sem):
        @pl.when(pl.program_id(0) == 0)
        def _():
            cp = pltpu.make_async_copy(w_hbm, w_vmem, sem)
            cp.start()
            cp.wait()
            dg_o[...] = gi_ref[...]

        dxn = jnp.zeros((tm, D), F32)
        for ref, (off, wd) in zip((da, db, dq, dd, de), groups):
            for c0 in range(0, wd, 512):
                dxn = dxn + _dot_nt(ref[:, c0:c0 + 512], w_vmem[:, off + c0:off + c0 + 512])
        v = x_ref[...]
        r = lax.rsqrt(jnp.mean(v * v, axis=-1, keepdims=True) + EPS)
        gx_o[...] = dy_ref[...] + _rms_bwd(dxn, v, r, g_ref[...])
        dg_o[...] += jnp.sum(dxn * v * r, axis=0, keepdims=True)

    row = lambda wd: pl.BlockSpec((tm, wd), lambda i: (i, 0))
    vec = pl.BlockSpec((1, D), lambda i: (0, 0))
    return pl.pallas_call(
        body, name="dxn", grid=(rows // tm,),
        out_shape=(_sds((rows, D), F32), _sds((1, D), F32)),
        in_specs=[row(wd) for _, wd in groups] + [pl.BlockSpec(memory_space=pl.ANY), row(D), row(D), vec, vec],
        out_specs=(row(D), vec),
        scratch_shapes=[pltpu.VMEM((D, IN_DIM), BF16), pltpu.SemaphoreType.DMA],
        compiler_params=_params(("arbitrary",), 56),
    )(*_pin(*d_groups, w_full, x2, dy, norm_g, dg_init))


def _wgrad(at, b, bufs, slot, col_off, name, meta=None):
    buf, bufb = bufs
    rows, n = b.shape
    tn = next(t for t in (1536, 1024, 512) if n % t == 0 and col_off % t == 0)
    tk = min(2048, rows)
    nk = rows // tk
    j0 = col_off // tn

    def body(*refs):
        if meta is None:
            at_ref, b_ref, _, _, o_ref, ob_ref = refs
        else:
            at_ref, b_ref, xm_ref, dm_ref, _, _, o_ref, ob_ref = refs
        k = pl.program_id(1)

        @pl.when(k == 0)
        def _():
            if meta is None:
                o_ref[0] = jnp.zeros((D, tn), F32)
            else:
                dm = dm_ref[0]
                for e in range(1, dm_ref.shape[0]):
                    dm = dm + dm_ref[e]
                dm = jnp.concatenate([dm, jnp.zeros((128 - N_META, tn), F32)], axis=0)
                o_ref[0] = _dot(xm_ref[...], dm.astype(BF16))

        o_ref[0] += _dot(at_ref[...], b_ref[...].astype(BF16))

        @pl.when(k == nk - 1)
        def _():
            ob_ref[0] = o_ref[0].astype(BF16)

    in_specs = [pl.BlockSpec((D, tk), lambda j, k: (0, k)), pl.BlockSpec((tk, tn), lambda j, k: (k, j))]
    args = [at, b]
    if meta is not None:
        xmt, dm = meta
        in_specs += [pl.BlockSpec((D, 128), lambda j, k: (0, 0)),
                     pl.BlockSpec((dm.shape[0], N_META, tn), lambda j, k: (0, 0, j))]
        args += [xmt, dm]
    in_specs += [pl.BlockSpec(memory_space=pl.ANY)] * 2
    args += [buf, bufb]
    blk = pl.BlockSpec((1, D, tn), lambda j, k: (slot, 0, j0 + j))
    return pl.pallas_call(
        body, name=name, grid=(n // tn, nk),
        out_shape=(_sds(buf.shape, F32), _sds(buf.shape, BF16)),
        in_specs=in_specs,
        out_specs=(blk, blk),
        input_output_aliases={len(args) - 2: 0, len(args) - 1: 1},
        compiler_params=_params(("parallel", "arbitrary"), 56),
    )(*_pin(*args))


def _rope_tables(s_len):
    pos = jnp.arange(s_len, dtype=jnp.int32)
    row_ids = (pos // GRID_W).astype(F32)
    col_ids = (pos % GRID_W).astype(F32)
    inv_freq = ROPE_THETA ** (-jnp.arange(ROPE_FREQS, dtype=F32) / ROPE_FREQS)
    a_row = row_ids[:, None] * inv_freq[None, :]
    a_col = col_ids[:, None] * inv_freq[None, :]
    ang = jnp.concatenate([a_row, a_row, a_col, a_col], axis=-1)
    return jnp.tile(jnp.cos(ang), (1, GQA)), jnp.tile(jnp.sin(ang), (1, GQA))


def _local_step(x, loss_target, norm_g, conv_b, cn_g, cn_b, q_g, k_g, w_full, w3_full, conv_w_full, meta_full,
                hooks=None):
    nb, s_len, _ = x.shape
    rows = nb * s_len
    x2 = x.reshape(rows, D)
    t2 = loss_target.reshape(rows, D)
    cos, sin = _rope_tables(s_len)
    qg = jnp.tile(q_g, (1, GQA))
    kg = jnp.tile(k_g, (1, N_KV))

    xnmt, pm_conv, pm_kv = _meta_fwd(meta_full, norm_g, w_full)
    pconv, pcz, pq, pkv, pe, xnt, q2, qst = _in_proj(x2, norm_g, w_full, qg, cos, sin, nb, 256)
    c0 = _conv_fwd(pconv, pm_conv, conv_w_full, conv_b, nb, min(512, s_len))
    tq = min(512, s_len)
    kv4 = _kv_prep(pkv, pm_kv, kg, cos, sin, nb)
    o, lse = _attn_fwd(q2, kv4, nb, min(1024, s_len))
    if callable(w3_full):
        w3_full = w3_full(o)
    dy, mt, c3t, o2t, dyc, dya, d_o, dc0, dcz, de, sums = _mid(x2, t2, c0, pcz, o, pe, w3_full, cn_g, cn_b, 256)
    gw3 = (lax.empty((3, D, D), F32), lax.empty((3, D, D), BF16))
    gw3 = _wgrad(c3t, dyc, gw3, 0, 0, "wgrad_conv_out")
    gw3 = _wgrad(o2t, dya, gw3, 1, 0, "wgrad_attn_out")
    gw3 = _wgrad(mt, dy, gw3, 2, 0, "wgrad_out")
    if hooks is not None:
        lse = lse + hooks[0](gw3)[0, 0]
    doe, dot_ = _do_prep(d_o, o, min(512, s_len))
    dqr, dkt, dvt, dkt2, dvt2 = _attn_bwd(q2, qst, kv4, doe, dot_, lse, nb, min(1024, s_len))
    qg_post = qg if hooks is None else qg + hooks[1](dqr)[0:1, 0:1]
    dq, dqg = _q_post(dqr, pq, qg_post, cos, sin, nb, min(512, s_len))
    dd, ddm, dkg = _kv_bwd(dkt.reshape(nb, GROUP_W, s_len), dvt.reshape(nb, GROUP_W, s_len),
                           dkt2.reshape(nb, GROUP_W, KEY_PAD), dvt2.reshape(nb, GROUP_W, KEY_PAD),
                           pkv, pm_kv, kg, cos, sin, nb)
    da, dam, gcw = _conv_bwd(dc0, pconv, pm_conv, conv_w_full, nb, min(512, s_len))
    gmeta, dng_m = _meta_bwd(dam, ddm, w_full, meta_full, norm_g)

    gwin = (lax.empty((1, D, IN_DIM), F32), lax.empty((1, D, IN_DIM), BF16))
    gwin = _wgrad(xnt, da, gwin, 0, G_CONV[0], "wgrad_in_conv", meta=(xnmt, dam))
    gwin = _wgrad(xnt, dcz, gwin, 0, G_CZ[0], "wgrad_in_cz")
    gwin = _wgrad(xnt, dq, gwin, 0, G_Q[0], "wgrad_in_q")
    gwin = _wgrad(xnt, dd, gwin, 0, G_KV[0], "wgrad_in_kv", meta=(xnmt, ddm))
    gwin = _wgrad(xnt, de, gwin, 0, G_E[0], "wgrad_in_e")

    pending = None
    if hooks is not None:
        token, pending = hooks[2](gwin, gcw, gmeta)
        dng_m = dng_m + token[0:1, 0:1]
    gx, dng = _dxn((da, dcz, dq, dd, de), w_full, x2, dy, norm_g, dng_m, 512)

    zeros = jnp.zeros((1, D - 2 * GROUP_W), F32)
    smalls = jnp.concatenate([dng, sums[2:3], sums[0:1], sums[1:2], jnp.concatenate([dqg, dkg, zeros], axis=1),
                              sums[3:4], jnp.zeros((2, D), F32)], axis=0)
    return gx.reshape(nb, s_len, D), gwin, gw3, gcw, gmeta, smalls, pending


def _xyc():
    return lax.axis_index("x"), lax.axis_index("y"), lax.axis_index("c")


_HBM = pl.BlockSpec(memory_space=pltpu.HBM)
_SEM = pl.BlockSpec(memory_space=pltpu.SEMAPHORE)
_EFFECT = pltpu.SideEffectType.DATAFLOW_SIDE_EFFECTING


def _reduce_sibling(gwin, gcm):
    def body(gwin_ref, gcm_ref, r_win, r_cm, send, recv):
        x, y, c = _xyc()
        o = 1 - c
        half = D // 2
        outs = ((gwin_ref.at[pl.ds(o * half, half), :], r_win), (gcm_ref.at[o], r_cm))
        cps = []
        for a, (src, dst) in enumerate(outs):
            cp = pltpu.make_async_remote_copy(src_ref=src, dst_ref=dst, send_sem=send.at[a], recv_sem=recv.at[a],
                                              device_id=(x, y, o), device_id_type=MESH)
            cp.start()
            cps.append(cp)
        for cp in cps:
            cp.wait()

    any_spec = pl.BlockSpec(memory_space=pl.ANY)
    return pl.pallas_call(
        body, name="reduce_sibling",
        out_shape=(_sds((D // 2, IN_DIM), BF16), _sds((24, D), F32)),
        in_specs=[any_spec] * 2, out_specs=(any_spec,) * 2,
        scratch_shapes=[pltpu.SemaphoreType.DMA((2,)), pltpu.SemaphoreType.DMA((2,))],
    )(*_pin(gwin, gcm))


def _w3_sibling_copy(gw3_ref, land_ref, send, recv):
    x, y, c = _xyc()
    return pltpu.make_async_remote_copy(src_ref=gw3_ref.at[:, :, 1 - c], dst_ref=land_ref, send_sem=send.at[0],
                                        recv_sem=recv.at[0], device_id=(x, y, 1 - c), device_id_type=MESH)


def _w3_sibling_start(gw3vb):
    land = lax.empty((3, N_CHIPS, 128, D), BF16)

    def body(src_ref, land_ref, send, recv, src_thru, land_thru, token):
        _w3_sibling_copy(src_ref, land_ref, send, recv).start()
        token[...] = jnp.zeros_like(token)

    outs = pl.pallas_call(
        body, name="w3_sibling_start",
        out_shape=(pltpu.SemaphoreType.DMA((1,)), pltpu.SemaphoreType.DMA((1,)),
                   pltpu.HBM(gw3vb.shape, BF16), pltpu.HBM(land.shape, BF16), jax.ShapeDtypeStruct((8, 128), F32)),
        in_specs=[_HBM, _HBM],
        out_specs=(_SEM, _SEM, _HBM, _HBM, pl.BlockSpec(memory_space=pltpu.VMEM)),
        input_output_aliases={0: 2, 1: 3},
        compiler_params=pltpu.CompilerParams(has_side_effects=_EFFECT),
    )(*_pin(gw3vb, land))
    return outs[0:4], outs[4]


def _w3_sibling_wait(send, recv, src, land, after):
    def body(src_ref, land_ref, send_ref, recv_ref, after_ref, src_out, land_out):
        cp = _w3_sibling_copy(src_ref, land_ref, send_ref, recv_ref)
        cp.wait_send()
        cp.wait_recv()

    outs = pl.pallas_call(
        body, name="w3_sibling_wait",
        out_shape=(pltpu.HBM(src.shape, BF16), pltpu.HBM(land.shape, BF16)),
        in_specs=[_HBM, _HBM, _SEM, _SEM, pl.BlockSpec(memory_space=pl.ANY)],
        out_specs=(_HBM, _HBM),
        input_output_aliases={0: 0, 1: 1},
        compiler_params=pltpu.CompilerParams(has_side_effects=_EFFECT),
    )(src, land, send, recv, after)
    return outs[1]


def _add_sibling_w3(gw3v, r_w3):
    c = lax.axis_index("c").astype(jnp.int32).reshape(1)

    def body(c_ref, a_ref, b_ref, o_ref):
        o_ref[0, 0] = (a_ref[0, 0, 0] + b_ref[0, 0].astype(F32)).astype(BF16)

    return pl.pallas_call(
        body, name="add_sibling_w3", out_shape=_sds((3, 4, 128, D), BF16),
        grid_spec=pltpu.PrefetchScalarGridSpec(
            num_scalar_prefetch=1, grid=(3, 4),
            in_specs=[pl.BlockSpec((1, 1, 1, 128, D), lambda w, s, c_ref: (w, s, c_ref[0], 0, 0)),
                      pl.BlockSpec((1, 1, 128, D), lambda w, s, c_ref: (w, s, 0, 0))],
            out_specs=pl.BlockSpec((1, 1, 128, D), lambda w, s, c_ref: (w, s, 0, 0))),
        compiler_params=_params(("parallel", "parallel"), 32),
    )(c, *_pin(gw3v, r_w3))


def _add_sibling(gwin, gcm, r_win, r_cm):
    c = lax.axis_index("c").astype(jnp.int32).reshape(1)
    half = D // 2
    tr = 64

    def body1(c_ref, a_ref, b_ref, o_ref):
        o_ref[...] = (a_ref[...] + b_ref[...].astype(F32)).astype(BF16)

    cs_win = pl.pallas_call(
        body1, name="add_sibling_w_in", out_shape=_sds((half, IN_DIM), BF16),
        grid_spec=pltpu.PrefetchScalarGridSpec(
            num_scalar_prefetch=1, grid=(half // tr,),
            in_specs=[pl.BlockSpec((tr, IN_DIM), lambda i, c_ref: (c_ref[0] * (half // tr) + i, 0)),
                      pl.BlockSpec((tr, IN_DIM), lambda i, c_ref: (i, 0))],
            out_specs=pl.BlockSpec((tr, IN_DIM), lambda i, c_ref: (i, 0))),
        compiler_params=_params(("parallel",), 32),
    )(c, *_pin(gwin, r_win))

    def body3(c_ref, a_ref, b_ref, o_ref):
        o_ref[...] = a_ref[0] + b_ref[...]

    cs_cm = pl.pallas_call(
        body3, name="add_sibling_cm", out_shape=_sds((24, D), F32),
        grid_spec=pltpu.PrefetchScalarGridSpec(
            num_scalar_prefetch=1, grid=(1,),
            in_specs=[pl.BlockSpec((1, 24, D), lambda i, c_ref: (c_ref[0], 0, 0)),
                      pl.BlockSpec((24, D), lambda i, c_ref: (0, 0))],
            out_specs=pl.BlockSpec((24, D), lambda i, c_ref: (0, 0))),
        compiler_params=_params(("arbitrary",), 32),
    )(c, *_pin(gcm, r_cm))
    return cs_win, cs_cm


_CHIP_PARTS = {
    "win": (lambda ref, p: ref.at[:, pl.ds(p * W_IN_SHARD, W_IN_SHARD)], (D // 2, W_IN_SHARD), BF16),
    "w3": (lambda ref, p: ref.at[:, p], (3, 128, D), BF16),
    "cm": (lambda ref, p: ref.at[:, pl.ds(p * ROW_SHARD, ROW_SHARD)], (24, ROW_SHARD), F32),
}


def _reduce_chips_copies(kinds, srcs, lands, send, recv):
    x, y, c = _xyc()
    peers = ((1 - x, y), (x, 1 - y), (1 - x, 1 - y))
    cps = []
    for k, (px, py) in enumerate(peers):
        for a, (kind, src, land) in enumerate(zip(kinds, srcs, lands)):
            cps.append(pltpu.make_async_remote_copy(
                src_ref=_CHIP_PARTS[kind][0](src, 2 * px + py), dst_ref=land.at[k], send_sem=send.at[3 * a + k],
                recv_sem=recv.at[3 * a + k], device_id=(px, py, c), device_id_type=MESH))
    return cps


def _reduce_chips_start(kinds, srcs, name):
    n = len(kinds)
    lands = tuple(lax.empty((3,) + _CHIP_PARTS[kind][1], _CHIP_PARTS[kind][2]) for kind in kinds)

    def body(*refs):
        srcs_in, lands_in, send, recv, token = refs[0:n], refs[n:2 * n], refs[2 * n], refs[2 * n + 1], refs[-1]
        for cp in _reduce_chips_copies(kinds, srcs_in, lands_in, send, recv):
            cp.start()
        token[...] = jnp.zeros_like(token)

    hbm = lambda a: pltpu.HBM(a.shape, a.dtype)
    outs = pl.pallas_call(
        body, name=name,
        out_shape=(pltpu.SemaphoreType.DMA((3 * n,)), pltpu.SemaphoreType.DMA((3 * n,)),
                   *[hbm(a) for a in srcs], *[hbm(a) for a in lands], jax.ShapeDtypeStruct((8, 128), F32)),
        in_specs=[_HBM] * (2 * n),
        out_specs=(_SEM, _SEM, *[_HBM] * (2 * n), pl.BlockSpec(memory_space=pltpu.VMEM)),
        input_output_aliases={i: i + 2 for i in range(2 * n)},
        compiler_params=pltpu.CompilerParams(has_side_effects=_EFFECT),
    )(*_pin(*srcs, *lands))
    return (outs[0], outs[1], outs[2:2 + n], outs[2 + n:2 + 2 * n]), outs[-1]


def _reduce_chips_wait(kinds, pending, after, name):
    send, recv, srcs, lands = pending
    n = len(kinds)

    def body(*refs):
        srcs_in, lands_in, send_ref, recv_ref = refs[0:n], refs[n:2 * n], refs[2 * n], refs[2 * n + 1]
        for cp in _reduce_chips_copies(kinds, srcs_in, lands_in, send_ref, recv_ref):
            cp.wait_send()
            cp.wait_recv()

    hbm = lambda a: pltpu.HBM(a.shape, a.dtype)
    outs = pl.pallas_call(
        body, name=name,
        out_shape=(*[hbm(a) for a in srcs], *[hbm(a) for a in lands]),
        in_specs=[_HBM] * (2 * n) + [_SEM, _SEM, pl.BlockSpec(memory_space=pl.ANY)],
        out_specs=(_HBM,) * (2 * n),
        input_output_aliases={i: i for i in range(2 * n)},
        compiler_params=pltpu.CompilerParams(has_side_effects=_EFFECT),
    )(*srcs, *lands, send, recv, after)
    return outs[0:n], outs[n:2 * n]


def _add_chips(cs_win, cs_w3, cs_cm, r_win, r_w3, r_cm):
    x, y, c = _xyc()
    idx = jnp.stack([2 * x + y, c]).astype(jnp.int32)
    half = D // 2
    tr = 128

    def body1(i_ref, a_ref, b_ref, o_ref):
        f = lambda v: v.astype(F32)
        o_ref[0] = (f(a_ref[...]) + f(b_ref[2])) + (f(b_ref[0]) + f(b_ref[1]))

    f_win = pl.pallas_call(
        body1, name="add_chips_w_in", out_shape=_sds((2, half, W_IN_SHARD), F32),
        grid_spec=pltpu.PrefetchScalarGridSpec(
            num_scalar_prefetch=1, grid=(half // tr,),
            in_specs=[pl.BlockSpec((tr, W_IN_SHARD), lambda i, r: (i, r[0])),
                      pl.BlockSpec((3, tr, W_IN_SHARD), lambda i, r: (0, i, 0))],
            out_specs=pl.BlockSpec((1, tr, W_IN_SHARD), lambda i, r: (r[1], i, 0))),
        compiler_params=_params(("parallel",), 32),
    )(idx, *_pin(cs_win, r_win))

    def body2(i_ref, a_ref, b_ref, o_ref):
        f = lambda v: v.astype(F32)
        o_ref[0, 0] = (f(a_ref[0, 0]) + f(b_ref[2, 0])) + (f(b_ref[0, 0]) + f(b_ref[1, 0]))

    f_w3 = pl.pallas_call(
        body2, name="add_chips_w3", out_shape=_sds((3, 2, 128, D), F32),
        grid_spec=pltpu.PrefetchScalarGridSpec(
            num_scalar_prefetch=1, grid=(3,),
            in_specs=[pl.BlockSpec((1, 1, 128, D), lambda w, r: (w, r[0], 0, 0)),
                      pl.BlockSpec((3, 1, 128, D), lambda w, r: (0, w, 0, 0))],
            out_specs=pl.BlockSpec((1, 1, 128, D), lambda w, r: (w, r[1], 0, 0))),
        compiler_params=_params(("parallel",), 32),
    )(idx, *_pin(cs_w3, r_w3))

    def body3(i_ref, a_ref, b_ref, o_ref):
        o_ref[0] = (a_ref[...] + b_ref[2]) + (b_ref[0] + b_ref[1])

    f_cm = pl.pallas_call(
        body3, name="add_chips_cm", out_shape=_sds((2, 24, ROW_SHARD), F32),
        grid_spec=pltpu.PrefetchScalarGridSpec(
            num_scalar_prefetch=1, grid=(1,),
            in_specs=[pl.BlockSpec((24, ROW_SHARD), lambda i, r: (0, r[0])),
                      pl.BlockSpec((3, 24, ROW_SHARD), lambda i, r: (0, 0, 0))],
            out_specs=pl.BlockSpec((1, 24, ROW_SHARD), lambda i, r: (r[1], 0, 0))),
        compiler_params=_params(("arbitrary",), 32),
    )(idx, *_pin(cs_cm, r_cm))
    return f_win, f_w3, f_cm


def _share_sibling(f_win, f_w3, f_cm, smalls):
    def body(win_in, w3_in, cm_in, sm_ref, win_ref, w3_ref, cm_ref, r_sm, send, recv, ssend, srecv, lsem):
        x, y, c = _xyc()
        o = 1 - c
        cps = []
        for a, (ref, sl) in enumerate(((win_ref, lambda h: win_ref.at[h]), (w3_ref, lambda h: w3_ref.at[:, h]),
                                       (cm_ref, lambda h: cm_ref.at[h]))):
            cp = pltpu.make_async_remote_copy(src_ref=sl(c), dst_ref=sl(c), send_sem=send.at[a], recv_sem=recv.at[a],
                                              device_id=(x, y, o), device_id_type=MESH)
            cp.start()
            cps.append((cp, sl))
        me = 4 * x + 2 * y + c
        loc = pltpu.make_async_copy(sm_ref, r_sm.at[me], lsem)
        loc.start()
        scps = []
        for d in range(1, 8):
            px, py, pc = (x + (d >> 2)) % 2, (y + ((d >> 1) & 1)) % 2, (c + (d & 1)) % 2
            cp = pltpu.make_async_remote_copy(src_ref=sm_ref, dst_ref=r_sm.at[me], send_sem=ssend.at[d - 1],
                                              recv_sem=srecv.at[d - 1], device_id=(px, py, pc), device_id_type=MESH)
            cp.start()
            scps.append((cp, 4 * px + 2 * py + pc))
        for a, (cp, sl) in enumerate(cps):
            pltpu.make_async_remote_copy(src_ref=sl(o), dst_ref=sl(o), send_sem=send.at[a], recv_sem=recv.at[a],
                                         device_id=(x, y, o), device_id_type=MESH).wait_recv()
            cp.wait_send()
        for d, (cp, pid) in enumerate(scps):
            pltpu.make_async_remote_copy(src_ref=sm_ref, dst_ref=r_sm.at[pid], send_sem=ssend.at[d],
                                         recv_sem=srecv.at[d], device_id=(x, y, c), device_id_type=MESH).wait_recv()
            cp.wait_send()
        loc.wait()

    any_spec = pl.BlockSpec(memory_space=pl.ANY)
    return pl.pallas_call(
        body, name="share_sibling",
        out_shape=(_sds(f_win.shape, F32), _sds(f_w3.shape, F32), _sds(f_cm.shape, F32), _sds((8, 8, D), F32)),
        in_specs=[any_spec] * 4, out_specs=(any_spec,) * 4,
        input_output_aliases={0: 0, 1: 1, 2: 2},
        scratch_shapes=[pltpu.SemaphoreType.DMA((3,)), pltpu.SemaphoreType.DMA((3,)),
                        pltpu.SemaphoreType.DMA((7,)), pltpu.SemaphoreType.DMA((7,)), pltpu.SemaphoreType.DMA],
    )(*_pin(f_win, f_w3, f_cm, smalls))


def _adamw_math(w, g, m, v):
    m = ADAM_B1 * m + (1.0 - ADAM_B1) * g
    v = ADAM_B2 * v + (1.0 - ADAM_B2) * (g * g)
    m_hat = m / (1.0 - ADAM_B1 ** ADAM_STEP)
    v_hat = v / (1.0 - ADAM_B2 ** ADAM_STEP)
    delta = -ADAM_LR * (m_hat / (jnp.sqrt(v_hat) + ADAM_EPS) + ADAM_WD * w)
    return delta, m, v


def _adamw(w, g, m, v, tr, name):
    rows, cols = w.shape

    def body(w_ref, g_ref, m_ref, v_ref, g_o, d_o, m_o, v_o):
        g = g_ref[...]
        g_o[...] = g
        d_o[...], m_o[...], v_o[...] = _adamw_math(w_ref[...], g, m_ref[...], v_ref[...])

    spec = pl.BlockSpec((tr, cols), lambda i: (i, 0))
    return pl.pallas_call(
        body, name=name, grid=(rows // tr,),
        out_shape=(_sds((rows, cols), F32),) * 4,
        in_specs=[spec] * 4, out_specs=(spec,) * 4,
        compiler_params=_params(("parallel",), 32),
    )(*_pin(w, g, m, v))


def _adamw3(g3, ws, ms, vs):
    def body(g_ref, *refs):
        w_refs, m_refs, v_refs, outs = refs[0:3], refs[3:6], refs[6:9], refs[9:]
        g_os, d_os, m_os, v_os = outs[0:3], outs[3:6], outs[6:9], outs[9:12]
        for i in range(3):
            g = g_ref[i]
            g_os[i][0] = g
            d_os[i][0], m_os[i][0], v_os[i][0] = _adamw_math(w_refs[i][0], g, m_refs[i][0], v_refs[i][0])

    return pl.pallas_call(
        body, name="adamw_w3", out_shape=(jax.ShapeDtypeStruct((1, ROW_SHARD, D), F32),) * 12,
        compiler_params=pltpu.CompilerParams(vmem_limit_bytes=48 << 20),
    )(g3, *ws, *ms, *vs)


def _adamw_cm(f_cm, ws, ms, vs):
    def body(f_ref, *refs):
        w_refs, m_refs, v_refs, outs = refs[0:2], refs[2:4], refs[4:6], refs[6:14]
        gcw, gmt = refs[14], refs[15]
        gcw[0:16] = f_ref[0, 0:16]
        gcw[16:32] = f_ref[1, 0:16]
        gmt[0:8] = f_ref[0, 16:24]
        gmt[8:16] = f_ref[1, 16:24]
        g_conv = gcw[0:CONV_K, :]
        g_meta = gmt[...]
        outs[0][0] = g_conv
        outs[1][...] = g_meta
        outs[2][0], outs[4][0], outs[6][0] = _adamw_math(w_refs[0][0], g_conv, m_refs[0][0], v_refs[0][0])
        outs[3][...], outs[5][...], outs[7][...] = _adamw_math(w_refs[1][...], g_meta, m_refs[1][...], v_refs[1][...])

    pair = (jax.ShapeDtypeStruct((1, CONV_K, ROW_SHARD), F32), jax.ShapeDtypeStruct((N_META, ROW_SHARD), F32))
    return pl.pallas_call(
        body, name="adamw_cm", out_shape=pair * 4,
        scratch_shapes=[pltpu.VMEM((32, ROW_SHARD), F32), pltpu.VMEM((N_META, ROW_SHARD), F32)],
    )(f_cm, *ws, *ms, *vs)


def _adamw_small(r_sm, ws, ms, vs):
    def body(s_ref, *refs):
        w_refs, m_refs, v_refs, outs = refs[0:6], refs[6:12], refs[12:18], refs[18:]
        loss_o, g_os, d_os, m_os, v_os = outs[0], outs[1:7], outs[7:13], outs[13:19], outs[19:25]
        g = s_ref[0]
        for dev in range(1, 8):
            g = g + s_ref[dev]
        qk = g[4:5, :]
        qg = qk[:, 0:HEAD_DIM]
        kg = qk[:, GROUP_W:GROUP_W + HEAD_DIM]
        for h in range(1, GQA):
            qg = qg + qk[:, HEAD_DIM * h:HEAD_DIM * (h + 1)]
            kg = kg + qk[:, GROUP_W + HEAD_DIM * h:GROUP_W + HEAD_DIM * (h + 1)]
        loss_o[...] = (0.5 / D) * jnp.sum(g[5:6, :], axis=-1, keepdims=True)
        for i, gi in enumerate((g[0:1], g[1:2], g[2:3], g[3:4], qg, kg)):
            g_os[i][...] = gi
            d_os[i][...], m_os[i][...], v_os[i][...] = _adamw_math(w_refs[i][...], gi, m_refs[i][...], v_refs[i][...])

    six = tuple(jax.ShapeDtypeStruct(w.shape, F32) for w in ws)
    return pl.pallas_call(
        body, name="adamw_small", out_shape=(jax.ShapeDtypeStruct((1, 1), F32),) + six * 4,
    )(r_sm, *ws, *ms, *vs)


def kernel(x, meta_tokens, norm_g, w_in, conv_w, conv_b, conv_norm_g, conv_norm_b, w_conv_out, q_norm_g, k_norm_g, w_attn_out, w_out, loss_target, m_meta_tokens, m_norm_g, m_w_in, m_conv_w, m_conv_b, m_conv_norm_g, m_conv_norm_b, m_w_conv_out, m_q_norm_g, m_k_norm_g, m_w_attn_out, m_w_out, v_meta_tokens, v_norm_g, v_w_in, v_conv_w, v_conv_b, v_conv_norm_g, v_conv_norm_b, v_w_conv_out, v_q_norm_g, v_k_norm_g, v_w_attn_out, v_w_out):
    pad_k = lambda a: jnp.pad(a[0], ((0, 32 - CONV_K), (0, 0)))
    w3_s = (w_conv_out, w_attn_out, w_out)
    w_full, conv_w_full, meta_full, w3b, w3_land = _gather_weights(w_in[0], w3_s, pad_k(conv_w), meta_tokens)
    w3_pending, token = _w3_start(w3b, w3_land)
    norm_g_fwd = norm_g + token[0:1, 0:1]

    def w3_full(after):
        return _w3_wait(*w3_pending, after)

    early = {}

    def w3_sent(gw3):
        gw3v, gw3vb = (a.reshape(3, N_CHIPS, 2, 128, D) for a in gw3)
        early["sibling"], token = _w3_sibling_start(gw3vb)
        early["mine"] = gw3v
        return token

    def w3_reduced(after):
        r_w3 = _w3_sibling_wait(*early["sibling"], after)
        cs_w3 = _add_sibling_w3(early["mine"], r_w3)
        early["chips"], token = _reduce_chips_start(("w3",), (cs_w3,), "reduce_w3_start")
        return token

    def reduce_start(gwin, gcw, gmeta):
        gwin2, gwin2b = (a.reshape(D, IN_DIM) for a in gwin)
        gcm = jnp.concatenate([gcw.reshape(2, 16, D), gmeta.reshape(2, 8, D)], axis=1)
        r_win, r_cm = _reduce_sibling(gwin2b, gcm)
        cs = _add_sibling(gwin2, gcm, r_win, r_cm)
        pending, token = _reduce_chips_start(("win", "cm"), cs, "reduce_chips_start")
        return token, pending

    gx, _, _, _, _, smalls, pending = _local_step(
        x, loss_target, norm_g_fwd, conv_b, conv_norm_g, conv_norm_b, q_norm_g, k_norm_g,
        w_full, w3_full, conv_w_full, meta_full, (w3_sent, w3_reduced, reduce_start))
    (cs_win, cs_cm), (r2_win, r2_cm) = _reduce_chips_wait(("win", "cm"), pending, gx, "reduce_chips_wait")
    (cs_w3,), (r2_w3,) = _reduce_chips_wait(("w3",), early["chips"], gx, "reduce_w3_wait")
    f_win, f_w3, f_cm = _add_chips(cs_win, cs_w3, cs_cm, r2_win, r2_w3, r2_cm)
    f_win, f_w3, f_cm, r_sm = _share_sibling(f_win, f_w3, f_cm, smalls)

    g_w_in, d_w_in, nm_w_in, nv_w_in = _adamw(w_in[0], f_win.reshape(D, W_IN_SHARD), m_w_in[0], v_w_in[0], 128,
                                              "adamw_w_in")
    w3 = _adamw3(f_w3.reshape(3, ROW_SHARD, D), w3_s, (m_w_conv_out, m_w_attn_out, m_w_out),
                 (v_w_conv_out, v_w_attn_out, v_w_out))
    cm = _adamw_cm(f_cm, (conv_w, meta_tokens), (m_conv_w, m_meta_tokens), (v_conv_w, v_meta_tokens))
    small = _adamw_small(
        r_sm, (norm_g, conv_b, conv_norm_g, conv_norm_b, q_norm_g, k_norm_g),
        (m_norm_g, m_conv_b, m_conv_norm_g, m_conv_norm_b, m_q_norm_g, m_k_norm_g),
        (v_norm_g, v_conv_b, v_conv_norm_g, v_conv_norm_b, v_q_norm_g, v_k_norm_g))

    def assemble(big_in, w3x, cmx, s6):
        ng, cb, cng, cnb, qg, kg = s6
        return (cmx[1], ng, big_in[None], cmx[0], cb, cng, cnb, w3x[0], qg, kg, w3x[1], w3x[2])

    loss = small[0].reshape(())
    grads = assemble(g_w_in, w3[0:3], cm[0:2], small[1:7])
    deltas = assemble(d_w_in, w3[3:6], cm[2:4], small[7:13])
    new_m = assemble(nm_w_in, w3[6:9], cm[4:6], small[13:19])
    new_v = assemble(nv_w_in, w3[9:12], cm[6:8], small[19:25])
    return (loss, gx, *grads, *deltas, *new_m, *new_v)
```

```python
import math

import jax
import jax.numpy as jnp
from jax import lax
from jax.experimental import pallas as pl
from jax.experimental.pallas import tpu as pltpu

F32, BF16 = jnp.float32, jnp.bfloat16
MESH = pl.DeviceIdType.MESH

D = 1024
N_META = 16
CONV_K = 31
N_KV = 4
GQA = 4
HEAD_DIM = 64
GROUP_W = GQA * HEAD_DIM
GRID_W = 64
ROPE_FREQS = 16
ROPE_THETA = 10000.0
EPS = 1e-6
IN_DIM = 7680
KEY_PAD = 128
G_CONV, G_CZ, G_Q, G_KV, G_E = (0, 2048), (2048, 1024), (3072, 1024), (4096, 512), (4608, 3072)
N_CHIPS = 4
W_IN_SHARD = IN_DIM // N_CHIPS
ROW_SHARD = D // N_CHIPS

ADAM_LR, ADAM_B1, ADAM_B2, ADAM_EPS, ADAM_WD, ADAM_STEP = 0.001, 0.9, 0.999, 1e-08, 0.01, 10

NT_DIMS = (((1,), (1,)), ((), ()))


def _params(sem=None, vmem_mb=48):
    return pltpu.CompilerParams(dimension_semantics=sem, vmem_limit_bytes=vmem_mb << 20)


def _sds(shape, dtype):
    return pltpu.HBM(tuple(shape), dtype)


def _pin(*arrays):
    return [pltpu.with_memory_space_constraint(a, pltpu.HBM) for a in arrays]


def _sig(v):
    return jax.nn.sigmoid(v)


def _dsilu(v, s):
    return s * (1.0 + v * (1.0 - s))


def _dot(a, b):
    return jnp.dot(a, b, preferred_element_type=F32)


def _dot_nt(a, b):
    return lax.dot_general(a, b, NT_DIMS, preferred_element_type=F32)


def _qk_mats():
    i = lax.broadcasted_iota(jnp.int32, (GROUP_W, GROUP_W), 0)
    j = lax.broadcasted_iota(jnp.int32, (GROUP_W, GROUP_W), 1)
    mean = jnp.where((i >> 6) == (j >> 6), 1.0 / HEAD_DIM, 0.0).astype(BF16)
    turn = jnp.where((i == j + 16) & ((j & 16) == 0), -1.0,
                     jnp.where((i == j - 16) & ((j & 16) != 0), 1.0, 0.0)).astype(BF16)
    return mean, turn


def _apply(v, mat):
    hi = v.astype(BF16)
    lo = (v - hi.astype(F32)).astype(BF16)
    return _dot(hi, mat) + _dot(lo, mat)


def _qk_fwd(v, g, cos, sin, mats):
    mean, turn = mats
    r = lax.rsqrt(_apply(v * v, mean) + EPS)
    n = v * r * g
    return n * cos + _apply(n, turn) * sin, r


def _qk_bwd(dy, v, r, g, cos, sin, mats):
    mean, turn = mats
    dn = dy * cos - _apply(dy, turn) * sin
    dyg = dn * g
    dv = r * dyg - v * (r * r * r) * _apply(dyg * v, mean)
    return dv, dn * v * r


def _rms_bwd(dxn, v, r, g):
    dxg = dxn * g
    return r * dxg - v * (r * r * r) * jnp.mean(dxg * v, axis=-1, keepdims=True)


def _glu(a):
    return a[:, :D] * _sig(a[:, D:])


def _gather_weights(w_in_s, w3_s, conv_w_s, meta_s):
    def body(win_ref, wa_ref, wb_ref, wc_ref, cw_ref, mt_ref, win_o, cw_o, mt_o, w3b_o, w3_o, win_b, w3_b,
             send, recv, fsend, frecv, lsem, csem):
        x, y, c = _xyc()
        o = 1 - c
        me = 2 * x + y
        win_b[...] = win_ref[...].astype(BF16)
        for i, ref in enumerate((wa_ref, wb_ref, wc_ref)):
            w3_b[i] = ref[0].astype(BF16)
        cast = pltpu.make_async_copy(w3_b, w3b_o, csem.at[0])
        cast.start()
        own = pltpu.make_async_copy(w3_b, _w3_place(w3_o, me), csem.at[1])
        own.start()
        items = (
            (lambda h: win_b.at[pl.ds(h * 512, 512), :],
             lambda p, h: win_o.at[pl.ds(h * 512, 512), pl.ds(p * W_IN_SHARD, W_IN_SHARD)]),
            (lambda h: cw_ref.at[pl.ds(h * 16, 16), :],
             lambda p, h: cw_o.at[pl.ds(h * 16, 16), pl.ds(p * ROW_SHARD, ROW_SHARD)]),
            (lambda h: mt_ref.at[pl.ds(h * 8, 8), :],
             lambda p, h: mt_o.at[pl.ds(h * 8, 8), pl.ds(p * ROW_SHARD, ROW_SHARD)]),
        )
        peers = ((1 - x, y), (x, 1 - y), (1 - x, 1 - y))

        def remote(src, dst, s_sem, r_sem, to):
            return pltpu.make_async_remote_copy(src_ref=src, dst_ref=dst, send_sem=s_sem, recv_sem=r_sem,
                                                device_id=to, device_id_type=MESH)

        started = []
        for a, (half, place) in enumerate(items):
            for h in range(2):
                loc = pltpu.make_async_copy(half(h), place(me, h), lsem.at[a, h])
                loc.start()
                started.append(loc.wait)
            for k, (px, py) in enumerate(peers):
                cp = remote(half(c), place(me, c), send.at[a, k], recv.at[a, k], (px, py, c))
                cp.start()
                started.append(cp.wait_send)
        for k, (px, py) in enumerate(peers):
            for a, (half, place) in enumerate(items):
                got = place(2 * px + py, c)
                remote(got, got, send.at[a, k], recv.at[a, k], (px, py, c)).wait_recv()
                fw = remote(got, got, fsend.at[a, k], frecv.at[a, k], (x, y, o))
                fw.start()
                started.append(fw.wait_send)
        for k, (px, py) in enumerate(peers):
            for a, (half, place) in enumerate(items):
                theirs = place(2 * px + py, o)
                remote(theirs, theirs, fsend.at[a, k], frecv.at[a, k], (x, y, o)).wait_recv()
        for wait in started:
            wait()
        cast.wait()
        own.wait()

    any_spec = pl.BlockSpec(memory_space=pl.ANY)
    vmem = pl.BlockSpec(memory_space=pltpu.VMEM)
    return pl.pallas_call(
        body, name="gather_weights",
        out_shape=(_sds((D, IN_DIM), BF16), _sds((32, D), F32), _sds((N_META, D), F32),
                   _sds((3, ROW_SHARD, D), BF16), _sds((3, D, D), BF16)),
        in_specs=[vmem] * 6,
        out_specs=(any_spec,) * 5,
        scratch_shapes=[pltpu.VMEM((D, W_IN_SHARD), BF16), pltpu.VMEM((3, ROW_SHARD, D), BF16),
                        pltpu.SemaphoreType.DMA((3, 3)), pltpu.SemaphoreType.DMA((3, 3)),
                        pltpu.SemaphoreType.DMA((3, 3)), pltpu.SemaphoreType.DMA((3, 3)),
                        pltpu.SemaphoreType.DMA((3, 2)), pltpu.SemaphoreType.DMA((2,))],
        compiler_params=pltpu.CompilerParams(vmem_limit_bytes=40 << 20),
    )(w_in_s, *w3_s, conv_w_s, meta_s)


def _w3_place(ref, p):
    return ref.at[:, pl.ds(p * ROW_SHARD, ROW_SHARD), :]


def _w3_copies(w3b_ref, land_ref, send, recv):
    x, y, c = _xyc()
    me = 2 * x + y
    peers = ((1 - x, y), (x, 1 - y), (1 - x, 1 - y))
    return [pltpu.make_async_remote_copy(src_ref=w3b_ref, dst_ref=_w3_place(land_ref, me),
                                         send_sem=send.at[k], recv_sem=recv.at[k], device_id=(px, py, c),
                                         device_id_type=MESH)
            for k, (px, py) in enumerate(peers)]


def _w3_start(w3b, land):
    def body(w3b_ref, land_ref, send, recv, w3b_thru, land_thru, token):
        for cp in _w3_copies(w3b_ref, land_ref, send, recv):
            cp.start()
        token[...] = jnp.zeros_like(token)

    outs = pl.pallas_call(
        body, name="w3_start",
        out_shape=(pltpu.SemaphoreType.DMA((3,)), pltpu.SemaphoreType.DMA((3,)),
                   pltpu.HBM(w3b.shape, BF16), pltpu.HBM(land.shape, BF16), jax.ShapeDtypeStruct((8, 128), F32)),
        in_specs=[_HBM, _HBM],
        out_specs=(_SEM, _SEM, _HBM, _HBM, pl.BlockSpec(memory_space=pltpu.VMEM)),
        input_output_aliases={0: 2, 1: 3},
        compiler_params=pltpu.CompilerParams(has_side_effects=_EFFECT),
    )(*_pin(w3b, land))
    return outs[0:4], outs[4]


def _w3_wait(send, recv, w3b, land, after):
    def body(w3b_ref, land_ref, send_ref, recv_ref, after_ref, w3b_out, land_out):
        x, y, c = _xyc()
        peers = ((1 - x, y), (x, 1 - y), (1 - x, 1 - y))
        for k, (cp, (px, py)) in enumerate(zip(_w3_copies(w3b_ref, land_ref, send_ref, recv_ref), peers)):
            cp.wait_send()
            got = _w3_place(land_ref, 2 * px + py)
            pltpu.make_async_remote_copy(src_ref=got, dst_ref=got, send_sem=send_ref.at[k], recv_sem=recv_ref.at[k],
                                         device_id=(px, py, c), device_id_type=MESH).wait_recv()

    outs = pl.pallas_call(
        body, name="w3_wait",
        out_shape=(pltpu.HBM(w3b.shape, BF16), pltpu.HBM(land.shape, BF16)),
        in_specs=[_HBM, _HBM, _SEM, _SEM, pl.BlockSpec(memory_space=pl.ANY)],
        out_specs=(_HBM, _HBM),
        input_output_aliases={0: 0, 1: 1},
        compiler_params=pltpu.CompilerParams(has_side_effects=_EFFECT),
    )(w3b, land, send, recv, after)
    return outs[1]


def _meta_fwd(meta_full, norm_g, w_full):
    def body(m_ref, g_ref, wc_ref, wkv_ref, xnt_ref, pc_ref, pkv_ref):
        v = m_ref[...]
        r = lax.rsqrt(jnp.mean(v * v, axis=-1, keepdims=True) + EPS)
        xn = v * r * g_ref[...]
        xnb = xn.astype(BF16)
        pad = jnp.concatenate([xn, jnp.zeros((128 - N_META, D), F32)], axis=0)
        xnt_ref[...] = pad.T.astype(BF16)
        pc_ref[...] = _dot(xnb, wc_ref[...])
        pkv_ref[...] = _dot(xnb, wkv_ref[...])

    return pl.pallas_call(
        body, name="meta_fwd", grid=(1,),
        out_shape=(_sds((D, 128), BF16), _sds((N_META, 2048), F32),
                   _sds((N_META, 512), F32)),
        in_specs=[pl.BlockSpec((N_META, D), lambda i: (0, 0)), pl.BlockSpec((1, D), lambda i: (0, 0)),
                  pl.BlockSpec((D, 2048), lambda i: (0, 0)), pl.BlockSpec((D, 512), lambda i: (0, G_KV[0] // 512))],
        out_specs=(pl.BlockSpec((D, 128), lambda i: (0, 0)), pl.BlockSpec((N_META, 2048), lambda i: (0, 0)),
                   pl.BlockSpec((N_META, 512), lambda i: (0, 0))),
        compiler_params=_params(("arbitrary",), 32),
    )(*_pin(meta_full, norm_g, w_full, w_full))


def _w_group_loads(w_hbm, w_vmem, sems, groups):
    return [pltpu.make_async_copy(w_hbm.at[:, pl.ds(off, wd)], w_vmem.at[:, pl.ds(off, wd)], sems.at[g])
            for g, (off, wd) in enumerate(groups)]


def _in_proj(x2, norm_g, w_full, qg, cos, sin, nb, tm):
    rows = x2.shape[0]
    nt = rows // nb // tm
    groups = (G_CONV, G_CZ, G_Q, G_KV, G_E)
    scale = 1.0 / math.sqrt(HEAD_DIM)

    def body(x_ref, g_ref, w_hbm, qg_ref, cos_ref, sin_ref, *rest):
        outs, xnt_ref, q2_o, qt_o, w_vmem, sem = rest[:5], rest[5], rest[6], rest[7], rest[8], rest[9]

        first = pl.program_id(0) == 0
        loads = _w_group_loads(w_hbm, w_vmem, sem, groups)

        @pl.when(first)
        def _():
            for cp in loads:
                cp.start()

        v = x_ref[...]
        r = lax.rsqrt(jnp.mean(v * v, axis=-1, keepdims=True) + EPS)
        xn = v * r * g_ref[...]
        xnb = xn.astype(BF16)
        xnt_ref[...] = xn.T.astype(BF16)
        for cp, ref, (off, wd) in zip(loads, outs, groups):
            pl.when(first)(cp.wait)
            for c0 in range(0, wd, 512):
                ref[:, c0:c0 + 512] = _dot(xnb, w_vmem[:, off + c0:off + c0 + 512])
        gv, cosv, sinv = qg_ref[...], cos_ref[...], sin_ref[...]
        mats = _qk_mats()
        for g in range(N_KV):
            gs = slice(GROUP_W * g, GROUP_W * (g + 1))
            qr, _ = _qk_fwd(outs[2][:, gs], gv, cosv, sinv, mats)
            q2_o[:, gs] = (qr * (scale * LOG2E)).astype(BF16)
            qt_o[gs, :] = (qr * scale).T.astype(BF16)

    rope = pl.BlockSpec((tm, GROUP_W), lambda i: (lax.rem(i, nt), 0))
    return pl.pallas_call(
        body, name="in_proj", grid=(rows // tm,),
        out_shape=tuple(_sds((rows, wd), F32) for _, wd in groups)
        + (_sds((D, rows), BF16), _sds((rows, D), BF16), _sds((D, rows), BF16)),
        in_specs=[pl.BlockSpec((tm, D), lambda i: (i, 0)), pl.BlockSpec((1, D), lambda i: (0, 0)),
                  pl.BlockSpec(memory_space=pl.ANY), pl.BlockSpec((1, GROUP_W), lambda i: (0, 0)), rope, rope],
        out_specs=tuple(pl.BlockSpec((tm, wd), lambda i: (i, 0)) for _, wd in groups)
        + (pl.BlockSpec((D, tm), lambda i: (0, i)), pl.BlockSpec((tm, D), lambda i: (i, 0)),
           pl.BlockSpec((D, tm), lambda i: (0, i))),
        scratch_shapes=[pltpu.VMEM((D, IN_DIM), BF16), pltpu.SemaphoreType.DMA((5,))],
        compiler_params=_params(("arbitrary",), 58),
    )(*_pin(x2, norm_g, w_full, qg, cos, sin))


def _halo_specs(width, tm, nt, rows):
    h16 = tm // 16
    return [pl.BlockSpec((tm, width), lambda b, i: (b * nt + i, 0)),
            pl.BlockSpec((16, width), lambda b, i: (jnp.maximum((b * nt + i) * h16 - 1, 0), 0)),
            pl.BlockSpec((16, width), lambda b, i: (jnp.minimum((b * nt + i + 1) * h16, rows // 16 - 1), 0))]


def _fill_uext(uext, cur, prev, nxt, meta, i, nt, tm):
    uext[0:16] = jnp.where(i == 0, _glu(meta[...]), _glu(prev[...]))
    uext[16:16 + tm] = _glu(cur[...])
    uext[16 + tm:32 + tm] = jnp.where(i == nt - 1, 0.0, _glu(nxt[...]))


def _shifted_copies(dst, src, n):
    for r in range(1, 8):
        dst[r, 0:n] = src[r:r + n]


def _rows32(shifted, src, start, cols):
    q8, r = divmod(start, 8)
    if r == 0:
        return src[start:start + 32, cols]
    return shifted[r, 8 * q8:8 * q8 + 32, cols]


def _conv_fwd(pconv, pm_conv, conv_w, conv_b, nb, tm):
    rows = pconv.shape[0]
    nt = rows // nb // tm

    def body(cur, prev, nxt, meta, w_ref, b_ref, o_ref, uext, ush):
        i = pl.program_id(1)
        _fill_uext(uext, cur, prev, nxt, meta, i, nt, tm)
        _shifted_copies(ush, uext, tm + 24)
        for r0 in range(0, tm, 32):
            for c0 in range(0, D, 256):
                acc = jnp.zeros((32, 256), F32) + b_ref[:, c0:c0 + 256]
                for j in range(CONV_K):
                    acc = acc + _rows32(ush, uext, r0 + j + 1, slice(c0, c0 + 256)) * w_ref[j:j + 1, c0:c0 + 256]
                o_ref[r0:r0 + 32, c0:c0 + 256] = acc

    return pl.pallas_call(
        body, name="conv_fwd", grid=(nb, nt),
        out_shape=_sds((rows, D), F32),
        in_specs=_halo_specs(2048, tm, nt, rows)
        + [pl.BlockSpec((16, 2048), lambda b, i: (0, 0)), pl.BlockSpec((32, D), lambda b, i: (0, 0)),
           pl.BlockSpec((1, D), lambda b, i: (0, 0))],
        out_specs=pl.BlockSpec((tm, D), lambda b, i: (b * nt + i, 0)),
        scratch_shapes=[pltpu.VMEM((tm + 32, D), F32), pltpu.VMEM((8, tm + 24, D), F32)],
        compiler_params=_params(("parallel", "parallel"), 48),
    )(*_pin(pconv, pconv, pconv, pm_conv, conv_w, conv_b))


def _kv_prep(pkv, pm_kv, kg, cos, sin, nb):
    rows = pkv.shape[0]
    s_len = rows // nb
    tk = min(512, s_len)
    nt = s_len // tk

    def body(kv_ref, m_ref, g_ref, cos_ref, sin_ref, k_o, v_o, k2_o, v2_o):
        i = pl.program_id(1)
        mats = _qk_mats()
        kv = kv_ref[...]
        kr, _ = _qk_fwd(kv[:, :GROUP_W], g_ref[...], cos_ref[...], sin_ref[...], mats)
        ones = _ones_cols(tk, tk)
        for h in range(N_KV):
            k_o[0, h] = kr[:, HEAD_DIM * h:HEAD_DIM * (h + 1)].astype(BF16)
            vh = kv[:, GROUP_W + HEAD_DIM * h:GROUP_W + HEAD_DIM * (h + 1)]
            v_o[0, h] = jnp.concatenate([vh, ones], axis=1).astype(BF16)

        @pl.when(i == 0)
        def _():
            kvm = m_ref[...]
            km = kvm[:, :GROUP_W]
            kn = km * lax.rsqrt(_apply(km * km, mats[0]) + EPS) * g_ref[...]
            zeros = jnp.zeros((KEY_PAD - N_META, GROUP_W), F32)
            kfull = jnp.concatenate([kn, zeros], axis=0)
            vfull = jnp.concatenate([kvm[:, GROUP_W:], zeros], axis=0)
            ones_m = _ones_cols(KEY_PAD, N_META)
            for h in range(N_KV):
                k2_o[0, h] = kfull[:, HEAD_DIM * h:HEAD_DIM * (h + 1)].astype(BF16)
                v2_o[0, h] = jnp.concatenate([vfull[:, HEAD_DIM * h:HEAD_DIM * (h + 1)], ones_m], axis=1).astype(BF16)

    return pl.pallas_call(
        body, name="kv_prep", grid=(nb, nt),
        out_shape=(_sds((nb, N_KV, s_len, HEAD_DIM), BF16), _sds((nb, N_KV, s_len, 2 * HEAD_DIM), BF16),
                   _sds((nb, N_KV, KEY_PAD, HEAD_DIM), BF16), _sds((nb, N_KV, KEY_PAD, 2 * HEAD_DIM), BF16)),
        in_specs=[pl.BlockSpec((tk, 512), lambda b, i: (b * nt + i, 0)),
                  pl.BlockSpec((N_META, 512), lambda b, i: (0, 0)), pl.BlockSpec((1, GROUP_W), lambda b, i: (0, 0)),
                  pl.BlockSpec((tk, GROUP_W), lambda b, i: (i, 0)),
                  pl.BlockSpec((tk, GROUP_W), lambda b, i: (i, 0))],
        out_specs=(pl.BlockSpec((1, N_KV, tk, HEAD_DIM), lambda b, i: (b, 0, i, 0)),
                   pl.BlockSpec((1, N_KV, tk, 2 * HEAD_DIM), lambda b, i: (b, 0, i, 0)),
                   pl.BlockSpec((1, N_KV, KEY_PAD, HEAD_DIM), lambda b, i: (b, 0, 0, 0)),
                   pl.BlockSpec((1, N_KV, KEY_PAD, 2 * HEAD_DIM), lambda b, i: (b, 0, 0, 0))),
        compiler_params=_params(("parallel", "arbitrary"), 40),
    )(*_pin(pkv, pm_kv, kg, cos, sin))


def _ones_cols(rows, valid):
    r = lax.broadcasted_iota(jnp.int32, (rows, HEAD_DIM), 0)
    col = lax.broadcasted_iota(jnp.int32, (rows, HEAD_DIM), 1)
    return jnp.where((col < 2) & (r < valid), 1.0, 0.0).astype(F32)


def _tail_bias():
    col = lax.broadcasted_iota(jnp.int32, (1, KEY_PAD), 1)
    return jnp.where(col < N_META, 0.0, -1e30).astype(F32)


LOG2E = 1.4426950408889634


def _kv_specs(s_len):
    return [pl.BlockSpec((1, 1, s_len, HEAD_DIM), lambda b, g, i: (b, g, 0, 0)),
            pl.BlockSpec((1, 1, s_len, 2 * HEAD_DIM), lambda b, g, i: (b, g, 0, 0)),
            pl.BlockSpec((1, 1, KEY_PAD, HEAD_DIM), lambda b, g, i: (b, g, 0, 0)),
            pl.BlockSpec((1, 1, KEY_PAD, 2 * HEAD_DIM), lambda b, g, i: (b, g, 0, 0))]


def _attn_fwd(q2, kv4, nb, tq):
    rows = q2.shape[0]
    s_len = rows // nb
    nq = s_len // tq

    def body(q_ref, k1_ref, v1_ref, k2_ref, v2_ref, o_ref, lse_ref):
        qs = q_ref[...]
        k1, k2, v1, v2 = k1_ref[0, 0], k2_ref[0, 0], v1_ref[0, 0], v2_ref[0, 0]
        bias = _tail_bias()
        outs, lses = [], []

        def scores(h):
            qh = qs[:, HEAD_DIM * h:HEAD_DIM * (h + 1)]
            return _dot_nt(qh, k1), _dot_nt(qh, k2) + bias

        ahead = scores(0)
        for h in range(GQA):
            s1, s2 = ahead
            if h + 1 < GQA:
                ahead = scores(h + 1)
            m = jnp.maximum(jnp.max(s1, axis=-1, keepdims=True), jnp.max(s2, axis=-1, keepdims=True))
            oe = _dot(jnp.exp2(s1 - m).astype(BF16), v1) + _dot(jnp.exp2(s2 - m).astype(BF16), v2)
            l = oe[:, HEAD_DIM:HEAD_DIM + 1]
            outs.append(oe[:, :HEAD_DIM] / l)
            lses.append(m + jnp.log2(l))
        o_ref[...] = jnp.concatenate(outs, axis=1)
        lse_ref[0, 0] = jnp.concatenate(lses, axis=1)

    return pl.pallas_call(
        body, name="attn_fwd", grid=(nb, N_KV, nq),
        out_shape=(_sds((rows, D), F32), _sds((nb, N_KV, s_len, GQA), F32)),
        in_specs=[pl.BlockSpec((tq, GROUP_W), lambda b, g, i: (b * nq + i, g))] + _kv_specs(s_len),
        out_specs=(pl.BlockSpec((tq, GROUP_W), lambda b, g, i: (b * nq + i, g)),
                   pl.BlockSpec((1, 1, tq, GQA), lambda b, g, i: (b, g, i, 0))),
        compiler_params=_params(("parallel", "parallel", "parallel"), 48),
    )(*_pin(q2, *kv4))


def _mid(x2, t2, c0, cz, o, e, w3, cn_g, cn_b, tm):
    rows = x2.shape[0]

    def body(x_ref, t_ref, c0_ref, cz_ref, o_ref, e_ref, w_ref, g_ref, b_ref,
             dy_o, mt_o, c3t_o, o2t_o, dyc_o, dya_o, do_o, dc0_o, dcz_o, de_o, sums_o):
        wco, wao, wo = w_ref[0], w_ref[1], w_ref[2]
        cn_g_v = g_ref[...]

        @pl.when(pl.program_id(0) == 0)
        def _():
            sums_o[...] = jnp.zeros_like(sums_o)

        az, gc, ga = e_ref[:, :D], e_ref[:, D:2 * D], e_ref[:, 2 * D:]
        saz = _sig(az)
        gaz = az * saz
        ov = o_ref[...]
        o2 = ov * gaz
        ya = _dot(o2.astype(BF16), wao)
        o2t_o[...] = o2.T.astype(BF16)
        c0v = c0_ref[...]
        xc = c0v - jnp.mean(c0v, axis=-1, keepdims=True)
        rstd = lax.rsqrt(jnp.mean(xc * xc, axis=-1, keepdims=True) + EPS)
        n = xc * rstd
        c1 = n * cn_g_v + b_ref[...]
        s1 = _sig(c1)
        c2 = c1 * s1
        czv = cz_ref[...]
        sz = _sig(czv)
        gz = czv * sz
        c3 = c2 * gz
        yc = _dot(c3.astype(BF16), wco)
        c3t_o[...] = c3.T.astype(BF16)
        sc, sa = _sig(gc), _sig(ga)
        merged = sc * yc + sa * ya
        out = _dot(merged.astype(BF16), wo)
        mt_o[...] = merged.T.astype(BF16)
        err = x_ref[...] + out - t_ref[...]
        dy = err * (1.0 / D)
        dy_o[...] = dy
        dm = _dot_nt(dy.astype(BF16), wo)
        sums_o[3:4, :] += jnp.sum(err * err, axis=0, keepdims=True)
        dyc = dm * sc
        dya = dm * sa
        dycb, dyab = dyc.astype(BF16), dya.astype(BF16)
        dc3 = _dot_nt(dycb, wco)
        do2 = _dot_nt(dyab, wao)
        dyc_o[...] = dycb
        dya_o[...] = dyab
        de_o[:, D:2 * D] = (dyc * yc * (1.0 - sc)).astype(BF16)
        de_o[:, 2 * D:] = (dya * ya * (1.0 - sa)).astype(BF16)
        dcz_o[...] = (dc3 * c2 * _dsilu(czv, sz)).astype(BF16)
        dc1 = dc3 * gz * _dsilu(c1, s1)
        dn = dc1 * cn_g_v
        dc0 = rstd * (dn - jnp.mean(dn, axis=-1, keepdims=True) - n * jnp.mean(dn * n, axis=-1, keepdims=True))
        dc0_o[...] = dc0
        sums_o[0:1, :] += jnp.sum(dc1 * n, axis=0, keepdims=True)
        sums_o[1:2, :] += jnp.sum(dc1, axis=0, keepdims=True)
        sums_o[2:3, :] += jnp.sum(dc0, axis=0, keepdims=True)
        do_o[...] = do2 * gaz
        de_o[:, :D] = (do2 * ov * _dsilu(az, saz)).astype(BF16)

    row = lambda wd: pl.BlockSpec((tm, wd), lambda i: (i, 0))
    col = pl.BlockSpec((D, tm), lambda i: (0, i))
    vec = pl.BlockSpec((1, D), lambda i: (0, 0))
    f32o = lambda wd: _sds((rows, wd), F32)
    b16o = lambda wd: _sds((rows, wd), BF16)
    tpo = _sds((D, rows), BF16)
    return pl.pallas_call(
        body, name="mid", grid=(rows // tm,),
        out_shape=(f32o(D), tpo, tpo, tpo, b16o(D), b16o(D), f32o(D), f32o(D), b16o(D), b16o(3 * D),
                   _sds((8, D), F32)),
        in_specs=[row(D), row(D), row(D), row(D), row(D), row(3 * D),
                  pl.BlockSpec((3, D, D), lambda i: (0, 0, 0)), vec, vec],
        out_specs=(row(D), col, col, col, row(D), row(D), row(D), row(D), row(D), row(3 * D),
                   pl.BlockSpec((8, D), lambda i: (0, 0))),
        compiler_params=_params(("arbitrary",), 60),
    )(*_pin(x2, t2, c0, cz, o, e, w3, cn_g, cn_b))


def _do_prep(d_o, o, tm):
    rows = d_o.shape[0]

    def body(do_ref, o_ref, doe_o, dot_o, ind_ref, spread_ref, place_ref):
        @pl.when(pl.program_id(0) == 0)
        def _():
            def grid(shape):
                return (lax.broadcasted_iota(jnp.int32, shape, 0), lax.broadcasted_iota(jnp.int32, shape, 1))

            r, c = grid((D, 128))
            ind_ref[...] = jnp.where((r >> 6) == c, 1.0, 0.0).astype(BF16)
            r, c = grid((D, 2 * D))
            spread_ref[...] = jnp.where(c == 128 * (r >> 6) + (r & 63), 1.0, 0.0).astype(BF16)
            r, c = grid((2 * 128, 2 * D))
            place_ref[...] = jnp.where(c == 128 * (r & 127) + 64 + (r >> 7), -1.0, 0.0).astype(BF16)

        dov = do_ref[...]
        delta = _apply(dov * o_ref[...], ind_ref[...])
        d_hi = delta.astype(BF16)
        d_lo = (delta - d_hi.astype(F32)).astype(BF16)
        tails = _dot(jnp.concatenate([d_hi, d_lo], axis=1), place_ref[...])
        doe_o[...] = (_dot(dov.astype(BF16), spread_ref[...]) + tails).astype(BF16)
        dot_o[...] = dov.T.astype(BF16)

    row = pl.BlockSpec((tm, D), lambda i: (i, 0))
    return pl.pallas_call(
        body, name="do_prep", grid=(rows // tm,),
        out_shape=(_sds((rows, 2 * D), BF16), _sds((D, rows), BF16)),
        in_specs=[row, row],
        out_specs=(pl.BlockSpec((tm, 2 * D), lambda i: (i, 0)), pl.BlockSpec((D, tm), lambda i: (0, i))),
        scratch_shapes=[pltpu.VMEM((D, 128), BF16), pltpu.VMEM((D, 2 * D), BF16), pltpu.VMEM((2 * 128, 2 * D), BF16)],
        compiler_params=_params(("arbitrary",), 40),
    )(*_pin(d_o, o))


def _q_post(dqr, pq, qg, cos, sin, nb, tm):
    rows = pq.shape[0]
    nt = rows // nb // tm

    def body(dq_ref, q_ref, g_ref, cos_ref, sin_ref, dq_o, dg_o):
        @pl.when((pl.program_id(0) == 0) & (pl.program_id(1) == 0))
        def _():
            dg_o[...] = jnp.zeros_like(dg_o)

        gv, cosv, sinv = g_ref[...], cos_ref[...], sin_ref[...]
        acc = jnp.zeros((1, GROUP_W), F32)
        mats = _qk_mats()
        for g in range(N_KV):
            gs = slice(GROUP_W * g, GROUP_W * (g + 1))
            qv = q_ref[:, gs]
            r = lax.rsqrt(_apply(qv * qv, mats[0]) + EPS)
            dq, dgr = _qk_bwd(dq_ref[:, gs], qv, r, gv, cosv, sinv, mats)
            dq_o[:, gs] = dq.astype(BF16)
            acc = acc + jnp.sum(dgr, axis=0, keepdims=True)
        dg_o[...] += acc

    row = pl.BlockSpec((tm, D), lambda b, i: (b * nt + i, 0))
    rope = pl.BlockSpec((tm, GROUP_W), lambda b, i: (i, 0))
    vec = pl.BlockSpec((1, GROUP_W), lambda b, i: (0, 0))
    return pl.pallas_call(
        body, name="q_post", grid=(nb, nt),
        out_shape=(_sds((rows, D), BF16), _sds((1, GROUP_W), F32)),
        in_specs=[row, row, vec, rope, rope], out_specs=(row, vec),
        compiler_params=_params(("arbitrary", "arbitrary"), 32),
    )(*_pin(dqr, pq, qg, cos, sin))


def _attn_bwd(q2, qst, kv4, doe, dot_, lse, nb, tq):
    rows = q2.shape[0]
    s_len = rows // nb
    nq = s_len // tq
    scale = 1.0 / math.sqrt(HEAD_DIM)

    def body(q_ref, qt_ref, k1_ref, v1_ref, k2_ref, v2_ref, doe_ref, dot_ref, lse_ref,
             dq_o, dkt_o, dvt_o, dkt2_o, dvt2_o):
        i = pl.program_id(2)
        lse = lse_ref[0, 0]
        k1, k2, v1, v2 = k1_ref[0, 0], k2_ref[0, 0], v1_ref[0, 0], v2_ref[0, 0]
        bias = _tail_bias()
        dkt1, dkt2 = jnp.zeros((HEAD_DIM, s_len), F32), jnp.zeros((HEAD_DIM, KEY_PAD), F32)
        dvt1, dvt2 = jnp.zeros((HEAD_DIM, s_len), F32), jnp.zeros((HEAD_DIM, KEY_PAD), F32)

        def products(h):
            qh = q_ref[:, HEAD_DIM * h:HEAD_DIM * (h + 1)]
            dh = doe_ref[:, 2 * HEAD_DIM * h:2 * HEAD_DIM * (h + 1)]
            return _dot_nt(qh, k1), _dot_nt(qh, k2) + bias, _dot_nt(dh, v1), _dot_nt(dh, v2)

        ahead = products(0)
        for h in range(GQA):
            hs = slice(HEAD_DIM * h, HEAD_DIM * (h + 1))
            s1, s2, dp1, dp2 = ahead
            if h + 1 < GQA:
                ahead = products(h + 1)
            lse_h = lse[:, h:h + 1]
            p1 = jnp.exp2(s1 - lse_h)
            p2 = jnp.exp2(s2 - lse_h)
            ds1 = (p1 * dp1).astype(BF16)
            ds2 = (p2 * dp2).astype(BF16)
            dq_o[:, hs] = (_dot(ds1, k1) + _dot(ds2, k2)) * scale
            dkt1 = dkt1 + _dot(qt_ref[hs, :], ds1)
            dkt2 = dkt2 + _dot(qt_ref[hs, :], ds2)
            dvt1 = dvt1 + _dot(dot_ref[hs, :], p1.astype(BF16))
            dvt2 = dvt2 + _dot(dot_ref[hs, :], p2.astype(BF16))

        @pl.when(i == 0)
        def _():
            dkt_o[0, 0], dkt2_o[0, 0], dvt_o[0, 0], dvt2_o[0, 0] = dkt1, dkt2, dvt1, dvt2

        @pl.when(i > 0)
        def _():
            dkt_o[0, 0] += dkt1
            dkt2_o[0, 0] += dkt2
            dvt_o[0, 0] += dvt1
            dvt2_o[0, 0] += dvt2

    qspec = pl.BlockSpec((tq, GROUP_W), lambda b, g, i: (b * nq + i, g))
    qtspec = pl.BlockSpec((GROUP_W, tq), lambda b, g, i: (g, b * nq + i))
    tspec = pl.BlockSpec((1, 1, HEAD_DIM, s_len), lambda b, g, i: (b, g, 0, 0))
    t2spec = pl.BlockSpec((1, 1, HEAD_DIM, KEY_PAD), lambda b, g, i: (b, g, 0, 0))
    tshape = _sds((nb, N_KV, HEAD_DIM, s_len), F32)
    t2shape = _sds((nb, N_KV, HEAD_DIM, KEY_PAD), F32)
    return pl.pallas_call(
        body, name="attn_bwd", grid=(nb, N_KV, nq),
        out_shape=(_sds((rows, D), F32), tshape, tshape, t2shape, t2shape),
        in_specs=[qspec, qtspec] + _kv_specs(s_len)
        + [pl.BlockSpec((tq, 2 * GROUP_W), lambda b, g, i: (b * nq + i, g)), qtspec,
           pl.BlockSpec((1, 1, tq, GQA), lambda b, g, i: (b, g, i, 0))],
        out_specs=(qspec, tspec, tspec, t2spec, t2spec),
        compiler_params=_params(("parallel", "parallel", "arbitrary"), 56),
    )(*_pin(q2, qst, *kv4, doe, dot_, lse))


def _kv_bwd(dkt, dvt, dkt2, dvt2, pkv, pm_kv, kg, cos, sin, nb):
    rows = pkv.shape[0]
    s_len = rows // nb
    tk = min(512, s_len)
    nt = s_len // tk

    def body(dk_ref, dv_ref, dk2_ref, dv2_ref, kv_ref, m_ref, g_ref, cos_ref, sin_ref, d_o, dm_o, dg_o):
        b, i = pl.program_id(0), pl.program_id(1)
        gv = g_ref[...]
        mats = _qk_mats()

        @pl.when((b == 0) & (i == 0))
        def _():
            dg_o[...] = jnp.zeros_like(dg_o)

        kx = kv_ref[:, :GROUP_W]
        r = lax.rsqrt(_apply(kx * kx, mats[0]) + EPS)
        dk, dgr = _qk_bwd(dk_ref[0].T, kx, r, gv, cos_ref[...], sin_ref[...], mats)
        d_o[:, :GROUP_W] = dk.astype(BF16)
        d_o[:, GROUP_W:] = dv_ref[0].T.astype(BF16)
        dg_o[...] += jnp.sum(dgr, axis=0, keepdims=True)

        @pl.when(i == 0)
        def _():
            kxm = m_ref[:, :GROUP_W]
            rm = lax.rsqrt(_apply(kxm * kxm, mats[0]) + EPS)
            dn = dk2_ref[0].T[0:N_META]
            dyg = dn * gv
            dm_o[0, :, :GROUP_W] = rm * dyg - kxm * (rm * rm * rm) * _apply(dyg * kxm, mats[0])
            dm_o[0, :, GROUP_W:] = dv2_ref[0].T[0:N_META]
            dg_o[...] += jnp.sum(dn * kxm * rm, axis=0, keepdims=True)

    tspec = pl.BlockSpec((1, GROUP_W, tk), lambda b, i: (b, 0, i))
    t2spec = pl.BlockSpec((1, GROUP_W, KEY_PAD), lambda b, i: (b, 0, 0))
    rope = pl.BlockSpec((tk, GROUP_W), lambda b, i: (i, 0))
    return pl.pallas_call(
        body, name="kv_bwd", grid=(nb, nt),
        out_shape=(_sds((rows, 512), BF16), _sds((nb, N_META, 512), F32),
                   _sds((1, GROUP_W), F32)),
        in_specs=[tspec, tspec, t2spec, t2spec, pl.BlockSpec((tk, 512), lambda b, i: (b * nt + i, 0)),
                  pl.BlockSpec((N_META, 512), lambda b, i: (0, 0)), pl.BlockSpec((1, GROUP_W), lambda b, i: (0, 0)),
                  rope, rope],
        out_specs=(pl.BlockSpec((tk, 512), lambda b, i: (b * nt + i, 0)),
                   pl.BlockSpec((1, N_META, 512), lambda b, i: (b, 0, 0)),
                   pl.BlockSpec((1, GROUP_W), lambda b, i: (0, 0))),
        compiler_params=_params(("arbitrary", "arbitrary"), 40),
    )(*_pin(dkt, dvt, dkt2, dvt2, pkv, pm_kv, kg, cos, sin))


def _conv_bwd(dc0, pconv, pm_conv, conv_w, nb, tm):
    rows = pconv.shape[0]
    nt = rows // nb // tm

    def body(dcur, dprev, dnxt, cur, meta, w_ref, da_o, dam_o, gw_o, dext, dsh, accw):
        b, i = pl.program_id(0), pl.program_id(1)
        dext[0:16] = jnp.zeros((16, D), F32)
        dext[16:32] = jnp.where(i == 0, 0.0, dprev[...])
        dext[32:32 + tm] = dcur[...]
        dext[32 + tm:48 + tm] = jnp.where(i == nt - 1, 0.0, dnxt[...])
        _shifted_copies(dsh, dext, tm + 40)
        accw[...] = jnp.zeros_like(accw)

        @pl.when((b == 0) & (i == 0))
        def _():
            gw_o[...] = jnp.zeros_like(gw_o)

        for c0 in range(0, D, 256):
            cs = slice(c0, c0 + 256)
            for r0 in range(0, tm, 32):
                cv = cur[r0:r0 + 32, c0:c0 + 256]
                sg = _sig(cur[r0:r0 + 32, D + c0:D + c0 + 256])
                u = cv * sg
                acc = jnp.zeros((32, 256), F32)
                for j in range(CONV_K):
                    d = _rows32(dsh, dext, r0 + 47 - j, cs)
                    acc = acc + d * w_ref[j:j + 1, cs]
                    p = d * u
                    accw[8 * j:8 * j + 8, cs] += (p[0:8] + p[8:16]) + (p[16:24] + p[24:32])
                da_o[r0:r0 + 32, cs] = (acc * sg).astype(BF16)
                da_o[r0:r0 + 32, D + c0:D + c0 + 256] = (acc * cv * sg * (1.0 - sg)).astype(BF16)
        for j in range(CONV_K):
            gw_o[j:j + 1, :] += jnp.sum(accw[8 * j:8 * j + 8, :], axis=0, keepdims=True)

        @pl.when(i == 0)
        def _():
            for c0 in range(0, D, 256):
                cs = slice(c0, c0 + 256)
                cv = meta[:, c0:c0 + 256]
                sg = _sig(meta[:, D + c0:D + c0 + 256])
                um = cv * sg
                acc = jnp.zeros((16, 256), F32)
                for j in range(CONV_K):
                    d = dext[31 - j:47 - j, cs]
                    acc = acc + d * w_ref[j:j + 1, cs]
                    gw_o[j:j + 1, cs] += jnp.sum(d * um, axis=0, keepdims=True)
                dam_o[0, :, cs] = acc * sg
                dam_o[0, :, D + c0:D + c0 + 256] = acc * cv * sg * (1.0 - sg)

    return pl.pallas_call(
        body, name="conv_bwd", grid=(nb, nt),
        out_shape=(_sds((rows, 2048), BF16), _sds((nb, N_META, 2048), F32),
                   _sds((32, D), F32)),
        in_specs=_halo_specs(D, tm, nt, rows)
        + [pl.BlockSpec((tm, 2048), lambda b, i: (b * nt + i, 0)),
           pl.BlockSpec((16, 2048), lambda b, i: (0, 0)), pl.BlockSpec((32, D), lambda b, i: (0, 0))],
        out_specs=(pl.BlockSpec((tm, 2048), lambda b, i: (b * nt + i, 0)),
                   pl.BlockSpec((1, N_META, 2048), lambda b, i: (b, 0, 0)),
                   pl.BlockSpec((32, D), lambda b, i: (0, 0))),
        scratch_shapes=[pltpu.VMEM((tm + 48, D), F32), pltpu.VMEM((8, tm + 40, D), F32),
                        pltpu.VMEM((8 * CONV_K, D), F32)],
        compiler_params=_params(("arbitrary", "arbitrary"), 48),
    )(*_pin(dc0, dc0, dc0, pconv, pm_conv, conv_w))


def _meta_bwd(dam, ddm, w_full, meta_full, norm_g):
    nb = dam.shape[0]

    def body(a_ref, d_ref, wc_ref, wkv_ref, m_ref, g_ref, gm_o, dg_o):
        a, d = a_ref[0], d_ref[0]
        for b in range(1, nb):
            a = a + a_ref[b]
            d = d + d_ref[b]
        dxn = _dot_nt(a.astype(BF16), wc_ref[...]) + _dot_nt(d.astype(BF16), wkv_ref[...])
        v = m_ref[...]
        r = lax.rsqrt(jnp.mean(v * v, axis=-1, keepdims=True) + EPS)
        gm_o[...] = _rms_bwd(dxn, v, r, g_ref[...])
        dg_o[...] = jnp.sum(dxn * v * r, axis=0, keepdims=True)

    return pl.pallas_call(
        body, name="meta_bwd", grid=(1,),
        out_shape=(_sds((N_META, D), F32), _sds((1, D), F32)),
        in_specs=[pl.BlockSpec((nb, N_META, 2048), lambda i: (0, 0, 0)), pl.BlockSpec((nb, N_META, 512), lambda i: (0, 0, 0)),
                  pl.BlockSpec((D, 2048), lambda i: (0, 0)), pl.BlockSpec((D, 512), lambda i: (0, G_KV[0] // 512)),
                  pl.BlockSpec((N_META, D), lambda i: (0, 0)), pl.BlockSpec((1, D), lambda i: (0, 0))],
        out_specs=(pl.BlockSpec((N_META, D), lambda i: (0, 0)), pl.BlockSpec((1, D), lambda i: (0, 0))),
        compiler_params=_params(("arbitrary",), 32),
    )(*_pin(dam, ddm, w_full, w_full, meta_full, norm_g))


def _dxn(d_groups, w_full, x2, dy, norm_g, dg_init, tm):
    rows = x2.shape[0]
    groups = (G_CONV, G_CZ, G_Q, G_KV, G_E)

    def body(da, db, dq, dd, de, w_hbm, x_ref, dy_ref, g_ref, gi_ref, gx_o, dg_o, w_vmem, sem):
        first = pl.program_id(0) == 0
        loads = _w_group_loads(w_hbm, w_vmem, sem, groups)

        @pl.when(first)
        def _():
            for cp in loads:
                cp.start()
            dg_o[...] = gi_ref[...]

        dxn = jnp.zeros((tm, D), F32)
        for cp, ref, (off, wd) in zip(loads, (da, db, dq, dd, de), groups):
            pl.when(first)(cp.wait)
            for c0 in range(0, wd, 512):
                dxn = dxn + _dot_nt(ref[:, c0:c0 + 512], w_vmem[:, off + c0:off + c0 + 512])
        v = x_ref[...]
        r = lax.rsqrt(jnp.mean(v * v, axis=-1, keepdims=True) + EPS)
        gx_o[...] = dy_ref[...] + _rms_bwd(dxn, v, r, g_ref[...])
        dg_o[...] += jnp.sum(dxn * v * r, axis=0, keepdims=True)

    row = lambda wd: pl.BlockSpec((tm, wd), lambda i: (i, 0))
    vec = pl.BlockSpec((1, D), lambda i: (0, 0))
    return pl.pallas_call(
        body, name="dxn", grid=(rows // tm,),
        out_shape=(_sds((rows, D), F32), _sds((1, D), F32)),
        in_specs=[row(wd) for _, wd in groups] + [pl.BlockSpec(memory_space=pl.ANY), row(D), row(D), vec, vec],
        out_specs=(row(D), vec),
        scratch_shapes=[pltpu.VMEM((D, IN_DIM), BF16), pltpu.SemaphoreType.DMA((5,))],
        compiler_params=_params(("arbitrary",), 56),
    )(*_pin(*d_groups, w_full, x2, dy, norm_g, dg_init))


def _wgrad(at, b, bufs, slot, col_off, name, meta=None):
    buf, bufb = bufs
    rows, n = b.shape
    tn = next(t for t in (1536, 1024, 512) if n % t == 0 and col_off % t == 0)
    tk = min(2048, rows)
    nk = rows // tk
    j0 = col_off // tn

    def body(*refs):
        if meta is None:
            at_ref, b_ref, _, _, o_ref, ob_ref = refs
        else:
            at_ref, b_ref, xm_ref, dm_ref, _, _, o_ref, ob_ref = refs
        k = pl.program_id(1)

        @pl.when(k == 0)
        def _():
            if meta is None:
                o_ref[0] = jnp.zeros((D, tn), F32)
            else:
                dm = dm_ref[0]
                for e in range(1, dm_ref.shape[0]):
                    dm = dm + dm_ref[e]
                dm = jnp.concatenate([dm, jnp.zeros((128 - N_META, tn), F32)], axis=0)
                o_ref[0] = _dot(xm_ref[...], dm.astype(BF16))

        o_ref[0] += _dot(at_ref[...], b_ref[...].astype(BF16))

        @pl.when(k == nk - 1)
        def _():
            ob_ref[0] = o_ref[0].astype(BF16)

    in_specs = [pl.BlockSpec((D, tk), lambda j, k: (0, k)), pl.BlockSpec((tk, tn), lambda j, k: (k, j))]
    args = [at, b]
    if meta is not None:
        xmt, dm = meta
        in_specs += [pl.BlockSpec((D, 128), lambda j, k: (0, 0)),
                     pl.BlockSpec((dm.shape[0], N_META, tn), lambda j, k: (0, 0, j))]
        args += [xmt, dm]
    in_specs += [pl.BlockSpec(memory_space=pl.ANY)] * 2
    args += [buf, bufb]
    blk = pl.BlockSpec((1, D, tn), lambda j, k: (slot, 0, j0 + j))
    return pl.pallas_call(
        body, name=name, grid=(n // tn, nk),
        out_shape=(_sds(buf.shape, F32), _sds(buf.shape, BF16)),
        in_specs=in_specs,
        out_specs=(blk, blk),
        input_output_aliases={len(args) - 2: 0, len(args) - 1: 1},
        compiler_params=_params(("parallel", "arbitrary"), 56),
    )(*_pin(*args))


def _rope_tables(s_len):
    pos = jnp.arange(s_len, dtype=jnp.int32)
    row_ids = (pos // GRID_W).astype(F32)
    col_ids = (pos % GRID_W).astype(F32)
    inv_freq = ROPE_THETA ** (-jnp.arange(ROPE_FREQS, dtype=F32) / ROPE_FREQS)
    a_row = row_ids[:, None] * inv_freq[None, :]
    a_col = col_ids[:, None] * inv_freq[None, :]
    ang = jnp.concatenate([a_row, a_row, a_col, a_col], axis=-1)
    return jnp.tile(jnp.cos(ang), (1, GQA)), jnp.tile(jnp.sin(ang), (1, GQA))


def _local_step(x, loss_target, norm_g, conv_b, cn_g, cn_b, q_g, k_g, w_full, w3_full, conv_w_full, meta_full,
                hooks=None):
    nb, s_len, _ = x.shape
    rows = nb * s_len
    x2 = x.reshape(rows, D)
    t2 = loss_target.reshape(rows, D)
    cos, sin = _rope_tables(s_len)
    qg = jnp.tile(q_g, (1, GQA))
    kg = jnp.tile(k_g, (1, N_KV))

    xnmt, pm_conv, pm_kv = _meta_fwd(meta_full, norm_g, w_full)
    pconv, pcz, pq, pkv, pe, xnt, q2, qst = _in_proj(x2, norm_g, w_full, qg, cos, sin, nb, 256)
    c0 = _conv_fwd(pconv, pm_conv, conv_w_full, conv_b, nb, min(512, s_len))
    tq = min(512, s_len)
    kv4 = _kv_prep(pkv, pm_kv, kg, cos, sin, nb)
    o, lse = _attn_fwd(q2, kv4, nb, min(1024, s_len))
    if callable(w3_full):
        w3_full = w3_full(o)
    dy, mt, c3t, o2t, dyc, dya, d_o, dc0, dcz, de, sums = _mid(x2, t2, c0, pcz, o, pe, w3_full, cn_g, cn_b, 256)
    gw3 = (lax.empty((3, D, D), F32), lax.empty((3, D, D), BF16))
    gw3 = _wgrad(c3t, dyc, gw3, 0, 0, "wgrad_conv_out")
    gw3 = _wgrad(o2t, dya, gw3, 1, 0, "wgrad_attn_out")
    gw3 = _wgrad(mt, dy, gw3, 2, 0, "wgrad_out")
    if hooks is not None:
        lse = lse + hooks[0](gw3)[0, 0]
    doe, dot_ = _do_prep(d_o, o, min(512, s_len))
    dqr, dkt, dvt, dkt2, dvt2 = _attn_bwd(q2, qst, kv4, doe, dot_, lse, nb, min(1024, s_len))
    qg_post = qg if hooks is None else qg + hooks[1](dqr)[0:1, 0:1]
    dq, dqg = _q_post(dqr, pq, qg_post, cos, sin, nb, min(512, s_len))
    dd, ddm, dkg = _kv_bwd(dkt.reshape(nb, GROUP_W, s_len), dvt.reshape(nb, GROUP_W, s_len),
                           dkt2.reshape(nb, GROUP_W, KEY_PAD), dvt2.reshape(nb, GROUP_W, KEY_PAD),
                           pkv, pm_kv, kg, cos, sin, nb)
    da, dam, gcw = _conv_bwd(dc0, pconv, pm_conv, conv_w_full, nb, min(512, s_len))
    gmeta, dng_m = _meta_bwd(dam, ddm, w_full, meta_full, norm_g)

    gwin = (lax.empty((1, D, IN_DIM), F32), lax.empty((1, D, IN_DIM), BF16))
    gwin = _wgrad(xnt, da, gwin, 0, G_CONV[0], "wgrad_in_conv", meta=(xnmt, dam))
    gwin = _wgrad(xnt, dcz, gwin, 0, G_CZ[0], "wgrad_in_cz")
    gwin = _wgrad(xnt, dq, gwin, 0, G_Q[0], "wgrad_in_q")
    gwin = _wgrad(xnt, dd, gwin, 0, G_KV[0], "wgrad_in_kv", meta=(xnmt, ddm))
    gwin = _wgrad(xnt, de, gwin, 0, G_E[0], "wgrad_in_e")

    pending = None
    if hooks is not None:
        token, pending = hooks[2](gwin, gcw, gmeta)
        dng_m = dng_m + token[0:1, 0:1]
    gx, dng = _dxn((da, dcz, dq, dd, de), w_full, x2, dy, norm_g, dng_m, 512)

    zeros = jnp.zeros((1, D - 2 * GROUP_W), F32)
    smalls = jnp.concatenate([dng, sums[2:3], sums[0:1], sums[1:2], jnp.concatenate([dqg, dkg, zeros], axis=1),
                              sums[3:4], jnp.zeros((2, D), F32)], axis=0)
    return gx.reshape(nb, s_len, D), gwin, gw3, gcw, gmeta, smalls, pending


def _xyc():
    return lax.axis_index("x"), lax.axis_index("y"), lax.axis_index("c")


_HBM = pl.BlockSpec(memory_space=pltpu.HBM)
_SEM = pl.BlockSpec(memory_space=pltpu.SEMAPHORE)
_EFFECT = pltpu.SideEffectType.DATAFLOW_SIDE_EFFECTING


def _reduce_sibling(gwin, gcm):
    def body(gwin_ref, gcm_ref, r_win, r_cm, send, recv):
        x, y, c = _xyc()
        o = 1 - c
        half = D // 2
        outs = ((gwin_ref.at[pl.ds(o * half, half), :], r_win), (gcm_ref.at[o], r_cm))
        cps = []
        for a, (src, dst) in enumerate(outs):
            cp = pltpu.make_async_remote_copy(src_ref=src, dst_ref=dst, send_sem=send.at[a], recv_sem=recv.at[a],
                                              device_id=(x, y, o), device_id_type=MESH)
            cp.start()
            cps.append(cp)
        for cp in cps:
            cp.wait()

    any_spec = pl.BlockSpec(memory_space=pl.ANY)
    return pl.pallas_call(
        body, name="reduce_sibling",
        out_shape=(_sds((D // 2, IN_DIM), BF16), _sds((24, D), F32)),
        in_specs=[any_spec] * 2, out_specs=(any_spec,) * 2,
        scratch_shapes=[pltpu.SemaphoreType.DMA((2,)), pltpu.SemaphoreType.DMA((2,))],
    )(*_pin(gwin, gcm))


def _w3_sibling_copy(gw3_ref, land_ref, send, recv):
    x, y, c = _xyc()
    return pltpu.make_async_remote_copy(src_ref=gw3_ref.at[:, :, 1 - c], dst_ref=land_ref, send_sem=send.at[0],
                                        recv_sem=recv.at[0], device_id=(x, y, 1 - c), device_id_type=MESH)


def _w3_sibling_start(gw3vb):
    land = lax.empty((3, N_CHIPS, 128, D), BF16)

    def body(src_ref, land_ref, send, recv, src_thru, land_thru, token):
        _w3_sibling_copy(src_ref, land_ref, send, recv).start()
        token[...] = jnp.zeros_like(token)

    outs = pl.pallas_call(
        body, name="w3_sibling_start",
        out_shape=(pltpu.SemaphoreType.DMA((1,)), pltpu.SemaphoreType.DMA((1,)),
                   pltpu.HBM(gw3vb.shape, BF16), pltpu.HBM(land.shape, BF16), jax.ShapeDtypeStruct((8, 128), F32)),
        in_specs=[_HBM, _HBM],
        out_specs=(_SEM, _SEM, _HBM, _HBM, pl.BlockSpec(memory_space=pltpu.VMEM)),
        input_output_aliases={0: 2, 1: 3},
        compiler_params=pltpu.CompilerParams(has_side_effects=_EFFECT),
    )(*_pin(gw3vb, land))
    return outs[0:4], outs[4]


def _w3_sibling_wait(send, recv, src, land, after):
    def body(src_ref, land_ref, send_ref, recv_ref, after_ref, src_out, land_out):
        cp = _w3_sibling_copy(src_ref, land_ref, send_ref, recv_ref)
        cp.wait_send()
        cp.wait_recv()

    outs = pl.pallas_call(
        body, name="w3_sibling_wait",
        out_shape=(pltpu.HBM(src.shape, BF16), pltpu.HBM(land.shape, BF16)),
        in_specs=[_HBM, _HBM, _SEM, _SEM, pl.BlockSpec(memory_space=pl.ANY)],
        out_specs=(_HBM, _HBM),
        input_output_aliases={0: 0, 1: 1},
        compiler_params=pltpu.CompilerParams(has_side_effects=_EFFECT),
    )(src, land, send, recv, after)
    return outs[1]


def _add_sibling_w3(gw3v, r_w3):
    c = lax.axis_index("c").astype(jnp.int32).reshape(1)

    def body(c_ref, a_ref, b_ref, o_ref):
        o_ref[0, 0] = (a_ref[0, 0, 0] + b_ref[0, 0].astype(F32)).astype(BF16)

    return pl.pallas_call(
        body, name="add_sibling_w3", out_shape=_sds((3, 4, 128, D), BF16),
        grid_spec=pltpu.PrefetchScalarGridSpec(
            num_scalar_prefetch=1, grid=(3, 4),
            in_specs=[pl.BlockSpec((1, 1, 1, 128, D), lambda w, s, c_ref: (w, s, c_ref[0], 0, 0)),
                      pl.BlockSpec((1, 1, 128, D), lambda w, s, c_ref: (w, s, 0, 0))],
            out_specs=pl.BlockSpec((1, 1, 128, D), lambda w, s, c_ref: (w, s, 0, 0))),
        compiler_params=_params(("parallel", "parallel"), 32),
    )(c, *_pin(gw3v, r_w3))


def _add_sibling(gwin, gcm, r_win, r_cm):
    c = lax.axis_index("c").astype(jnp.int32).reshape(1)
    half = D // 2
    tr = 64

    def body1(c_ref, a_ref, b_ref, o_ref):
        o_ref[...] = (a_ref[...] + b_ref[...].astype(F32)).astype(BF16)

    cs_win = pl.pallas_call(
        body1, name="add_sibling_w_in", out_shape=_sds((half, IN_DIM), BF16),
        grid_spec=pltpu.PrefetchScalarGridSpec(
            num_scalar_prefetch=1, grid=(half // tr,),
            in_specs=[pl.BlockSpec((tr, IN_DIM), lambda i, c_ref: (c_ref[0] * (half // tr) + i, 0)),
                      pl.BlockSpec((tr, IN_DIM), lambda i, c_ref: (i, 0))],
            out_specs=pl.BlockSpec((tr, IN_DIM), lambda i, c_ref: (i, 0))),
        compiler_params=_params(("parallel",), 32),
    )(c, *_pin(gwin, r_win))

    def body3(c_ref, a_ref, b_ref, o_ref):
        o_ref[...] = a_ref[0] + b_ref[...]

    cs_cm = pl.pallas_call(
        body3, name="add_sibling_cm", out_shape=_sds((24, D), F32),
        grid_spec=pltpu.PrefetchScalarGridSpec(
            num_scalar_prefetch=1, grid=(1,),
            in_specs=[pl.BlockSpec((1, 24, D), lambda i, c_ref: (c_ref[0], 0, 0)),
                      pl.BlockSpec((24, D), lambda i, c_ref: (0, 0))],
            out_specs=pl.BlockSpec((24, D), lambda i, c_ref: (0, 0))),
        compiler_params=_params(("arbitrary",), 32),
    )(c, *_pin(gcm, r_cm))
    return cs_win, cs_cm


_CHIP_PARTS = {
    "win": (lambda ref, p: ref.at[:, pl.ds(p * W_IN_SHARD, W_IN_SHARD)], (D // 2, W_IN_SHARD), BF16),
    "w3": (lambda ref, p: ref.at[:, p], (3, 128, D), BF16),
    "cm": (lambda ref, p: ref.at[:, pl.ds(p * ROW_SHARD, ROW_SHARD)], (24, ROW_SHARD), F32),
}


def _reduce_chips_copies(kinds, srcs, lands, send, recv):
    x, y, c = _xyc()
    peers = ((1 - x, y), (x, 1 - y), (1 - x, 1 - y))
    cps = []
    for k, (px, py) in enumerate(peers):
        for a, (kind, src, land) in enumerate(zip(kinds, srcs, lands)):
            cps.append(pltpu.make_async_remote_copy(
                src_ref=_CHIP_PARTS[kind][0](src, 2 * px + py), dst_ref=land.at[k], send_sem=send.at[3 * a + k],
                recv_sem=recv.at[3 * a + k], device_id=(px, py, c), device_id_type=MESH))
    return cps


def _reduce_chips_start(kinds, srcs, name):
    n = len(kinds)
    lands = tuple(lax.empty((3,) + _CHIP_PARTS[kind][1], _CHIP_PARTS[kind][2]) for kind in kinds)

    def body(*refs):
        srcs_in, lands_in, send, recv, token = refs[0:n], refs[n:2 * n], refs[2 * n], refs[2 * n + 1], refs[-1]
        for cp in _reduce_chips_copies(kinds, srcs_in, lands_in, send, recv):
            cp.start()
        token[...] = jnp.zeros_like(token)

    hbm = lambda a: pltpu.HBM(a.shape, a.dtype)
    outs = pl.pallas_call(
        body, name=name,
        out_shape=(pltpu.SemaphoreType.DMA((3 * n,)), pltpu.SemaphoreType.DMA((3 * n,)),
                   *[hbm(a) for a in srcs], *[hbm(a) for a in lands], jax.ShapeDtypeStruct((8, 128), F32)),
        in_specs=[_HBM] * (2 * n),
        out_specs=(_SEM, _SEM, *[_HBM] * (2 * n), pl.BlockSpec(memory_space=pltpu.VMEM)),
        input_output_aliases={i: i + 2 for i in range(2 * n)},
        compiler_params=pltpu.CompilerParams(has_side_effects=_EFFECT),
    )(*_pin(*srcs, *lands))
    return (outs[0], outs[1], outs[2:2 + n], outs[2 + n:2 + 2 * n]), outs[-1]


def _reduce_chips_wait(kinds, pending, after, name):
    send, recv, srcs, lands = pending
    n = len(kinds)

    def body(*refs):
        srcs_in, lands_in, send_ref, recv_ref = refs[0:n], refs[n:2 * n], refs[2 * n], refs[2 * n + 1]
        for cp in _reduce_chips_copies(kinds, srcs_in, lands_in, send_ref, recv_ref):
            cp.wait_send()
            cp.wait_recv()

    hbm = lambda a: pltpu.HBM(a.shape, a.dtype)
    outs = pl.pallas_call(
        body, name=name,
        out_shape=(*[hbm(a) for a in srcs], *[hbm(a) for a in lands]),
        in_specs=[_HBM] * (2 * n) + [_SEM, _SEM, pl.BlockSpec(memory_space=pl.ANY)],
        out_specs=(_HBM,) * (2 * n),
        input_output_aliases={i: i for i in range(2 * n)},
        compiler_params=pltpu.CompilerParams(has_side_effects=_EFFECT),
    )(*srcs, *lands, send, recv, after)
    return outs[0:n], outs[n:2 * n]


def _add_chips(cs_win, cs_w3, cs_cm, r_win, r_w3, r_cm):
    x, y, c = _xyc()
    idx = jnp.stack([2 * x + y, c]).astype(jnp.int32)
    half = D // 2
    tr = 128

    def body1(i_ref, a_ref, b_ref, o_ref):
        f = lambda v: v.astype(F32)
        o_ref[0] = (f(a_ref[...]) + f(b_ref[2])) + (f(b_ref[0]) + f(b_ref[1]))

    f_win = pl.pallas_call(
        body1, name="add_chips_w_in", out_shape=_sds((2, half, W_IN_SHARD), F32),
        grid_spec=pltpu.PrefetchScalarGridSpec(
            num_scalar_prefetch=1, grid=(half // tr,),
            in_specs=[pl.BlockSpec((tr, W_IN_SHARD), lambda i, r: (i, r[0])),
                      pl.BlockSpec((3, tr, W_IN_SHARD), lambda i, r: (0, i, 0))],
            out_specs=pl.BlockSpec((1, tr, W_IN_SHARD), lambda i, r: (r[1], i, 0))),
        compiler_params=_params(("parallel",), 32),
    )(idx, *_pin(cs_win, r_win))

    def body2(i_ref, a_ref, b_ref, o_ref):
        f = lambda v: v.astype(F32)
        o_ref[0, 0] = (f(a_ref[0, 0]) + f(b_ref[2, 0])) + (f(b_ref[0, 0]) + f(b_ref[1, 0]))

    f_w3 = pl.pallas_call(
        body2, name="add_chips_w3", out_shape=_sds((3, 2, 128, D), F32),
        grid_spec=pltpu.PrefetchScalarGridSpec(
            num_scalar_prefetch=1, grid=(3,),
            in_specs=[pl.BlockSpec((1, 1, 128, D), lambda w, r: (w, r[0], 0, 0)),
                      pl.BlockSpec((3, 1, 128, D), lambda w, r: (0, w, 0, 0))],
            out_specs=pl.BlockSpec((1, 1, 128, D), lambda w, r: (w, r[1], 0, 0))),
        compiler_params=_params(("parallel",), 32),
    )(idx, *_pin(cs_w3, r_w3))

    def body3(i_ref, a_ref, b_ref, o_ref):
        o_ref[0] = (a_ref[...] + b_ref[2]) + (b_ref[0] + b_ref[1])

    f_cm = pl.pallas_call(
        body3, name="add_chips_cm", out_shape=_sds((2, 24, ROW_SHARD), F32),
        grid_spec=pltpu.PrefetchScalarGridSpec(
            num_scalar_prefetch=1, grid=(1,),
            in_specs=[pl.BlockSpec((24, ROW_SHARD), lambda i, r: (0, r[0])),
                      pl.BlockSpec((3, 24, ROW_SHARD), lambda i, r: (0, 0, 0))],
            out_specs=pl.BlockSpec((1, 24, ROW_SHARD), lambda i, r: (r[1], 0, 0))),
        compiler_params=_params(("arbitrary",), 32),
    )(idx, *_pin(cs_cm, r_cm))
    return f_win, f_w3, f_cm


def _share_sibling(f_win, f_w3, f_cm, smalls):
    def body(win_in, w3_in, cm_in, sm_ref, win_ref, w3_ref, cm_ref, r_sm, send, recv, ssend, srecv, lsem):
        x, y, c = _xyc()
        o = 1 - c
        cps = []
        for a, (ref, sl) in enumerate(((win_ref, lambda h: win_ref.at[h]), (w3_ref, lambda h: w3_ref.at[:, h]),
                                       (cm_ref, lambda h: cm_ref.at[h]))):
            cp = pltpu.make_async_remote_copy(src_ref=sl(c), dst_ref=sl(c), send_sem=send.at[a], recv_sem=recv.at[a],
                                              device_id=(x, y, o), device_id_type=MESH)
            cp.start()
            cps.append((cp, sl))
        me = 4 * x + 2 * y + c
        loc = pltpu.make_async_copy(sm_ref, r_sm.at[me], lsem)
        loc.start()
        scps = []
        for d in range(1, 8):
            px, py, pc = (x + (d >> 2)) % 2, (y + ((d >> 1) & 1)) % 2, (c + (d & 1)) % 2
            cp = pltpu.make_async_remote_copy(src_ref=sm_ref, dst_ref=r_sm.at[me], send_sem=ssend.at[d - 1],
                                              recv_sem=srecv.at[d - 1], device_id=(px, py, pc), device_id_type=MESH)
            cp.start()
            scps.append((cp, 4 * px + 2 * py + pc))
        for a, (cp, sl) in enumerate(cps):
            pltpu.make_async_remote_copy(src_ref=sl(o), dst_ref=sl(o), send_sem=send.at[a], recv_sem=recv.at[a],
                                         device_id=(x, y, o), device_id_type=MESH).wait_recv()
            cp.wait_send()
        for d, (cp, pid) in enumerate(scps):
            pltpu.make_async_remote_copy(src_ref=sm_ref, dst_ref=r_sm.at[pid], send_sem=ssend.at[d],
                                         recv_sem=srecv.at[d], device_id=(x, y, c), device_id_type=MESH).wait_recv()
            cp.wait_send()
        loc.wait()

    any_spec = pl.BlockSpec(memory_space=pl.ANY)
    return pl.pallas_call(
        body, name="share_sibling",
        out_shape=(_sds(f_win.shape, F32), _sds(f_w3.shape, F32), _sds(f_cm.shape, F32), _sds((8, 8, D), F32)),
        in_specs=[any_spec] * 4, out_specs=(any_spec,) * 4,
        input_output_aliases={0: 0, 1: 1, 2: 2},
        scratch_shapes=[pltpu.SemaphoreType.DMA((3,)), pltpu.SemaphoreType.DMA((3,)),
                        pltpu.SemaphoreType.DMA((7,)), pltpu.SemaphoreType.DMA((7,)), pltpu.SemaphoreType.DMA],
    )(*_pin(f_win, f_w3, f_cm, smalls))


def _adamw_math(w, g, m, v):
    m = ADAM_B1 * m + (1.0 - ADAM_B1) * g
    v = ADAM_B2 * v + (1.0 - ADAM_B2) * (g * g)
    m_hat = m / (1.0 - ADAM_B1 ** ADAM_STEP)
    v_hat = v / (1.0 - ADAM_B2 ** ADAM_STEP)
    delta = -ADAM_LR * (m_hat / (jnp.sqrt(v_hat) + ADAM_EPS) + ADAM_WD * w)
    return delta, m, v


def _adamw(w, g, m, v, tr, name):
    rows, cols = w.shape

    def body(w_ref, g_ref, m_ref, v_ref, g_o, d_o, m_o, v_o):
        g = g_ref[...]
        g_o[...] = g
        d_o[...], m_o[...], v_o[...] = _adamw_math(w_ref[...], g, m_ref[...], v_ref[...])

    spec = pl.BlockSpec((tr, cols), lambda i: (i, 0))
    return pl.pallas_call(
        body, name=name, grid=(rows // tr,),
        out_shape=(_sds((rows, cols), F32),) * 4,
        in_specs=[spec] * 4, out_specs=(spec,) * 4,
        compiler_params=_params(("parallel",), 32),
    )(*_pin(w, g, m, v))


def _adamw3(g3, ws, ms, vs):
    def body(g_ref, *refs):
        w_refs, m_refs, v_refs, outs = refs[0:3], refs[3:6], refs[6:9], refs[9:]
        g_os, d_os, m_os, v_os = outs[0:3], outs[3:6], outs[6:9], outs[9:12]
        for i in range(3):
            g = g_ref[i]
            g_os[i][0] = g
            d_os[i][0], m_os[i][0], v_os[i][0] = _adamw_math(w_refs[i][0], g, m_refs[i][0], v_refs[i][0])

    return pl.pallas_call(
        body, name="adamw_w3", out_shape=(jax.ShapeDtypeStruct((1, ROW_SHARD, D), F32),) * 12,
        compiler_params=pltpu.CompilerParams(vmem_limit_bytes=48 << 20),
    )(g3, *ws, *ms, *vs)


def _adamw_cm(f_cm, ws, ms, vs):
    def body(f_ref, *refs):
        w_refs, m_refs, v_refs, outs = refs[0:2], refs[2:4], refs[4:6], refs[6:14]
        gcw, gmt = refs[14], refs[15]
        gcw[0:16] = f_ref[0, 0:16]
        gcw[16:32] = f_ref[1, 0:16]
        gmt[0:8] = f_ref[0, 16:24]
        gmt[8:16] = f_ref[1, 16:24]
        g_conv = gcw[0:CONV_K, :]
        g_meta = gmt[...]
        outs[0][0] = g_conv
        outs[1][...] = g_meta
        outs[2][0], outs[4][0], outs[6][0] = _adamw_math(w_refs[0][0], g_conv, m_refs[0][0], v_refs[0][0])
        outs[3][...], outs[5][...], outs[7][...] = _adamw_math(w_refs[1][...], g_meta, m_refs[1][...], v_refs[1][...])

    pair = (jax.ShapeDtypeStruct((1, CONV_K, ROW_SHARD), F32), jax.ShapeDtypeStruct((N_META, ROW_SHARD), F32))
    return pl.pallas_call(
        body, name="adamw_cm", out_shape=pair * 4,
        scratch_shapes=[pltpu.VMEM((32, ROW_SHARD), F32), pltpu.VMEM((N_META, ROW_SHARD), F32)],
    )(f_cm, *ws, *ms, *vs)


def _adamw_small(r_sm, ws, ms, vs):
    def body(s_ref, *refs):
        w_refs, m_refs, v_refs, outs = refs[0:6], refs[6:12], refs[12:18], refs[18:]
        loss_o, g_os, d_os, m_os, v_os = outs[0], outs[1:7], outs[7:13], outs[13:19], outs[19:25]
        g = s_ref[0]
        for dev in range(1, 8):
            g = g + s_ref[dev]
        qk = g[4:5, :]
        qg = qk[:, 0:HEAD_DIM]
        kg = qk[:, GROUP_W:GROUP_W + HEAD_DIM]
        for h in range(1, GQA):
            qg = qg + qk[:, HEAD_DIM * h:HEAD_DIM * (h + 1)]
            kg = kg + qk[:, GROUP_W + HEAD_DIM * h:GROUP_W + HEAD_DIM * (h + 1)]
        loss_o[...] = (0.5 / D) * jnp.sum(g[5:6, :], axis=-1, keepdims=True)
        for i, gi in enumerate((g[0:1], g[1:2], g[2:3], g[3:4], qg, kg)):
            g_os[i][...] = gi
            d_os[i][...], m_os[i][...], v_os[i][...] = _adamw_math(w_refs[i][...], gi, m_refs[i][...], v_refs[i][...])

    six = tuple(jax.ShapeDtypeStruct(w.shape, F32) for w in ws)
    return pl.pallas_call(
        body, name="adamw_small", out_shape=(jax.ShapeDtypeStruct((1, 1), F32),) + six * 4,
    )(r_sm, *ws, *ms, *vs)


def kernel(x, meta_tokens, norm_g, w_in, conv_w, conv_b, conv_norm_g, conv_norm_b, w_conv_out, q_norm_g, k_norm_g, w_attn_out, w_out, loss_target, m_meta_tokens, m_norm_g, m_w_in, m_conv_w, m_conv_b, m_conv_norm_g, m_conv_norm_b, m_w_conv_out, m_q_norm_g, m_k_norm_g, m_w_attn_out, m_w_out, v_meta_tokens, v_norm_g, v_w_in, v_conv_w, v_conv_b, v_conv_norm_g, v_conv_norm_b, v_w_conv_out, v_q_norm_g, v_k_norm_g, v_w_attn_out, v_w_out):
    pad_k = lambda a: jnp.pad(a[0], ((0, 32 - CONV_K), (0, 0)))
    w3_s = (w_conv_out, w_attn_out, w_out)
    w_full, conv_w_full, meta_full, w3b, w3_land = _gather_weights(w_in[0], w3_s, pad_k(conv_w), meta_tokens)
    w3_pending, token = _w3_start(w3b, w3_land)
    norm_g_fwd = norm_g + token[0:1, 0:1]

    def w3_full(after):
        return _w3_wait(*w3_pending, after)

    early = {}

    def w3_sent(gw3):
        gw3v, gw3vb = (a.reshape(3, N_CHIPS, 2, 128, D) for a in gw3)
        early["sibling"], token = _w3_sibling_start(gw3vb)
        early["mine"] = gw3v
        return token

    def w3_reduced(after):
        r_w3 = _w3_sibling_wait(*early["sibling"], after)
        cs_w3 = _add_sibling_w3(early["mine"], r_w3)
        early["chips"], token = _reduce_chips_start(("w3",), (cs_w3,), "reduce_w3_start")
        return token

    def reduce_start(gwin, gcw, gmeta):
        gwin2, gwin2b = (a.reshape(D, IN_DIM) for a in gwin)
        gcm = jnp.concatenate([gcw.reshape(2, 16, D), gmeta.reshape(2, 8, D)], axis=1)
        r_win, r_cm = _reduce_sibling(gwin2b, gcm)
        cs = _add_sibling(gwin2, gcm, r_win, r_cm)
        pending, token = _reduce_chips_start(("win", "cm"), cs, "reduce_chips_start")
        return token, pending

    gx, _, _, _, _, smalls, pending = _local_step(
        x, loss_target, norm_g_fwd, conv_b, conv_norm_g, conv_norm_b, q_norm_g, k_norm_g,
        w_full, w3_full, conv_w_full, meta_full, (w3_sent, w3_reduced, reduce_start))
    (cs_win, cs_cm), (r2_win, r2_cm) = _reduce_chips_wait(("win", "cm"), pending, gx, "reduce_chips_wait")
    (cs_w3,), (r2_w3,) = _reduce_chips_wait(("w3",), early["chips"], gx, "reduce_w3_wait")
    f_win, f_w3, f_cm = _add_chips(cs_win, cs_w3, cs_cm, r2_win, r2_w3, r2_cm)
    f_win, f_w3, f_cm, r_sm = _share_sibling(f_win, f_w3, f_cm, smalls)

    g_w_in, d_w_in, nm_w_in, nv_w_in = _adamw(w_in[0], f_win.reshape(D, W_IN_SHARD), m_w_in[0], v_w_in[0], 128,
                                              "adamw_w_in")
    w3 = _adamw3(f_w3.reshape(3, ROW_SHARD, D), w3_s, (m_w_conv_out, m_w_attn_out, m_w_out),
                 (v_w_conv_out, v_w_attn_out, v_w_out))
    cm = _adamw_cm(f_cm, (conv_w, meta_tokens), (m_conv_w, m_meta_tokens), (v_conv_w, v_meta_tokens))
    small = _adamw_small(
        r_sm, (norm_g, conv_b, conv_norm_g, conv_norm_b, q_norm_g, k_norm_g),
        (m_norm_g, m_conv_b, m_conv_norm_g, m_conv_norm_b, m_q_norm_g, m_k_norm_g),
        (v_norm_g, v_conv_b, v_conv_norm_g, v_conv_norm_b, v_q_norm_g, v_k_norm_g))

    def assemble(big_in, w3x, cmx, s6):
        ng, cb, cng, cnb, qg, kg = s6
        return (cmx[1], ng, big_in[None], cmx[0], cb, cng, cnb, w3x[0], qg, kg, w3x[1], w3x[2])

    loss = small[0].reshape(())
    grads = assemble(g_w_in, w3[0:3], cm[0:2], small[1:7])
    deltas = assemble(d_w_in, w3[3:6], cm[2:4], small[7:13])
    new_m = assemble(nm_w_in, w3[6:9], cm[4:6], small[13:19])
    new_v = assemble(nv_w_in, w3[9:12], cm[6:8], small[19:25])
    return (loss, gx, *grads, *deltas, *new_m, *new_v)
```
